```python
import jax, jax.numpy as jnp
from jax import lax
import numpy as np

D_MODEL = 1024
BATCH = 32
SEQ = 2048
DEPTH = 2

PLE_DIM = 256
BLOCK = 128
EPS = 1e-6
NEG = -1e30

SWA_HEADS = 8
SWA_KV_HEADS = 2
SWA_HEAD_DIM = 64
SWA_WINDOW = 128
SWA_WIDTH = SWA_HEADS * SWA_HEAD_DIM
SWA_KV_WIDTH = SWA_KV_HEADS * SWA_HEAD_DIM

MLA_HEADS = 8
MLA_NOPE = 64
MLA_ROPE = 32
MLA_V = 64
MLA_Q_LORA = 256
MLA_KV_LORA = 128
MLA_WIDTH = MLA_HEADS * MLA_V
MLA_QK = MLA_NOPE + MLA_ROPE
ROPE_THETA = 10000.0

IN_SIZES = (SWA_WIDTH, SWA_KV_WIDTH, SWA_KV_WIDTH, SWA_WIDTH,
            MLA_Q_LORA, MLA_KV_LORA, MLA_ROPE, MLA_WIDTH,
            D_MODEL, D_MODEL)
IN_WIDTH = sum(IN_SIZES)

kernel_name = "hybrid_swa_sink_mla_gated_merge"


def rms_norm(x, g):
    xf = x.astype(jnp.float32)
    y = xf * lax.rsqrt(jnp.mean(xf * xf, axis=-1, keepdims=True) + EPS)
    return (y * g.astype(jnp.float32)).astype(x.dtype)


def split_columns(z, sizes):
    idx = []
    acc = 0
    for sz in sizes[:-1]:
        acc += sz
        idx.append(acc)
    return jnp.split(z, idx, axis=-1)


def alibi_slopes(n):
    return jnp.exp2(-8.0 * (jnp.arange(n, dtype=jnp.float32) + 1.0) / n)


def apply_rope(x, pos):
    r = x.shape[-1]
    inv = ROPE_THETA ** (-jnp.arange(0, r, 2, dtype=jnp.float32) / r)
    ang = pos.astype(jnp.float32)[..., None] * inv
    cos = jnp.cos(ang)[:, :, None, :]
    sin = jnp.sin(ang)[:, :, None, :]
    xf = x.astype(jnp.float32)
    x1, x2 = xf[..., : r // 2], xf[..., r // 2:]
    out = jnp.concatenate([x1 * cos - x2 * sin, x2 * cos + x1 * sin], axis=-1)
    return out.astype(x.dtype)


def swa_sink_attention(q, k, v, sink, pos):
    b, s, h, dh = q.shape
    kvh = k.shape[2]
    g = h // kvh
    nb = s // BLOCK
    qb = q.reshape(b, nb, BLOCK, kvh, g, dh)

    def band(t):
        tail = t.shape[2:]
        pad = jnp.zeros((b, BLOCK) + tail, t.dtype)
        prev = jnp.concatenate([pad, t[:, :-BLOCK]], axis=1).reshape((b, nb, BLOCK) + tail)
        cur = t.reshape((b, nb, BLOCK) + tail)
        return jnp.concatenate([prev, cur], axis=2)

    kb, vb, pk = band(k), band(v), band(pos)
    pq = pos.reshape(b, nb, BLOCK)
    scores = jnp.einsum('bnqkgd,bnskd->bnkgqs', qb, kb,
                        preferred_element_type=jnp.float32) * (dh ** -0.5)
    dist = (pq[:, :, :, None] - pk[:, :, None, :]).astype(jnp.float32)
    slopes = alibi_slopes(h).reshape(kvh, g)
    scores = scores - slopes[None, None, :, :, None, None] * dist[:, :, None, None, :, :]
    n_i = jnp.arange(nb)[:, None, None]
    q_i = jnp.arange(BLOCK)[None, :, None]
    k_j = jnp.arange(2 * BLOCK)[None, None, :]
    t_abs = n_i * BLOCK + q_i
    s_abs = n_i * BLOCK - BLOCK + k_j
    valid = (s_abs >= 0) & (s_abs <= t_abs) & (t_abs - s_abs < SWA_WINDOW)
    scores = jnp.where(valid[None, :, None, None, :, :], scores, NEG)
    sink_b = sink.astype(jnp.float32).reshape(kvh, g)[None, None, :, :, None]
    m = jnp.maximum(jnp.max(scores, axis=-1), sink_b)
    e = jnp.exp(scores - m[..., None])
    denom = jnp.sum(e, axis=-1) + jnp.exp(sink_b - m)
    probs = e / denom[..., None]
    out = jnp.einsum('bnkgqs,bnskd->bnqkgd', probs.astype(v.dtype), vb)
    return out.reshape(b, s, h * dh)


def mla_causal_attention(q, k, v):
    b, s, h, dq = q.shape
    nb = s // BLOCK
    qb = q.reshape(b, nb, BLOCK, h, dq).transpose(1, 0, 2, 3, 4)
    kpos = jnp.arange(s)
    scale = dq ** -0.5

    def one_block(args):
        qblk, n = args
        sc = jnp.einsum('bqhd,bshd->bhqs', qblk, k,
                        preferred_element_type=jnp.float32) * scale
        qpos = n * BLOCK + jnp.arange(BLOCK)
        sc = jnp.where(kpos[None, :] <= qpos[:, None], sc, NEG)
        pr = jax.nn.softmax(sc, axis=-1)
        return jnp.einsum('bhqs,bshd->bqhd', pr.astype(v.dtype), v)

    out = lax.map(one_block, (qb, jnp.arange(nb)))
    return out.transpose(1, 0, 2, 3, 4).reshape(b, s, h * v.shape[-1])


def _fwd_setup_inputs(seed: int = 0) -> dict:
    key = jax.random.key(seed)
    ks = jax.random.split(key, 20)
    f32 = jnp.float32

    def nrm(k, shape, fan_in):
        return jax.random.normal(k, shape, f32) * (fan_in ** -0.5)

    def gain(k, shape):
        return 1.0 + 0.02 * jax.random.normal(k, shape, f32)

    x = jax.random.normal(ks[0], (BATCH, SEQ, D_MODEL), f32)
    p = jax.random.normal(ks[1], (DEPTH, BATCH, SEQ, PLE_DIM), f32)
    positions = jnp.broadcast_to(jnp.arange(SEQ, dtype=jnp.int32)[None, :], (BATCH, SEQ))
    return {
        "x": x,
        "p": p,
        "positions": positions,
        "g_mix": gain(ks[2], (DEPTH, D_MODEL)),
        "w_in": nrm(ks[3], (DEPTH, D_MODEL, IN_WIDTH), D_MODEL),
        "sink": 0.5 * jax.random.normal(ks[4], (DEPTH, SWA_HEADS), f32),
        "g_q": gain(ks[5], (DEPTH, MLA_Q_LORA)),
        "w_uq": nrm(ks[6], (DEPTH, MLA_Q_LORA, MLA_HEADS * MLA_QK), MLA_Q_LORA),
        "g_kv": gain(ks[7], (DEPTH, MLA_KV_LORA)),
        "w_ukv": nrm(ks[8], (DEPTH, MLA_KV_LORA, MLA_HEADS * (MLA_NOPE + MLA_V)), MLA_KV_LORA),
        "w_br_a": nrm(ks[9], (DEPTH, SWA_WIDTH, D_MODEL), SWA_WIDTH),
        "w_br_b": nrm(ks[10], (DEPTH, MLA_WIDTH, D_MODEL), MLA_WIDTH),
        "w_out": nrm(ks[11], (DEPTH, D_MODEL, D_MODEL), D_MODEL),
        "g_ple": gain(ks[12], (DEPTH, D_MODEL)),
        "w_ple_gate": nrm(ks[13], (DEPTH, D_MODEL, D_MODEL), D_MODEL),
        "w_ple_proj": nrm(ks[14], (DEPTH, PLE_DIM, D_MODEL), PLE_DIM),
        "g_final": gain(ks[15], (D_MODEL,)),
    }


def _fwd_reference(x, p, positions, g_mix, w_in, sink, g_q, w_uq, g_kv, w_ukv,
              w_br_a, w_br_b, w_out, g_ple, w_ple_gate, w_ple_proj, g_final):
    b, s, _ = x.shape
    for i in range(DEPTH):
        h = rms_norm(x, g_mix[i])
        z = h @ w_in[i]
        (a_q, a_k, a_v, a_gate, b_qd, b_kvd, b_kr, b_gate,
         m_a, m_b) = split_columns(z, IN_SIZES)

        qa = a_q.reshape(b, s, SWA_HEADS, SWA_HEAD_DIM)
        ka = a_k.reshape(b, s, SWA_KV_HEADS, SWA_HEAD_DIM)
        va = a_v.reshape(b, s, SWA_KV_HEADS, SWA_HEAD_DIM)
        o_a = swa_sink_attention(qa, ka, va, sink[i], positions) * jax.nn.silu(a_gate)

        qb = (rms_norm(b_qd, g_q[i]) @ w_uq[i]).reshape(b, s, MLA_HEADS, MLA_QK)
        q_nope, q_rope = qb[..., :MLA_NOPE], qb[..., MLA_NOPE:]
        q_rope = apply_rope(q_rope, positions)
        kv = (rms_norm(b_kvd, g_kv[i]) @ w_ukv[i]).reshape(b, s, MLA_HEADS, MLA_NOPE + MLA_V)
        k_nope, vb = kv[..., :MLA_NOPE], kv[..., MLA_NOPE:]
        k_rope = apply_rope(b_kr[:, :, None, :], positions)
        q_full = jnp.concatenate([q_nope, q_rope], axis=-1)
        k_full = jnp.concatenate(
            [k_nope, jnp.broadcast_to(k_rope, (b, s, MLA_HEADS, MLA_ROPE))], axis=-1)
        o_b = mla_causal_attention(q_full, k_full, vb) * jax.nn.silu(b_gate)

        y = jax.nn.sigmoid(m_a) * (o_a @ w_br_a[i]) + jax.nn.sigmoid(m_b) * (o_b @ w_br_b[i])
        x = x + y @ w_out[i]

        pg = jax.nn.sigmoid(rms_norm(x, g_ple[i]) @ w_ple_gate[i])
        x = x + pg * (p[i].astype(x.dtype) @ w_ple_proj[i])
    return rms_norm(x, g_final)


import jax as _jax
import jax.numpy as _jnp

TWIN_FORMAT = 'train_step'
FWD_PARAMS = ['x', 'p', 'positions', 'g_mix', 'w_in', 'sink', 'g_q', 'w_uq', 'g_kv', 'w_ukv', 'w_br_a', 'w_br_b', 'w_out', 'g_ple', 'w_ple_gate', 'w_ple_proj', 'g_final']
TWIN_WEIGHTS = ['g_mix', 'w_in', 'sink', 'g_q', 'w_uq', 'g_kv', 'w_ukv', 'w_br_a', 'w_br_b', 'w_out', 'g_ple', 'w_ple_gate', 'w_ple_proj', 'g_final']
TWIN_DIFF_INPUT = 'x'
TWIN_INPUTS = ['x', 'p', 'positions', 'g_mix', 'w_in', 'sink', 'g_q', 'w_uq', 'g_kv', 'w_ukv', 'w_br_a', 'w_br_b', 'w_out', 'g_ple', 'w_ple_gate', 'w_ple_proj', 'g_final', 'loss_target', 'm_g_mix', 'm_w_in', 'm_sink', 'm_g_q', 'm_w_uq', 'm_g_kv', 'm_w_ukv', 'm_w_br_a', 'm_w_br_b', 'm_w_out', 'm_g_ple', 'm_w_ple_gate', 'm_w_ple_proj', 'm_g_final', 'v_g_mix', 'v_w_in', 'v_sink', 'v_g_q', 'v_w_uq', 'v_g_kv', 'v_w_ukv', 'v_w_br_a', 'v_w_br_b', 'v_w_out', 'v_g_ple', 'v_w_ple_gate', 'v_w_ple_proj', 'v_g_final']
TWIN_OUTPUTS = ['loss', 'grad_x', 'grad_g_mix', 'grad_w_in', 'grad_sink', 'grad_g_q', 'grad_w_uq', 'grad_g_kv', 'grad_w_ukv', 'grad_w_br_a', 'grad_w_br_b', 'grad_w_out', 'grad_g_ple', 'grad_w_ple_gate', 'grad_w_ple_proj', 'grad_g_final', 'delta_g_mix', 'delta_w_in', 'delta_sink', 'delta_g_q', 'delta_w_uq', 'delta_g_kv', 'delta_w_ukv', 'delta_w_br_a', 'delta_w_br_b', 'delta_w_out', 'delta_g_ple', 'delta_w_ple_gate', 'delta_w_ple_proj', 'delta_g_final', 'new_m_g_mix', 'new_m_w_in', 'new_m_sink', 'new_m_g_q', 'new_m_w_uq', 'new_m_g_kv', 'new_m_w_ukv', 'new_m_w_br_a', 'new_m_w_br_b', 'new_m_w_out', 'new_m_g_ple', 'new_m_w_ple_gate', 'new_m_w_ple_proj', 'new_m_g_final', 'new_v_g_mix', 'new_v_w_in', 'new_v_sink', 'new_v_g_q', 'new_v_w_uq', 'new_v_g_kv', 'new_v_w_ukv', 'new_v_w_br_a', 'new_v_w_br_b', 'new_v_w_out', 'new_v_g_ple', 'new_v_w_ple_gate', 'new_v_w_ple_proj', 'new_v_g_final']
TWIN_LEAF_KINDS = {'loss': 'loss', 'grad_x': 'grad_x', 'grad_g_mix': 'grad_w', 'grad_w_in': 'grad_w', 'grad_sink': 'grad_w', 'grad_g_q': 'grad_w', 'grad_w_uq': 'grad_w', 'grad_g_kv': 'grad_w', 'grad_w_ukv': 'grad_w', 'grad_w_br_a': 'grad_w', 'grad_w_br_b': 'grad_w', 'grad_w_out': 'grad_w', 'grad_g_ple': 'grad_w', 'grad_w_ple_gate': 'grad_w', 'grad_w_ple_proj': 'grad_w', 'grad_g_final': 'grad_w', 'delta_g_mix': 'delta_w', 'delta_w_in': 'delta_w', 'delta_sink': 'delta_w', 'delta_g_q': 'delta_w', 'delta_w_uq': 'delta_w', 'delta_g_kv': 'delta_w', 'delta_w_ukv': 'delta_w', 'delta_w_br_a': 'delta_w', 'delta_w_br_b': 'delta_w', 'delta_w_out': 'delta_w', 'delta_g_ple': 'delta_w', 'delta_w_ple_gate': 'delta_w', 'delta_w_ple_proj': 'delta_w', 'delta_g_final': 'delta_w', 'new_m_g_mix': 'new_m', 'new_m_w_in': 'new_m', 'new_m_sink': 'new_m', 'new_m_g_q': 'new_m', 'new_m_w_uq': 'new_m', 'new_m_g_kv': 'new_m', 'new_m_w_ukv': 'new_m', 'new_m_w_br_a': 'new_m', 'new_m_w_br_b': 'new_m', 'new_m_w_out': 'new_m', 'new_m_g_ple': 'new_m', 'new_m_w_ple_gate': 'new_m', 'new_m_w_ple_proj': 'new_m', 'new_m_g_final': 'new_m', 'new_v_g_mix': 'new_v', 'new_v_w_in': 'new_v', 'new_v_sink': 'new_v', 'new_v_g_q': 'new_v', 'new_v_w_uq': 'new_v', 'new_v_g_kv': 'new_v', 'new_v_w_ukv': 'new_v', 'new_v_w_br_a': 'new_v', 'new_v_w_br_b': 'new_v', 'new_v_w_out': 'new_v', 'new_v_g_ple': 'new_v', 'new_v_w_ple_gate': 'new_v', 'new_v_w_ple_proj': 'new_v', 'new_v_g_final': 'new_v'}


def _forward(args):
    return _fwd_reference(*[args[k] for k in FWD_PARAMS])


def _output_shape():
    out = _jax.eval_shape(lambda: _forward(_fwd_setup_inputs(0)))
    return out.shape, out.dtype

N_MICROBATCH = 1
ADAM_LR = 0.001
ADAM_B1 = 0.9
ADAM_B2 = 0.999
ADAM_EPS = 1e-08
ADAM_WD = 0.01
ADAM_STEP = 10
PER_EXAMPLE_BATCH_AXIS = {'x': 0, 'p': 1, 'positions': 0, 'loss_target': 0}
SHARED_INPUTS = []
_WEIGHT_DTYPES = {'g_mix': _jnp.float32, 'w_in': _jnp.float32, 'sink': _jnp.float32, 'g_q': _jnp.float32, 'w_uq': _jnp.float32, 'g_kv': _jnp.float32, 'w_ukv': _jnp.float32, 'w_br_a': _jnp.float32, 'w_br_b': _jnp.float32, 'w_out': _jnp.float32, 'g_ple': _jnp.float32, 'w_ple_gate': _jnp.float32, 'w_ple_proj': _jnp.float32, 'g_final': _jnp.float32}
MOMENT_SCALE = {'g_mix': 4.704943e-02, 'w_in': 2.288329e-02, 'sink': 3.680293e-02, 'g_q': 2.428490e-02, 'w_uq': 1.425211e-02, 'g_kv': 5.628648e-02, 'w_ukv': 1.811869e-02, 'w_br_a': 2.042587e-02, 'w_br_b': 1.474224e-02, 'w_out': 2.518272e-02, 'g_ple': 4.356883e-02, 'w_ple_gate': 4.247539e-02, 'w_ple_proj': 1.087687e-01, 'g_final': 6.378043e+01}


def _to_microbatches(a, axis):
    t = _jnp.moveaxis(a, axis, 0)
    t = t.reshape((N_MICROBATCH, t.shape[0] // N_MICROBATCH) + t.shape[1:])
    return _jnp.moveaxis(t, 1, axis + 1)


def setup_inputs(seed: int = 0) -> dict:
    inp = _fwd_setup_inputs(seed)
    key = _jax.random.fold_in(_jax.random.key(seed), 7919)
    shape, _ = _output_shape()
    out = dict(inp)
    out["loss_target"] = _jax.random.normal(_jax.random.fold_in(key, 0), shape, _jnp.float32)
    for i, name in enumerate(TWIN_WEIGHTS):
        w = inp[name].astype(_jnp.float32)
        if MOMENT_SCALE is None:
            s = _jnp.sqrt(_jnp.mean(_jnp.square(w)) + 1e-30)
        else:
            s = MOMENT_SCALE[name]
        km, kv = _jax.random.split(_jax.random.fold_in(key, i + 1))
        out[name] = w
        out["m_" + name] = s * _jax.random.normal(km, w.shape, _jnp.float32)
        out["v_" + name] = (s * s) * _jax.random.uniform(kv, w.shape, _jnp.float32, 0.5, 1.5)
    if N_MICROBATCH > 1:
        for name, axis in PER_EXAMPLE_BATCH_AXIS.items():
            out[name] = _to_microbatches(out[name], axis)
    return {'x': out['x'], 'p': out['p'], 'positions': out['positions'], 'g_mix': out['g_mix'], 'w_in': out['w_in'], 'sink': out['sink'], 'g_q': out['g_q'], 'w_uq': out['w_uq'], 'g_kv': out['g_kv'], 'w_ukv': out['w_ukv'], 'w_br_a': out['w_br_a'], 'w_br_b': out['w_br_b'], 'w_out': out['w_out'], 'g_ple': out['g_ple'], 'w_ple_gate': out['w_ple_gate'], 'w_ple_proj': out['w_ple_proj'], 'g_final': out['g_final'], 'loss_target': out['loss_target'], 'm_g_mix': out['m_g_mix'], 'm_w_in': out['m_w_in'], 'm_sink': out['m_sink'], 'm_g_q': out['m_g_q'], 'm_w_uq': out['m_w_uq'], 'm_g_kv': out['m_g_kv'], 'm_w_ukv': out['m_w_ukv'], 'm_w_br_a': out['m_w_br_a'], 'm_w_br_b': out['m_w_br_b'], 'm_w_out': out['m_w_out'], 'm_g_ple': out['m_g_ple'], 'm_w_ple_gate': out['m_w_ple_gate'], 'm_w_ple_proj': out['m_w_ple_proj'], 'm_g_final': out['m_g_final'], 'v_g_mix': out['v_g_mix'], 'v_w_in': out['v_w_in'], 'v_sink': out['v_sink'], 'v_g_q': out['v_g_q'], 'v_w_uq': out['v_w_uq'], 'v_g_kv': out['v_g_kv'], 'v_w_ukv': out['v_w_ukv'], 'v_w_br_a': out['v_w_br_a'], 'v_w_br_b': out['v_w_br_b'], 'v_w_out': out['v_w_out'], 'v_g_ple': out['v_g_ple'], 'v_w_ple_gate': out['v_w_ple_gate'], 'v_w_ple_proj': out['v_w_ple_proj'], 'v_g_final': out['v_g_final']}


def _loss(weights, diff, rest, loss_target):
    with _jax.named_scope("forward"):
        args = {**rest, TWIN_DIFF_INPUT: diff, **{k: w.astype(_WEIGHT_DTYPES[k]) for k, w in weights.items()}}
        y = _forward(args)
    with _jax.named_scope("loss_head"):
        err = _jnp.square(y.astype(_jnp.float32) - loss_target)
        return 0.5 * _jnp.sum(_jnp.mean(err, axis=-1)) if err.ndim else 0.5 * err


def _adamw(w, g, m, v):
    m = ADAM_B1 * m + (1.0 - ADAM_B1) * g
    v = ADAM_B2 * v + (1.0 - ADAM_B2) * _jnp.square(g)
    m_hat = m / (1.0 - ADAM_B1 ** ADAM_STEP)
    v_hat = v / (1.0 - ADAM_B2 ** ADAM_STEP)
    delta = -ADAM_LR * (m_hat / (_jnp.sqrt(v_hat) + ADAM_EPS) + ADAM_WD * w)
    return delta, m, v


def reference(x, p, positions, g_mix, w_in, sink, g_q, w_uq, g_kv, w_ukv, w_br_a, w_br_b, w_out, g_ple, w_ple_gate, w_ple_proj, g_final, loss_target, m_g_mix, m_w_in, m_sink, m_g_q, m_w_uq, m_g_kv, m_w_ukv, m_w_br_a, m_w_br_b, m_w_out, m_g_ple, m_w_ple_gate, m_w_ple_proj, m_g_final, v_g_mix, v_w_in, v_sink, v_g_q, v_w_uq, v_g_kv, v_w_ukv, v_w_br_a, v_w_br_b, v_w_out, v_g_ple, v_w_ple_gate, v_w_ple_proj, v_g_final):
    given = dict(x=x, p=p, positions=positions, g_mix=g_mix, w_in=w_in, sink=sink, g_q=g_q, w_uq=w_uq, g_kv=g_kv, w_ukv=w_ukv, w_br_a=w_br_a, w_br_b=w_br_b, w_out=w_out, g_ple=g_ple, w_ple_gate=w_ple_gate, w_ple_proj=w_ple_proj, g_final=g_final, loss_target=loss_target, m_g_mix=m_g_mix, m_w_in=m_w_in, m_sink=m_sink, m_g_q=m_g_q, m_w_uq=m_w_uq, m_g_kv=m_g_kv, m_w_ukv=m_w_ukv, m_w_br_a=m_w_br_a, m_w_br_b=m_w_br_b, m_w_out=m_w_out, m_g_ple=m_g_ple, m_w_ple_gate=m_w_ple_gate, m_w_ple_proj=m_w_ple_proj, m_g_final=m_g_final, v_g_mix=v_g_mix, v_w_in=v_w_in, v_sink=v_sink, v_g_q=v_g_q, v_w_uq=v_w_uq, v_g_kv=v_g_kv, v_w_ukv=v_w_ukv, v_w_br_a=v_w_br_a, v_w_br_b=v_w_br_b, v_w_out=v_w_out, v_g_ple=v_g_ple, v_w_ple_gate=v_w_ple_gate, v_w_ple_proj=v_w_ple_proj, v_g_final=v_g_final)
    weights = {n: given[n] for n in TWIN_WEIGHTS}
    shared = {n: given[n] for n in SHARED_INPUTS}
    per_example = {n: given[n] for n in ['x', 'p', 'positions']}
    grad_fn = _jax.value_and_grad(_loss, argnums=(0, 1))

    def one_microbatch(ex, loss_target):
        ex = dict(ex)
        diff = ex.pop(TWIN_DIFF_INPUT)
        return grad_fn(weights, diff, {**shared, **ex}, loss_target)

    if N_MICROBATCH == 1:
        loss, (grad_w, grad_x) = one_microbatch(per_example, given["loss_target"])
    else:
        def body(carry, xs):
            loss_sum, grad_sum = carry
            l_k, (gw_k, gx_k) = one_microbatch(xs[0], xs[1])
            with _jax.named_scope("update"):
                return (loss_sum + l_k, _jax.tree.map(_jnp.add, grad_sum, gw_k)), gx_k

        init = (_jnp.zeros((), _jnp.float32), _jax.tree.map(_jnp.zeros_like, weights))
        (loss, grad_w), grad_x = _jax.lax.scan(body, init, (per_example, given["loss_target"]))
    with _jax.named_scope("update"):
        delta_w, new_m, new_v = {}, {}, {}
        for n in TWIN_WEIGHTS:
            delta_w[n], new_m[n], new_v[n] = _adamw(weights[n], grad_w[n], given["m_" + n], given["v_" + n])
    return (loss, grad_x, *[grad_w[n] for n in TWIN_WEIGHTS], *[delta_w[n] for n in TWIN_WEIGHTS],
            *[new_m[n] for n in TWIN_WEIGHTS], *[new_v[n] for n in TWIN_WEIGHTS])
```

```python
import functools
import math

import jax
import jax.numpy as jnp
from jax import lax
from jax.experimental import pallas as pl
from jax.experimental.pallas import tpu as pltpu

F32 = jnp.float32
BF16 = jnp.bfloat16

D_MODEL = 1024
DEPTH = 2
PLE_DIM = 256
BLOCK = 128
EPS = 1e-6
NEG = -1e30
HEADS = 8
SWA_KV_HEADS = 2
HEAD_DIM = 64
LANES = 128
HPAD = HEADS * LANES
MLA_QK = 96
MLA_ROPE = 32
MLA_Q_LORA = 256
MLA_KV_LORA = 128
ROPE_THETA = 10000.0
IN_SIZES = (512, 128, 128, 512, 256, 128, 32, 512, 1024, 1024)

Z_MA, Z_MB, Z_AQ, Z_AGATE, Z_BGATE = 0, 1024, 2048, 3072, 4096
Z_BQD, Z_AK, Z_AV, Z_BKVD, Z_BKR = 5120, 5376, 5632, 5888, 6016
Z_WIDTH = 6144

ADAM_LR, ADAM_B1, ADAM_B2, ADAM_EPS, ADAM_WD, ADAM_STEP = 0.001, 0.9, 0.999, 1e-08, 0.01, 10

VMEM_LIMIT = 56 * 1024 * 1024
MESH = pl.DeviceIdType.MESH

WEIGHT_NAMES = ('g_mix', 'w_in', 'sink', 'g_q', 'w_uq', 'g_kv', 'w_ukv', 'w_br_a', 'w_br_b',
                'w_out', 'g_ple', 'w_ple_gate', 'w_ple_proj', 'g_final')
SHARDED = (('w_in', 2), ('w_uq', 2), ('w_ukv', 2), ('w_br_a', 2), ('w_br_b', 2),
           ('w_out', 1), ('w_ple_gate', 1), ('w_ple_proj', 2))
SMALL = ('g_mix', 'sink', 'g_q', 'g_kv', 'g_ple', 'g_final')
N_CHIPS = 4
FLAT_W = 1024


def _params(sem):
    return pltpu.CompilerParams(dimension_semantics=sem, vmem_limit_bytes=VMEM_LIMIT)


def _ew(name, body, ins, outs, rows, tm, accs=()):
    n_in, n_out = len(ins), len(outs)
    in_specs, args = [], []
    for arr, width, cb in ins:
        if width is None:
            in_specs.append(pl.BlockSpec(arr.shape, lambda i, nd=arr.ndim: (0,) * nd))
        else:
            in_specs.append(pl.BlockSpec((tm, width), lambda i, cb=cb: (i, cb)))
        args.append(arr)
    out_shape = [jax.ShapeDtypeStruct((rows, w), dt) for w, dt in outs]
    out_shape += [jax.ShapeDtypeStruct(s, F32) for s in accs]
    out_specs = [pl.BlockSpec((tm, w), lambda i: (i, 0)) for w, _ in outs]
    out_specs += [pl.BlockSpec(s, lambda i: (0, 0)) for s in accs]

    def kern(*refs):
        acc_refs = refs[n_in + n_out:]
        if acc_refs:
            @pl.when(pl.program_id(0) == 0)
            def _():
                for r in acc_refs:
                    r[...] = jnp.zeros_like(r)
        body(refs[:n_in], refs[n_in:n_in + n_out], acc_refs)

    res = pl.pallas_call(kern, name=name, grid=(rows // tm,), in_specs=in_specs, out_specs=out_specs,
                         out_shape=out_shape, compiler_params=_params(("arbitrary",)))(*args)
    return res


def _rms_fwd(name, x, width, cb, g, rows):
    def body(ins, outs, _):
        xv = ins[0][...]
        r = lax.rsqrt(jnp.mean(xv * xv, axis=-1, keepdims=True) + EPS)
        outs[0][...] = ((xv * r) * ins[1][...]).astype(BF16)
    return _ew(name, body, [(x, width, cb), (g.reshape(1, width), None, None)], [(width, BF16)], rows, 256)[0]


def _rms_bwd(name, x, width, cb, g, dh, rows, out_dtype, dres=None):
    def body(ins, outs, accs):
        xv, gv, dhv = ins[0][...], ins[1][...], ins[2][...].astype(F32)
        r = lax.rsqrt(jnp.mean(xv * xv, axis=-1, keepdims=True) + EPS)
        xhat = xv * r
        accs[0][...] += jnp.sum(dhv * xhat, axis=0, keepdims=True)
        dy = dhv * gv
        dx = r * (dy - xhat * jnp.mean(dy * xhat, axis=-1, keepdims=True))
        if dres is not None:
            dx = dx + ins[3][...]
        outs[0][...] = dx.astype(out_dtype)
    ins = [(x, width, cb), (g.reshape(1, width), None, None), (dh, width, 0)]
    if dres is not None:
        ins.append((dres, width, 0))
    return _ew(name, body, ins, [(width, out_dtype)], rows, 256, accs=[(1, width)])


def _mm(name, a, b, out_dtype, residual=None, tm=512, tn=512, tk=1024):
    M, K = a.shape
    N = b.shape[1]
    tm, tn, tk = min(tm, M), min(tn, N), min(tk, K)
    nk = K // tk
    has_res = residual is not None

    def kern(*refs):
        a_ref, b_ref = refs[0], refs[1]
        res_ref = refs[2] if has_res else None
        o_ref = refs[3] if has_res else refs[2]
        part = jnp.dot(a_ref[...].astype(BF16), b_ref[...].astype(BF16), preferred_element_type=F32)
        if nk == 1:
            if has_res:
                part = part + res_ref[...]
            o_ref[...] = part.astype(o_ref.dtype)
            return
        acc_ref = refs[-1]
        k = pl.program_id(2)

        @pl.when(k == 0)
        def _():
            acc_ref[...] = part

        @pl.when(k > 0)
        def _():
            acc_ref[...] += part

        @pl.when(k == nk - 1)
        def _():
            tot = acc_ref[...]
            if has_res:
                tot = tot + res_ref[...]
            o_ref[...] = tot.astype(o_ref.dtype)

    in_specs = [pl.BlockSpec((tm, tk), lambda i, j, k: (i, k)), pl.BlockSpec((tk, tn), lambda i, j, k: (k, j))]
    args = [a, b]
    if has_res:
        in_specs.append(pl.BlockSpec((tm, tn), lambda i, j, k: (i, j)))
        args.append(residual)
    scratch = [pltpu.VMEM((tm, tn), F32)] if nk > 1 else []
    return pl.pallas_call(
        kern, name=name, grid=(M // tm, N // tn, nk), in_specs=in_specs,
        out_specs=pl.BlockSpec((tm, tn), lambda i, j, k: (i, j)),
        out_shape=jax.ShapeDtypeStruct((M, N), out_dtype), scratch_shapes=scratch,
        compiler_params=_params(("parallel", "parallel", "arbitrary")))(*args)


def _mm_tn(name, a, b, tm=512, tn=1024, tk=512):
    T, M = a.shape
    N = b.shape[1]
    tm, tn, tk = min(tm, M), min(tn, N), min(tk, T)
    nk = T // tk

    def kern(a_ref, b_ref, o_ref):
        k = pl.program_id(2)
        part = lax.dot_general(a_ref[...].astype(BF16), b_ref[...].astype(BF16),
                               (((0,), (0,)), ((), ())), preferred_element_type=F32)

        @pl.when(k == 0)
        def _():
            o_ref[...] = part

        @pl.when(k > 0)
        def _():
            o_ref[...] += part

    return pl.pallas_call(
        kern, name=name, grid=(M // tm, N // tn, nk),
        in_specs=[pl.BlockSpec((tk, tm), lambda i, j, k: (k, i)), pl.BlockSpec((tk, tn), lambda i, j, k: (k, j))],
        out_specs=pl.BlockSpec((tm, tn), lambda i, j, k: (i, j)),
        out_shape=jax.ShapeDtypeStruct((M, N), F32),
        compiler_params=_params(("parallel", "parallel", "arbitrary")))(a, b)


def _dot_nt(a, b):
    return lax.dot_general(a, b, (((1,), (1,)), ((), ())), preferred_element_type=F32)


def _dot_tn(a, b):
    return lax.dot_general(a, b, (((0,), (0,)), ((), ())), preferred_element_type=F32)


def _swa_scores(n, q_all, kb, posq, posk, h):
    kvh = h // (HEADS // SWA_KV_HEADS)
    qh = q_all[:, h * LANES:(h + 1) * LANES].astype(BF16)
    kh = kb[:, kvh * LANES:(kvh + 1) * LANES]
    dist = (posq - posk).astype(F32)
    qi = lax.broadcasted_iota(jnp.int32, (BLOCK, 2 * BLOCK), 0)
    kj = lax.broadcasted_iota(jnp.int32, (BLOCK, 2 * BLOCK), 1)
    t_abs = n * BLOCK + qi
    s_abs = n * BLOCK - BLOCK + kj
    valid = (s_abs >= 0) & (s_abs <= t_abs) & (t_abs - s_abs < BLOCK)
    s = _dot_nt(qh, kh) * (HEAD_DIM ** -0.5) - (2.0 ** -(h + 1)) * dist
    return jnp.where(valid, s, NEG), qh, kh


def _swa_specs(nb):
    prev = lambda b, n: b * nb + jnp.maximum(n - 1, 0)
    own = lambda b, n: b * nb + n
    return [
        pl.BlockSpec((BLOCK, HPAD), lambda b, n: (own(b, n), Z_AQ // HPAD)),
        pl.BlockSpec((BLOCK, 256), lambda b, n: (prev(b, n), Z_AK // 256)),
        pl.BlockSpec((BLOCK, 256), lambda b, n: (own(b, n), Z_AK // 256)),
        pl.BlockSpec((BLOCK, 256), lambda b, n: (prev(b, n), Z_AV // 256)),
        pl.BlockSpec((BLOCK, 256), lambda b, n: (own(b, n), Z_AV // 256)),
        pl.BlockSpec((BLOCK, 1), lambda b, n: (own(b, n), 0)),
        pl.BlockSpec((1, 1, BLOCK), lambda b, n: (prev(b, n), 0, 0)),
        pl.BlockSpec((1, 1, BLOCK), lambda b, n: (own(b, n), 0, 0)),
    ]


def _swa_fwd(z, pos_col, pos_row, sink_row, B, S):
    nb = S // BLOCK
    T = B * S

    def kern(q_ref, kp_ref, kc_ref, vp_ref, vc_ref, pq_ref, pkp_ref, pkc_ref, gate_ref, sink_ref,
             oraw_ref, og_ref, lse_ref):
        n = pl.program_id(1)
        q_all = q_ref[...]
        kb = jnp.concatenate([kp_ref[...], kc_ref[...]], axis=0).astype(BF16)
        vb = jnp.concatenate([vp_ref[...], vc_ref[...]], axis=0).astype(BF16)
        posq = pq_ref[...]
        posk = jnp.concatenate([pkp_ref[0], pkc_ref[0]], axis=1)
        lane = lax.broadcasted_iota(jnp.int32, (BLOCK, LANES), 1)
        lse_all = jnp.zeros((BLOCK, LANES), F32)
        for h in range(HEADS):
            kvh = h // (HEADS // SWA_KV_HEADS)
            s, _, _ = _swa_scores(n, q_all, kb, posq, posk, h)
            sink_h = sink_ref[0:1, h:h + 1]
            m = jnp.maximum(jnp.max(s, axis=-1, keepdims=True), sink_h)
            e = jnp.exp(s - m)
            denom = jnp.sum(e, axis=-1, keepdims=True) + jnp.exp(sink_h - m)
            probs = e * (1.0 / denom)
            o = jnp.dot(probs.astype(BF16), vb[:, kvh * LANES:(kvh + 1) * LANES], preferred_element_type=F32)
            cols = slice(h * LANES, (h + 1) * LANES)
            oraw_ref[:, cols] = o
            g = gate_ref[:, cols]
            og_ref[:, cols] = (o * (g * jax.nn.sigmoid(g))).astype(BF16)
            lse_all = jnp.where(lane == h, m + jnp.log(denom), lse_all)
        lse_ref[...] = lse_all

    own = lambda b, n: b * nb + n
    in_specs = _swa_specs(nb) + [
        pl.BlockSpec((BLOCK, HPAD), lambda b, n: (own(b, n), Z_AGATE // HPAD)),
        pl.BlockSpec((1, LANES), lambda b, n: (0, 0)),
    ]
    out_specs = [pl.BlockSpec((BLOCK, HPAD), lambda b, n: (own(b, n), 0)),
                 pl.BlockSpec((BLOCK, HPAD), lambda b, n: (own(b, n), 0)),
                 pl.BlockSpec((BLOCK, LANES), lambda b, n: (own(b, n), 0))]
    out_shape = [jax.ShapeDtypeStruct((T, HPAD), F32), jax.ShapeDtypeStruct((T, HPAD), BF16),
                 jax.ShapeDtypeStruct((T, LANES), F32)]
    return pl.pallas_call(kern, name="swa_fwd", grid=(B, nb), in_specs=in_specs, out_specs=out_specs,
                          out_shape=out_shape, compiler_params=_params(("parallel", "arbitrary")))(
        z, z, z, z, z, pos_col, pos_row, pos_row, z, sink_row)


def _swa_bwd(z, pos_col, pos_row, sink_row, lse, do_raw, delta, B, S):
    nb = S // BLOCK
    T = B * S

    def kern(q_ref, kp_ref, kc_ref, vp_ref, vc_ref, pq_ref, pkp_ref, pkc_ref, sink_ref, lse_ref, do_ref,
             delta_ref, dq_ref, dk_ref, dv_ref, dsink_ref):
        b, n = pl.program_id(0), pl.program_id(1)

        @pl.when(n == 0)
        def _():
            dk_ref[...] = jnp.zeros_like(dk_ref)
            dv_ref[...] = jnp.zeros_like(dv_ref)

        @pl.when((b == 0) & (n == 0))
        def _():
            dsink_ref[...] = jnp.zeros_like(dsink_ref)

        q_all = q_ref[...]
        kb = jnp.concatenate([kp_ref[...], kc_ref[...]], axis=0).astype(BF16)
        vb = jnp.concatenate([vp_ref[...], vc_ref[...]], axis=0).astype(BF16)
        posq = pq_ref[...]
        posk = jnp.concatenate([pkp_ref[0], pkc_ref[0]], axis=1)
        lane1 = lax.broadcasted_iota(jnp.int32, (1, LANES), 1)
        dsink = jnp.zeros((1, LANES), F32)
        dk_band = [jnp.zeros((2 * BLOCK, LANES), F32) for _ in range(SWA_KV_HEADS)]
        dv_band = [jnp.zeros((2 * BLOCK, LANES), F32) for _ in range(SWA_KV_HEADS)]
        for h in range(HEADS):
            kvh = h // (HEADS // SWA_KV_HEADS)
            cols = slice(h * LANES, (h + 1) * LANES)
            s, qh, kh = _swa_scores(n, q_all, kb, posq, posk, h)
            lse_h = lse_ref[:, h:h + 1]
            p = jnp.exp(s - lse_h)
            do = do_ref[:, cols]
            delta_h = delta_ref[:, h * LANES:h * LANES + 1]
            dp = _dot_nt(do, vb[:, kvh * LANES:(kvh + 1) * LANES])
            ds = (p * (dp - delta_h) * (HEAD_DIM ** -0.5)).astype(BF16)
            dq_ref[:, cols] = jnp.dot(ds, kh, preferred_element_type=F32).astype(BF16)
            dk_band[kvh] = dk_band[kvh] + _dot_tn(ds, qh)
            dv_band[kvh] = dv_band[kvh] + _dot_tn(p.astype(BF16), do)
            psink = jnp.exp(sink_ref[0:1, h:h + 1] - lse_h)
            dsink = dsink + jnp.where(lane1 == h, -jnp.sum(psink * delta_h, axis=0, keepdims=True), 0.0)
        dsink_ref[...] += dsink
        dkb = jnp.concatenate(dk_band, axis=1)
        dvb = jnp.concatenate(dv_band, axis=1)
        r_prev = pl.ds(pl.multiple_of(jnp.maximum(n - 1, 0) * BLOCK, BLOCK), BLOCK)
        r_own = pl.ds(pl.multiple_of(n * BLOCK, BLOCK), BLOCK)
        dk_ref[r_prev, :] += dkb[:BLOCK]
        dk_ref[r_own, :] += dkb[BLOCK:]
        dv_ref[r_prev, :] += dvb[:BLOCK]
        dv_ref[r_own, :] += dvb[BLOCK:]

    own = lambda b, n: b * nb + n
    in_specs = _swa_specs(nb) + [
        pl.BlockSpec((1, LANES), lambda b, n: (0, 0)),
        pl.BlockSpec((BLOCK, LANES), lambda b, n: (own(b, n), 0)),
        pl.BlockSpec((BLOCK, HPAD), lambda b, n: (own(b, n), 0)),
        pl.BlockSpec((BLOCK, HPAD), lambda b, n: (own(b, n), 0)),
    ]
    out_specs = [pl.BlockSpec((BLOCK, HPAD), lambda b, n: (own(b, n), 0)),
                 pl.BlockSpec((S, 256), lambda b, n: (b, 0)),
                 pl.BlockSpec((S, 256), lambda b, n: (b, 0)),
                 pl.BlockSpec((1, LANES), lambda b, n: (0, 0))]
    out_shape = [jax.ShapeDtypeStruct((T, HPAD), BF16), jax.ShapeDtypeStruct((T, 256), F32),
                 jax.ShapeDtypeStruct((T, 256), F32), jax.ShapeDtypeStruct((1, LANES), F32)]
    return pl.pallas_call(kern, name="swa_bwd", grid=(B, nb), in_specs=in_specs, out_specs=out_specs,
                          out_shape=out_shape, compiler_params=_params(("arbitrary", "arbitrary")))(
        z, z, z, z, z, pos_col, pos_row, pos_row, sink_row, lse, do_raw, delta)


MLA_T = 256
MLA_SCALE = MLA_QK ** -0.5


def _causal(s):
    row = lax.broadcasted_iota(jnp.int32, s.shape, 0)
    col = lax.broadcasted_iota(jnp.int32, s.shape, 1)
    return jnp.where(col <= row, s, NEG)


def _mla_fwd(q, k, v, z, B, S):
    T = B * S
    nq = S // MLA_T

    def kern(q_ref, k_ref, v_ref, gate_ref, oraw_ref, og_ref, lse_ref):
        i = pl.program_id(2)
        qv = q_ref[...]

        def step(j, carry, masked):
            m, l, acc = carry
            rows = pl.ds(pl.multiple_of(j * MLA_T, MLA_T), MLA_T)
            s = _dot_nt(qv, k_ref[rows, :]) * MLA_SCALE
            if masked:
                s = _causal(s)
            m_new = jnp.maximum(m, jnp.max(s, axis=-1, keepdims=True))
            alpha = jnp.exp(m - m_new)
            p = jnp.exp(s - m_new)
            l = alpha * l + jnp.sum(p, axis=-1, keepdims=True)
            acc = alpha * acc + jnp.dot(p.astype(BF16), v_ref[rows, :], preferred_element_type=F32)
            return m_new, l, acc

        init = (jnp.full((MLA_T, 1), NEG, F32), jnp.zeros((MLA_T, 1), F32), jnp.zeros((MLA_T, LANES), F32))
        carry = lax.fori_loop(0, i, lambda j, c: step(j, c, False), init)
        m, l, acc = step(i, carry, True)
        o = acc * (1.0 / l)
        oraw_ref[...] = o
        g = gate_ref[...]
        og_ref[...] = (o * (g * jax.nn.sigmoid(g))).astype(BF16)
        lse_ref[...] = jnp.broadcast_to(m + jnp.log(l), (MLA_T, LANES))

    blk = lambda b, h, i: (b * nq + i, h)
    in_specs = [pl.BlockSpec((MLA_T, LANES), blk),
                pl.BlockSpec((S, LANES), lambda b, h, i: (b, h)),
                pl.BlockSpec((S, LANES), lambda b, h, i: (b, h)),
                pl.BlockSpec((MLA_T, LANES), lambda b, h, i: (b * nq + i, Z_BGATE // LANES + h))]
    out_specs = [pl.BlockSpec((MLA_T, LANES), blk)] * 3
    out_shape = [jax.ShapeDtypeStruct((T, HPAD), F32), jax.ShapeDtypeStruct((T, HPAD), BF16),
                 jax.ShapeDtypeStruct((T, HPAD), F32)]
    return pl.pallas_call(kern, name="mla_fwd", grid=(B, HEADS, nq), in_specs=in_specs, out_specs=out_specs,
                          out_shape=out_shape,
                          compiler_params=_params(("parallel", "parallel", "arbitrary")))(q, k, v, z)


def _mla_bwd(q, k, v, do_raw, lse, delta, B, S):
    T = B * S
    nk = S // MLA_T

    def kern(q_ref, k_ref, v_ref, do_ref, lse_ref, delta_ref, dq_ref, dk_ref, dv_ref, dk_acc, dv_acc):
        j = pl.program_id(2)

        @pl.when(j == 0)
        def _():
            dq_ref[...] = jnp.zeros_like(dq_ref)

        dk_acc[...] = jnp.zeros_like(dk_acc)
        dv_acc[...] = jnp.zeros_like(dv_acc)
        kv, vv = k_ref[...], v_ref[...]

        def step(i, masked):
            rows = pl.ds(pl.multiple_of(i * MLA_T, MLA_T), MLA_T)
            qv, do = q_ref[rows, :], do_ref[rows, :]
            s = _dot_nt(qv, kv) * MLA_SCALE
            if masked:
                s = _causal(s)
            p = jnp.exp(s - lse_ref[rows, 0:1])
            dp = _dot_nt(do, vv)
            ds = (p * (dp - delta_ref[rows, 0:1]) * MLA_SCALE).astype(BF16)
            dv_acc[...] += _dot_tn(p.astype(BF16), do)
            dk_acc[...] += _dot_tn(ds, qv)
            dq_ref[rows, :] += jnp.dot(ds, kv, preferred_element_type=F32)

        step(j, True)

        def body(i, c):
            step(i, False)
            return c

        lax.fori_loop(j + 1, nk, body, 0)
        dk_ref[...] = dk_acc[...]
        dv_ref[...] = dv_acc[...]

    whole = lambda b, h, j: (b, h)
    tile = lambda b, h, j: (b * nk + j, h)
    in_specs = [pl.BlockSpec((S, LANES), whole), pl.BlockSpec((MLA_T, LANES), tile),
                pl.BlockSpec((MLA_T, LANES), tile), pl.BlockSpec((S, LANES), whole),
                pl.BlockSpec((S, LANES), whole), pl.BlockSpec((S, LANES), whole)]
    out_specs = [pl.BlockSpec((S, LANES), whole), pl.BlockSpec((MLA_T, LANES), tile),
                 pl.BlockSpec((MLA_T, LANES), tile)]
    out_shape = [jax.ShapeDtypeStruct((T, HPAD), F32)] * 3
    return pl.pallas_call(kern, name="mla_bwd", grid=(B, HEADS, nk), in_specs=in_specs, out_specs=out_specs,
                          out_shape=out_shape,
                          scratch_shapes=[pltpu.VMEM((MLA_T, LANES), F32), pltpu.VMEM((MLA_T, LANES), F32)],
                          compiler_params=_params(("parallel", "parallel", "arbitrary")))(
        q, k, v, do_raw, lse, delta)


def _rope_tables(pos_col, inv_lane, rows):
    def body(ins, outs, _):
        ang = ins[0][...].astype(F32) * ins[1][...]
        lane = lax.broadcasted_iota(jnp.int32, ang.shape, 1)
        cos, sin = jnp.cos(ang), jnp.sin(ang)
        first = (lane >= HEAD_DIM) & (lane < HEAD_DIM + MLA_ROPE // 2)
        second = (lane >= HEAD_DIM + MLA_ROPE // 2) & (lane < MLA_QK)
        outs[0][...] = jnp.where(lane < HEAD_DIM, 1.0, jnp.where(lane < MLA_QK, cos, 0.0))
        outs[1][...] = jnp.where(first, -sin, 0.0)
        outs[2][...] = jnp.where(second, sin, 0.0)
    return _ew("rope_tables", body, [(pos_col, 1, 0), (inv_lane, None, None)], [(LANES, F32)] * 3, rows, 256)


def _rope(x, c, s1, s2):
    return x * c + pltpu.roll(x, 112, 1) * s1 + pltpu.roll(x, 16, 1) * s2


def _rope_t(d, c, s1, s2):
    return d * c + pltpu.roll(d * s1, 16, 1) + pltpu.roll(d * s2, 112, 1)


def _mla_prep(q_pre, kv_pre, z, tabs, rows):
    def body(ins, outs, _):
        c, s1, s2 = ins[3][...], ins[4][...], ins[5][...]
        kr = _rope(ins[2][...], c, s1, s2)
        for h in range(HEADS):
            cols = slice(h * LANES, (h + 1) * LANES)
            outs[0][:, cols] = _rope(ins[0][:, cols], c, s1, s2).astype(BF16)
            outs[1][:, cols] = (ins[1][:, cols] + kr).astype(BF16)
        outs[2][...] = ins[6][...].astype(BF16)
    ins = [(q_pre, HPAD, 0), (kv_pre, HPAD, 0), (z, LANES, Z_BKR // LANES),
           (tabs[0], LANES, 0), (tabs[1], LANES, 0), (tabs[2], LANES, 0), (kv_pre, HPAD, 1)]
    return _ew("mla_prep", body, ins, [(HPAD, BF16)] * 3, rows, 256)


def _mla_prep_bwd(dq, dk, dv, tabs, rows):
    def body(ins, outs, _):
        c, s1, s2 = ins[3][...], ins[4][...], ins[5][...]
        lane = lax.broadcasted_iota(jnp.int32, c.shape, 1)
        dkr = jnp.zeros(c.shape, F32)
        for h in range(HEADS):
            cols = slice(h * LANES, (h + 1) * LANES)
            outs[0][:, cols] = _rope_t(ins[0][:, cols], c, s1, s2).astype(BF16)
            dkh = ins[1][:, cols]
            outs[1][:, cols] = jnp.where(lane < HEAD_DIM, dkh, 0.0).astype(BF16)
            dkr = dkr + dkh
        outs[1][:, HPAD:] = ins[2][...].astype(BF16)
        live = (lane >= HEAD_DIM) & (lane < MLA_QK)
        outs[2][...] = jnp.where(live, _rope_t(jnp.where(live, dkr, 0.0), c, s1, s2), 0.0).astype(BF16)
    ins = [(dq, HPAD, 0), (dk, HPAD, 0), (dv, HPAD, 0), (tabs[0], LANES, 0), (tabs[1], LANES, 0),
           (tabs[2], LANES, 0)]
    return _ew("mla_prep_bwd", body, ins, [(HPAD, BF16), (2 * HPAD, BF16), (LANES, BF16)], rows, 256)


def _gate_bwd(name, d_o, o_raw, z, gate_cb, rows):
    def body(ins, outs, _):
        for h in range(HEADS):
            cols = slice(h * LANES, (h + 1) * LANES)
            dog, o, g = ins[0][:, cols], ins[1][:, cols], ins[2][:, cols]
            sg = jax.nn.sigmoid(g)
            do = dog * (g * sg)
            outs[0][:, cols] = do.astype(BF16)
            outs[1][:, cols] = (dog * o * (sg * (1.0 + g * (1.0 - sg)))).astype(BF16)
            outs[2][:, cols] = jnp.broadcast_to(jnp.sum(do * o, axis=-1, keepdims=True), do.shape)
    ins = [(d_o, HPAD, 0), (o_raw, HPAD, 0), (z, HPAD, gate_cb)]
    return _ew(name, body, ins, [(HPAD, BF16), (HPAD, BF16), (HPAD, F32)], rows, 256)


def _merge_fwd(ua, ub, z, rows):
    def body(ins, outs, _):
        y = jax.nn.sigmoid(ins[2][...]) * ins[0][...] + jax.nn.sigmoid(ins[3][...]) * ins[1][...]
        outs[0][...] = y.astype(BF16)
    ins = [(ua, D_MODEL, 0), (ub, D_MODEL, 0), (z, D_MODEL, Z_MA // D_MODEL), (z, D_MODEL, Z_MB // D_MODEL)]
    return _ew("merge_fwd", body, ins, [(D_MODEL, BF16)], rows, 256)[0]


def _merge_bwd(dy, ua, ub, z, rows):
    def body(ins, outs, _):
        dyv = ins[0][...]
        for idx in range(2):
            s = jax.nn.sigmoid(ins[3 + idx][...])
            outs[idx][...] = (dyv * s).astype(BF16)
            outs[2 + idx][...] = (dyv * ins[1 + idx][...] * (s * (1.0 - s))).astype(BF16)
    ins = [(dy, D_MODEL, 0), (ua, D_MODEL, 0), (ub, D_MODEL, 0),
           (z, D_MODEL, Z_MA // D_MODEL), (z, D_MODEL, Z_MB // D_MODEL)]
    return _ew("merge_bwd", body, ins, [(D_MODEL, BF16)] * 4, rows, 256)


def _ple_fwd(x1, u, e, rows):
    def body(ins, outs, _):
        outs[0][...] = ins[0][...] + jax.nn.sigmoid(ins[1][...]) * ins[2][...]
    return _ew("ple_fwd", body, [(x1, D_MODEL, 0), (u, D_MODEL, 0), (e, D_MODEL, 0)], [(D_MODEL, F32)], rows, 256)[0]


def _ple_bwd(dx2, u, e, rows):
    def body(ins, outs, _):
        d, s = ins[0][...], jax.nn.sigmoid(ins[1][...])
        outs[0][...] = (d * s).astype(BF16)
        outs[1][...] = (d * ins[2][...] * (s * (1.0 - s))).astype(BF16)
    return _ew("ple_bwd", body, [(dx2, D_MODEL, 0), (u, D_MODEL, 0), (e, D_MODEL, 0)],
               [(D_MODEL, BF16)] * 2, rows, 256)


def _loss_head(x, g, target, rows):
    def body(ins, outs, accs):
        xv, gv = ins[0][...], ins[1][...]
        r = lax.rsqrt(jnp.mean(xv * xv, axis=-1, keepdims=True) + EPS)
        xhat = xv * r
        err = xhat * gv - ins[2][...]
        accs[0][...] += jnp.broadcast_to(0.5 * jnp.sum(jnp.mean(err * err, axis=-1, keepdims=True),
                                                       axis=0, keepdims=True), (1, LANES))
        dyv = err * (1.0 / D_MODEL)
        accs[1][...] += jnp.sum(dyv * xhat, axis=0, keepdims=True)
        dy = dyv * gv
        outs[0][...] = r * (dy - xhat * jnp.mean(dy * xhat, axis=-1, keepdims=True))
    ins = [(x, D_MODEL, 0), (g.reshape(1, D_MODEL), None, None), (target, D_MODEL, 0)]
    return _ew("loss_head", body, ins, [(D_MODEL, F32)], rows, 256, accs=[(1, LANES), (1, D_MODEL)])


def _pad_heads_cols(w, n_heads, dim):
    k = w.shape[0]
    return jnp.pad(w.reshape(k, n_heads, dim), ((0, 0), (0, 0), (0, LANES - dim))).reshape(k, n_heads * LANES)


def _unpad_heads_cols(w, n_heads, dim):
    k = w.shape[0]
    return w.reshape(k, n_heads, LANES)[:, :, :dim].reshape(k, n_heads * dim)


def _layer_weights(w, i):
    segs = jnp.split(w['w_in'][i], list(_cumsum(IN_SIZES))[:-1], axis=1)
    a_q, a_k, a_v, a_gate, b_qd, b_kvd, b_kr, b_gate, m_a, m_b = segs
    kr = jnp.pad(b_kr, ((0, 0), (HEAD_DIM, LANES - MLA_QK)))
    w_in = jnp.concatenate([
        m_a, m_b, _pad_heads_cols(a_q, HEADS, HEAD_DIM), _pad_heads_cols(a_gate, HEADS, HEAD_DIM),
        _pad_heads_cols(b_gate, HEADS, HEAD_DIM), b_qd, _pad_heads_cols(a_k, SWA_KV_HEADS, HEAD_DIM),
        _pad_heads_cols(a_v, SWA_KV_HEADS, HEAD_DIM), b_kvd, kr], axis=1)
    w_uq = _pad_heads_cols(w['w_uq'][i], HEADS, MLA_QK)
    ukv = w['w_ukv'][i].reshape(MLA_KV_LORA, HEADS, 2 * HEAD_DIM)
    pad = ((0, 0), (0, 0), (0, HEAD_DIM))
    w_ukv = jnp.concatenate([jnp.pad(ukv[:, :, :HEAD_DIM], pad).reshape(MLA_KV_LORA, HPAD),
                             jnp.pad(ukv[:, :, HEAD_DIM:], pad).reshape(MLA_KV_LORA, HPAD)], axis=1)
    w_br_a = _pad_heads_cols(w['w_br_a'][i].T, HEADS, HEAD_DIM).T
    w_br_b = _pad_heads_cols(w['w_br_b'][i].T, HEADS, HEAD_DIM).T
    out = dict(w_in=w_in, w_uq=w_uq, w_ukv=w_ukv, w_br_a=w_br_a, w_br_b=w_br_b, w_out=w['w_out'][i],
               w_pg=w['w_ple_gate'][i], w_pp=w['w_ple_proj'][i])
    for name in ('w_in', 'w_uq', 'w_ukv', 'w_br_a', 'w_br_b', 'w_out', 'w_pg'):
        out[name + '_t'] = out[name].T
    return out


def _cumsum(sizes):
    acc, out = 0, []
    for s in sizes:
        acc += s
        out.append(acc)
    return out


def _unpad_grads(g):
    d = g['w_in']
    seg = lambda off, width: d[:, off:off + width]
    b_kr = seg(Z_BKR, LANES)[:, HEAD_DIM:MLA_QK]
    w_in = jnp.concatenate([
        _unpad_heads_cols(seg(Z_AQ, HPAD), HEADS, HEAD_DIM), _unpad_heads_cols(seg(Z_AK, 256), SWA_KV_HEADS, HEAD_DIM),
        _unpad_heads_cols(seg(Z_AV, 256), SWA_KV_HEADS, HEAD_DIM), _unpad_heads_cols(seg(Z_AGATE, HPAD), HEADS, HEAD_DIM),
        seg(Z_BQD, MLA_Q_LORA), seg(Z_BKVD, MLA_KV_LORA), b_kr, _unpad_heads_cols(seg(Z_BGATE, HPAD), HEADS, HEAD_DIM),
        seg(Z_MA, D_MODEL), seg(Z_MB, D_MODEL)], axis=1)
    w_uq = _unpad_heads_cols(g['w_uq'], HEADS, MLA_QK)
    ukv = g['w_ukv'].reshape(MLA_KV_LORA, 2, HEADS, LANES)[:, :, :, :HEAD_DIM]
    w_ukv = jnp.concatenate([ukv[:, 0], ukv[:, 1]], axis=-1).reshape(MLA_KV_LORA, HEADS * 2 * HEAD_DIM)
    w_br_a = _unpad_heads_cols(g['w_br_a'].T, HEADS, HEAD_DIM).T
    w_br_b = _unpad_heads_cols(g['w_br_b'].T, HEADS, HEAD_DIM).T
    return dict(w_in=w_in, w_uq=w_uq, w_ukv=w_ukv, w_br_a=w_br_a, w_br_b=w_br_b, w_out=g['w_out'],
                w_ple_gate=g['w_pg'], w_ple_proj=g['w_pp'], g_mix=g['g_mix'], sink=g['sink'], g_q=g['g_q'],
                g_kv=g['g_kv'], g_ple=g['g_ple'])


def _layer_fwd(x0, p_i, lw, sm, i, pos_col, pos_row, tabs, B, S):
    T = B * S
    h = _rms_fwd("norm_mix", x0, D_MODEL, 0, sm['g_mix'][i], T)
    z = _mm("proj_in", h, lw['w_in'], F32)
    sink_row = jnp.pad(sm['sink'][i], (0, LANES - HEADS)).reshape(1, LANES)
    oa_raw, oa, lse_a = _swa_fwd(z, pos_col, pos_row, sink_row, B, S)
    qdn = _rms_fwd("norm_q", z, MLA_Q_LORA, Z_BQD // MLA_Q_LORA, sm['g_q'][i], T)
    kvdn = _rms_fwd("norm_kv", z, MLA_KV_LORA, Z_BKVD // MLA_KV_LORA, sm['g_kv'][i], T)
    q_pre = _mm("proj_uq", qdn, lw['w_uq'], F32)
    kv_pre = _mm("proj_ukv", kvdn, lw['w_ukv'], F32)
    qf, kf, vf = _mla_prep(q_pre, kv_pre, z, tabs, T)
    ob_raw, ob, lse_b = _mla_fwd(qf, kf, vf, z, B, S)
    ua = _mm("proj_br_a", oa, lw['w_br_a'], F32)
    ub = _mm("proj_br_b", ob, lw['w_br_b'], F32)
    y = _merge_fwd(ua, ub, z, T)
    x1 = _mm("proj_out", y, lw['w_out'], F32, residual=x0)
    hn = _rms_fwd("norm_ple", x1, D_MODEL, 0, sm['g_ple'][i], T)
    u = _mm("proj_pg", hn, lw['w_pg'], F32)
    e = _mm("proj_pp", p_i, lw['w_pp'], F32)
    x2 = _ple_fwd(x1, u, e, T)
    saved = dict(x0=x0, h=h, z=z, sink_row=sink_row, oa_raw=oa_raw, oa=oa, lse_a=lse_a, qdn=qdn, kvdn=kvdn,
                 qf=qf, kf=kf, vf=vf, ob_raw=ob_raw, ob=ob, lse_b=lse_b, ua=ua, ub=ub, y=y, x1=x1, hn=hn,
                 u=u, e=e, p=p_i)
    return x2, saved


def _layer_bwd(dx2, sv, lw, sm, i, pos_col, pos_row, tabs, B, S):
    T = B * S
    z = sv['z']
    g = {}
    d_e, d_u = _ple_bwd(dx2, sv['u'], sv['e'], T)
    g['w_pp'] = _mm_tn("grad_pp", sv['p'], d_e)
    g['w_pg'] = _mm_tn("grad_pg", sv['hn'], d_u)
    dhn = _mm("back_pg", d_u, lw['w_pg_t'], F32)
    dx1, g['g_ple'] = _rms_bwd("norm_ple_bwd", sv['x1'], D_MODEL, 0, sm['g_ple'][i], dhn, T, F32, dres=dx2)
    g['w_out'] = _mm_tn("grad_out", sv['y'], dx1)
    dy = _mm("back_out", dx1, lw['w_out_t'], F32)
    d_ua, d_ub, d_ma, d_mb = _merge_bwd(dy, sv['ua'], sv['ub'], z, T)
    g['w_br_a'] = _mm_tn("grad_br_a", sv['oa'], d_ua)
    g['w_br_b'] = _mm_tn("grad_br_b", sv['ob'], d_ub)
    d_oa = _mm("back_br_a", d_ua, lw['w_br_a_t'], F32)
    d_ob = _mm("back_br_b", d_ub, lw['w_br_b_t'], F32)
    dob_raw, d_bgate, delta_b = _gate_bwd("gate_b_bwd", d_ob, sv['ob_raw'], z, Z_BGATE // HPAD, T)
    dq, dk, dv = _mla_bwd(sv['qf'], sv['kf'], sv['vf'], dob_raw, sv['lse_b'], delta_b, B, S)
    dq_pre, dkv_pre, d_bkr = _mla_prep_bwd(dq, dk, dv, tabs, T)
    g['w_uq'] = _mm_tn("grad_uq", sv['qdn'], dq_pre)
    g['w_ukv'] = _mm_tn("grad_ukv", sv['kvdn'], dkv_pre)
    dqdn = _mm("back_uq", dq_pre, lw['w_uq_t'], F32)
    dkvdn = _mm("back_ukv", dkv_pre, lw['w_ukv_t'], F32)
    d_bqd, g['g_q'] = _rms_bwd("norm_q_bwd", z, MLA_Q_LORA, Z_BQD // MLA_Q_LORA, sm['g_q'][i], dqdn, T, BF16)
    d_bkvd, g['g_kv'] = _rms_bwd("norm_kv_bwd", z, MLA_KV_LORA, Z_BKVD // MLA_KV_LORA, sm['g_kv'][i], dkvdn, T, BF16)
    doa_raw, d_agate, delta_a = _gate_bwd("gate_a_bwd", d_oa, sv['oa_raw'], z, Z_AGATE // HPAD, T)
    d_aq, d_ak, d_av, dsink = _swa_bwd(z, pos_col, pos_row, sv['sink_row'], sv['lse_a'], doa_raw, delta_a, B, S)
    g['sink'] = dsink[0, :HEADS]
    dz = jnp.concatenate([d_ma, d_mb, d_aq, d_agate, d_bgate, d_bqd, d_ak.astype(BF16), d_av.astype(BF16),
                          d_bkvd, d_bkr], axis=1)
    g['w_in'] = _mm_tn("grad_in", sv['h'], dz)
    dh = _mm("back_in", dz, lw['w_in_t'], F32)
    dx0, g['g_mix'] = _rms_bwd("norm_mix_bwd", sv['x0'], D_MODEL, 0, sm['g_mix'][i], dh, T, F32, dres=dx1)
    for name in ('g_ple', 'g_q', 'g_kv', 'g_mix'):
        g[name] = g[name][0]
    return dx0, g


def _local_step(x, p, positions, wfull, sm, loss_target):
    B, S, _ = x.shape
    T = B * S
    pos_col = positions.reshape(T, 1)
    pos_row = positions.reshape(T // BLOCK, 1, BLOCK)
    half = MLA_ROPE // 2
    inv = ROPE_THETA ** (-jnp.arange(0, MLA_ROPE, 2, dtype=F32) / MLA_ROPE)
    inv_lane = jnp.tile(inv, LANES // half).reshape(1, LANES)
    tabs = _rope_tables(pos_col, inv_lane, T)
    xc = x.reshape(T, D_MODEL)
    lws, saved = [], []
    for i in range(DEPTH):
        lw = _layer_weights(wfull, i)
        xc, sv = _layer_fwd(xc, p[i].reshape(T, PLE_DIM), lw, sm, i, pos_col, pos_row, tabs, B, S)
        lws.append(lw)
        saved.append(sv)
    dx, loss, dg_final = _loss_head(xc, sm['g_final'], loss_target.reshape(T, D_MODEL), T)
    layer_grads = [None] * DEPTH
    for i in reversed(range(DEPTH)):
        dx, g = _layer_bwd(dx, saved[i], lws[i], sm, i, pos_col, pos_row, tabs, B, S)
        layer_grads[i] = _unpad_grads(g)
    grads = {k: jnp.stack([layer_grads[i][k] for i in range(DEPTH)]) for k in layer_grads[0]}
    grads['g_final'] = dg_final[0]
    return loss, dx.reshape(B, S, D_MODEL), grads


def _flat_rows(shapes):
    n = sum(math.prod(shapes[name]) // N_CHIPS for name, _ in SHARDED)
    n_small = sum(math.prod(shapes[name]) for name in SMALL)
    rows = n // FLAT_W + -(-n_small // FLAT_W)
    return -(-rows // 32) * 32


def _pack_shard(arrs, rows, dtype):
    parts = [arrs[name].reshape(-1) for name, _ in SHARDED] + [arrs[name].reshape(-1) for name in SMALL]
    flat = jnp.concatenate(parts).astype(dtype)
    return jnp.pad(flat, (0, rows * FLAT_W - flat.shape[0])).reshape(rows, FLAT_W)


def _unpack_shard(flat, shard_shapes, small_shapes):
    flat = flat.reshape(-1)
    out, off = {}, 0
    for name, _ in SHARDED:
        n = math.prod(shard_shapes[name])
        out[name] = flat[off:off + n].reshape(shard_shapes[name])
        off += n
    for name in SMALL:
        n = math.prod(small_shapes[name])
        out[name] = flat[off:off + n].reshape(small_shapes[name])
        off += n
    return out


def _chip_shard(full, axis, q):
    n = full.shape[axis] // N_CHIPS
    return lax.slice_in_dim(full, q * n, (q + 1) * n, axis=axis)


def _place():
    x, y, c = lax.axis_index("x"), lax.axis_index("y"), lax.axis_index("c")
    chips = [(1 - x, y), (x, 1 - y), (1 - x, 1 - y)]
    return x, y, c, chips


ANY = pl.BlockSpec(memory_space=pl.ANY)


def _gather_weights(wflat):
    R = wflat.shape[0]
    H = R // 2

    def body(w_ref, out_ref, send_sems, recv_sems, local_sem):
        x, y, c, chips = _place()
        me = 2 * x + y
        sibling = (x, y, 1 - c)

        def half(q, hc):
            return out_ref.at[q, pl.ds(hc * H, H), :]

        def copy(k, src, dst, to):
            return pltpu.make_async_remote_copy(src_ref=src, dst_ref=dst, send_sem=send_sems.at[k],
                                                recv_sem=recv_sems.at[k], device_id=to, device_id_type=MESH)

        mine = pltpu.make_async_copy(w_ref, out_ref.at[me], local_sem)
        mine.start()
        first = [copy(j, w_ref.at[pl.ds(c * H, H), :], half(me, c), (cx, cy, c)) for j, (cx, cy) in enumerate(chips)]
        for cp in first:
            cp.start()
        passed = []
        for j, (cx, cy) in enumerate(chips):
            landed = half(2 * cx + cy, c)
            copy(j, landed, landed, (cx, cy, c)).wait_recv()
            fwd = copy(3 + j, landed, landed, sibling)
            fwd.start()
            passed.append(fwd)
        for j, (cx, cy) in enumerate(chips):
            other = half(2 * cx + cy, 1 - c)
            copy(3 + j, other, other, sibling).wait_recv()
        for cp in first + passed:
            cp.wait_send()
        mine.wait()

    return pl.pallas_call(
        body, name="gather_weights", out_shape=jax.ShapeDtypeStruct((N_CHIPS, R, FLAT_W), wflat.dtype),
        in_specs=[ANY], out_specs=ANY,
        scratch_shapes=[pltpu.SemaphoreType.DMA((6,)), pltpu.SemaphoreType.DMA((6,)), pltpu.SemaphoreType.DMA])(wflat)


def _pair_exchange(g):
    _, R, _ = g.shape
    H = R // 2

    def body(g_ref, mine_ref, theirs_ref, send_sem, recv_sem, local_sem):
        x, y, c, _ = _place()
        keep = pltpu.make_async_copy(g_ref.at[:, pl.ds(c * H, H), :], mine_ref, local_sem)
        keep.start()
        cp = pltpu.make_async_remote_copy(src_ref=g_ref.at[:, pl.ds((1 - c) * H, H), :], dst_ref=theirs_ref,
                                          send_sem=send_sem, recv_sem=recv_sem, device_id=(x, y, 1 - c),
                                          device_id_type=MESH)
        cp.start()
        cp.wait()
        keep.wait()

    shp = jax.ShapeDtypeStruct((N_CHIPS, H, FLAT_W), g.dtype)
    return pl.pallas_call(
        body, name="pair_exchange", out_shape=[shp, shp], in_specs=[ANY], out_specs=[ANY, ANY],
        scratch_shapes=[pltpu.SemaphoreType.DMA, pltpu.SemaphoreType.DMA, pltpu.SemaphoreType.DMA])(g)


def _chip_exchange(part):
    _, H, _ = part.shape

    def body(p_ref, out_ref, send_sems, recv_sems, local_sem):
        x, y, c, chips = _place()
        me = 2 * x + y
        keep = pltpu.make_async_copy(p_ref.at[me], out_ref.at[me], local_sem)
        keep.start()
        sends = []
        for j, (cx, cy) in enumerate(chips):
            cp = pltpu.make_async_remote_copy(src_ref=p_ref.at[2 * cx + cy], dst_ref=out_ref.at[me],
                                              send_sem=send_sems.at[j], recv_sem=recv_sems.at[j],
                                              device_id=(cx, cy, c), device_id_type=MESH)
            cp.start()
            sends.append(cp)
        for j, (cx, cy) in enumerate(chips):
            landed = out_ref.at[2 * cx + cy]
            pltpu.make_async_remote_copy(src_ref=landed, dst_ref=landed, send_sem=send_sems.at[j],
                                         recv_sem=recv_sems.at[j], device_id=(cx, cy, c),
                                         device_id_type=MESH).wait_recv()
        for cp in sends:
            cp.wait_send()
        keep.wait()

    return pl.pallas_call(
        body, name="chip_exchange", out_shape=jax.ShapeDtypeStruct(part.shape, part.dtype),
        in_specs=[ANY], out_specs=ANY,
        scratch_shapes=[pltpu.SemaphoreType.DMA((3,)), pltpu.SemaphoreType.DMA((3,)), pltpu.SemaphoreType.DMA])(part)


def _pair_broadcast(half):
    H = half.shape[0]

    def body(h_ref, out_ref, send_sem, recv_sem, local_sem):
        x, y, c, _ = _place()
        keep = pltpu.make_async_copy(h_ref, out_ref.at[c], local_sem)
        keep.start()
        cp = pltpu.make_async_remote_copy(src_ref=h_ref, dst_ref=out_ref.at[c], send_sem=send_sem,
                                          recv_sem=recv_sem, device_id=(x, y, 1 - c), device_id_type=MESH)
        cp.start()
        other = out_ref.at[1 - c]
        pltpu.make_async_remote_copy(src_ref=other, dst_ref=other, send_sem=send_sem, recv_sem=recv_sem,
                                     device_id=(x, y, 1 - c), device_id_type=MESH).wait_recv()
        cp.wait_send()
        keep.wait()

    return pl.pallas_call(
        body, name="pair_broadcast", out_shape=jax.ShapeDtypeStruct((2, H, FLAT_W), half.dtype),
        in_specs=[ANY], out_specs=ANY,
        scratch_shapes=[pltpu.SemaphoreType.DMA, pltpu.SemaphoreType.DMA, pltpu.SemaphoreType.DMA])(half)


def _add2(name, a, b, rows):
    def body(ins, outs, _):
        outs[0][...] = ins[0][...] + ins[1][...]
    return _ew(name, body, [(a, FLAT_W, 0), (b, FLAT_W, 0)], [(FLAT_W, F32)], rows, 248)[0]


def _add4(name, parts, rows):
    def body(ins, outs, _):
        outs[0][...] = ((ins[0][0] + ins[0][1]) + ins[0][2]) + ins[0][3]
    tm = 248
    return pl.pallas_call(
        lambda p_ref, o_ref: body([p_ref], [o_ref], None), name=name, grid=(rows // tm,),
        in_specs=[pl.BlockSpec((N_CHIPS, tm, FLAT_W), lambda i: (0, i, 0))],
        out_specs=pl.BlockSpec((tm, FLAT_W), lambda i: (i, 0)),
        out_shape=jax.ShapeDtypeStruct((rows, FLAT_W), F32), compiler_params=_params(("arbitrary",)))(parts)


def _reduce_scatter(g):
    _, R, _ = g.shape
    H = R // 2
    mine, theirs = _pair_exchange(g)
    part = _add2("pair_sum", mine.reshape(N_CHIPS * H, FLAT_W), theirs.reshape(N_CHIPS * H, FLAT_W), N_CHIPS * H)
    landed = _chip_exchange(part.reshape(N_CHIPS, H, FLAT_W))
    half = _add4("chip_sum", landed, H)
    return _pair_broadcast(half).reshape(R, FLAT_W)


def _adamw(g, w, m, v, rows):
    def body(ins, outs, _):
        gv, wv, mv, vv = (r[...] for r in ins)
        mv = ADAM_B1 * mv + (1.0 - ADAM_B1) * gv
        vv = ADAM_B2 * vv + (1.0 - ADAM_B2) * (gv * gv)
        m_hat = mv / (1.0 - ADAM_B1 ** ADAM_STEP)
        v_hat = vv / (1.0 - ADAM_B2 ** ADAM_STEP)
        outs[0][...] = -ADAM_LR * (m_hat / (jnp.sqrt(v_hat) + ADAM_EPS) + ADAM_WD * wv)
        outs[1][...] = mv
        outs[2][...] = vv
    ins = [(a, FLAT_W, 0) for a in (g, w, m, v)]
    return _ew("adamw", body, ins, [(FLAT_W, F32)] * 3, rows, 248)


def kernel(x, p, positions, g_mix, w_in, sink, g_q, w_uq, g_kv, w_ukv, w_br_a, w_br_b, w_out, g_ple, w_ple_gate, w_ple_proj, g_final, loss_target, m_g_mix, m_w_in, m_sink, m_g_q, m_w_uq, m_g_kv, m_w_ukv, m_w_br_a, m_w_br_b, m_w_out, m_g_ple, m_w_ple_gate, m_w_ple_proj, m_g_final, v_g_mix, v_w_in, v_sink, v_g_q, v_w_uq, v_g_kv, v_w_ukv, v_w_br_a, v_w_br_b, v_w_out, v_g_ple, v_w_ple_gate, v_w_ple_proj, v_g_final):
    w = dict(g_mix=g_mix, w_in=w_in, sink=sink, g_q=g_q, w_uq=w_uq, g_kv=g_kv, w_ukv=w_ukv, w_br_a=w_br_a,
             w_br_b=w_br_b, w_out=w_out, g_ple=g_ple, w_ple_gate=w_ple_gate, w_ple_proj=w_ple_proj, g_final=g_final)
    m = dict(g_mix=m_g_mix, w_in=m_w_in, sink=m_sink, g_q=m_g_q, w_uq=m_w_uq, g_kv=m_g_kv, w_ukv=m_w_ukv,
             w_br_a=m_w_br_a, w_br_b=m_w_br_b, w_out=m_w_out, g_ple=m_g_ple, w_ple_gate=m_w_ple_gate,
             w_ple_proj=m_w_ple_proj, g_final=m_g_final)
    v = dict(g_mix=v_g_mix, w_in=v_w_in, sink=v_sink, g_q=v_g_q, w_uq=v_w_uq, g_kv=v_g_kv, w_ukv=v_w_ukv,
             w_br_a=v_w_br_a, w_br_b=v_w_br_b, w_out=v_w_out, g_ple=v_g_ple, w_ple_gate=v_w_ple_gate,
             w_ple_proj=v_w_ple_proj, g_final=v_g_final)
    shard_shapes = {name: w[name].shape for name, _ in SHARDED}
    small_shapes = {name: w[name].shape for name in SMALL}
    full_shapes = {name: tuple(d * (N_CHIPS if a == axis else 1) for a, d in enumerate(w[name].shape))
                   for name, axis in SHARDED}
    R = _flat_rows({**full_shapes, **small_shapes})

    gathered = _gather_weights(_pack_shard(w, R, BF16))
    pieces = [_unpack_shard(gathered[q], shard_shapes, small_shapes) for q in range(N_CHIPS)]
    wfull = {name: jnp.concatenate([pieces[q][name] for q in range(N_CHIPS)], axis=axis) for name, axis in SHARDED}
    sm = {name: w[name] for name in SMALL}

    loss_row, grad_x, grads = _local_step(x, p, positions, wfull, sm, loss_target)
    loss = lax.psum(loss_row[0, 0], ("x", "y", "c"))

    gflat = jnp.stack([_pack_shard({**{name: _chip_shard(grads[name], axis, q) for name, axis in SHARDED},
                                    **{name: grads[name] for name in SMALL}}, R, F32) for q in range(N_CHIPS)])
    gsum = _reduce_scatter(gflat)

    delta, new_m, new_v = _adamw(gsum, _pack_shard(w, R, F32), _pack_shard(m, R, F32), _pack_shard(v, R, F32), R)
    outs = [_unpack_shard(a, shard_shapes, small_shapes) for a in (gsum, delta, new_m, new_v)]
    return (loss, grad_x, *[o[name] for o in outs for name in WEIGHT_NAMES])
```

```python
import functools
import math

import jax
import jax.numpy as jnp
from jax import lax
from jax.experimental import pallas as pl
from jax.experimental.pallas import tpu as pltpu

F32 = jnp.float32
BF16 = jnp.bfloat16

D_MODEL = 1024
DEPTH = 2
PLE_DIM = 256
BLOCK = 128
EPS = 1e-6
NEG = -1e30
HEADS = 8
SWA_KV_HEADS = 2
HEAD_DIM = 64
LANES = 128
HPAD = HEADS * LANES
MLA_QK = 96
MLA_ROPE = 32
MLA_Q_LORA = 256
MLA_KV_LORA = 128
ROPE_THETA = 10000.0
IN_SIZES = (512, 128, 128, 512, 256, 128, 32, 512, 1024, 1024)

Z_MA, Z_MB, Z_AQ, Z_AGATE, Z_BGATE = 0, 1024, 2048, 3072, 4096
Z_BQD, Z_AK, Z_AV, Z_BKVD, Z_BKR = 5120, 5376, 5632, 5888, 6016
Z_WIDTH = 6144

ADAM_LR, ADAM_B1, ADAM_B2, ADAM_EPS, ADAM_WD, ADAM_STEP = 0.001, 0.9, 0.999, 1e-08, 0.01, 10

VMEM_LIMIT = 56 * 1024 * 1024
MESH = pl.DeviceIdType.MESH

WEIGHT_NAMES = ('g_mix', 'w_in', 'sink', 'g_q', 'w_uq', 'g_kv', 'w_ukv', 'w_br_a', 'w_br_b',
                'w_out', 'g_ple', 'w_ple_gate', 'w_ple_proj', 'g_final')
SHARDED = (('w_in', 2), ('w_uq', 2), ('w_ukv', 2), ('w_br_a', 2), ('w_br_b', 2),
           ('w_out', 1), ('w_ple_gate', 1), ('w_ple_proj', 2))
SMALL = ('g_mix', 'sink', 'g_q', 'g_kv', 'g_ple', 'g_final')
N_CHIPS = 4


def _params(sem):
    return pltpu.CompilerParams(dimension_semantics=sem, vmem_limit_bytes=VMEM_LIMIT)


def _ew(name, body, ins, outs, rows, tm, accs=()):
    n_in, n_out = len(ins), len(outs)
    in_specs, args = [], []
    for arr, width, cb in ins:
        if width is None:
            in_specs.append(pl.BlockSpec(arr.shape, lambda i, nd=arr.ndim: (0,) * nd))
        else:
            in_specs.append(pl.BlockSpec((tm, width), lambda i, cb=cb: (i, cb)))
        args.append(arr)
    out_shape = [jax.ShapeDtypeStruct((rows, w), dt) for w, dt in outs]
    out_shape += [jax.ShapeDtypeStruct(s, F32) for s in accs]
    out_specs = [pl.BlockSpec((tm, w), lambda i: (i, 0)) for w, _ in outs]
    out_specs += [pl.BlockSpec(s, lambda i: (0, 0)) for s in accs]

    def kern(*refs):
        acc_refs = refs[n_in + n_out:]
        if acc_refs:
            @pl.when(pl.program_id(0) == 0)
            def _():
                for r in acc_refs:
                    r[...] = jnp.zeros_like(r)
        body(refs[:n_in], refs[n_in:n_in + n_out], acc_refs)

    res = pl.pallas_call(kern, name=name, grid=(rows // tm,), in_specs=in_specs, out_specs=out_specs,
                         out_shape=out_shape, compiler_params=_params(("arbitrary",)))(*args)
    return res


def _rms_fwd(name, x, width, cb, g, rows):
    def body(ins, outs, _):
        xv = ins[0][...]
        r = lax.rsqrt(jnp.mean(xv * xv, axis=-1, keepdims=True) + EPS)
        outs[0][...] = ((xv * r) * ins[1][...]).astype(BF16)
    return _ew(name, body, [(x, width, cb), (g.reshape(1, width), None, None)], [(width, BF16)], rows, 256)[0]


def _rms_bwd(name, x, width, cb, g, dh, rows, out_dtype, dres=None):
    def body(ins, outs, accs):
        xv, gv, dhv = ins[0][...], ins[1][...], ins[2][...].astype(F32)
        r = lax.rsqrt(jnp.mean(xv * xv, axis=-1, keepdims=True) + EPS)
        xhat = xv * r
        accs[0][...] += jnp.sum(dhv * xhat, axis=0, keepdims=True)
        dy = dhv * gv
        dx = r * (dy - xhat * jnp.mean(dy * xhat, axis=-1, keepdims=True))
        if dres is not None:
            dx = dx + ins[3][...]
        outs[0][...] = dx.astype(out_dtype)
    ins = [(x, width, cb), (g.reshape(1, width), None, None), (dh, width, 0)]
    if dres is not None:
        ins.append((dres, width, 0))
    return _ew(name, body, ins, [(width, out_dtype)], rows, 256, accs=[(1, width)])


def _mm(name, a, b, out_dtype, residual=None, tm=512, tn=512, tk=1024):
    M, K = a.shape
    N = b.shape[1]
    tm, tn, tk = min(tm, M), min(tn, N), min(tk, K)
    nk = K // tk
    has_res = residual is not None

    def kern(*refs):
        a_ref, b_ref = refs[0], refs[1]
        res_ref = refs[2] if has_res else None
        o_ref = refs[3] if has_res else refs[2]
        part = jnp.dot(a_ref[...].astype(BF16), b_ref[...].astype(BF16), preferred_element_type=F32)
        if nk == 1:
            if has_res:
                part = part + res_ref[...]
            o_ref[...] = part.astype(o_ref.dtype)
            return
        acc_ref = refs[-1]
        k = pl.program_id(2)

        @pl.when(k == 0)
        def _():
            acc_ref[...] = part

        @pl.when(k > 0)
        def _():
            acc_ref[...] += part

        @pl.when(k == nk - 1)
        def _():
            tot = acc_ref[...]
            if has_res:
                tot = tot + res_ref[...]
            o_ref[...] = tot.astype(o_ref.dtype)

    in_specs = [pl.BlockSpec((tm, tk), lambda i, j, k: (i, k)), pl.BlockSpec((tk, tn), lambda i, j, k: (k, j))]
    args = [a, b]
    if has_res:
        in_specs.append(pl.BlockSpec((tm, tn), lambda i, j, k: (i, j)))
        args.append(residual)
    scratch = [pltpu.VMEM((tm, tn), F32)] if nk > 1 else []
    return pl.pallas_call(
        kern, name=name, grid=(M // tm, N // tn, nk), in_specs=in_specs,
        out_specs=pl.BlockSpec((tm, tn), lambda i, j, k: (i, j)),
        out_shape=jax.ShapeDtypeStruct((M, N), out_dtype), scratch_shapes=scratch,
        compiler_params=_params(("parallel", "parallel", "arbitrary")))(*args)


def _mm_tn(name, a, b, tm=512, tn=1024, tk=512):
    T, M = a.shape
    N = b.shape[1]
    tm, tn, tk = min(tm, M), min(tn, N), min(tk, T)
    nk = T // tk

    def kern(a_ref, b_ref, o_ref):
        k = pl.program_id(2)
        part = lax.dot_general(a_ref[...].astype(BF16), b_ref[...].astype(BF16),
                               (((0,), (0,)), ((), ())), preferred_element_type=F32)

        @pl.when(k == 0)
        def _():
            o_ref[...] = part

        @pl.when(k > 0)
        def _():
            o_ref[...] += part

    return pl.pallas_call(
        kern, name=name, grid=(M // tm, N // tn, nk),
        in_specs=[pl.BlockSpec((tk, tm), lambda i, j, k: (k, i)), pl.BlockSpec((tk, tn), lambda i, j, k: (k, j))],
        out_specs=pl.BlockSpec((tm, tn), lambda i, j, k: (i, j)),
        out_shape=jax.ShapeDtypeStruct((M, N), F32),
        compiler_params=_params(("parallel", "parallel", "arbitrary")))(a, b)


def _dot_nt(a, b):
    return lax.dot_general(a, b, (((1,), (1,)), ((), ())), preferred_element_type=F32)


def _dot_tn(a, b):
    return lax.dot_general(a, b, (((0,), (0,)), ((), ())), preferred_element_type=F32)


def _swa_scores(n, q_all, kb, posq, posk, h):
    kvh = h // (HEADS // SWA_KV_HEADS)
    qh = q_all[:, h * LANES:(h + 1) * LANES].astype(BF16)
    kh = kb[:, kvh * LANES:(kvh + 1) * LANES]
    dist = (posq - posk).astype(F32)
    qi = lax.broadcasted_iota(jnp.int32, (BLOCK, 2 * BLOCK), 0)
    kj = lax.broadcasted_iota(jnp.int32, (BLOCK, 2 * BLOCK), 1)
    t_abs = n * BLOCK + qi
    s_abs = n * BLOCK - BLOCK + kj
    valid = (s_abs >= 0) & (s_abs <= t_abs) & (t_abs - s_abs < BLOCK)
    s = _dot_nt(qh, kh) * (HEAD_DIM ** -0.5) - (2.0 ** -(h + 1)) * dist
    return jnp.where(valid, s, NEG), qh, kh


def _swa_specs(nb):
    prev = lambda b, n: b * nb + jnp.maximum(n - 1, 0)
    own = lambda b, n: b * nb + n
    return [
        pl.BlockSpec((BLOCK, HPAD), lambda b, n: (own(b, n), Z_AQ // HPAD)),
        pl.BlockSpec((BLOCK, 256), lambda b, n: (prev(b, n), Z_AK // 256)),
        pl.BlockSpec((BLOCK, 256), lambda b, n: (own(b, n), Z_AK // 256)),
        pl.BlockSpec((BLOCK, 256), lambda b, n: (prev(b, n), Z_AV // 256)),
        pl.BlockSpec((BLOCK, 256), lambda b, n: (own(b, n), Z_AV // 256)),
        pl.BlockSpec((BLOCK, 1), lambda b, n: (own(b, n), 0)),
        pl.BlockSpec((1, 1, BLOCK), lambda b, n: (prev(b, n), 0, 0)),
        pl.BlockSpec((1, 1, BLOCK), lambda b, n: (own(b, n), 0, 0)),
    ]


def _swa_fwd(z, pos_col, pos_row, sink_row, B, S):
    nb = S // BLOCK
    T = B * S

    def kern(q_ref, kp_ref, kc_ref, vp_ref, vc_ref, pq_ref, pkp_ref, pkc_ref, gate_ref, sink_ref,
             oraw_ref, og_ref, lse_ref):
        n = pl.program_id(1)
        q_all = q_ref[...]
        kb = jnp.concatenate([kp_ref[...], kc_ref[...]], axis=0).astype(BF16)
        vb = jnp.concatenate([vp_ref[...], vc_ref[...]], axis=0).astype(BF16)
        posq = pq_ref[...]
        posk = jnp.concatenate([pkp_ref[0], pkc_ref[0]], axis=1)
        lane = lax.broadcasted_iota(jnp.int32, (BLOCK, LANES), 1)
        lse_all = jnp.zeros((BLOCK, LANES), F32)
        for h in range(HEADS):
            kvh = h // (HEADS // SWA_KV_HEADS)
            s, _, _ = _swa_scores(n, q_all, kb, posq, posk, h)
            sink_h = sink_ref[0:1, h:h + 1]
            m = jnp.maximum(jnp.max(s, axis=-1, keepdims=True), sink_h)
            e = jnp.exp(s - m)
            denom = jnp.sum(e, axis=-1, keepdims=True) + jnp.exp(sink_h - m)
            probs = e * (1.0 / denom)
            o = jnp.dot(probs.astype(BF16), vb[:, kvh * LANES:(kvh + 1) * LANES], preferred_element_type=F32)
            cols = slice(h * LANES, (h + 1) * LANES)
            oraw_ref[:, cols] = o
            g = gate_ref[:, cols]
            og_ref[:, cols] = (o * (g * jax.nn.sigmoid(g))).astype(BF16)
            lse_all = jnp.where(lane == h, m + jnp.log(denom), lse_all)
        lse_ref[...] = lse_all

    own = lambda b, n: b * nb + n
    in_specs = _swa_specs(nb) + [
        pl.BlockSpec((BLOCK, HPAD), lambda b, n: (own(b, n), Z_AGATE // HPAD)),
        pl.BlockSpec((1, LANES), lambda b, n: (0, 0)),
    ]
    out_specs = [pl.BlockSpec((BLOCK, HPAD), lambda b, n: (own(b, n), 0)),
                 pl.BlockSpec((BLOCK, HPAD), lambda b, n: (own(b, n), 0)),
                 pl.BlockSpec((BLOCK, LANES), lambda b, n: (own(b, n), 0))]
    out_shape = [jax.ShapeDtypeStruct((T, HPAD), F32), jax.ShapeDtypeStruct((T, HPAD), BF16),
                 jax.ShapeDtypeStruct((T, LANES), F32)]
    return pl.pallas_call(kern, name="swa_fwd", grid=(B, nb), in_specs=in_specs, out_specs=out_specs,
                          out_shape=out_shape, compiler_params=_params(("parallel", "arbitrary")))(
        z, z, z, z, z, pos_col, pos_row, pos_row, z, sink_row)


def _swa_bwd(z, pos_col, pos_row, sink_row, lse, do_raw, delta, B, S):
    nb = S // BLOCK
    T = B * S

    def kern(q_ref, kp_ref, kc_ref, vp_ref, vc_ref, pq_ref, pkp_ref, pkc_ref, sink_ref, lse_ref, do_ref,
             delta_ref, dq_ref, dk_ref, dv_ref, dsink_ref):
        b, n = pl.program_id(0), pl.program_id(1)

        @pl.when(n == 0)
        def _():
            dk_ref[...] = jnp.zeros_like(dk_ref)
            dv_ref[...] = jnp.zeros_like(dv_ref)

        @pl.when((b == 0) & (n == 0))
        def _():
            dsink_ref[...] = jnp.zeros_like(dsink_ref)

        q_all = q_ref[...]
        kb = jnp.concatenate([kp_ref[...], kc_ref[...]], axis=0).astype(BF16)
        vb = jnp.concatenate([vp_ref[...], vc_ref[...]], axis=0).astype(BF16)
        posq = pq_ref[...]
        posk = jnp.concatenate([pkp_ref[0], pkc_ref[0]], axis=1)
        lane1 = lax.broadcasted_iota(jnp.int32, (1, LANES), 1)
        dsink = jnp.zeros((1, LANES), F32)
        dk_band = [jnp.zeros((2 * BLOCK, LANES), F32) for _ in range(SWA_KV_HEADS)]
        dv_band = [jnp.zeros((2 * BLOCK, LANES), F32) for _ in range(SWA_KV_HEADS)]
        for h in range(HEADS):
            kvh = h // (HEADS // SWA_KV_HEADS)
            cols = slice(h * LANES, (h + 1) * LANES)
            s, qh, kh = _swa_scores(n, q_all, kb, posq, posk, h)
            lse_h = lse_ref[:, h:h + 1]
            p = jnp.exp(s - lse_h)
            do = do_ref[:, cols]
            delta_h = delta_ref[:, h * LANES:h * LANES + 1]
            dp = _dot_nt(do, vb[:, kvh * LANES:(kvh + 1) * LANES])
            ds = (p * (dp - delta_h) * (HEAD_DIM ** -0.5)).astype(BF16)
            dq_ref[:, cols] = jnp.dot(ds, kh, preferred_element_type=F32).astype(BF16)
            dk_band[kvh] = dk_band[kvh] + _dot_tn(ds, qh)
            dv_band[kvh] = dv_band[kvh] + _dot_tn(p.astype(BF16), do)
            psink = jnp.exp(sink_ref[0:1, h:h + 1] - lse_h)
            dsink = dsink + jnp.where(lane1 == h, -jnp.sum(psink * delta_h, axis=0, keepdims=True), 0.0)
        dsink_ref[...] += dsink
        dkb = jnp.concatenate(dk_band, axis=1)
        dvb = jnp.concatenate(dv_band, axis=1)
        r_prev = pl.ds(pl.multiple_of(jnp.maximum(n - 1, 0) * BLOCK, BLOCK), BLOCK)
        r_own = pl.ds(pl.multiple_of(n * BLOCK, BLOCK), BLOCK)
        dk_ref[r_prev, :] += dkb[:BLOCK]
        dk_ref[r_own, :] += dkb[BLOCK:]
        dv_ref[r_prev, :] += dvb[:BLOCK]
        dv_ref[r_own, :] += dvb[BLOCK:]

    own = lambda b, n: b * nb + n
    in_specs = _swa_specs(nb) + [
        pl.BlockSpec((1, LANES), lambda b, n: (0, 0)),
        pl.BlockSpec((BLOCK, LANES), lambda b, n: (own(b, n), 0)),
        pl.BlockSpec((BLOCK, HPAD), lambda b, n: (own(b, n), 0)),
        pl.BlockSpec((BLOCK, HPAD), lambda b, n: (own(b, n), 0)),
    ]
    out_specs = [pl.BlockSpec((BLOCK, HPAD), lambda b, n: (own(b, n), 0)),
                 pl.BlockSpec((S, 256), lambda b, n: (b, 0)),
                 pl.BlockSpec((S, 256), lambda b, n: (b, 0)),
                 pl.BlockSpec((1, LANES), lambda b, n: (0, 0))]
    out_shape = [jax.ShapeDtypeStruct((T, HPAD), BF16), jax.ShapeDtypeStruct((T, 256), F32),
                 jax.ShapeDtypeStruct((T, 256), F32), jax.ShapeDtypeStruct((1, LANES), F32)]
    return pl.pallas_call(kern, name="swa_bwd", grid=(B, nb), in_specs=in_specs, out_specs=out_specs,
                          out_shape=out_shape, compiler_params=_params(("arbitrary", "arbitrary")))(
        z, z, z, z, z, pos_col, pos_row, pos_row, sink_row, lse, do_raw, delta)


MLA_T = 256
MLA_SCALE = MLA_QK ** -0.5


def _causal(s):
    row = lax.broadcasted_iota(jnp.int32, s.shape, 0)
    col = lax.broadcasted_iota(jnp.int32, s.shape, 1)
    return jnp.where(col <= row, s, NEG)


def _mla_fwd(q, k, v, z, B, S):
    T = B * S
    nq = S // MLA_T

    def kern(q_ref, k_ref, v_ref, gate_ref, oraw_ref, og_ref, lse_ref):
        i = pl.program_id(2)
        qv = q_ref[...]

        def step(j, carry, masked):
            m, l, acc = carry
            rows = pl.ds(pl.multiple_of(j * MLA_T, MLA_T), MLA_T)
            s = _dot_nt(qv, k_ref[rows, :]) * MLA_SCALE
            if masked:
                s = _causal(s)
            m_new = jnp.maximum(m, jnp.max(s, axis=-1, keepdims=True))
            alpha = jnp.exp(m - m_new)
            p = jnp.exp(s - m_new)
            l = alpha * l + jnp.sum(p, axis=-1, keepdims=True)
            acc = alpha * acc + jnp.dot(p.astype(BF16), v_ref[rows, :], preferred_element_type=F32)
            return m_new, l, acc

        init = (jnp.full((MLA_T, 1), NEG, F32), jnp.zeros((MLA_T, 1), F32), jnp.zeros((MLA_T, LANES), F32))
        carry = lax.fori_loop(0, i, lambda j, c: step(j, c, False), init)
        m, l, acc = step(i, carry, True)
        o = acc * (1.0 / l)
        oraw_ref[...] = o
        g = gate_ref[...]
        og_ref[...] = (o * (g * jax.nn.sigmoid(g))).astype(BF16)
        lse_ref[...] = jnp.broadcast_to(m + jnp.log(l), (MLA_T, LANES))

    blk = lambda b, h, i: (b * nq + i, h)
    in_specs = [pl.BlockSpec((MLA_T, LANES), blk),
                pl.BlockSpec((S, LANES), lambda b, h, i: (b, h)),
                pl.BlockSpec((S, LANES), lambda b, h, i: (b, h)),
                pl.BlockSpec((MLA_T, LANES), lambda b, h, i: (b * nq + i, Z_BGATE // LANES + h))]
    out_specs = [pl.BlockSpec((MLA_T, LANES), blk)] * 3
    out_shape = [jax.ShapeDtypeStruct((T, HPAD), F32), jax.ShapeDtypeStruct((T, HPAD), BF16),
                 jax.ShapeDtypeStruct((T, HPAD), F32)]
    return pl.pallas_call(kern, name="mla_fwd", grid=(B, HEADS, nq), in_specs=in_specs, out_specs=out_specs,
                          out_shape=out_shape,
                          compiler_params=_params(("parallel", "parallel", "arbitrary")))(q, k, v, z)


def _mla_bwd(q, k, v, do_raw, lse, delta, B, S):
    T = B * S
    nk = S // MLA_T

    def kern(q_ref, k_ref, v_ref, do_ref, lse_ref, delta_ref, dq_ref, dk_ref, dv_ref, dk_acc, dv_acc):
        j = pl.program_id(2)

        @pl.when(j == 0)
        def _():
            dq_ref[...] = jnp.zeros_like(dq_ref)

        dk_acc[...] = jnp.zeros_like(dk_acc)
        dv_acc[...] = jnp.zeros_like(dv_acc)
        kv, vv = k_ref[...], v_ref[...]

        def step(i, masked):
            rows = pl.ds(pl.multiple_of(i * MLA_T, MLA_T), MLA_T)
            qv, do = q_ref[rows, :], do_ref[rows, :]
            s = _dot_nt(qv, kv) * MLA_SCALE
            if masked:
                s = _causal(s)
            p = jnp.exp(s - lse_ref[rows, 0:1])
            dp = _dot_nt(do, vv)
            ds = (p * (dp - delta_ref[rows, 0:1]) * MLA_SCALE).astype(BF16)
            dv_acc[...] += _dot_tn(p.astype(BF16), do)
            dk_acc[...] += _dot_tn(ds, qv)
            dq_ref[rows, :] += jnp.dot(ds, kv, preferred_element_type=F32)

        step(j, True)

        def body(i, c):
            step(i, False)
            return c

        lax.fori_loop(j + 1, nk, body, 0)
        dk_ref[...] = dk_acc[...]
        dv_ref[...] = dv_acc[...]

    whole = lambda b, h, j: (b, h)
    tile = lambda b, h, j: (b * nk + j, h)
    in_specs = [pl.BlockSpec((S, LANES), whole), pl.BlockSpec((MLA_T, LANES), tile),
                pl.BlockSpec((MLA_T, LANES), tile), pl.BlockSpec((S, LANES), whole),
                pl.BlockSpec((S, LANES), whole), pl.BlockSpec((S, LANES), whole)]
    out_specs = [pl.BlockSpec((S, LANES), whole), pl.BlockSpec((MLA_T, LANES), tile),
                 pl.BlockSpec((MLA_T, LANES), tile)]
    out_shape = [jax.ShapeDtypeStruct((T, HPAD), F32)] * 3
    return pl.pallas_call(kern, name="mla_bwd", grid=(B, HEADS, nk), in_specs=in_specs, out_specs=out_specs,
                          out_shape=out_shape,
                          scratch_shapes=[pltpu.VMEM((MLA_T, LANES), F32), pltpu.VMEM((MLA_T, LANES), F32)],
                          compiler_params=_params(("parallel", "parallel", "arbitrary")))(
        q, k, v, do_raw, lse, delta)


def _rope_tables(pos_col, inv_lane, rows):
    def body(ins, outs, _):
        ang = ins[0][...].astype(F32) * ins[1][...]
        lane = lax.broadcasted_iota(jnp.int32, ang.shape, 1)
        cos, sin = jnp.cos(ang), jnp.sin(ang)
        first = (lane >= HEAD_DIM) & (lane < HEAD_DIM + MLA_ROPE // 2)
        second = (lane >= HEAD_DIM + MLA_ROPE // 2) & (lane < MLA_QK)
        outs[0][...] = jnp.where(lane < HEAD_DIM, 1.0, jnp.where(lane < MLA_QK, cos, 0.0))
        outs[1][...] = jnp.where(first, -sin, 0.0)
        outs[2][...] = jnp.where(second, sin, 0.0)
    return _ew("rope_tables", body, [(pos_col, 1, 0), (inv_lane, None, None)], [(LANES, F32)] * 3, rows, 256)


def _rope(x, c, s1, s2):
    return x * c + pltpu.roll(x, 112, 1) * s1 + pltpu.roll(x, 16, 1) * s2


def _rope_t(d, c, s1, s2):
    return d * c + pltpu.roll(d * s1, 16, 1) + pltpu.roll(d * s2, 112, 1)


def _mla_prep(q_pre, kv_pre, z, tabs, rows):
    def body(ins, outs, _):
        c, s1, s2 = ins[3][...], ins[4][...], ins[5][...]
        kr = _rope(ins[2][...], c, s1, s2)
        for h in range(HEADS):
            cols = slice(h * LANES, (h + 1) * LANES)
            outs[0][:, cols] = _rope(ins[0][:, cols], c, s1, s2).astype(BF16)
            outs[1][:, cols] = (ins[1][:, cols] + kr).astype(BF16)
        outs[2][...] = ins[6][...].astype(BF16)
    ins = [(q_pre, HPAD, 0), (kv_pre, HPAD, 0), (z, LANES, Z_BKR // LANES),
           (tabs[0], LANES, 0), (tabs[1], LANES, 0), (tabs[2], LANES, 0), (kv_pre, HPAD, 1)]
    return _ew("mla_prep", body, ins, [(HPAD, BF16)] * 3, rows, 256)


def _mla_prep_bwd(dq, dk, dv, tabs, rows):
    def body(ins, outs, _):
        c, s1, s2 = ins[3][...], ins[4][...], ins[5][...]
        lane = lax.broadcasted_iota(jnp.int32, c.shape, 1)
        dkr = jnp.zeros(c.shape, F32)
        for h in range(HEADS):
            cols = slice(h * LANES, (h + 1) * LANES)
            outs[0][:, cols] = _rope_t(ins[0][:, cols], c, s1, s2).astype(BF16)
            dkh = ins[1][:, cols]
            outs[1][:, cols] = jnp.where(lane < HEAD_DIM, dkh, 0.0).astype(BF16)
            dkr = dkr + dkh
        outs[1][:, HPAD:] = ins[2][...].astype(BF16)
        live = (lane >= HEAD_DIM) & (lane < MLA_QK)
        outs[2][...] = jnp.where(live, _rope_t(jnp.where(live, dkr, 0.0), c, s1, s2), 0.0).astype(BF16)
    ins = [(dq, HPAD, 0), (dk, HPAD, 0), (dv, HPAD, 0), (tabs[0], LANES, 0), (tabs[1], LANES, 0),
           (tabs[2], LANES, 0)]
    return _ew("mla_prep_bwd", body, ins, [(HPAD, BF16), (2 * HPAD, BF16), (LANES, BF16)], rows, 256)


def _gate_bwd(name, d_o, o_raw, z, gate_cb, rows):
    def body(ins, outs, _):
        for h in range(HEADS):
            cols = slice(h * LANES, (h + 1) * LANES)
            dog, o, g = ins[0][:, cols], ins[1][:, cols], ins[2][:, cols]
            sg = jax.nn.sigmoid(g)
            do = dog * (g * sg)
            outs[0][:, cols] = do.astype(BF16)
            outs[1][:, cols] = (dog * o * (sg * (1.0 + g * (1.0 - sg)))).astype(BF16)
            outs[2][:, cols] = jnp.broadcast_to(jnp.sum(do * o, axis=-1, keepdims=True), do.shape)
    ins = [(d_o, HPAD, 0), (o_raw, HPAD, 0), (z, HPAD, gate_cb)]
    return _ew(name, body, ins, [(HPAD, BF16), (HPAD, BF16), (HPAD, F32)], rows, 256)


def _merge_fwd(ua, ub, z, rows):
    def body(ins, outs, _):
        y = jax.nn.sigmoid(ins[2][...]) * ins[0][...] + jax.nn.sigmoid(ins[3][...]) * ins[1][...]
        outs[0][...] = y.astype(BF16)
    ins = [(ua, D_MODEL, 0), (ub, D_MODEL, 0), (z, D_MODEL, Z_MA // D_MODEL), (z, D_MODEL, Z_MB // D_MODEL)]
    return _ew("merge_fwd", body, ins, [(D_MODEL, BF16)], rows, 256)[0]


def _merge_bwd(dy, ua, ub, z, rows):
    def body(ins, outs, _):
        dyv = ins[0][...]
        for idx in range(2):
            s = jax.nn.sigmoid(ins[3 + idx][...])
            outs[idx][...] = (dyv * s).astype(BF16)
            outs[2 + idx][...] = (dyv * ins[1 + idx][...] * (s * (1.0 - s))).astype(BF16)
    ins = [(dy, D_MODEL, 0), (ua, D_MODEL, 0), (ub, D_MODEL, 0),
           (z, D_MODEL, Z_MA // D_MODEL), (z, D_MODEL, Z_MB // D_MODEL)]
    return _ew("merge_bwd", body, ins, [(D_MODEL, BF16)] * 4, rows, 256)


def _ple_fwd(x1, u, e, rows):
    def body(ins, outs, _):
        outs[0][...] = ins[0][...] + jax.nn.sigmoid(ins[1][...]) * ins[2][...]
    return _ew("ple_fwd", body, [(x1, D_MODEL, 0), (u, D_MODEL, 0), (e, D_MODEL, 0)], [(D_MODEL, F32)], rows, 256)[0]


def _ple_bwd(dx2, u, e, rows):
    def body(ins, outs, _):
        d, s = ins[0][...], jax.nn.sigmoid(ins[1][...])
        outs[0][...] = (d * s).astype(BF16)
        outs[1][...] = (d * ins[2][...] * (s * (1.0 - s))).astype(BF16)
    return _ew("ple_bwd", body, [(dx2, D_MODEL, 0), (u, D_MODEL, 0), (e, D_MODEL, 0)],
               [(D_MODEL, BF16)] * 2, rows, 256)


def _loss_head(x, g, target, rows):
    def body(ins, outs, accs):
        xv, gv = ins[0][...], ins[1][...]
        r = lax.rsqrt(jnp.mean(xv * xv, axis=-1, keepdims=True) + EPS)
        xhat = xv * r
        err = xhat * gv - ins[2][...]
        accs[0][...] += jnp.broadcast_to(0.5 * jnp.sum(jnp.mean(err * err, axis=-1, keepdims=True),
                                                       axis=0, keepdims=True), (1, LANES))
        dyv = err * (1.0 / D_MODEL)
        accs[1][...] += jnp.sum(dyv * xhat, axis=0, keepdims=True)
        dy = dyv * gv
        outs[0][...] = r * (dy - xhat * jnp.mean(dy * xhat, axis=-1, keepdims=True))
    ins = [(x, D_MODEL, 0), (g.reshape(1, D_MODEL), None, None), (target, D_MODEL, 0)]
    return _ew("loss_head", body, ins, [(D_MODEL, F32)], rows, 256, accs=[(1, LANES), (1, D_MODEL)])


def _pad_heads_cols(w, n_heads, dim):
    k = w.shape[0]
    return jnp.pad(w.reshape(k, n_heads, dim), ((0, 0), (0, 0), (0, LANES - dim))).reshape(k, n_heads * LANES)


def _unpad_heads_cols(w, n_heads, dim):
    k = w.shape[0]
    return w.reshape(k, n_heads, LANES)[:, :, :dim].reshape(k, n_heads * dim)


def _layer_weights(w, i):
    segs = jnp.split(w['w_in'][i], list(_cumsum(IN_SIZES))[:-1], axis=1)
    a_q, a_k, a_v, a_gate, b_qd, b_kvd, b_kr, b_gate, m_a, m_b = segs
    kr = jnp.pad(b_kr, ((0, 0), (HEAD_DIM, LANES - MLA_QK)))
    w_in = jnp.concatenate([
        m_a, m_b, _pad_heads_cols(a_q, HEADS, HEAD_DIM), _pad_heads_cols(a_gate, HEADS, HEAD_DIM),
        _pad_heads_cols(b_gate, HEADS, HEAD_DIM), b_qd, _pad_heads_cols(a_k, SWA_KV_HEADS, HEAD_DIM),
        _pad_heads_cols(a_v, SWA_KV_HEADS, HEAD_DIM), b_kvd, kr], axis=1)
    w_uq = _pad_heads_cols(w['w_uq'][i], HEADS, MLA_QK)
    ukv = w['w_ukv'][i].reshape(MLA_KV_LORA, HEADS, 2 * HEAD_DIM)
    pad = ((0, 0), (0, 0), (0, HEAD_DIM))
    w_ukv = jnp.concatenate([jnp.pad(ukv[:, :, :HEAD_DIM], pad).reshape(MLA_KV_LORA, HPAD),
                             jnp.pad(ukv[:, :, HEAD_DIM:], pad).reshape(MLA_KV_LORA, HPAD)], axis=1)
    w_br_a = _pad_heads_cols(w['w_br_a'][i].T, HEADS, HEAD_DIM).T
    w_br_b = _pad_heads_cols(w['w_br_b'][i].T, HEADS, HEAD_DIM).T
    out = dict(w_in=w_in, w_uq=w_uq, w_ukv=w_ukv, w_br_a=w_br_a, w_br_b=w_br_b, w_out=w['w_out'][i],
               w_pg=w['w_ple_gate'][i], w_pp=w['w_ple_proj'][i])
    for name in ('w_in', 'w_uq', 'w_ukv', 'w_br_a', 'w_br_b', 'w_out', 'w_pg'):
        out[name + '_t'] = out[name].T
    return out


def _cumsum(sizes):
    acc, out = 0, []
    for s in sizes:
        acc += s
        out.append(acc)
    return out


def _unpad_grads(g):
    d = g['w_in']
    seg = lambda off, width: d[:, off:off + width]
    b_kr = seg(Z_BKR, LANES)[:, HEAD_DIM:MLA_QK]
    w_in = jnp.concatenate([
        _unpad_heads_cols(seg(Z_AQ, HPAD), HEADS, HEAD_DIM), _unpad_heads_cols(seg(Z_AK, 256), SWA_KV_HEADS, HEAD_DIM),
        _unpad_heads_cols(seg(Z_AV, 256), SWA_KV_HEADS, HEAD_DIM), _unpad_heads_cols(seg(Z_AGATE, HPAD), HEADS, HEAD_DIM),
        seg(Z_BQD, MLA_Q_LORA), seg(Z_BKVD, MLA_KV_LORA), b_kr, _unpad_heads_cols(seg(Z_BGATE, HPAD), HEADS, HEAD_DIM),
        seg(Z_MA, D_MODEL), seg(Z_MB, D_MODEL)], axis=1)
    w_uq = _unpad_heads_cols(g['w_uq'], HEADS, MLA_QK)
    ukv = g['w_ukv'].reshape(MLA_KV_LORA, 2, HEADS, LANES)[:, :, :, :HEAD_DIM]
    w_ukv = jnp.concatenate([ukv[:, 0], ukv[:, 1]], axis=-1).reshape(MLA_KV_LORA, HEADS * 2 * HEAD_DIM)
    w_br_a = _unpad_heads_cols(g['w_br_a'].T, HEADS, HEAD_DIM).T
    w_br_b = _unpad_heads_cols(g['w_br_b'].T, HEADS, HEAD_DIM).T
    return dict(w_in=w_in, w_uq=w_uq, w_ukv=w_ukv, w_br_a=w_br_a, w_br_b=w_br_b, w_out=g['w_out'],
                w_ple_gate=g['w_pg'], w_ple_proj=g['w_pp'], g_mix=g['g_mix'], sink=g['sink'], g_q=g['g_q'],
                g_kv=g['g_kv'], g_ple=g['g_ple'])


def _layer_fwd(x0, p_i, lw, sm, i, pos_col, pos_row, tabs, B, S):
    T = B * S
    h = _rms_fwd("norm_mix", x0, D_MODEL, 0, sm['g_mix'][i], T)
    z = _mm("proj_in", h, lw['w_in'], F32)
    sink_row = jnp.pad(sm['sink'][i], (0, LANES - HEADS)).reshape(1, LANES)
    oa_raw, oa, lse_a = _swa_fwd(z, pos_col, pos_row, sink_row, B, S)
    qdn = _rms_fwd("norm_q", z, MLA_Q_LORA, Z_BQD // MLA_Q_LORA, sm['g_q'][i], T)
    kvdn = _rms_fwd("norm_kv", z, MLA_KV_LORA, Z_BKVD // MLA_KV_LORA, sm['g_kv'][i], T)
    q_pre = _mm("proj_uq", qdn, lw['w_uq'], F32)
    kv_pre = _mm("proj_ukv", kvdn, lw['w_ukv'], F32)
    qf, kf, vf = _mla_prep(q_pre, kv_pre, z, tabs, T)
    ob_raw, ob, lse_b = _mla_fwd(qf, kf, vf, z, B, S)
    ua = _mm("proj_br_a", oa, lw['w_br_a'], F32)
    ub = _mm("proj_br_b", ob, lw['w_br_b'], F32)
    y = _merge_fwd(ua, ub, z, T)
    x1 = _mm("proj_out", y, lw['w_out'], F32, residual=x0)
    hn = _rms_fwd("norm_ple", x1, D_MODEL, 0, sm['g_ple'][i], T)
    u = _mm("proj_pg", hn, lw['w_pg'], F32)
    e = _mm("proj_pp", p_i, lw['w_pp'], F32)
    x2 = _ple_fwd(x1, u, e, T)
    saved = dict(x0=x0, h=h, z=z, sink_row=sink_row, oa_raw=oa_raw, oa=oa, lse_a=lse_a, qdn=qdn, kvdn=kvdn,
                 qf=qf, kf=kf, vf=vf, ob_raw=ob_raw, ob=ob, lse_b=lse_b, ua=ua, ub=ub, y=y, x1=x1, hn=hn,
                 u=u, e=e, p=p_i)
    return x2, saved


def _layer_bwd(dx2, sv, lw, sm, i, pos_col, pos_row, tabs, B, S):
    T = B * S
    z = sv['z']
    g = {}
    d_e, d_u = _ple_bwd(dx2, sv['u'], sv['e'], T)
    g['w_pp'] = _mm_tn("grad_pp", sv['p'], d_e)
    g['w_pg'] = _mm_tn("grad_pg", sv['hn'], d_u)
    dhn = _mm("back_pg", d_u, lw['w_pg_t'], F32)
    dx1, g['g_ple'] = _rms_bwd("norm_ple_bwd", sv['x1'], D_MODEL, 0, sm['g_ple'][i], dhn, T, F32, dres=dx2)
    g['w_out'] = _mm_tn("grad_out", sv['y'], dx1)
    dy = _mm("back_out", dx1, lw['w_out_t'], F32)
    d_ua, d_ub, d_ma, d_mb = _merge_bwd(dy, sv['ua'], sv['ub'], z, T)
    g['w_br_a'] = _mm_tn("grad_br_a", sv['oa'], d_ua)
    g['w_br_b'] = _mm_tn("grad_br_b", sv['ob'], d_ub)
    d_oa = _mm("back_br_a", d_ua, lw['w_br_a_t'], F32)
    d_ob = _mm("back_br_b", d_ub, lw['w_br_b_t'], F32)
    dob_raw, d_bgate, delta_b = _gate_bwd("gate_b_bwd", d_ob, sv['ob_raw'], z, Z_BGATE // HPAD, T)
    dq, dk, dv = _mla_bwd(sv['qf'], sv['kf'], sv['vf'], dob_raw, sv['lse_b'], delta_b, B, S)
    dq_pre, dkv_pre, d_bkr = _mla_prep_bwd(dq, dk, dv, tabs, T)
    g['w_uq'] = _mm_tn("grad_uq", sv['qdn'], dq_pre)
    g['w_ukv'] = _mm_tn("grad_ukv", sv['kvdn'], dkv_pre)
    dqdn = _mm("back_uq", dq_pre, lw['w_uq_t'], F32)
    dkvdn = _mm("back_ukv", dkv_pre, lw['w_ukv_t'], F32)
    d_bqd, g['g_q'] = _rms_bwd("norm_q_bwd", z, MLA_Q_LORA, Z_BQD // MLA_Q_LORA, sm['g_q'][i], dqdn, T, BF16)
    d_bkvd, g['g_kv'] = _rms_bwd("norm_kv_bwd", z, MLA_KV_LORA, Z_BKVD // MLA_KV_LORA, sm['g_kv'][i], dkvdn, T, BF16)
    doa_raw, d_agate, delta_a = _gate_bwd("gate_a_bwd", d_oa, sv['oa_raw'], z, Z_AGATE // HPAD, T)
    d_aq, d_ak, d_av, dsink = _swa_bwd(z, pos_col, pos_row, sv['sink_row'], sv['lse_a'], doa_raw, delta_a, B, S)
    g['sink'] = dsink[0, :HEADS]
    dz = jnp.concatenate([d_ma, d_mb, d_aq, d_agate, d_bgate, d_bqd, d_ak.astype(BF16), d_av.astype(BF16),
                          d_bkvd, d_bkr], axis=1)
    g['w_in'] = _mm_tn("grad_in", sv['h'], dz)
    dh = _mm("back_in", dz, lw['w_in_t'], F32)
    dx0, g['g_mix'] = _rms_bwd("norm_mix_bwd", sv['x0'], D_MODEL, 0, sm['g_mix'][i], dh, T, F32, dres=dx1)
    for name in ('g_ple', 'g_q', 'g_kv', 'g_mix'):
        g[name] = g[name][0]
    return dx0, g


def _local_step(x, p, positions, wfull, sm, loss_target):
    B, S, _ = x.shape
    T = B * S
    pos_col = positions.reshape(T, 1)
    pos_row = positions.reshape(T // BLOCK, 1, BLOCK)
    half = MLA_ROPE // 2
    inv = ROPE_THETA ** (-jnp.arange(0, MLA_ROPE, 2, dtype=F32) / MLA_ROPE)
    inv_lane = jnp.tile(inv, LANES // half).reshape(1, LANES)
    tabs = _rope_tables(pos_col, inv_lane, T)
    xc = x.reshape(T, D_MODEL)
    lws, saved = [], []
    for i in range(DEPTH):
        lw = _layer_weights(wfull, i)
        xc, sv = _layer_fwd(xc, p[i].reshape(T, PLE_DIM), lw, sm, i, pos_col, pos_row, tabs, B, S)
        lws.append(lw)
        saved.append(sv)
    dx, loss, dg_final = _loss_head(xc, sm['g_final'], loss_target.reshape(T, D_MODEL), T)
    layer_grads = [None] * DEPTH
    for i in reversed(range(DEPTH)):
        dx, g = _layer_bwd(dx, saved[i], lws[i], sm, i, pos_col, pos_row, tabs, B, S)
        layer_grads[i] = _unpad_grads(g)
    return loss, dx.reshape(B, S, D_MODEL), layer_grads, dg_final[0]


SMALL_ROWS = 48


def _pack_small(arrs):
    flat = jnp.concatenate([arrs[name].reshape(-1) for name in SMALL])
    return jnp.pad(flat, (0, SMALL_ROWS * LANES - flat.shape[0])).reshape(SMALL_ROWS, LANES)


def _unpack_small(block, shapes):
    flat = block.reshape(-1)
    out, off = {}, 0
    for name in SMALL:
        n = math.prod(shapes[name])
        out[name] = flat[off:off + n].reshape(shapes[name])
        off += n
    return out


def _to_slots(g, axis):
    r, c = g.shape
    if axis == 0:
        return g.reshape(N_CHIPS, r // N_CHIPS, c)
    return g.reshape(r, N_CHIPS, c // N_CHIPS).transpose(1, 0, 2)


def _units(shapes):
    units = []
    for w, shape in enumerate(shapes):
        r = shape[-2]
        n = 4 if r >= 1024 else 1
        units += [(w, k * (r // n), r // n) for k in range(n)]
    return units


def _place():
    x, y, c = lax.axis_index("x"), lax.axis_index("y"), lax.axis_index("c")
    chips = [(1 - x, y), (x, 1 - y), (1 - x, 1 - y)]
    return x, y, c, chips


ANY = pl.BlockSpec(memory_space=pl.ANY)


def _remote(send_sems, recv_sems, k, src, dst, to):
    return pltpu.make_async_remote_copy(src_ref=src, dst_ref=dst, send_sem=send_sems.at[k],
                                        recv_sem=recv_sems.at[k], device_id=to, device_id_type=MESH)


def _gather_weights(shards):
    n = len(shards)
    units = _units([s.shape for s in shards])
    nu = len(units)

    def body(*refs):
        ins, outs = refs[:n], refs[n:2 * n]
        send_sems, recv_sems, local_sems = refs[2 * n:]
        x, y, c, chips = _place()
        me = 2 * x + y
        sibling = (x, y, 1 - c)
        copy = functools.partial(_remote, send_sems, recv_sems)
        keeps, sends = [], []
        for u, (w, r0, nr) in enumerate(units):
            rows = pl.ds(r0, nr)
            keeps.append(pltpu.make_async_copy(ins[w].at[:, rows, :], outs[w].at[me, :, rows, :], local_sems.at[u]))
            keeps[-1].start()
        for j, (cx, cy) in enumerate(chips):
            for u, (w, r0, nr) in enumerate(units):
                rows = pl.ds(r0, nr)
                sends.append(copy(j * nu + u, ins[w].at[c, rows, :], outs[w].at[me, c, rows, :], (cx, cy, c)))
                sends[-1].start()
        for j, (cx, cy) in enumerate(chips):
            for u, (w, r0, nr) in enumerate(units):
                landed = outs[w].at[2 * cx + cy, c, pl.ds(r0, nr), :]
                copy(j * nu + u, landed, landed, (cx, cy, c)).wait_recv()
                sends.append(copy((3 + j) * nu + u, landed, landed, sibling))
                sends[-1].start()
        for j, (cx, cy) in enumerate(chips):
            for u, (w, r0, nr) in enumerate(units):
                other = outs[w].at[2 * cx + cy, 1 - c, pl.ds(r0, nr), :]
                copy((3 + j) * nu + u, other, other, sibling).wait_recv()
        for cp in sends:
            cp.wait_send()
        for keep in keeps:
            keep.wait()

    return pl.pallas_call(
        body, name="gather_weights",
        out_shape=[jax.ShapeDtypeStruct((N_CHIPS,) + s.shape, s.dtype) for s in shards],
        in_specs=[ANY] * n, out_specs=[ANY] * n,
        scratch_shapes=[pltpu.SemaphoreType.DMA((6 * nu,)), pltpu.SemaphoreType.DMA((6 * nu,)),
                        pltpu.SemaphoreType.DMA((nu,))])(*shards)


def _pair_exchange(g0, g1):
    n = len(g0)

    def body(*refs):
        layers, outs = (refs[:n], refs[n:2 * n]), refs[2 * n:3 * n]
        send_sems, recv_sems = refs[3 * n:]
        x, y, c, _ = _place()
        copy = functools.partial(_remote, send_sems, recv_sems)
        for w in range(n):
            for q in range(N_CHIPS):
                for layer in range(DEPTH):
                    cp = copy(N_CHIPS * w + q, layers[layer][w].at[q], outs[w].at[q], (x, y, 1 - c))
                    pl.when(c == 1 - layer)(cp.start)
        for w in range(n):
            for q in range(N_CHIPS):
                copy(N_CHIPS * w + q, layers[0][w].at[q], outs[w].at[q], (x, y, 1 - c)).wait()

    return pl.pallas_call(
        body, name="pair_exchange", out_shape=[jax.ShapeDtypeStruct(g.shape, g.dtype) for g in g0],
        in_specs=[ANY] * (2 * n), out_specs=[ANY] * n,
        scratch_shapes=[pltpu.SemaphoreType.DMA((N_CHIPS * n,)), pltpu.SemaphoreType.DMA((N_CHIPS * n,))])(*g0, *g1)


def _pair_sum(name, g0, g1, theirs, cflag):
    shape = theirs.shape
    rows, width = shape[0] * shape[1], shape[2]

    def body(ins, outs, _):
        mine = jnp.where(ins[3][0:1, 0:1] == 0.0, ins[0][...], ins[1][...])
        tot = mine + ins[2][...]
        outs[0][...] = tot
        outs[1][...] = tot.astype(BF16)
    ins = [(a.reshape(rows, width), width, 0) for a in (g0, g1, theirs)] + [(cflag, None, None)]
    f32, bf16 = _ew(name, body, ins, [(width, F32), (width, BF16)], rows, 256)
    return f32.reshape(shape), bf16.reshape(shape)


def _chip_exchange(parts):
    n = len(parts)

    def body(*refs):
        ins, outs = refs[:n], refs[n:2 * n]
        send_sems, recv_sems = refs[2 * n:]
        x, y, c, chips = _place()
        copy = functools.partial(_remote, send_sems, recv_sems)
        sends = []
        for j, (cx, cy) in enumerate(chips):
            for w in range(n):
                sends.append(copy(j * n + w, ins[w].at[2 * cx + cy], outs[w].at[j], (cx, cy, c)))
                sends[-1].start()
        for j, (cx, cy) in enumerate(chips):
            for w in range(n):
                copy(j * n + w, outs[w].at[j], outs[w].at[j], (cx, cy, c)).wait_recv()
        for cp in sends:
            cp.wait_send()

    return pl.pallas_call(
        body, name="chip_exchange",
        out_shape=[jax.ShapeDtypeStruct((3,) + a.shape[1:], a.dtype) for a in parts],
        in_specs=[ANY] * n, out_specs=[ANY] * n,
        scratch_shapes=[pltpu.SemaphoreType.DMA((3 * n,)), pltpu.SemaphoreType.DMA((3 * n,))])(*parts)


def _chip_sum(name, part, landed, chipflag):
    _, r, width = part.shape
    tm = min(r, 256)

    def kern(p_ref, l_ref, flag_ref, o_ref):
        me = flag_ref[0:1, 0:1]
        own = jnp.where(me == 0.0, p_ref[0], jnp.where(me == 1.0, p_ref[1], jnp.where(me == 2.0, p_ref[2], p_ref[3])))
        o_ref[...] = ((own + l_ref[0].astype(F32)) + l_ref[1].astype(F32)) + l_ref[2].astype(F32)

    return pl.pallas_call(
        kern, name=name, grid=(r // tm,),
        in_specs=[pl.BlockSpec((N_CHIPS, tm, width), lambda i: (0, i, 0)),
                  pl.BlockSpec((3, tm, width), lambda i: (0, i, 0)),
                  pl.BlockSpec((1, LANES), lambda i: (0, 0))],
        out_specs=pl.BlockSpec((tm, width), lambda i: (i, 0)),
        out_shape=jax.ShapeDtypeStruct((r, width), F32), compiler_params=_params(("arbitrary",)))(part, landed, chipflag)


def _pair_broadcast(mine):
    n = len(mine)
    units = _units([a.shape for a in mine])

    def body(*refs):
        ins, outs = refs[:n], refs[n:2 * n]
        send_sems, recv_sems = refs[2 * n:]
        x, y, c, _ = _place()
        copy = functools.partial(_remote, send_sems, recv_sems)
        cps = [copy(u, ins[w].at[pl.ds(r0, nr), :], outs[w].at[pl.ds(r0, nr), :], (x, y, 1 - c))
               for u, (w, r0, nr) in enumerate(units)]
        for cp in cps:
            cp.start()
        for cp in cps:
            cp.wait()

    return pl.pallas_call(
        body, name="pair_broadcast", out_shape=[jax.ShapeDtypeStruct(a.shape, a.dtype) for a in mine],
        in_specs=[ANY] * n, out_specs=[ANY] * n,
        scratch_shapes=[pltpu.SemaphoreType.DMA((len(units),)), pltpu.SemaphoreType.DMA((len(units),))])(*mine)


def _small_allreduce(v):
    offsets = [(dx, dy, dc) for dx in (0, 1) for dy in (0, 1) for dc in (0, 1)][1:]

    def body(v_ref, out_ref, recv_ref, send_sems, recv_sems):
        x, y, c, _ = _place()
        flip = lambda a, d: 1 - a if d else a
        peers = [(flip(x, dx), flip(y, dy), flip(c, dc)) for dx, dy, dc in offsets]
        copy = functools.partial(_remote, send_sems, recv_sems)
        me = 4 * x + 2 * y + c
        recv_ref[me] = v_ref[...]
        cps = [copy(k, v_ref, recv_ref.at[me], peer) for k, peer in enumerate(peers)]
        for cp in cps:
            cp.start()
        for k, (px, py, pc) in enumerate(peers):
            landed = recv_ref.at[4 * px + 2 * py + pc]
            copy(k, landed, landed, (px, py, pc)).wait_recv()
        for cp in cps:
            cp.wait_send()
        tot = recv_ref[0]
        for d in range(1, 8):
            tot = tot + recv_ref[d]
        out_ref[...] = tot

    vmem = pl.BlockSpec(memory_space=pltpu.VMEM)
    return pl.pallas_call(
        body, name="small_allreduce", out_shape=jax.ShapeDtypeStruct(v.shape, v.dtype),
        in_specs=[vmem], out_specs=vmem,
        scratch_shapes=[pltpu.VMEM((8,) + v.shape, v.dtype), pltpu.SemaphoreType.DMA((7,)),
                        pltpu.SemaphoreType.DMA((7,))])(v)


def _adam_math(gv, wv, mv, vv):
    mv = ADAM_B1 * mv + (1.0 - ADAM_B1) * gv
    vv = ADAM_B2 * vv + (1.0 - ADAM_B2) * (gv * gv)
    m_hat = mv / (1.0 - ADAM_B1 ** ADAM_STEP)
    v_hat = vv / (1.0 - ADAM_B2 ** ADAM_STEP)
    return -ADAM_LR * (m_hat / (jnp.sqrt(v_hat) + ADAM_EPS) + ADAM_WD * wv), mv, vv


def _adamw_big(name, mine, theirs, cflag, w, m, v):
    _, r, width = w.shape
    tm = min(r, 256)

    def kern(mine_ref, theirs_ref, flag_ref, w_ref, m_ref, v_ref, g_ref, d_ref, nm_ref, nv_ref):
        layer = pl.program_id(0).astype(F32)
        gv = jnp.where(flag_ref[0:1, 0:1] == layer, mine_ref[...], theirs_ref[...])
        g_ref[0] = gv
        d_ref[0], nm_ref[0], nv_ref[0] = _adam_math(gv, w_ref[0], m_ref[0], v_ref[0])

    flat = pl.BlockSpec((tm, width), lambda l, i: (i, 0))
    stacked = pl.BlockSpec((1, tm, width), lambda l, i: (l, i, 0))
    return pl.pallas_call(
        kern, name=name, grid=(DEPTH, r // tm),
        in_specs=[flat, flat, pl.BlockSpec((1, LANES), lambda l, i: (0, 0)), stacked, stacked, stacked],
        out_specs=[stacked] * 4, out_shape=[jax.ShapeDtypeStruct(w.shape, F32)] * 4,
        compiler_params=_params(("arbitrary", "arbitrary")))(mine, theirs, cflag, w, m, v)


def _adamw_small(g, w, m, v):
    def body(ins, outs, _):
        outs[0][...], outs[1][...], outs[2][...] = _adam_math(*(r[...] for r in ins))
    return _ew("adamw_small", body, [(a, LANES, 0) for a in (g, w, m, v)], [(LANES, F32)] * 3, SMALL_ROWS, SMALL_ROWS)


def kernel(x, p, positions, g_mix, w_in, sink, g_q, w_uq, g_kv, w_ukv, w_br_a, w_br_b, w_out, g_ple, w_ple_gate, w_ple_proj, g_final, loss_target, m_g_mix, m_w_in, m_sink, m_g_q, m_w_uq, m_g_kv, m_w_ukv, m_w_br_a, m_w_br_b, m_w_out, m_g_ple, m_w_ple_gate, m_w_ple_proj, m_g_final, v_g_mix, v_w_in, v_sink, v_g_q, v_w_uq, v_g_kv, v_w_ukv, v_w_br_a, v_w_br_b, v_w_out, v_g_ple, v_w_ple_gate, v_w_ple_proj, v_g_final):
    w = dict(g_mix=g_mix, w_in=w_in, sink=sink, g_q=g_q, w_uq=w_uq, g_kv=g_kv, w_ukv=w_ukv, w_br_a=w_br_a,
             w_br_b=w_br_b, w_out=w_out, g_ple=g_ple, w_ple_gate=w_ple_gate, w_ple_proj=w_ple_proj, g_final=g_final)
    m = dict(g_mix=m_g_mix, w_in=m_w_in, sink=m_sink, g_q=m_g_q, w_uq=m_w_uq, g_kv=m_g_kv, w_ukv=m_w_ukv,
             w_br_a=m_w_br_a, w_br_b=m_w_br_b, w_out=m_w_out, g_ple=m_g_ple, w_ple_gate=m_w_ple_gate,
             w_ple_proj=m_w_ple_proj, g_final=m_g_final)
    v = dict(g_mix=v_g_mix, w_in=v_w_in, sink=v_sink, g_q=v_g_q, w_uq=v_w_uq, g_kv=v_g_kv, w_ukv=v_w_ukv,
             w_br_a=v_w_br_a, w_br_b=v_w_br_b, w_out=v_w_out, g_ple=v_g_ple, w_ple_gate=v_w_ple_gate,
             w_ple_proj=v_w_ple_proj, g_final=v_g_final)
    wfull = _gather_full(w)
    sm = {name: w[name] for name in SMALL}
    loss_row, grad_x, layer_grads, dg_final = _local_step(x, p, positions, wfull, sm, loss_target)
    loss = lax.psum(loss_row[0, 0], ("x", "y", "c"))
    res = _update(layer_grads, dg_final, w, m, v)
    return (loss, grad_x, *[res[name][kind] for kind in range(4) for name in WEIGHT_NAMES])


def _gather_full(w):
    gathered = _gather_weights([w[name].astype(BF16) for name, _ in SHARDED])
    return {name: [jnp.concatenate([gathered[k][q, layer] for q in range(N_CHIPS)], axis=axis - 1)
                   for layer in range(DEPTH)] for k, (name, axis) in enumerate(SHARDED)}


def _update(layer_grads, dg_final, w, m, v):
    small_shapes = {name: w[name].shape for name in SMALL}
    cflag = jnp.full((1, LANES), lax.axis_index("c"), F32)
    chipflag = jnp.full((1, LANES), 2 * lax.axis_index("x") + lax.axis_index("y"), F32)

    slots = [[_to_slots(layer_grads[layer][name], axis - 1) for name, axis in SHARDED] for layer in range(DEPTH)]
    theirs = _pair_exchange(slots[0], slots[1])
    pair = [_pair_sum("pair_sum_" + name, slots[0][k], slots[1][k], theirs[k], cflag)
            for k, (name, _) in enumerate(SHARDED)]
    landed = _chip_exchange([bf16 for _, bf16 in pair])
    mine = [_chip_sum("chip_sum_" + name, pair[k][0], landed[k], chipflag) for k, (name, _) in enumerate(SHARDED)]
    other = _pair_broadcast(mine)
    res = {name: _adamw_big("adamw_" + name, mine[k], other[k], cflag, w[name], m[name], v[name])
           for k, (name, _) in enumerate(SHARDED)}

    gsmall = {name: jnp.stack([layer_grads[layer][name] for layer in range(DEPTH)]) for name in SMALL[:-1]}
    gsmall['g_final'] = dg_final
    gsum = _small_allreduce(_pack_small(gsmall))
    small = (gsum,) + tuple(_adamw_small(gsum, _pack_small(w), _pack_small(m), _pack_small(v)))
    for name, arrs in zip(SMALL, zip(*[[_unpack_small(a, small_shapes)[n] for n in SMALL] for a in small])):
        res[name] = arrs
    return res
```

```python
import functools
import math

import jax
import jax.numpy as jnp
from jax import lax
from jax.experimental import pallas as pl
from jax.experimental.pallas import tpu as pltpu

F32 = jnp.float32
BF16 = jnp.bfloat16

D_MODEL = 1024
DEPTH = 2
PLE_DIM = 256
BLOCK = 128
EPS = 1e-6
NEG = -1e30
HEADS = 8
SWA_KV_HEADS = 2
HEAD_DIM = 64
LANES = 128
HPAD = HEADS * LANES
MLA_QK = 96
MLA_ROPE = 32
MLA_Q_LORA = 256
MLA_KV_LORA = 128
ROPE_THETA = 10000.0
IN_SIZES = (512, 128, 128, 512, 256, 128, 32, 512, 1024, 1024)

Z_MA, Z_MB, Z_AQ, Z_AGATE, Z_BGATE = 0, 1024, 2048, 3072, 4096
Z_BQD, Z_AK, Z_AV, Z_BKVD, Z_BKR = 5120, 5376, 5632, 5888, 6016
Z_WIDTH = 6144

ADAM_LR, ADAM_B1, ADAM_B2, ADAM_EPS, ADAM_WD, ADAM_STEP = 0.001, 0.9, 0.999, 1e-08, 0.01, 10

VMEM_LIMIT = 56 * 1024 * 1024
MESH = pl.DeviceIdType.MESH

WEIGHT_NAMES = ('g_mix', 'w_in', 'sink', 'g_q', 'w_uq', 'g_kv', 'w_ukv', 'w_br_a', 'w_br_b',
                'w_out', 'g_ple', 'w_ple_gate', 'w_ple_proj', 'g_final')
SHARDED = (('w_in', 2), ('w_uq', 2), ('w_ukv', 2), ('w_br_a', 2), ('w_br_b', 2),
           ('w_out', 1), ('w_ple_gate', 1), ('w_ple_proj', 2))
SMALL = ('g_mix', 'sink', 'g_q', 'g_kv', 'g_ple', 'g_final')
N_CHIPS = 4


def _params(sem):
    return pltpu.CompilerParams(dimension_semantics=sem, vmem_limit_bytes=VMEM_LIMIT)


def _ew(name, body, ins, outs, rows, tm, accs=()):
    n_in, n_out = len(ins), len(outs)
    in_specs, args = [], []
    for arr, width, cb in ins:
        if width is None:
            in_specs.append(pl.BlockSpec(arr.shape, lambda i, nd=arr.ndim: (0,) * nd))
        else:
            in_specs.append(pl.BlockSpec((tm, width), lambda i, cb=cb: (i, cb)))
        args.append(arr)
    out_shape = [jax.ShapeDtypeStruct((rows, w), dt) for w, dt in outs]
    out_shape += [jax.ShapeDtypeStruct(s, F32) for s in accs]
    out_specs = [pl.BlockSpec((tm, w), lambda i: (i, 0)) for w, _ in outs]
    out_specs += [pl.BlockSpec(s, lambda i: (0, 0)) for s in accs]

    def kern(*refs):
        acc_refs = refs[n_in + n_out:]
        if acc_refs:
            @pl.when(pl.program_id(0) == 0)
            def _():
                for r in acc_refs:
                    r[...] = jnp.zeros_like(r)
        body(refs[:n_in], refs[n_in:n_in + n_out], acc_refs)

    res = pl.pallas_call(kern, name=name, grid=(rows // tm,), in_specs=in_specs, out_specs=out_specs,
                         out_shape=out_shape, compiler_params=_params(("arbitrary",)))(*args)
    return res


def _rms_fwd(name, x, width, cb, g, rows):
    def body(ins, outs, _):
        xv = ins[0][...].astype(F32)
        r = lax.rsqrt(jnp.mean(xv * xv, axis=-1, keepdims=True) + EPS)
        outs[0][...] = ((xv * r) * ins[1][...]).astype(BF16)
    return _ew(name, body, [(x, width, cb), (g.reshape(1, width), None, None)], [(width, BF16)], rows, 256)[0]


def _rms_bwd(name, x, width, cb, g, dh, rows, out_dtype, dres=None):
    def body(ins, outs, accs):
        xv, gv, dhv = ins[0][...].astype(F32), ins[1][...], ins[2][...].astype(F32)
        r = lax.rsqrt(jnp.mean(xv * xv, axis=-1, keepdims=True) + EPS)
        xhat = xv * r
        accs[0][...] += jnp.sum(dhv * xhat, axis=0, keepdims=True)
        dy = dhv * gv
        dx = r * (dy - xhat * jnp.mean(dy * xhat, axis=-1, keepdims=True))
        if dres is not None:
            dx = dx + ins[3][...]
        outs[0][...] = dx.astype(out_dtype)
    ins = [(x, width, cb), (g.reshape(1, width), None, None), (dh, width, 0)]
    if dres is not None:
        ins.append((dres, width, 0))
    return _ew(name, body, ins, [(width, out_dtype)], rows, 256, accs=[(1, width)])


def _mm(name, a, b, out_dtype, residual=None, tm=256, tn=512):
    M, K = a.shape
    N = b.shape[1]
    tm, tn = min(tm, M), min(tn, N)
    has_res = residual is not None

    def kern(*refs):
        a_ref, b_ref, o_ref = refs[0], refs[1], refs[-1]
        av = a_ref[...].astype(BF16)
        for j in range(N // tn):
            cols = slice(j * tn, (j + 1) * tn)
            part = jnp.dot(av, b_ref[:, cols], preferred_element_type=F32)
            if has_res:
                part = part + refs[2][:, cols]
            o_ref[:, cols] = part.astype(o_ref.dtype)

    in_specs = [pl.BlockSpec((tm, K), lambda i: (i, 0)), pl.BlockSpec((K, N), lambda i: (0, 0))]
    args = [a, b]
    if has_res:
        in_specs.append(pl.BlockSpec((tm, N), lambda i: (i, 0)))
        args.append(residual)
    return pl.pallas_call(
        kern, name=name, grid=(M // tm,), in_specs=in_specs, out_specs=pl.BlockSpec((tm, N), lambda i: (i, 0)),
        out_shape=jax.ShapeDtypeStruct((M, N), out_dtype), compiler_params=_params(("parallel",)))(*args)


def _mm_tn(name, a, b, tk=512, tn=2048):
    T, M = a.shape
    N = b.shape[1]
    tn, tk = min(tn, N), min(tk, T)

    def kern(a_ref, b_ref, o_ref):
        k = pl.program_id(1)
        part = _dot_tn(a_ref[...].astype(BF16), b_ref[...].astype(BF16))

        @pl.when(k == 0)
        def _():
            o_ref[...] = part

        @pl.when(k > 0)
        def _():
            o_ref[...] += part

    return pl.pallas_call(
        kern, name=name, grid=(N // tn, T // tk),
        in_specs=[pl.BlockSpec((tk, M), lambda j, k: (k, 0)), pl.BlockSpec((tk, tn), lambda j, k: (k, j))],
        out_specs=pl.BlockSpec((M, tn), lambda j, k: (0, j)),
        out_shape=jax.ShapeDtypeStruct((M, N), F32),
        compiler_params=_params(("parallel", "arbitrary")))(a, b)


def _dot_nt(a, b):
    return lax.dot_general(a, b, (((1,), (1,)), ((), ())), preferred_element_type=F32)


def _dot_tn(a, b):
    return lax.dot_general(a, b, (((0,), (0,)), ((), ())), preferred_element_type=F32)


SWA_SCALE = HEAD_DIM ** -0.5


def _swa_band(n, pq_ref, pkp_ref, pkc_ref):
    posk = jnp.concatenate([pkp_ref[0], pkc_ref[0]], axis=1)
    dist = (pq_ref[...] - posk).astype(F32)
    qi = lax.broadcasted_iota(jnp.int32, (BLOCK, 2 * BLOCK), 0)
    kj = lax.broadcasted_iota(jnp.int32, (BLOCK, 2 * BLOCK), 1)
    t_abs = n * BLOCK + qi
    s_abs = n * BLOCK - BLOCK + kj
    return dist, (s_abs >= 0) & (s_abs <= t_abs) & (t_abs - s_abs < BLOCK)


def _swa_scores(q_all, kb, dist, valid, h):
    kvh = h // (HEADS // SWA_KV_HEADS)
    qh = (q_all[:, h * LANES:(h + 1) * LANES] * SWA_SCALE).astype(BF16)
    kh = kb[:, kvh * LANES:(kvh + 1) * LANES]
    s = _dot_nt(qh, kh) - (2.0 ** -(h + 1)) * dist
    return jnp.where(valid, s, NEG), qh, kh


def _swa_specs(nb):
    prev = lambda b, n: b * nb + jnp.maximum(n - 1, 0)
    own = lambda b, n: b * nb + n
    return [
        pl.BlockSpec((BLOCK, HPAD), lambda b, n: (own(b, n), Z_AQ // HPAD)),
        pl.BlockSpec((BLOCK, 256), lambda b, n: (prev(b, n), Z_AK // 256)),
        pl.BlockSpec((BLOCK, 256), lambda b, n: (own(b, n), Z_AK // 256)),
        pl.BlockSpec((BLOCK, 256), lambda b, n: (prev(b, n), Z_AV // 256)),
        pl.BlockSpec((BLOCK, 256), lambda b, n: (own(b, n), Z_AV // 256)),
        pl.BlockSpec((BLOCK, 1), lambda b, n: (own(b, n), 0)),
        pl.BlockSpec((1, 1, BLOCK), lambda b, n: (prev(b, n), 0, 0)),
        pl.BlockSpec((1, 1, BLOCK), lambda b, n: (own(b, n), 0, 0)),
    ]


def _swa_fwd(z, pos_col, pos_row, sink_row, B, S):
    nb = S // BLOCK
    T = B * S

    def kern(q_ref, kp_ref, kc_ref, vp_ref, vc_ref, pq_ref, pkp_ref, pkc_ref, gate_ref, sink_ref,
             oraw_ref, og_ref, lse_ref):
        q_all = q_ref[...]
        kb = jnp.concatenate([kp_ref[...], kc_ref[...]], axis=0).astype(BF16)
        vb = jnp.concatenate([vp_ref[...], vc_ref[...]], axis=0).astype(BF16)
        dist, valid = _swa_band(pl.program_id(1), pq_ref, pkp_ref, pkc_ref)
        lane = lax.broadcasted_iota(jnp.int32, (BLOCK, LANES), 1)
        lse_all = jnp.zeros((BLOCK, LANES), F32)
        for h in range(HEADS):
            kvh = h // (HEADS // SWA_KV_HEADS)
            s, _, _ = _swa_scores(q_all, kb, dist, valid, h)
            sink_h = sink_ref[0:1, h:h + 1]
            m = jnp.maximum(jnp.max(s, axis=-1, keepdims=True), sink_h)
            e = jnp.exp(s - m)
            denom = jnp.sum(e, axis=-1, keepdims=True) + jnp.exp(sink_h - m)
            probs = e * (1.0 / denom)
            o = jnp.dot(probs.astype(BF16), vb[:, kvh * LANES:(kvh + 1) * LANES], preferred_element_type=F32)
            cols = slice(h * LANES, (h + 1) * LANES)
            oraw_ref[:, cols] = o
            g = gate_ref[:, cols].astype(F32)
            og_ref[:, cols] = (o * (g * jax.nn.sigmoid(g))).astype(BF16)
            lse_all = jnp.where(lane == h, m + jnp.log(denom), lse_all)
        lse_ref[...] = lse_all

    own = lambda b, n: b * nb + n
    in_specs = _swa_specs(nb) + [
        pl.BlockSpec((BLOCK, HPAD), lambda b, n: (own(b, n), Z_AGATE // HPAD)),
        pl.BlockSpec((1, LANES), lambda b, n: (0, 0)),
    ]
    out_specs = [pl.BlockSpec((BLOCK, HPAD), lambda b, n: (own(b, n), 0)),
                 pl.BlockSpec((BLOCK, HPAD), lambda b, n: (own(b, n), 0)),
                 pl.BlockSpec((BLOCK, LANES), lambda b, n: (own(b, n), 0))]
    out_shape = [jax.ShapeDtypeStruct((T, HPAD), F32), jax.ShapeDtypeStruct((T, HPAD), BF16),
                 jax.ShapeDtypeStruct((T, LANES), F32)]
    return pl.pallas_call(kern, name="swa_fwd", grid=(B, nb), in_specs=in_specs, out_specs=out_specs,
                          out_shape=out_shape, compiler_params=_params(("parallel", "arbitrary")))(
        z, z, z, z, z, pos_col, pos_row, pos_row, z, sink_row)


def _swa_bwd(z, pos_col, pos_row, sink_row, lse, do_raw, delta, B, S):
    nb = S // BLOCK
    T = B * S

    def kern(q_ref, kp_ref, kc_ref, vp_ref, vc_ref, pq_ref, pkp_ref, pkc_ref, sink_ref, lse_ref, do_ref,
             delta_ref, dq_ref, dk_ref, dv_ref, dsink_ref):
        b, n = pl.program_id(0), pl.program_id(1)

        @pl.when(n == 0)
        def _():
            dk_ref[...] = jnp.zeros_like(dk_ref)
            dv_ref[...] = jnp.zeros_like(dv_ref)

        @pl.when((b == 0) & (n == 0))
        def _():
            dsink_ref[...] = jnp.zeros_like(dsink_ref)

        q_all = q_ref[...]
        kb = jnp.concatenate([kp_ref[...], kc_ref[...]], axis=0).astype(BF16)
        vb = jnp.concatenate([vp_ref[...], vc_ref[...]], axis=0).astype(BF16)
        dist, valid = _swa_band(n, pq_ref, pkp_ref, pkc_ref)
        lane1 = lax.broadcasted_iota(jnp.int32, (1, LANES), 1)
        dsink = jnp.zeros((1, LANES), F32)
        dk_band = [jnp.zeros((2 * BLOCK, LANES), F32) for _ in range(SWA_KV_HEADS)]
        dv_band = [jnp.zeros((2 * BLOCK, LANES), F32) for _ in range(SWA_KV_HEADS)]
        for h in range(HEADS):
            kvh = h // (HEADS // SWA_KV_HEADS)
            cols = slice(h * LANES, (h + 1) * LANES)
            s, qh, kh = _swa_scores(q_all, kb, dist, valid, h)
            lse_h = lse_ref[:, h:h + 1]
            p = jnp.exp(s - lse_h)
            do = do_ref[:, cols]
            delta_h = delta_ref[:, h * LANES:h * LANES + 1]
            dp = _dot_nt(do, vb[:, kvh * LANES:(kvh + 1) * LANES])
            ds = (p * (dp - delta_h)).astype(BF16)
            dq_ref[:, cols] = (jnp.dot(ds, kh, preferred_element_type=F32) * SWA_SCALE).astype(BF16)
            dk_band[kvh] = dk_band[kvh] + _dot_tn(ds, qh)
            dv_band[kvh] = dv_band[kvh] + _dot_tn(p.astype(BF16), do)
            psink = jnp.exp(sink_ref[0:1, h:h + 1] - lse_h)
            dsink = dsink + jnp.where(lane1 == h, -jnp.sum(psink * delta_h, axis=0, keepdims=True), 0.0)
        dsink_ref[...] += dsink
        dkb = jnp.concatenate(dk_band, axis=1)
        dvb = jnp.concatenate(dv_band, axis=1)
        r_prev = pl.ds(pl.multiple_of(jnp.maximum(n - 1, 0) * BLOCK, BLOCK), BLOCK)
        r_own = pl.ds(pl.multiple_of(n * BLOCK, BLOCK), BLOCK)
        dk_ref[r_prev, :] += dkb[:BLOCK]
        dk_ref[r_own, :] += dkb[BLOCK:]
        dv_ref[r_prev, :] += dvb[:BLOCK]
        dv_ref[r_own, :] += dvb[BLOCK:]

    own = lambda b, n: b * nb + n
    in_specs = _swa_specs(nb) + [
        pl.BlockSpec((1, LANES), lambda b, n: (0, 0)),
        pl.BlockSpec((BLOCK, LANES), lambda b, n: (own(b, n), 0)),
        pl.BlockSpec((BLOCK, HPAD), lambda b, n: (own(b, n), 0)),
        pl.BlockSpec((BLOCK, HPAD), lambda b, n: (own(b, n), 0)),
    ]
    out_specs = [pl.BlockSpec((BLOCK, HPAD), lambda b, n: (own(b, n), 0)),
                 pl.BlockSpec((S, 256), lambda b, n: (b, 0)),
                 pl.BlockSpec((S, 256), lambda b, n: (b, 0)),
                 pl.BlockSpec((1, LANES), lambda b, n: (0, 0))]
    out_shape = [jax.ShapeDtypeStruct((T, HPAD), BF16), jax.ShapeDtypeStruct((T, 256), F32),
                 jax.ShapeDtypeStruct((T, 256), F32), jax.ShapeDtypeStruct((1, LANES), F32)]
    return pl.pallas_call(kern, name="swa_bwd", grid=(B, nb), in_specs=in_specs, out_specs=out_specs,
                          out_shape=out_shape, compiler_params=_params(("arbitrary", "arbitrary")))(
        z, z, z, z, z, pos_col, pos_row, pos_row, sink_row, lse, do_raw, delta)


MLA_T = 256
MLA_SCALE = MLA_QK ** -0.5
LOG2E = 1.4426950408889634
MLA_QSCALE = MLA_SCALE * LOG2E


def _causal(s):
    row = lax.broadcasted_iota(jnp.int32, s.shape, 0)
    col = lax.broadcasted_iota(jnp.int32, s.shape, 1)
    return jnp.where(col <= row, s, NEG)


def _mla_fwd(q, k, v, z, B, S):
    T = B * S
    nq = S // MLA_T

    def kern(q_ref, k_ref, v_ref, gate_ref, oraw_ref, og_ref, lse_ref):
        i = pl.program_id(2)
        qv = q_ref[...]

        def scores(j):
            return _dot_nt(qv, k_ref[pl.ds(pl.multiple_of(j * MLA_T, MLA_T), MLA_T), :])

        def update(j, s, m, l, acc):
            m_new = jnp.maximum(m, jnp.max(s, axis=-1, keepdims=True))
            alpha = jnp.exp2(m - m_new)
            p = jnp.exp2(s - m_new)
            l = alpha * l + jnp.sum(p, axis=-1, keepdims=True)
            vv = v_ref[pl.ds(pl.multiple_of(j * MLA_T, MLA_T), MLA_T), :]
            return m_new, l, alpha * acc + jnp.dot(p.astype(BF16), vv, preferred_element_type=F32)

        def body(j, carry):
            m, l, acc, s = carry
            s_next = scores(j + 1)
            return update(j, s, m, l, acc) + (s_next,)

        init = (jnp.full((MLA_T, 1), NEG, F32), jnp.zeros((MLA_T, 1), F32), jnp.zeros((MLA_T, LANES), F32),
                scores(0))
        m, l, acc, s = lax.fori_loop(0, i, body, init)
        m, l, acc = update(i, _causal(s), m, l, acc)
        o = acc * (1.0 / l)
        oraw_ref[...] = o
        g = gate_ref[...].astype(F32)
        og_ref[...] = (o * (g * jax.nn.sigmoid(g))).astype(BF16)
        lse_ref[...] = jnp.broadcast_to(m + jnp.log2(l), (MLA_T, LANES))

    blk = lambda b, h, i: (b * nq + i, h)
    in_specs = [pl.BlockSpec((MLA_T, LANES), blk),
                pl.BlockSpec((S, LANES), lambda b, h, i: (b, h)),
                pl.BlockSpec((S, LANES), lambda b, h, i: (b, h)),
                pl.BlockSpec((MLA_T, LANES), lambda b, h, i: (b * nq + i, Z_BGATE // LANES + h))]
    out_specs = [pl.BlockSpec((MLA_T, LANES), blk)] * 3
    out_shape = [jax.ShapeDtypeStruct((T, HPAD), F32), jax.ShapeDtypeStruct((T, HPAD), BF16),
                 jax.ShapeDtypeStruct((T, HPAD), F32)]
    return pl.pallas_call(kern, name="mla_fwd", grid=(B, HEADS, nq), in_specs=in_specs, out_specs=out_specs,
                          out_shape=out_shape,
                          compiler_params=_params(("parallel", "parallel", "arbitrary")))(q, k, v, z)


def _mla_bwd(q, k, v, do_raw, lse, delta, B, S):
    T = B * S
    nk = S // MLA_T

    def kern(q_ref, k_ref, v_ref, do_ref, lse_ref, delta_ref, dq_ref, dk_ref, dv_ref, dk_acc, dv_acc):
        j = pl.program_id(2)

        @pl.when(j == 0)
        def _():
            dq_ref[...] = jnp.zeros_like(dq_ref)

        dk_acc[...] = jnp.zeros_like(dk_acc)
        dv_acc[...] = jnp.zeros_like(dv_acc)
        kv, vv = k_ref[...], v_ref[...]

        def step(i, masked):
            rows = pl.ds(pl.multiple_of(i * MLA_T, MLA_T), MLA_T)
            qv, do = q_ref[rows, :], do_ref[rows, :]
            s = _dot_nt(qv, kv)
            if masked:
                s = _causal(s)
            p = jnp.exp2(s - lse_ref[rows, 0:1])
            dp = _dot_nt(do, vv)
            ds = (p * (dp - delta_ref[rows, 0:1])).astype(BF16)
            dv_acc[...] += _dot_tn(p.astype(BF16), do)
            dk_acc[...] += _dot_tn(ds, qv)
            dq_ref[rows, :] += jnp.dot(ds, kv, preferred_element_type=F32)

        step(j, True)

        def body(i, c):
            step(i, False)
            return c

        lax.fori_loop(j + 1, nk, body, 0)
        dk_ref[...] = dk_acc[...] * (1.0 / LOG2E)
        dv_ref[...] = dv_acc[...]

    whole = lambda b, h, j: (b, h)
    tile = lambda b, h, j: (b * nk + j, h)
    in_specs = [pl.BlockSpec((S, LANES), whole), pl.BlockSpec((MLA_T, LANES), tile),
                pl.BlockSpec((MLA_T, LANES), tile), pl.BlockSpec((S, LANES), whole),
                pl.BlockSpec((S, LANES), whole), pl.BlockSpec((S, LANES), whole)]
    out_specs = [pl.BlockSpec((S, LANES), whole), pl.BlockSpec((MLA_T, LANES), tile),
                 pl.BlockSpec((MLA_T, LANES), tile)]
    out_shape = [jax.ShapeDtypeStruct((T, HPAD), F32)] * 3
    return pl.pallas_call(kern, name="mla_bwd", grid=(B, HEADS, nk), in_specs=in_specs, out_specs=out_specs,
                          out_shape=out_shape,
                          scratch_shapes=[pltpu.VMEM((MLA_T, LANES), F32), pltpu.VMEM((MLA_T, LANES), F32)],
                          compiler_params=_params(("parallel", "parallel", "arbitrary")))(
        q, k, v, do_raw, lse, delta)


def _rope_tables(pos_col, inv_lane, rows):
    def body(ins, outs, _):
        ang = ins[0][...].astype(F32) * ins[1][...]
        lane = lax.broadcasted_iota(jnp.int32, ang.shape, 1)
        cos, sin = jnp.cos(ang), jnp.sin(ang)
        first = (lane >= HEAD_DIM) & (lane < HEAD_DIM + MLA_ROPE // 2)
        second = (lane >= HEAD_DIM + MLA_ROPE // 2) & (lane < MLA_QK)
        outs[0][...] = jnp.where(lane < HEAD_DIM, 1.0, jnp.where(lane < MLA_QK, cos, 0.0))
        outs[1][...] = jnp.where(first, -sin, 0.0)
        outs[2][...] = jnp.where(second, sin, 0.0)
    return _ew("rope_tables", body, [(pos_col, 1, 0), (inv_lane, None, None)], [(LANES, F32)] * 3, rows, 256)


def _rope(x, c, s1, s2):
    return x * c + pltpu.roll(x, 112, 1) * s1 + pltpu.roll(x, 16, 1) * s2


def _rope_t(d, c, s1, s2):
    return d * c + pltpu.roll(d * s1, 16, 1) + pltpu.roll(d * s2, 112, 1)


def _mla_prep(q_pre, kv_pre, z, tabs, rows):
    def body(ins, outs, _):
        c, s1, s2 = ins[3][...], ins[4][...], ins[5][...]
        kr = _rope(ins[2][...].astype(F32), c, s1, s2)
        for h in range(HEADS):
            cols = slice(h * LANES, (h + 1) * LANES)
            outs[0][:, cols] = (_rope(ins[0][:, cols], c, s1, s2) * MLA_QSCALE).astype(BF16)
            outs[1][:, cols] = (ins[1][:, cols] + kr).astype(BF16)
        outs[2][...] = ins[6][...].astype(BF16)
    ins = [(q_pre, HPAD, 0), (kv_pre, HPAD, 0), (z, LANES, Z_BKR // LANES),
           (tabs[0], LANES, 0), (tabs[1], LANES, 0), (tabs[2], LANES, 0), (kv_pre, HPAD, 1)]
    return _ew("mla_prep", body, ins, [(HPAD, BF16)] * 3, rows, 256)


def _mla_prep_bwd(dq, dk, dv, tabs, rows):
    def body(ins, outs, _):
        c, s1, s2 = ins[3][...], ins[4][...], ins[5][...]
        lane = lax.broadcasted_iota(jnp.int32, c.shape, 1)
        dkr = jnp.zeros(c.shape, F32)
        for h in range(HEADS):
            cols = slice(h * LANES, (h + 1) * LANES)
            outs[0][:, cols] = _rope_t(ins[0][:, cols] * MLA_SCALE, c, s1, s2).astype(BF16)
            dkh = ins[1][:, cols]
            outs[1][:, cols] = jnp.where(lane < HEAD_DIM, dkh, 0.0).astype(BF16)
            dkr = dkr + dkh
        outs[1][:, HPAD:] = ins[2][...].astype(BF16)
        live = (lane >= HEAD_DIM) & (lane < MLA_QK)
        outs[2][...] = jnp.where(live, _rope_t(jnp.where(live, dkr, 0.0), c, s1, s2), 0.0).astype(BF16)
    ins = [(dq, HPAD, 0), (dk, HPAD, 0), (dv, HPAD, 0), (tabs[0], LANES, 0), (tabs[1], LANES, 0),
           (tabs[2], LANES, 0)]
    return _ew("mla_prep_bwd", body, ins, [(HPAD, BF16), (2 * HPAD, BF16), (LANES, BF16)], rows, 256)


def _gate_bwd(name, d_o, o_raw, z, gate_cb, rows):
    def body(ins, outs, _):
        for h in range(HEADS):
            cols = slice(h * LANES, (h + 1) * LANES)
            dog, o, g = ins[0][:, cols], ins[1][:, cols], ins[2][:, cols].astype(F32)
            sg = jax.nn.sigmoid(g)
            do = dog * (g * sg)
            outs[0][:, cols] = do.astype(BF16)
            outs[1][:, cols] = (dog * o * (sg * (1.0 + g * (1.0 - sg)))).astype(BF16)
            outs[2][:, cols] = jnp.broadcast_to(jnp.sum(do * o, axis=-1, keepdims=True), do.shape)
    ins = [(d_o, HPAD, 0), (o_raw, HPAD, 0), (z, HPAD, gate_cb)]
    return _ew(name, body, ins, [(HPAD, BF16), (HPAD, BF16), (HPAD, F32)], rows, 256)


def _merge_fwd(ua, ub, z, rows):
    def body(ins, outs, _):
        ua, ub, m_a, m_b = (r[...].astype(F32) for r in ins)
        outs[0][...] = (jax.nn.sigmoid(m_a) * ua + jax.nn.sigmoid(m_b) * ub).astype(BF16)
    ins = [(ua, D_MODEL, 0), (ub, D_MODEL, 0), (z, D_MODEL, Z_MA // D_MODEL), (z, D_MODEL, Z_MB // D_MODEL)]
    return _ew("merge_fwd", body, ins, [(D_MODEL, BF16)], rows, 256)[0]


def _merge_bwd(dy, ua, ub, z, rows):
    def body(ins, outs, _):
        dyv = ins[0][...]
        for idx in range(2):
            s = jax.nn.sigmoid(ins[3 + idx][...].astype(F32))
            outs[idx][...] = (dyv * s).astype(BF16)
            outs[2 + idx][...] = (dyv * ins[1 + idx][...].astype(F32) * (s * (1.0 - s))).astype(BF16)
    ins = [(dy, D_MODEL, 0), (ua, D_MODEL, 0), (ub, D_MODEL, 0),
           (z, D_MODEL, Z_MA // D_MODEL), (z, D_MODEL, Z_MB // D_MODEL)]
    return _ew("merge_bwd", body, ins, [(D_MODEL, BF16)] * 4, rows, 256)


def _ple_fwd(x1, u, e, rows):
    def body(ins, outs, _):
        outs[0][...] = ins[0][...] + jax.nn.sigmoid(ins[1][...]) * ins[2][...]
    return _ew("ple_fwd", body, [(x1, D_MODEL, 0), (u, D_MODEL, 0), (e, D_MODEL, 0)], [(D_MODEL, F32)], rows, 256)[0]


def _ple_bwd(dx2, u, e, rows):
    def body(ins, outs, _):
        d, s = ins[0][...], jax.nn.sigmoid(ins[1][...])
        outs[0][...] = (d * s).astype(BF16)
        outs[1][...] = (d * ins[2][...] * (s * (1.0 - s))).astype(BF16)
    return _ew("ple_bwd", body, [(dx2, D_MODEL, 0), (u, D_MODEL, 0), (e, D_MODEL, 0)],
               [(D_MODEL, BF16)] * 2, rows, 256)


def _loss_head(x, g, target, rows):
    def body(ins, outs, accs):
        xv, gv = ins[0][...], ins[1][...]
        r = lax.rsqrt(jnp.mean(xv * xv, axis=-1, keepdims=True) + EPS)
        xhat = xv * r
        err = xhat * gv - ins[2][...]
        accs[0][...] += jnp.broadcast_to(0.5 * jnp.sum(jnp.mean(err * err, axis=-1, keepdims=True),
                                                       axis=0, keepdims=True), (1, LANES))
        dyv = err * (1.0 / D_MODEL)
        accs[1][...] += jnp.sum(dyv * xhat, axis=0, keepdims=True)
        dy = dyv * gv
        outs[0][...] = r * (dy - xhat * jnp.mean(dy * xhat, axis=-1, keepdims=True))
    ins = [(x, D_MODEL, 0), (g.reshape(1, D_MODEL), None, None), (target, D_MODEL, 0)]
    return _ew("loss_head", body, ins, [(D_MODEL, F32)], rows, 256, accs=[(1, LANES), (1, D_MODEL)])


def _pad_heads_cols(w, n_heads, dim):
    k = w.shape[0]
    return jnp.pad(w.reshape(k, n_heads, dim), ((0, 0), (0, 0), (0, LANES - dim))).reshape(k, n_heads * LANES)


def _unpad_heads_cols(w, n_heads, dim):
    k = w.shape[0]
    return w.reshape(k, n_heads, LANES)[:, :, :dim].reshape(k, n_heads * dim)


def _layer_weights(w, i):
    segs = jnp.split(w['w_in'][i], list(_cumsum(IN_SIZES))[:-1], axis=1)
    a_q, a_k, a_v, a_gate, b_qd, b_kvd, b_kr, b_gate, m_a, m_b = segs
    kr = jnp.pad(b_kr, ((0, 0), (HEAD_DIM, LANES - MLA_QK)))
    w_in = jnp.concatenate([
        m_a, m_b, _pad_heads_cols(a_q, HEADS, HEAD_DIM), _pad_heads_cols(a_gate, HEADS, HEAD_DIM),
        _pad_heads_cols(b_gate, HEADS, HEAD_DIM), b_qd, _pad_heads_cols(a_k, SWA_KV_HEADS, HEAD_DIM),
        _pad_heads_cols(a_v, SWA_KV_HEADS, HEAD_DIM), b_kvd, kr], axis=1)
    w_uq = _pad_heads_cols(w['w_uq'][i], HEADS, MLA_QK)
    ukv = w['w_ukv'][i].reshape(MLA_KV_LORA, HEADS, 2 * HEAD_DIM)
    pad = ((0, 0), (0, 0), (0, HEAD_DIM))
    w_ukv = jnp.concatenate([jnp.pad(ukv[:, :, :HEAD_DIM], pad).reshape(MLA_KV_LORA, HPAD),
                             jnp.pad(ukv[:, :, HEAD_DIM:], pad).reshape(MLA_KV_LORA, HPAD)], axis=1)
    w_br_a = _pad_heads_cols(w['w_br_a'][i].T, HEADS, HEAD_DIM).T
    w_br_b = _pad_heads_cols(w['w_br_b'][i].T, HEADS, HEAD_DIM).T
    out = dict(w_in=w_in, w_uq=w_uq, w_ukv=w_ukv, w_br_a=w_br_a, w_br_b=w_br_b, w_out=w['w_out'][i],
               w_pg=w['w_ple_gate'][i], w_pp=w['w_ple_proj'][i])
    for name in ('w_in', 'w_uq', 'w_ukv', 'w_br_a', 'w_br_b', 'w_out', 'w_pg'):
        out[name + '_t'] = out[name].T
    return out


def _cumsum(sizes):
    acc, out = 0, []
    for s in sizes:
        acc += s
        out.append(acc)
    return out


def _unpad_grads(g):
    d = g['w_in']
    seg = lambda off, width: d[:, off:off + width]
    b_kr = seg(Z_BKR, LANES)[:, HEAD_DIM:MLA_QK]
    w_in = jnp.concatenate([
        _unpad_heads_cols(seg(Z_AQ, HPAD), HEADS, HEAD_DIM), _unpad_heads_cols(seg(Z_AK, 256), SWA_KV_HEADS, HEAD_DIM),
        _unpad_heads_cols(seg(Z_AV, 256), SWA_KV_HEADS, HEAD_DIM), _unpad_heads_cols(seg(Z_AGATE, HPAD), HEADS, HEAD_DIM),
        seg(Z_BQD, MLA_Q_LORA), seg(Z_BKVD, MLA_KV_LORA), b_kr, _unpad_heads_cols(seg(Z_BGATE, HPAD), HEADS, HEAD_DIM),
        seg(Z_MA, D_MODEL), seg(Z_MB, D_MODEL)], axis=1)
    w_uq = _unpad_heads_cols(g['w_uq'], HEADS, MLA_QK)
    ukv = g['w_ukv'].reshape(MLA_KV_LORA, 2, HEADS, LANES)[:, :, :, :HEAD_DIM]
    w_ukv = jnp.concatenate([ukv[:, 0], ukv[:, 1]], axis=-1).reshape(MLA_KV_LORA, HEADS * 2 * HEAD_DIM)
    w_br_a = _unpad_heads_cols(g['w_br_a'].T, HEADS, HEAD_DIM).T
    w_br_b = _unpad_heads_cols(g['w_br_b'].T, HEADS, HEAD_DIM).T
    return dict(w_in=w_in, w_uq=w_uq, w_ukv=w_ukv, w_br_a=w_br_a, w_br_b=w_br_b, w_out=g['w_out'],
                w_ple_gate=g['w_pg'], w_ple_proj=g['w_pp'], g_mix=g['g_mix'], sink=g['sink'], g_q=g['g_q'],
                g_kv=g['g_kv'], g_ple=g['g_ple'])


def _layer_fwd(x0, p_i, lw, sm, i, pos_col, pos_row, tabs, B, S):
    T = B * S
    h = _rms_fwd("norm_mix", x0, D_MODEL, 0, sm['g_mix'][i], T)
    z = _mm("proj_in", h, lw['w_in'], BF16)
    sink_row = jnp.pad(sm['sink'][i], (0, LANES - HEADS)).reshape(1, LANES)
    oa_raw, oa, lse_a = _swa_fwd(z, pos_col, pos_row, sink_row, B, S)
    qdn = _rms_fwd("norm_q", z, MLA_Q_LORA, Z_BQD // MLA_Q_LORA, sm['g_q'][i], T)
    kvdn = _rms_fwd("norm_kv", z, MLA_KV_LORA, Z_BKVD // MLA_KV_LORA, sm['g_kv'][i], T)
    q_pre = _mm("proj_uq", qdn, lw['w_uq'], F32)
    kv_pre = _mm("proj_ukv", kvdn, lw['w_ukv'], F32)
    qf, kf, vf = _mla_prep(q_pre, kv_pre, z, tabs, T)
    ob_raw, ob, lse_b = _mla_fwd(qf, kf, vf, z, B, S)
    ua = _mm("proj_br_a", oa, lw['w_br_a'], BF16)
    ub = _mm("proj_br_b", ob, lw['w_br_b'], BF16)
    y = _merge_fwd(ua, ub, z, T)
    x1 = _mm("proj_out", y, lw['w_out'], F32, residual=x0)
    hn = _rms_fwd("norm_ple", x1, D_MODEL, 0, sm['g_ple'][i], T)
    u = _mm("proj_pg", hn, lw['w_pg'], F32)
    e = _mm("proj_pp", p_i, lw['w_pp'], F32)
    x2 = _ple_fwd(x1, u, e, T)
    saved = dict(x0=x0, h=h, z=z, sink_row=sink_row, oa_raw=oa_raw, oa=oa, lse_a=lse_a, qdn=qdn, kvdn=kvdn,
                 qf=qf, kf=kf, vf=vf, ob_raw=ob_raw, ob=ob, lse_b=lse_b, ua=ua, ub=ub, y=y, x1=x1, hn=hn,
                 u=u, e=e, p=p_i)
    return x2, saved


def _layer_bwd(dx2, sv, lw, sm, i, pos_col, pos_row, tabs, B, S):
    T = B * S
    z = sv['z']
    g = {}
    d_e, d_u = _ple_bwd(dx2, sv['u'], sv['e'], T)
    g['w_pp'] = _mm_tn("grad_pp", sv['p'], d_e)
    g['w_pg'] = _mm_tn("grad_pg", sv['hn'], d_u)
    dhn = _mm("back_pg", d_u, lw['w_pg_t'], F32)
    dx1, g['g_ple'] = _rms_bwd("norm_ple_bwd", sv['x1'], D_MODEL, 0, sm['g_ple'][i], dhn, T, F32, dres=dx2)
    g['w_out'] = _mm_tn("grad_out", sv['y'], dx1)
    dy = _mm("back_out", dx1, lw['w_out_t'], F32)
    d_ua, d_ub, d_ma, d_mb = _merge_bwd(dy, sv['ua'], sv['ub'], z, T)
    g['w_br_a'] = _mm_tn("grad_br_a", sv['oa'], d_ua)
    g['w_br_b'] = _mm_tn("grad_br_b", sv['ob'], d_ub)
    d_oa = _mm("back_br_a", d_ua, lw['w_br_a_t'], F32)
    d_ob = _mm("back_br_b", d_ub, lw['w_br_b_t'], F32)
    dob_raw, d_bgate, delta_b = _gate_bwd("gate_b_bwd", d_ob, sv['ob_raw'], z, Z_BGATE // HPAD, T)
    dq, dk, dv = _mla_bwd(sv['qf'], sv['kf'], sv['vf'], dob_raw, sv['lse_b'], delta_b, B, S)
    dq_pre, dkv_pre, d_bkr = _mla_prep_bwd(dq, dk, dv, tabs, T)
    g['w_uq'] = _mm_tn("grad_uq", sv['qdn'], dq_pre)
    g['w_ukv'] = _mm_tn("grad_ukv", sv['kvdn'], dkv_pre)
    dqdn = _mm("back_uq", dq_pre, lw['w_uq_t'], F32)
    dkvdn = _mm("back_ukv", dkv_pre, lw['w_ukv_t'], F32)
    d_bqd, g['g_q'] = _rms_bwd("norm_q_bwd", z, MLA_Q_LORA, Z_BQD // MLA_Q_LORA, sm['g_q'][i], dqdn, T, BF16)
    d_bkvd, g['g_kv'] = _rms_bwd("norm_kv_bwd", z, MLA_KV_LORA, Z_BKVD // MLA_KV_LORA, sm['g_kv'][i], dkvdn, T, BF16)
    doa_raw, d_agate, delta_a = _gate_bwd("gate_a_bwd", d_oa, sv['oa_raw'], z, Z_AGATE // HPAD, T)
    d_aq, d_ak, d_av, dsink = _swa_bwd(z, pos_col, pos_row, sv['sink_row'], sv['lse_a'], doa_raw, delta_a, B, S)
    g['sink'] = dsink[0, :HEADS]
    dz = jnp.concatenate([d_ma, d_mb, d_aq, d_agate, d_bgate, d_bqd, d_ak.astype(BF16), d_av.astype(BF16),
                          d_bkvd, d_bkr], axis=1)
    g['w_in'] = _mm_tn("grad_in", sv['h'], dz)
    dh = _mm("back_in", dz, lw['w_in_t'], F32)
    dx0, g['g_mix'] = _rms_bwd("norm_mix_bwd", sv['x0'], D_MODEL, 0, sm['g_mix'][i], dh, T, F32, dres=dx1)
    for name in ('g_ple', 'g_q', 'g_kv', 'g_mix'):
        g[name] = g[name][0]
    return dx0, g


def _local_step(x, p, positions, wfull, sm, loss_target):
    B, S, _ = x.shape
    T = B * S
    pos_col = positions.reshape(T, 1)
    pos_row = positions.reshape(T // BLOCK, 1, BLOCK)
    half = MLA_ROPE // 2
    inv = ROPE_THETA ** (-jnp.arange(0, MLA_ROPE, 2, dtype=F32) / MLA_ROPE)
    inv_lane = jnp.tile(inv, LANES // half).reshape(1, LANES)
    tabs = _rope_tables(pos_col, inv_lane, T)
    xc = x.reshape(T, D_MODEL)
    lws, saved = [], []
    for i in range(DEPTH):
        lw = _layer_weights(wfull, i)
        xc, sv = _layer_fwd(xc, p[i].reshape(T, PLE_DIM), lw, sm, i, pos_col, pos_row, tabs, B, S)
        lws.append(lw)
        saved.append(sv)
    dx, loss, dg_final = _loss_head(xc, sm['g_final'], loss_target.reshape(T, D_MODEL), T)
    layer_grads = [None] * DEPTH
    for i in reversed(range(DEPTH)):
        dx, g = _layer_bwd(dx, saved[i], lws[i], sm, i, pos_col, pos_row, tabs, B, S)
        layer_grads[i] = _unpad_grads(g)
    return loss, dx.reshape(B, S, D_MODEL), layer_grads, dg_final[0]


SMALL_ROWS = 48


def _pack_small(arrs):
    flat = jnp.concatenate([arrs[name].reshape(-1) for name in SMALL])
    return jnp.pad(flat, (0, SMALL_ROWS * LANES - flat.shape[0])).reshape(SMALL_ROWS, LANES)


def _unpack_small(block, shapes):
    flat = block.reshape(-1)
    out, off = {}, 0
    for name in SMALL:
        n = math.prod(shapes[name])
        out[name] = flat[off:off + n].reshape(shapes[name])
        off += n
    return out


def _to_slots(g, axis):
    r, c = g.shape
    if axis == 0:
        return g.reshape(N_CHIPS, r // N_CHIPS, c)
    return g.reshape(r, N_CHIPS, c // N_CHIPS).transpose(1, 0, 2)


def _units(shapes):
    units = []
    for w, shape in enumerate(shapes):
        r = shape[-2]
        n = 4 if r >= 1024 else 1
        units += [(w, k * (r // n), r // n) for k in range(n)]
    return units


def _place():
    x, y, c = lax.axis_index("x"), lax.axis_index("y"), lax.axis_index("c")
    chips = [(1 - x, y), (x, 1 - y), (1 - x, 1 - y)]
    return x, y, c, chips


ANY = pl.BlockSpec(memory_space=pl.ANY)


def _remote(send_sems, recv_sems, k, src, dst, to):
    return pltpu.make_async_remote_copy(src_ref=src, dst_ref=dst, send_sem=send_sems.at[k],
                                        recv_sem=recv_sems.at[k], device_id=to, device_id_type=MESH)


def _gather_weights(shards):
    n = len(shards)
    units = _units([s.shape for s in shards])
    nu = len(units)

    def body(*refs):
        ins, outs = refs[:n], refs[n:2 * n]
        send_sems, recv_sems, local_sems = refs[2 * n:]
        x, y, c, chips = _place()
        me = 2 * x + y
        sibling = (x, y, 1 - c)
        copy = functools.partial(_remote, send_sems, recv_sems)
        keeps, sends = [], []
        for u, (w, r0, nr) in enumerate(units):
            rows = pl.ds(r0, nr)
            keeps.append(pltpu.make_async_copy(ins[w].at[:, rows, :], outs[w].at[me, :, rows, :], local_sems.at[u]))
            keeps[-1].start()
        for j, (cx, cy) in enumerate(chips):
            for u, (w, r0, nr) in enumerate(units):
                rows = pl.ds(r0, nr)
                sends.append(copy(j * nu + u, ins[w].at[c, rows, :], outs[w].at[me, c, rows, :], (cx, cy, c)))
                sends[-1].start()
        for j, (cx, cy) in enumerate(chips):
            for u, (w, r0, nr) in enumerate(units):
                landed = outs[w].at[2 * cx + cy, c, pl.ds(r0, nr), :]
                copy(j * nu + u, landed, landed, (cx, cy, c)).wait_recv()
                sends.append(copy((3 + j) * nu + u, landed, landed, sibling))
                sends[-1].start()
        for j, (cx, cy) in enumerate(chips):
            for u, (w, r0, nr) in enumerate(units):
                other = outs[w].at[2 * cx + cy, 1 - c, pl.ds(r0, nr), :]
                copy((3 + j) * nu + u, other, other, sibling).wait_recv()
        for cp in sends:
            cp.wait_send()
        for keep in keeps:
            keep.wait()

    return pl.pallas_call(
        body, name="gather_weights",
        out_shape=[jax.ShapeDtypeStruct((N_CHIPS,) + s.shape, s.dtype) for s in shards],
        in_specs=[ANY] * n, out_specs=[ANY] * n,
        scratch_shapes=[pltpu.SemaphoreType.DMA((6 * nu,)), pltpu.SemaphoreType.DMA((6 * nu,)),
                        pltpu.SemaphoreType.DMA((nu,))])(*shards)


def _pair_exchange(g0, g1):
    n = len(g0)

    def body(*refs):
        layers, outs = (refs[:n], refs[n:2 * n]), refs[2 * n:3 * n]
        send_sems, recv_sems = refs[3 * n:]
        x, y, c, _ = _place()
        copy = functools.partial(_remote, send_sems, recv_sems)
        for w in range(n):
            for q in range(N_CHIPS):
                for layer in range(DEPTH):
                    cp = copy(N_CHIPS * w + q, layers[layer][w].at[q], outs[w].at[q], (x, y, 1 - c))
                    pl.when(c == 1 - layer)(cp.start)
        for w in range(n):
            for q in range(N_CHIPS):
                copy(N_CHIPS * w + q, layers[0][w].at[q], outs[w].at[q], (x, y, 1 - c)).wait()

    return pl.pallas_call(
        body, name="pair_exchange", out_shape=[jax.ShapeDtypeStruct(g.shape, g.dtype) for g in g0],
        in_specs=[ANY] * (2 * n), out_specs=[ANY] * n,
        scratch_shapes=[pltpu.SemaphoreType.DMA((N_CHIPS * n,)), pltpu.SemaphoreType.DMA((N_CHIPS * n,))])(*g0, *g1)


def _pair_sum(name, g0, g1, theirs, cflag):
    shape = theirs.shape
    rows, width = shape[0] * shape[1], shape[2]

    def body(ins, outs, _):
        mine = jnp.where(ins[3][0:1, 0:1] == 0.0, ins[0][...], ins[1][...])
        tot = mine + ins[2][...]
        outs[0][...] = tot
        outs[1][...] = tot.astype(BF16)
    ins = [(a.reshape(rows, width), width, 0) for a in (g0, g1, theirs)] + [(cflag, None, None)]
    f32, bf16 = _ew(name, body, ins, [(width, F32), (width, BF16)], rows, 256)
    return f32.reshape(shape), bf16.reshape(shape)


def _chip_exchange(parts):
    n = len(parts)

    def body(*refs):
        ins, outs = refs[:n], refs[n:2 * n]
        send_sems, recv_sems = refs[2 * n:]
        x, y, c, chips = _place()
        copy = functools.partial(_remote, send_sems, recv_sems)
        sends = []
        for j, (cx, cy) in enumerate(chips):
            for w in range(n):
                sends.append(copy(j * n + w, ins[w].at[2 * cx + cy], outs[w].at[j], (cx, cy, c)))
                sends[-1].start()
        for j, (cx, cy) in enumerate(chips):
            for w in range(n):
                copy(j * n + w, outs[w].at[j], outs[w].at[j], (cx, cy, c)).wait_recv()
        for cp in sends:
            cp.wait_send()

    return pl.pallas_call(
        body, name="chip_exchange",
        out_shape=[jax.ShapeDtypeStruct((3,) + a.shape[1:], a.dtype) for a in parts],
        in_specs=[ANY] * n, out_specs=[ANY] * n,
        scratch_shapes=[pltpu.SemaphoreType.DMA((3 * n,)), pltpu.SemaphoreType.DMA((3 * n,))])(*parts)


def _chip_sum(name, part, landed, chipflag):
    _, r, width = part.shape
    tm = min(r, 256)

    def kern(p_ref, l_ref, flag_ref, o_ref):
        me = flag_ref[0:1, 0:1]
        own = jnp.where(me == 0.0, p_ref[0], jnp.where(me == 1.0, p_ref[1], jnp.where(me == 2.0, p_ref[2], p_ref[3])))
        o_ref[...] = ((own + l_ref[0].astype(F32)) + l_ref[1].astype(F32)) + l_ref[2].astype(F32)

    return pl.pallas_call(
        kern, name=name, grid=(r // tm,),
        in_specs=[pl.BlockSpec((N_CHIPS, tm, width), lambda i: (0, i, 0)),
                  pl.BlockSpec((3, tm, width), lambda i: (0, i, 0)),
                  pl.BlockSpec((1, LANES), lambda i: (0, 0))],
        out_specs=pl.BlockSpec((tm, width), lambda i: (i, 0)),
        out_shape=jax.ShapeDtypeStruct((r, width), F32), compiler_params=_params(("arbitrary",)))(part, landed, chipflag)


def _pair_broadcast(mine):
    n = len(mine)
    units = _units([a.shape for a in mine])

    def body(*refs):
        ins, outs = refs[:n], refs[n:2 * n]
        send_sems, recv_sems = refs[2 * n:]
        x, y, c, _ = _place()
        copy = functools.partial(_remote, send_sems, recv_sems)
        cps = [copy(u, ins[w].at[pl.ds(r0, nr), :], outs[w].at[pl.ds(r0, nr), :], (x, y, 1 - c))
               for u, (w, r0, nr) in enumerate(units)]
        for cp in cps:
            cp.start()
        for cp in cps:
            cp.wait()

    return pl.pallas_call(
        body, name="pair_broadcast", out_shape=[jax.ShapeDtypeStruct(a.shape, a.dtype) for a in mine],
        in_specs=[ANY] * n, out_specs=[ANY] * n,
        scratch_shapes=[pltpu.SemaphoreType.DMA((len(units),)), pltpu.SemaphoreType.DMA((len(units),))])(*mine)


def _small_allreduce(v):
    offsets = [(dx, dy, dc) for dx in (0, 1) for dy in (0, 1) for dc in (0, 1)][1:]

    def body(v_ref, out_ref, recv_ref, send_sems, recv_sems):
        x, y, c, _ = _place()
        flip = lambda a, d: 1 - a if d else a
        peers = [(flip(x, dx), flip(y, dy), flip(c, dc)) for dx, dy, dc in offsets]
        copy = functools.partial(_remote, send_sems, recv_sems)
        me = 4 * x + 2 * y + c
        recv_ref[me] = v_ref[...]
        cps = [copy(k, v_ref, recv_ref.at[me], peer) for k, peer in enumerate(peers)]
        for cp in cps:
            cp.start()
        for k, (px, py, pc) in enumerate(peers):
            landed = recv_ref.at[4 * px + 2 * py + pc]
            copy(k, landed, landed, (px, py, pc)).wait_recv()
        for cp in cps:
            cp.wait_send()
        tot = recv_ref[0]
        for d in range(1, 8):
            tot = tot + recv_ref[d]
        out_ref[...] = tot

    vmem = pl.BlockSpec(memory_space=pltpu.VMEM)
    return pl.pallas_call(
        body, name="small_allreduce", out_shape=jax.ShapeDtypeStruct(v.shape, v.dtype),
        in_specs=[vmem], out_specs=vmem,
        scratch_shapes=[pltpu.VMEM((8,) + v.shape, v.dtype), pltpu.SemaphoreType.DMA((7,)),
                        pltpu.SemaphoreType.DMA((7,))])(v)


def _adam_math(gv, wv, mv, vv):
    mv = ADAM_B1 * mv + (1.0 - ADAM_B1) * gv
    vv = ADAM_B2 * vv + (1.0 - ADAM_B2) * (gv * gv)
    m_hat = mv / (1.0 - ADAM_B1 ** ADAM_STEP)
    v_hat = vv / (1.0 - ADAM_B2 ** ADAM_STEP)
    return -ADAM_LR * (m_hat / (jnp.sqrt(v_hat) + ADAM_EPS) + ADAM_WD * wv), mv, vv


def _adamw_big(name, mine, theirs, cflag, w, m, v):
    _, r, width = w.shape
    tm = min(r, 256)

    def kern(mine_ref, theirs_ref, flag_ref, w_ref, m_ref, v_ref, g_ref, d_ref, nm_ref, nv_ref):
        layer = pl.program_id(0).astype(F32)
        gv = jnp.where(flag_ref[0:1, 0:1] == layer, mine_ref[...], theirs_ref[...])
        g_ref[0] = gv
        d_ref[0], nm_ref[0], nv_ref[0] = _adam_math(gv, w_ref[0], m_ref[0], v_ref[0])

    flat = pl.BlockSpec((tm, width), lambda l, i: (i, 0))
    stacked = pl.BlockSpec((1, tm, width), lambda l, i: (l, i, 0))
    return pl.pallas_call(
        kern, name=name, grid=(DEPTH, r // tm),
        in_specs=[flat, flat, pl.BlockSpec((1, LANES), lambda l, i: (0, 0)), stacked, stacked, stacked],
        out_specs=[stacked] * 4, out_shape=[jax.ShapeDtypeStruct(w.shape, F32)] * 4,
        compiler_params=_params(("arbitrary", "arbitrary")))(mine, theirs, cflag, w, m, v)


def _adamw_small(g, w, m, v):
    def body(ins, outs, _):
        outs[0][...], outs[1][...], outs[2][...] = _adam_math(*(r[...] for r in ins))
    return _ew("adamw_small", body, [(a, LANES, 0) for a in (g, w, m, v)], [(LANES, F32)] * 3, SMALL_ROWS, SMALL_ROWS)


def kernel(x, p, positions, g_mix, w_in, sink, g_q, w_uq, g_kv, w_ukv, w_br_a, w_br_b, w_out, g_ple, w_ple_gate, w_ple_proj, g_final, loss_target, m_g_mix, m_w_in, m_sink, m_g_q, m_w_uq, m_g_kv, m_w_ukv, m_w_br_a, m_w_br_b, m_w_out, m_g_ple, m_w_ple_gate, m_w_ple_proj, m_g_final, v_g_mix, v_w_in, v_sink, v_g_q, v_w_uq, v_g_kv, v_w_ukv, v_w_br_a, v_w_br_b, v_w_out, v_g_ple, v_w_ple_gate, v_w_ple_proj, v_g_final):
    w = dict(g_mix=g_mix, w_in=w_in, sink=sink, g_q=g_q, w_uq=w_uq, g_kv=g_kv, w_ukv=w_ukv, w_br_a=w_br_a,
             w_br_b=w_br_b, w_out=w_out, g_ple=g_ple, w_ple_gate=w_ple_gate, w_ple_proj=w_ple_proj, g_final=g_final)
    m = dict(g_mix=m_g_mix, w_in=m_w_in, sink=m_sink, g_q=m_g_q, w_uq=m_w_uq, g_kv=m_g_kv, w_ukv=m_w_ukv,
             w_br_a=m_w_br_a, w_br_b=m_w_br_b, w_out=m_w_out, g_ple=m_g_ple, w_ple_gate=m_w_ple_gate,
             w_ple_proj=m_w_ple_proj, g_final=m_g_final)
    v = dict(g_mix=v_g_mix, w_in=v_w_in, sink=v_sink, g_q=v_g_q, w_uq=v_w_uq, g_kv=v_g_kv, w_ukv=v_w_ukv,
             w_br_a=v_w_br_a, w_br_b=v_w_br_b, w_out=v_w_out, g_ple=v_g_ple, w_ple_gate=v_w_ple_gate,
             w_ple_proj=v_w_ple_proj, g_final=v_g_final)
    wfull = _gather_full(w)
    sm = {name: w[name] for name in SMALL}
    loss_row, grad_x, layer_grads, dg_final = _local_step(x, p, positions, wfull, sm, loss_target)
    loss = lax.psum(loss_row[0, 0], ("x", "y", "c"))
    res = _update(layer_grads, dg_final, w, m, v)
    return (loss, grad_x, *[res[name][kind] for kind in range(4) for name in WEIGHT_NAMES])


def _gather_full(w):
    gathered = _gather_weights([w[name].astype(BF16) for name, _ in SHARDED])
    return {name: [jnp.concatenate([gathered[k][q, layer] for q in range(N_CHIPS)], axis=axis - 1)
                   for layer in range(DEPTH)] for k, (name, axis) in enumerate(SHARDED)}


def _update(layer_grads, dg_final, w, m, v):
    small_shapes = {name: w[name].shape for name in SMALL}
    cflag = jnp.full((1, LANES), lax.axis_index("c"), F32)
    chipflag = jnp.full((1, LANES), 2 * lax.axis_index("x") + lax.axis_index("y"), F32)

    slots = [[_to_slots(layer_grads[layer][name], axis - 1) for name, axis in SHARDED] for layer in range(DEPTH)]
    theirs = _pair_exchange(slots[0], slots[1])
    pair = [_pair_sum("pair_sum_" + name, slots[0][k], slots[1][k], theirs[k], cflag)
            for k, (name, _) in enumerate(SHARDED)]
    landed = _chip_exchange([bf16 for _, bf16 in pair])
    mine = [_chip_sum("chip_sum_" + name, pair[k][0], landed[k], chipflag) for k, (name, _) in enumerate(SHARDED)]
    other = _pair_broadcast(mine)
    res = {name: _adamw_big("adamw_" + name, mine[k], other[k], cflag, w[name], m[name], v[name])
           for k, (name, _) in enumerate(SHARDED)}

    gsmall = {name: jnp.stack([layer_grads[layer][name] for layer in range(DEPTH)]) for name in SMALL[:-1]}
    gsmall['g_final'] = dg_final
    gsum = _small_allreduce(_pack_small(gsmall))
    small = (gsum,) + tuple(_adamw_small(gsum, _pack_small(w), _pack_small(m), _pack_small(v)))
    for name, arrs in zip(SMALL, zip(*[[_unpack_small(a, small_shapes)[n] for n in SMALL] for a in small])):
        res[name] = arrs
    return res
```

```python
import functools
import math

import jax
import jax.numpy as jnp
from jax import lax
from jax.experimental import pallas as pl
from jax.experimental.pallas import tpu as pltpu

F32 = jnp.float32
BF16 = jnp.bfloat16

D_MODEL = 1024
DEPTH = 2
PLE_DIM = 256
BLOCK = 128
EPS = 1e-6
NEG = -1e30
HEADS = 8
SWA_KV_HEADS = 2
HEAD_DIM = 64
LANES = 128
HPAD = HEADS * LANES
MLA_QK = 96
MLA_ROPE = 32
MLA_Q_LORA = 256
MLA_KV_LORA = 128
ROPE_THETA = 10000.0
IN_SIZES = (512, 128, 128, 512, 256, 128, 32, 512, 1024, 1024)

Z_MA, Z_MB, Z_AQ, Z_AGATE, Z_BGATE = 0, 1024, 2048, 3072, 4096
Z_BQD, Z_AK, Z_AV, Z_BKVD, Z_BKR = 5120, 5376, 5632, 5888, 6016
Z_WIDTH = 6144

ADAM_LR, ADAM_B1, ADAM_B2, ADAM_EPS, ADAM_WD, ADAM_STEP = 0.001, 0.9, 0.999, 1e-08, 0.01, 10

VMEM_LIMIT = 56 * 1024 * 1024
MESH = pl.DeviceIdType.MESH

WEIGHT_NAMES = ('g_mix', 'w_in', 'sink', 'g_q', 'w_uq', 'g_kv', 'w_ukv', 'w_br_a', 'w_br_b',
                'w_out', 'g_ple', 'w_ple_gate', 'w_ple_proj', 'g_final')
SHARDED = (('w_in', 2), ('w_uq', 2), ('w_ukv', 2), ('w_br_a', 2), ('w_br_b', 2),
           ('w_out', 1), ('w_ple_gate', 1), ('w_ple_proj', 2))
SMALL = ('g_mix', 'sink', 'g_q', 'g_kv', 'g_ple', 'g_final')
N_CHIPS = 4


def _params(sem):
    return pltpu.CompilerParams(dimension_semantics=sem, vmem_limit_bytes=VMEM_LIMIT)


def _ew(name, body, ins, outs, rows, tm, accs=()):
    n_in, n_out = len(ins), len(outs)
    in_specs, args = [], []
    for arr, width, cb in ins:
        if width is None:
            in_specs.append(pl.BlockSpec(arr.shape, lambda i, nd=arr.ndim: (0,) * nd))
        else:
            in_specs.append(pl.BlockSpec((tm, width), lambda i, cb=cb: (i, cb)))
        args.append(arr)
    out_shape = [jax.ShapeDtypeStruct((rows, w), dt) for w, dt in outs]
    out_shape += [jax.ShapeDtypeStruct(s, F32) for s in accs]
    out_specs = [pl.BlockSpec((tm, w), lambda i: (i, 0)) for w, _ in outs]
    out_specs += [pl.BlockSpec(s, lambda i: (0, 0)) for s in accs]

    def kern(*refs):
        acc_refs = refs[n_in + n_out:]
        if acc_refs:
            @pl.when(pl.program_id(0) == 0)
            def _():
                for r in acc_refs:
                    r[...] = jnp.zeros_like(r)
        body(refs[:n_in], refs[n_in:n_in + n_out], acc_refs)

    res = pl.pallas_call(kern, name=name, grid=(rows // tm,), in_specs=in_specs, out_specs=out_specs,
                         out_shape=out_shape, compiler_params=_params(("arbitrary",)))(*args)
    return res


def _rms_fwd(name, x, width, cb, g, rows):
    def body(ins, outs, _):
        xv = ins[0][...].astype(F32)
        r = lax.rsqrt(jnp.mean(xv * xv, axis=-1, keepdims=True) + EPS)
        outs[0][...] = ((xv * r) * ins[1][...]).astype(BF16)
    return _ew(name, body, [(x, width, cb), (g.reshape(1, width), None, None)], [(width, BF16)], rows, 256)[0]


def _rms_bwd(name, x, width, cb, g, dh, rows, out_dtype, dres=None):
    def body(ins, outs, accs):
        xv, gv, dhv = ins[0][...].astype(F32), ins[1][...], ins[2][...].astype(F32)
        r = lax.rsqrt(jnp.mean(xv * xv, axis=-1, keepdims=True) + EPS)
        xhat = xv * r
        accs[0][...] += jnp.sum(dhv * xhat, axis=0, keepdims=True)
        dy = dhv * gv
        dx = r * (dy - xhat * jnp.mean(dy * xhat, axis=-1, keepdims=True))
        if dres is not None:
            dx = dx + ins[3][...]
        outs[0][...] = dx.astype(out_dtype)
    ins = [(x, width, cb), (g.reshape(1, width), None, None), (dh, width, 0)]
    if dres is not None:
        ins.append((dres, width, 0))
    return _ew(name, body, ins, [(width, out_dtype)], rows, 256, accs=[(1, width)])


def _mm(name, a, b, out_dtype, residual=None, tm=256, tn=512):
    M, K = a.shape
    N = b.shape[1]
    tm, tn = min(tm, M), min(tn, N)
    has_res = residual is not None

    def kern(*refs):
        a_ref, b_ref, o_ref = refs[0], refs[1], refs[-1]
        av = a_ref[...].astype(BF16)
        for j in range(N // tn):
            cols = slice(j * tn, (j + 1) * tn)
            part = jnp.dot(av, b_ref[:, cols], preferred_element_type=F32)
            if has_res:
                part = part + refs[2][:, cols]
            o_ref[:, cols] = part.astype(o_ref.dtype)

    in_specs = [pl.BlockSpec((tm, K), lambda i: (i, 0)), pl.BlockSpec((K, N), lambda i: (0, 0))]
    args = [a, b]
    if has_res:
        in_specs.append(pl.BlockSpec((tm, N), lambda i: (i, 0)))
        args.append(residual)
    return pl.pallas_call(
        kern, name=name, grid=(M // tm,), in_specs=in_specs, out_specs=pl.BlockSpec((tm, N), lambda i: (i, 0)),
        out_shape=jax.ShapeDtypeStruct((M, N), out_dtype), compiler_params=_params(("parallel",)))(*args)


def _mm_tn(name, a, b, tk=512, tn=2048):
    T, M = a.shape
    N = b.shape[1]
    tn, tk = min(tn, N), min(tk, T)

    def kern(a_ref, b_ref, o_ref):
        k = pl.program_id(1)
        part = _dot_tn(a_ref[...].astype(BF16), b_ref[...].astype(BF16))

        @pl.when(k == 0)
        def _():
            o_ref[...] = part

        @pl.when(k > 0)
        def _():
            o_ref[...] += part

    return pl.pallas_call(
        kern, name=name, grid=(N // tn, T // tk),
        in_specs=[pl.BlockSpec((tk, M), lambda j, k: (k, 0)), pl.BlockSpec((tk, tn), lambda j, k: (k, j))],
        out_specs=pl.BlockSpec((M, tn), lambda j, k: (0, j)),
        out_shape=jax.ShapeDtypeStruct((M, N), F32),
        compiler_params=_params(("parallel", "arbitrary")))(a, b)


def _dot_nt(a, b):
    return lax.dot_general(a, b, (((1,), (1,)), ((), ())), preferred_element_type=F32)


def _dot_tn(a, b):
    return lax.dot_general(a, b, (((0,), (0,)), ((), ())), preferred_element_type=F32)


SWA_SCALE = HEAD_DIM ** -0.5


def _swa_band(n, pq_ref, pkp_ref, pkc_ref):
    posk = jnp.concatenate([pkp_ref[0], pkc_ref[0]], axis=1)
    dist = (pq_ref[...] - posk).astype(F32)
    qi = lax.broadcasted_iota(jnp.int32, (BLOCK, 2 * BLOCK), 0)
    kj = lax.broadcasted_iota(jnp.int32, (BLOCK, 2 * BLOCK), 1)
    t_abs = n * BLOCK + qi
    s_abs = n * BLOCK - BLOCK + kj
    return dist, (s_abs >= 0) & (s_abs <= t_abs) & (t_abs - s_abs < BLOCK)


SWA_GROUP = HEADS // SWA_KV_HEADS


def _swa_group_q(q_all, g):
    heads = range(g * SWA_GROUP, (g + 1) * SWA_GROUP)
    return jnp.concatenate([(q_all[:, h * LANES:(h + 1) * LANES] * SWA_SCALE).astype(BF16) for h in heads], axis=0)


def _swa_mask(s, dist, valid, h):
    return jnp.where(valid, s - (2.0 ** -(h + 1)) * dist, NEG)


def _swa_specs(nb):
    prev = lambda b, n: b * nb + jnp.maximum(n - 1, 0)
    own = lambda b, n: b * nb + n
    return [
        pl.BlockSpec((BLOCK, HPAD), lambda b, n: (own(b, n), Z_AQ // HPAD)),
        pl.BlockSpec((BLOCK, 256), lambda b, n: (prev(b, n), Z_AK // 256)),
        pl.BlockSpec((BLOCK, 256), lambda b, n: (own(b, n), Z_AK // 256)),
        pl.BlockSpec((BLOCK, 256), lambda b, n: (prev(b, n), Z_AV // 256)),
        pl.BlockSpec((BLOCK, 256), lambda b, n: (own(b, n), Z_AV // 256)),
        pl.BlockSpec((BLOCK, 1), lambda b, n: (own(b, n), 0)),
        pl.BlockSpec((1, 1, BLOCK), lambda b, n: (prev(b, n), 0, 0)),
        pl.BlockSpec((1, 1, BLOCK), lambda b, n: (own(b, n), 0, 0)),
    ]


def _swa_fwd(z, pos_col, pos_row, sink_row, B, S):
    nb = S // BLOCK
    T = B * S

    def kern(q_ref, kp_ref, kc_ref, vp_ref, vc_ref, pq_ref, pkp_ref, pkc_ref, gate_ref, sink_ref,
             oraw_ref, og_ref, lse_ref):
        q_all = q_ref[...]
        kb = jnp.concatenate([kp_ref[...], kc_ref[...]], axis=0).astype(BF16)
        vb = jnp.concatenate([vp_ref[...], vc_ref[...]], axis=0).astype(BF16)
        dist, valid = _swa_band(pl.program_id(1), pq_ref, pkp_ref, pkc_ref)
        lane = lax.broadcasted_iota(jnp.int32, (BLOCK, LANES), 1)
        lse_all = jnp.zeros((BLOCK, LANES), F32)
        for grp in range(SWA_KV_HEADS):
            gcols = slice(grp * LANES, (grp + 1) * LANES)
            s_all = _dot_nt(_swa_group_q(q_all, grp), kb[:, gcols])
            probs = []
            for hh in range(SWA_GROUP):
                h = grp * SWA_GROUP + hh
                s = _swa_mask(s_all[hh * BLOCK:(hh + 1) * BLOCK], dist, valid, h)
                sink_h = sink_ref[0:1, h:h + 1]
                m = jnp.maximum(jnp.max(s, axis=-1, keepdims=True), sink_h)
                e = jnp.exp(s - m)
                denom = jnp.sum(e, axis=-1, keepdims=True) + jnp.exp(sink_h - m)
                probs.append((e * (1.0 / denom)).astype(BF16))
                lse_all = jnp.where(lane == h, m + jnp.log(denom), lse_all)
            o_all = jnp.dot(jnp.concatenate(probs, axis=0), vb[:, gcols], preferred_element_type=F32)
            for hh in range(SWA_GROUP):
                cols = slice((grp * SWA_GROUP + hh) * LANES, (grp * SWA_GROUP + hh + 1) * LANES)
                o = o_all[hh * BLOCK:(hh + 1) * BLOCK]
                oraw_ref[:, cols] = o
                g = gate_ref[:, cols].astype(F32)
                og_ref[:, cols] = (o * (g * jax.nn.sigmoid(g))).astype(BF16)
        lse_ref[...] = lse_all

    own = lambda b, n: b * nb + n
    in_specs = _swa_specs(nb) + [
        pl.BlockSpec((BLOCK, HPAD), lambda b, n: (own(b, n), Z_AGATE // HPAD)),
        pl.BlockSpec((1, LANES), lambda b, n: (0, 0)),
    ]
    out_specs = [pl.BlockSpec((BLOCK, HPAD), lambda b, n: (own(b, n), 0)),
                 pl.BlockSpec((BLOCK, HPAD), lambda b, n: (own(b, n), 0)),
                 pl.BlockSpec((BLOCK, LANES), lambda b, n: (own(b, n), 0))]
    out_shape = [jax.ShapeDtypeStruct((T, HPAD), F32), jax.ShapeDtypeStruct((T, HPAD), BF16),
                 jax.ShapeDtypeStruct((T, LANES), F32)]
    return pl.pallas_call(kern, name="swa_fwd", grid=(B, nb), in_specs=in_specs, out_specs=out_specs,
                          out_shape=out_shape, compiler_params=_params(("parallel", "arbitrary")))(
        z, z, z, z, z, pos_col, pos_row, pos_row, z, sink_row)


def _swa_bwd(z, pos_col, pos_row, sink_row, lse, do_raw, delta, B, S):
    nb = S // BLOCK
    T = B * S

    def kern(q_ref, kp_ref, kc_ref, vp_ref, vc_ref, pq_ref, pkp_ref, pkc_ref, sink_ref, lse_ref, do_ref,
             delta_ref, dq_ref, dk_ref, dv_ref, dsink_ref):
        b, n = pl.program_id(0), pl.program_id(1)

        @pl.when(n == 0)
        def _():
            dk_ref[...] = jnp.zeros_like(dk_ref)
            dv_ref[...] = jnp.zeros_like(dv_ref)

        @pl.when((b == 0) & (n == 0))
        def _():
            dsink_ref[...] = jnp.zeros_like(dsink_ref)

        q_all = q_ref[...]
        kb = jnp.concatenate([kp_ref[...], kc_ref[...]], axis=0).astype(BF16)
        vb = jnp.concatenate([vp_ref[...], vc_ref[...]], axis=0).astype(BF16)
        dist, valid = _swa_band(n, pq_ref, pkp_ref, pkc_ref)
        lane1 = lax.broadcasted_iota(jnp.int32, (1, LANES), 1)
        dsink = jnp.zeros((1, LANES), F32)
        dk_band, dv_band = [], []
        for grp in range(SWA_KV_HEADS):
            gcols = slice(grp * LANES, (grp + 1) * LANES)
            heads = range(grp * SWA_GROUP, (grp + 1) * SWA_GROUP)
            qg = _swa_group_q(q_all, grp)
            dog = jnp.concatenate([do_ref[:, h * LANES:(h + 1) * LANES] for h in heads], axis=0)
            s_all = _dot_nt(qg, kb[:, gcols])
            dp_all = _dot_nt(dog, vb[:, gcols])
            ps, dss = [], []
            for hh, h in enumerate(heads):
                blk = slice(hh * BLOCK, (hh + 1) * BLOCK)
                lse_h = lse_ref[:, h:h + 1]
                delta_h = delta_ref[:, h * LANES:h * LANES + 1]
                p = jnp.exp(_swa_mask(s_all[blk], dist, valid, h) - lse_h)
                ps.append(p.astype(BF16))
                dss.append((p * (dp_all[blk] - delta_h)).astype(BF16))
                psink = jnp.exp(sink_ref[0:1, h:h + 1] - lse_h)
                dsink = dsink + jnp.where(lane1 == h, -jnp.sum(psink * delta_h, axis=0, keepdims=True), 0.0)
            dsg = jnp.concatenate(dss, axis=0)
            dq_all = jnp.dot(dsg, kb[:, gcols], preferred_element_type=F32) * SWA_SCALE
            for hh, h in enumerate(heads):
                dq_ref[:, h * LANES:(h + 1) * LANES] = dq_all[hh * BLOCK:(hh + 1) * BLOCK].astype(BF16)
            dk_band.append(_dot_tn(dsg, qg))
            dv_band.append(_dot_tn(jnp.concatenate(ps, axis=0), dog))
        dsink_ref[...] += dsink
        dkb = jnp.concatenate(dk_band, axis=1)
        dvb = jnp.concatenate(dv_band, axis=1)
        r_prev = pl.ds(pl.multiple_of(jnp.maximum(n - 1, 0) * BLOCK, BLOCK), BLOCK)
        r_own = pl.ds(pl.multiple_of(n * BLOCK, BLOCK), BLOCK)
        dk_ref[r_prev, :] += dkb[:BLOCK]
        dk_ref[r_own, :] += dkb[BLOCK:]
        dv_ref[r_prev, :] += dvb[:BLOCK]
        dv_ref[r_own, :] += dvb[BLOCK:]

    own = lambda b, n: b * nb + n
    in_specs = _swa_specs(nb) + [
        pl.BlockSpec((1, LANES), lambda b, n: (0, 0)),
        pl.BlockSpec((BLOCK, LANES), lambda b, n: (own(b, n), 0)),
        pl.BlockSpec((BLOCK, HPAD), lambda b, n: (own(b, n), 0)),
        pl.BlockSpec((BLOCK, HPAD), lambda b, n: (own(b, n), 0)),
    ]
    out_specs = [pl.BlockSpec((BLOCK, HPAD), lambda b, n: (own(b, n), 0)),
                 pl.BlockSpec((S, 256), lambda b, n: (b, 0)),
                 pl.BlockSpec((S, 256), lambda b, n: (b, 0)),
                 pl.BlockSpec((1, LANES), lambda b, n: (0, 0))]
    out_shape = [jax.ShapeDtypeStruct((T, HPAD), BF16), jax.ShapeDtypeStruct((T, 256), F32),
                 jax.ShapeDtypeStruct((T, 256), F32), jax.ShapeDtypeStruct((1, LANES), F32)]
    return pl.pallas_call(kern, name="swa_bwd", grid=(B, nb), in_specs=in_specs, out_specs=out_specs,
                          out_shape=out_shape, compiler_params=_params(("arbitrary", "arbitrary")))(
        z, z, z, z, z, pos_col, pos_row, pos_row, sink_row, lse, do_raw, delta)


MLA_T = 256
MLA_HG = 2
MLA_W = MLA_HG * LANES
MLA_SCALE = MLA_QK ** -0.5
LOG2E = 1.4426950408889634
MLA_QSCALE = MLA_SCALE * LOG2E


def _causal(s):
    row = lax.broadcasted_iota(jnp.int32, s.shape, 0)
    col = lax.broadcasted_iota(jnp.int32, s.shape, 1)
    return jnp.where(col <= row, s, NEG)


def _mla_fwd(q, k, v, z, B, S):
    T = B * S
    nq = S // MLA_T

    def kern(q_ref, k_ref, v_ref, gate_ref, oraw_ref, og_ref, lse_ref):
        i = pl.program_id(2)

        def scores(j):
            rows = pl.ds(pl.multiple_of(j * MLA_T, MLA_T), MLA_T)
            return tuple(_dot_nt(q_ref[:, hh * LANES:(hh + 1) * LANES], k_ref[rows, hh * LANES:(hh + 1) * LANES])
                         for hh in range(MLA_HG))

        def update(j, ss, state):
            rows = pl.ds(pl.multiple_of(j * MLA_T, MLA_T), MLA_T)
            out = []
            for hh in range(MLA_HG):
                (m, l, acc), s = state[hh], ss[hh]
                m_new = jnp.maximum(m, jnp.max(s, axis=-1, keepdims=True))
                alpha = jnp.exp2(m - m_new)
                p = jnp.exp2(s - m_new)
                l = alpha * l + jnp.sum(p, axis=-1, keepdims=True)
                pv = jnp.dot(p.astype(BF16), v_ref[rows, hh * LANES:(hh + 1) * LANES], preferred_element_type=F32)
                out.append((m_new, l, alpha * acc + pv))
            return tuple(out)

        def body(j, carry):
            state, ss = carry
            s_next = scores(j + 1)
            return update(j, ss, state), s_next

        init = tuple((jnp.full((MLA_T, 1), NEG, F32), jnp.zeros((MLA_T, 1), F32), jnp.zeros((MLA_T, LANES), F32))
                     for _ in range(MLA_HG))
        state, ss = lax.fori_loop(0, i, body, (init, scores(0)))
        state = update(i, tuple(_causal(s) for s in ss), state)
        for hh in range(MLA_HG):
            m, l, acc = state[hh]
            cols = slice(hh * LANES, (hh + 1) * LANES)
            o = acc * (1.0 / l)
            oraw_ref[:, cols] = o
            g = gate_ref[:, cols].astype(F32)
            og_ref[:, cols] = (o * (g * jax.nn.sigmoid(g))).astype(BF16)
            lse_ref[:, cols] = jnp.broadcast_to(m + jnp.log2(l), (MLA_T, LANES))

    blk = lambda b, h, i: (b * nq + i, h)
    in_specs = [pl.BlockSpec((MLA_T, MLA_W), blk),
                pl.BlockSpec((S, MLA_W), lambda b, h, i: (b, h)),
                pl.BlockSpec((S, MLA_W), lambda b, h, i: (b, h)),
                pl.BlockSpec((MLA_T, MLA_W), lambda b, h, i: (b * nq + i, Z_BGATE // MLA_W + h))]
    out_specs = [pl.BlockSpec((MLA_T, MLA_W), blk)] * 3
    out_shape = [jax.ShapeDtypeStruct((T, HPAD), F32), jax.ShapeDtypeStruct((T, HPAD), BF16),
                 jax.ShapeDtypeStruct((T, HPAD), F32)]
    return pl.pallas_call(kern, name="mla_fwd", grid=(B, HEADS // MLA_HG, nq), in_specs=in_specs,
                          out_specs=out_specs, out_shape=out_shape,
                          compiler_params=_params(("parallel", "parallel", "arbitrary")))(q, k, v, z)


def _mla_bwd(q, k, v, do_raw, lse, delta, B, S):
    T = B * S
    nk = S // MLA_T

    def kern(q_ref, k_ref, v_ref, do_ref, lse_ref, delta_ref, dq_ref, dk_ref, dv_ref, dk_acc, dv_acc):
        j = pl.program_id(2)

        @pl.when(j == 0)
        def _():
            dq_ref[...] = jnp.zeros_like(dq_ref)

        dk_acc[...] = jnp.zeros_like(dk_acc)
        dv_acc[...] = jnp.zeros_like(dv_acc)

        def step(i, masked):
            rows = pl.ds(pl.multiple_of(i * MLA_T, MLA_T), MLA_T)
            for hh in range(MLA_HG):
                cols = slice(hh * LANES, (hh + 1) * LANES)
                kv, vv = k_ref[:, cols], v_ref[:, cols]
                qv, do = q_ref[rows, cols], do_ref[rows, cols]
                s = _dot_nt(qv, kv)
                if masked:
                    s = _causal(s)
                p = jnp.exp2(s - lse_ref[rows, hh * LANES:hh * LANES + 1])
                dp = _dot_nt(do, vv)
                ds = (p * (dp - delta_ref[rows, hh * LANES:hh * LANES + 1])).astype(BF16)
                dv_acc[:, cols] += _dot_tn(p.astype(BF16), do)
                dk_acc[:, cols] += _dot_tn(ds, qv)
                dq_ref[rows, cols] += jnp.dot(ds, kv, preferred_element_type=F32)

        step(j, True)

        def body(i, c):
            step(i, False)
            return c

        lax.fori_loop(j + 1, nk, body, 0)
        dk_ref[...] = dk_acc[...] * (1.0 / LOG2E)
        dv_ref[...] = dv_acc[...]

    whole = lambda b, h, j: (b, h)
    tile = lambda b, h, j: (b * nk + j, h)
    in_specs = [pl.BlockSpec((S, MLA_W), whole), pl.BlockSpec((MLA_T, MLA_W), tile),
                pl.BlockSpec((MLA_T, MLA_W), tile), pl.BlockSpec((S, MLA_W), whole),
                pl.BlockSpec((S, MLA_W), whole), pl.BlockSpec((S, MLA_W), whole)]
    out_specs = [pl.BlockSpec((S, MLA_W), whole), pl.BlockSpec((MLA_T, MLA_W), tile),
                 pl.BlockSpec((MLA_T, MLA_W), tile)]
    out_shape = [jax.ShapeDtypeStruct((T, HPAD), F32)] * 3
    return pl.pallas_call(kern, name="mla_bwd", grid=(B, HEADS // MLA_HG, nk), in_specs=in_specs,
                          out_specs=out_specs, out_shape=out_shape,
                          scratch_shapes=[pltpu.VMEM((MLA_T, MLA_W), F32), pltpu.VMEM((MLA_T, MLA_W), F32)],
                          compiler_params=_params(("parallel", "parallel", "arbitrary")))(
        q, k, v, do_raw, lse, delta)


def _rope_tables(pos_col, inv_lane, rows):
    def body(ins, outs, _):
        ang = ins[0][...].astype(F32) * ins[1][...]
        lane = lax.broadcasted_iota(jnp.int32, ang.shape, 1)
        cos, sin = jnp.cos(ang), jnp.sin(ang)
        first = (lane >= HEAD_DIM) & (lane < HEAD_DIM + MLA_ROPE // 2)
        second = (lane >= HEAD_DIM + MLA_ROPE // 2) & (lane < MLA_QK)
        outs[0][...] = jnp.where(lane < HEAD_DIM, 1.0, jnp.where(lane < MLA_QK, cos, 0.0))
        outs[1][...] = jnp.where(first, -sin, 0.0)
        outs[2][...] = jnp.where(second, sin, 0.0)
    return _ew("rope_tables", body, [(pos_col, 1, 0), (inv_lane, None, None)], [(LANES, F32)] * 3, rows, 256)


def _rope(x, c, s1, s2):
    return x * c + pltpu.roll(x, 112, 1) * s1 + pltpu.roll(x, 16, 1) * s2


def _rope_t(d, c, s1, s2):
    return d * c + pltpu.roll(d * s1, 16, 1) + pltpu.roll(d * s2, 112, 1)


def _mla_prep(q_pre, kv_pre, z, tabs, rows):
    def body(ins, outs, _):
        c, s1, s2 = ins[3][...], ins[4][...], ins[5][...]
        kr = _rope(ins[2][...].astype(F32), c, s1, s2)
        for h in range(HEADS):
            cols = slice(h * LANES, (h + 1) * LANES)
            outs[0][:, cols] = (_rope(ins[0][:, cols], c, s1, s2) * MLA_QSCALE).astype(BF16)
            outs[1][:, cols] = (ins[1][:, cols] + kr).astype(BF16)
        outs[2][...] = ins[6][...].astype(BF16)
    ins = [(q_pre, HPAD, 0), (kv_pre, HPAD, 0), (z, LANES, Z_BKR // LANES),
           (tabs[0], LANES, 0), (tabs[1], LANES, 0), (tabs[2], LANES, 0), (kv_pre, HPAD, 1)]
    return _ew("mla_prep", body, ins, [(HPAD, BF16)] * 3, rows, 256)


def _mla_prep_bwd(dq, dk, dv, tabs, rows):
    def body(ins, outs, _):
        c, s1, s2 = ins[3][...], ins[4][...], ins[5][...]
        lane = lax.broadcasted_iota(jnp.int32, c.shape, 1)
        dkr = jnp.zeros(c.shape, F32)
        for h in range(HEADS):
            cols = slice(h * LANES, (h + 1) * LANES)
            outs[0][:, cols] = _rope_t(ins[0][:, cols] * MLA_SCALE, c, s1, s2).astype(BF16)
            dkh = ins[1][:, cols]
            outs[1][:, cols] = jnp.where(lane < HEAD_DIM, dkh, 0.0).astype(BF16)
            dkr = dkr + dkh
        outs[1][:, HPAD:] = ins[2][...].astype(BF16)
        live = (lane >= HEAD_DIM) & (lane < MLA_QK)
        outs[2][...] = jnp.where(live, _rope_t(jnp.where(live, dkr, 0.0), c, s1, s2), 0.0).astype(BF16)
    ins = [(dq, HPAD, 0), (dk, HPAD, 0), (dv, HPAD, 0), (tabs[0], LANES, 0), (tabs[1], LANES, 0),
           (tabs[2], LANES, 0)]
    return _ew("mla_prep_bwd", body, ins, [(HPAD, BF16), (2 * HPAD, BF16), (LANES, BF16)], rows, 256)


def _gate_bwd(name, d_o, o_raw, z, gate_cb, rows):
    def body(ins, outs, _):
        for h in range(HEADS):
            cols = slice(h * LANES, (h + 1) * LANES)
            dog, o, g = ins[0][:, cols], ins[1][:, cols], ins[2][:, cols].astype(F32)
            sg = jax.nn.sigmoid(g)
            do = dog * (g * sg)
            outs[0][:, cols] = do.astype(BF16)
            outs[1][:, cols] = (dog * o * (sg * (1.0 + g * (1.0 - sg)))).astype(BF16)
            outs[2][:, cols] = jnp.broadcast_to(jnp.sum(do * o, axis=-1, keepdims=True), do.shape)
    ins = [(d_o, HPAD, 0), (o_raw, HPAD, 0), (z, HPAD, gate_cb)]
    return _ew(name, body, ins, [(HPAD, BF16), (HPAD, BF16), (HPAD, F32)], rows, 256)


def _merge_fwd(ua, ub, z, rows):
    def body(ins, outs, _):
        ua, ub, m_a, m_b = (r[...].astype(F32) for r in ins)
        outs[0][...] = (jax.nn.sigmoid(m_a) * ua + jax.nn.sigmoid(m_b) * ub).astype(BF16)
    ins = [(ua, D_MODEL, 0), (ub, D_MODEL, 0), (z, D_MODEL, Z_MA // D_MODEL), (z, D_MODEL, Z_MB // D_MODEL)]
    return _ew("merge_fwd", body, ins, [(D_MODEL, BF16)], rows, 256)[0]


def _merge_bwd(dy, ua, ub, z, rows):
    def body(ins, outs, _):
        dyv = ins[0][...]
        for idx in range(2):
            s = jax.nn.sigmoid(ins[3 + idx][...].astype(F32))
            outs[idx][...] = (dyv * s).astype(BF16)
            outs[2 + idx][...] = (dyv * ins[1 + idx][...].astype(F32) * (s * (1.0 - s))).astype(BF16)
    ins = [(dy, D_MODEL, 0), (ua, D_MODEL, 0), (ub, D_MODEL, 0),
           (z, D_MODEL, Z_MA // D_MODEL), (z, D_MODEL, Z_MB // D_MODEL)]
    return _ew("merge_bwd", body, ins, [(D_MODEL, BF16)] * 4, rows, 256)


def _ple_fwd(x1, u, e, rows):
    def body(ins, outs, _):
        outs[0][...] = ins[0][...] + jax.nn.sigmoid(ins[1][...]) * ins[2][...]
    return _ew("ple_fwd", body, [(x1, D_MODEL, 0), (u, D_MODEL, 0), (e, D_MODEL, 0)], [(D_MODEL, F32)], rows, 256)[0]


def _ple_bwd(dx2, u, e, rows):
    def body(ins, outs, _):
        d, s = ins[0][...], jax.nn.sigmoid(ins[1][...])
        outs[0][...] = (d * s).astype(BF16)
        outs[1][...] = (d * ins[2][...] * (s * (1.0 - s))).astype(BF16)
    return _ew("ple_bwd", body, [(dx2, D_MODEL, 0), (u, D_MODEL, 0), (e, D_MODEL, 0)],
               [(D_MODEL, BF16)] * 2, rows, 256)


def _loss_head(x, g, target, rows):
    def body(ins, outs, accs):
        xv, gv = ins[0][...], ins[1][...]
        r = lax.rsqrt(jnp.mean(xv * xv, axis=-1, keepdims=True) + EPS)
        xhat = xv * r
        err = xhat * gv - ins[2][...]
        accs[0][...] += jnp.broadcast_to(0.5 * jnp.sum(jnp.mean(err * err, axis=-1, keepdims=True),
                                                       axis=0, keepdims=True), (1, LANES))
        dyv = err * (1.0 / D_MODEL)
        accs[1][...] += jnp.sum(dyv * xhat, axis=0, keepdims=True)
        dy = dyv * gv
        outs[0][...] = r * (dy - xhat * jnp.mean(dy * xhat, axis=-1, keepdims=True))
    ins = [(x, D_MODEL, 0), (g.reshape(1, D_MODEL), None, None), (target, D_MODEL, 0)]
    return _ew("loss_head", body, ins, [(D_MODEL, F32)], rows, 256, accs=[(1, LANES), (1, D_MODEL)])


def _pad_heads_cols(w, n_heads, dim):
    k = w.shape[0]
    return jnp.pad(w.reshape(k, n_heads, dim), ((0, 0), (0, 0), (0, LANES - dim))).reshape(k, n_heads * LANES)


def _unpad_heads_cols(w, n_heads, dim):
    k = w.shape[0]
    return w.reshape(k, n_heads, LANES)[:, :, :dim].reshape(k, n_heads * dim)


def _layer_weights(w, i):
    segs = jnp.split(w['w_in'][i], list(_cumsum(IN_SIZES))[:-1], axis=1)
    a_q, a_k, a_v, a_gate, b_qd, b_kvd, b_kr, b_gate, m_a, m_b = segs
    kr = jnp.pad(b_kr, ((0, 0), (HEAD_DIM, LANES - MLA_QK)))
    w_in = jnp.concatenate([
        m_a, m_b, _pad_heads_cols(a_q, HEADS, HEAD_DIM), _pad_heads_cols(a_gate, HEADS, HEAD_DIM),
        _pad_heads_cols(b_gate, HEADS, HEAD_DIM), b_qd, _pad_heads_cols(a_k, SWA_KV_HEADS, HEAD_DIM),
        _pad_heads_cols(a_v, SWA_KV_HEADS, HEAD_DIM), b_kvd, kr], axis=1)
    w_uq = _pad_heads_cols(w['w_uq'][i], HEADS, MLA_QK)
    ukv = w['w_ukv'][i].reshape(MLA_KV_LORA, HEADS, 2 * HEAD_DIM)
    pad = ((0, 0), (0, 0), (0, HEAD_DIM))
    w_ukv = jnp.concatenate([jnp.pad(ukv[:, :, :HEAD_DIM], pad).reshape(MLA_KV_LORA, HPAD),
                             jnp.pad(ukv[:, :, HEAD_DIM:], pad).reshape(MLA_KV_LORA, HPAD)], axis=1)
    w_br_a = _pad_heads_cols(w['w_br_a'][i].T, HEADS, HEAD_DIM).T
    w_br_b = _pad_heads_cols(w['w_br_b'][i].T, HEADS, HEAD_DIM).T
    out = dict(w_in=w_in, w_uq=w_uq, w_ukv=w_ukv, w_br_a=w_br_a, w_br_b=w_br_b, w_out=w['w_out'][i],
               w_pg=w['w_ple_gate'][i], w_pp=w['w_ple_proj'][i])
    for name in ('w_in', 'w_uq', 'w_ukv', 'w_br_a', 'w_br_b', 'w_out', 'w_pg'):
        out[name + '_t'] = out[name].T
    return out


def _cumsum(sizes):
    acc, out = 0, []
    for s in sizes:
        acc += s
        out.append(acc)
    return out


def _unpad_grads(g):
    d = g['w_in']
    seg = lambda off, width: d[:, off:off + width]
    b_kr = seg(Z_BKR, LANES)[:, HEAD_DIM:MLA_QK]
    w_in = jnp.concatenate([
        _unpad_heads_cols(seg(Z_AQ, HPAD), HEADS, HEAD_DIM), _unpad_heads_cols(seg(Z_AK, 256), SWA_KV_HEADS, HEAD_DIM),
        _unpad_heads_cols(seg(Z_AV, 256), SWA_KV_HEADS, HEAD_DIM), _unpad_heads_cols(seg(Z_AGATE, HPAD), HEADS, HEAD_DIM),
        seg(Z_BQD, MLA_Q_LORA), seg(Z_BKVD, MLA_KV_LORA), b_kr, _unpad_heads_cols(seg(Z_BGATE, HPAD), HEADS, HEAD_DIM),
        seg(Z_MA, D_MODEL), seg(Z_MB, D_MODEL)], axis=1)
    w_uq = _unpad_heads_cols(g['w_uq'], HEADS, MLA_QK)
    ukv = g['w_ukv'].reshape(MLA_KV_LORA, 2, HEADS, LANES)[:, :, :, :HEAD_DIM]
    w_ukv = jnp.concatenate([ukv[:, 0], ukv[:, 1]], axis=-1).reshape(MLA_KV_LORA, HEADS * 2 * HEAD_DIM)
    w_br_a = _unpad_heads_cols(g['w_br_a'].T, HEADS, HEAD_DIM).T
    w_br_b = _unpad_heads_cols(g['w_br_b'].T, HEADS, HEAD_DIM).T
    return dict(w_in=w_in, w_uq=w_uq, w_ukv=w_ukv, w_br_a=w_br_a, w_br_b=w_br_b, w_out=g['w_out'],
                w_ple_gate=g['w_pg'], w_ple_proj=g['w_pp'], g_mix=g['g_mix'], sink=g['sink'], g_q=g['g_q'],
                g_kv=g['g_kv'], g_ple=g['g_ple'])


def _layer_fwd(x0, p_i, lw, sm, i, pos_col, pos_row, tabs, B, S):
    T = B * S
    h = _rms_fwd("norm_mix", x0, D_MODEL, 0, sm['g_mix'][i], T)
    z = _mm("proj_in", h, lw['w_in'], BF16)
    sink_row = jnp.pad(sm['sink'][i], (0, LANES - HEADS)).reshape(1, LANES)
    oa_raw, oa, lse_a = _swa_fwd(z, pos_col, pos_row, sink_row, B, S)
    qdn = _rms_fwd("norm_q", z, MLA_Q_LORA, Z_BQD // MLA_Q_LORA, sm['g_q'][i], T)
    kvdn = _rms_fwd("norm_kv", z, MLA_KV_LORA, Z_BKVD // MLA_KV_LORA, sm['g_kv'][i], T)
    q_pre = _mm("proj_uq", qdn, lw['w_uq'], F32)
    kv_pre = _mm("proj_ukv", kvdn, lw['w_ukv'], F32)
    qf, kf, vf = _mla_prep(q_pre, kv_pre, z, tabs, T)
    ob_raw, ob, lse_b = _mla_fwd(qf, kf, vf, z, B, S)
    ua = _mm("proj_br_a", oa, lw['w_br_a'], BF16)
    ub = _mm("proj_br_b", ob, lw['w_br_b'], BF16)
    y = _merge_fwd(ua, ub, z, T)
    x1 = _mm("proj_out", y, lw['w_out'], F32, residual=x0)
    hn = _rms_fwd("norm_ple", x1, D_MODEL, 0, sm['g_ple'][i], T)
    u = _mm("proj_pg", hn, lw['w_pg'], F32)
    e = _mm("proj_pp", p_i, lw['w_pp'], F32)
    x2 = _ple_fwd(x1, u, e, T)
    saved = dict(x0=x0, h=h, z=z, sink_row=sink_row, oa_raw=oa_raw, oa=oa, lse_a=lse_a, qdn=qdn, kvdn=kvdn,
                 qf=qf, kf=kf, vf=vf, ob_raw=ob_raw, ob=ob, lse_b=lse_b, ua=ua, ub=ub, y=y, x1=x1, hn=hn,
                 u=u, e=e, p=p_i)
    return x2, saved


def _layer_bwd(dx2, sv, lw, sm, i, pos_col, pos_row, tabs, B, S):
    T = B * S
    z = sv['z']
    g = {}
    d_e, d_u = _ple_bwd(dx2, sv['u'], sv['e'], T)
    g['w_pp'] = _mm_tn("grad_pp", sv['p'], d_e)
    g['w_pg'] = _mm_tn("grad_pg", sv['hn'], d_u)
    dhn = _mm("back_pg", d_u, lw['w_pg_t'], F32)
    dx1, g['g_ple'] = _rms_bwd("norm_ple_bwd", sv['x1'], D_MODEL, 0, sm['g_ple'][i], dhn, T, F32, dres=dx2)
    g['w_out'] = _mm_tn("grad_out", sv['y'], dx1)
    dy = _mm("back_out", dx1, lw['w_out_t'], F32)
    d_ua, d_ub, d_ma, d_mb = _merge_bwd(dy, sv['ua'], sv['ub'], z, T)
    g['w_br_a'] = _mm_tn("grad_br_a", sv['oa'], d_ua)
    g['w_br_b'] = _mm_tn("grad_br_b", sv['ob'], d_ub)
    d_oa = _mm("back_br_a", d_ua, lw['w_br_a_t'], F32)
    d_ob = _mm("back_br_b", d_ub, lw['w_br_b_t'], F32)
    dob_raw, d_bgate, delta_b = _gate_bwd("gate_b_bwd", d_ob, sv['ob_raw'], z, Z_BGATE // HPAD, T)
    dq, dk, dv = _mla_bwd(sv['qf'], sv['kf'], sv['vf'], dob_raw, sv['lse_b'], delta_b, B, S)
    dq_pre, dkv_pre, d_bkr = _mla_prep_bwd(dq, dk, dv, tabs, T)
    g['w_uq'] = _mm_tn("grad_uq", sv['qdn'], dq_pre)
    g['w_ukv'] = _mm_tn("grad_ukv", sv['kvdn'], dkv_pre)
    dqdn = _mm("back_uq", dq_pre, lw['w_uq_t'], F32)
    dkvdn = _mm("back_ukv", dkv_pre, lw['w_ukv_t'], F32)
    d_bqd, g['g_q'] = _rms_bwd("norm_q_bwd", z, MLA_Q_LORA, Z_BQD // MLA_Q_LORA, sm['g_q'][i], dqdn, T, BF16)
    d_bkvd, g['g_kv'] = _rms_bwd("norm_kv_bwd", z, MLA_KV_LORA, Z_BKVD // MLA_KV_LORA, sm['g_kv'][i], dkvdn, T, BF16)
    doa_raw, d_agate, delta_a = _gate_bwd("gate_a_bwd", d_oa, sv['oa_raw'], z, Z_AGATE // HPAD, T)
    d_aq, d_ak, d_av, dsink = _swa_bwd(z, pos_col, pos_row, sv['sink_row'], sv['lse_a'], doa_raw, delta_a, B, S)
    g['sink'] = dsink[0, :HEADS]
    dz = jnp.concatenate([d_ma, d_mb, d_aq, d_agate, d_bgate, d_bqd, d_ak.astype(BF16), d_av.astype(BF16),
                          d_bkvd, d_bkr], axis=1)
    g['w_in'] = _mm_tn("grad_in", sv['h'], dz)
    dh = _mm("back_in", dz, lw['w_in_t'], F32)
    dx0, g['g_mix'] = _rms_bwd("norm_mix_bwd", sv['x0'], D_MODEL, 0, sm['g_mix'][i], dh, T, F32, dres=dx1)
    for name in ('g_ple', 'g_q', 'g_kv', 'g_mix'):
        g[name] = g[name][0]
    return dx0, g


def _local_step(x, p, positions, wfull, sm, loss_target):
    B, S, _ = x.shape
    T = B * S
    pos_col = positions.reshape(T, 1)
    pos_row = positions.reshape(T // BLOCK, 1, BLOCK)
    half = MLA_ROPE // 2
    inv = ROPE_THETA ** (-jnp.arange(0, MLA_ROPE, 2, dtype=F32) / MLA_ROPE)
    inv_lane = jnp.tile(inv, LANES // half).reshape(1, LANES)
    tabs = _rope_tables(pos_col, inv_lane, T)
    xc = x.reshape(T, D_MODEL)
    lws, saved = [], []
    for i in range(DEPTH):
        lw = _layer_weights(wfull, i)
        xc, sv = _layer_fwd(xc, p[i].reshape(T, PLE_DIM), lw, sm, i, pos_col, pos_row, tabs, B, S)
        lws.append(lw)
        saved.append(sv)
    dx, loss, dg_final = _loss_head(xc, sm['g_final'], loss_target.reshape(T, D_MODEL), T)
    layer_grads = [None] * DEPTH
    for i in reversed(range(DEPTH)):
        dx, g = _layer_bwd(dx, saved[i], lws[i], sm, i, pos_col, pos_row, tabs, B, S)
        layer_grads[i] = _unpad_grads(g)
    return loss, dx.reshape(B, S, D_MODEL), layer_grads, dg_final[0]


SMALL_ROWS = 48


def _pack_small(arrs):
    flat = jnp.concatenate([arrs[name].reshape(-1) for name in SMALL])
    return jnp.pad(flat, (0, SMALL_ROWS * LANES - flat.shape[0])).reshape(SMALL_ROWS, LANES)


def _unpack_small(block, shapes):
    flat = block.reshape(-1)
    out, off = {}, 0
    for name in SMALL:
        n = math.prod(shapes[name])
        out[name] = flat[off:off + n].reshape(shapes[name])
        off += n
    return out


def _to_slots(g, axis):
    r, c = g.shape
    if axis == 0:
        return g.reshape(N_CHIPS, r // N_CHIPS, c)
    return g.reshape(r, N_CHIPS, c // N_CHIPS).transpose(1, 0, 2)


def _units(shapes):
    units = []
    for w, shape in enumerate(shapes):
        r = shape[-2]
        n = 4 if r >= 1024 else 1
        units += [(w, k * (r // n), r // n) for k in range(n)]
    return units


def _place():
    x, y, c = lax.axis_index("x"), lax.axis_index("y"), lax.axis_index("c")
    chips = [(1 - x, y), (x, 1 - y), (1 - x, 1 - y)]
    return x, y, c, chips


ANY = pl.BlockSpec(memory_space=pl.ANY)


def _remote(send_sems, recv_sems, k, src, dst, to):
    return pltpu.make_async_remote_copy(src_ref=src, dst_ref=dst, send_sem=send_sems.at[k],
                                        recv_sem=recv_sems.at[k], device_id=to, device_id_type=MESH)


def _gather_weights(shards):
    n = len(shards)
    units = _units([s.shape for s in shards])
    nu = len(units)

    def body(*refs):
        ins, outs = refs[:n], refs[n:2 * n]
        send_sems, recv_sems, local_sems = refs[2 * n:]
        x, y, c, chips = _place()
        me = 2 * x + y
        sibling = (x, y, 1 - c)
        copy = functools.partial(_remote, send_sems, recv_sems)
        keeps, sends = [], []
        for u, (w, r0, nr) in enumerate(units):
            rows = pl.ds(r0, nr)
            keeps.append(pltpu.make_async_copy(ins[w].at[:, rows, :], outs[w].at[me, :, rows, :], local_sems.at[u]))
            keeps[-1].start()
        for j, (cx, cy) in enumerate(chips):
            for u, (w, r0, nr) in enumerate(units):
                rows = pl.ds(r0, nr)
                sends.append(copy(j * nu + u, ins[w].at[c, rows, :], outs[w].at[me, c, rows, :], (cx, cy, c)))
                sends[-1].start()
        for j, (cx, cy) in enumerate(chips):
            for u, (w, r0, nr) in enumerate(units):
                landed = outs[w].at[2 * cx + cy, c, pl.ds(r0, nr), :]
                copy(j * nu + u, landed, landed, (cx, cy, c)).wait_recv()
                sends.append(copy((3 + j) * nu + u, landed, landed, sibling))
                sends[-1].start()
        for j, (cx, cy) in enumerate(chips):
            for u, (w, r0, nr) in enumerate(units):
                other = outs[w].at[2 * cx + cy, 1 - c, pl.ds(r0, nr), :]
                copy((3 + j) * nu + u, other, other, sibling).wait_recv()
        for cp in sends:
            cp.wait_send()
        for keep in keeps:
            keep.wait()

    return pl.pallas_call(
        body, name="gather_weights",
        out_shape=[jax.ShapeDtypeStruct((N_CHIPS,) + s.shape, s.dtype) for s in shards],
        in_specs=[ANY] * n, out_specs=[ANY] * n,
        scratch_shapes=[pltpu.SemaphoreType.DMA((6 * nu,)), pltpu.SemaphoreType.DMA((6 * nu,)),
                        pltpu.SemaphoreType.DMA((nu,))])(*shards)


def _pair_exchange(g0, g1):
    n = len(g0)

    def body(*refs):
        layers, outs = (refs[:n], refs[n:2 * n]), refs[2 * n:3 * n]
        send_sems, recv_sems = refs[3 * n:]
        x, y, c, _ = _place()
        copy = functools.partial(_remote, send_sems, recv_sems)
        for w in range(n):
            for q in range(N_CHIPS):
                for layer in range(DEPTH):
                    cp = copy(N_CHIPS * w + q, layers[layer][w].at[q], outs[w].at[q], (x, y, 1 - c))
                    pl.when(c == 1 - layer)(cp.start)
        for w in range(n):
            for q in range(N_CHIPS):
                copy(N_CHIPS * w + q, layers[0][w].at[q], outs[w].at[q], (x, y, 1 - c)).wait()

    return pl.pallas_call(
        body, name="pair_exchange", out_shape=[jax.ShapeDtypeStruct(g.shape, g.dtype) for g in g0],
        in_specs=[ANY] * (2 * n), out_specs=[ANY] * n,
        scratch_shapes=[pltpu.SemaphoreType.DMA((N_CHIPS * n,)), pltpu.SemaphoreType.DMA((N_CHIPS * n,))])(*g0, *g1)


def _pair_sum(name, g0, g1, theirs, cflag):
    shape = theirs.shape
    rows, width = shape[0] * shape[1], shape[2]

    def body(ins, outs, _):
        mine = jnp.where(ins[3][0:1, 0:1] == 0.0, ins[0][...], ins[1][...])
        tot = mine + ins[2][...]
        outs[0][...] = tot
        outs[1][...] = tot.astype(BF16)
    ins = [(a.reshape(rows, width), width, 0) for a in (g0, g1, theirs)] + [(cflag, None, None)]
    f32, bf16 = _ew(name, body, ins, [(width, F32), (width, BF16)], rows, 256)
    return f32.reshape(shape), bf16.reshape(shape)


def _chip_exchange(parts):
    n = len(parts)

    def body(*refs):
        ins, outs = refs[:n], refs[n:2 * n]
        send_sems, recv_sems = refs[2 * n:]
        x, y, c, chips = _place()
        copy = functools.partial(_remote, send_sems, recv_sems)
        sends = []
        for j, (cx, cy) in enumerate(chips):
            for w in range(n):
                sends.append(copy(j * n + w, ins[w].at[2 * cx + cy], outs[w].at[j], (cx, cy, c)))
                sends[-1].start()
        for j, (cx, cy) in enumerate(chips):
            for w in range(n):
                copy(j * n + w, outs[w].at[j], outs[w].at[j], (cx, cy, c)).wait_recv()
        for cp in sends:
            cp.wait_send()

    return pl.pallas_call(
        body, name="chip_exchange",
        out_shape=[jax.ShapeDtypeStruct((3,) + a.shape[1:], a.dtype) for a in parts],
        in_specs=[ANY] * n, out_specs=[ANY] * n,
        scratch_shapes=[pltpu.SemaphoreType.DMA((3 * n,)), pltpu.SemaphoreType.DMA((3 * n,))])(*parts)


def _chip_sum(name, part, landed, chipflag):
    _, r, width = part.shape
    tm = min(r, 256)

    def kern(p_ref, l_ref, flag_ref, o_ref):
        me = flag_ref[0:1, 0:1]
        own = jnp.where(me == 0.0, p_ref[0], jnp.where(me == 1.0, p_ref[1], jnp.where(me == 2.0, p_ref[2], p_ref[3])))
        o_ref[...] = ((own + l_ref[0].astype(F32)) + l_ref[1].astype(F32)) + l_ref[2].astype(F32)

    return pl.pallas_call(
        kern, name=name, grid=(r // tm,),
        in_specs=[pl.BlockSpec((N_CHIPS, tm, width), lambda i: (0, i, 0)),
                  pl.BlockSpec((3, tm, width), lambda i: (0, i, 0)),
                  pl.BlockSpec((1, LANES), lambda i: (0, 0))],
        out_specs=pl.BlockSpec((tm, width), lambda i: (i, 0)),
        out_shape=jax.ShapeDtypeStruct((r, width), F32), compiler_params=_params(("arbitrary",)))(part, landed, chipflag)


def _pair_broadcast(mine):
    n = len(mine)
    units = _units([a.shape for a in mine])

    def body(*refs):
        ins, outs = refs[:n], refs[n:2 * n]
        send_sems, recv_sems = refs[2 * n:]
        x, y, c, _ = _place()
        copy = functools.partial(_remote, send_sems, recv_sems)
        cps = [copy(u, ins[w].at[pl.ds(r0, nr), :], outs[w].at[pl.ds(r0, nr), :], (x, y, 1 - c))
               for u, (w, r0, nr) in enumerate(units)]
        for cp in cps:
            cp.start()
        for cp in cps:
            cp.wait()

    return pl.pallas_call(
        body, name="pair_broadcast", out_shape=[jax.ShapeDtypeStruct(a.shape, a.dtype) for a in mine],
        in_specs=[ANY] * n, out_specs=[ANY] * n,
        scratch_shapes=[pltpu.SemaphoreType.DMA((len(units),)), pltpu.SemaphoreType.DMA((len(units),))])(*mine)


def _small_allreduce(v):
    offsets = [(dx, dy, dc) for dx in (0, 1) for dy in (0, 1) for dc in (0, 1)][1:]

    def body(v_ref, out_ref, recv_ref, send_sems, recv_sems):
        x, y, c, _ = _place()
        flip = lambda a, d: 1 - a if d else a
        peers = [(flip(x, dx), flip(y, dy), flip(c, dc)) for dx, dy, dc in offsets]
        copy = functools.partial(_remote, send_sems, recv_sems)
        me = 4 * x + 2 * y + c
        recv_ref[me] = v_ref[...]
        cps = [copy(k, v_ref, recv_ref.at[me], peer) for k, peer in enumerate(peers)]
        for cp in cps:
            cp.start()
        for k, (px, py, pc) in enumerate(peers):
            landed = recv_ref.at[4 * px + 2 * py + pc]
            copy(k, landed, landed, (px, py, pc)).wait_recv()
        for cp in cps:
            cp.wait_send()
        tot = recv_ref[0]
        for d in range(1, 8):
            tot = tot + recv_ref[d]
        out_ref[...] = tot

    vmem = pl.BlockSpec(memory_space=pltpu.VMEM)
    return pl.pallas_call(
        body, name="small_allreduce", out_shape=jax.ShapeDtypeStruct(v.shape, v.dtype),
        in_specs=[vmem], out_specs=vmem,
        scratch_shapes=[pltpu.VMEM((8,) + v.shape, v.dtype), pltpu.SemaphoreType.DMA((7,)),
                        pltpu.SemaphoreType.DMA((7,))])(v)


def _adam_math(gv, wv, mv, vv):
    mv = ADAM_B1 * mv + (1.0 - ADAM_B1) * gv
    vv = ADAM_B2 * vv + (1.0 - ADAM_B2) * (gv * gv)
    m_hat = mv / (1.0 - ADAM_B1 ** ADAM_STEP)
    v_hat = vv / (1.0 - ADAM_B2 ** ADAM_STEP)
    return -ADAM_LR * (m_hat / (jnp.sqrt(v_hat) + ADAM_EPS) + ADAM_WD * wv), mv, vv


def _adamw_big(name, mine, theirs, cflag, w, m, v):
    _, r, width = w.shape
    tm = min(r, 256)

    def kern(mine_ref, theirs_ref, flag_ref, w_ref, m_ref, v_ref, g_ref, d_ref, nm_ref, nv_ref):
        layer = pl.program_id(0).astype(F32)
        gv = jnp.where(flag_ref[0:1, 0:1] == layer, mine_ref[...], theirs_ref[...])
        g_ref[0] = gv
        d_ref[0], nm_ref[0], nv_ref[0] = _adam_math(gv, w_ref[0], m_ref[0], v_ref[0])

    flat = pl.BlockSpec((tm, width), lambda l, i: (i, 0))
    stacked = pl.BlockSpec((1, tm, width), lambda l, i: (l, i, 0))
    return pl.pallas_call(
        kern, name=name, grid=(DEPTH, r // tm),
        in_specs=[flat, flat, pl.BlockSpec((1, LANES), lambda l, i: (0, 0)), stacked, stacked, stacked],
        out_specs=[stacked] * 4, out_shape=[jax.ShapeDtypeStruct(w.shape, F32)] * 4,
        compiler_params=_params(("arbitrary", "arbitrary")))(mine, theirs, cflag, w, m, v)


def _adamw_small(g, w, m, v):
    def body(ins, outs, _):
        outs[0][...], outs[1][...], outs[2][...] = _adam_math(*(r[...] for r in ins))
    return _ew("adamw_small", body, [(a, LANES, 0) for a in (g, w, m, v)], [(LANES, F32)] * 3, SMALL_ROWS, SMALL_ROWS)


def kernel(x, p, positions, g_mix, w_in, sink, g_q, w_uq, g_kv, w_ukv, w_br_a, w_br_b, w_out, g_ple, w_ple_gate, w_ple_proj, g_final, loss_target, m_g_mix, m_w_in, m_sink, m_g_q, m_w_uq, m_g_kv, m_w_ukv, m_w_br_a, m_w_br_b, m_w_out, m_g_ple, m_w_ple_gate, m_w_ple_proj, m_g_final, v_g_mix, v_w_in, v_sink, v_g_q, v_w_uq, v_g_kv, v_w_ukv, v_w_br_a, v_w_br_b, v_w_out, v_g_ple, v_w_ple_gate, v_w_ple_proj, v_g_final):
    w = dict(g_mix=g_mix, w_in=w_in, sink=sink, g_q=g_q, w_uq=w_uq, g_kv=g_kv, w_ukv=w_ukv, w_br_a=w_br_a,
             w_br_b=w_br_b, w_out=w_out, g_ple=g_ple, w_ple_gate=w_ple_gate, w_ple_proj=w_ple_proj, g_final=g_final)
    m = dict(g_mix=m_g_mix, w_in=m_w_in, sink=m_sink, g_q=m_g_q, w_uq=m_w_uq, g_kv=m_g_kv, w_ukv=m_w_ukv,
             w_br_a=m_w_br_a, w_br_b=m_w_br_b, w_out=m_w_out, g_ple=m_g_ple, w_ple_gate=m_w_ple_gate,
             w_ple_proj=m_w_ple_proj, g_final=m_g_final)
    v = dict(g_mix=v_g_mix, w_in=v_w_in, sink=v_sink, g_q=v_g_q, w_uq=v_w_uq, g_kv=v_g_kv, w_ukv=v_w_ukv,
             w_br_a=v_w_br_a, w_br_b=v_w_br_b, w_out=v_w_out, g_ple=v_g_ple, w_ple_gate=v_w_ple_gate,
             w_ple_proj=v_w_ple_proj, g_final=v_g_final)
    wfull = _gather_full(w)
    sm = {name: w[name] for name in SMALL}
    loss_row, grad_x, layer_grads, dg_final = _local_step(x, p, positions, wfull, sm, loss_target)
    loss = lax.psum(loss_row[0, 0], ("x", "y", "c"))
    res = _update(layer_grads, dg_final, w, m, v)
    return (loss, grad_x, *[res[name][kind] for kind in range(4) for name in WEIGHT_NAMES])


def _gather_full(w):
    gathered = _gather_weights([w[name].astype(BF16) for name, _ in SHARDED])
    return {name: [jnp.concatenate([gathered[k][q, layer] for q in range(N_CHIPS)], axis=axis - 1)
                   for layer in range(DEPTH)] for k, (name, axis) in enumerate(SHARDED)}


def _update(layer_grads, dg_final, w, m, v):
    small_shapes = {name: w[name].shape for name in SMALL}
    cflag = jnp.full((1, LANES), lax.axis_index("c"), F32)
    chipflag = jnp.full((1, LANES), 2 * lax.axis_index("x") + lax.axis_index("y"), F32)

    slots = [[_to_slots(layer_grads[layer][name], axis - 1) for name, axis in SHARDED] for layer in range(DEPTH)]
    theirs = _pair_exchange(slots[0], slots[1])
    pair = [_pair_sum("pair_sum_" + name, slots[0][k], slots[1][k], theirs[k], cflag)
            for k, (name, _) in enumerate(SHARDED)]
    landed = _chip_exchange([bf16 for _, bf16 in pair])
    mine = [_chip_sum("chip_sum_" + name, pair[k][0], landed[k], chipflag) for k, (name, _) in enumerate(SHARDED)]
    other = _pair_broadcast(mine)
    res = {name: _adamw_big("adamw_" + name, mine[k], other[k], cflag, w[name], m[name], v[name])
           for k, (name, _) in enumerate(SHARDED)}

    gsmall = {name: jnp.stack([layer_grads[layer][name] for layer in range(DEPTH)]) for name in SMALL[:-1]}
    gsmall['g_final'] = dg_final
    gsum = _small_allreduce(_pack_small(gsmall))
    small = (gsum,) + tuple(_adamw_small(gsum, _pack_small(w), _pack_small(m), _pack_small(v)))
    for name, arrs in zip(SMALL, zip(*[[_unpack_small(a, small_shapes)[n] for n in SMALL] for a in small])):
        res[name] = arrs
    return res
```

```python
import functools
import math

import jax
import jax.numpy as jnp
from jax import lax
from jax.experimental import pallas as pl
from jax.experimental.pallas import tpu as pltpu

F32 = jnp.float32
BF16 = jnp.bfloat16

D_MODEL = 1024
DEPTH = 2
PLE_DIM = 256
BLOCK = 128
EPS = 1e-6
NEG = -1e30
HEADS = 8
SWA_KV_HEADS = 2
HEAD_DIM = 64
LANES = 128
HPAD = HEADS * LANES
MLA_QK = 96
MLA_ROPE = 32
MLA_Q_LORA = 256
MLA_KV_LORA = 128
ROPE_THETA = 10000.0
IN_SIZES = (512, 128, 128, 512, 256, 128, 32, 512, 1024, 1024)

Z_MA, Z_MB, Z_AQ, Z_AGATE, Z_BGATE = 0, 1024, 2048, 3072, 4096
Z_BQD, Z_AK, Z_AV, Z_BKVD, Z_BKR = 5120, 5376, 5632, 5888, 6016
Z_WIDTH = 6144

ADAM_LR, ADAM_B1, ADAM_B2, ADAM_EPS, ADAM_WD, ADAM_STEP = 0.001, 0.9, 0.999, 1e-08, 0.01, 10

VMEM_LIMIT = 56 * 1024 * 1024
MESH = pl.DeviceIdType.MESH

WEIGHT_NAMES = ('g_mix', 'w_in', 'sink', 'g_q', 'w_uq', 'g_kv', 'w_ukv', 'w_br_a', 'w_br_b',
                'w_out', 'g_ple', 'w_ple_gate', 'w_ple_proj', 'g_final')
SHARDED = (('w_in', 2), ('w_uq', 2), ('w_ukv', 2), ('w_br_a', 2), ('w_br_b', 2),
           ('w_out', 1), ('w_ple_gate', 1), ('w_ple_proj', 2))
SMALL = ('g_mix', 'sink', 'g_q', 'g_kv', 'g_ple', 'g_final')
N_CHIPS = 4


def _params(sem):
    return pltpu.CompilerParams(dimension_semantics=sem, vmem_limit_bytes=VMEM_LIMIT)


MM_TN = 512
ROW_TILE = 512
BIG_WEIGHT_BYTES = 8 * 1024 * 1024


def _row_tile(rows, weight_bytes=0):
    tm = ROW_TILE // 2 if weight_bytes > BIG_WEIGHT_BYTES else ROW_TILE
    return min(tm, rows)


def _ew(name, body, ins, outs, rows, accs=(), mms=(), tm=None):
    n_mm, n_in, n_out = len(mms), len(ins), len(outs)
    if tm is None:
        tm = _row_tile(rows, sum(b.size * b.dtype.itemsize for _, b in mms))
    in_specs, args = [], []
    for a, b in mms:
        in_specs += [pl.BlockSpec((tm, a.shape[1]), lambda i: (i, 0)), pl.BlockSpec(b.shape, lambda i: (0, 0))]
        args += [a, b]
    for arr, width, cb in ins:
        if width is None:
            in_specs.append(pl.BlockSpec(arr.shape, lambda i, nd=arr.ndim: (0,) * nd))
        else:
            in_specs.append(pl.BlockSpec((tm, width), lambda i, cb=cb: (i, cb)))
        args.append(arr)
    out_shape = [jax.ShapeDtypeStruct((rows, w), dt) for w, dt in outs]
    out_shape += [jax.ShapeDtypeStruct(s, F32) for s in accs]
    out_specs = [pl.BlockSpec((tm, w), lambda i: (i, 0)) for w, _ in outs]
    out_specs += [pl.BlockSpec(s, lambda i: (0, 0)) for s in accs]

    def kern(*refs):
        mm_refs, refs = refs[:2 * n_mm], refs[2 * n_mm:]
        in_refs, out_refs = refs[:n_in], refs[n_in:n_in + n_out]
        acc_refs, prod_refs = refs[n_in + n_out:n_in + n_out + len(accs)], refs[n_in + n_out + len(accs):]
        if acc_refs:
            @pl.when(pl.program_id(0) == 0)
            def _():
                for r in acc_refs:
                    r[...] = jnp.zeros_like(r)
        for k in range(n_mm):
            a_ref, b_ref, prod = mm_refs[2 * k], mm_refs[2 * k + 1], prod_refs[k]
            av = a_ref[...].astype(BF16)
            n = b_ref.shape[1]
            tn = min(MM_TN, n)
            for j in range(n // tn):
                cols = slice(j * tn, (j + 1) * tn)
                prod[:, cols] = jnp.dot(av, b_ref[:, cols], preferred_element_type=F32)
        body(tuple(prod_refs) + tuple(in_refs), out_refs, acc_refs)

    scratch = [pltpu.VMEM((tm, b.shape[1]), F32) for _, b in mms]
    res = pl.pallas_call(kern, name=name, grid=(rows // tm,), in_specs=in_specs, out_specs=out_specs,
                         out_shape=out_shape, scratch_shapes=scratch,
                         compiler_params=_params(("arbitrary",)))(*args)
    return res


def _rms_fwd(name, x, width, cb, g, rows):
    def body(ins, outs, _):
        xv = ins[0][...].astype(F32)
        r = lax.rsqrt(jnp.mean(xv * xv, axis=-1, keepdims=True) + EPS)
        outs[0][...] = ((xv * r) * ins[1][...]).astype(BF16)
    return _ew(name, body, [(x, width, cb), (g.reshape(1, width), None, None)], [(width, BF16)], rows)[0]


def _rms_bwd(name, x, width, cb, g, dh_mm, rows, out_dtype, dres=None):
    def body(ins, outs, accs):
        dhv, xv, gv = ins[0][...], ins[1][...].astype(F32), ins[2][...]
        r = lax.rsqrt(jnp.mean(xv * xv, axis=-1, keepdims=True) + EPS)
        xhat = xv * r
        accs[0][...] += jnp.sum(dhv * xhat, axis=0, keepdims=True)
        dy = dhv * gv
        dx = r * (dy - xhat * jnp.mean(dy * xhat, axis=-1, keepdims=True))
        if dres is not None:
            dx = dx + ins[3][...]
        outs[0][...] = dx.astype(out_dtype)
    ins = [(x, width, cb), (g.reshape(1, width), None, None)]
    if dres is not None:
        ins.append((dres, width, 0))
    return _ew(name, body, ins, [(width, out_dtype)], rows, accs=[(1, width)], mms=[dh_mm])


def _mm(name, a, b, out_dtype, residual=None, tn=MM_TN):
    M, K = a.shape
    N = b.shape[1]
    tm, tn = _row_tile(M, b.size * b.dtype.itemsize), min(tn, N)
    has_res = residual is not None

    def kern(*refs):
        a_ref, b_ref, o_ref = refs[0], refs[1], refs[-1]
        av = a_ref[...].astype(BF16)
        for j in range(N // tn):
            cols = slice(j * tn, (j + 1) * tn)
            part = jnp.dot(av, b_ref[:, cols], preferred_element_type=F32)
            if has_res:
                part = part + refs[2][:, cols]
            o_ref[:, cols] = part.astype(o_ref.dtype)

    in_specs = [pl.BlockSpec((tm, K), lambda i: (i, 0)), pl.BlockSpec((K, N), lambda i: (0, 0))]
    args = [a, b]
    if has_res:
        in_specs.append(pl.BlockSpec((tm, N), lambda i: (i, 0)))
        args.append(residual)
    return pl.pallas_call(
        kern, name=name, grid=(M // tm,), in_specs=in_specs, out_specs=pl.BlockSpec((tm, N), lambda i: (i, 0)),
        out_shape=jax.ShapeDtypeStruct((M, N), out_dtype), compiler_params=_params(("parallel",)))(*args)


def _mm_tn(name, a, b, tk=512, tn=2048):
    T, M = a.shape
    N = b.shape[1]
    tn, tk = min(tn, N), min(tk, T)

    def kern(a_ref, b_ref, o_ref):
        k = pl.program_id(1)
        part = _dot_tn(a_ref[...].astype(BF16), b_ref[...].astype(BF16))

        @pl.when(k == 0)
        def _():
            o_ref[...] = part

        @pl.when(k > 0)
        def _():
            o_ref[...] += part

    return pl.pallas_call(
        kern, name=name, grid=(N // tn, T // tk),
        in_specs=[pl.BlockSpec((tk, M), lambda j, k: (k, 0)), pl.BlockSpec((tk, tn), lambda j, k: (k, j))],
        out_specs=pl.BlockSpec((M, tn), lambda j, k: (0, j)),
        out_shape=jax.ShapeDtypeStruct((M, N), F32),
        compiler_params=_params(("parallel", "arbitrary")))(a, b)


def _dot_nt(a, b):
    return lax.dot_general(a, b, (((1,), (1,)), ((), ())), preferred_element_type=F32)


def _dot_tn(a, b):
    return lax.dot_general(a, b, (((0,), (0,)), ((), ())), preferred_element_type=F32)


SWA_SCALE = HEAD_DIM ** -0.5


def _swa_band(n, pq_ref, pkp_ref, pkc_ref):
    posk = jnp.concatenate([pkp_ref[0], pkc_ref[0]], axis=1)
    dist = (pq_ref[...] - posk).astype(F32)
    qi = lax.broadcasted_iota(jnp.int32, (BLOCK, 2 * BLOCK), 0)
    kj = lax.broadcasted_iota(jnp.int32, (BLOCK, 2 * BLOCK), 1)
    t_abs = n * BLOCK + qi
    s_abs = n * BLOCK - BLOCK + kj
    return dist, (s_abs >= 0) & (s_abs <= t_abs) & (t_abs - s_abs < BLOCK)


SWA_GROUP = HEADS // SWA_KV_HEADS


def _swa_group_q(q_all, g):
    heads = range(g * SWA_GROUP, (g + 1) * SWA_GROUP)
    return jnp.concatenate([(q_all[:, h * LANES:(h + 1) * LANES] * SWA_SCALE).astype(BF16) for h in heads], axis=0)


def _swa_mask(s, dist, valid, h):
    return jnp.where(valid, s - (2.0 ** -(h + 1)) * dist, NEG)


def _swa_specs(nb):
    prev = lambda b, n: b * nb + jnp.maximum(n - 1, 0)
    own = lambda b, n: b * nb + n
    return [
        pl.BlockSpec((BLOCK, HPAD), lambda b, n: (own(b, n), Z_AQ // HPAD)),
        pl.BlockSpec((BLOCK, 256), lambda b, n: (prev(b, n), Z_AK // 256)),
        pl.BlockSpec((BLOCK, 256), lambda b, n: (own(b, n), Z_AK // 256)),
        pl.BlockSpec((BLOCK, 256), lambda b, n: (prev(b, n), Z_AV // 256)),
        pl.BlockSpec((BLOCK, 256), lambda b, n: (own(b, n), Z_AV // 256)),
        pl.BlockSpec((BLOCK, 1), lambda b, n: (own(b, n), 0)),
        pl.BlockSpec((1, 1, BLOCK), lambda b, n: (prev(b, n), 0, 0)),
        pl.BlockSpec((1, 1, BLOCK), lambda b, n: (own(b, n), 0, 0)),
    ]


def _swa_fwd(z, pos_col, pos_row, sink_row, B, S):
    nb = S // BLOCK
    T = B * S

    def kern(q_ref, kp_ref, kc_ref, vp_ref, vc_ref, pq_ref, pkp_ref, pkc_ref, gate_ref, sink_ref,
             oraw_ref, og_ref, lse_ref):
        q_all = q_ref[...]
        kb = jnp.concatenate([kp_ref[...], kc_ref[...]], axis=0).astype(BF16)
        vb = jnp.concatenate([vp_ref[...], vc_ref[...]], axis=0).astype(BF16)
        dist, valid = _swa_band(pl.program_id(1), pq_ref, pkp_ref, pkc_ref)
        lane = lax.broadcasted_iota(jnp.int32, (BLOCK, LANES), 1)
        lse_all = jnp.zeros((BLOCK, LANES), F32)
        for grp in range(SWA_KV_HEADS):
            gcols = slice(grp * LANES, (grp + 1) * LANES)
            s_all = _dot_nt(_swa_group_q(q_all, grp), kb[:, gcols])
            probs = []
            for hh in range(SWA_GROUP):
                h = grp * SWA_GROUP + hh
                s = _swa_mask(s_all[hh * BLOCK:(hh + 1) * BLOCK], dist, valid, h)
                sink_h = sink_ref[0:1, h:h + 1]
                m = jnp.maximum(jnp.max(s, axis=-1, keepdims=True), sink_h)
                e = jnp.exp(s - m)
                denom = jnp.sum(e, axis=-1, keepdims=True) + jnp.exp(sink_h - m)
                probs.append((e * (1.0 / denom)).astype(BF16))
                lse_all = jnp.where(lane == h, m + jnp.log(denom), lse_all)
            o_all = jnp.dot(jnp.concatenate(probs, axis=0), vb[:, gcols], preferred_element_type=F32)
            for hh in range(SWA_GROUP):
                cols = slice((grp * SWA_GROUP + hh) * LANES, (grp * SWA_GROUP + hh + 1) * LANES)
                o = o_all[hh * BLOCK:(hh + 1) * BLOCK]
                oraw_ref[:, cols] = o
                g = gate_ref[:, cols].astype(F32)
                og_ref[:, cols] = (o * (g * jax.nn.sigmoid(g))).astype(BF16)
        lse_ref[...] = lse_all

    own = lambda b, n: b * nb + n
    in_specs = _swa_specs(nb) + [
        pl.BlockSpec((BLOCK, HPAD), lambda b, n: (own(b, n), Z_AGATE // HPAD)),
        pl.BlockSpec((1, LANES), lambda b, n: (0, 0)),
    ]
    out_specs = [pl.BlockSpec((BLOCK, HPAD), lambda b, n: (own(b, n), 0)),
                 pl.BlockSpec((BLOCK, HPAD), lambda b, n: (own(b, n), 0)),
                 pl.BlockSpec((BLOCK, LANES), lambda b, n: (own(b, n), 0))]
    out_shape = [jax.ShapeDtypeStruct((T, HPAD), F32), jax.ShapeDtypeStruct((T, HPAD), BF16),
                 jax.ShapeDtypeStruct((T, LANES), F32)]
    return pl.pallas_call(kern, name="swa_fwd", grid=(B, nb), in_specs=in_specs, out_specs=out_specs,
                          out_shape=out_shape, compiler_params=_params(("parallel", "arbitrary")))(
        z, z, z, z, z, pos_col, pos_row, pos_row, z, sink_row)


def _swa_bwd(z, pos_col, pos_row, sink_row, lse, do_raw, delta, B, S):
    nb = S // BLOCK
    T = B * S

    def kern(q_ref, kp_ref, kc_ref, vp_ref, vc_ref, pq_ref, pkp_ref, pkc_ref, sink_ref, lse_ref, do_ref,
             delta_ref, dq_ref, dk_ref, dv_ref, dsink_ref):
        b, n = pl.program_id(0), pl.program_id(1)

        @pl.when(n == 0)
        def _():
            dk_ref[...] = jnp.zeros_like(dk_ref)
            dv_ref[...] = jnp.zeros_like(dv_ref)

        @pl.when((b == 0) & (n == 0))
        def _():
            dsink_ref[...] = jnp.zeros_like(dsink_ref)

        q_all = q_ref[...]
        kb = jnp.concatenate([kp_ref[...], kc_ref[...]], axis=0).astype(BF16)
        vb = jnp.concatenate([vp_ref[...], vc_ref[...]], axis=0).astype(BF16)
        dist, valid = _swa_band(n, pq_ref, pkp_ref, pkc_ref)
        lane1 = lax.broadcasted_iota(jnp.int32, (1, LANES), 1)
        dsink = jnp.zeros((1, LANES), F32)
        dk_band, dv_band = [], []
        for grp in range(SWA_KV_HEADS):
            gcols = slice(grp * LANES, (grp + 1) * LANES)
            heads = range(grp * SWA_GROUP, (grp + 1) * SWA_GROUP)
            qg = _swa_group_q(q_all, grp)
            dog = jnp.concatenate([do_ref[:, h * LANES:(h + 1) * LANES] for h in heads], axis=0)
            s_all = _dot_nt(qg, kb[:, gcols])
            dp_all = _dot_nt(dog, vb[:, gcols])
            ps, dss = [], []
            for hh, h in enumerate(heads):
                blk = slice(hh * BLOCK, (hh + 1) * BLOCK)
                lse_h = lse_ref[:, h:h + 1]
                delta_h = delta_ref[:, h * LANES:h * LANES + 1]
                p = jnp.exp(_swa_mask(s_all[blk], dist, valid, h) - lse_h)
                ps.append(p.astype(BF16))
                dss.append((p * (dp_all[blk] - delta_h)).astype(BF16))
                psink = jnp.exp(sink_ref[0:1, h:h + 1] - lse_h)
                dsink = dsink + jnp.where(lane1 == h, -jnp.sum(psink * delta_h, axis=0, keepdims=True), 0.0)
            dsg = jnp.concatenate(dss, axis=0)
            dq_all = jnp.dot(dsg, kb[:, gcols], preferred_element_type=F32) * SWA_SCALE
            for hh, h in enumerate(heads):
                dq_ref[:, h * LANES:(h + 1) * LANES] = dq_all[hh * BLOCK:(hh + 1) * BLOCK].astype(BF16)
            dk_band.append(_dot_tn(dsg, qg))
            dv_band.append(_dot_tn(jnp.concatenate(ps, axis=0), dog))
        dsink_ref[...] += dsink
        dkb = jnp.concatenate(dk_band, axis=1)
        dvb = jnp.concatenate(dv_band, axis=1)
        r_prev = pl.ds(pl.multiple_of(jnp.maximum(n - 1, 0) * BLOCK, BLOCK), BLOCK)
        r_own = pl.ds(pl.multiple_of(n * BLOCK, BLOCK), BLOCK)
        dk_ref[r_prev, :] += dkb[:BLOCK]
        dk_ref[r_own, :] += dkb[BLOCK:]
        dv_ref[r_prev, :] += dvb[:BLOCK]
        dv_ref[r_own, :] += dvb[BLOCK:]

    own = lambda b, n: b * nb + n
    in_specs = _swa_specs(nb) + [
        pl.BlockSpec((1, LANES), lambda b, n: (0, 0)),
        pl.BlockSpec((BLOCK, LANES), lambda b, n: (own(b, n), 0)),
        pl.BlockSpec((BLOCK, HPAD), lambda b, n: (own(b, n), 0)),
        pl.BlockSpec((BLOCK, HPAD), lambda b, n: (own(b, n), 0)),
    ]
    out_specs = [pl.BlockSpec((BLOCK, HPAD), lambda b, n: (own(b, n), 0)),
                 pl.BlockSpec((S, 256), lambda b, n: (b, 0)),
                 pl.BlockSpec((S, 256), lambda b, n: (b, 0)),
                 pl.BlockSpec((1, LANES), lambda b, n: (0, 0))]
    out_shape = [jax.ShapeDtypeStruct((T, HPAD), BF16), jax.ShapeDtypeStruct((T, 256), F32),
                 jax.ShapeDtypeStruct((T, 256), F32), jax.ShapeDtypeStruct((1, LANES), F32)]
    return pl.pallas_call(kern, name="swa_bwd", grid=(B, nb), in_specs=in_specs, out_specs=out_specs,
                          out_shape=out_shape, compiler_params=_params(("arbitrary", "arbitrary")))(
        z, z, z, z, z, pos_col, pos_row, pos_row, sink_row, lse, do_raw, delta)


MLA_T = 256
MLA_HG = 2
MLA_W = MLA_HG * LANES
MLA_SCALE = MLA_QK ** -0.5
LOG2E = 1.4426950408889634
MLA_QSCALE = MLA_SCALE * LOG2E


def _causal(s):
    row = lax.broadcasted_iota(jnp.int32, s.shape, 0)
    col = lax.broadcasted_iota(jnp.int32, s.shape, 1)
    return jnp.where(col <= row, s, NEG)


def _mla_fwd(q, k, v, z, B, S):
    T = B * S
    nq = S // MLA_T

    def kern(q_ref, k_ref, v_ref, gate_ref, oraw_ref, og_ref, lse_ref):
        i = pl.program_id(2)

        def scores(j):
            rows = pl.ds(pl.multiple_of(j * MLA_T, MLA_T), MLA_T)
            return tuple(_dot_nt(q_ref[:, hh * LANES:(hh + 1) * LANES], k_ref[rows, hh * LANES:(hh + 1) * LANES])
                         for hh in range(MLA_HG))

        def update(j, ss, state):
            rows = pl.ds(pl.multiple_of(j * MLA_T, MLA_T), MLA_T)
            out = []
            for hh in range(MLA_HG):
                (m, l, acc), s = state[hh], ss[hh]
                m_new = jnp.maximum(m, jnp.max(s, axis=-1, keepdims=True))
                alpha = jnp.exp2(m - m_new)
                p = jnp.exp2(s - m_new)
                l = alpha * l + jnp.sum(p, axis=-1, keepdims=True)
                pv = jnp.dot(p.astype(BF16), v_ref[rows, hh * LANES:(hh + 1) * LANES], preferred_element_type=F32)
                out.append((m_new, l, alpha * acc + pv))
            return tuple(out)

        def body(j, carry):
            state, ss = carry
            s_next = scores(j + 1)
            return update(j, ss, state), s_next

        init = tuple((jnp.full((MLA_T, 1), NEG, F32), jnp.zeros((MLA_T, 1), F32), jnp.zeros((MLA_T, LANES), F32))
                     for _ in range(MLA_HG))
        state, ss = lax.fori_loop(0, i, body, (init, scores(0)))
        state = update(i, tuple(_causal(s) for s in ss), state)
        for hh in range(MLA_HG):
            m, l, acc = state[hh]
            cols = slice(hh * LANES, (hh + 1) * LANES)
            o = acc * (1.0 / l)
            oraw_ref[:, cols] = o
            g = gate_ref[:, cols].astype(F32)
            og_ref[:, cols] = (o * (g * jax.nn.sigmoid(g))).astype(BF16)
            lse_ref[:, cols] = jnp.broadcast_to(m + jnp.log2(l), (MLA_T, LANES))

    blk = lambda b, h, i: (b * nq + i, h)
    in_specs = [pl.BlockSpec((MLA_T, MLA_W), blk),
                pl.BlockSpec((S, MLA_W), lambda b, h, i: (b, h)),
                pl.BlockSpec((S, MLA_W), lambda b, h, i: (b, h)),
                pl.BlockSpec((MLA_T, MLA_W), lambda b, h, i: (b * nq + i, Z_BGATE // MLA_W + h))]
    out_specs = [pl.BlockSpec((MLA_T, MLA_W), blk)] * 3
    out_shape = [jax.ShapeDtypeStruct((T, HPAD), F32), jax.ShapeDtypeStruct((T, HPAD), BF16),
                 jax.ShapeDtypeStruct((T, HPAD), F32)]
    return pl.pallas_call(kern, name="mla_fwd", grid=(B, HEADS // MLA_HG, nq), in_specs=in_specs,
                          out_specs=out_specs, out_shape=out_shape,
                          compiler_params=_params(("parallel", "parallel", "arbitrary")))(q, k, v, z)


def _mla_bwd(q, k, v, do_raw, lse, delta, B, S):
    T = B * S
    nk = S // MLA_T

    def kern(q_ref, k_ref, v_ref, do_ref, lse_ref, delta_ref, dq_ref, dk_ref, dv_ref, dk_acc, dv_acc):
        j = pl.program_id(2)

        @pl.when(j == 0)
        def _():
            dq_ref[...] = jnp.zeros_like(dq_ref)

        dk_acc[...] = jnp.zeros_like(dk_acc)
        dv_acc[...] = jnp.zeros_like(dv_acc)

        def step(i, masked):
            rows = pl.ds(pl.multiple_of(i * MLA_T, MLA_T), MLA_T)
            for hh in range(MLA_HG):
                cols = slice(hh * LANES, (hh + 1) * LANES)
                kv, vv = k_ref[:, cols], v_ref[:, cols]
                qv, do = q_ref[rows, cols], do_ref[rows, cols]
                s = _dot_nt(qv, kv)
                if masked:
                    s = _causal(s)
                p = jnp.exp2(s - lse_ref[rows, hh * LANES:hh * LANES + 1])
                dp = _dot_nt(do, vv)
                ds = (p * (dp - delta_ref[rows, hh * LANES:hh * LANES + 1])).astype(BF16)
                dv_acc[:, cols] += _dot_tn(p.astype(BF16), do)
                dk_acc[:, cols] += _dot_tn(ds, qv)
                dq_ref[rows, cols] += jnp.dot(ds, kv, preferred_element_type=F32)

        step(j, True)

        def body(i, c):
            step(i, False)
            return c

        lax.fori_loop(j + 1, nk, body, 0)
        dk_ref[...] = dk_acc[...] * (1.0 / LOG2E)
        dv_ref[...] = dv_acc[...]

    whole = lambda b, h, j: (b, h)
    tile = lambda b, h, j: (b * nk + j, h)
    in_specs = [pl.BlockSpec((S, MLA_W), whole), pl.BlockSpec((MLA_T, MLA_W), tile),
                pl.BlockSpec((MLA_T, MLA_W), tile), pl.BlockSpec((S, MLA_W), whole),
                pl.BlockSpec((S, MLA_W), whole), pl.BlockSpec((S, MLA_W), whole)]
    out_specs = [pl.BlockSpec((S, MLA_W), whole), pl.BlockSpec((MLA_T, MLA_W), tile),
                 pl.BlockSpec((MLA_T, MLA_W), tile)]
    out_shape = [jax.ShapeDtypeStruct((T, HPAD), F32)] * 3
    return pl.pallas_call(kern, name="mla_bwd", grid=(B, HEADS // MLA_HG, nk), in_specs=in_specs,
                          out_specs=out_specs, out_shape=out_shape,
                          scratch_shapes=[pltpu.VMEM((MLA_T, MLA_W), F32), pltpu.VMEM((MLA_T, MLA_W), F32)],
                          compiler_params=_params(("parallel", "parallel", "arbitrary")))(
        q, k, v, do_raw, lse, delta)


def _rope_tables(pos_col, inv_lane, rows):
    def body(ins, outs, _):
        ang = ins[0][...].astype(F32) * ins[1][...]
        lane = lax.broadcasted_iota(jnp.int32, ang.shape, 1)
        cos, sin = jnp.cos(ang), jnp.sin(ang)
        first = (lane >= HEAD_DIM) & (lane < HEAD_DIM + MLA_ROPE // 2)
        second = (lane >= HEAD_DIM + MLA_ROPE // 2) & (lane < MLA_QK)
        outs[0][...] = jnp.where(lane < HEAD_DIM, 1.0, jnp.where(lane < MLA_QK, cos, 0.0))
        outs[1][...] = jnp.where(first, -sin, 0.0)
        outs[2][...] = jnp.where(second, sin, 0.0)
    return _ew("rope_tables", body, [(pos_col, 1, 0), (inv_lane, None, None)], [(LANES, F32)] * 3, rows)


def _rope(x, c, s1, s2):
    return x * c + pltpu.roll(x, 112, 1) * s1 + pltpu.roll(x, 16, 1) * s2


def _rope_t(d, c, s1, s2):
    return d * c + pltpu.roll(d * s1, 16, 1) + pltpu.roll(d * s2, 112, 1)


def _mla_prep(qdn, w_uq, kvdn, w_ukv, z, tabs, rows):
    def body(ins, outs, _):
        q_pre, kv_pre = ins[0], ins[1]
        c, s1, s2 = ins[3][...], ins[4][...], ins[5][...]
        kr = _rope(ins[2][...].astype(F32), c, s1, s2)
        for h in range(HEADS):
            cols = slice(h * LANES, (h + 1) * LANES)
            outs[0][:, cols] = (_rope(q_pre[:, cols], c, s1, s2) * MLA_QSCALE).astype(BF16)
            outs[1][:, cols] = (kv_pre[:, cols] + kr).astype(BF16)
        outs[2][...] = kv_pre[:, HPAD:].astype(BF16)
    ins = [(z, LANES, Z_BKR // LANES), (tabs[0], LANES, 0), (tabs[1], LANES, 0), (tabs[2], LANES, 0)]
    return _ew("mla_prep", body, ins, [(HPAD, BF16)] * 3, rows, mms=[(qdn, w_uq), (kvdn, w_ukv)])


def _mla_prep_bwd(dq, dk, dv, tabs, rows):
    def body(ins, outs, _):
        c, s1, s2 = ins[3][...], ins[4][...], ins[5][...]
        lane = lax.broadcasted_iota(jnp.int32, c.shape, 1)
        dkr = jnp.zeros(c.shape, F32)
        for h in range(HEADS):
            cols = slice(h * LANES, (h + 1) * LANES)
            outs[0][:, cols] = _rope_t(ins[0][:, cols] * MLA_SCALE, c, s1, s2).astype(BF16)
            dkh = ins[1][:, cols]
            outs[1][:, cols] = jnp.where(lane < HEAD_DIM, dkh, 0.0).astype(BF16)
            dkr = dkr + dkh
        outs[1][:, HPAD:] = ins[2][...].astype(BF16)
        live = (lane >= HEAD_DIM) & (lane < MLA_QK)
        outs[2][...] = jnp.where(live, _rope_t(jnp.where(live, dkr, 0.0), c, s1, s2), 0.0).astype(BF16)
    ins = [(dq, HPAD, 0), (dk, HPAD, 0), (dv, HPAD, 0), (tabs[0], LANES, 0), (tabs[1], LANES, 0),
           (tabs[2], LANES, 0)]
    return _ew("mla_prep_bwd", body, ins, [(HPAD, BF16), (2 * HPAD, BF16), (LANES, BF16)], rows)


def _gate_bwd(name, d_o_mm, o_raw, z, gate_cb, rows):
    def body(ins, outs, _):
        for h in range(HEADS):
            cols = slice(h * LANES, (h + 1) * LANES)
            dog, o, g = ins[0][:, cols], ins[1][:, cols], ins[2][:, cols].astype(F32)
            sg = jax.nn.sigmoid(g)
            do = dog * (g * sg)
            outs[0][:, cols] = do.astype(BF16)
            outs[1][:, cols] = (dog * o * (sg * (1.0 + g * (1.0 - sg)))).astype(BF16)
            outs[2][:, cols] = jnp.broadcast_to(jnp.sum(do * o, axis=-1, keepdims=True), do.shape)
    ins = [(o_raw, HPAD, 0), (z, HPAD, gate_cb)]
    return _ew(name, body, ins, [(HPAD, BF16), (HPAD, BF16), (HPAD, F32)], rows, mms=[d_o_mm])


def _merge_fwd(ua, ub, z, rows):
    def body(ins, outs, _):
        ua, ub, m_a, m_b = (r[...].astype(F32) for r in ins)
        outs[0][...] = (jax.nn.sigmoid(m_a) * ua + jax.nn.sigmoid(m_b) * ub).astype(BF16)
    ins = [(ua, D_MODEL, 0), (ub, D_MODEL, 0), (z, D_MODEL, Z_MA // D_MODEL), (z, D_MODEL, Z_MB // D_MODEL)]
    return _ew("merge_fwd", body, ins, [(D_MODEL, BF16)], rows)[0]


def _merge_bwd(dy_mm, ua, ub, z, rows):
    def body(ins, outs, _):
        dyv = ins[0][...]
        for idx in range(2):
            s = jax.nn.sigmoid(ins[3 + idx][...].astype(F32))
            outs[idx][...] = (dyv * s).astype(BF16)
            outs[2 + idx][...] = (dyv * ins[1 + idx][...].astype(F32) * (s * (1.0 - s))).astype(BF16)
    ins = [(ua, D_MODEL, 0), (ub, D_MODEL, 0), (z, D_MODEL, Z_MA // D_MODEL), (z, D_MODEL, Z_MB // D_MODEL)]
    return _ew("merge_bwd", body, ins, [(D_MODEL, BF16)] * 4, rows, mms=[dy_mm])


def _ple_fwd(x1, hn, w_pg, p, w_pp, rows):
    def body(ins, outs, _):
        u, e = ins[0][...], ins[1][...]
        outs[0][...] = ins[2][...] + jax.nn.sigmoid(u) * e
        outs[1][...] = u.astype(BF16)
        outs[2][...] = e.astype(BF16)
    return _ew("ple_fwd", body, [(x1, D_MODEL, 0)], [(D_MODEL, F32), (D_MODEL, BF16), (D_MODEL, BF16)], rows,
               mms=[(hn, w_pg), (p, w_pp)])


def _ple_bwd(dx2, u, e, rows):
    def body(ins, outs, _):
        d, s = ins[0][...], jax.nn.sigmoid(ins[1][...].astype(F32))
        outs[0][...] = (d * s).astype(BF16)
        outs[1][...] = (d * ins[2][...].astype(F32) * (s * (1.0 - s))).astype(BF16)
    return _ew("ple_bwd", body, [(dx2, D_MODEL, 0), (u, D_MODEL, 0), (e, D_MODEL, 0)],
               [(D_MODEL, BF16)] * 2, rows)


def _loss_head(x, g, target, rows):
    def body(ins, outs, accs):
        xv, gv = ins[0][...], ins[1][...]
        r = lax.rsqrt(jnp.mean(xv * xv, axis=-1, keepdims=True) + EPS)
        xhat = xv * r
        err = xhat * gv - ins[2][...]
        accs[0][...] += jnp.broadcast_to(0.5 * jnp.sum(jnp.mean(err * err, axis=-1, keepdims=True),
                                                       axis=0, keepdims=True), (1, LANES))
        dyv = err * (1.0 / D_MODEL)
        accs[1][...] += jnp.sum(dyv * xhat, axis=0, keepdims=True)
        dy = dyv * gv
        outs[0][...] = r * (dy - xhat * jnp.mean(dy * xhat, axis=-1, keepdims=True))
    ins = [(x, D_MODEL, 0), (g.reshape(1, D_MODEL), None, None), (target, D_MODEL, 0)]
    return _ew("loss_head", body, ins, [(D_MODEL, F32)], rows, accs=[(1, LANES), (1, D_MODEL)])


def _pad_heads_cols(w, n_heads, dim):
    k = w.shape[0]
    return jnp.pad(w.reshape(k, n_heads, dim), ((0, 0), (0, 0), (0, LANES - dim))).reshape(k, n_heads * LANES)


def _unpad_heads_cols(w, n_heads, dim):
    k = w.shape[0]
    return w.reshape(k, n_heads, LANES)[:, :, :dim].reshape(k, n_heads * dim)


def _layer_weights(w, i):
    segs = jnp.split(w['w_in'][i], list(_cumsum(IN_SIZES))[:-1], axis=1)
    a_q, a_k, a_v, a_gate, b_qd, b_kvd, b_kr, b_gate, m_a, m_b = segs
    kr = jnp.pad(b_kr, ((0, 0), (HEAD_DIM, LANES - MLA_QK)))
    w_in = jnp.concatenate([
        m_a, m_b, _pad_heads_cols(a_q, HEADS, HEAD_DIM), _pad_heads_cols(a_gate, HEADS, HEAD_DIM),
        _pad_heads_cols(b_gate, HEADS, HEAD_DIM), b_qd, _pad_heads_cols(a_k, SWA_KV_HEADS, HEAD_DIM),
        _pad_heads_cols(a_v, SWA_KV_HEADS, HEAD_DIM), b_kvd, kr], axis=1)
    w_uq = _pad_heads_cols(w['w_uq'][i], HEADS, MLA_QK)
    ukv = w['w_ukv'][i].reshape(MLA_KV_LORA, HEADS, 2 * HEAD_DIM)
    pad = ((0, 0), (0, 0), (0, HEAD_DIM))
    w_ukv = jnp.concatenate([jnp.pad(ukv[:, :, :HEAD_DIM], pad).reshape(MLA_KV_LORA, HPAD),
                             jnp.pad(ukv[:, :, HEAD_DIM:], pad).reshape(MLA_KV_LORA, HPAD)], axis=1)
    w_br_a = _pad_heads_cols(w['w_br_a'][i].T, HEADS, HEAD_DIM).T
    w_br_b = _pad_heads_cols(w['w_br_b'][i].T, HEADS, HEAD_DIM).T
    out = dict(w_in=w_in, w_uq=w_uq, w_ukv=w_ukv, w_br_a=w_br_a, w_br_b=w_br_b, w_out=w['w_out'][i],
               w_pg=w['w_ple_gate'][i], w_pp=w['w_ple_proj'][i])
    for name in ('w_in', 'w_uq', 'w_ukv', 'w_br_a', 'w_br_b', 'w_out', 'w_pg'):
        out[name + '_t'] = out[name].T
    return out


def _cumsum(sizes):
    acc, out = 0, []
    for s in sizes:
        acc += s
        out.append(acc)
    return out


def _unpad_grads(g):
    d = g['w_in']
    seg = lambda off, width: d[:, off:off + width]
    b_kr = seg(Z_BKR, LANES)[:, HEAD_DIM:MLA_QK]
    w_in = jnp.concatenate([
        _unpad_heads_cols(seg(Z_AQ, HPAD), HEADS, HEAD_DIM), _unpad_heads_cols(seg(Z_AK, 256), SWA_KV_HEADS, HEAD_DIM),
        _unpad_heads_cols(seg(Z_AV, 256), SWA_KV_HEADS, HEAD_DIM), _unpad_heads_cols(seg(Z_AGATE, HPAD), HEADS, HEAD_DIM),
        seg(Z_BQD, MLA_Q_LORA), seg(Z_BKVD, MLA_KV_LORA), b_kr, _unpad_heads_cols(seg(Z_BGATE, HPAD), HEADS, HEAD_DIM),
        seg(Z_MA, D_MODEL), seg(Z_MB, D_MODEL)], axis=1)
    w_uq = _unpad_heads_cols(g['w_uq'], HEADS, MLA_QK)
    ukv = g['w_ukv'].reshape(MLA_KV_LORA, 2, HEADS, LANES)[:, :, :, :HEAD_DIM]
    w_ukv = jnp.concatenate([ukv[:, 0], ukv[:, 1]], axis=-1).reshape(MLA_KV_LORA, HEADS * 2 * HEAD_DIM)
    w_br_a = _unpad_heads_cols(g['w_br_a'].T, HEADS, HEAD_DIM).T
    w_br_b = _unpad_heads_cols(g['w_br_b'].T, HEADS, HEAD_DIM).T
    return dict(w_in=w_in, w_uq=w_uq, w_ukv=w_ukv, w_br_a=w_br_a, w_br_b=w_br_b, w_out=g['w_out'],
                w_ple_gate=g['w_pg'], w_ple_proj=g['w_pp'], g_mix=g['g_mix'], sink=g['sink'], g_q=g['g_q'],
                g_kv=g['g_kv'], g_ple=g['g_ple'])


def _layer_fwd(x0, p_i, lw, sm, i, pos_col, pos_row, tabs, B, S):
    T = B * S
    h = _rms_fwd("norm_mix", x0, D_MODEL, 0, sm['g_mix'][i], T)
    z = _mm("proj_in", h, lw['w_in'], BF16)
    sink_row = jnp.pad(sm['sink'][i], (0, LANES - HEADS)).reshape(1, LANES)
    oa_raw, oa, lse_a = _swa_fwd(z, pos_col, pos_row, sink_row, B, S)
    qdn = _rms_fwd("norm_q", z, MLA_Q_LORA, Z_BQD // MLA_Q_LORA, sm['g_q'][i], T)
    kvdn = _rms_fwd("norm_kv", z, MLA_KV_LORA, Z_BKVD // MLA_KV_LORA, sm['g_kv'][i], T)
    qf, kf, vf = _mla_prep(qdn, lw['w_uq'], kvdn, lw['w_ukv'], z, tabs, T)
    ob_raw, ob, lse_b = _mla_fwd(qf, kf, vf, z, B, S)
    ua = _mm("proj_br_a", oa, lw['w_br_a'], BF16)
    ub = _mm("proj_br_b", ob, lw['w_br_b'], BF16)
    y = _merge_fwd(ua, ub, z, T)
    x1 = _mm("proj_out", y, lw['w_out'], F32, residual=x0)
    hn = _rms_fwd("norm_ple", x1, D_MODEL, 0, sm['g_ple'][i], T)
    x2, u, e = _ple_fwd(x1, hn, lw['w_pg'], p_i, lw['w_pp'], T)
    saved = dict(x0=x0, h=h, z=z, sink_row=sink_row, oa_raw=oa_raw, oa=oa, lse_a=lse_a, qdn=qdn, kvdn=kvdn,
                 qf=qf, kf=kf, vf=vf, ob_raw=ob_raw, ob=ob, lse_b=lse_b, ua=ua, ub=ub, y=y, x1=x1, hn=hn,
                 u=u, e=e, p=p_i)
    return x2, saved


def _layer_bwd(dx2, sv, lw, sm, i, pos_col, pos_row, tabs, B, S):
    T = B * S
    z = sv['z']
    g = {}
    d_e, d_u = _ple_bwd(dx2, sv['u'], sv['e'], T)
    g['w_pp'] = _mm_tn("grad_pp", sv['p'], d_e)
    g['w_pg'] = _mm_tn("grad_pg", sv['hn'], d_u)
    dx1, g['g_ple'] = _rms_bwd("norm_ple_bwd", sv['x1'], D_MODEL, 0, sm['g_ple'][i], (d_u, lw['w_pg_t']), T, F32,
                               dres=dx2)
    g['w_out'] = _mm_tn("grad_out", sv['y'], dx1)
    d_ua, d_ub, d_ma, d_mb = _merge_bwd((dx1, lw['w_out_t']), sv['ua'], sv['ub'], z, T)
    g['w_br_a'] = _mm_tn("grad_br_a", sv['oa'], d_ua)
    g['w_br_b'] = _mm_tn("grad_br_b", sv['ob'], d_ub)
    dob_raw, d_bgate, delta_b = _gate_bwd("gate_b_bwd", (d_ub, lw['w_br_b_t']), sv['ob_raw'], z, Z_BGATE // HPAD, T)
    dq, dk, dv = _mla_bwd(sv['qf'], sv['kf'], sv['vf'], dob_raw, sv['lse_b'], delta_b, B, S)
    dq_pre, dkv_pre, d_bkr = _mla_prep_bwd(dq, dk, dv, tabs, T)
    g['w_uq'] = _mm_tn("grad_uq", sv['qdn'], dq_pre)
    g['w_ukv'] = _mm_tn("grad_ukv", sv['kvdn'], dkv_pre)
    d_bqd, g['g_q'] = _rms_bwd("norm_q_bwd", z, MLA_Q_LORA, Z_BQD // MLA_Q_LORA, sm['g_q'][i],
                               (dq_pre, lw['w_uq_t']), T, BF16)
    d_bkvd, g['g_kv'] = _rms_bwd("norm_kv_bwd", z, MLA_KV_LORA, Z_BKVD // MLA_KV_LORA, sm['g_kv'][i],
                                 (dkv_pre, lw['w_ukv_t']), T, BF16)
    doa_raw, d_agate, delta_a = _gate_bwd("gate_a_bwd", (d_ua, lw['w_br_a_t']), sv['oa_raw'], z, Z_AGATE // HPAD, T)
    d_aq, d_ak, d_av, dsink = _swa_bwd(z, pos_col, pos_row, sv['sink_row'], sv['lse_a'], doa_raw, delta_a, B, S)
    g['sink'] = dsink[0, :HEADS]
    dz = jnp.concatenate([d_ma, d_mb, d_aq, d_agate, d_bgate, d_bqd, d_ak.astype(BF16), d_av.astype(BF16),
                          d_bkvd, d_bkr], axis=1)
    g['w_in'] = _mm_tn("grad_in", sv['h'], dz)
    dx0, g['g_mix'] = _rms_bwd("norm_mix_bwd", sv['x0'], D_MODEL, 0, sm['g_mix'][i], (dz, lw['w_in_t']), T, F32,
                               dres=dx1)
    for name in ('g_ple', 'g_q', 'g_kv', 'g_mix'):
        g[name] = g[name][0]
    return dx0, g


def _local_step(x, p, positions, wfull, sm, loss_target):
    B, S, _ = x.shape
    T = B * S
    pos_col = positions.reshape(T, 1)
    pos_row = positions.reshape(T // BLOCK, 1, BLOCK)
    half = MLA_ROPE // 2
    inv = ROPE_THETA ** (-jnp.arange(0, MLA_ROPE, 2, dtype=F32) / MLA_ROPE)
    inv_lane = jnp.tile(inv, LANES // half).reshape(1, LANES)
    tabs = _rope_tables(pos_col, inv_lane, T)
    xc = x.reshape(T, D_MODEL)
    lws, saved = [], []
    for i in range(DEPTH):
        lw = _layer_weights(wfull, i)
        xc, sv = _layer_fwd(xc, p[i].reshape(T, PLE_DIM), lw, sm, i, pos_col, pos_row, tabs, B, S)
        lws.append(lw)
        saved.append(sv)
    dx, loss, dg_final = _loss_head(xc, sm['g_final'], loss_target.reshape(T, D_MODEL), T)
    layer_grads = [None] * DEPTH
    for i in reversed(range(DEPTH)):
        dx, g = _layer_bwd(dx, saved[i], lws[i], sm, i, pos_col, pos_row, tabs, B, S)
        layer_grads[i] = _unpad_grads(g)
    return loss, dx.reshape(B, S, D_MODEL), layer_grads, dg_final[0]


SMALL_ROWS = 48


def _pack_small(arrs):
    flat = jnp.concatenate([arrs[name].reshape(-1) for name in SMALL])
    return jnp.pad(flat, (0, SMALL_ROWS * LANES - flat.shape[0])).reshape(SMALL_ROWS, LANES)


def _unpack_small(block, shapes):
    flat = block.reshape(-1)
    out, off = {}, 0
    for name in SMALL:
        n = math.prod(shapes[name])
        out[name] = flat[off:off + n].reshape(shapes[name])
        off += n
    return out


def _to_slots(g, axis):
    r, c = g.shape
    if axis == 0:
        return g.reshape(N_CHIPS, r // N_CHIPS, c)
    return g.reshape(r, N_CHIPS, c // N_CHIPS).transpose(1, 0, 2)


def _units(shapes):
    units = []
    for w, shape in enumerate(shapes):
        r = shape[-2]
        n = 4 if r >= 1024 else 1
        units += [(w, k * (r // n), r // n) for k in range(n)]
    return units


def _place():
    x, y, c = lax.axis_index("x"), lax.axis_index("y"), lax.axis_index("c")
    chips = [(1 - x, y), (x, 1 - y), (1 - x, 1 - y)]
    return x, y, c, chips


ANY = pl.BlockSpec(memory_space=pl.ANY)


def _remote(send_sems, recv_sems, k, src, dst, to):
    return pltpu.make_async_remote_copy(src_ref=src, dst_ref=dst, send_sem=send_sems.at[k],
                                        recv_sem=recv_sems.at[k], device_id=to, device_id_type=MESH)


def _gather_weights(shards):
    n = len(shards)
    units = _units([s.shape for s in shards])
    nu = len(units)

    def body(*refs):
        ins, outs = refs[:n], refs[n:2 * n]
        send_sems, recv_sems, local_sems = refs[2 * n:]
        x, y, c, chips = _place()
        me = 2 * x + y
        sibling = (x, y, 1 - c)
        copy = functools.partial(_remote, send_sems, recv_sems)
        keeps, sends = [], []
        for u, (w, r0, nr) in enumerate(units):
            rows = pl.ds(r0, nr)
            keeps.append(pltpu.make_async_copy(ins[w].at[:, rows, :], outs[w].at[me, :, rows, :], local_sems.at[u]))
            keeps[-1].start()
        for j, (cx, cy) in enumerate(chips):
            for u, (w, r0, nr) in enumerate(units):
                rows = pl.ds(r0, nr)
                sends.append(copy(j * nu + u, ins[w].at[c, rows, :], outs[w].at[me, c, rows, :], (cx, cy, c)))
                sends[-1].start()
        for j, (cx, cy) in enumerate(chips):
            for u, (w, r0, nr) in enumerate(units):
                landed = outs[w].at[2 * cx + cy, c, pl.ds(r0, nr), :]
                copy(j * nu + u, landed, landed, (cx, cy, c)).wait_recv()
                sends.append(copy((3 + j) * nu + u, landed, landed, sibling))
                sends[-1].start()
        for j, (cx, cy) in enumerate(chips):
            for u, (w, r0, nr) in enumerate(units):
                other = outs[w].at[2 * cx + cy, 1 - c, pl.ds(r0, nr), :]
                copy((3 + j) * nu + u, other, other, sibling).wait_recv()
        for cp in sends:
            cp.wait_send()
        for keep in keeps:
            keep.wait()

    return pl.pallas_call(
        body, name="gather_weights",
        out_shape=[jax.ShapeDtypeStruct((N_CHIPS,) + s.shape, s.dtype) for s in shards],
        in_specs=[ANY] * n, out_specs=[ANY] * n,
        scratch_shapes=[pltpu.SemaphoreType.DMA((6 * nu,)), pltpu.SemaphoreType.DMA((6 * nu,)),
                        pltpu.SemaphoreType.DMA((nu,))])(*shards)


def _pair_exchange(g0, g1):
    n = len(g0)

    def body(*refs):
        layers, outs = (refs[:n], refs[n:2 * n]), refs[2 * n:3 * n]
        send_sems, recv_sems = refs[3 * n:]
        x, y, c, _ = _place()
        copy = functools.partial(_remote, send_sems, recv_sems)
        for w in range(n):
            for q in range(N_CHIPS):
                for layer in range(DEPTH):
                    cp = copy(N_CHIPS * w + q, layers[layer][w].at[q], outs[w].at[q], (x, y, 1 - c))
                    pl.when(c == 1 - layer)(cp.start)
        for w in range(n):
            for q in range(N_CHIPS):
                copy(N_CHIPS * w + q, layers[0][w].at[q], outs[w].at[q], (x, y, 1 - c)).wait()

    return pl.pallas_call(
        body, name="pair_exchange", out_shape=[jax.ShapeDtypeStruct(g.shape, g.dtype) for g in g0],
        in_specs=[ANY] * (2 * n), out_specs=[ANY] * n,
        scratch_shapes=[pltpu.SemaphoreType.DMA((N_CHIPS * n,)), pltpu.SemaphoreType.DMA((N_CHIPS * n,))])(*g0, *g1)


def _pair_sum(name, g0, g1, theirs, cflag):
    shape = theirs.shape
    rows, width = shape[0] * shape[1], shape[2]

    def body(ins, outs, _):
        mine = jnp.where(ins[3][0:1, 0:1] == 0.0, ins[0][...], ins[1][...])
        tot = mine + ins[2][...]
        outs[0][...] = tot
        outs[1][...] = tot.astype(BF16)
    ins = [(a.reshape(rows, width), width, 0) for a in (g0, g1, theirs)] + [(cflag, None, None)]
    f32, bf16 = _ew(name, body, ins, [(width, F32), (width, BF16)], rows)
    return f32.reshape(shape), bf16.reshape(shape)


def _chip_exchange(parts):
    n = len(parts)

    def body(*refs):
        ins, outs = refs[:n], refs[n:2 * n]
        send_sems, recv_sems = refs[2 * n:]
        x, y, c, chips = _place()
        copy = functools.partial(_remote, send_sems, recv_sems)
        sends = []
        for j, (cx, cy) in enumerate(chips):
            for w in range(n):
                sends.append(copy(j * n + w, ins[w].at[2 * cx + cy], outs[w].at[j], (cx, cy, c)))
                sends[-1].start()
        for j, (cx, cy) in enumerate(chips):
            for w in range(n):
                copy(j * n + w, outs[w].at[j], outs[w].at[j], (cx, cy, c)).wait_recv()
        for cp in sends:
            cp.wait_send()

    return pl.pallas_call(
        body, name="chip_exchange",
        out_shape=[jax.ShapeDtypeStruct((3,) + a.shape[1:], a.dtype) for a in parts],
        in_specs=[ANY] * n, out_specs=[ANY] * n,
        scratch_shapes=[pltpu.SemaphoreType.DMA((3 * n,)), pltpu.SemaphoreType.DMA((3 * n,))])(*parts)


def _chip_sum(name, part, landed, chipflag):
    _, r, width = part.shape
    tm = min(r, 256)

    def kern(p_ref, l_ref, flag_ref, o_ref):
        me = flag_ref[0:1, 0:1]
        own = jnp.where(me == 0.0, p_ref[0], jnp.where(me == 1.0, p_ref[1], jnp.where(me == 2.0, p_ref[2], p_ref[3])))
        o_ref[...] = ((own + l_ref[0].astype(F32)) + l_ref[1].astype(F32)) + l_ref[2].astype(F32)

    return pl.pallas_call(
        kern, name=name, grid=(r // tm,),
        in_specs=[pl.BlockSpec((N_CHIPS, tm, width), lambda i: (0, i, 0)),
                  pl.BlockSpec((3, tm, width), lambda i: (0, i, 0)),
                  pl.BlockSpec((1, LANES), lambda i: (0, 0))],
        out_specs=pl.BlockSpec((tm, width), lambda i: (i, 0)),
        out_shape=jax.ShapeDtypeStruct((r, width), F32), compiler_params=_params(("arbitrary",)))(part, landed, chipflag)


def _pair_broadcast(mine):
    n = len(mine)
    units = _units([a.shape for a in mine])

    def body(*refs):
        ins, outs = refs[:n], refs[n:2 * n]
        send_sems, recv_sems = refs[2 * n:]
        x, y, c, _ = _place()
        copy = functools.partial(_remote, send_sems, recv_sems)
        cps = [copy(u, ins[w].at[pl.ds(r0, nr), :], outs[w].at[pl.ds(r0, nr), :], (x, y, 1 - c))
               for u, (w, r0, nr) in enumerate(units)]
        for cp in cps:
            cp.start()
        for cp in cps:
            cp.wait()

    return pl.pallas_call(
        body, name="pair_broadcast", out_shape=[jax.ShapeDtypeStruct(a.shape, a.dtype) for a in mine],
        in_specs=[ANY] * n, out_specs=[ANY] * n,
        scratch_shapes=[pltpu.SemaphoreType.DMA((len(units),)), pltpu.SemaphoreType.DMA((len(units),))])(*mine)


def _small_allreduce(v):
    offsets = [(dx, dy, dc) for dx in (0, 1) for dy in (0, 1) for dc in (0, 1)][1:]

    def body(v_ref, out_ref, recv_ref, send_sems, recv_sems):
        x, y, c, _ = _place()
        flip = lambda a, d: 1 - a if d else a
        peers = [(flip(x, dx), flip(y, dy), flip(c, dc)) for dx, dy, dc in offsets]
        copy = functools.partial(_remote, send_sems, recv_sems)
        me = 4 * x + 2 * y + c
        recv_ref[me] = v_ref[...]
        cps = [copy(k, v_ref, recv_ref.at[me], peer) for k, peer in enumerate(peers)]
        for cp in cps:
            cp.start()
        for k, (px, py, pc) in enumerate(peers):
            landed = recv_ref.at[4 * px + 2 * py + pc]
            copy(k, landed, landed, (px, py, pc)).wait_recv()
        for cp in cps:
            cp.wait_send()
        tot = recv_ref[0]
        for d in range(1, 8):
            tot = tot + recv_ref[d]
        out_ref[...] = tot

    vmem = pl.BlockSpec(memory_space=pltpu.VMEM)
    return pl.pallas_call(
        body, name="small_allreduce", out_shape=jax.ShapeDtypeStruct(v.shape, v.dtype),
        in_specs=[vmem], out_specs=vmem,
        scratch_shapes=[pltpu.VMEM((8,) + v.shape, v.dtype), pltpu.SemaphoreType.DMA((7,)),
                        pltpu.SemaphoreType.DMA((7,))])(v)


def _adam_math(gv, wv, mv, vv):
    mv = ADAM_B1 * mv + (1.0 - ADAM_B1) * gv
    vv = ADAM_B2 * vv + (1.0 - ADAM_B2) * (gv * gv)
    m_hat = mv / (1.0 - ADAM_B1 ** ADAM_STEP)
    v_hat = vv / (1.0 - ADAM_B2 ** ADAM_STEP)
    return -ADAM_LR * (m_hat / (jnp.sqrt(v_hat) + ADAM_EPS) + ADAM_WD * wv), mv, vv


def _adamw_big(name, mine, theirs, cflag, w, m, v):
    _, r, width = w.shape
    tm = min(r, 256)

    def kern(mine_ref, theirs_ref, flag_ref, w_ref, m_ref, v_ref, g_ref, d_ref, nm_ref, nv_ref):
        layer = pl.program_id(0).astype(F32)
        gv = jnp.where(flag_ref[0:1, 0:1] == layer, mine_ref[...], theirs_ref[...])
        g_ref[0] = gv
        d_ref[0], nm_ref[0], nv_ref[0] = _adam_math(gv, w_ref[0], m_ref[0], v_ref[0])

    flat = pl.BlockSpec((tm, width), lambda l, i: (i, 0))
    stacked = pl.BlockSpec((1, tm, width), lambda l, i: (l, i, 0))
    return pl.pallas_call(
        kern, name=name, grid=(DEPTH, r // tm),
        in_specs=[flat, flat, pl.BlockSpec((1, LANES), lambda l, i: (0, 0)), stacked, stacked, stacked],
        out_specs=[stacked] * 4, out_shape=[jax.ShapeDtypeStruct(w.shape, F32)] * 4,
        compiler_params=_params(("arbitrary", "arbitrary")))(mine, theirs, cflag, w, m, v)


def _adamw_small(g, w, m, v):
    def body(ins, outs, _):
        outs[0][...], outs[1][...], outs[2][...] = _adam_math(*(r[...] for r in ins))
    return _ew("adamw_small", body, [(a, LANES, 0) for a in (g, w, m, v)], [(LANES, F32)] * 3, SMALL_ROWS)


def kernel(x, p, positions, g_mix, w_in, sink, g_q, w_uq, g_kv, w_ukv, w_br_a, w_br_b, w_out, g_ple, w_ple_gate, w_ple_proj, g_final, loss_target, m_g_mix, m_w_in, m_sink, m_g_q, m_w_uq, m_g_kv, m_w_ukv, m_w_br_a, m_w_br_b, m_w_out, m_g_ple, m_w_ple_gate, m_w_ple_proj, m_g_final, v_g_mix, v_w_in, v_sink, v_g_q, v_w_uq, v_g_kv, v_w_ukv, v_w_br_a, v_w_br_b, v_w_out, v_g_ple, v_w_ple_gate, v_w_ple_proj, v_g_final):
    w = dict(g_mix=g_mix, w_in=w_in, sink=sink, g_q=g_q, w_uq=w_uq, g_kv=g_kv, w_ukv=w_ukv, w_br_a=w_br_a,
             w_br_b=w_br_b, w_out=w_out, g_ple=g_ple, w_ple_gate=w_ple_gate, w_ple_proj=w_ple_proj, g_final=g_final)
    m = dict(g_mix=m_g_mix, w_in=m_w_in, sink=m_sink, g_q=m_g_q, w_uq=m_w_uq, g_kv=m_g_kv, w_ukv=m_w_ukv,
             w_br_a=m_w_br_a, w_br_b=m_w_br_b, w_out=m_w_out, g_ple=m_g_ple, w_ple_gate=m_w_ple_gate,
             w_ple_proj=m_w_ple_proj, g_final=m_g_final)
    v = dict(g_mix=v_g_mix, w_in=v_w_in, sink=v_sink, g_q=v_g_q, w_uq=v_w_uq, g_kv=v_g_kv, w_ukv=v_w_ukv,
             w_br_a=v_w_br_a, w_br_b=v_w_br_b, w_out=v_w_out, g_ple=v_g_ple, w_ple_gate=v_w_ple_gate,
             w_ple_proj=v_w_ple_proj, g_final=v_g_final)
    wfull = _gather_full(w)
    sm = {name: w[name] for name in SMALL}
    loss_row, grad_x, layer_grads, dg_final = _local_step(x, p, positions, wfull, sm, loss_target)
    loss = lax.psum(loss_row[0, 0], ("x", "y", "c"))
    res = _update(layer_grads, dg_final, w, m, v)
    return (loss, grad_x, *[res[name][kind] for kind in range(4) for name in WEIGHT_NAMES])


def _gather_full(w):
    gathered = _gather_weights([w[name].astype(BF16) for name, _ in SHARDED])
    return {name: [jnp.concatenate([gathered[k][q, layer] for q in range(N_CHIPS)], axis=axis - 1)
                   for layer in range(DEPTH)] for k, (name, axis) in enumerate(SHARDED)}


def _update(layer_grads, dg_final, w, m, v):
    small_shapes = {name: w[name].shape for name in SMALL}
    cflag = jnp.full((1, LANES), lax.axis_index("c"), F32)
    chipflag = jnp.full((1, LANES), 2 * lax.axis_index("x") + lax.axis_index("y"), F32)

    slots = [[_to_slots(layer_grads[layer][name], axis - 1) for name, axis in SHARDED] for layer in range(DEPTH)]
    theirs = _pair_exchange(slots[0], slots[1])
    pair = [_pair_sum("pair_sum_" + name, slots[0][k], slots[1][k], theirs[k], cflag)
            for k, (name, _) in enumerate(SHARDED)]
    landed = _chip_exchange([bf16 for _, bf16 in pair])
    mine = [_chip_sum("chip_sum_" + name, pair[k][0], landed[k], chipflag) for k, (name, _) in enumerate(SHARDED)]
    other = _pair_broadcast(mine)
    res = {name: _adamw_big("adamw_" + name, mine[k], other[k], cflag, w[name], m[name], v[name])
           for k, (name, _) in enumerate(SHARDED)}

    gsmall = {name: jnp.stack([layer_grads[layer][name] for layer in range(DEPTH)]) for name in SMALL[:-1]}
    gsmall['g_final'] = dg_final
    gsum = _small_allreduce(_pack_small(gsmall))
    small = (gsum,) + tuple(_adamw_small(gsum, _pack_small(w), _pack_small(m), _pack_small(v)))
    for name, arrs in zip(SMALL, zip(*[[_unpack_small(a, small_shapes)[n] for n in SMALL] for a in small])):
        res[name] = arrs
    return res
```

```python
import functools
import math

import jax
import jax.numpy as jnp
from jax import lax
from jax.experimental import pallas as pl
from jax.experimental.pallas import tpu as pltpu

F32 = jnp.float32
BF16 = jnp.bfloat16

D_MODEL = 1024
DEPTH = 2
PLE_DIM = 256
BLOCK = 128
EPS = 1e-6
NEG = -1e30
HEADS = 8
SWA_KV_HEADS = 2
HEAD_DIM = 64
LANES = 128
HPAD = HEADS * LANES
MLA_QK = 96
MLA_ROPE = 32
MLA_Q_LORA = 256
MLA_KV_LORA = 128
ROPE_THETA = 10000.0
IN_SIZES = (512, 128, 128, 512, 256, 128, 32, 512, 1024, 1024)

Z_MA, Z_MB, Z_AQ, Z_AGATE, Z_BGATE = 0, 1024, 2048, 3072, 4096
Z_AK, Z_AV, Z_BQD, Z_BKVD, Z_BKR = 5120, 5376, 5632, 5888, 6016
Z_WIDTH = 6144

ADAM_LR, ADAM_B1, ADAM_B2, ADAM_EPS, ADAM_WD, ADAM_STEP = 0.001, 0.9, 0.999, 1e-08, 0.01, 10

VMEM_LIMIT = 56 * 1024 * 1024
MESH = pl.DeviceIdType.MESH

WEIGHT_NAMES = ('g_mix', 'w_in', 'sink', 'g_q', 'w_uq', 'g_kv', 'w_ukv', 'w_br_a', 'w_br_b',
                'w_out', 'g_ple', 'w_ple_gate', 'w_ple_proj', 'g_final')
SHARDED = (('w_in', 2), ('w_uq', 2), ('w_ukv', 2), ('w_br_a', 2), ('w_br_b', 2),
           ('w_out', 1), ('w_ple_gate', 1), ('w_ple_proj', 2))
SMALL = ('g_mix', 'sink', 'g_q', 'g_kv', 'g_ple', 'g_final')
N_CHIPS = 4


def _params(sem):
    return pltpu.CompilerParams(dimension_semantics=sem, vmem_limit_bytes=VMEM_LIMIT)


MM_TN = 512
ROW_TILE = 512
BIG_WEIGHT_BYTES = 8 * 1024 * 1024


def _row_tile(rows, weight_bytes=0):
    tm = ROW_TILE // 2 if weight_bytes > BIG_WEIGHT_BYTES else ROW_TILE
    return min(tm, rows)


def _ew(name, body, ins, outs, rows, accs=(), mms=(), tm=None):
    n_mm, n_in, n_out = len(mms), len(ins), len(outs)
    if tm is None:
        tm = _row_tile(rows, sum(b.size * b.dtype.itemsize for _, b in mms))
    in_specs, args = [], []
    for a, b in mms:
        in_specs += [pl.BlockSpec((tm, a.shape[1]), lambda i: (i, 0)), pl.BlockSpec(b.shape, lambda i: (0, 0))]
        args += [a, b]
    for arr, width, cb in ins:
        if width is None:
            in_specs.append(pl.BlockSpec(arr.shape, lambda i, nd=arr.ndim: (0,) * nd))
        else:
            in_specs.append(pl.BlockSpec((tm, width), lambda i, cb=cb: (i, cb)))
        args.append(arr)
    out_shape, out_specs, aliases = [], [], {}
    for k, out in enumerate(outs):
        if len(out) == 4:
            aliases[len(args)] = k
            in_specs.append(pl.BlockSpec(memory_space=pl.ANY))
            args.append(out[2])
            out_shape.append(jax.ShapeDtypeStruct(out[2].shape, out[2].dtype))
            out_specs.append(pl.BlockSpec((tm, out[0]), lambda i, cb=out[3]: (i, cb)))
        else:
            out_shape.append(jax.ShapeDtypeStruct((rows, out[0]), out[1]))
            out_specs.append(pl.BlockSpec((tm, out[0]), lambda i: (i, 0)))
    n_in += len(aliases)
    out_shape += [jax.ShapeDtypeStruct(s, F32) for s in accs]
    out_specs += [pl.BlockSpec(s, lambda i: (0, 0)) for s in accs]

    def kern(*refs):
        mm_refs, refs = refs[:2 * n_mm], refs[2 * n_mm:]
        in_refs, out_refs = refs[:n_in - len(aliases)], refs[n_in:n_in + n_out]
        acc_refs, prod_refs = refs[n_in + n_out:n_in + n_out + len(accs)], refs[n_in + n_out + len(accs):]
        if acc_refs:
            @pl.when(pl.program_id(0) == 0)
            def _():
                for r in acc_refs:
                    r[...] = jnp.zeros_like(r)
        for k in range(n_mm):
            a_ref, b_ref, prod = mm_refs[2 * k], mm_refs[2 * k + 1], prod_refs[k]
            av = a_ref[...].astype(BF16)
            n = b_ref.shape[1]
            tn = min(MM_TN, n)
            for j in range(n // tn):
                cols = slice(j * tn, (j + 1) * tn)
                prod[:, cols] = jnp.dot(av, b_ref[:, cols], preferred_element_type=F32)
        body(tuple(prod_refs) + tuple(in_refs), out_refs, acc_refs)

    scratch = [pltpu.VMEM((tm, b.shape[1]), F32) for _, b in mms]
    res = pl.pallas_call(kern, name=name, grid=(rows // tm,), in_specs=in_specs, out_specs=out_specs,
                         out_shape=out_shape, scratch_shapes=scratch, input_output_aliases=aliases,
                         compiler_params=_params(("arbitrary",)))(*args)
    return res


def _rms_fwd(name, x, width, cb, g, rows):
    def body(ins, outs, _):
        xv = ins[0][...].astype(F32)
        r = lax.rsqrt(jnp.mean(xv * xv, axis=-1, keepdims=True) + EPS)
        outs[0][...] = ((xv * r) * ins[1][...]).astype(BF16)
    return _ew(name, body, [(x, width, cb), (g.reshape(1, width), None, None)], [(width, BF16)], rows)[0]


def _rms_bwd(name, x, width, cb, g, dh_mm, rows, out_dtype, dres=None, into=()):
    def body(ins, outs, accs):
        dhv, xv, gv = ins[0][...], ins[1][...].astype(F32), ins[2][...]
        r = lax.rsqrt(jnp.mean(xv * xv, axis=-1, keepdims=True) + EPS)
        xhat = xv * r
        accs[0][...] += jnp.sum(dhv * xhat, axis=0, keepdims=True)
        dy = dhv * gv
        dx = r * (dy - xhat * jnp.mean(dy * xhat, axis=-1, keepdims=True))
        if dres is not None:
            dx = dx + ins[3][...]
        outs[0][...] = dx.astype(out_dtype)
    ins = [(x, width, cb), (g.reshape(1, width), None, None)]
    if dres is not None:
        ins.append((dres, width, 0))
    return _ew(name, body, ins, [(width, out_dtype) + tuple(into)], rows, accs=[(1, width)], mms=[dh_mm])


def _mm(name, a, b, out_dtype, residual=None, f32_cols=None, tn=MM_TN):
    M, K = a.shape
    N = b.shape[1]
    tm, tn = _row_tile(M, b.size * b.dtype.itemsize), min(tn, N)
    has_res = residual is not None
    c0, cw = f32_cols if f32_cols else (0, 0)

    def kern(*refs):
        a_ref, b_ref = refs[0], refs[1]
        o_ref = refs[3] if has_res else refs[2]
        av = a_ref[...].astype(BF16)
        for j in range(N // tn):
            cols = slice(j * tn, (j + 1) * tn)
            part = jnp.dot(av, b_ref[:, cols], preferred_element_type=F32)
            if has_res:
                part = part + refs[2][:, cols]
            o_ref[:, cols] = part.astype(o_ref.dtype)
            if c0 <= j * tn and (j + 1) * tn <= c0 + cw:
                refs[-1][:, j * tn - c0:(j + 1) * tn - c0] = part

    in_specs = [pl.BlockSpec((tm, K), lambda i: (i, 0)), pl.BlockSpec((K, N), lambda i: (0, 0))]
    args = [a, b]
    if has_res:
        in_specs.append(pl.BlockSpec((tm, N), lambda i: (i, 0)))
        args.append(residual)
    out_specs = [pl.BlockSpec((tm, N), lambda i: (i, 0))]
    out_shape = [jax.ShapeDtypeStruct((M, N), out_dtype)]
    if f32_cols:
        assert c0 % tn == 0 and cw % tn == 0
        out_specs.append(pl.BlockSpec((tm, cw), lambda i: (i, 0)))
        out_shape.append(jax.ShapeDtypeStruct((M, cw), F32))
    res = pl.pallas_call(kern, name=name, grid=(M // tm,), in_specs=in_specs, out_specs=out_specs,
                         out_shape=out_shape, compiler_params=_params(("parallel",)))(*args)
    return res if f32_cols else res[0]


def _mm_tn(name, a, b, tk=512, tn=2048):
    T, M = a.shape
    N = b.shape[1]
    tn, tk = min(tn, N), min(tk, T)

    def kern(a_ref, b_ref, o_ref):
        k = pl.program_id(1)
        part = _dot_tn(a_ref[...].astype(BF16), b_ref[...].astype(BF16))

        @pl.when(k == 0)
        def _():
            o_ref[...] = part

        @pl.when(k > 0)
        def _():
            o_ref[...] += part

    return pl.pallas_call(
        kern, name=name, grid=(N // tn, T // tk),
        in_specs=[pl.BlockSpec((tk, M), lambda j, k: (k, 0)), pl.BlockSpec((tk, tn), lambda j, k: (k, j))],
        out_specs=pl.BlockSpec((M, tn), lambda j, k: (0, j)),
        out_shape=jax.ShapeDtypeStruct((M, N), F32),
        compiler_params=_params(("parallel", "arbitrary")))(a, b)


def _dot_nt(a, b):
    return lax.dot_general(a, b, (((1,), (1,)), ((), ())), preferred_element_type=F32)


def _dot_tn(a, b):
    return lax.dot_general(a, b, (((0,), (0,)), ((), ())), preferred_element_type=F32)


SWA_SCALE = HEAD_DIM ** -0.5


def _swa_band(n, pq_ref, pkp_ref, pkc_ref):
    posk = jnp.concatenate([pkp_ref[0], pkc_ref[0]], axis=1)
    dist = (pq_ref[...] - posk).astype(F32)
    qi = lax.broadcasted_iota(jnp.int32, (BLOCK, 2 * BLOCK), 0)
    kj = lax.broadcasted_iota(jnp.int32, (BLOCK, 2 * BLOCK), 1)
    t_abs = n * BLOCK + qi
    s_abs = n * BLOCK - BLOCK + kj
    return dist, (s_abs >= 0) & (s_abs <= t_abs) & (t_abs - s_abs < BLOCK)


SWA_GROUP = HEADS // SWA_KV_HEADS


def _swa_group_q(q_all, g):
    heads = range(g * SWA_GROUP, (g + 1) * SWA_GROUP)
    return jnp.concatenate([(q_all[:, h * LANES:(h + 1) * LANES] * SWA_SCALE).astype(BF16) for h in heads], axis=0)


def _swa_mask(s, dist, valid, h):
    return jnp.where(valid, s - (2.0 ** -(h + 1)) * dist, NEG)


def _swa_specs(nb):
    prev = lambda b, n: b * nb + jnp.maximum(n - 1, 0)
    own = lambda b, n: b * nb + n
    return [
        pl.BlockSpec((BLOCK, HPAD), lambda b, n: (own(b, n), Z_AQ // HPAD)),
        pl.BlockSpec((BLOCK, 256), lambda b, n: (prev(b, n), Z_AK // 256)),
        pl.BlockSpec((BLOCK, 256), lambda b, n: (own(b, n), Z_AK // 256)),
        pl.BlockSpec((BLOCK, 256), lambda b, n: (prev(b, n), Z_AV // 256)),
        pl.BlockSpec((BLOCK, 256), lambda b, n: (own(b, n), Z_AV // 256)),
        pl.BlockSpec((BLOCK, 1), lambda b, n: (own(b, n), 0)),
        pl.BlockSpec((1, 1, BLOCK), lambda b, n: (prev(b, n), 0, 0)),
        pl.BlockSpec((1, 1, BLOCK), lambda b, n: (own(b, n), 0, 0)),
    ]


def _swa_fwd(z, gate, pos_col, pos_row, sink_row, B, S):
    nb = S // BLOCK
    T = B * S

    def kern(q_ref, kp_ref, kc_ref, vp_ref, vc_ref, pq_ref, pkp_ref, pkc_ref, gate_ref, sink_ref,
             oraw_ref, og_ref, lse_ref):
        q_all = q_ref[...]
        kb = jnp.concatenate([kp_ref[...], kc_ref[...]], axis=0).astype(BF16)
        vb = jnp.concatenate([vp_ref[...], vc_ref[...]], axis=0).astype(BF16)
        dist, valid = _swa_band(pl.program_id(1), pq_ref, pkp_ref, pkc_ref)
        lane = lax.broadcasted_iota(jnp.int32, (BLOCK, LANES), 1)
        lse_all = jnp.zeros((BLOCK, LANES), F32)
        for grp in range(SWA_KV_HEADS):
            gcols = slice(grp * LANES, (grp + 1) * LANES)
            s_all = _dot_nt(_swa_group_q(q_all, grp), kb[:, gcols])
            probs = []
            for hh in range(SWA_GROUP):
                h = grp * SWA_GROUP + hh
                s = _swa_mask(s_all[hh * BLOCK:(hh + 1) * BLOCK], dist, valid, h)
                sink_h = sink_ref[0:1, h:h + 1]
                m = jnp.maximum(jnp.max(s, axis=-1, keepdims=True), sink_h)
                e = jnp.exp(s - m)
                denom = jnp.sum(e, axis=-1, keepdims=True) + jnp.exp(sink_h - m)
                probs.append((e * (1.0 / denom)).astype(BF16))
                lse_all = jnp.where(lane == h, m + jnp.log(denom), lse_all)
            o_all = jnp.dot(jnp.concatenate(probs, axis=0), vb[:, gcols], preferred_element_type=F32)
            for hh in range(SWA_GROUP):
                cols = slice((grp * SWA_GROUP + hh) * LANES, (grp * SWA_GROUP + hh + 1) * LANES)
                o = o_all[hh * BLOCK:(hh + 1) * BLOCK]
                oraw_ref[:, cols] = o
                g = gate_ref[:, cols].astype(F32)
                og_ref[:, cols] = (o * (g * jax.nn.sigmoid(g))).astype(BF16)
        lse_ref[...] = lse_all

    own = lambda b, n: b * nb + n
    in_specs = _swa_specs(nb) + [
        pl.BlockSpec((BLOCK, HPAD), lambda b, n: (own(b, n), 0)),
        pl.BlockSpec((1, LANES), lambda b, n: (0, 0)),
    ]
    out_specs = [pl.BlockSpec((BLOCK, HPAD), lambda b, n: (own(b, n), 0)),
                 pl.BlockSpec((BLOCK, HPAD), lambda b, n: (own(b, n), 0)),
                 pl.BlockSpec((BLOCK, LANES), lambda b, n: (own(b, n), 0))]
    out_shape = [jax.ShapeDtypeStruct((T, HPAD), F32), jax.ShapeDtypeStruct((T, HPAD), BF16),
                 jax.ShapeDtypeStruct((T, LANES), F32)]
    return pl.pallas_call(kern, name="swa_fwd", grid=(B, nb), in_specs=in_specs, out_specs=out_specs,
                          out_shape=out_shape, compiler_params=_params(("parallel", "arbitrary")))(
        z, z, z, z, z, pos_col, pos_row, pos_row, gate, sink_row)


def _swa_bwd(z, pos_col, pos_row, sink_row, lse, do_raw, delta, dz, B, S):
    nb = S // BLOCK
    T = B * S

    def kern(q_ref, kp_ref, kc_ref, vp_ref, vc_ref, pq_ref, pkp_ref, pkc_ref, sink_ref, lse_ref, do_ref,
             delta_ref, dz_ref, dq_ref, dk_ref, dv_ref, dsink_ref):
        b, n = pl.program_id(0), pl.program_id(1)

        @pl.when(n == 0)
        def _():
            dk_ref[...] = jnp.zeros_like(dk_ref)
            dv_ref[...] = jnp.zeros_like(dv_ref)

        @pl.when((b == 0) & (n == 0))
        def _():
            dsink_ref[...] = jnp.zeros_like(dsink_ref)

        q_all = q_ref[...]
        kb = jnp.concatenate([kp_ref[...], kc_ref[...]], axis=0).astype(BF16)
        vb = jnp.concatenate([vp_ref[...], vc_ref[...]], axis=0).astype(BF16)
        dist, valid = _swa_band(n, pq_ref, pkp_ref, pkc_ref)
        lane1 = lax.broadcasted_iota(jnp.int32, (1, LANES), 1)
        dsink = jnp.zeros((1, LANES), F32)
        dk_band, dv_band = [], []
        for grp in range(SWA_KV_HEADS):
            gcols = slice(grp * LANES, (grp + 1) * LANES)
            heads = range(grp * SWA_GROUP, (grp + 1) * SWA_GROUP)
            qg = _swa_group_q(q_all, grp)
            dog = jnp.concatenate([do_ref[:, h * LANES:(h + 1) * LANES] for h in heads], axis=0)
            s_all = _dot_nt(qg, kb[:, gcols])
            dp_all = _dot_nt(dog, vb[:, gcols])
            ps, dss = [], []
            for hh, h in enumerate(heads):
                blk = slice(hh * BLOCK, (hh + 1) * BLOCK)
                lse_h = lse_ref[:, h:h + 1]
                delta_h = delta_ref[:, h * LANES:h * LANES + 1]
                p = jnp.exp(_swa_mask(s_all[blk], dist, valid, h) - lse_h)
                ps.append(p.astype(BF16))
                dss.append((p * (dp_all[blk] - delta_h)).astype(BF16))
                psink = jnp.exp(sink_ref[0:1, h:h + 1] - lse_h)
                dsink = dsink + jnp.where(lane1 == h, -jnp.sum(psink * delta_h, axis=0, keepdims=True), 0.0)
            dsg = jnp.concatenate(dss, axis=0)
            dq_all = jnp.dot(dsg, kb[:, gcols], preferred_element_type=F32) * SWA_SCALE
            for hh, h in enumerate(heads):
                dq_ref[:, h * LANES:(h + 1) * LANES] = dq_all[hh * BLOCK:(hh + 1) * BLOCK].astype(BF16)
            dk_band.append(jnp.dot(qg.T, dsg, preferred_element_type=F32).T)
            dv_band.append(jnp.dot(dog.T, jnp.concatenate(ps, axis=0), preferred_element_type=F32).T)
        dsink_ref[...] += dsink
        dkb = jnp.concatenate(dk_band, axis=1)
        dvb = jnp.concatenate(dv_band, axis=1)
        r_prev = pl.ds(pl.multiple_of(jnp.maximum(n - 1, 0) * BLOCK, BLOCK), BLOCK)
        r_own = pl.ds(pl.multiple_of(n * BLOCK, BLOCK), BLOCK)
        dk_ref[r_prev, :] += dkb[:BLOCK]
        dk_ref[r_own, :] += dkb[BLOCK:]
        dv_ref[r_prev, :] += dvb[:BLOCK]
        dv_ref[r_own, :] += dvb[BLOCK:]

    own = lambda b, n: b * nb + n
    in_specs = _swa_specs(nb) + [
        pl.BlockSpec((1, LANES), lambda b, n: (0, 0)),
        pl.BlockSpec((BLOCK, LANES), lambda b, n: (own(b, n), 0)),
        pl.BlockSpec((BLOCK, HPAD), lambda b, n: (own(b, n), 0)),
        pl.BlockSpec((BLOCK, HPAD), lambda b, n: (own(b, n), 0)),
        pl.BlockSpec(memory_space=pl.ANY),
    ]
    out_specs = [pl.BlockSpec((BLOCK, HPAD), lambda b, n: (own(b, n), Z_AQ // HPAD)),
                 pl.BlockSpec((S, 256), lambda b, n: (b, 0)),
                 pl.BlockSpec((S, 256), lambda b, n: (b, 0)),
                 pl.BlockSpec((1, LANES), lambda b, n: (0, 0))]
    out_shape = [jax.ShapeDtypeStruct(dz.shape, dz.dtype), jax.ShapeDtypeStruct((T, 256), F32),
                 jax.ShapeDtypeStruct((T, 256), F32), jax.ShapeDtypeStruct((1, LANES), F32)]
    return pl.pallas_call(kern, name="swa_bwd", grid=(B, nb), in_specs=in_specs, out_specs=out_specs,
                          out_shape=out_shape, input_output_aliases={len(in_specs) - 1: 0},
                          compiler_params=_params(("arbitrary", "arbitrary")))(
        z, z, z, z, z, pos_col, pos_row, pos_row, sink_row, lse, do_raw, delta, dz)


MLA_T = 256
MLA_HG = 2
MLA_W = MLA_HG * LANES
MLA_SCALE = MLA_QK ** -0.5
LOG2E = 1.4426950408889634
MLA_QSCALE = MLA_SCALE * LOG2E


def _causal(s):
    row = lax.broadcasted_iota(jnp.int32, s.shape, 0)
    col = lax.broadcasted_iota(jnp.int32, s.shape, 1)
    return jnp.where(col <= row, s, NEG)


def _mla_fwd(q, k, v, z, B, S):
    T = B * S
    nq = S // MLA_T

    def kern(q_ref, k_ref, v_ref, gate_ref, oraw_ref, og_ref, lse_ref):
        i = pl.program_id(2)

        def scores(j):
            rows = pl.ds(pl.multiple_of(j * MLA_T, MLA_T), MLA_T)
            return tuple(_dot_nt(q_ref[:, hh * LANES:(hh + 1) * LANES], k_ref[rows, hh * LANES:(hh + 1) * LANES])
                         for hh in range(MLA_HG))

        def update(j, ss, state):
            rows = pl.ds(pl.multiple_of(j * MLA_T, MLA_T), MLA_T)
            out = []
            for hh in range(MLA_HG):
                (m, l, acc), s = state[hh], ss[hh]
                m_new = jnp.maximum(m, jnp.max(s, axis=-1, keepdims=True))
                alpha = jnp.exp2(m - m_new)
                p = jnp.exp2(s - m_new)
                l = alpha * l + jnp.sum(p, axis=-1, keepdims=True)
                pv = jnp.dot(p.astype(BF16), v_ref[rows, hh * LANES:(hh + 1) * LANES], preferred_element_type=F32)
                out.append((m_new, l, alpha * acc + pv))
            return tuple(out)

        def body(j, carry):
            state, ss = carry
            s_next = scores(j + 1)
            return update(j, ss, state), s_next

        init = tuple((jnp.full((MLA_T, 1), NEG, F32), jnp.zeros((MLA_T, 1), F32), jnp.zeros((MLA_T, LANES), F32))
                     for _ in range(MLA_HG))
        state, ss = lax.fori_loop(0, i, body, (init, scores(0)))
        state = update(i, tuple(_causal(s) for s in ss), state)
        for hh in range(MLA_HG):
            m, l, acc = state[hh]
            cols = slice(hh * LANES, (hh + 1) * LANES)
            o = acc * (1.0 / l)
            oraw_ref[:, cols] = o
            g = gate_ref[:, cols].astype(F32)
            og_ref[:, cols] = (o * (g * jax.nn.sigmoid(g))).astype(BF16)
            lse_ref[:, cols] = jnp.broadcast_to(m + jnp.log2(l), (MLA_T, LANES))

    blk = lambda b, h, i: (b * nq + i, h)
    in_specs = [pl.BlockSpec((MLA_T, MLA_W), blk),
                pl.BlockSpec((S, MLA_W), lambda b, h, i: (b, h)),
                pl.BlockSpec((S, MLA_W), lambda b, h, i: (b, h)),
                pl.BlockSpec((MLA_T, MLA_W), lambda b, h, i: (b * nq + i, Z_BGATE // MLA_W + h))]
    out_specs = [pl.BlockSpec((MLA_T, MLA_W), blk)] * 3
    out_shape = [jax.ShapeDtypeStruct((T, HPAD), F32), jax.ShapeDtypeStruct((T, HPAD), BF16),
                 jax.ShapeDtypeStruct((T, HPAD), F32)]
    return pl.pallas_call(kern, name="mla_fwd", grid=(B, HEADS // MLA_HG, nq), in_specs=in_specs,
                          out_specs=out_specs, out_shape=out_shape,
                          compiler_params=_params(("parallel", "parallel", "arbitrary")))(q, k, v, z)


def _mla_bwd(q, k, v, do_raw, lse, delta, B, S):
    T = B * S
    nk = S // MLA_T

    def kern(q_ref, k_ref, v_ref, do_ref, lse_ref, delta_ref, dq_ref, dk_ref, dv_ref, dk_acc, dv_acc):
        j = pl.program_id(2)

        @pl.when(j == 0)
        def _():
            dq_ref[...] = jnp.zeros_like(dq_ref)

        dk_acc[...] = jnp.zeros_like(dk_acc)
        dv_acc[...] = jnp.zeros_like(dv_acc)

        def step(i, masked):
            rows = pl.ds(pl.multiple_of(i * MLA_T, MLA_T), MLA_T)
            for hh in range(MLA_HG):
                cols = slice(hh * LANES, (hh + 1) * LANES)
                kv, vv = k_ref[:, cols], v_ref[:, cols]
                qv, do = q_ref[rows, cols], do_ref[rows, cols]
                s = _dot_nt(qv, kv)
                if masked:
                    s = _causal(s)
                p = jnp.exp2(s - lse_ref[rows, hh * LANES:hh * LANES + 1])
                dp = _dot_nt(do, vv)
                ds = (p * (dp - delta_ref[rows, hh * LANES:hh * LANES + 1])).astype(BF16)
                dv_acc[hh] += jnp.dot(do.T, p.astype(BF16), preferred_element_type=F32)
                dk_acc[hh] += jnp.dot(qv.T, ds, preferred_element_type=F32)
                dq_ref[rows, cols] += jnp.dot(ds, kv, preferred_element_type=F32)

        step(j, True)

        def body(i, c):
            step(i, False)
            return c

        lax.fori_loop(j + 1, nk, body, 0)
        for hh in range(MLA_HG):
            cols = slice(hh * LANES, (hh + 1) * LANES)
            dk_ref[:, cols] = dk_acc[hh].T * (1.0 / LOG2E)
            dv_ref[:, cols] = dv_acc[hh].T

    whole = lambda b, h, j: (b, h)
    tile = lambda b, h, j: (b * nk + j, h)
    in_specs = [pl.BlockSpec((S, MLA_W), whole), pl.BlockSpec((MLA_T, MLA_W), tile),
                pl.BlockSpec((MLA_T, MLA_W), tile), pl.BlockSpec((S, MLA_W), whole),
                pl.BlockSpec((S, MLA_W), whole), pl.BlockSpec((S, MLA_W), whole)]
    out_specs = [pl.BlockSpec((S, MLA_W), whole), pl.BlockSpec((MLA_T, MLA_W), tile),
                 pl.BlockSpec((MLA_T, MLA_W), tile)]
    out_shape = [jax.ShapeDtypeStruct((T, HPAD), F32)] * 3
    return pl.pallas_call(kern, name="mla_bwd", grid=(B, HEADS // MLA_HG, nk), in_specs=in_specs,
                          out_specs=out_specs, out_shape=out_shape,
                          scratch_shapes=[pltpu.VMEM((MLA_HG, LANES, MLA_T), F32)] * 2,
                          compiler_params=_params(("parallel", "parallel", "arbitrary")))(
        q, k, v, do_raw, lse, delta)


def _rope_tables(pos_col, inv_lane, rows):
    def body(ins, outs, _):
        ang = ins[0][...].astype(F32) * ins[1][...]
        lane = lax.broadcasted_iota(jnp.int32, ang.shape, 1)
        cos, sin = jnp.cos(ang), jnp.sin(ang)
        first = (lane >= HEAD_DIM) & (lane < HEAD_DIM + MLA_ROPE // 2)
        second = (lane >= HEAD_DIM + MLA_ROPE // 2) & (lane < MLA_QK)
        outs[0][...] = jnp.where(lane < HEAD_DIM, 1.0, jnp.where(lane < MLA_QK, cos, 0.0))
        outs[1][...] = jnp.where(first, -sin, 0.0)
        outs[2][...] = jnp.where(second, sin, 0.0)
    return _ew("rope_tables", body, [(pos_col, 1, 0), (inv_lane, None, None)], [(LANES, F32)] * 3, rows)


def _rope(x, c, s1, s2):
    return x * c + pltpu.roll(x, 112, 1) * s1 + pltpu.roll(x, 16, 1) * s2


def _rope_t(d, c, s1, s2):
    return d * c + pltpu.roll(d * s1, 16, 1) + pltpu.roll(d * s2, 112, 1)


def _mla_prep(qdn, w_uq, kvdn, w_ukv, z, tabs, rows):
    def body(ins, outs, _):
        q_pre, kv_pre = ins[0], ins[1]
        c, s1, s2 = ins[3][...], ins[4][...], ins[5][...]
        kr = _rope(ins[2][...].astype(F32), c, s1, s2)
        for h in range(HEADS):
            cols = slice(h * LANES, (h + 1) * LANES)
            outs[0][:, cols] = (_rope(q_pre[:, cols], c, s1, s2) * MLA_QSCALE).astype(BF16)
            outs[1][:, cols] = (kv_pre[:, cols] + kr).astype(BF16)
        outs[2][...] = kv_pre[:, HPAD:].astype(BF16)
    ins = [(z, LANES, Z_BKR // LANES), (tabs[0], LANES, 0), (tabs[1], LANES, 0), (tabs[2], LANES, 0)]
    return _ew("mla_prep", body, ins, [(HPAD, BF16)] * 3, rows, mms=[(qdn, w_uq), (kvdn, w_ukv)])


def _mla_prep_bwd(dq, dk, dv, tabs, dz, rows):
    def body(ins, outs, _):
        c, s1, s2 = ins[3][...], ins[4][...], ins[5][...]
        lane = lax.broadcasted_iota(jnp.int32, c.shape, 1)
        dkr = jnp.zeros(c.shape, F32)
        for h in range(HEADS):
            cols = slice(h * LANES, (h + 1) * LANES)
            outs[0][:, cols] = _rope_t(ins[0][:, cols] * MLA_SCALE, c, s1, s2).astype(BF16)
            dkh = ins[1][:, cols]
            outs[1][:, cols] = jnp.where(lane < HEAD_DIM, dkh, 0.0).astype(BF16)
            dkr = dkr + dkh
        outs[1][:, HPAD:] = ins[2][...].astype(BF16)
        live = (lane >= HEAD_DIM) & (lane < MLA_QK)
        outs[2][...] = jnp.where(live, _rope_t(jnp.where(live, dkr, 0.0), c, s1, s2), 0.0).astype(BF16)
    ins = [(dq, HPAD, 0), (dk, HPAD, 0), (dv, HPAD, 0), (tabs[0], LANES, 0), (tabs[1], LANES, 0),
           (tabs[2], LANES, 0)]
    outs = [(HPAD, BF16), (2 * HPAD, BF16), (LANES, BF16, dz, Z_BKR // LANES)]
    return _ew("mla_prep_bwd", body, ins, outs, rows)


def _gate_bwd(name, d_o_mm, o_raw, gate, gate_cb, dz, dz_cb, rows):
    def body(ins, outs, _):
        for h in range(HEADS):
            cols = slice(h * LANES, (h + 1) * LANES)
            dog, o, g = ins[0][:, cols], ins[1][:, cols], ins[2][:, cols].astype(F32)
            sg = jax.nn.sigmoid(g)
            do = dog * (g * sg)
            outs[0][:, cols] = do.astype(BF16)
            outs[1][:, cols] = (dog * o * (sg * (1.0 + g * (1.0 - sg)))).astype(BF16)
            outs[2][:, cols] = jnp.broadcast_to(jnp.sum(do * o, axis=-1, keepdims=True), do.shape)
    ins = [(o_raw, HPAD, 0), (gate, HPAD, gate_cb)]
    outs = [(HPAD, BF16), (HPAD, BF16, dz, dz_cb), (HPAD, F32)]
    return _ew(name, body, ins, outs, rows, mms=[d_o_mm])


def _merge_out(ua, ub, z, w_out, x0, rows):
    tm = _row_tile(rows)

    def kern(ua_ref, ub_ref, ma_ref, mb_ref, w_ref, x0_ref, y_ref, x1_ref):
        ua_v, ub_v, m_a, m_b = (r[...].astype(F32) for r in (ua_ref, ub_ref, ma_ref, mb_ref))
        y = (jax.nn.sigmoid(m_a) * ua_v + jax.nn.sigmoid(m_b) * ub_v).astype(BF16)
        y_ref[...] = y
        for j in range(D_MODEL // MM_TN):
            cols = slice(j * MM_TN, (j + 1) * MM_TN)
            x1_ref[:, cols] = jnp.dot(y, w_ref[:, cols], preferred_element_type=F32) + x0_ref[:, cols]

    row = lambda cb: pl.BlockSpec((tm, D_MODEL), lambda i: (i, cb))
    return pl.pallas_call(
        kern, name="merge_out", grid=(rows // tm,),
        in_specs=[row(0), row(0), row(Z_MA // D_MODEL), row(Z_MB // D_MODEL),
                  pl.BlockSpec(w_out.shape, lambda i: (0, 0)), row(0)],
        out_specs=[row(0), row(0)],
        out_shape=[jax.ShapeDtypeStruct((rows, D_MODEL), BF16), jax.ShapeDtypeStruct((rows, D_MODEL), F32)],
        compiler_params=_params(("parallel",)))(ua, ub, z, z, w_out, x0)


def _merge_bwd(dy_mm, ua, ub, z, dz, rows):
    def body(ins, outs, _):
        dyv = ins[0][...]
        for idx in range(2):
            s = jax.nn.sigmoid(ins[3 + idx][...].astype(F32))
            outs[idx][...] = (dyv * s).astype(BF16)
            d_m = (dyv * ins[1 + idx][...].astype(F32) * (s * (1.0 - s))).astype(BF16)
            outs[2][:, idx * D_MODEL:(idx + 1) * D_MODEL] = d_m
    ins = [(ua, D_MODEL, 0), (ub, D_MODEL, 0), (z, D_MODEL, Z_MA // D_MODEL), (z, D_MODEL, Z_MB // D_MODEL)]
    outs = [(D_MODEL, BF16), (D_MODEL, BF16), (2 * D_MODEL, BF16, dz, Z_MA // (2 * D_MODEL))]
    return _ew("merge_bwd", body, ins, outs, rows, mms=[dy_mm])


def _kv_grad_cast(dk, dv, dz, rows):
    def body(ins, outs, _):
        outs[0][:, :256] = ins[0][...].astype(BF16)
        outs[0][:, 256:] = ins[1][...].astype(BF16)
    return _ew("kv_grad_cast", body, [(dk, 256, 0), (dv, 256, 0)], [(512, BF16, dz, Z_AK // 512)], rows)[0]


def _ple_fwd(x1, hn, w_pg, p, w_pp, rows):
    def body(ins, outs, _):
        u, e = ins[0][...], ins[1][...]
        outs[0][...] = ins[2][...] + jax.nn.sigmoid(u) * e
        outs[1][...] = u.astype(BF16)
        outs[2][...] = e.astype(BF16)
    return _ew("ple_fwd", body, [(x1, D_MODEL, 0)], [(D_MODEL, F32), (D_MODEL, BF16), (D_MODEL, BF16)], rows,
               mms=[(hn, w_pg), (p, w_pp)])


def _ple_bwd(dx2, u, e, rows):
    def body(ins, outs, _):
        d, s = ins[0][...], jax.nn.sigmoid(ins[1][...].astype(F32))
        outs[0][...] = (d * s).astype(BF16)
        outs[1][...] = (d * ins[2][...].astype(F32) * (s * (1.0 - s))).astype(BF16)
    return _ew("ple_bwd", body, [(dx2, D_MODEL, 0), (u, D_MODEL, 0), (e, D_MODEL, 0)],
               [(D_MODEL, BF16)] * 2, rows)


def _loss_head(x, g, target, rows):
    def body(ins, outs, accs):
        xv, gv = ins[0][...], ins[1][...]
        r = lax.rsqrt(jnp.mean(xv * xv, axis=-1, keepdims=True) + EPS)
        xhat = xv * r
        err = xhat * gv - ins[2][...]
        accs[0][...] += jnp.broadcast_to(0.5 * jnp.sum(jnp.mean(err * err, axis=-1, keepdims=True),
                                                       axis=0, keepdims=True), (1, LANES))
        dyv = err * (1.0 / D_MODEL)
        accs[1][...] += jnp.sum(dyv * xhat, axis=0, keepdims=True)
        dy = dyv * gv
        outs[0][...] = r * (dy - xhat * jnp.mean(dy * xhat, axis=-1, keepdims=True))
    ins = [(x, D_MODEL, 0), (g.reshape(1, D_MODEL), None, None), (target, D_MODEL, 0)]
    return _ew("loss_head", body, ins, [(D_MODEL, F32)], rows, accs=[(1, LANES), (1, D_MODEL)])


def _pad_heads_cols(w, n_heads, dim):
    k = w.shape[0]
    return jnp.pad(w.reshape(k, n_heads, dim), ((0, 0), (0, 0), (0, LANES - dim))).reshape(k, n_heads * LANES)


def _unpad_heads_cols(w, n_heads, dim):
    k = w.shape[0]
    return w.reshape(k, n_heads, LANES)[:, :, :dim].reshape(k, n_heads * dim)


def _layer_weights(w, i):
    segs = jnp.split(w['w_in'][i], list(_cumsum(IN_SIZES))[:-1], axis=1)
    a_q, a_k, a_v, a_gate, b_qd, b_kvd, b_kr, b_gate, m_a, m_b = segs
    kr = jnp.pad(b_kr, ((0, 0), (HEAD_DIM, LANES - MLA_QK)))
    w_in = jnp.concatenate([
        m_a, m_b, _pad_heads_cols(a_q, HEADS, HEAD_DIM), _pad_heads_cols(a_gate, HEADS, HEAD_DIM),
        _pad_heads_cols(b_gate, HEADS, HEAD_DIM), _pad_heads_cols(a_k, SWA_KV_HEADS, HEAD_DIM),
        _pad_heads_cols(a_v, SWA_KV_HEADS, HEAD_DIM), b_qd, b_kvd, kr], axis=1)
    w_uq = _pad_heads_cols(w['w_uq'][i], HEADS, MLA_QK)
    ukv = w['w_ukv'][i].reshape(MLA_KV_LORA, HEADS, 2 * HEAD_DIM)
    pad = ((0, 0), (0, 0), (0, HEAD_DIM))
    w_ukv = jnp.concatenate([jnp.pad(ukv[:, :, :HEAD_DIM], pad).reshape(MLA_KV_LORA, HPAD),
                             jnp.pad(ukv[:, :, HEAD_DIM:], pad).reshape(MLA_KV_LORA, HPAD)], axis=1)
    w_br_a = _pad_heads_cols(w['w_br_a'][i].T, HEADS, HEAD_DIM).T
    w_br_b = _pad_heads_cols(w['w_br_b'][i].T, HEADS, HEAD_DIM).T
    out = dict(w_in=w_in, w_uq=w_uq, w_ukv=w_ukv, w_br_a=w_br_a, w_br_b=w_br_b, w_out=w['w_out'][i],
               w_pg=w['w_ple_gate'][i], w_pp=w['w_ple_proj'][i])
    for name in ('w_in', 'w_uq', 'w_ukv', 'w_br_a', 'w_br_b', 'w_out', 'w_pg'):
        out[name + '_t'] = out[name].T
    return out


def _cumsum(sizes):
    acc, out = 0, []
    for s in sizes:
        acc += s
        out.append(acc)
    return out


def _unpad_grads(g):
    d = g['w_in']
    seg = lambda off, width: d[:, off:off + width]
    b_kr = seg(Z_BKR, LANES)[:, HEAD_DIM:MLA_QK]
    w_in = jnp.concatenate([
        _unpad_heads_cols(seg(Z_AQ, HPAD), HEADS, HEAD_DIM), _unpad_heads_cols(seg(Z_AK, 256), SWA_KV_HEADS, HEAD_DIM),
        _unpad_heads_cols(seg(Z_AV, 256), SWA_KV_HEADS, HEAD_DIM), _unpad_heads_cols(seg(Z_AGATE, HPAD), HEADS, HEAD_DIM),
        seg(Z_BQD, MLA_Q_LORA), seg(Z_BKVD, MLA_KV_LORA), b_kr, _unpad_heads_cols(seg(Z_BGATE, HPAD), HEADS, HEAD_DIM),
        seg(Z_MA, D_MODEL), seg(Z_MB, D_MODEL)], axis=1)
    w_uq = _unpad_heads_cols(g['w_uq'], HEADS, MLA_QK)
    ukv = g['w_ukv'].reshape(MLA_KV_LORA, 2, HEADS, LANES)[:, :, :, :HEAD_DIM]
    w_ukv = jnp.concatenate([ukv[:, 0], ukv[:, 1]], axis=-1).reshape(MLA_KV_LORA, HEADS * 2 * HEAD_DIM)
    w_br_a = _unpad_heads_cols(g['w_br_a'].T, HEADS, HEAD_DIM).T
    w_br_b = _unpad_heads_cols(g['w_br_b'].T, HEADS, HEAD_DIM).T
    return dict(w_in=w_in, w_uq=w_uq, w_ukv=w_ukv, w_br_a=w_br_a, w_br_b=w_br_b, w_out=g['w_out'],
                w_ple_gate=g['w_pg'], w_ple_proj=g['w_pp'], g_mix=g['g_mix'], sink=g['sink'], g_q=g['g_q'],
                g_kv=g['g_kv'], g_ple=g['g_ple'])


def _layer_fwd(x0, p_i, lw, sm, i, pos_col, pos_row, tabs, B, S):
    T = B * S
    h = _rms_fwd("norm_mix", x0, D_MODEL, 0, sm['g_mix'][i], T)
    z, a_gate = _mm("proj_in", h, lw['w_in'], BF16, f32_cols=(Z_AGATE, HPAD))
    sink_row = jnp.pad(sm['sink'][i], (0, LANES - HEADS)).reshape(1, LANES)
    oa_raw, oa, lse_a = _swa_fwd(z, a_gate, pos_col, pos_row, sink_row, B, S)
    qdn = _rms_fwd("norm_q", z, MLA_Q_LORA, Z_BQD // MLA_Q_LORA, sm['g_q'][i], T)
    kvdn = _rms_fwd("norm_kv", z, MLA_KV_LORA, Z_BKVD // MLA_KV_LORA, sm['g_kv'][i], T)
    qf, kf, vf = _mla_prep(qdn, lw['w_uq'], kvdn, lw['w_ukv'], z, tabs, T)
    ob_raw, ob, lse_b = _mla_fwd(qf, kf, vf, z, B, S)
    ua = _mm("proj_br_a", oa, lw['w_br_a'], BF16)
    ub = _mm("proj_br_b", ob, lw['w_br_b'], BF16)
    y, x1 = _merge_out(ua, ub, z, lw['w_out'], x0, T)
    hn = _rms_fwd("norm_ple", x1, D_MODEL, 0, sm['g_ple'][i], T)
    x2, u, e = _ple_fwd(x1, hn, lw['w_pg'], p_i, lw['w_pp'], T)
    saved = dict(x0=x0, h=h, z=z, a_gate=a_gate, sink_row=sink_row, oa_raw=oa_raw, oa=oa, lse_a=lse_a, qdn=qdn, kvdn=kvdn,
                 qf=qf, kf=kf, vf=vf, ob_raw=ob_raw, ob=ob, lse_b=lse_b, ua=ua, ub=ub, y=y, x1=x1, hn=hn,
                 u=u, e=e, p=p_i)
    return x2, saved


def _layer_bwd(dx2, sv, lw, sm, i, pos_col, pos_row, tabs, B, S):
    T = B * S
    z = sv['z']
    g = {}
    d_e, d_u = _ple_bwd(dx2, sv['u'], sv['e'], T)
    g['w_pp'] = _mm_tn("grad_pp", sv['p'], d_e)
    g['w_pg'] = _mm_tn("grad_pg", sv['hn'], d_u)
    dx1, g['g_ple'] = _rms_bwd("norm_ple_bwd", sv['x1'], D_MODEL, 0, sm['g_ple'][i], (d_u, lw['w_pg_t']), T, F32,
                               dres=dx2)
    g['w_out'] = _mm_tn("grad_out", sv['y'], dx1)
    dz = lax.empty((T, Z_WIDTH), BF16)
    d_ua, d_ub, dz = _merge_bwd((dx1, lw['w_out_t']), sv['ua'], sv['ub'], z, dz, T)
    g['w_br_a'] = _mm_tn("grad_br_a", sv['oa'], d_ua)
    g['w_br_b'] = _mm_tn("grad_br_b", sv['ob'], d_ub)
    dob_raw, dz, delta_b = _gate_bwd("gate_b_bwd", (d_ub, lw['w_br_b_t']), sv['ob_raw'], z, Z_BGATE // HPAD,
                                     dz, Z_BGATE // HPAD, T)
    dq, dk, dv = _mla_bwd(sv['qf'], sv['kf'], sv['vf'], dob_raw, sv['lse_b'], delta_b, B, S)
    dq_pre, dkv_pre, dz = _mla_prep_bwd(dq, dk, dv, tabs, dz, T)
    g['w_uq'] = _mm_tn("grad_uq", sv['qdn'], dq_pre)
    g['w_ukv'] = _mm_tn("grad_ukv", sv['kvdn'], dkv_pre)
    dz, g['g_q'] = _rms_bwd("norm_q_bwd", z, MLA_Q_LORA, Z_BQD // MLA_Q_LORA, sm['g_q'][i],
                            (dq_pre, lw['w_uq_t']), T, BF16, into=(dz, Z_BQD // MLA_Q_LORA))
    dz, g['g_kv'] = _rms_bwd("norm_kv_bwd", z, MLA_KV_LORA, Z_BKVD // MLA_KV_LORA, sm['g_kv'][i],
                             (dkv_pre, lw['w_ukv_t']), T, BF16, into=(dz, Z_BKVD // MLA_KV_LORA))
    doa_raw, dz, delta_a = _gate_bwd("gate_a_bwd", (d_ua, lw['w_br_a_t']), sv['oa_raw'], sv['a_gate'], 0,
                                     dz, Z_AGATE // HPAD, T)
    dz, d_ak, d_av, dsink = _swa_bwd(z, pos_col, pos_row, sv['sink_row'], sv['lse_a'], doa_raw, delta_a, dz, B, S)
    dz = _kv_grad_cast(d_ak, d_av, dz, T)
    g['sink'] = dsink[0, :HEADS]
    g['w_in'] = _mm_tn("grad_in", sv['h'], dz)
    dx0, g['g_mix'] = _rms_bwd("norm_mix_bwd", sv['x0'], D_MODEL, 0, sm['g_mix'][i], (dz, lw['w_in_t']), T, F32,
                               dres=dx1)
    for name in ('g_ple', 'g_q', 'g_kv', 'g_mix'):
        g[name] = g[name][0]
    return dx0, g


def _local_step(x, p, positions, wfull, sm, loss_target):
    B, S, _ = x.shape
    T = B * S
    pos_col = positions.reshape(T, 1)
    pos_row = positions.reshape(T // BLOCK, 1, BLOCK)
    half = MLA_ROPE // 2
    inv = ROPE_THETA ** (-jnp.arange(0, MLA_ROPE, 2, dtype=F32) / MLA_ROPE)
    inv_lane = jnp.tile(inv, LANES // half).reshape(1, LANES)
    tabs = _rope_tables(pos_col, inv_lane, T)
    xc = x.reshape(T, D_MODEL)
    lws, saved = [], []
    for i in range(DEPTH):
        lw = _layer_weights(wfull, i)
        xc, sv = _layer_fwd(xc, p[i].reshape(T, PLE_DIM), lw, sm, i, pos_col, pos_row, tabs, B, S)
        lws.append(lw)
        saved.append(sv)
    dx, loss, dg_final = _loss_head(xc, sm['g_final'], loss_target.reshape(T, D_MODEL), T)
    layer_grads = [None] * DEPTH
    for i in reversed(range(DEPTH)):
        dx, g = _layer_bwd(dx, saved[i], lws[i], sm, i, pos_col, pos_row, tabs, B, S)
        layer_grads[i] = _unpad_grads(g)
    return loss, dx.reshape(B, S, D_MODEL), layer_grads, dg_final[0]


SMALL_ROWS = 48


def _pack_small(arrs):
    flat = jnp.concatenate([arrs[name].reshape(-1) for name in SMALL])
    return jnp.pad(flat, (0, SMALL_ROWS * LANES - flat.shape[0])).reshape(SMALL_ROWS, LANES)


def _unpack_small(block, shapes):
    flat = block.reshape(-1)
    out, off = {}, 0
    for name in SMALL:
        n = math.prod(shapes[name])
        out[name] = flat[off:off + n].reshape(shapes[name])
        off += n
    return out


def _to_slots(g, axis):
    r, c = g.shape
    if axis == 0:
        return g.reshape(N_CHIPS, r // N_CHIPS, c)
    return g.reshape(r, N_CHIPS, c // N_CHIPS).transpose(1, 0, 2)


def _units(shapes):
    units = []
    for w, shape in enumerate(shapes):
        r = shape[-2]
        n = 4 if r >= 1024 else 1
        units += [(w, k * (r // n), r // n) for k in range(n)]
    return units


def _place():
    x, y, c = lax.axis_index("x"), lax.axis_index("y"), lax.axis_index("c")
    chips = [(1 - x, y), (x, 1 - y), (1 - x, 1 - y)]
    return x, y, c, chips


ANY = pl.BlockSpec(memory_space=pl.ANY)


def _remote(send_sems, recv_sems, k, src, dst, to):
    return pltpu.make_async_remote_copy(src_ref=src, dst_ref=dst, send_sem=send_sems.at[k],
                                        recv_sem=recv_sems.at[k], device_id=to, device_id_type=MESH)


def _gather_weights(shards):
    n = len(shards)
    units = _units([s.shape for s in shards])
    nu = len(units)

    def body(*refs):
        ins, outs = refs[:n], refs[n:2 * n]
        send_sems, recv_sems, local_sems = refs[2 * n:]
        x, y, c, chips = _place()
        me = 2 * x + y
        sibling = (x, y, 1 - c)
        copy = functools.partial(_remote, send_sems, recv_sems)
        keeps, sends = [], []
        for u, (w, r0, nr) in enumerate(units):
            rows = pl.ds(r0, nr)
            keeps.append(pltpu.make_async_copy(ins[w].at[:, rows, :], outs[w].at[me, :, rows, :], local_sems.at[u]))
            keeps[-1].start()
        for j, (cx, cy) in enumerate(chips):
            for u, (w, r0, nr) in enumerate(units):
                rows = pl.ds(r0, nr)
                sends.append(copy(j * nu + u, ins[w].at[c, rows, :], outs[w].at[me, c, rows, :], (cx, cy, c)))
                sends[-1].start()
        for j, (cx, cy) in enumerate(chips):
            for u, (w, r0, nr) in enumerate(units):
                landed = outs[w].at[2 * cx + cy, c, pl.ds(r0, nr), :]
                copy(j * nu + u, landed, landed, (cx, cy, c)).wait_recv()
                sends.append(copy((3 + j) * nu + u, landed, landed, sibling))
                sends[-1].start()
        for j, (cx, cy) in enumerate(chips):
            for u, (w, r0, nr) in enumerate(units):
                other = outs[w].at[2 * cx + cy, 1 - c, pl.ds(r0, nr), :]
                copy((3 + j) * nu + u, other, other, sibling).wait_recv()
        for cp in sends:
            cp.wait_send()
        for keep in keeps:
            keep.wait()

    return pl.pallas_call(
        body, name="gather_weights",
        out_shape=[jax.ShapeDtypeStruct((N_CHIPS,) + s.shape, s.dtype) for s in shards],
        in_specs=[ANY] * n, out_specs=[ANY] * n,
        scratch_shapes=[pltpu.SemaphoreType.DMA((6 * nu,)), pltpu.SemaphoreType.DMA((6 * nu,)),
                        pltpu.SemaphoreType.DMA((nu,))])(*shards)


def _pair_exchange(g0, g1):
    n = len(g0)

    def body(*refs):
        layers, outs = (refs[:n], refs[n:2 * n]), refs[2 * n:3 * n]
        send_sems, recv_sems = refs[3 * n:]
        x, y, c, _ = _place()
        copy = functools.partial(_remote, send_sems, recv_sems)
        for w in range(n):
            for q in range(N_CHIPS):
                for layer in range(DEPTH):
                    cp = copy(N_CHIPS * w + q, layers[layer][w].at[q], outs[w].at[q], (x, y, 1 - c))
                    pl.when(c == 1 - layer)(cp.start)
        for w in range(n):
            for q in range(N_CHIPS):
                copy(N_CHIPS * w + q, layers[0][w].at[q], outs[w].at[q], (x, y, 1 - c)).wait()

    return pl.pallas_call(
        body, name="pair_exchange", out_shape=[jax.ShapeDtypeStruct(g.shape, g.dtype) for g in g0],
        in_specs=[ANY] * (2 * n), out_specs=[ANY] * n,
        scratch_shapes=[pltpu.SemaphoreType.DMA((N_CHIPS * n,)), pltpu.SemaphoreType.DMA((N_CHIPS * n,))])(*g0, *g1)


def _pair_sum(name, g0, g1, theirs, cflag):
    shape = theirs.shape
    rows, width = shape[0] * shape[1], shape[2]

    def body(ins, outs, _):
        mine = jnp.where(ins[3][0:1, 0:1] == 0.0, ins[0][...], ins[1][...])
        tot = mine + ins[2][...]
        outs[0][...] = tot
        outs[1][...] = tot.astype(BF16)
    ins = [(a.reshape(rows, width), width, 0) for a in (g0, g1, theirs)] + [(cflag, None, None)]
    f32, bf16 = _ew(name, body, ins, [(width, F32), (width, BF16)], rows)
    return f32.reshape(shape), bf16.reshape(shape)


def _chip_exchange(parts):
    n = len(parts)

    def body(*refs):
        ins, outs = refs[:n], refs[n:2 * n]
        send_sems, recv_sems = refs[2 * n:]
        x, y, c, chips = _place()
        copy = functools.partial(_remote, send_sems, recv_sems)
        sends = []
        for j, (cx, cy) in enumerate(chips):
            for w in range(n):
                sends.append(copy(j * n + w, ins[w].at[2 * cx + cy], outs[w].at[j], (cx, cy, c)))
                sends[-1].start()
        for j, (cx, cy) in enumerate(chips):
            for w in range(n):
                copy(j * n + w, outs[w].at[j], outs[w].at[j], (cx, cy, c)).wait_recv()
        for cp in sends:
            cp.wait_send()

    return pl.pallas_call(
        body, name="chip_exchange",
        out_shape=[jax.ShapeDtypeStruct((3,) + a.shape[1:], a.dtype) for a in parts],
        in_specs=[ANY] * n, out_specs=[ANY] * n,
        scratch_shapes=[pltpu.SemaphoreType.DMA((3 * n,)), pltpu.SemaphoreType.DMA((3 * n,))])(*parts)


def _chip_sum(name, part, landed, chipflag):
    _, r, width = part.shape
    tm = min(r, 256)

    def kern(p_ref, l_ref, flag_ref, o_ref):
        me = flag_ref[0:1, 0:1]
        own = jnp.where(me == 0.0, p_ref[0], jnp.where(me == 1.0, p_ref[1], jnp.where(me == 2.0, p_ref[2], p_ref[3])))
        o_ref[...] = ((own + l_ref[0].astype(F32)) + l_ref[1].astype(F32)) + l_ref[2].astype(F32)

    return pl.pallas_call(
        kern, name=name, grid=(r // tm,),
        in_specs=[pl.BlockSpec((N_CHIPS, tm, width), lambda i: (0, i, 0)),
                  pl.BlockSpec((3, tm, width), lambda i: (0, i, 0)),
                  pl.BlockSpec((1, LANES), lambda i: (0, 0))],
        out_specs=pl.BlockSpec((tm, width), lambda i: (i, 0)),
        out_shape=jax.ShapeDtypeStruct((r, width), F32), compiler_params=_params(("arbitrary",)))(part, landed, chipflag)


def _pair_broadcast(mine):
    n = len(mine)
    units = _units([a.shape for a in mine])

    def body(*refs):
        ins, outs = refs[:n], refs[n:2 * n]
        send_sems, recv_sems = refs[2 * n:]
        x, y, c, _ = _place()
        copy = functools.partial(_remote, send_sems, recv_sems)
        cps = [copy(u, ins[w].at[pl.ds(r0, nr), :], outs[w].at[pl.ds(r0, nr), :], (x, y, 1 - c))
               for u, (w, r0, nr) in enumerate(units)]
        for cp in cps:
            cp.start()
        for cp in cps:
            cp.wait()

    return pl.pallas_call(
        body, name="pair_broadcast", out_shape=[jax.ShapeDtypeStruct(a.shape, a.dtype) for a in mine],
        in_specs=[ANY] * n, out_specs=[ANY] * n,
        scratch_shapes=[pltpu.SemaphoreType.DMA((len(units),)), pltpu.SemaphoreType.DMA((len(units),))])(*mine)


def _small_allreduce(v):
    offsets = [(dx, dy, dc) for dx in (0, 1) for dy in (0, 1) for dc in (0, 1)][1:]

    def body(v_ref, out_ref, recv_ref, send_sems, recv_sems):
        x, y, c, _ = _place()
        flip = lambda a, d: 1 - a if d else a
        peers = [(flip(x, dx), flip(y, dy), flip(c, dc)) for dx, dy, dc in offsets]
        copy = functools.partial(_remote, send_sems, recv_sems)
        me = 4 * x + 2 * y + c
        recv_ref[me] = v_ref[...]
        cps = [copy(k, v_ref, recv_ref.at[me], peer) for k, peer in enumerate(peers)]
        for cp in cps:
            cp.start()
        for k, (px, py, pc) in enumerate(peers):
            landed = recv_ref.at[4 * px + 2 * py + pc]
            copy(k, landed, landed, (px, py, pc)).wait_recv()
        for cp in cps:
            cp.wait_send()
        tot = recv_ref[0]
        for d in range(1, 8):
            tot = tot + recv_ref[d]
        out_ref[...] = tot

    vmem = pl.BlockSpec(memory_space=pltpu.VMEM)
    return pl.pallas_call(
        body, name="small_allreduce", out_shape=jax.ShapeDtypeStruct(v.shape, v.dtype),
        in_specs=[vmem], out_specs=vmem,
        scratch_shapes=[pltpu.VMEM((8,) + v.shape, v.dtype), pltpu.SemaphoreType.DMA((7,)),
                        pltpu.SemaphoreType.DMA((7,))])(v)


def _adam_math(gv, wv, mv, vv):
    mv = ADAM_B1 * mv + (1.0 - ADAM_B1) * gv
    vv = ADAM_B2 * vv + (1.0 - ADAM_B2) * (gv * gv)
    m_hat = mv / (1.0 - ADAM_B1 ** ADAM_STEP)
    v_hat = vv / (1.0 - ADAM_B2 ** ADAM_STEP)
    return -ADAM_LR * (m_hat / (jnp.sqrt(v_hat) + ADAM_EPS) + ADAM_WD * wv), mv, vv


def _adamw_big(name, mine, theirs, cflag, w, m, v):
    _, r, width = w.shape
    tm = min(r, 256)

    def kern(mine_ref, theirs_ref, flag_ref, w_ref, m_ref, v_ref, g_ref, d_ref, nm_ref, nv_ref):
        layer = pl.program_id(0).astype(F32)
        gv = jnp.where(flag_ref[0:1, 0:1] == layer, mine_ref[...], theirs_ref[...])
        g_ref[0] = gv
        d_ref[0], nm_ref[0], nv_ref[0] = _adam_math(gv, w_ref[0], m_ref[0], v_ref[0])

    flat = pl.BlockSpec((tm, width), lambda l, i: (i, 0))
    stacked = pl.BlockSpec((1, tm, width), lambda l, i: (l, i, 0))
    return pl.pallas_call(
        kern, name=name, grid=(DEPTH, r // tm),
        in_specs=[flat, flat, pl.BlockSpec((1, LANES), lambda l, i: (0, 0)), stacked, stacked, stacked],
        out_specs=[stacked] * 4, out_shape=[jax.ShapeDtypeStruct(w.shape, F32)] * 4,
        compiler_params=_params(("arbitrary", "arbitrary")))(mine, theirs, cflag, w, m, v)


def _adamw_small(g, w, m, v):
    def body(ins, outs, _):
        outs[0][...], outs[1][...], outs[2][...] = _adam_math(*(r[...] for r in ins))
    return _ew("adamw_small", body, [(a, LANES, 0) for a in (g, w, m, v)], [(LANES, F32)] * 3, SMALL_ROWS)


def kernel(x, p, positions, g_mix, w_in, sink, g_q, w_uq, g_kv, w_ukv, w_br_a, w_br_b, w_out, g_ple, w_ple_gate, w_ple_proj, g_final, loss_target, m_g_mix, m_w_in, m_sink, m_g_q, m_w_uq, m_g_kv, m_w_ukv, m_w_br_a, m_w_br_b, m_w_out, m_g_ple, m_w_ple_gate, m_w_ple_proj, m_g_final, v_g_mix, v_w_in, v_sink, v_g_q, v_w_uq, v_g_kv, v_w_ukv, v_w_br_a, v_w_br_b, v_w_out, v_g_ple, v_w_ple_gate, v_w_ple_proj, v_g_final):
    w = dict(g_mix=g_mix, w_in=w_in, sink=sink, g_q=g_q, w_uq=w_uq, g_kv=g_kv, w_ukv=w_ukv, w_br_a=w_br_a,
             w_br_b=w_br_b, w_out=w_out, g_ple=g_ple, w_ple_gate=w_ple_gate, w_ple_proj=w_ple_proj, g_final=g_final)
    m = dict(g_mix=m_g_mix, w_in=m_w_in, sink=m_sink, g_q=m_g_q, w_uq=m_w_uq, g_kv=m_g_kv, w_ukv=m_w_ukv,
             w_br_a=m_w_br_a, w_br_b=m_w_br_b, w_out=m_w_out, g_ple=m_g_ple, w_ple_gate=m_w_ple_gate,
             w_ple_proj=m_w_ple_proj, g_final=m_g_final)
    v = dict(g_mix=v_g_mix, w_in=v_w_in, sink=v_sink, g_q=v_g_q, w_uq=v_w_uq, g_kv=v_g_kv, w_ukv=v_w_ukv,
             w_br_a=v_w_br_a, w_br_b=v_w_br_b, w_out=v_w_out, g_ple=v_g_ple, w_ple_gate=v_w_ple_gate,
             w_ple_proj=v_w_ple_proj, g_final=v_g_final)
    wfull = _gather_full(w)
    sm = {name: w[name] for name in SMALL}
    loss_row, grad_x, layer_grads, dg_final = _local_step(x, p, positions, wfull, sm, loss_target)
    loss = lax.psum(loss_row[0, 0], ("x", "y", "c"))
    res = _update(layer_grads, dg_final, w, m, v)
    return (loss, grad_x, *[res[name][kind] for kind in range(4) for name in WEIGHT_NAMES])


def _gather_full(w):
    gathered = _gather_weights([w[name].astype(BF16) for name, _ in SHARDED])
    return {name: [jnp.concatenate([gathered[k][q, layer] for q in range(N_CHIPS)], axis=axis - 1)
                   for layer in range(DEPTH)] for k, (name, axis) in enumerate(SHARDED)}


def _update(layer_grads, dg_final, w, m, v):
    small_shapes = {name: w[name].shape for name in SMALL}
    cflag = jnp.full((1, LANES), lax.axis_index("c"), F32)
    chipflag = jnp.full((1, LANES), 2 * lax.axis_index("x") + lax.axis_index("y"), F32)

    slots = [[_to_slots(layer_grads[layer][name], axis - 1) for name, axis in SHARDED] for layer in range(DEPTH)]
    theirs = _pair_exchange(slots[0], slots[1])
    pair = [_pair_sum("pair_sum_" + name, slots[0][k], slots[1][k], theirs[k], cflag)
            for k, (name, _) in enumerate(SHARDED)]
    landed = _chip_exchange([bf16 for _, bf16 in pair])
    mine = [_chip_sum("chip_sum_" + name, pair[k][0], landed[k], chipflag) for k, (name, _) in enumerate(SHARDED)]
    other = _pair_broadcast(mine)
    res = {name: _adamw_big("adamw_" + name, mine[k], other[k], cflag, w[name], m[name], v[name])
           for k, (name, _) in enumerate(SHARDED)}

    gsmall = {name: jnp.stack([layer_grads[layer][name] for layer in range(DEPTH)]) for name in SMALL[:-1]}
    gsmall['g_final'] = dg_final
    gsum = _small_allreduce(_pack_small(gsmall))
    small = (gsum,) + tuple(_adamw_small(gsum, _pack_small(w), _pack_small(m), _pack_small(v)))
    for name, arrs in zip(SMALL, zip(*[[_unpack_small(a, small_shapes)[n] for n in SMALL] for a in small])):
        res[name] = arrs
    return res
```

```python
import functools
import math

import jax
import jax.numpy as jnp
from jax import lax
from jax.experimental import pallas as pl
from jax.experimental.pallas import tpu as pltpu

F32 = jnp.float32
BF16 = jnp.bfloat16

D_MODEL = 1024
DEPTH = 2
PLE_DIM = 256
BLOCK = 128
EPS = 1e-6
NEG = -1e30
HEADS = 8
SWA_KV_HEADS = 2
HEAD_DIM = 64
LANES = 128
HPAD = HEADS * LANES
MLA_QK = 96
MLA_ROPE = 32
MLA_Q_LORA = 256
MLA_KV_LORA = 128
ROPE_THETA = 10000.0
IN_SIZES = (512, 128, 128, 512, 256, 128, 32, 512, 1024, 1024)

Z_MA, Z_MB, Z_AQ, Z_AGATE, Z_BGATE = 0, 1024, 2048, 3072, 4096
Z_AK, Z_AV, Z_BQD, Z_BKVD, Z_BKR = 5120, 5376, 5632, 5888, 6016
Z_WIDTH = 6144

ADAM_LR, ADAM_B1, ADAM_B2, ADAM_EPS, ADAM_WD, ADAM_STEP = 0.001, 0.9, 0.999, 1e-08, 0.01, 10

VMEM_LIMIT = 56 * 1024 * 1024
MESH = pl.DeviceIdType.MESH

WEIGHT_NAMES = ('g_mix', 'w_in', 'sink', 'g_q', 'w_uq', 'g_kv', 'w_ukv', 'w_br_a', 'w_br_b',
                'w_out', 'g_ple', 'w_ple_gate', 'w_ple_proj', 'g_final')
SHARDED = (('w_in', 2), ('w_uq', 2), ('w_ukv', 2), ('w_br_a', 2), ('w_br_b', 2),
           ('w_out', 1), ('w_ple_gate', 1), ('w_ple_proj', 2))
SMALL = ('g_mix', 'sink', 'g_q', 'g_kv', 'g_ple', 'g_final')
N_CHIPS = 4


def _params(sem):
    return pltpu.CompilerParams(dimension_semantics=sem, vmem_limit_bytes=VMEM_LIMIT)


MM_TN = 512
ROW_TILE = 512
BIG_WEIGHT_BYTES = 8 * 1024 * 1024


def _row_tile(rows, weight_bytes=0):
    tm = ROW_TILE // 2 if weight_bytes > BIG_WEIGHT_BYTES else ROW_TILE
    return min(tm, rows)


def _ew(name, body, ins, outs, rows, accs=(), mms=(), tm=None):
    n_mm, n_in, n_out = len(mms), len(ins), len(outs)
    if tm is None:
        tm = _row_tile(rows, sum(b.size * b.dtype.itemsize for _, b in mms))
    in_specs, args = [], []
    for a, b in mms:
        in_specs += [pl.BlockSpec((tm, a.shape[1]), lambda i: (i, 0)), pl.BlockSpec(b.shape, lambda i: (0, 0))]
        args += [a, b]
    for arr, width, cb in ins:
        if width is None:
            in_specs.append(pl.BlockSpec(arr.shape, lambda i, nd=arr.ndim: (0,) * nd))
        else:
            in_specs.append(pl.BlockSpec((tm, width), lambda i, cb=cb: (i, cb)))
        args.append(arr)
    out_shape, out_specs, aliases = [], [], {}
    for k, out in enumerate(outs):
        if len(out) == 4:
            aliases[len(args)] = k
            in_specs.append(pl.BlockSpec(memory_space=pl.ANY))
            args.append(out[2])
            out_shape.append(jax.ShapeDtypeStruct(out[2].shape, out[2].dtype))
            out_specs.append(pl.BlockSpec((tm, out[0]), lambda i, cb=out[3]: (i, cb)))
        else:
            out_shape.append(jax.ShapeDtypeStruct((rows, out[0]), out[1]))
            out_specs.append(pl.BlockSpec((tm, out[0]), lambda i: (i, 0)))
    n_in += len(aliases)
    out_shape += [jax.ShapeDtypeStruct(s, F32) for s in accs]
    out_specs += [pl.BlockSpec(s, lambda i: (0, 0)) for s in accs]

    def kern(*refs):
        mm_refs, refs = refs[:2 * n_mm], refs[2 * n_mm:]
        in_refs, out_refs = refs[:n_in - len(aliases)], refs[n_in:n_in + n_out]
        acc_refs, prod_refs = refs[n_in + n_out:n_in + n_out + len(accs)], refs[n_in + n_out + len(accs):]
        if acc_refs:
            @pl.when(pl.program_id(0) == 0)
            def _():
                for r in acc_refs:
                    r[...] = jnp.zeros_like(r)
        for k in range(n_mm):
            a_ref, b_ref, prod = mm_refs[2 * k], mm_refs[2 * k + 1], prod_refs[k]
            av = a_ref[...].astype(BF16)
            n = b_ref.shape[1]
            tn = min(MM_TN, n)
            for j in range(n // tn):
                cols = slice(j * tn, (j + 1) * tn)
                prod[:, cols] = jnp.dot(av, b_ref[:, cols], preferred_element_type=F32)
        body(tuple(prod_refs) + tuple(in_refs), out_refs, acc_refs)

    scratch = [pltpu.VMEM((tm, b.shape[1]), F32) for _, b in mms]
    res = pl.pallas_call(kern, name=name, grid=(rows // tm,), in_specs=in_specs, out_specs=out_specs,
                         out_shape=out_shape, scratch_shapes=scratch, input_output_aliases=aliases,
                         compiler_params=_params(("arbitrary",)))(*args)
    return res


def _rms_fwd(name, x, width, cb, g, rows):
    def body(ins, outs, _):
        xv = ins[0][...].astype(F32)
        r = lax.rsqrt(jnp.mean(xv * xv, axis=-1, keepdims=True) + EPS)
        outs[0][...] = ((xv * r) * ins[1][...]).astype(BF16)
    return _ew(name, body, [(x, width, cb), (g.reshape(1, width), None, None)], [(width, BF16)], rows)[0]


def _rms_bwd(name, x, width, cb, g, dh_mm, rows, out_dtype, dres=None, into=()):
    def body(ins, outs, accs):
        dhv, xv, gv = ins[0][...], ins[1][...].astype(F32), ins[2][...]
        r = lax.rsqrt(jnp.mean(xv * xv, axis=-1, keepdims=True) + EPS)
        xhat = xv * r
        accs[0][...] += jnp.sum(dhv * xhat, axis=0, keepdims=True)
        dy = dhv * gv
        dx = r * (dy - xhat * jnp.mean(dy * xhat, axis=-1, keepdims=True))
        if dres is not None:
            dx = dx + ins[3][...]
        outs[0][...] = dx.astype(out_dtype)
    ins = [(x, width, cb), (g.reshape(1, width), None, None)]
    if dres is not None:
        ins.append((dres, width, 0))
    return _ew(name, body, ins, [(width, out_dtype) + tuple(into)], rows, accs=[(1, width)], mms=[dh_mm])


def _mm(name, a, b, out_dtype, residual=None, f32_cols=None, tn=MM_TN):
    M, K = a.shape
    N = b.shape[1]
    tm, tn = _row_tile(M, b.size * b.dtype.itemsize), min(tn, N)
    has_res = residual is not None
    c0, cw = f32_cols if f32_cols else (0, 0)

    def kern(*refs):
        a_ref, b_ref = refs[0], refs[1]
        o_ref = refs[3] if has_res else refs[2]
        av = a_ref[...].astype(BF16)
        for j in range(N // tn):
            cols = slice(j * tn, (j + 1) * tn)
            part = jnp.dot(av, b_ref[:, cols], preferred_element_type=F32)
            if has_res:
                part = part + refs[2][:, cols]
            o_ref[:, cols] = part.astype(o_ref.dtype)
            if c0 <= j * tn and (j + 1) * tn <= c0 + cw:
                refs[-1][:, j * tn - c0:(j + 1) * tn - c0] = part

    in_specs = [pl.BlockSpec((tm, K), lambda i: (i, 0)), pl.BlockSpec((K, N), lambda i: (0, 0))]
    args = [a, b]
    if has_res:
        in_specs.append(pl.BlockSpec((tm, N), lambda i: (i, 0)))
        args.append(residual)
    out_specs = [pl.BlockSpec((tm, N), lambda i: (i, 0))]
    out_shape = [jax.ShapeDtypeStruct((M, N), out_dtype)]
    if f32_cols:
        assert c0 % tn == 0 and cw % tn == 0
        out_specs.append(pl.BlockSpec((tm, cw), lambda i: (i, 0)))
        out_shape.append(jax.ShapeDtypeStruct((M, cw), F32))
    res = pl.pallas_call(kern, name=name, grid=(M // tm,), in_specs=in_specs, out_specs=out_specs,
                         out_shape=out_shape, compiler_params=_params(("parallel",)))(*args)
    return res if f32_cols else res[0]


def _mm_tn(name, a, b, tk=512, tn=2048):
    T, M = a.shape
    N = b.shape[1]
    tn, tk = min(tn, N), min(tk, T)

    def kern(a_ref, b_ref, o_ref):
        k = pl.program_id(1)
        part = _dot_tn(a_ref[...].astype(BF16), b_ref[...].astype(BF16))

        @pl.when(k == 0)
        def _():
            o_ref[...] = part

        @pl.when(k > 0)
        def _():
            o_ref[...] += part

    return pl.pallas_call(
        kern, name=name, grid=(N // tn, T // tk),
        in_specs=[pl.BlockSpec((tk, M), lambda j, k: (k, 0)), pl.BlockSpec((tk, tn), lambda j, k: (k, j))],
        out_specs=pl.BlockSpec((M, tn), lambda j, k: (0, j)),
        out_shape=jax.ShapeDtypeStruct((M, N), F32),
        compiler_params=_params(("parallel", "arbitrary")))(a, b)


def _dot_nt(a, b):
    return lax.dot_general(a, b, (((1,), (1,)), ((), ())), preferred_element_type=F32)


def _dot_tn(a, b):
    return lax.dot_general(a, b, (((0,), (0,)), ((), ())), preferred_element_type=F32)


SWA_SCALE = HEAD_DIM ** -0.5


def _swa_band(n, pq_ref, pkp_ref, pkc_ref):
    posk = jnp.concatenate([pkp_ref[0], pkc_ref[0]], axis=1)
    dist = (pq_ref[...] - posk).astype(F32)
    qi = lax.broadcasted_iota(jnp.int32, (BLOCK, 2 * BLOCK), 0)
    kj = lax.broadcasted_iota(jnp.int32, (BLOCK, 2 * BLOCK), 1)
    t_abs = n * BLOCK + qi
    s_abs = n * BLOCK - BLOCK + kj
    return dist, (s_abs >= 0) & (s_abs <= t_abs) & (t_abs - s_abs < BLOCK)


SWA_GROUP = HEADS // SWA_KV_HEADS


def _swa_group_q(q_all, g):
    heads = range(g * SWA_GROUP, (g + 1) * SWA_GROUP)
    return jnp.concatenate([(q_all[:, h * LANES:(h + 1) * LANES] * SWA_SCALE).astype(BF16) for h in heads], axis=0)


def _swa_mask(s, dist, valid, h):
    return jnp.where(valid, s - (2.0 ** -(h + 1)) * dist, NEG)


def _swa_specs(nb):
    prev = lambda b, n: b * nb + jnp.maximum(n - 1, 0)
    own = lambda b, n: b * nb + n
    return [
        pl.BlockSpec((BLOCK, HPAD), lambda b, n: (own(b, n), Z_AQ // HPAD)),
        pl.BlockSpec((BLOCK, 256), lambda b, n: (prev(b, n), Z_AK // 256)),
        pl.BlockSpec((BLOCK, 256), lambda b, n: (own(b, n), Z_AK // 256)),
        pl.BlockSpec((BLOCK, 256), lambda b, n: (prev(b, n), Z_AV // 256)),
        pl.BlockSpec((BLOCK, 256), lambda b, n: (own(b, n), Z_AV // 256)),
        pl.BlockSpec((BLOCK, 1), lambda b, n: (own(b, n), 0)),
        pl.BlockSpec((1, 1, BLOCK), lambda b, n: (prev(b, n), 0, 0)),
        pl.BlockSpec((1, 1, BLOCK), lambda b, n: (own(b, n), 0, 0)),
    ]


def _swa_fwd(z, gate, pos_col, pos_row, sink_row, B, S):
    nb = S // BLOCK
    T = B * S

    def kern(q_ref, kp_ref, kc_ref, vp_ref, vc_ref, pq_ref, pkp_ref, pkc_ref, gate_ref, sink_ref,
             oraw_ref, og_ref, lse_ref):
        q_all = q_ref[...]
        kb = jnp.concatenate([kp_ref[...], kc_ref[...]], axis=0).astype(BF16)
        vb = jnp.concatenate([vp_ref[...], vc_ref[...]], axis=0).astype(BF16)
        dist, valid = _swa_band(pl.program_id(1), pq_ref, pkp_ref, pkc_ref)
        lane = lax.broadcasted_iota(jnp.int32, (BLOCK, LANES), 1)
        lse_all = jnp.zeros((BLOCK, LANES), F32)
        for grp in range(SWA_KV_HEADS):
            gcols = slice(grp * LANES, (grp + 1) * LANES)
            s_all = _dot_nt(_swa_group_q(q_all, grp), kb[:, gcols])
            probs = []
            for hh in range(SWA_GROUP):
                h = grp * SWA_GROUP + hh
                s = _swa_mask(s_all[hh * BLOCK:(hh + 1) * BLOCK], dist, valid, h)
                sink_h = sink_ref[0:1, h:h + 1]
                m = jnp.maximum(jnp.max(s, axis=-1, keepdims=True), sink_h)
                e = jnp.exp(s - m)
                denom = jnp.sum(e, axis=-1, keepdims=True) + jnp.exp(sink_h - m)
                probs.append((e * (1.0 / denom)).astype(BF16))
                lse_all = jnp.where(lane == h, m + jnp.log(denom), lse_all)
            o_all = jnp.dot(jnp.concatenate(probs, axis=0), vb[:, gcols], preferred_element_type=F32)
            for hh in range(SWA_GROUP):
                cols = slice((grp * SWA_GROUP + hh) * LANES, (grp * SWA_GROUP + hh + 1) * LANES)
                o = o_all[hh * BLOCK:(hh + 1) * BLOCK]
                oraw_ref[:, cols] = o
                g = gate_ref[:, cols].astype(F32)
                og_ref[:, cols] = (o * (g * jax.nn.sigmoid(g))).astype(BF16)
        lse_ref[...] = lse_all

    own = lambda b, n: b * nb + n
    in_specs = _swa_specs(nb) + [
        pl.BlockSpec((BLOCK, HPAD), lambda b, n: (own(b, n), 0)),
        pl.BlockSpec((1, LANES), lambda b, n: (0, 0)),
    ]
    out_specs = [pl.BlockSpec((BLOCK, HPAD), lambda b, n: (own(b, n), 0)),
                 pl.BlockSpec((BLOCK, HPAD), lambda b, n: (own(b, n), 0)),
                 pl.BlockSpec((BLOCK, LANES), lambda b, n: (own(b, n), 0))]
    out_shape = [jax.ShapeDtypeStruct((T, HPAD), F32), jax.ShapeDtypeStruct((T, HPAD), BF16),
                 jax.ShapeDtypeStruct((T, LANES), F32)]
    return pl.pallas_call(kern, name="swa_fwd", grid=(B, nb), in_specs=in_specs, out_specs=out_specs,
                          out_shape=out_shape, compiler_params=_params(("parallel", "arbitrary")))(
        z, z, z, z, z, pos_col, pos_row, pos_row, gate, sink_row)


def _swa_bwd(z, pos_col, pos_row, sink_row, lse, do_raw, delta, dz, B, S):
    nb = S // BLOCK
    T = B * S

    def kern(q_ref, kp_ref, kc_ref, vp_ref, vc_ref, pq_ref, pkp_ref, pkc_ref, sink_ref, lse_ref, do_ref,
             delta_ref, dz_ref, dq_ref, dk_ref, dv_ref, dsink_ref):
        b, n = pl.program_id(0), pl.program_id(1)

        @pl.when(n == 0)
        def _():
            dk_ref[...] = jnp.zeros_like(dk_ref)
            dv_ref[...] = jnp.zeros_like(dv_ref)

        @pl.when((b == 0) & (n == 0))
        def _():
            dsink_ref[...] = jnp.zeros_like(dsink_ref)

        q_all = q_ref[...]
        kb = jnp.concatenate([kp_ref[...], kc_ref[...]], axis=0).astype(BF16)
        vb = jnp.concatenate([vp_ref[...], vc_ref[...]], axis=0).astype(BF16)
        dist, valid = _swa_band(n, pq_ref, pkp_ref, pkc_ref)
        lane1 = lax.broadcasted_iota(jnp.int32, (1, LANES), 1)
        dsink = jnp.zeros((1, LANES), F32)
        dk_band, dv_band = [], []
        for grp in range(SWA_KV_HEADS):
            gcols = slice(grp * LANES, (grp + 1) * LANES)
            heads = range(grp * SWA_GROUP, (grp + 1) * SWA_GROUP)
            qg = _swa_group_q(q_all, grp)
            dog = jnp.concatenate([do_ref[:, h * LANES:(h + 1) * LANES] for h in heads], axis=0)
            s_all = _dot_nt(qg, kb[:, gcols])
            dp_all = _dot_nt(dog, vb[:, gcols])
            ps, dss = [], []
            for hh, h in enumerate(heads):
                blk = slice(hh * BLOCK, (hh + 1) * BLOCK)
                lse_h = lse_ref[:, h:h + 1]
                delta_h = delta_ref[:, h * LANES:h * LANES + 1]
                p = jnp.exp(_swa_mask(s_all[blk], dist, valid, h) - lse_h)
                ps.append(p.astype(BF16))
                dss.append((p * (dp_all[blk] - delta_h)).astype(BF16))
                psink = jnp.exp(sink_ref[0:1, h:h + 1] - lse_h)
                dsink = dsink + jnp.where(lane1 == h, -jnp.sum(psink * delta_h, axis=0, keepdims=True), 0.0)
            dsg = jnp.concatenate(dss, axis=0)
            dq_all = jnp.dot(dsg, kb[:, gcols], preferred_element_type=F32) * SWA_SCALE
            for hh, h in enumerate(heads):
                dq_ref[:, h * LANES:(h + 1) * LANES] = dq_all[hh * BLOCK:(hh + 1) * BLOCK].astype(BF16)
            dk_band.append(jnp.dot(qg.T, dsg, preferred_element_type=F32).T)
            dv_band.append(jnp.dot(dog.T, jnp.concatenate(ps, axis=0), preferred_element_type=F32).T)
        dsink_ref[...] += dsink
        dkb = jnp.concatenate(dk_band, axis=1)
        dvb = jnp.concatenate(dv_band, axis=1)
        r_prev = pl.ds(pl.multiple_of(jnp.maximum(n - 1, 0) * BLOCK, BLOCK), BLOCK)
        r_own = pl.ds(pl.multiple_of(n * BLOCK, BLOCK), BLOCK)
        dk_ref[r_prev, :] += dkb[:BLOCK]
        dk_ref[r_own, :] += dkb[BLOCK:]
        dv_ref[r_prev, :] += dvb[:BLOCK]
        dv_ref[r_own, :] += dvb[BLOCK:]

    own = lambda b, n: b * nb + n
    in_specs = _swa_specs(nb) + [
        pl.BlockSpec((1, LANES), lambda b, n: (0, 0)),
        pl.BlockSpec((BLOCK, LANES), lambda b, n: (own(b, n), 0)),
        pl.BlockSpec((BLOCK, HPAD), lambda b, n: (own(b, n), 0)),
        pl.BlockSpec((BLOCK, HPAD), lambda b, n: (own(b, n), 0)),
        pl.BlockSpec(memory_space=pl.ANY),
    ]
    out_specs = [pl.BlockSpec((BLOCK, HPAD), lambda b, n: (own(b, n), Z_AQ // HPAD)),
                 pl.BlockSpec((S, 256), lambda b, n: (b, 0)),
                 pl.BlockSpec((S, 256), lambda b, n: (b, 0)),
                 pl.BlockSpec((1, LANES), lambda b, n: (0, 0))]
    out_shape = [jax.ShapeDtypeStruct(dz.shape, dz.dtype), jax.ShapeDtypeStruct((T, 256), F32),
                 jax.ShapeDtypeStruct((T, 256), F32), jax.ShapeDtypeStruct((1, LANES), F32)]
    return pl.pallas_call(kern, name="swa_bwd", grid=(B, nb), in_specs=in_specs, out_specs=out_specs,
                          out_shape=out_shape, input_output_aliases={len(in_specs) - 1: 0},
                          compiler_params=_params(("arbitrary", "arbitrary")))(
        z, z, z, z, z, pos_col, pos_row, pos_row, sink_row, lse, do_raw, delta, dz)


MLA_T = 256
MLA_HG = 4
MLA_W = MLA_HG * LANES
MLA_SCALE = MLA_QK ** -0.5
LOG2E = 1.4426950408889634
MLA_QSCALE = MLA_SCALE * LOG2E


def _causal(s):
    row = lax.broadcasted_iota(jnp.int32, s.shape, 0)
    col = lax.broadcasted_iota(jnp.int32, s.shape, 1)
    return jnp.where(col <= row, s, NEG)


def _mla_fwd(q, k, v, z, B, S):
    T = B * S
    nq = S // MLA_T

    def kern(q_ref, k_ref, v_ref, gate_ref, oraw_ref, og_ref, lse_ref):
        i = pl.program_id(2)

        def scores(j):
            rows = pl.ds(pl.multiple_of(j * MLA_T, MLA_T), MLA_T)
            return tuple(_dot_nt(k_ref[rows, hh * LANES:(hh + 1) * LANES], q_ref[:, hh * LANES:(hh + 1) * LANES])
                         for hh in range(MLA_HG))

        def update(j, ss, state):
            rows = pl.ds(pl.multiple_of(j * MLA_T, MLA_T), MLA_T)
            out = []
            for hh in range(MLA_HG):
                (m, l, acc), s = state[hh], ss[hh]
                m_new = jnp.maximum(m, jnp.max(s, axis=0, keepdims=True))
                alpha = jnp.exp2(m - m_new)
                p = jnp.exp2(s - m_new)
                l = alpha * l + jnp.sum(p, axis=0, keepdims=True)
                pv = jnp.dot(v_ref[rows, hh * LANES:(hh + 1) * LANES].T, p.astype(BF16), preferred_element_type=F32)
                out.append((m_new, l, alpha * acc + pv))
            return tuple(out)

        def body(j, carry):
            state, ss = carry
            s_next = scores(j + 1)
            return update(j, ss, state), s_next

        def causal_t(s):
            key = lax.broadcasted_iota(jnp.int32, s.shape, 0)
            query = lax.broadcasted_iota(jnp.int32, s.shape, 1)
            return jnp.where(key <= query, s, NEG)

        init = tuple((jnp.full((1, MLA_T), NEG, F32), jnp.zeros((1, MLA_T), F32), jnp.zeros((LANES, MLA_T), F32))
                     for _ in range(MLA_HG))
        state, ss = lax.fori_loop(0, i, body, (init, scores(0)))
        state = update(i, tuple(causal_t(s) for s in ss), state)
        for hh in range(MLA_HG):
            m, l, acc = state[hh]
            cols = slice(hh * LANES, (hh + 1) * LANES)
            o = (acc * (1.0 / l)).T
            oraw_ref[:, cols] = o
            g = gate_ref[:, cols].astype(F32)
            og_ref[:, cols] = (o * (g * jax.nn.sigmoid(g))).astype(BF16)
            lse_ref[:, cols] = jnp.broadcast_to(m + jnp.log2(l), (LANES, MLA_T)).T

    blk = lambda b, h, i: (b * nq + i, h)
    in_specs = [pl.BlockSpec((MLA_T, MLA_W), blk),
                pl.BlockSpec((S, MLA_W), lambda b, h, i: (b, h)),
                pl.BlockSpec((S, MLA_W), lambda b, h, i: (b, h)),
                pl.BlockSpec((MLA_T, MLA_W), lambda b, h, i: (b * nq + i, Z_BGATE // MLA_W + h))]
    out_specs = [pl.BlockSpec((MLA_T, MLA_W), blk)] * 3
    out_shape = [jax.ShapeDtypeStruct((T, HPAD), F32), jax.ShapeDtypeStruct((T, HPAD), BF16),
                 jax.ShapeDtypeStruct((T, HPAD), F32)]
    return pl.pallas_call(kern, name="mla_fwd", grid=(B, HEADS // MLA_HG, nq), in_specs=in_specs,
                          out_specs=out_specs, out_shape=out_shape,
                          compiler_params=_params(("parallel", "parallel", "arbitrary")))(q, k, v, z)


def _mla_bwd(q, k, v, do_raw, lse, delta, B, S):
    T = B * S
    nk = S // MLA_T

    def kern(q_ref, k_ref, v_ref, do_ref, lse_ref, delta_ref, dq_ref, dk_ref, dv_ref, dk_acc, dv_acc):
        j = pl.program_id(2)

        @pl.when(j == 0)
        def _():
            dq_ref[...] = jnp.zeros_like(dq_ref)

        dk_acc[...] = jnp.zeros_like(dk_acc)
        dv_acc[...] = jnp.zeros_like(dv_acc)

        def step(i, masked):
            rows = pl.ds(pl.multiple_of(i * MLA_T, MLA_T), MLA_T)
            for hh in range(MLA_HG):
                cols = slice(hh * LANES, (hh + 1) * LANES)
                kv, vv = k_ref[:, cols], v_ref[:, cols]
                qv, do = q_ref[rows, cols], do_ref[rows, cols]
                s = _dot_nt(qv, kv)
                if masked:
                    s = _causal(s)
                p = jnp.exp2(s - lse_ref[rows, hh * LANES:hh * LANES + 1])
                dp = _dot_nt(do, vv)
                ds = (p * (dp - delta_ref[rows, hh * LANES:hh * LANES + 1])).astype(BF16)
                dv_acc[hh] += jnp.dot(do.T, p.astype(BF16), preferred_element_type=F32)
                dk_acc[hh] += jnp.dot(qv.T, ds, preferred_element_type=F32)
                dq_ref[rows, cols] += jnp.dot(ds, kv, preferred_element_type=F32)

        step(j, True)

        def body(i, c):
            step(i, False)
            return c

        lax.fori_loop(j + 1, nk, body, 0)
        for hh in range(MLA_HG):
            cols = slice(hh * LANES, (hh + 1) * LANES)
            dk_ref[:, cols] = dk_acc[hh].T * (1.0 / LOG2E)
            dv_ref[:, cols] = dv_acc[hh].T

    whole = lambda b, h, j: (b, h)
    tile = lambda b, h, j: (b * nk + j, h)
    in_specs = [pl.BlockSpec((S, MLA_W), whole), pl.BlockSpec((MLA_T, MLA_W), tile),
                pl.BlockSpec((MLA_T, MLA_W), tile), pl.BlockSpec((S, MLA_W), whole),
                pl.BlockSpec((S, MLA_W), whole), pl.BlockSpec((S, MLA_W), whole)]
    out_specs = [pl.BlockSpec((S, MLA_W), whole), pl.BlockSpec((MLA_T, MLA_W), tile),
                 pl.BlockSpec((MLA_T, MLA_W), tile)]
    out_shape = [jax.ShapeDtypeStruct((T, HPAD), F32)] * 3
    return pl.pallas_call(kern, name="mla_bwd", grid=(B, HEADS // MLA_HG, nk), in_specs=in_specs,
                          out_specs=out_specs, out_shape=out_shape,
                          scratch_shapes=[pltpu.VMEM((MLA_HG, LANES, MLA_T), F32)] * 2,
                          compiler_params=_params(("parallel", "parallel", "arbitrary")))(
        q, k, v, do_raw, lse, delta)


def _rope_tables(pos_col, inv_lane, rows):
    def body(ins, outs, _):
        ang = ins[0][...].astype(F32) * ins[1][...]
        lane = lax.broadcasted_iota(jnp.int32, ang.shape, 1)
        cos, sin = jnp.cos(ang), jnp.sin(ang)
        first = (lane >= HEAD_DIM) & (lane < HEAD_DIM + MLA_ROPE // 2)
        second = (lane >= HEAD_DIM + MLA_ROPE // 2) & (lane < MLA_QK)
        outs[0][...] = jnp.where(lane < HEAD_DIM, 1.0, jnp.where(lane < MLA_QK, cos, 0.0))
        outs[1][...] = jnp.where(first, -sin, 0.0)
        outs[2][...] = jnp.where(second, sin, 0.0)
    return _ew("rope_tables", body, [(pos_col, 1, 0), (inv_lane, None, None)], [(LANES, F32)] * 3, rows)


def _rope(x, c, s1, s2):
    return x * c + pltpu.roll(x, 112, 1) * s1 + pltpu.roll(x, 16, 1) * s2


def _rope_t(d, c, s1, s2):
    return d * c + pltpu.roll(d * s1, 16, 1) + pltpu.roll(d * s2, 112, 1)


def _mla_prep(qdn, w_uq, kvdn, w_ukv, z, tabs, rows):
    def body(ins, outs, _):
        q_pre, kv_pre = ins[0], ins[1]
        c, s1, s2 = ins[3][...], ins[4][...], ins[5][...]
        kr = _rope(ins[2][...].astype(F32), c, s1, s2)
        for h in range(HEADS):
            cols = slice(h * LANES, (h + 1) * LANES)
            outs[0][:, cols] = (_rope(q_pre[:, cols], c, s1, s2) * MLA_QSCALE).astype(BF16)
            outs[1][:, cols] = (kv_pre[:, cols] + kr).astype(BF16)
        outs[2][...] = kv_pre[:, HPAD:].astype(BF16)
    ins = [(z, LANES, Z_BKR // LANES), (tabs[0], LANES, 0), (tabs[1], LANES, 0), (tabs[2], LANES, 0)]
    return _ew("mla_prep", body, ins, [(HPAD, BF16)] * 3, rows, mms=[(qdn, w_uq), (kvdn, w_ukv)])


def _mla_prep_bwd(dq, dk, dv, tabs, dz, rows):
    def body(ins, outs, _):
        c, s1, s2 = ins[3][...], ins[4][...], ins[5][...]
        lane = lax.broadcasted_iota(jnp.int32, c.shape, 1)
        dkr = jnp.zeros(c.shape, F32)
        for h in range(HEADS):
            cols = slice(h * LANES, (h + 1) * LANES)
            outs[0][:, cols] = _rope_t(ins[0][:, cols] * MLA_SCALE, c, s1, s2).astype(BF16)
            dkh = ins[1][:, cols]
            outs[1][:, cols] = jnp.where(lane < HEAD_DIM, dkh, 0.0).astype(BF16)
            dkr = dkr + dkh
        outs[1][:, HPAD:] = ins[2][...].astype(BF16)
        live = (lane >= HEAD_DIM) & (lane < MLA_QK)
        outs[2][...] = jnp.where(live, _rope_t(jnp.where(live, dkr, 0.0), c, s1, s2), 0.0).astype(BF16)
    ins = [(dq, HPAD, 0), (dk, HPAD, 0), (dv, HPAD, 0), (tabs[0], LANES, 0), (tabs[1], LANES, 0),
           (tabs[2], LANES, 0)]
    outs = [(HPAD, BF16), (2 * HPAD, BF16), (LANES, BF16, dz, Z_BKR // LANES)]
    return _ew("mla_prep_bwd", body, ins, outs, rows)


def _gate_bwd(name, d_o_mm, o_raw, gate, gate_cb, dz, dz_cb, rows):
    def body(ins, outs, _):
        for h in range(HEADS):
            cols = slice(h * LANES, (h + 1) * LANES)
            dog, o, g = ins[0][:, cols], ins[1][:, cols], ins[2][:, cols].astype(F32)
            sg = jax.nn.sigmoid(g)
            do = dog * (g * sg)
            outs[0][:, cols] = do.astype(BF16)
            outs[1][:, cols] = (dog * o * (sg * (1.0 + g * (1.0 - sg)))).astype(BF16)
            outs[2][:, cols] = jnp.broadcast_to(jnp.sum(do * o, axis=-1, keepdims=True), do.shape)
    ins = [(o_raw, HPAD, 0), (gate, HPAD, gate_cb)]
    outs = [(HPAD, BF16), (HPAD, BF16, dz, dz_cb), (HPAD, F32)]
    return _ew(name, body, ins, outs, rows, mms=[d_o_mm])


def _merge_out(ua, ub, z, w_out, x0, rows):
    tm = _row_tile(rows)

    def kern(ua_ref, ub_ref, ma_ref, mb_ref, w_ref, x0_ref, y_ref, x1_ref):
        ua_v, ub_v, m_a, m_b = (r[...].astype(F32) for r in (ua_ref, ub_ref, ma_ref, mb_ref))
        y = (jax.nn.sigmoid(m_a) * ua_v + jax.nn.sigmoid(m_b) * ub_v).astype(BF16)
        y_ref[...] = y
        for j in range(D_MODEL // MM_TN):
            cols = slice(j * MM_TN, (j + 1) * MM_TN)
            x1_ref[:, cols] = jnp.dot(y, w_ref[:, cols], preferred_element_type=F32) + x0_ref[:, cols]

    row = lambda cb: pl.BlockSpec((tm, D_MODEL), lambda i: (i, cb))
    return pl.pallas_call(
        kern, name="merge_out", grid=(rows // tm,),
        in_specs=[row(0), row(0), row(Z_MA // D_MODEL), row(Z_MB // D_MODEL),
                  pl.BlockSpec(w_out.shape, lambda i: (0, 0)), row(0)],
        out_specs=[row(0), row(0)],
        out_shape=[jax.ShapeDtypeStruct((rows, D_MODEL), BF16), jax.ShapeDtypeStruct((rows, D_MODEL), F32)],
        compiler_params=_params(("parallel",)))(ua, ub, z, z, w_out, x0)


def _merge_bwd(dy_mm, ua, ub, z, dz, rows):
    def body(ins, outs, _):
        dyv = ins[0][...]
        for idx in range(2):
            s = jax.nn.sigmoid(ins[3 + idx][...].astype(F32))
            outs[idx][...] = (dyv * s).astype(BF16)
            d_m = (dyv * ins[1 + idx][...].astype(F32) * (s * (1.0 - s))).astype(BF16)
            outs[2][:, idx * D_MODEL:(idx + 1) * D_MODEL] = d_m
    ins = [(ua, D_MODEL, 0), (ub, D_MODEL, 0), (z, D_MODEL, Z_MA // D_MODEL), (z, D_MODEL, Z_MB // D_MODEL)]
    outs = [(D_MODEL, BF16), (D_MODEL, BF16), (2 * D_MODEL, BF16, dz, Z_MA // (2 * D_MODEL))]
    return _ew("merge_bwd", body, ins, outs, rows, mms=[dy_mm])


def _kv_grad_cast(dk, dv, dz, rows):
    def body(ins, outs, _):
        outs[0][:, :256] = ins[0][...].astype(BF16)
        outs[0][:, 256:] = ins[1][...].astype(BF16)
    return _ew("kv_grad_cast", body, [(dk, 256, 0), (dv, 256, 0)], [(512, BF16, dz, Z_AK // 512)], rows)[0]


def _ple_fwd(x1, hn, w_pg, p, w_pp, rows):
    def body(ins, outs, _):
        u, e = ins[0][...], ins[1][...]
        outs[0][...] = ins[2][...] + jax.nn.sigmoid(u) * e
        outs[1][...] = u.astype(BF16)
        outs[2][...] = e.astype(BF16)
    return _ew("ple_fwd", body, [(x1, D_MODEL, 0)], [(D_MODEL, F32), (D_MODEL, BF16), (D_MODEL, BF16)], rows,
               mms=[(hn, w_pg), (p, w_pp)])


def _ple_bwd(dx2, u, e, rows):
    def body(ins, outs, _):
        d, s = ins[0][...], jax.nn.sigmoid(ins[1][...].astype(F32))
        outs[0][...] = (d * s).astype(BF16)
        outs[1][...] = (d * ins[2][...].astype(F32) * (s * (1.0 - s))).astype(BF16)
    return _ew("ple_bwd", body, [(dx2, D_MODEL, 0), (u, D_MODEL, 0), (e, D_MODEL, 0)],
               [(D_MODEL, BF16)] * 2, rows)


def _loss_head(x, g, target, rows):
    def body(ins, outs, accs):
        xv, gv = ins[0][...], ins[1][...]
        r = lax.rsqrt(jnp.mean(xv * xv, axis=-1, keepdims=True) + EPS)
        xhat = xv * r
        err = xhat * gv - ins[2][...]
        accs[0][...] += jnp.broadcast_to(0.5 * jnp.sum(jnp.mean(err * err, axis=-1, keepdims=True),
                                                       axis=0, keepdims=True), (1, LANES))
        dyv = err * (1.0 / D_MODEL)
        accs[1][...] += jnp.sum(dyv * xhat, axis=0, keepdims=True)
        dy = dyv * gv
        outs[0][...] = r * (dy - xhat * jnp.mean(dy * xhat, axis=-1, keepdims=True))
    ins = [(x, D_MODEL, 0), (g.reshape(1, D_MODEL), None, None), (target, D_MODEL, 0)]
    return _ew("loss_head", body, ins, [(D_MODEL, F32)], rows, accs=[(1, LANES), (1, D_MODEL)])


def _pad_heads_cols(w, n_heads, dim):
    k = w.shape[0]
    return jnp.pad(w.reshape(k, n_heads, dim), ((0, 0), (0, 0), (0, LANES - dim))).reshape(k, n_heads * LANES)


def _unpad_heads_cols(w, n_heads, dim):
    k = w.shape[0]
    return w.reshape(k, n_heads, LANES)[:, :, :dim].reshape(k, n_heads * dim)


def _layer_weights(w, i):
    segs = jnp.split(w['w_in'][i], list(_cumsum(IN_SIZES))[:-1], axis=1)
    a_q, a_k, a_v, a_gate, b_qd, b_kvd, b_kr, b_gate, m_a, m_b = segs
    kr = jnp.pad(b_kr, ((0, 0), (HEAD_DIM, LANES - MLA_QK)))
    w_in = jnp.concatenate([
        m_a, m_b, _pad_heads_cols(a_q, HEADS, HEAD_DIM), _pad_heads_cols(a_gate, HEADS, HEAD_DIM),
        _pad_heads_cols(b_gate, HEADS, HEAD_DIM), _pad_heads_cols(a_k, SWA_KV_HEADS, HEAD_DIM),
        _pad_heads_cols(a_v, SWA_KV_HEADS, HEAD_DIM), b_qd, b_kvd, kr], axis=1)
    w_uq = _pad_heads_cols(w['w_uq'][i], HEADS, MLA_QK)
    ukv = w['w_ukv'][i].reshape(MLA_KV_LORA, HEADS, 2 * HEAD_DIM)
    pad = ((0, 0), (0, 0), (0, HEAD_DIM))
    w_ukv = jnp.concatenate([jnp.pad(ukv[:, :, :HEAD_DIM], pad).reshape(MLA_KV_LORA, HPAD),
                             jnp.pad(ukv[:, :, HEAD_DIM:], pad).reshape(MLA_KV_LORA, HPAD)], axis=1)
    w_br_a = _pad_heads_cols(w['w_br_a'][i].T, HEADS, HEAD_DIM).T
    w_br_b = _pad_heads_cols(w['w_br_b'][i].T, HEADS, HEAD_DIM).T
    out = dict(w_in=w_in, w_uq=w_uq, w_ukv=w_ukv, w_br_a=w_br_a, w_br_b=w_br_b, w_out=w['w_out'][i],
               w_pg=w['w_ple_gate'][i], w_pp=w['w_ple_proj'][i])
    for name in ('w_in', 'w_uq', 'w_ukv', 'w_br_a', 'w_br_b', 'w_out', 'w_pg'):
        out[name + '_t'] = out[name].T
    return out


def _cumsum(sizes):
    acc, out = 0, []
    for s in sizes:
        acc += s
        out.append(acc)
    return out


def _unpad_grads(g):
    d = g['w_in']
    seg = lambda off, width: d[:, off:off + width]
    b_kr = seg(Z_BKR, LANES)[:, HEAD_DIM:MLA_QK]
    w_in = jnp.concatenate([
        _unpad_heads_cols(seg(Z_AQ, HPAD), HEADS, HEAD_DIM), _unpad_heads_cols(seg(Z_AK, 256), SWA_KV_HEADS, HEAD_DIM),
        _unpad_heads_cols(seg(Z_AV, 256), SWA_KV_HEADS, HEAD_DIM), _unpad_heads_cols(seg(Z_AGATE, HPAD), HEADS, HEAD_DIM),
        seg(Z_BQD, MLA_Q_LORA), seg(Z_BKVD, MLA_KV_LORA), b_kr, _unpad_heads_cols(seg(Z_BGATE, HPAD), HEADS, HEAD_DIM),
        seg(Z_MA, D_MODEL), seg(Z_MB, D_MODEL)], axis=1)
    w_uq = _unpad_heads_cols(g['w_uq'], HEADS, MLA_QK)
    ukv = g['w_ukv'].reshape(MLA_KV_LORA, 2, HEADS, LANES)[:, :, :, :HEAD_DIM]
    w_ukv = jnp.concatenate([ukv[:, 0], ukv[:, 1]], axis=-1).reshape(MLA_KV_LORA, HEADS * 2 * HEAD_DIM)
    w_br_a = _unpad_heads_cols(g['w_br_a'].T, HEADS, HEAD_DIM).T
    w_br_b = _unpad_heads_cols(g['w_br_b'].T, HEADS, HEAD_DIM).T
    return dict(w_in=w_in, w_uq=w_uq, w_ukv=w_ukv, w_br_a=w_br_a, w_br_b=w_br_b, w_out=g['w_out'],
                w_ple_gate=g['w_pg'], w_ple_proj=g['w_pp'], g_mix=g['g_mix'], sink=g['sink'], g_q=g['g_q'],
                g_kv=g['g_kv'], g_ple=g['g_ple'])


def _layer_fwd(x0, p_i, lw, sm, i, pos_col, pos_row, tabs, B, S):
    T = B * S
    h = _rms_fwd("norm_mix", x0, D_MODEL, 0, sm['g_mix'][i], T)
    z, a_gate = _mm("proj_in", h, lw['w_in'], BF16, f32_cols=(Z_AGATE, HPAD))
    sink_row = jnp.pad(sm['sink'][i], (0, LANES - HEADS)).reshape(1, LANES)
    oa_raw, oa, lse_a = _swa_fwd(z, a_gate, pos_col, pos_row, sink_row, B, S)
    qdn = _rms_fwd("norm_q", z, MLA_Q_LORA, Z_BQD // MLA_Q_LORA, sm['g_q'][i], T)
    kvdn = _rms_fwd("norm_kv", z, MLA_KV_LORA, Z_BKVD // MLA_KV_LORA, sm['g_kv'][i], T)
    qf, kf, vf = _mla_prep(qdn, lw['w_uq'], kvdn, lw['w_ukv'], z, tabs, T)
    ob_raw, ob, lse_b = _mla_fwd(qf, kf, vf, z, B, S)
    ua = _mm("proj_br_a", oa, lw['w_br_a'], BF16)
    ub = _mm("proj_br_b", ob, lw['w_br_b'], BF16)
    y, x1 = _merge_out(ua, ub, z, lw['w_out'], x0, T)
    hn = _rms_fwd("norm_ple", x1, D_MODEL, 0, sm['g_ple'][i], T)
    x2, u, e = _ple_fwd(x1, hn, lw['w_pg'], p_i, lw['w_pp'], T)
    saved = dict(x0=x0, h=h, z=z, a_gate=a_gate, sink_row=sink_row, oa_raw=oa_raw, oa=oa, lse_a=lse_a, qdn=qdn, kvdn=kvdn,
                 qf=qf, kf=kf, vf=vf, ob_raw=ob_raw, ob=ob, lse_b=lse_b, ua=ua, ub=ub, y=y, x1=x1, hn=hn,
                 u=u, e=e, p=p_i)
    return x2, saved


def _layer_bwd(dx2, sv, lw, sm, i, pos_col, pos_row, tabs, B, S):
    T = B * S
    z = sv['z']
    g = {}
    d_e, d_u = _ple_bwd(dx2, sv['u'], sv['e'], T)
    g['w_pp'] = _mm_tn("grad_pp", sv['p'], d_e)
    g['w_pg'] = _mm_tn("grad_pg", sv['hn'], d_u)
    dx1, g['g_ple'] = _rms_bwd("norm_ple_bwd", sv['x1'], D_MODEL, 0, sm['g_ple'][i], (d_u, lw['w_pg_t']), T, F32,
                               dres=dx2)
    g['w_out'] = _mm_tn("grad_out", sv['y'], dx1)
    dz = lax.empty((T, Z_WIDTH), BF16)
    d_ua, d_ub, dz = _merge_bwd((dx1, lw['w_out_t']), sv['ua'], sv['ub'], z, dz, T)
    g['w_br_a'] = _mm_tn("grad_br_a", sv['oa'], d_ua)
    g['w_br_b'] = _mm_tn("grad_br_b", sv['ob'], d_ub)
    dob_raw, dz, delta_b = _gate_bwd("gate_b_bwd", (d_ub, lw['w_br_b_t']), sv['ob_raw'], z, Z_BGATE // HPAD,
                                     dz, Z_BGATE // HPAD, T)
    dq, dk, dv = _mla_bwd(sv['qf'], sv['kf'], sv['vf'], dob_raw, sv['lse_b'], delta_b, B, S)
    dq_pre, dkv_pre, dz = _mla_prep_bwd(dq, dk, dv, tabs, dz, T)
    g['w_uq'] = _mm_tn("grad_uq", sv['qdn'], dq_pre)
    g['w_ukv'] = _mm_tn("grad_ukv", sv['kvdn'], dkv_pre)
    dz, g['g_q'] = _rms_bwd("norm_q_bwd", z, MLA_Q_LORA, Z_BQD // MLA_Q_LORA, sm['g_q'][i],
                            (dq_pre, lw['w_uq_t']), T, BF16, into=(dz, Z_BQD // MLA_Q_LORA))
    dz, g['g_kv'] = _rms_bwd("norm_kv_bwd", z, MLA_KV_LORA, Z_BKVD // MLA_KV_LORA, sm['g_kv'][i],
                             (dkv_pre, lw['w_ukv_t']), T, BF16, into=(dz, Z_BKVD // MLA_KV_LORA))
    doa_raw, dz, delta_a = _gate_bwd("gate_a_bwd", (d_ua, lw['w_br_a_t']), sv['oa_raw'], sv['a_gate'], 0,
                                     dz, Z_AGATE // HPAD, T)
    dz, d_ak, d_av, dsink = _swa_bwd(z, pos_col, pos_row, sv['sink_row'], sv['lse_a'], doa_raw, delta_a, dz, B, S)
    dz = _kv_grad_cast(d_ak, d_av, dz, T)
    g['sink'] = dsink[0, :HEADS]
    g['w_in'] = _mm_tn("grad_in", sv['h'], dz)
    dx0, g['g_mix'] = _rms_bwd("norm_mix_bwd", sv['x0'], D_MODEL, 0, sm['g_mix'][i], (dz, lw['w_in_t']), T, F32,
                               dres=dx1)
    for name in ('g_ple', 'g_q', 'g_kv', 'g_mix'):
        g[name] = g[name][0]
    return dx0, g


def _local_step(x, p, positions, wfull, sm, loss_target):
    B, S, _ = x.shape
    T = B * S
    pos_col = positions.reshape(T, 1)
    pos_row = positions.reshape(T // BLOCK, 1, BLOCK)
    half = MLA_ROPE // 2
    inv = ROPE_THETA ** (-jnp.arange(0, MLA_ROPE, 2, dtype=F32) / MLA_ROPE)
    inv_lane = jnp.tile(inv, LANES // half).reshape(1, LANES)
    tabs = _rope_tables(pos_col, inv_lane, T)
    xc = x.reshape(T, D_MODEL)
    lws, saved = [], []
    for i in range(DEPTH):
        lw = _layer_weights(wfull, i)
        xc, sv = _layer_fwd(xc, p[i].reshape(T, PLE_DIM), lw, sm, i, pos_col, pos_row, tabs, B, S)
        lws.append(lw)
        saved.append(sv)
    dx, loss, dg_final = _loss_head(xc, sm['g_final'], loss_target.reshape(T, D_MODEL), T)
    layer_grads = [None] * DEPTH
    for i in reversed(range(DEPTH)):
        dx, g = _layer_bwd(dx, saved[i], lws[i], sm, i, pos_col, pos_row, tabs, B, S)
        layer_grads[i] = _unpad_grads(g)
    return loss, dx.reshape(B, S, D_MODEL), layer_grads, dg_final[0]


SMALL_ROWS = 48


def _pack_small(arrs):
    flat = jnp.concatenate([arrs[name].reshape(-1) for name in SMALL])
    return jnp.pad(flat, (0, SMALL_ROWS * LANES - flat.shape[0])).reshape(SMALL_ROWS, LANES)


def _unpack_small(block, shapes):
    flat = block.reshape(-1)
    out, off = {}, 0
    for name in SMALL:
        n = math.prod(shapes[name])
        out[name] = flat[off:off + n].reshape(shapes[name])
        off += n
    return out


def _to_slots(g, axis):
    r, c = g.shape
    if axis == 0:
        return g.reshape(N_CHIPS, r // N_CHIPS, c)
    return g.reshape(r, N_CHIPS, c // N_CHIPS).transpose(1, 0, 2)


def _units(shapes):
    units = []
    for w, shape in enumerate(shapes):
        r = shape[-2]
        n = 4 if r >= 1024 else 1
        units += [(w, k * (r // n), r // n) for k in range(n)]
    return units


def _place():
    x, y, c = lax.axis_index("x"), lax.axis_index("y"), lax.axis_index("c")
    chips = [(1 - x, y), (x, 1 - y), (1 - x, 1 - y)]
    return x, y, c, chips


ANY = pl.BlockSpec(memory_space=pl.ANY)


def _remote(send_sems, recv_sems, k, src, dst, to):
    return pltpu.make_async_remote_copy(src_ref=src, dst_ref=dst, send_sem=send_sems.at[k],
                                        recv_sem=recv_sems.at[k], device_id=to, device_id_type=MESH)


def _gather_weights(shards):
    n = len(shards)
    units = _units([s.shape for s in shards])
    nu = len(units)

    def body(*refs):
        ins, outs = refs[:n], refs[n:2 * n]
        send_sems, recv_sems, local_sems = refs[2 * n:]
        x, y, c, chips = _place()
        me = 2 * x + y
        sibling = (x, y, 1 - c)
        copy = functools.partial(_remote, send_sems, recv_sems)
        keeps, sends = [], []
        for u, (w, r0, nr) in enumerate(units):
            rows = pl.ds(r0, nr)
            keeps.append(pltpu.make_async_copy(ins[w].at[:, rows, :], outs[w].at[me, :, rows, :], local_sems.at[u]))
            keeps[-1].start()
        for j, (cx, cy) in enumerate(chips):
            for u, (w, r0, nr) in enumerate(units):
                rows = pl.ds(r0, nr)
                sends.append(copy(j * nu + u, ins[w].at[c, rows, :], outs[w].at[me, c, rows, :], (cx, cy, c)))
                sends[-1].start()
        for j, (cx, cy) in enumerate(chips):
            for u, (w, r0, nr) in enumerate(units):
                landed = outs[w].at[2 * cx + cy, c, pl.ds(r0, nr), :]
                copy(j * nu + u, landed, landed, (cx, cy, c)).wait_recv()
                sends.append(copy((3 + j) * nu + u, landed, landed, sibling))
                sends[-1].start()
        for j, (cx, cy) in enumerate(chips):
            for u, (w, r0, nr) in enumerate(units):
                other = outs[w].at[2 * cx + cy, 1 - c, pl.ds(r0, nr), :]
                copy((3 + j) * nu + u, other, other, sibling).wait_recv()
        for cp in sends:
            cp.wait_send()
        for keep in keeps:
            keep.wait()

    return pl.pallas_call(
        body, name="gather_weights",
        out_shape=[jax.ShapeDtypeStruct((N_CHIPS,) + s.shape, s.dtype) for s in shards],
        in_specs=[ANY] * n, out_specs=[ANY] * n,
        scratch_shapes=[pltpu.SemaphoreType.DMA((6 * nu,)), pltpu.SemaphoreType.DMA((6 * nu,)),
                        pltpu.SemaphoreType.DMA((nu,))])(*shards)


def _pair_exchange(g0, g1):
    n = len(g0)

    def body(*refs):
        layers, outs = (refs[:n], refs[n:2 * n]), refs[2 * n:3 * n]
        send_sems, recv_sems = refs[3 * n:]
        x, y, c, _ = _place()
        copy = functools.partial(_remote, send_sems, recv_sems)
        for w in range(n):
            for q in range(N_CHIPS):
                for layer in range(DEPTH):
                    cp = copy(N_CHIPS * w + q, layers[layer][w].at[q], outs[w].at[q], (x, y, 1 - c))
                    pl.when(c == 1 - layer)(cp.start)
        for w in range(n):
            for q in range(N_CHIPS):
                copy(N_CHIPS * w + q, layers[0][w].at[q], outs[w].at[q], (x, y, 1 - c)).wait()

    return pl.pallas_call(
        body, name="pair_exchange", out_shape=[jax.ShapeDtypeStruct(g.shape, g.dtype) for g in g0],
        in_specs=[ANY] * (2 * n), out_specs=[ANY] * n,
        scratch_shapes=[pltpu.SemaphoreType.DMA((N_CHIPS * n,)), pltpu.SemaphoreType.DMA((N_CHIPS * n,))])(*g0, *g1)


def _pair_sum(name, g0, g1, theirs, cflag):
    shape = theirs.shape
    rows, width = shape[0] * shape[1], shape[2]

    def body(ins, outs, _):
        mine = jnp.where(ins[3][0:1, 0:1] == 0.0, ins[0][...], ins[1][...])
        tot = mine + ins[2][...]
        outs[0][...] = tot
        outs[1][...] = tot.astype(BF16)
    ins = [(a.reshape(rows, width), width, 0) for a in (g0, g1, theirs)] + [(cflag, None, None)]
    f32, bf16 = _ew(name, body, ins, [(width, F32), (width, BF16)], rows)
    return f32.reshape(shape), bf16.reshape(shape)


def _chip_exchange(parts):
    n = len(parts)

    def body(*refs):
        ins, outs = refs[:n], refs[n:2 * n]
        send_sems, recv_sems = refs[2 * n:]
        x, y, c, chips = _place()
        copy = functools.partial(_remote, send_sems, recv_sems)
        sends = []
        for j, (cx, cy) in enumerate(chips):
            for w in range(n):
                sends.append(copy(j * n + w, ins[w].at[2 * cx + cy], outs[w].at[j], (cx, cy, c)))
                sends[-1].start()
        for j, (cx, cy) in enumerate(chips):
            for w in range(n):
                copy(j * n + w, outs[w].at[j], outs[w].at[j], (cx, cy, c)).wait_recv()
        for cp in sends:
            cp.wait_send()

    return pl.pallas_call(
        body, name="chip_exchange",
        out_shape=[jax.ShapeDtypeStruct((3,) + a.shape[1:], a.dtype) for a in parts],
        in_specs=[ANY] * n, out_specs=[ANY] * n,
        scratch_shapes=[pltpu.SemaphoreType.DMA((3 * n,)), pltpu.SemaphoreType.DMA((3 * n,))])(*parts)


def _chip_sum(name, part, landed, chipflag):
    _, r, width = part.shape
    tm = min(r, 256)

    def kern(p_ref, l_ref, flag_ref, o_ref):
        me = flag_ref[0:1, 0:1]
        own = jnp.where(me == 0.0, p_ref[0], jnp.where(me == 1.0, p_ref[1], jnp.where(me == 2.0, p_ref[2], p_ref[3])))
        o_ref[...] = ((own + l_ref[0].astype(F32)) + l_ref[1].astype(F32)) + l_ref[2].astype(F32)

    return pl.pallas_call(
        kern, name=name, grid=(r // tm,),
        in_specs=[pl.BlockSpec((N_CHIPS, tm, width), lambda i: (0, i, 0)),
                  pl.BlockSpec((3, tm, width), lambda i: (0, i, 0)),
                  pl.BlockSpec((1, LANES), lambda i: (0, 0))],
        out_specs=pl.BlockSpec((tm, width), lambda i: (i, 0)),
        out_shape=jax.ShapeDtypeStruct((r, width), F32), compiler_params=_params(("arbitrary",)))(part, landed, chipflag)


def _pair_broadcast(mine):
    n = len(mine)
    units = _units([a.shape for a in mine])

    def body(*refs):
        ins, outs = refs[:n], refs[n:2 * n]
        send_sems, recv_sems = refs[2 * n:]
        x, y, c, _ = _place()
        copy = functools.partial(_remote, send_sems, recv_sems)
        cps = [copy(u, ins[w].at[pl.ds(r0, nr), :], outs[w].at[pl.ds(r0, nr), :], (x, y, 1 - c))
               for u, (w, r0, nr) in enumerate(units)]
        for cp in cps:
            cp.start()
        for cp in cps:
            cp.wait()

    return pl.pallas_call(
        body, name="pair_broadcast", out_shape=[jax.ShapeDtypeStruct(a.shape, a.dtype) for a in mine],
        in_specs=[ANY] * n, out_specs=[ANY] * n,
        scratch_shapes=[pltpu.SemaphoreType.DMA((len(units),)), pltpu.SemaphoreType.DMA((len(units),))])(*mine)


def _small_allreduce(v):
    offsets = [(dx, dy, dc) for dx in (0, 1) for dy in (0, 1) for dc in (0, 1)][1:]

    def body(v_ref, out_ref, recv_ref, send_sems, recv_sems):
        x, y, c, _ = _place()
        flip = lambda a, d: 1 - a if d else a
        peers = [(flip(x, dx), flip(y, dy), flip(c, dc)) for dx, dy, dc in offsets]
        copy = functools.partial(_remote, send_sems, recv_sems)
        me = 4 * x + 2 * y + c
        recv_ref[me] = v_ref[...]
        cps = [copy(k, v_ref, recv_ref.at[me], peer) for k, peer in enumerate(peers)]
        for cp in cps:
            cp.start()
        for k, (px, py, pc) in enumerate(peers):
            landed = recv_ref.at[4 * px + 2 * py + pc]
            copy(k, landed, landed, (px, py, pc)).wait_recv()
        for cp in cps:
            cp.wait_send()
        tot = recv_ref[0]
        for d in range(1, 8):
            tot = tot + recv_ref[d]
        out_ref[...] = tot

    vmem = pl.BlockSpec(memory_space=pltpu.VMEM)
    return pl.pallas_call(
        body, name="small_allreduce", out_shape=jax.ShapeDtypeStruct(v.shape, v.dtype),
        in_specs=[vmem], out_specs=vmem,
        scratch_shapes=[pltpu.VMEM((8,) + v.shape, v.dtype), pltpu.SemaphoreType.DMA((7,)),
                        pltpu.SemaphoreType.DMA((7,))])(v)


def _adam_math(gv, wv, mv, vv):
    mv = ADAM_B1 * mv + (1.0 - ADAM_B1) * gv
    vv = ADAM_B2 * vv + (1.0 - ADAM_B2) * (gv * gv)
    m_hat = mv / (1.0 - ADAM_B1 ** ADAM_STEP)
    v_hat = vv / (1.0 - ADAM_B2 ** ADAM_STEP)
    return -ADAM_LR * (m_hat / (jnp.sqrt(v_hat) + ADAM_EPS) + ADAM_WD * wv), mv, vv


def _adamw_big(name, mine, theirs, cflag, w, m, v):
    _, r, width = w.shape
    tm = min(r, 256)

    def kern(mine_ref, theirs_ref, flag_ref, w_ref, m_ref, v_ref, g_ref, d_ref, nm_ref, nv_ref):
        layer = pl.program_id(0).astype(F32)
        gv = jnp.where(flag_ref[0:1, 0:1] == layer, mine_ref[...], theirs_ref[...])
        g_ref[0] = gv
        d_ref[0], nm_ref[0], nv_ref[0] = _adam_math(gv, w_ref[0], m_ref[0], v_ref[0])

    flat = pl.BlockSpec((tm, width), lambda l, i: (i, 0))
    stacked = pl.BlockSpec((1, tm, width), lambda l, i: (l, i, 0))
    return pl.pallas_call(
        kern, name=name, grid=(DEPTH, r // tm),
        in_specs=[flat, flat, pl.BlockSpec((1, LANES), lambda l, i: (0, 0)), stacked, stacked, stacked],
        out_specs=[stacked] * 4, out_shape=[jax.ShapeDtypeStruct(w.shape, F32)] * 4,
        compiler_params=_params(("arbitrary", "arbitrary")))(mine, theirs, cflag, w, m, v)


def _adamw_small(g, w, m, v):
    def body(ins, outs, _):
        outs[0][...], outs[1][...], outs[2][...] = _adam_math(*(r[...] for r in ins))
    return _ew("adamw_small", body, [(a, LANES, 0) for a in (g, w, m, v)], [(LANES, F32)] * 3, SMALL_ROWS)


def kernel(x, p, positions, g_mix, w_in, sink, g_q, w_uq, g_kv, w_ukv, w_br_a, w_br_b, w_out, g_ple, w_ple_gate, w_ple_proj, g_final, loss_target, m_g_mix, m_w_in, m_sink, m_g_q, m_w_uq, m_g_kv, m_w_ukv, m_w_br_a, m_w_br_b, m_w_out, m_g_ple, m_w_ple_gate, m_w_ple_proj, m_g_final, v_g_mix, v_w_in, v_sink, v_g_q, v_w_uq, v_g_kv, v_w_ukv, v_w_br_a, v_w_br_b, v_w_out, v_g_ple, v_w_ple_gate, v_w_ple_proj, v_g_final):
    w = dict(g_mix=g_mix, w_in=w_in, sink=sink, g_q=g_q, w_uq=w_uq, g_kv=g_kv, w_ukv=w_ukv, w_br_a=w_br_a,
             w_br_b=w_br_b, w_out=w_out, g_ple=g_ple, w_ple_gate=w_ple_gate, w_ple_proj=w_ple_proj, g_final=g_final)
    m = dict(g_mix=m_g_mix, w_in=m_w_in, sink=m_sink, g_q=m_g_q, w_uq=m_w_uq, g_kv=m_g_kv, w_ukv=m_w_ukv,
             w_br_a=m_w_br_a, w_br_b=m_w_br_b, w_out=m_w_out, g_ple=m_g_ple, w_ple_gate=m_w_ple_gate,
             w_ple_proj=m_w_ple_proj, g_final=m_g_final)
    v = dict(g_mix=v_g_mix, w_in=v_w_in, sink=v_sink, g_q=v_g_q, w_uq=v_w_uq, g_kv=v_g_kv, w_ukv=v_w_ukv,
             w_br_a=v_w_br_a, w_br_b=v_w_br_b, w_out=v_w_out, g_ple=v_g_ple, w_ple_gate=v_w_ple_gate,
             w_ple_proj=v_w_ple_proj, g_final=v_g_final)
    wfull = _gather_full(w)
    sm = {name: w[name] for name in SMALL}
    loss_row, grad_x, layer_grads, dg_final = _local_step(x, p, positions, wfull, sm, loss_target)
    loss = lax.psum(loss_row[0, 0], ("x", "y", "c"))
    res = _update(layer_grads, dg_final, w, m, v)
    return (loss, grad_x, *[res[name][kind] for kind in range(4) for name in WEIGHT_NAMES])


def _gather_full(w):
    gathered = _gather_weights([w[name].astype(BF16) for name, _ in SHARDED])
    return {name: [jnp.concatenate([gathered[k][q, layer] for q in range(N_CHIPS)], axis=axis - 1)
                   for layer in range(DEPTH)] for k, (name, axis) in enumerate(SHARDED)}


def _update(layer_grads, dg_final, w, m, v):
    small_shapes = {name: w[name].shape for name in SMALL}
    cflag = jnp.full((1, LANES), lax.axis_index("c"), F32)
    chipflag = jnp.full((1, LANES), 2 * lax.axis_index("x") + lax.axis_index("y"), F32)

    slots = [[_to_slots(layer_grads[layer][name], axis - 1) for name, axis in SHARDED] for layer in range(DEPTH)]
    theirs = _pair_exchange(slots[0], slots[1])
    pair = [_pair_sum("pair_sum_" + name, slots[0][k], slots[1][k], theirs[k], cflag)
            for k, (name, _) in enumerate(SHARDED)]
    landed = _chip_exchange([bf16 for _, bf16 in pair])
    mine = [_chip_sum("chip_sum_" + name, pair[k][0], landed[k], chipflag) for k, (name, _) in enumerate(SHARDED)]
    other = _pair_broadcast(mine)
    res = {name: _adamw_big("adamw_" + name, mine[k], other[k], cflag, w[name], m[name], v[name])
           for k, (name, _) in enumerate(SHARDED)}

    gsmall = {name: jnp.stack([layer_grads[layer][name] for layer in range(DEPTH)]) for name in SMALL[:-1]}
    gsmall['g_final'] = dg_final
    gsum = _small_allreduce(_pack_small(gsmall))
    small = (gsum,) + tuple(_adamw_small(gsum, _pack_small(w), _pack_small(m), _pack_small(v)))
    for name, arrs in zip(SMALL, zip(*[[_unpack_small(a, small_shapes)[n] for n in SMALL] for a in small])):
        res[name] = arrs
    return res
```

```python
import functools
import math

import jax
import jax.numpy as jnp
from jax import lax
from jax.experimental import pallas as pl
from jax.experimental.pallas import tpu as pltpu

F32 = jnp.float32
BF16 = jnp.bfloat16

D_MODEL = 1024
DEPTH = 2
PLE_DIM = 256
BLOCK = 128
EPS = 1e-6
NEG = -1e30
HEADS = 8
SWA_KV_HEADS = 2
HEAD_DIM = 64
LANES = 128
HPAD = HEADS * LANES
MLA_QK = 96
MLA_ROPE = 32
MLA_Q_LORA = 256
MLA_KV_LORA = 128
ROPE_THETA = 10000.0
IN_SIZES = (512, 128, 128, 512, 256, 128, 32, 512, 1024, 1024)

Z_MA, Z_MB, Z_AQ, Z_AGATE, Z_BGATE = 0, 1024, 2048, 3072, 4096
Z_AK, Z_AV, Z_BQD, Z_BKVD, Z_BKR = 5120, 5376, 5632, 5888, 6016
Z_WIDTH = 6144

ADAM_LR, ADAM_B1, ADAM_B2, ADAM_EPS, ADAM_WD, ADAM_STEP = 0.001, 0.9, 0.999, 1e-08, 0.01, 10

VMEM_LIMIT = 56 * 1024 * 1024
MESH = pl.DeviceIdType.MESH

WEIGHT_NAMES = ('g_mix', 'w_in', 'sink', 'g_q', 'w_uq', 'g_kv', 'w_ukv', 'w_br_a', 'w_br_b',
                'w_out', 'g_ple', 'w_ple_gate', 'w_ple_proj', 'g_final')
SHARDED = (('w_in', 2), ('w_uq', 2), ('w_ukv', 2), ('w_br_a', 2), ('w_br_b', 2),
           ('w_out', 1), ('w_ple_gate', 1), ('w_ple_proj', 2))
SMALL = ('g_mix', 'sink', 'g_q', 'g_kv', 'g_ple', 'g_final')
N_CHIPS = 4


def _params(sem):
    return pltpu.CompilerParams(dimension_semantics=sem, vmem_limit_bytes=VMEM_LIMIT)


MM_TN = 512
ROW_TILE = 512
BIG_WEIGHT_BYTES = 8 * 1024 * 1024


def _row_tile(rows, weight_bytes=0):
    tm = ROW_TILE // 2 if weight_bytes > BIG_WEIGHT_BYTES else ROW_TILE
    return min(tm, rows)


def _ew(name, body, ins, outs, rows, accs=(), mms=(), tm=None):
    n_mm, n_in, n_out = len(mms), len(ins), len(outs)
    if tm is None:
        tm = _row_tile(rows, sum(b.size * b.dtype.itemsize for _, b in mms))
    in_specs, args = [], []
    for a, b in mms:
        in_specs += [pl.BlockSpec((tm, a.shape[1]), lambda i: (i, 0)), pl.BlockSpec(b.shape, lambda i: (0, 0))]
        args += [a, b]
    for arr, width, cb in ins:
        if width is None:
            in_specs.append(pl.BlockSpec(arr.shape, lambda i, nd=arr.ndim: (0,) * nd))
        else:
            in_specs.append(pl.BlockSpec((tm, width), lambda i, cb=cb: (i, cb)))
        args.append(arr)
    out_shape, out_specs, aliases = [], [], {}
    for k, out in enumerate(outs):
        if len(out) == 4:
            aliases[len(args)] = k
            in_specs.append(pl.BlockSpec(memory_space=pl.ANY))
            args.append(out[2])
            out_shape.append(jax.ShapeDtypeStruct(out[2].shape, out[2].dtype))
            out_specs.append(pl.BlockSpec((tm, out[0]), lambda i, cb=out[3]: (i, cb)))
        else:
            out_shape.append(jax.ShapeDtypeStruct((rows, out[0]), out[1]))
            out_specs.append(pl.BlockSpec((tm, out[0]), lambda i: (i, 0)))
    n_in += len(aliases)
    out_shape += [jax.ShapeDtypeStruct(s, F32) for s in accs]
    out_specs += [pl.BlockSpec(s, lambda i: (0, 0)) for s in accs]

    def kern(*refs):
        mm_refs, refs = refs[:2 * n_mm], refs[2 * n_mm:]
        in_refs, out_refs = refs[:n_in - len(aliases)], refs[n_in:n_in + n_out]
        acc_refs, prod_refs = refs[n_in + n_out:n_in + n_out + len(accs)], refs[n_in + n_out + len(accs):]
        if acc_refs:
            @pl.when(pl.program_id(0) == 0)
            def _():
                for r in acc_refs:
                    r[...] = jnp.zeros_like(r)
        for k in range(n_mm):
            a_ref, b_ref, prod = mm_refs[2 * k], mm_refs[2 * k + 1], prod_refs[k]
            av = a_ref[...].astype(BF16)
            n = b_ref.shape[1]
            tn = min(MM_TN, n)
            for j in range(n // tn):
                cols = slice(j * tn, (j + 1) * tn)
                prod[:, cols] = jnp.dot(av, b_ref[:, cols], preferred_element_type=F32)
        body(tuple(prod_refs) + tuple(in_refs), out_refs, acc_refs)

    scratch = [pltpu.VMEM((tm, b.shape[1]), F32) for _, b in mms]
    res = pl.pallas_call(kern, name=name, grid=(rows // tm,), in_specs=in_specs, out_specs=out_specs,
                         out_shape=out_shape, scratch_shapes=scratch, input_output_aliases=aliases,
                         compiler_params=_params(("arbitrary",)))(*args)
    return res


def _rms_fwd(name, x, width, cb, g, rows):
    def body(ins, outs, _):
        xv = ins[0][...].astype(F32)
        r = lax.rsqrt(jnp.mean(xv * xv, axis=-1, keepdims=True) + EPS)
        outs[0][...] = ((xv * r) * ins[1][...]).astype(BF16)
    return _ew(name, body, [(x, width, cb), (g.reshape(1, width), None, None)], [(width, BF16)], rows)[0]


def _rms_bwd(name, x, width, cb, g, dh_mm, rows, out_dtype, dres=None, into=()):
    def body(ins, outs, accs):
        dhv, xv, gv = ins[0][...], ins[1][...].astype(F32), ins[2][...]
        r = lax.rsqrt(jnp.mean(xv * xv, axis=-1, keepdims=True) + EPS)
        xhat = xv * r
        accs[0][...] += jnp.sum(dhv * xhat, axis=0, keepdims=True)
        dy = dhv * gv
        dx = r * (dy - xhat * jnp.mean(dy * xhat, axis=-1, keepdims=True))
        if dres is not None:
            dx = dx + ins[3][...]
        outs[0][...] = dx.astype(out_dtype)
    ins = [(x, width, cb), (g.reshape(1, width), None, None)]
    if dres is not None:
        ins.append((dres, width, 0))
    return _ew(name, body, ins, [(width, out_dtype) + tuple(into)], rows, accs=[(1, width)], mms=[dh_mm])


def _mm(name, a, b, out_dtype, residual=None, f32_cols=None, tn=MM_TN):
    M, K = a.shape
    N = b.shape[1]
    tm, tn = _row_tile(M, b.size * b.dtype.itemsize), min(tn, N)
    has_res = residual is not None
    c0, cw = f32_cols if f32_cols else (0, 0)

    def kern(*refs):
        a_ref, b_ref = refs[0], refs[1]
        o_ref = refs[3] if has_res else refs[2]
        av = a_ref[...].astype(BF16)
        for j in range(N // tn):
            cols = slice(j * tn, (j + 1) * tn)
            part = jnp.dot(av, b_ref[:, cols], preferred_element_type=F32)
            if has_res:
                part = part + refs[2][:, cols]
            o_ref[:, cols] = part.astype(o_ref.dtype)
            if c0 <= j * tn and (j + 1) * tn <= c0 + cw:
                refs[-1][:, j * tn - c0:(j + 1) * tn - c0] = part

    in_specs = [pl.BlockSpec((tm, K), lambda i: (i, 0)), pl.BlockSpec((K, N), lambda i: (0, 0))]
    args = [a, b]
    if has_res:
        in_specs.append(pl.BlockSpec((tm, N), lambda i: (i, 0)))
        args.append(residual)
    out_specs = [pl.BlockSpec((tm, N), lambda i: (i, 0))]
    out_shape = [jax.ShapeDtypeStruct((M, N), out_dtype)]
    if f32_cols:
        assert c0 % tn == 0 and cw % tn == 0
        out_specs.append(pl.BlockSpec((tm, cw), lambda i: (i, 0)))
        out_shape.append(jax.ShapeDtypeStruct((M, cw), F32))
    res = pl.pallas_call(kern, name=name, grid=(M // tm,), in_specs=in_specs, out_specs=out_specs,
                         out_shape=out_shape, compiler_params=_params(("parallel",)))(*args)
    return res if f32_cols else res[0]


def _mm_tn(name, a, b, tk=512, tn=2048):
    T, M = a.shape
    N = b.shape[1]
    tn, tk = min(tn, N), min(tk, T)

    def kern(a_ref, b_ref, o_ref):
        k = pl.program_id(1)
        part = _dot_tn(a_ref[...].astype(BF16), b_ref[...].astype(BF16))

        @pl.when(k == 0)
        def _():
            o_ref[...] = part

        @pl.when(k > 0)
        def _():
            o_ref[...] += part

    return pl.pallas_call(
        kern, name=name, grid=(N // tn, T // tk),
        in_specs=[pl.BlockSpec((tk, M), lambda j, k: (k, 0)), pl.BlockSpec((tk, tn), lambda j, k: (k, j))],
        out_specs=pl.BlockSpec((M, tn), lambda j, k: (0, j)),
        out_shape=jax.ShapeDtypeStruct((M, N), F32),
        compiler_params=_params(("parallel", "arbitrary")))(a, b)


def _dot_nt(a, b):
    return lax.dot_general(a, b, (((1,), (1,)), ((), ())), preferred_element_type=F32)


def _dot_tn(a, b):
    return lax.dot_general(a, b, (((0,), (0,)), ((), ())), preferred_element_type=F32)


SWA_SCALE = HEAD_DIM ** -0.5


def _swa_band(n, pq_ref, pkp_ref, pkc_ref):
    posk = jnp.concatenate([pkp_ref[0], pkc_ref[0]], axis=1)
    dist = (pq_ref[...] - posk).astype(F32)
    qi = lax.broadcasted_iota(jnp.int32, (BLOCK, 2 * BLOCK), 0)
    kj = lax.broadcasted_iota(jnp.int32, (BLOCK, 2 * BLOCK), 1)
    t_abs = n * BLOCK + qi
    s_abs = n * BLOCK - BLOCK + kj
    return dist, (s_abs >= 0) & (s_abs <= t_abs) & (t_abs - s_abs < BLOCK)


SWA_GROUP = HEADS // SWA_KV_HEADS


def _swa_group_q(q_all, g):
    heads = range(g * SWA_GROUP, (g + 1) * SWA_GROUP)
    return jnp.concatenate([(q_all[:, h * LANES:(h + 1) * LANES] * SWA_SCALE).astype(BF16) for h in heads], axis=0)


def _swa_mask(s, dist, valid, h):
    return jnp.where(valid, s - (2.0 ** -(h + 1)) * dist, NEG)


def _swa_specs(nb):
    prev = lambda b, n: b * nb + jnp.maximum(n - 1, 0)
    own = lambda b, n: b * nb + n
    return [
        pl.BlockSpec((BLOCK, HPAD), lambda b, n: (own(b, n), Z_AQ // HPAD)),
        pl.BlockSpec((BLOCK, 256), lambda b, n: (prev(b, n), Z_AK // 256)),
        pl.BlockSpec((BLOCK, 256), lambda b, n: (own(b, n), Z_AK // 256)),
        pl.BlockSpec((BLOCK, 256), lambda b, n: (prev(b, n), Z_AV // 256)),
        pl.BlockSpec((BLOCK, 256), lambda b, n: (own(b, n), Z_AV // 256)),
        pl.BlockSpec((BLOCK, 1), lambda b, n: (own(b, n), 0)),
        pl.BlockSpec((1, 1, BLOCK), lambda b, n: (prev(b, n), 0, 0)),
        pl.BlockSpec((1, 1, BLOCK), lambda b, n: (own(b, n), 0, 0)),
    ]


def _swa_fwd(z, gate, pos_col, pos_row, sink_row, B, S):
    nb = S // BLOCK
    T = B * S

    def kern(q_ref, kp_ref, kc_ref, vp_ref, vc_ref, pq_ref, pkp_ref, pkc_ref, gate_ref, sink_ref,
             oraw_ref, og_ref, lse_ref):
        q_all = q_ref[...]
        kb = jnp.concatenate([kp_ref[...], kc_ref[...]], axis=0).astype(BF16)
        vb = jnp.concatenate([vp_ref[...], vc_ref[...]], axis=0).astype(BF16)
        dist, valid = _swa_band(pl.program_id(1), pq_ref, pkp_ref, pkc_ref)
        lane = lax.broadcasted_iota(jnp.int32, (BLOCK, LANES), 1)
        lse_all = jnp.zeros((BLOCK, LANES), F32)
        for grp in range(SWA_KV_HEADS):
            gcols = slice(grp * LANES, (grp + 1) * LANES)
            s_all = _dot_nt(_swa_group_q(q_all, grp), kb[:, gcols])
            probs = []
            for hh in range(SWA_GROUP):
                h = grp * SWA_GROUP + hh
                s = _swa_mask(s_all[hh * BLOCK:(hh + 1) * BLOCK], dist, valid, h)
                sink_h = sink_ref[0:1, h:h + 1]
                m = jnp.maximum(jnp.max(s, axis=-1, keepdims=True), sink_h)
                e = jnp.exp(s - m)
                denom = jnp.sum(e, axis=-1, keepdims=True) + jnp.exp(sink_h - m)
                probs.append((e * (1.0 / denom)).astype(BF16))
                lse_all = jnp.where(lane == h, m + jnp.log(denom), lse_all)
            o_all = jnp.dot(jnp.concatenate(probs, axis=0), vb[:, gcols], preferred_element_type=F32)
            for hh in range(SWA_GROUP):
                cols = slice((grp * SWA_GROUP + hh) * LANES, (grp * SWA_GROUP + hh + 1) * LANES)
                o = o_all[hh * BLOCK:(hh + 1) * BLOCK]
                oraw_ref[:, cols] = o
                g = gate_ref[:, cols].astype(F32)
                og_ref[:, cols] = (o * (g * jax.nn.sigmoid(g))).astype(BF16)
        lse_ref[...] = lse_all

    own = lambda b, n: b * nb + n
    in_specs = _swa_specs(nb) + [
        pl.BlockSpec((BLOCK, HPAD), lambda b, n: (own(b, n), 0)),
        pl.BlockSpec((1, LANES), lambda b, n: (0, 0)),
    ]
    out_specs = [pl.BlockSpec((BLOCK, HPAD), lambda b, n: (own(b, n), 0)),
                 pl.BlockSpec((BLOCK, HPAD), lambda b, n: (own(b, n), 0)),
                 pl.BlockSpec((BLOCK, LANES), lambda b, n: (own(b, n), 0))]
    out_shape = [jax.ShapeDtypeStruct((T, HPAD), F32), jax.ShapeDtypeStruct((T, HPAD), BF16),
                 jax.ShapeDtypeStruct((T, LANES), F32)]
    return pl.pallas_call(kern, name="swa_fwd", grid=(B, nb), in_specs=in_specs, out_specs=out_specs,
                          out_shape=out_shape, compiler_params=_params(("parallel", "arbitrary")))(
        z, z, z, z, z, pos_col, pos_row, pos_row, gate, sink_row)


def _swa_bwd(z, pos_col, pos_row, sink_row, lse, do_raw, delta, dz, B, S):
    nb = S // BLOCK
    T = B * S

    def kern(q_ref, kp_ref, kc_ref, vp_ref, vc_ref, pq_ref, pkp_ref, pkc_ref, sink_ref, lse_ref, do_ref,
             delta_ref, dz_ref, dq_ref, dk_ref, dv_ref, dsink_ref):
        b, n = pl.program_id(0), pl.program_id(1)

        @pl.when(n == 0)
        def _():
            dk_ref[...] = jnp.zeros_like(dk_ref)
            dv_ref[...] = jnp.zeros_like(dv_ref)

        @pl.when((b == 0) & (n == 0))
        def _():
            dsink_ref[...] = jnp.zeros_like(dsink_ref)

        q_all = q_ref[...]
        kb = jnp.concatenate([kp_ref[...], kc_ref[...]], axis=0).astype(BF16)
        vb = jnp.concatenate([vp_ref[...], vc_ref[...]], axis=0).astype(BF16)
        dist, valid = _swa_band(n, pq_ref, pkp_ref, pkc_ref)
        lane1 = lax.broadcasted_iota(jnp.int32, (1, LANES), 1)
        dsink = jnp.zeros((1, LANES), F32)
        dk_band, dv_band = [], []
        for grp in range(SWA_KV_HEADS):
            gcols = slice(grp * LANES, (grp + 1) * LANES)
            heads = range(grp * SWA_GROUP, (grp + 1) * SWA_GROUP)
            qg = _swa_group_q(q_all, grp)
            dog = jnp.concatenate([do_ref[:, h * LANES:(h + 1) * LANES] for h in heads], axis=0)
            s_all = _dot_nt(qg, kb[:, gcols])
            dp_all = _dot_nt(dog, vb[:, gcols])
            ps, dss = [], []
            for hh, h in enumerate(heads):
                blk = slice(hh * BLOCK, (hh + 1) * BLOCK)
                lse_h = lse_ref[:, h:h + 1]
                delta_h = delta_ref[:, h:h + 1]
                p = jnp.exp(_swa_mask(s_all[blk], dist, valid, h) - lse_h)
                ps.append(p.astype(BF16))
                dss.append((p * (dp_all[blk] - delta_h)).astype(BF16))
                psink = jnp.exp(sink_ref[0:1, h:h + 1] - lse_h)
                dsink = dsink + jnp.where(lane1 == h, -jnp.sum(psink * delta_h, axis=0, keepdims=True), 0.0)
            dsg = jnp.concatenate(dss, axis=0)
            dq_all = jnp.dot(dsg, kb[:, gcols], preferred_element_type=F32) * SWA_SCALE
            for hh, h in enumerate(heads):
                dq_ref[:, h * LANES:(h + 1) * LANES] = dq_all[hh * BLOCK:(hh + 1) * BLOCK].astype(BF16)
            dk_band.append(jnp.dot(qg.T, dsg, preferred_element_type=F32).T)
            dv_band.append(jnp.dot(dog.T, jnp.concatenate(ps, axis=0), preferred_element_type=F32).T)
        dsink_ref[...] += dsink
        dkb = jnp.concatenate(dk_band, axis=1)
        dvb = jnp.concatenate(dv_band, axis=1)
        r_prev = pl.ds(pl.multiple_of(jnp.maximum(n - 1, 0) * BLOCK, BLOCK), BLOCK)
        r_own = pl.ds(pl.multiple_of(n * BLOCK, BLOCK), BLOCK)
        dk_ref[r_prev, :] += dkb[:BLOCK]
        dk_ref[r_own, :] += dkb[BLOCK:]
        dv_ref[r_prev, :] += dvb[:BLOCK]
        dv_ref[r_own, :] += dvb[BLOCK:]

    own = lambda b, n: b * nb + n
    in_specs = _swa_specs(nb) + [
        pl.BlockSpec((1, LANES), lambda b, n: (0, 0)),
        pl.BlockSpec((BLOCK, LANES), lambda b, n: (own(b, n), 0)),
        pl.BlockSpec((BLOCK, HPAD), lambda b, n: (own(b, n), 0)),
        pl.BlockSpec((BLOCK, LANES), lambda b, n: (own(b, n), 0)),
        pl.BlockSpec(memory_space=pl.ANY),
    ]
    out_specs = [pl.BlockSpec((BLOCK, HPAD), lambda b, n: (own(b, n), Z_AQ // HPAD)),
                 pl.BlockSpec((S, 256), lambda b, n: (b, 0)),
                 pl.BlockSpec((S, 256), lambda b, n: (b, 0)),
                 pl.BlockSpec((1, LANES), lambda b, n: (0, 0))]
    out_shape = [jax.ShapeDtypeStruct(dz.shape, dz.dtype), jax.ShapeDtypeStruct((T, 256), F32),
                 jax.ShapeDtypeStruct((T, 256), F32), jax.ShapeDtypeStruct((1, LANES), F32)]
    return pl.pallas_call(kern, name="swa_bwd", grid=(B, nb), in_specs=in_specs, out_specs=out_specs,
                          out_shape=out_shape, input_output_aliases={len(in_specs) - 1: 0},
                          compiler_params=_params(("arbitrary", "arbitrary")))(
        z, z, z, z, z, pos_col, pos_row, pos_row, sink_row, lse, do_raw, delta, dz)


MLA_T = 256
MLA_HG = 4
MLA_W = MLA_HG * LANES
MLA_SCALE = MLA_QK ** -0.5
LOG2E = 1.4426950408889634
MLA_QSCALE = MLA_SCALE * LOG2E


def _causal_t(s):
    key = lax.broadcasted_iota(jnp.int32, s.shape, 0)
    query = lax.broadcasted_iota(jnp.int32, s.shape, 1)
    return jnp.where(key <= query, s, NEG)


def _mla_fwd(q, k, v, z, B, S):
    T = B * S
    nq = S // MLA_T

    def kern(q_ref, k_ref, v_ref, gate_ref, oraw_ref, og_ref, lse_ref):
        i = pl.program_id(2)

        def scores(j):
            rows = pl.ds(pl.multiple_of(j * MLA_T, MLA_T), MLA_T)
            return tuple(_dot_nt(k_ref[rows, hh * LANES:(hh + 1) * LANES], q_ref[:, hh * LANES:(hh + 1) * LANES])
                         for hh in range(MLA_HG))

        def update(j, ss, state):
            rows = pl.ds(pl.multiple_of(j * MLA_T, MLA_T), MLA_T)
            out = []
            for hh in range(MLA_HG):
                (m, l, acc), s = state[hh], ss[hh]
                m_new = jnp.maximum(m, jnp.max(s, axis=0, keepdims=True))
                alpha = jnp.exp2(m - m_new)
                p = jnp.exp2(s - m_new)
                l = alpha * l + jnp.sum(p, axis=0, keepdims=True)
                pv = jnp.dot(v_ref[rows, hh * LANES:(hh + 1) * LANES].T, p.astype(BF16), preferred_element_type=F32)
                out.append((m_new, l, alpha * acc + pv))
            return tuple(out)

        def body(j, carry):
            state, ss = carry
            s_next = scores(j + 1)
            return update(j, ss, state), s_next

        init = tuple((jnp.full((1, MLA_T), NEG, F32), jnp.zeros((1, MLA_T), F32), jnp.zeros((LANES, MLA_T), F32))
                     for _ in range(MLA_HG))
        state, ss = lax.fori_loop(0, i, body, (init, scores(0)))
        state = update(i, tuple(_causal_t(s) for s in ss), state)
        for hh in range(MLA_HG):
            m, l, acc = state[hh]
            cols = slice(hh * LANES, (hh + 1) * LANES)
            o = (acc * (1.0 / l)).T
            oraw_ref[:, cols] = o
            g = gate_ref[:, cols].astype(F32)
            og_ref[:, cols] = (o * (g * jax.nn.sigmoid(g))).astype(BF16)
            lse_ref[0, 0, 0, hh:hh + 1, :] = m + jnp.log2(l)

    blk = lambda b, h, i: (b * nq + i, h)
    in_specs = [pl.BlockSpec((MLA_T, MLA_W), blk),
                pl.BlockSpec((S, MLA_W), lambda b, h, i: (b, h)),
                pl.BlockSpec((S, MLA_W), lambda b, h, i: (b, h)),
                pl.BlockSpec((MLA_T, MLA_W), lambda b, h, i: (b * nq + i, Z_BGATE // MLA_W + h))]
    out_specs = [pl.BlockSpec((MLA_T, MLA_W), blk), pl.BlockSpec((MLA_T, MLA_W), blk),
                 pl.BlockSpec((1, 1, 1, MLA_HG, MLA_T), lambda b, h, i: (b, h, i, 0, 0))]
    out_shape = [jax.ShapeDtypeStruct((T, HPAD), F32), jax.ShapeDtypeStruct((T, HPAD), BF16),
                 jax.ShapeDtypeStruct((B, HEADS // MLA_HG, nq, MLA_HG, MLA_T), F32)]
    return pl.pallas_call(kern, name="mla_fwd", grid=(B, HEADS // MLA_HG, nq), in_specs=in_specs,
                          out_specs=out_specs, out_shape=out_shape,
                          compiler_params=_params(("parallel", "parallel", "arbitrary")))(q, k, v, z)


def _mla_bwd(q, k, v, do_raw, lse, delta, B, S):
    T = B * S
    nk = S // MLA_T

    def kern(q_ref, k_ref, v_ref, do_ref, lse_ref, delta_ref, dq_ref, dk_ref, dv_ref, dq_acc, dk_acc, dv_acc):
        j = pl.program_id(2)

        @pl.when(j == 0)
        def _():
            dq_acc[...] = jnp.zeros_like(dq_acc)

        dk_acc[...] = jnp.zeros_like(dk_acc)
        dv_acc[...] = jnp.zeros_like(dv_acc)
        kts = [k_ref[:, hh * LANES:(hh + 1) * LANES].T for hh in range(MLA_HG)]

        def step(i, masked):
            rows = pl.ds(pl.multiple_of(i * MLA_T, MLA_T), MLA_T)
            for hh in range(MLA_HG):
                cols = slice(hh * LANES, (hh + 1) * LANES)
                qv, do = q_ref[rows, cols], do_ref[rows, cols]
                st = _dot_nt(k_ref[:, cols], qv)
                if masked:
                    st = _causal_t(st)
                pt = jnp.exp2(st - lse_ref[0, 0, i, hh:hh + 1, :])
                dpt = _dot_nt(v_ref[:, cols], do)
                dst = (pt * (dpt - delta_ref[0, 0, i, hh:hh + 1, :])).astype(BF16)
                dv_acc[:, cols] += jnp.dot(pt.astype(BF16), do, preferred_element_type=F32)
                dk_acc[:, cols] += jnp.dot(dst, qv, preferred_element_type=F32)
                dq_acc[hh, i] += jnp.dot(kts[hh], dst, preferred_element_type=F32)

        step(j, True)

        def body(i, c):
            step(i, False)
            return c

        lax.fori_loop(j + 1, nk, body, 0)
        dk_ref[...] = dk_acc[...] * (1.0 / LOG2E)
        dv_ref[...] = dv_acc[...]

        @pl.when(j == nk - 1)
        def _():
            for hh in range(MLA_HG):
                for t in range(nk):
                    dq_ref[t * MLA_T:(t + 1) * MLA_T, hh * LANES:(hh + 1) * LANES] = dq_acc[hh, t].T

    whole = lambda b, h, j: (b, h)
    tile = lambda b, h, j: (b * nk + j, h)
    stats = pl.BlockSpec((1, 1, nk, MLA_HG, MLA_T), lambda b, h, j: (b, h, 0, 0, 0))
    in_specs = [pl.BlockSpec((S, MLA_W), whole), pl.BlockSpec((MLA_T, MLA_W), tile),
                pl.BlockSpec((MLA_T, MLA_W), tile), pl.BlockSpec((S, MLA_W), whole), stats, stats]
    out_specs = [pl.BlockSpec((S, MLA_W), whole), pl.BlockSpec((MLA_T, MLA_W), tile),
                 pl.BlockSpec((MLA_T, MLA_W), tile)]
    out_shape = [jax.ShapeDtypeStruct((T, HPAD), F32)] * 3
    scratch = [pltpu.VMEM((MLA_HG, nk, LANES, MLA_T), F32), pltpu.VMEM((MLA_T, MLA_W), F32),
               pltpu.VMEM((MLA_T, MLA_W), F32)]
    return pl.pallas_call(kern, name="mla_bwd", grid=(B, HEADS // MLA_HG, nk), in_specs=in_specs,
                          out_specs=out_specs, out_shape=out_shape, scratch_shapes=scratch,
                          compiler_params=_params(("parallel", "parallel", "arbitrary")))(
        q, k, v, do_raw, lse, delta)


def _rope_tables(pos_col, inv_lane, rows):
    def body(ins, outs, _):
        ang = ins[0][...].astype(F32) * ins[1][...]
        lane = lax.broadcasted_iota(jnp.int32, ang.shape, 1)
        cos, sin = jnp.cos(ang), jnp.sin(ang)
        first = (lane >= HEAD_DIM) & (lane < HEAD_DIM + MLA_ROPE // 2)
        second = (lane >= HEAD_DIM + MLA_ROPE // 2) & (lane < MLA_QK)
        outs[0][...] = jnp.where(lane < HEAD_DIM, 1.0, jnp.where(lane < MLA_QK, cos, 0.0))
        outs[1][...] = jnp.where(first, -sin, 0.0)
        outs[2][...] = jnp.where(second, sin, 0.0)
    return _ew("rope_tables", body, [(pos_col, 1, 0), (inv_lane, None, None)], [(LANES, F32)] * 3, rows)


def _rope(x, c, s1, s2):
    return x * c + pltpu.roll(x, 112, 1) * s1 + pltpu.roll(x, 16, 1) * s2


def _rope_t(d, c, s1, s2):
    return d * c + pltpu.roll(d * s1, 16, 1) + pltpu.roll(d * s2, 112, 1)


def _mla_prep(qdn, w_uq, kvdn, w_ukv, z, tabs, rows):
    def body(ins, outs, _):
        q_pre, kv_pre = ins[0], ins[1]
        c, s1, s2 = ins[3][...], ins[4][...], ins[5][...]
        kr = _rope(ins[2][...].astype(F32), c, s1, s2)
        for h in range(HEADS):
            cols = slice(h * LANES, (h + 1) * LANES)
            outs[0][:, cols] = (_rope(q_pre[:, cols], c, s1, s2) * MLA_QSCALE).astype(BF16)
            outs[1][:, cols] = (kv_pre[:, cols] + kr).astype(BF16)
        outs[2][...] = kv_pre[:, HPAD:].astype(BF16)
    ins = [(z, LANES, Z_BKR // LANES), (tabs[0], LANES, 0), (tabs[1], LANES, 0), (tabs[2], LANES, 0)]
    return _ew("mla_prep", body, ins, [(HPAD, BF16)] * 3, rows, mms=[(qdn, w_uq), (kvdn, w_ukv)])


def _mla_prep_bwd(dq, dk, dv, tabs, dz, rows):
    def body(ins, outs, _):
        c, s1, s2 = ins[3][...], ins[4][...], ins[5][...]
        lane = lax.broadcasted_iota(jnp.int32, c.shape, 1)
        dkr = jnp.zeros(c.shape, F32)
        for h in range(HEADS):
            cols = slice(h * LANES, (h + 1) * LANES)
            outs[0][:, cols] = _rope_t(ins[0][:, cols] * MLA_SCALE, c, s1, s2).astype(BF16)
            dkh = ins[1][:, cols]
            outs[1][:, cols] = jnp.where(lane < HEAD_DIM, dkh, 0.0).astype(BF16)
            dkr = dkr + dkh
        outs[1][:, HPAD:] = ins[2][...].astype(BF16)
        live = (lane >= HEAD_DIM) & (lane < MLA_QK)
        outs[2][...] = jnp.where(live, _rope_t(jnp.where(live, dkr, 0.0), c, s1, s2), 0.0).astype(BF16)
    ins = [(dq, HPAD, 0), (dk, HPAD, 0), (dv, HPAD, 0), (tabs[0], LANES, 0), (tabs[1], LANES, 0),
           (tabs[2], LANES, 0)]
    outs = [(HPAD, BF16), (2 * HPAD, BF16), (LANES, BF16, dz, Z_BKR // LANES)]
    return _ew("mla_prep_bwd", body, ins, outs, rows)


def _gate_bwd(name, d_o_mm, o_raw, gate, gate_cb, dz, dz_cb, rows):
    def body(ins, outs, _):
        lane = lax.broadcasted_iota(jnp.int32, outs[2].shape, 1)
        delta = jnp.zeros(outs[2].shape, F32)
        for h in range(HEADS):
            cols = slice(h * LANES, (h + 1) * LANES)
            dog, o, g = ins[0][:, cols], ins[1][:, cols], ins[2][:, cols].astype(F32)
            sg = jax.nn.sigmoid(g)
            do = dog * (g * sg)
            outs[0][:, cols] = do.astype(BF16)
            outs[1][:, cols] = (dog * o * (sg * (1.0 + g * (1.0 - sg)))).astype(BF16)
            delta = jnp.where(lane == h, jnp.sum(do * o, axis=-1, keepdims=True), delta)
        outs[2][...] = delta
    ins = [(o_raw, HPAD, 0), (gate, HPAD, gate_cb)]
    outs = [(HPAD, BF16), (HPAD, BF16, dz, dz_cb), (LANES, F32)]
    return _ew(name, body, ins, outs, rows, mms=[d_o_mm])


def _merge_out(ua, ub, z, w_out, x0, rows):
    tm = _row_tile(rows)

    def kern(ua_ref, ub_ref, ma_ref, mb_ref, w_ref, x0_ref, y_ref, x1_ref):
        ua_v, ub_v, m_a, m_b = (r[...].astype(F32) for r in (ua_ref, ub_ref, ma_ref, mb_ref))
        y = (jax.nn.sigmoid(m_a) * ua_v + jax.nn.sigmoid(m_b) * ub_v).astype(BF16)
        y_ref[...] = y
        for j in range(D_MODEL // MM_TN):
            cols = slice(j * MM_TN, (j + 1) * MM_TN)
            x1_ref[:, cols] = jnp.dot(y, w_ref[:, cols], preferred_element_type=F32) + x0_ref[:, cols]

    row = lambda cb: pl.BlockSpec((tm, D_MODEL), lambda i: (i, cb))
    return pl.pallas_call(
        kern, name="merge_out", grid=(rows // tm,),
        in_specs=[row(0), row(0), row(Z_MA // D_MODEL), row(Z_MB // D_MODEL),
                  pl.BlockSpec(w_out.shape, lambda i: (0, 0)), row(0)],
        out_specs=[row(0), row(0)],
        out_shape=[jax.ShapeDtypeStruct((rows, D_MODEL), BF16), jax.ShapeDtypeStruct((rows, D_MODEL), F32)],
        compiler_params=_params(("parallel",)))(ua, ub, z, z, w_out, x0)


def _merge_bwd(dy_mm, ua, ub, z, dz, rows):
    def body(ins, outs, _):
        dyv = ins[0][...]
        for idx in range(2):
            s = jax.nn.sigmoid(ins[3 + idx][...].astype(F32))
            outs[idx][...] = (dyv * s).astype(BF16)
            d_m = (dyv * ins[1 + idx][...].astype(F32) * (s * (1.0 - s))).astype(BF16)
            outs[2][:, idx * D_MODEL:(idx + 1) * D_MODEL] = d_m
    ins = [(ua, D_MODEL, 0), (ub, D_MODEL, 0), (z, D_MODEL, Z_MA // D_MODEL), (z, D_MODEL, Z_MB // D_MODEL)]
    outs = [(D_MODEL, BF16), (D_MODEL, BF16), (2 * D_MODEL, BF16, dz, Z_MA // (2 * D_MODEL))]
    return _ew("merge_bwd", body, ins, outs, rows, mms=[dy_mm])


def _kv_grad_cast(dk, dv, dz, rows):
    def body(ins, outs, _):
        outs[0][:, :256] = ins[0][...].astype(BF16)
        outs[0][:, 256:] = ins[1][...].astype(BF16)
    return _ew("kv_grad_cast", body, [(dk, 256, 0), (dv, 256, 0)], [(512, BF16, dz, Z_AK // 512)], rows)[0]


def _ple_fwd(x1, hn, w_pg, p, w_pp, rows):
    def body(ins, outs, _):
        u, e = ins[0][...], ins[1][...]
        outs[0][...] = ins[2][...] + jax.nn.sigmoid(u) * e
        outs[1][...] = u.astype(BF16)
        outs[2][...] = e.astype(BF16)
    return _ew("ple_fwd", body, [(x1, D_MODEL, 0)], [(D_MODEL, F32), (D_MODEL, BF16), (D_MODEL, BF16)], rows,
               mms=[(hn, w_pg), (p, w_pp)])


def _ple_bwd(dx2, u, e, rows):
    def body(ins, outs, _):
        d, s = ins[0][...], jax.nn.sigmoid(ins[1][...].astype(F32))
        outs[0][...] = (d * s).astype(BF16)
        outs[1][...] = (d * ins[2][...].astype(F32) * (s * (1.0 - s))).astype(BF16)
    return _ew("ple_bwd", body, [(dx2, D_MODEL, 0), (u, D_MODEL, 0), (e, D_MODEL, 0)],
               [(D_MODEL, BF16)] * 2, rows)


def _loss_head(x, g, target, rows):
    def body(ins, outs, accs):
        xv, gv = ins[0][...], ins[1][...]
        r = lax.rsqrt(jnp.mean(xv * xv, axis=-1, keepdims=True) + EPS)
        xhat = xv * r
        err = xhat * gv - ins[2][...]
        accs[0][...] += jnp.broadcast_to(0.5 * jnp.sum(jnp.mean(err * err, axis=-1, keepdims=True),
                                                       axis=0, keepdims=True), (1, LANES))
        dyv = err * (1.0 / D_MODEL)
        accs[1][...] += jnp.sum(dyv * xhat, axis=0, keepdims=True)
        dy = dyv * gv
        outs[0][...] = r * (dy - xhat * jnp.mean(dy * xhat, axis=-1, keepdims=True))
    ins = [(x, D_MODEL, 0), (g.reshape(1, D_MODEL), None, None), (target, D_MODEL, 0)]
    return _ew("loss_head", body, ins, [(D_MODEL, F32)], rows, accs=[(1, LANES), (1, D_MODEL)])


def _pad_heads_cols(w, n_heads, dim):
    k = w.shape[0]
    return jnp.pad(w.reshape(k, n_heads, dim), ((0, 0), (0, 0), (0, LANES - dim))).reshape(k, n_heads * LANES)


def _unpad_heads_cols(w, n_heads, dim):
    k = w.shape[0]
    return w.reshape(k, n_heads, LANES)[:, :, :dim].reshape(k, n_heads * dim)


def _layer_weights(w, i):
    segs = jnp.split(w['w_in'][i], list(_cumsum(IN_SIZES))[:-1], axis=1)
    a_q, a_k, a_v, a_gate, b_qd, b_kvd, b_kr, b_gate, m_a, m_b = segs
    kr = jnp.pad(b_kr, ((0, 0), (HEAD_DIM, LANES - MLA_QK)))
    w_in = jnp.concatenate([
        m_a, m_b, _pad_heads_cols(a_q, HEADS, HEAD_DIM), _pad_heads_cols(a_gate, HEADS, HEAD_DIM),
        _pad_heads_cols(b_gate, HEADS, HEAD_DIM), _pad_heads_cols(a_k, SWA_KV_HEADS, HEAD_DIM),
        _pad_heads_cols(a_v, SWA_KV_HEADS, HEAD_DIM), b_qd, b_kvd, kr], axis=1)
    w_uq = _pad_heads_cols(w['w_uq'][i], HEADS, MLA_QK)
    ukv = w['w_ukv'][i].reshape(MLA_KV_LORA, HEADS, 2 * HEAD_DIM)
    pad = ((0, 0), (0, 0), (0, HEAD_DIM))
    w_ukv = jnp.concatenate([jnp.pad(ukv[:, :, :HEAD_DIM], pad).reshape(MLA_KV_LORA, HPAD),
                             jnp.pad(ukv[:, :, HEAD_DIM:], pad).reshape(MLA_KV_LORA, HPAD)], axis=1)
    w_br_a = _pad_heads_cols(w['w_br_a'][i].T, HEADS, HEAD_DIM).T
    w_br_b = _pad_heads_cols(w['w_br_b'][i].T, HEADS, HEAD_DIM).T
    out = dict(w_in=w_in, w_uq=w_uq, w_ukv=w_ukv, w_br_a=w_br_a, w_br_b=w_br_b, w_out=w['w_out'][i],
               w_pg=w['w_ple_gate'][i], w_pp=w['w_ple_proj'][i])
    for name in ('w_in', 'w_uq', 'w_ukv', 'w_br_a', 'w_br_b', 'w_out', 'w_pg'):
        out[name + '_t'] = out[name].T
    return out


def _cumsum(sizes):
    acc, out = 0, []
    for s in sizes:
        acc += s
        out.append(acc)
    return out


def _unpad_grads(g):
    d = g['w_in']
    seg = lambda off, width: d[:, off:off + width]
    b_kr = seg(Z_BKR, LANES)[:, HEAD_DIM:MLA_QK]
    w_in = jnp.concatenate([
        _unpad_heads_cols(seg(Z_AQ, HPAD), HEADS, HEAD_DIM), _unpad_heads_cols(seg(Z_AK, 256), SWA_KV_HEADS, HEAD_DIM),
        _unpad_heads_cols(seg(Z_AV, 256), SWA_KV_HEADS, HEAD_DIM), _unpad_heads_cols(seg(Z_AGATE, HPAD), HEADS, HEAD_DIM),
        seg(Z_BQD, MLA_Q_LORA), seg(Z_BKVD, MLA_KV_LORA), b_kr, _unpad_heads_cols(seg(Z_BGATE, HPAD), HEADS, HEAD_DIM),
        seg(Z_MA, D_MODEL), seg(Z_MB, D_MODEL)], axis=1)
    w_uq = _unpad_heads_cols(g['w_uq'], HEADS, MLA_QK)
    ukv = g['w_ukv'].reshape(MLA_KV_LORA, 2, HEADS, LANES)[:, :, :, :HEAD_DIM]
    w_ukv = jnp.concatenate([ukv[:, 0], ukv[:, 1]], axis=-1).reshape(MLA_KV_LORA, HEADS * 2 * HEAD_DIM)
    w_br_a = _unpad_heads_cols(g['w_br_a'].T, HEADS, HEAD_DIM).T
    w_br_b = _unpad_heads_cols(g['w_br_b'].T, HEADS, HEAD_DIM).T
    return dict(w_in=w_in, w_uq=w_uq, w_ukv=w_ukv, w_br_a=w_br_a, w_br_b=w_br_b, w_out=g['w_out'],
                w_ple_gate=g['w_pg'], w_ple_proj=g['w_pp'], g_mix=g['g_mix'], sink=g['sink'], g_q=g['g_q'],
                g_kv=g['g_kv'], g_ple=g['g_ple'])


def _layer_fwd(x0, p_i, lw, sm, i, pos_col, pos_row, tabs, B, S):
    T = B * S
    h = _rms_fwd("norm_mix", x0, D_MODEL, 0, sm['g_mix'][i], T)
    z, a_gate = _mm("proj_in", h, lw['w_in'], BF16, f32_cols=(Z_AGATE, HPAD))
    sink_row = jnp.pad(sm['sink'][i], (0, LANES - HEADS)).reshape(1, LANES)
    oa_raw, oa, lse_a = _swa_fwd(z, a_gate, pos_col, pos_row, sink_row, B, S)
    qdn = _rms_fwd("norm_q", z, MLA_Q_LORA, Z_BQD // MLA_Q_LORA, sm['g_q'][i], T)
    kvdn = _rms_fwd("norm_kv", z, MLA_KV_LORA, Z_BKVD // MLA_KV_LORA, sm['g_kv'][i], T)
    qf, kf, vf = _mla_prep(qdn, lw['w_uq'], kvdn, lw['w_ukv'], z, tabs, T)
    ob_raw, ob, lse_b = _mla_fwd(qf, kf, vf, z, B, S)
    ua = _mm("proj_br_a", oa, lw['w_br_a'], BF16)
    ub = _mm("proj_br_b", ob, lw['w_br_b'], BF16)
    y, x1 = _merge_out(ua, ub, z, lw['w_out'], x0, T)
    hn = _rms_fwd("norm_ple", x1, D_MODEL, 0, sm['g_ple'][i], T)
    x2, u, e = _ple_fwd(x1, hn, lw['w_pg'], p_i, lw['w_pp'], T)
    saved = dict(x0=x0, h=h, z=z, a_gate=a_gate, sink_row=sink_row, oa_raw=oa_raw, oa=oa, lse_a=lse_a, qdn=qdn, kvdn=kvdn,
                 qf=qf, kf=kf, vf=vf, ob_raw=ob_raw, ob=ob, lse_b=lse_b, ua=ua, ub=ub, y=y, x1=x1, hn=hn,
                 u=u, e=e, p=p_i)
    return x2, saved


def _layer_bwd(dx2, sv, lw, sm, i, pos_col, pos_row, tabs, B, S):
    T = B * S
    z = sv['z']
    g = {}
    d_e, d_u = _ple_bwd(dx2, sv['u'], sv['e'], T)
    g['w_pp'] = _mm_tn("grad_pp", sv['p'], d_e)
    g['w_pg'] = _mm_tn("grad_pg", sv['hn'], d_u)
    dx1, g['g_ple'] = _rms_bwd("norm_ple_bwd", sv['x1'], D_MODEL, 0, sm['g_ple'][i], (d_u, lw['w_pg_t']), T, F32,
                               dres=dx2)
    g['w_out'] = _mm_tn("grad_out", sv['y'], dx1)
    dz = lax.empty((T, Z_WIDTH), BF16)
    d_ua, d_ub, dz = _merge_bwd((dx1, lw['w_out_t']), sv['ua'], sv['ub'], z, dz, T)
    g['w_br_a'] = _mm_tn("grad_br_a", sv['oa'], d_ua)
    g['w_br_b'] = _mm_tn("grad_br_b", sv['ob'], d_ub)
    dob_raw, dz, delta_b = _gate_bwd("gate_b_bwd", (d_ub, lw['w_br_b_t']), sv['ob_raw'], z, Z_BGATE // HPAD,
                                     dz, Z_BGATE // HPAD, T)
    delta_rows = delta_b[:, :HEADS].reshape(B, S // MLA_T, MLA_T, HEADS // MLA_HG, MLA_HG).transpose(0, 3, 1, 4, 2)
    dq, dk, dv = _mla_bwd(sv['qf'], sv['kf'], sv['vf'], dob_raw, sv['lse_b'], delta_rows, B, S)
    dq_pre, dkv_pre, dz = _mla_prep_bwd(dq, dk, dv, tabs, dz, T)
    g['w_uq'] = _mm_tn("grad_uq", sv['qdn'], dq_pre)
    g['w_ukv'] = _mm_tn("grad_ukv", sv['kvdn'], dkv_pre)
    dz, g['g_q'] = _rms_bwd("norm_q_bwd", z, MLA_Q_LORA, Z_BQD // MLA_Q_LORA, sm['g_q'][i],
                            (dq_pre, lw['w_uq_t']), T, BF16, into=(dz, Z_BQD // MLA_Q_LORA))
    dz, g['g_kv'] = _rms_bwd("norm_kv_bwd", z, MLA_KV_LORA, Z_BKVD // MLA_KV_LORA, sm['g_kv'][i],
                             (dkv_pre, lw['w_ukv_t']), T, BF16, into=(dz, Z_BKVD // MLA_KV_LORA))
    doa_raw, dz, delta_a = _gate_bwd("gate_a_bwd", (d_ua, lw['w_br_a_t']), sv['oa_raw'], sv['a_gate'], 0,
                                     dz, Z_AGATE // HPAD, T)
    dz, d_ak, d_av, dsink = _swa_bwd(z, pos_col, pos_row, sv['sink_row'], sv['lse_a'], doa_raw, delta_a, dz, B, S)
    dz = _kv_grad_cast(d_ak, d_av, dz, T)
    g['sink'] = dsink[0, :HEADS]
    g['w_in'] = _mm_tn("grad_in", sv['h'], dz)
    dx0, g['g_mix'] = _rms_bwd("norm_mix_bwd", sv['x0'], D_MODEL, 0, sm['g_mix'][i], (dz, lw['w_in_t']), T, F32,
                               dres=dx1)
    for name in ('g_ple', 'g_q', 'g_kv', 'g_mix'):
        g[name] = g[name][0]
    return dx0, g


def _local_step(x, p, positions, wfull, sm, loss_target):
    B, S, _ = x.shape
    T = B * S
    pos_col = positions.reshape(T, 1)
    pos_row = positions.reshape(T // BLOCK, 1, BLOCK)
    half = MLA_ROPE // 2
    inv = ROPE_THETA ** (-jnp.arange(0, MLA_ROPE, 2, dtype=F32) / MLA_ROPE)
    inv_lane = jnp.tile(inv, LANES // half).reshape(1, LANES)
    tabs = _rope_tables(pos_col, inv_lane, T)
    xc = x.reshape(T, D_MODEL)
    lws, saved = [], []
    for i in range(DEPTH):
        lw = _layer_weights(wfull, i)
        xc, sv = _layer_fwd(xc, p[i].reshape(T, PLE_DIM), lw, sm, i, pos_col, pos_row, tabs, B, S)
        lws.append(lw)
        saved.append(sv)
    dx, loss, dg_final = _loss_head(xc, sm['g_final'], loss_target.reshape(T, D_MODEL), T)
    layer_grads = [None] * DEPTH
    for i in reversed(range(DEPTH)):
        dx, g = _layer_bwd(dx, saved[i], lws[i], sm, i, pos_col, pos_row, tabs, B, S)
        layer_grads[i] = _unpad_grads(g)
    return loss, dx.reshape(B, S, D_MODEL), layer_grads, dg_final[0]


SMALL_ROWS = 48


def _pack_small(arrs):
    flat = jnp.concatenate([arrs[name].reshape(-1) for name in SMALL])
    return jnp.pad(flat, (0, SMALL_ROWS * LANES - flat.shape[0])).reshape(SMALL_ROWS, LANES)


def _unpack_small(block, shapes):
    flat = block.reshape(-1)
    out, off = {}, 0
    for name in SMALL:
        n = math.prod(shapes[name])
        out[name] = flat[off:off + n].reshape(shapes[name])
        off += n
    return out


def _to_slots(g, axis):
    r, c = g.shape
    if axis == 0:
        return g.reshape(N_CHIPS, r // N_CHIPS, c)
    return g.reshape(r, N_CHIPS, c // N_CHIPS).transpose(1, 0, 2)


def _units(shapes):
    units = []
    for w, shape in enumerate(shapes):
        r = shape[-2]
        n = 4 if r >= 1024 else 1
        units += [(w, k * (r // n), r // n) for k in range(n)]
    return units


def _place():
    x, y, c = lax.axis_index("x"), lax.axis_index("y"), lax.axis_index("c")
    chips = [(1 - x, y), (x, 1 - y), (1 - x, 1 - y)]
    return x, y, c, chips


ANY = pl.BlockSpec(memory_space=pl.ANY)


def _remote(send_sems, recv_sems, k, src, dst, to):
    return pltpu.make_async_remote_copy(src_ref=src, dst_ref=dst, send_sem=send_sems.at[k],
                                        recv_sem=recv_sems.at[k], device_id=to, device_id_type=MESH)


def _gather_weights(shards):
    n = len(shards)
    units = _units([s.shape for s in shards])
    nu = len(units)

    def body(*refs):
        ins, outs = refs[:n], refs[n:2 * n]
        send_sems, recv_sems, local_sems = refs[2 * n:]
        x, y, c, chips = _place()
        me = 2 * x + y
        sibling = (x, y, 1 - c)
        copy = functools.partial(_remote, send_sems, recv_sems)
        keeps, sends = [], []
        for u, (w, r0, nr) in enumerate(units):
            rows = pl.ds(r0, nr)
            keeps.append(pltpu.make_async_copy(ins[w].at[:, rows, :], outs[w].at[me, :, rows, :], local_sems.at[u]))
            keeps[-1].start()
        for j, (cx, cy) in enumerate(chips):
            for u, (w, r0, nr) in enumerate(units):
                rows = pl.ds(r0, nr)
                sends.append(copy(j * nu + u, ins[w].at[c, rows, :], outs[w].at[me, c, rows, :], (cx, cy, c)))
                sends[-1].start()
        for j, (cx, cy) in enumerate(chips):
            for u, (w, r0, nr) in enumerate(units):
                landed = outs[w].at[2 * cx + cy, c, pl.ds(r0, nr), :]
                copy(j * nu + u, landed, landed, (cx, cy, c)).wait_recv()
                sends.append(copy((3 + j) * nu + u, landed, landed, sibling))
                sends[-1].start()
        for j, (cx, cy) in enumerate(chips):
            for u, (w, r0, nr) in enumerate(units):
                other = outs[w].at[2 * cx + cy, 1 - c, pl.ds(r0, nr), :]
                copy((3 + j) * nu + u, other, other, sibling).wait_recv()
        for cp in sends:
            cp.wait_send()
        for keep in keeps:
            keep.wait()

    return pl.pallas_call(
        body, name="gather_weights",
        out_shape=[jax.ShapeDtypeStruct((N_CHIPS,) + s.shape, s.dtype) for s in shards],
        in_specs=[ANY] * n, out_specs=[ANY] * n,
        scratch_shapes=[pltpu.SemaphoreType.DMA((6 * nu,)), pltpu.SemaphoreType.DMA((6 * nu,)),
                        pltpu.SemaphoreType.DMA((nu,))])(*shards)


def _pair_exchange(g0, g1):
    n = len(g0)

    def body(*refs):
        layers, outs = (refs[:n], refs[n:2 * n]), refs[2 * n:3 * n]
        send_sems, recv_sems = refs[3 * n:]
        x, y, c, _ = _place()
        copy = functools.partial(_remote, send_sems, recv_sems)
        for w in range(n):
            for q in range(N_CHIPS):
                for layer in range(DEPTH):
                    cp = copy(N_CHIPS * w + q, layers[layer][w].at[q], outs[w].at[q], (x, y, 1 - c))
                    pl.when(c == 1 - layer)(cp.start)
        for w in range(n):
            for q in range(N_CHIPS):
                copy(N_CHIPS * w + q, layers[0][w].at[q], outs[w].at[q], (x, y, 1 - c)).wait()

    return pl.pallas_call(
        body, name="pair_exchange", out_shape=[jax.ShapeDtypeStruct(g.shape, g.dtype) for g in g0],
        in_specs=[ANY] * (2 * n), out_specs=[ANY] * n,
        scratch_shapes=[pltpu.SemaphoreType.DMA((N_CHIPS * n,)), pltpu.SemaphoreType.DMA((N_CHIPS * n,))])(*g0, *g1)


def _pair_sum(name, g0, g1, theirs, cflag):
    shape = theirs.shape
    rows, width = shape[0] * shape[1], shape[2]

    def body(ins, outs, _):
        mine = jnp.where(ins[3][0:1, 0:1] == 0.0, ins[0][...], ins[1][...])
        tot = mine + ins[2][...]
        outs[0][...] = tot
        outs[1][...] = tot.astype(BF16)
    ins = [(a.reshape(rows, width), width, 0) for a in (g0, g1, theirs)] + [(cflag, None, None)]
    f32, bf16 = _ew(name, body, ins, [(width, F32), (width, BF16)], rows)
    return f32.reshape(shape), bf16.reshape(shape)


def _chip_exchange(parts):
    n = len(parts)

    def body(*refs):
        ins, outs = refs[:n], refs[n:2 * n]
        send_sems, recv_sems = refs[2 * n:]
        x, y, c, chips = _place()
        copy = functools.partial(_remote, send_sems, recv_sems)
        sends = []
        for j, (cx, cy) in enumerate(chips):
            for w in range(n):
                sends.append(copy(j * n + w, ins[w].at[2 * cx + cy], outs[w].at[j], (cx, cy, c)))
                sends[-1].start()
        for j, (cx, cy) in enumerate(chips):
            for w in range(n):
                copy(j * n + w, outs[w].at[j], outs[w].at[j], (cx, cy, c)).wait_recv()
        for cp in sends:
            cp.wait_send()

    return pl.pallas_call(
        body, name="chip_exchange",
        out_shape=[jax.ShapeDtypeStruct((3,) + a.shape[1:], a.dtype) for a in parts],
        in_specs=[ANY] * n, out_specs=[ANY] * n,
        scratch_shapes=[pltpu.SemaphoreType.DMA((3 * n,)), pltpu.SemaphoreType.DMA((3 * n,))])(*parts)


def _chip_sum(name, part, landed, chipflag):
    _, r, width = part.shape
    tm = min(r, 256)

    def kern(p_ref, l_ref, flag_ref, o_ref):
        me = flag_ref[0:1, 0:1]
        own = jnp.where(me == 0.0, p_ref[0], jnp.where(me == 1.0, p_ref[1], jnp.where(me == 2.0, p_ref[2], p_ref[3])))
        o_ref[...] = ((own + l_ref[0].astype(F32)) + l_ref[1].astype(F32)) + l_ref[2].astype(F32)

    return pl.pallas_call(
        kern, name=name, grid=(r // tm,),
        in_specs=[pl.BlockSpec((N_CHIPS, tm, width), lambda i: (0, i, 0)),
                  pl.BlockSpec((3, tm, width), lambda i: (0, i, 0)),
                  pl.BlockSpec((1, LANES), lambda i: (0, 0))],
        out_specs=pl.BlockSpec((tm, width), lambda i: (i, 0)),
        out_shape=jax.ShapeDtypeStruct((r, width), F32), compiler_params=_params(("arbitrary",)))(part, landed, chipflag)


def _pair_broadcast(mine):
    n = len(mine)
    units = _units([a.shape for a in mine])

    def body(*refs):
        ins, outs = refs[:n], refs[n:2 * n]
        send_sems, recv_sems = refs[2 * n:]
        x, y, c, _ = _place()
        copy = functools.partial(_remote, send_sems, recv_sems)
        cps = [copy(u, ins[w].at[pl.ds(r0, nr), :], outs[w].at[pl.ds(r0, nr), :], (x, y, 1 - c))
               for u, (w, r0, nr) in enumerate(units)]
        for cp in cps:
            cp.start()
        for cp in cps:
            cp.wait()

    return pl.pallas_call(
        body, name="pair_broadcast", out_shape=[jax.ShapeDtypeStruct(a.shape, a.dtype) for a in mine],
        in_specs=[ANY] * n, out_specs=[ANY] * n,
        scratch_shapes=[pltpu.SemaphoreType.DMA((len(units),)), pltpu.SemaphoreType.DMA((len(units),))])(*mine)


def _small_allreduce(v):
    offsets = [(dx, dy, dc) for dx in (0, 1) for dy in (0, 1) for dc in (0, 1)][1:]

    def body(v_ref, out_ref, recv_ref, send_sems, recv_sems):
        x, y, c, _ = _place()
        flip = lambda a, d: 1 - a if d else a
        peers = [(flip(x, dx), flip(y, dy), flip(c, dc)) for dx, dy, dc in offsets]
        copy = functools.partial(_remote, send_sems, recv_sems)
        me = 4 * x + 2 * y + c
        recv_ref[me] = v_ref[...]
        cps = [copy(k, v_ref, recv_ref.at[me], peer) for k, peer in enumerate(peers)]
        for cp in cps:
            cp.start()
        for k, (px, py, pc) in enumerate(peers):
            landed = recv_ref.at[4 * px + 2 * py + pc]
            copy(k, landed, landed, (px, py, pc)).wait_recv()
        for cp in cps:
            cp.wait_send()
        tot = recv_ref[0]
        for d in range(1, 8):
            tot = tot + recv_ref[d]
        out_ref[...] = tot

    vmem = pl.BlockSpec(memory_space=pltpu.VMEM)
    return pl.pallas_call(
        body, name="small_allreduce", out_shape=jax.ShapeDtypeStruct(v.shape, v.dtype),
        in_specs=[vmem], out_specs=vmem,
        scratch_shapes=[pltpu.VMEM((8,) + v.shape, v.dtype), pltpu.SemaphoreType.DMA((7,)),
                        pltpu.SemaphoreType.DMA((7,))])(v)


def _adam_math(gv, wv, mv, vv):
    mv = ADAM_B1 * mv + (1.0 - ADAM_B1) * gv
    vv = ADAM_B2 * vv + (1.0 - ADAM_B2) * (gv * gv)
    m_hat = mv / (1.0 - ADAM_B1 ** ADAM_STEP)
    v_hat = vv / (1.0 - ADAM_B2 ** ADAM_STEP)
    return -ADAM_LR * (m_hat / (jnp.sqrt(v_hat) + ADAM_EPS) + ADAM_WD * wv), mv, vv


def _adamw_big(name, mine, theirs, cflag, w, m, v):
    _, r, width = w.shape
    tm = min(r, 256)

    def kern(mine_ref, theirs_ref, flag_ref, w_ref, m_ref, v_ref, g_ref, d_ref, nm_ref, nv_ref):
        layer = pl.program_id(0).astype(F32)
        gv = jnp.where(flag_ref[0:1, 0:1] == layer, mine_ref[...], theirs_ref[...])
        g_ref[0] = gv
        d_ref[0], nm_ref[0], nv_ref[0] = _adam_math(gv, w_ref[0], m_ref[0], v_ref[0])

    flat = pl.BlockSpec((tm, width), lambda l, i: (i, 0))
    stacked = pl.BlockSpec((1, tm, width), lambda l, i: (l, i, 0))
    return pl.pallas_call(
        kern, name=name, grid=(DEPTH, r // tm),
        in_specs=[flat, flat, pl.BlockSpec((1, LANES), lambda l, i: (0, 0)), stacked, stacked, stacked],
        out_specs=[stacked] * 4, out_shape=[jax.ShapeDtypeStruct(w.shape, F32)] * 4,
        compiler_params=_params(("arbitrary", "arbitrary")))(mine, theirs, cflag, w, m, v)


def _adamw_small(g, w, m, v):
    def body(ins, outs, _):
        outs[0][...], outs[1][...], outs[2][...] = _adam_math(*(r[...] for r in ins))
    return _ew("adamw_small", body, [(a, LANES, 0) for a in (g, w, m, v)], [(LANES, F32)] * 3, SMALL_ROWS)


def kernel(x, p, positions, g_mix, w_in, sink, g_q, w_uq, g_kv, w_ukv, w_br_a, w_br_b, w_out, g_ple, w_ple_gate, w_ple_proj, g_final, loss_target, m_g_mix, m_w_in, m_sink, m_g_q, m_w_uq, m_g_kv, m_w_ukv, m_w_br_a, m_w_br_b, m_w_out, m_g_ple, m_w_ple_gate, m_w_ple_proj, m_g_final, v_g_mix, v_w_in, v_sink, v_g_q, v_w_uq, v_g_kv, v_w_ukv, v_w_br_a, v_w_br_b, v_w_out, v_g_ple, v_w_ple_gate, v_w_ple_proj, v_g_final):
    w = dict(g_mix=g_mix, w_in=w_in, sink=sink, g_q=g_q, w_uq=w_uq, g_kv=g_kv, w_ukv=w_ukv, w_br_a=w_br_a,
             w_br_b=w_br_b, w_out=w_out, g_ple=g_ple, w_ple_gate=w_ple_gate, w_ple_proj=w_ple_proj, g_final=g_final)
    m = dict(g_mix=m_g_mix, w_in=m_w_in, sink=m_sink, g_q=m_g_q, w_uq=m_w_uq, g_kv=m_g_kv, w_ukv=m_w_ukv,
             w_br_a=m_w_br_a, w_br_b=m_w_br_b, w_out=m_w_out, g_ple=m_g_ple, w_ple_gate=m_w_ple_gate,
             w_ple_proj=m_w_ple_proj, g_final=m_g_final)
    v = dict(g_mix=v_g_mix, w_in=v_w_in, sink=v_sink, g_q=v_g_q, w_uq=v_w_uq, g_kv=v_g_kv, w_ukv=v_w_ukv,
             w_br_a=v_w_br_a, w_br_b=v_w_br_b, w_out=v_w_out, g_ple=v_g_ple, w_ple_gate=v_w_ple_gate,
             w_ple_proj=v_w_ple_proj, g_final=v_g_final)
    wfull = _gather_full(w)
    sm = {name: w[name] for name in SMALL}
    loss_row, grad_x, layer_grads, dg_final = _local_step(x, p, positions, wfull, sm, loss_target)
    loss = lax.psum(loss_row[0, 0], ("x", "y", "c"))
    res = _update(layer_grads, dg_final, w, m, v)
    return (loss, grad_x, *[res[name][kind] for kind in range(4) for name in WEIGHT_NAMES])


def _gather_full(w):
    gathered = _gather_weights([w[name].astype(BF16) for name, _ in SHARDED])
    return {name: [jnp.concatenate([gathered[k][q, layer] for q in range(N_CHIPS)], axis=axis - 1)
                   for layer in range(DEPTH)] for k, (name, axis) in enumerate(SHARDED)}


def _update(layer_grads, dg_final, w, m, v):
    small_shapes = {name: w[name].shape for name in SMALL}
    cflag = jnp.full((1, LANES), lax.axis_index("c"), F32)
    chipflag = jnp.full((1, LANES), 2 * lax.axis_index("x") + lax.axis_index("y"), F32)

    slots = [[_to_slots(layer_grads[layer][name], axis - 1) for name, axis in SHARDED] for layer in range(DEPTH)]
    theirs = _pair_exchange(slots[0], slots[1])
    pair = [_pair_sum("pair_sum_" + name, slots[0][k], slots[1][k], theirs[k], cflag)
            for k, (name, _) in enumerate(SHARDED)]
    landed = _chip_exchange([bf16 for _, bf16 in pair])
    mine = [_chip_sum("chip_sum_" + name, pair[k][0], landed[k], chipflag) for k, (name, _) in enumerate(SHARDED)]
    other = _pair_broadcast(mine)
    res = {name: _adamw_big("adamw_" + name, mine[k], other[k], cflag, w[name], m[name], v[name])
           for k, (name, _) in enumerate(SHARDED)}

    gsmall = {name: jnp.stack([layer_grads[layer][name] for layer in range(DEPTH)]) for name in SMALL[:-1]}
    gsmall['g_final'] = dg_final
    gsum = _small_allreduce(_pack_small(gsmall))
    small = (gsum,) + tuple(_adamw_small(gsum, _pack_small(w), _pack_small(m), _pack_small(v)))
    for name, arrs in zip(SMALL, zip(*[[_unpack_small(a, small_shapes)[n] for n in SMALL] for a in small])):
        res[name] = arrs
    return res
```

```python
import functools
import math

import jax
import jax.numpy as jnp
from jax import lax
from jax.experimental import pallas as pl
from jax.experimental.pallas import tpu as pltpu

F32 = jnp.float32
BF16 = jnp.bfloat16

D_MODEL = 1024
DEPTH = 2
PLE_DIM = 256
BLOCK = 128
EPS = 1e-6
NEG = -1e30
HEADS = 8
SWA_KV_HEADS = 2
HEAD_DIM = 64
LANES = 128
HPAD = HEADS * LANES
MLA_QK = 96
MLA_ROPE = 32
MLA_Q_LORA = 256
MLA_KV_LORA = 128
ROPE_THETA = 10000.0
IN_SIZES = (512, 128, 128, 512, 256, 128, 32, 512, 1024, 1024)

Z_MA, Z_MB, Z_AQ, Z_AGATE, Z_BGATE = 0, 1024, 2048, 3072, 3584
Z_AK, Z_AV, Z_BQD, Z_BKVD, Z_BKR = 4096, 4352, 4608, 4864, 4992
Z_WIDTH = 5120
GATE_W = HEADS * HEAD_DIM

ADAM_LR, ADAM_B1, ADAM_B2, ADAM_EPS, ADAM_WD, ADAM_STEP = 0.001, 0.9, 0.999, 1e-08, 0.01, 10

VMEM_LIMIT = 56 * 1024 * 1024
MESH = pl.DeviceIdType.MESH

WEIGHT_NAMES = ('g_mix', 'w_in', 'sink', 'g_q', 'w_uq', 'g_kv', 'w_ukv', 'w_br_a', 'w_br_b',
                'w_out', 'g_ple', 'w_ple_gate', 'w_ple_proj', 'g_final')
SHARDED = (('w_in', 2), ('w_uq', 2), ('w_ukv', 2), ('w_br_a', 2), ('w_br_b', 2),
           ('w_out', 1), ('w_ple_gate', 1), ('w_ple_proj', 2))
SMALL = ('g_mix', 'sink', 'g_q', 'g_kv', 'g_ple', 'g_final')
N_CHIPS = 4


def _params(sem):
    return pltpu.CompilerParams(dimension_semantics=sem, vmem_limit_bytes=VMEM_LIMIT)


MM_TN = 512
ROW_TILE = 512
BIG_WEIGHT_BYTES = 8 * 1024 * 1024


def _row_tile(rows, weight_bytes=0):
    tm = ROW_TILE // 2 if weight_bytes > BIG_WEIGHT_BYTES else ROW_TILE
    return min(tm, rows)


def _ew(name, body, ins, outs, rows, accs=(), mms=(), tm=None):
    n_mm, n_in, n_out = len(mms), len(ins), len(outs)
    if tm is None:
        tm = _row_tile(rows, sum(b.size * b.dtype.itemsize for _, b in mms))
    in_specs, args = [], []
    for a, b in mms:
        in_specs += [pl.BlockSpec((tm, a.shape[1]), lambda i: (i, 0)), pl.BlockSpec(b.shape, lambda i: (0, 0))]
        args += [a, b]
    for arr, width, cb in ins:
        if width is None:
            in_specs.append(pl.BlockSpec(arr.shape, lambda i, nd=arr.ndim: (0,) * nd))
        else:
            in_specs.append(pl.BlockSpec((tm, width), lambda i, cb=cb: (i, cb)))
        args.append(arr)
    out_shape, out_specs, aliases = [], [], {}
    for k, out in enumerate(outs):
        if len(out) == 4:
            aliases[len(args)] = k
            in_specs.append(pl.BlockSpec(memory_space=pl.ANY))
            args.append(out[2])
            out_shape.append(jax.ShapeDtypeStruct(out[2].shape, out[2].dtype))
            out_specs.append(pl.BlockSpec((tm, out[0]), lambda i, cb=out[3]: (i, cb)))
        else:
            out_shape.append(jax.ShapeDtypeStruct((rows, out[0]), out[1]))
            out_specs.append(pl.BlockSpec((tm, out[0]), lambda i: (i, 0)))
    n_in += len(aliases)
    out_shape += [jax.ShapeDtypeStruct(s, F32) for s in accs]
    out_specs += [pl.BlockSpec(s, lambda i: (0, 0)) for s in accs]

    def kern(*refs):
        mm_refs, refs = refs[:2 * n_mm], refs[2 * n_mm:]
        in_refs, out_refs = refs[:n_in - len(aliases)], refs[n_in:n_in + n_out]
        acc_refs, prod_refs = refs[n_in + n_out:n_in + n_out + len(accs)], refs[n_in + n_out + len(accs):]
        if acc_refs:
            @pl.when(pl.program_id(0) == 0)
            def _():
                for r in acc_refs:
                    r[...] = jnp.zeros_like(r)
        for k in range(n_mm):
            a_ref, b_ref, prod = mm_refs[2 * k], mm_refs[2 * k + 1], prod_refs[k]
            av = a_ref[...].astype(BF16)
            n = b_ref.shape[1]
            tn = min(MM_TN, n)
            for j in range(n // tn):
                cols = slice(j * tn, (j + 1) * tn)
                prod[:, cols] = jnp.dot(av, b_ref[:, cols], preferred_element_type=F32)
        body(tuple(prod_refs) + tuple(in_refs), out_refs, acc_refs)

    scratch = [pltpu.VMEM((tm, b.shape[1]), F32) for _, b in mms]
    res = pl.pallas_call(kern, name=name, grid=(rows // tm,), in_specs=in_specs, out_specs=out_specs,
                         out_shape=out_shape, scratch_shapes=scratch, input_output_aliases=aliases,
                         compiler_params=_params(("arbitrary",)))(*args)
    return res


def _rms_fwd(name, x, width, cb, g, rows):
    def body(ins, outs, _):
        xv = ins[0][...].astype(F32)
        r = lax.rsqrt(jnp.mean(xv * xv, axis=-1, keepdims=True) + EPS)
        outs[0][...] = ((xv * r) * ins[1][...]).astype(BF16)
    return _ew(name, body, [(x, width, cb), (g.reshape(1, width), None, None)], [(width, BF16)], rows)[0]


def _rms_bwd(name, x, width, cb, g, dh_mm, rows, out_dtype, dres=None, into=()):
    def body(ins, outs, accs):
        dhv, xv, gv = ins[0][...], ins[1][...].astype(F32), ins[2][...]
        r = lax.rsqrt(jnp.mean(xv * xv, axis=-1, keepdims=True) + EPS)
        xhat = xv * r
        accs[0][...] += jnp.sum(dhv * xhat, axis=0, keepdims=True)
        dy = dhv * gv
        dx = r * (dy - xhat * jnp.mean(dy * xhat, axis=-1, keepdims=True))
        if dres is not None:
            dx = dx + ins[3][...]
        outs[0][...] = dx.astype(out_dtype)
    ins = [(x, width, cb), (g.reshape(1, width), None, None)]
    if dres is not None:
        ins.append((dres, width, 0))
    return _ew(name, body, ins, [(width, out_dtype) + tuple(into)], rows, accs=[(1, width)], mms=[dh_mm])


def _mm(name, a, b, out_dtype, residual=None, f32_cols=None, tn=MM_TN):
    M, K = a.shape
    N = b.shape[1]
    tm, tn = _row_tile(M, b.size * b.dtype.itemsize), min(tn, N)
    has_res = residual is not None
    c0, cw = f32_cols if f32_cols else (0, 0)

    def kern(*refs):
        a_ref, b_ref = refs[0], refs[1]
        o_ref = refs[3] if has_res else refs[2]
        av = a_ref[...].astype(BF16)
        for j in range(N // tn):
            cols = slice(j * tn, (j + 1) * tn)
            part = jnp.dot(av, b_ref[:, cols], preferred_element_type=F32)
            if has_res:
                part = part + refs[2][:, cols]
            o_ref[:, cols] = part.astype(o_ref.dtype)
            if c0 <= j * tn and (j + 1) * tn <= c0 + cw:
                refs[-1][:, j * tn - c0:(j + 1) * tn - c0] = part

    in_specs = [pl.BlockSpec((tm, K), lambda i: (i, 0)), pl.BlockSpec((K, N), lambda i: (0, 0))]
    args = [a, b]
    if has_res:
        in_specs.append(pl.BlockSpec((tm, N), lambda i: (i, 0)))
        args.append(residual)
    out_specs = [pl.BlockSpec((tm, N), lambda i: (i, 0))]
    out_shape = [jax.ShapeDtypeStruct((M, N), out_dtype)]
    if f32_cols:
        assert c0 % tn == 0 and cw % tn == 0
        out_specs.append(pl.BlockSpec((tm, cw), lambda i: (i, 0)))
        out_shape.append(jax.ShapeDtypeStruct((M, cw), F32))
    res = pl.pallas_call(kern, name=name, grid=(M // tm,), in_specs=in_specs, out_specs=out_specs,
                         out_shape=out_shape, compiler_params=_params(("parallel",)))(*args)
    return res if f32_cols else res[0]


def _mm_tn(name, a, b, tk=512, tn=2048):
    T, M = a.shape
    N = b.shape[1]
    tn, tk = min(tn, N), min(tk, T)

    def kern(a_ref, b_ref, o_ref):
        k = pl.program_id(1)
        part = _dot_tn(a_ref[...].astype(BF16), b_ref[...].astype(BF16))

        @pl.when(k == 0)
        def _():
            o_ref[...] = part

        @pl.when(k > 0)
        def _():
            o_ref[...] += part

    return pl.pallas_call(
        kern, name=name, grid=(N // tn, T // tk),
        in_specs=[pl.BlockSpec((tk, M), lambda j, k: (k, 0)), pl.BlockSpec((tk, tn), lambda j, k: (k, j))],
        out_specs=pl.BlockSpec((M, tn), lambda j, k: (0, j)),
        out_shape=jax.ShapeDtypeStruct((M, N), F32),
        compiler_params=_params(("parallel", "arbitrary")))(a, b)


def _dot_nt(a, b):
    return lax.dot_general(a, b, (((1,), (1,)), ((), ())), preferred_element_type=F32)


def _dot_tn(a, b):
    return lax.dot_general(a, b, (((0,), (0,)), ((), ())), preferred_element_type=F32)


SWA_SCALE = HEAD_DIM ** -0.5


def _swa_band(n, pq_ref, pkp_ref, pkc_ref):
    posk = jnp.concatenate([pkp_ref[0], pkc_ref[0]], axis=1)
    dist = (pq_ref[...] - posk).astype(F32)
    qi = lax.broadcasted_iota(jnp.int32, (BLOCK, 2 * BLOCK), 0)
    kj = lax.broadcasted_iota(jnp.int32, (BLOCK, 2 * BLOCK), 1)
    t_abs = n * BLOCK + qi
    s_abs = n * BLOCK - BLOCK + kj
    return dist, (s_abs >= 0) & (s_abs <= t_abs) & (t_abs - s_abs < BLOCK)


SWA_GROUP = HEADS // SWA_KV_HEADS


def _head_gate(gate_ref, h):
    pair = gate_ref[:, (h // 2) * LANES:(h // 2 + 1) * LANES].astype(F32)
    return pair if h % 2 == 0 else pltpu.roll(pair, HEAD_DIM, 1)


def _swa_group_q(q_all, g):
    heads = range(g * SWA_GROUP, (g + 1) * SWA_GROUP)
    return jnp.concatenate([(q_all[:, h * LANES:(h + 1) * LANES] * SWA_SCALE).astype(BF16) for h in heads], axis=0)


def _swa_mask(s, dist, valid, h):
    return jnp.where(valid, s - (2.0 ** -(h + 1)) * dist, NEG)


def _swa_specs(nb):
    prev = lambda b, n: b * nb + jnp.maximum(n - 1, 0)
    own = lambda b, n: b * nb + n
    return [
        pl.BlockSpec((BLOCK, HPAD), lambda b, n: (own(b, n), Z_AQ // HPAD)),
        pl.BlockSpec((BLOCK, 256), lambda b, n: (prev(b, n), Z_AK // 256)),
        pl.BlockSpec((BLOCK, 256), lambda b, n: (own(b, n), Z_AK // 256)),
        pl.BlockSpec((BLOCK, 256), lambda b, n: (prev(b, n), Z_AV // 256)),
        pl.BlockSpec((BLOCK, 256), lambda b, n: (own(b, n), Z_AV // 256)),
        pl.BlockSpec((BLOCK, 1), lambda b, n: (own(b, n), 0)),
        pl.BlockSpec((1, 1, BLOCK), lambda b, n: (prev(b, n), 0, 0)),
        pl.BlockSpec((1, 1, BLOCK), lambda b, n: (own(b, n), 0, 0)),
    ]


def _swa_fwd(z, gate, pos_col, pos_row, sink_row, B, S):
    nb = S // BLOCK
    T = B * S

    def kern(q_ref, kp_ref, kc_ref, vp_ref, vc_ref, pq_ref, pkp_ref, pkc_ref, gate_ref, sink_ref,
             oraw_ref, og_ref, lse_ref):
        q_all = q_ref[...]
        kb = jnp.concatenate([kp_ref[...], kc_ref[...]], axis=0).astype(BF16)
        vb = jnp.concatenate([vp_ref[...], vc_ref[...]], axis=0).astype(BF16)
        dist, valid = _swa_band(pl.program_id(1), pq_ref, pkp_ref, pkc_ref)
        lane = lax.broadcasted_iota(jnp.int32, (BLOCK, LANES), 1)
        lse_all = jnp.zeros((BLOCK, LANES), F32)
        for grp in range(SWA_KV_HEADS):
            gcols = slice(grp * LANES, (grp + 1) * LANES)
            s_all = _dot_nt(_swa_group_q(q_all, grp), kb[:, gcols])
            probs = []
            for hh in range(SWA_GROUP):
                h = grp * SWA_GROUP + hh
                s = _swa_mask(s_all[hh * BLOCK:(hh + 1) * BLOCK], dist, valid, h)
                sink_h = sink_ref[0:1, h:h + 1]
                m = jnp.maximum(jnp.max(s, axis=-1, keepdims=True), sink_h)
                e = jnp.exp(s - m)
                denom = jnp.sum(e, axis=-1, keepdims=True) + jnp.exp(sink_h - m)
                probs.append((e * (1.0 / denom)).astype(BF16))
                lse_all = jnp.where(lane == h, m + jnp.log(denom), lse_all)
            o_all = jnp.dot(jnp.concatenate(probs, axis=0), vb[:, gcols], preferred_element_type=F32)
            for hh in range(SWA_GROUP):
                cols = slice((grp * SWA_GROUP + hh) * LANES, (grp * SWA_GROUP + hh + 1) * LANES)
                o = o_all[hh * BLOCK:(hh + 1) * BLOCK]
                oraw_ref[:, cols] = o
                g = _head_gate(gate_ref, grp * SWA_GROUP + hh)
                og_ref[:, cols] = (o * (g * jax.nn.sigmoid(g))).astype(BF16)
        lse_ref[...] = lse_all

    own = lambda b, n: b * nb + n
    in_specs = _swa_specs(nb) + [
        pl.BlockSpec((BLOCK, GATE_W), lambda b, n: (own(b, n), 0)),
        pl.BlockSpec((1, LANES), lambda b, n: (0, 0)),
    ]
    out_specs = [pl.BlockSpec((BLOCK, HPAD), lambda b, n: (own(b, n), 0)),
                 pl.BlockSpec((BLOCK, HPAD), lambda b, n: (own(b, n), 0)),
                 pl.BlockSpec((BLOCK, LANES), lambda b, n: (own(b, n), 0))]
    out_shape = [jax.ShapeDtypeStruct((T, HPAD), F32), jax.ShapeDtypeStruct((T, HPAD), BF16),
                 jax.ShapeDtypeStruct((T, LANES), F32)]
    return pl.pallas_call(kern, name="swa_fwd", grid=(B, nb), in_specs=in_specs, out_specs=out_specs,
                          out_shape=out_shape, compiler_params=_params(("parallel", "arbitrary")))(
        z, z, z, z, z, pos_col, pos_row, pos_row, gate, sink_row)


def _swa_bwd(z, pos_col, pos_row, sink_row, lse, do_raw, delta, dz, B, S):
    nb = S // BLOCK
    T = B * S

    def kern(q_ref, kp_ref, kc_ref, vp_ref, vc_ref, pq_ref, pkp_ref, pkc_ref, sink_ref, lse_ref, do_ref,
             delta_ref, dz_ref, dq_ref, dk_ref, dv_ref, dsink_ref):
        b, n = pl.program_id(0), pl.program_id(1)

        @pl.when(n == 0)
        def _():
            dk_ref[...] = jnp.zeros_like(dk_ref)
            dv_ref[...] = jnp.zeros_like(dv_ref)

        @pl.when((b == 0) & (n == 0))
        def _():
            dsink_ref[...] = jnp.zeros_like(dsink_ref)

        q_all = q_ref[...]
        kb = jnp.concatenate([kp_ref[...], kc_ref[...]], axis=0).astype(BF16)
        vb = jnp.concatenate([vp_ref[...], vc_ref[...]], axis=0).astype(BF16)
        dist, valid = _swa_band(n, pq_ref, pkp_ref, pkc_ref)
        lane1 = lax.broadcasted_iota(jnp.int32, (1, LANES), 1)
        dsink = jnp.zeros((1, LANES), F32)
        dk_band, dv_band = [], []
        for grp in range(SWA_KV_HEADS):
            gcols = slice(grp * LANES, (grp + 1) * LANES)
            heads = range(grp * SWA_GROUP, (grp + 1) * SWA_GROUP)
            qg = _swa_group_q(q_all, grp)
            dog = jnp.concatenate([do_ref[:, h * LANES:(h + 1) * LANES] for h in heads], axis=0)
            s_all = _dot_nt(qg, kb[:, gcols])
            dp_all = _dot_nt(dog, vb[:, gcols])
            ps, dss = [], []
            for hh, h in enumerate(heads):
                blk = slice(hh * BLOCK, (hh + 1) * BLOCK)
                lse_h = lse_ref[:, h:h + 1]
                delta_h = delta_ref[:, h:h + 1]
                p = jnp.exp(_swa_mask(s_all[blk], dist, valid, h) - lse_h)
                ps.append(p.astype(BF16))
                dss.append((p * (dp_all[blk] - delta_h)).astype(BF16))
                psink = jnp.exp(sink_ref[0:1, h:h + 1] - lse_h)
                dsink = dsink + jnp.where(lane1 == h, -jnp.sum(psink * delta_h, axis=0, keepdims=True), 0.0)
            dsg = jnp.concatenate(dss, axis=0)
            dq_all = jnp.dot(dsg, kb[:, gcols], preferred_element_type=F32) * SWA_SCALE
            for hh, h in enumerate(heads):
                dq_ref[:, h * LANES:(h + 1) * LANES] = dq_all[hh * BLOCK:(hh + 1) * BLOCK].astype(BF16)
            dk_band.append(jnp.dot(qg.T, dsg, preferred_element_type=F32).T)
            dv_band.append(jnp.dot(dog.T, jnp.concatenate(ps, axis=0), preferred_element_type=F32).T)
        dsink_ref[...] += dsink
        dkb = jnp.concatenate(dk_band, axis=1)
        dvb = jnp.concatenate(dv_band, axis=1)
        r_prev = pl.ds(pl.multiple_of(jnp.maximum(n - 1, 0) * BLOCK, BLOCK), BLOCK)
        r_own = pl.ds(pl.multiple_of(n * BLOCK, BLOCK), BLOCK)
        dk_ref[r_prev, :] += dkb[:BLOCK]
        dk_ref[r_own, :] += dkb[BLOCK:]
        dv_ref[r_prev, :] += dvb[:BLOCK]
        dv_ref[r_own, :] += dvb[BLOCK:]

    own = lambda b, n: b * nb + n
    in_specs = _swa_specs(nb) + [
        pl.BlockSpec((1, LANES), lambda b, n: (0, 0)),
        pl.BlockSpec((BLOCK, LANES), lambda b, n: (own(b, n), 0)),
        pl.BlockSpec((BLOCK, HPAD), lambda b, n: (own(b, n), 0)),
        pl.BlockSpec((BLOCK, LANES), lambda b, n: (own(b, n), 0)),
        pl.BlockSpec(memory_space=pl.ANY),
    ]
    out_specs = [pl.BlockSpec((BLOCK, HPAD), lambda b, n: (own(b, n), Z_AQ // HPAD)),
                 pl.BlockSpec((S, 256), lambda b, n: (b, 0)),
                 pl.BlockSpec((S, 256), lambda b, n: (b, 0)),
                 pl.BlockSpec((1, LANES), lambda b, n: (0, 0))]
    out_shape = [jax.ShapeDtypeStruct(dz.shape, dz.dtype), jax.ShapeDtypeStruct((T, 256), F32),
                 jax.ShapeDtypeStruct((T, 256), F32), jax.ShapeDtypeStruct((1, LANES), F32)]
    return pl.pallas_call(kern, name="swa_bwd", grid=(B, nb), in_specs=in_specs, out_specs=out_specs,
                          out_shape=out_shape, input_output_aliases={len(in_specs) - 1: 0},
                          compiler_params=_params(("arbitrary", "arbitrary")))(
        z, z, z, z, z, pos_col, pos_row, pos_row, sink_row, lse, do_raw, delta, dz)


MLA_T = 256
MLA_HG = 4
MLA_W = MLA_HG * LANES
MLA_SCALE = MLA_QK ** -0.5
LOG2E = 1.4426950408889634
MLA_QSCALE = MLA_SCALE * LOG2E


def _causal_t(s):
    key = lax.broadcasted_iota(jnp.int32, s.shape, 0)
    query = lax.broadcasted_iota(jnp.int32, s.shape, 1)
    return jnp.where(key <= query, s, NEG)


def _mla_fwd(q, k, v, z, B, S):
    T = B * S
    nq = S // MLA_T

    def kern(q_ref, k_ref, v_ref, gate_ref, oraw_ref, og_ref, lse_ref):
        i = pl.program_id(2)

        def scores(j):
            rows = pl.ds(pl.multiple_of(j * MLA_T, MLA_T), MLA_T)
            return tuple(_dot_nt(k_ref[rows, hh * LANES:(hh + 1) * LANES], q_ref[:, hh * LANES:(hh + 1) * LANES])
                         for hh in range(MLA_HG))

        def update(j, ss, state):
            rows = pl.ds(pl.multiple_of(j * MLA_T, MLA_T), MLA_T)
            out = []
            for hh in range(MLA_HG):
                (m, l, acc), s = state[hh], ss[hh]
                m_new = jnp.maximum(m, jnp.max(s, axis=0, keepdims=True))
                alpha = jnp.exp2(m - m_new)
                p = jnp.exp2(s - m_new)
                l = alpha * l + jnp.sum(p, axis=0, keepdims=True)
                pv = jnp.dot(v_ref[rows, hh * LANES:(hh + 1) * LANES].T, p.astype(BF16), preferred_element_type=F32)
                out.append((m_new, l, alpha * acc + pv))
            return tuple(out)

        def body(j, carry):
            state, ss = carry
            s_next = scores(j + 1)
            return update(j, ss, state), s_next

        init = tuple((jnp.full((1, MLA_T), NEG, F32), jnp.zeros((1, MLA_T), F32), jnp.zeros((LANES, MLA_T), F32))
                     for _ in range(MLA_HG))
        state, ss = lax.fori_loop(0, i, body, (init, scores(0)))
        state = update(i, tuple(_causal_t(s) for s in ss), state)
        for hh in range(MLA_HG):
            m, l, acc = state[hh]
            cols = slice(hh * LANES, (hh + 1) * LANES)
            o = (acc * (1.0 / l)).T
            oraw_ref[:, cols] = o
            g = _head_gate(gate_ref, hh)
            og_ref[:, cols] = (o * (g * jax.nn.sigmoid(g))).astype(BF16)
            lse_ref[0, 0, 0, hh:hh + 1, :] = m + jnp.log2(l)

    blk = lambda b, h, i: (b * nq + i, h)
    in_specs = [pl.BlockSpec((MLA_T, MLA_W), blk),
                pl.BlockSpec((S, MLA_W), lambda b, h, i: (b, h)),
                pl.BlockSpec((S, MLA_W), lambda b, h, i: (b, h)),
                pl.BlockSpec((MLA_T, MLA_W // 2), lambda b, h, i: (b * nq + i, Z_BGATE // (MLA_W // 2) + h))]
    out_specs = [pl.BlockSpec((MLA_T, MLA_W), blk), pl.BlockSpec((MLA_T, MLA_W), blk),
                 pl.BlockSpec((1, 1, 1, MLA_HG, MLA_T), lambda b, h, i: (b, h, i, 0, 0))]
    out_shape = [jax.ShapeDtypeStruct((T, HPAD), F32), jax.ShapeDtypeStruct((T, HPAD), BF16),
                 jax.ShapeDtypeStruct((B, HEADS // MLA_HG, nq, MLA_HG, MLA_T), F32)]
    return pl.pallas_call(kern, name="mla_fwd", grid=(B, HEADS // MLA_HG, nq), in_specs=in_specs,
                          out_specs=out_specs, out_shape=out_shape,
                          compiler_params=_params(("parallel", "parallel", "arbitrary")))(q, k, v, z)


def _mla_bwd(q, k, v, do_raw, lse, delta, B, S):
    T = B * S
    nk = S // MLA_T

    def kern(q_ref, k_ref, v_ref, do_ref, lse_ref, delta_ref, dq_ref, dk_ref, dv_ref, dq_acc, dk_acc, dv_acc):
        j = pl.program_id(2)

        @pl.when(j == 0)
        def _():
            dq_acc[...] = jnp.zeros_like(dq_acc)

        dk_acc[...] = jnp.zeros_like(dk_acc)
        dv_acc[...] = jnp.zeros_like(dv_acc)
        kts = [k_ref[:, hh * LANES:(hh + 1) * LANES].T for hh in range(MLA_HG)]

        def step(i, masked):
            rows = pl.ds(pl.multiple_of(i * MLA_T, MLA_T), MLA_T)
            for hh in range(MLA_HG):
                cols = slice(hh * LANES, (hh + 1) * LANES)
                qv, do = q_ref[rows, cols], do_ref[rows, cols]
                st = _dot_nt(k_ref[:, cols], qv)
                if masked:
                    st = _causal_t(st)
                pt = jnp.exp2(st - lse_ref[0, 0, i, hh:hh + 1, :])
                dpt = _dot_nt(v_ref[:, cols], do)
                dst = (pt * (dpt - delta_ref[0, 0, i, hh:hh + 1, :])).astype(BF16)
                dv_acc[:, cols] += jnp.dot(pt.astype(BF16), do, preferred_element_type=F32)
                dk_acc[:, cols] += jnp.dot(dst, qv, preferred_element_type=F32)
                dq_acc[hh, i] += jnp.dot(kts[hh], dst, preferred_element_type=F32)

        step(j, True)

        def body(i, c):
            step(i, False)
            return c

        lax.fori_loop(j + 1, nk, body, 0)
        dk_ref[...] = dk_acc[...] * (1.0 / LOG2E)
        dv_ref[...] = dv_acc[...]

        @pl.when(j == nk - 1)
        def _():
            for hh in range(MLA_HG):
                for t in range(nk):
                    dq_ref[t * MLA_T:(t + 1) * MLA_T, hh * LANES:(hh + 1) * LANES] = dq_acc[hh, t].T

    whole = lambda b, h, j: (b, h)
    tile = lambda b, h, j: (b * nk + j, h)
    stats = pl.BlockSpec((1, 1, nk, MLA_HG, MLA_T), lambda b, h, j: (b, h, 0, 0, 0))
    in_specs = [pl.BlockSpec((S, MLA_W), whole), pl.BlockSpec((MLA_T, MLA_W), tile),
                pl.BlockSpec((MLA_T, MLA_W), tile), pl.BlockSpec((S, MLA_W), whole), stats, stats]
    out_specs = [pl.BlockSpec((S, MLA_W), whole), pl.BlockSpec((MLA_T, MLA_W), tile),
                 pl.BlockSpec((MLA_T, MLA_W), tile)]
    out_shape = [jax.ShapeDtypeStruct((T, HPAD), F32)] * 3
    scratch = [pltpu.VMEM((MLA_HG, nk, LANES, MLA_T), F32), pltpu.VMEM((MLA_T, MLA_W), F32),
               pltpu.VMEM((MLA_T, MLA_W), F32)]
    return pl.pallas_call(kern, name="mla_bwd", grid=(B, HEADS // MLA_HG, nk), in_specs=in_specs,
                          out_specs=out_specs, out_shape=out_shape, scratch_shapes=scratch,
                          compiler_params=_params(("parallel", "parallel", "arbitrary")))(
        q, k, v, do_raw, lse, delta)


def _rope_tables(pos_col, inv_lane, rows):
    def body(ins, outs, _):
        ang = ins[0][...].astype(F32) * ins[1][...]
        lane = lax.broadcasted_iota(jnp.int32, ang.shape, 1)
        cos, sin = jnp.cos(ang), jnp.sin(ang)
        first = (lane >= HEAD_DIM) & (lane < HEAD_DIM + MLA_ROPE // 2)
        second = (lane >= HEAD_DIM + MLA_ROPE // 2) & (lane < MLA_QK)
        outs[0][...] = jnp.where(lane < HEAD_DIM, 1.0, jnp.where(lane < MLA_QK, cos, 0.0))
        outs[1][...] = jnp.where(first, -sin, 0.0)
        outs[2][...] = jnp.where(second, sin, 0.0)
    return _ew("rope_tables", body, [(pos_col, 1, 0), (inv_lane, None, None)], [(LANES, F32)] * 3, rows)


def _rope(x, c, s1, s2):
    return x * c + pltpu.roll(x, 112, 1) * s1 + pltpu.roll(x, 16, 1) * s2


def _rope_t(d, c, s1, s2):
    return d * c + pltpu.roll(d * s1, 16, 1) + pltpu.roll(d * s2, 112, 1)


def _mla_prep(qdn, w_uq, kvdn, w_ukv, z, tabs, rows):
    def body(ins, outs, _):
        q_pre, kv_pre = ins[0], ins[1]
        c, s1, s2 = ins[3][...], ins[4][...], ins[5][...]
        kr = _rope(ins[2][...].astype(F32), c, s1, s2)
        for h in range(HEADS):
            cols = slice(h * LANES, (h + 1) * LANES)
            outs[0][:, cols] = (_rope(q_pre[:, cols], c, s1, s2) * MLA_QSCALE).astype(BF16)
            outs[1][:, cols] = (kv_pre[:, cols] + kr).astype(BF16)
        outs[2][...] = kv_pre[:, HPAD:].astype(BF16)
    ins = [(z, LANES, Z_BKR // LANES), (tabs[0], LANES, 0), (tabs[1], LANES, 0), (tabs[2], LANES, 0)]
    return _ew("mla_prep", body, ins, [(HPAD, BF16)] * 3, rows, mms=[(qdn, w_uq), (kvdn, w_ukv)])


def _mla_prep_bwd(dq, dk, dv, tabs, dz, rows):
    def body(ins, outs, _):
        c, s1, s2 = ins[3][...], ins[4][...], ins[5][...]
        lane = lax.broadcasted_iota(jnp.int32, c.shape, 1)
        dkr = jnp.zeros(c.shape, F32)
        for h in range(HEADS):
            cols = slice(h * LANES, (h + 1) * LANES)
            outs[0][:, cols] = _rope_t(ins[0][:, cols] * MLA_SCALE, c, s1, s2).astype(BF16)
            dkh = ins[1][:, cols]
            outs[1][:, cols] = jnp.where(lane < HEAD_DIM, dkh, 0.0).astype(BF16)
            dkr = dkr + dkh
        outs[1][:, HPAD:] = ins[2][...].astype(BF16)
        live = (lane >= HEAD_DIM) & (lane < MLA_QK)
        outs[2][...] = jnp.where(live, _rope_t(jnp.where(live, dkr, 0.0), c, s1, s2), 0.0).astype(BF16)
    ins = [(dq, HPAD, 0), (dk, HPAD, 0), (dv, HPAD, 0), (tabs[0], LANES, 0), (tabs[1], LANES, 0),
           (tabs[2], LANES, 0)]
    outs = [(HPAD, BF16), (2 * HPAD, BF16), (LANES, BF16, dz, Z_BKR // LANES)]
    return _ew("mla_prep_bwd", body, ins, outs, rows)


def _gate_bwd(name, d_o_mm, o_raw, gate, gate_cb, dz, dz_cb, rows):
    def body(ins, outs, _):
        lane = lax.broadcasted_iota(jnp.int32, outs[2].shape, 1)
        delta = jnp.zeros(outs[2].shape, F32)
        d_gate = [None] * HEADS
        for h in range(HEADS):
            cols = slice(h * LANES, (h + 1) * LANES)
            dog, o, g = ins[0][:, cols], ins[1][:, cols], _head_gate(ins[2], h)
            sg = jax.nn.sigmoid(g)
            do = dog * (g * sg)
            outs[0][:, cols] = do.astype(BF16)
            d_gate[h] = dog * o * (sg * (1.0 + g * (1.0 - sg)))
            delta = jnp.where(lane == h, jnp.sum(do * o, axis=-1, keepdims=True), delta)
        for pair in range(HEADS // 2):
            packed = d_gate[2 * pair] + pltpu.roll(d_gate[2 * pair + 1], HEAD_DIM, 1)
            outs[1][:, pair * LANES:(pair + 1) * LANES] = packed.astype(BF16)
        outs[2][...] = delta
    ins = [(o_raw, HPAD, 0), (gate, GATE_W, gate_cb)]
    outs = [(HPAD, BF16), (GATE_W, BF16, dz, dz_cb), (LANES, F32)]
    return _ew(name, body, ins, outs, rows, mms=[d_o_mm])


def _merge_out(ua, ub, z, w_out, x0, rows):
    tm = _row_tile(rows)

    def kern(ua_ref, ub_ref, ma_ref, mb_ref, w_ref, x0_ref, y_ref, x1_ref):
        ua_v, ub_v, m_a, m_b = (r[...].astype(F32) for r in (ua_ref, ub_ref, ma_ref, mb_ref))
        y = (jax.nn.sigmoid(m_a) * ua_v + jax.nn.sigmoid(m_b) * ub_v).astype(BF16)
        y_ref[...] = y
        for j in range(D_MODEL // MM_TN):
            cols = slice(j * MM_TN, (j + 1) * MM_TN)
            x1_ref[:, cols] = jnp.dot(y, w_ref[:, cols], preferred_element_type=F32) + x0_ref[:, cols]

    row = lambda cb: pl.BlockSpec((tm, D_MODEL), lambda i: (i, cb))
    return pl.pallas_call(
        kern, name="merge_out", grid=(rows // tm,),
        in_specs=[row(0), row(0), row(Z_MA // D_MODEL), row(Z_MB // D_MODEL),
                  pl.BlockSpec(w_out.shape, lambda i: (0, 0)), row(0)],
        out_specs=[row(0), row(0)],
        out_shape=[jax.ShapeDtypeStruct((rows, D_MODEL), BF16), jax.ShapeDtypeStruct((rows, D_MODEL), F32)],
        compiler_params=_params(("parallel",)))(ua, ub, z, z, w_out, x0)


def _merge_bwd(dy_mm, ua, ub, z, dz, rows):
    def body(ins, outs, _):
        dyv = ins[0][...]
        for idx in range(2):
            s = jax.nn.sigmoid(ins[3 + idx][...].astype(F32))
            outs[idx][...] = (dyv * s).astype(BF16)
            d_m = (dyv * ins[1 + idx][...].astype(F32) * (s * (1.0 - s))).astype(BF16)
            outs[2][:, idx * D_MODEL:(idx + 1) * D_MODEL] = d_m
    ins = [(ua, D_MODEL, 0), (ub, D_MODEL, 0), (z, D_MODEL, Z_MA // D_MODEL), (z, D_MODEL, Z_MB // D_MODEL)]
    outs = [(D_MODEL, BF16), (D_MODEL, BF16), (2 * D_MODEL, BF16, dz, Z_MA // (2 * D_MODEL))]
    return _ew("merge_bwd", body, ins, outs, rows, mms=[dy_mm])


def _kv_grad_cast(dk, dv, dz, rows):
    def body(ins, outs, _):
        outs[0][:, :256] = ins[0][...].astype(BF16)
        outs[0][:, 256:] = ins[1][...].astype(BF16)
    return _ew("kv_grad_cast", body, [(dk, 256, 0), (dv, 256, 0)], [(512, BF16, dz, Z_AK // 512)], rows)[0]


def _ple_fwd(x1, hn, w_pg, p, w_pp, rows):
    def body(ins, outs, _):
        u, e = ins[0][...], ins[1][...]
        outs[0][...] = ins[2][...] + jax.nn.sigmoid(u) * e
        outs[1][...] = u.astype(BF16)
        outs[2][...] = e.astype(BF16)
    return _ew("ple_fwd", body, [(x1, D_MODEL, 0)], [(D_MODEL, F32), (D_MODEL, BF16), (D_MODEL, BF16)], rows,
               mms=[(hn, w_pg), (p, w_pp)])


def _ple_bwd(dx2, u, e, rows):
    def body(ins, outs, _):
        d, s = ins[0][...], jax.nn.sigmoid(ins[1][...].astype(F32))
        outs[0][...] = (d * s).astype(BF16)
        outs[1][...] = (d * ins[2][...].astype(F32) * (s * (1.0 - s))).astype(BF16)
    return _ew("ple_bwd", body, [(dx2, D_MODEL, 0), (u, D_MODEL, 0), (e, D_MODEL, 0)],
               [(D_MODEL, BF16)] * 2, rows)


def _loss_head(x, g, target, rows):
    def body(ins, outs, accs):
        xv, gv = ins[0][...], ins[1][...]
        r = lax.rsqrt(jnp.mean(xv * xv, axis=-1, keepdims=True) + EPS)
        xhat = xv * r
        err = xhat * gv - ins[2][...]
        accs[0][...] += jnp.broadcast_to(0.5 * jnp.sum(jnp.mean(err * err, axis=-1, keepdims=True),
                                                       axis=0, keepdims=True), (1, LANES))
        dyv = err * (1.0 / D_MODEL)
        accs[1][...] += jnp.sum(dyv * xhat, axis=0, keepdims=True)
        dy = dyv * gv
        outs[0][...] = r * (dy - xhat * jnp.mean(dy * xhat, axis=-1, keepdims=True))
    ins = [(x, D_MODEL, 0), (g.reshape(1, D_MODEL), None, None), (target, D_MODEL, 0)]
    return _ew("loss_head", body, ins, [(D_MODEL, F32)], rows, accs=[(1, LANES), (1, D_MODEL)])


def _pad_heads_cols(w, n_heads, dim):
    k = w.shape[0]
    return jnp.pad(w.reshape(k, n_heads, dim), ((0, 0), (0, 0), (0, LANES - dim))).reshape(k, n_heads * LANES)


def _unpad_heads_cols(w, n_heads, dim):
    k = w.shape[0]
    return w.reshape(k, n_heads, LANES)[:, :, :dim].reshape(k, n_heads * dim)


def _layer_weights(w, i):
    segs = jnp.split(w['w_in'][i], list(_cumsum(IN_SIZES))[:-1], axis=1)
    a_q, a_k, a_v, a_gate, b_qd, b_kvd, b_kr, b_gate, m_a, m_b = segs
    kr = jnp.pad(b_kr, ((0, 0), (HEAD_DIM, LANES - MLA_QK)))
    w_in = jnp.concatenate([
        m_a, m_b, _pad_heads_cols(a_q, HEADS, HEAD_DIM), a_gate, b_gate, _pad_heads_cols(a_k, SWA_KV_HEADS, HEAD_DIM),
        _pad_heads_cols(a_v, SWA_KV_HEADS, HEAD_DIM), b_qd, b_kvd, kr], axis=1)
    w_uq = _pad_heads_cols(w['w_uq'][i], HEADS, MLA_QK)
    ukv = w['w_ukv'][i].reshape(MLA_KV_LORA, HEADS, 2 * HEAD_DIM)
    pad = ((0, 0), (0, 0), (0, HEAD_DIM))
    w_ukv = jnp.concatenate([jnp.pad(ukv[:, :, :HEAD_DIM], pad).reshape(MLA_KV_LORA, HPAD),
                             jnp.pad(ukv[:, :, HEAD_DIM:], pad).reshape(MLA_KV_LORA, HPAD)], axis=1)
    w_br_a = _pad_heads_cols(w['w_br_a'][i].T, HEADS, HEAD_DIM).T
    w_br_b = _pad_heads_cols(w['w_br_b'][i].T, HEADS, HEAD_DIM).T
    out = dict(w_in=w_in, w_uq=w_uq, w_ukv=w_ukv, w_br_a=w_br_a, w_br_b=w_br_b, w_out=w['w_out'][i],
               w_pg=w['w_ple_gate'][i], w_pp=w['w_ple_proj'][i])
    for name in ('w_in', 'w_uq', 'w_ukv', 'w_br_a', 'w_br_b', 'w_out', 'w_pg'):
        out[name + '_t'] = out[name].T
    return out


def _cumsum(sizes):
    acc, out = 0, []
    for s in sizes:
        acc += s
        out.append(acc)
    return out


def _unpad_grads(g):
    d = g['w_in']
    seg = lambda off, width: d[:, off:off + width]
    b_kr = seg(Z_BKR, LANES)[:, HEAD_DIM:MLA_QK]
    w_in = jnp.concatenate([
        _unpad_heads_cols(seg(Z_AQ, HPAD), HEADS, HEAD_DIM), _unpad_heads_cols(seg(Z_AK, 256), SWA_KV_HEADS, HEAD_DIM),
        _unpad_heads_cols(seg(Z_AV, 256), SWA_KV_HEADS, HEAD_DIM), seg(Z_AGATE, GATE_W),
        seg(Z_BQD, MLA_Q_LORA), seg(Z_BKVD, MLA_KV_LORA), b_kr, seg(Z_BGATE, GATE_W),
        seg(Z_MA, D_MODEL), seg(Z_MB, D_MODEL)], axis=1)
    w_uq = _unpad_heads_cols(g['w_uq'], HEADS, MLA_QK)
    ukv = g['w_ukv'].reshape(MLA_KV_LORA, 2, HEADS, LANES)[:, :, :, :HEAD_DIM]
    w_ukv = jnp.concatenate([ukv[:, 0], ukv[:, 1]], axis=-1).reshape(MLA_KV_LORA, HEADS * 2 * HEAD_DIM)
    w_br_a = _unpad_heads_cols(g['w_br_a'].T, HEADS, HEAD_DIM).T
    w_br_b = _unpad_heads_cols(g['w_br_b'].T, HEADS, HEAD_DIM).T
    return dict(w_in=w_in, w_uq=w_uq, w_ukv=w_ukv, w_br_a=w_br_a, w_br_b=w_br_b, w_out=g['w_out'],
                w_ple_gate=g['w_pg'], w_ple_proj=g['w_pp'], g_mix=g['g_mix'], sink=g['sink'], g_q=g['g_q'],
                g_kv=g['g_kv'], g_ple=g['g_ple'])


def _layer_fwd(x0, p_i, lw, sm, i, pos_col, pos_row, tabs, B, S):
    T = B * S
    h = _rms_fwd("norm_mix", x0, D_MODEL, 0, sm['g_mix'][i], T)
    z, a_gate = _mm("proj_in", h, lw['w_in'], BF16, f32_cols=(Z_AGATE, GATE_W))
    sink_row = jnp.pad(sm['sink'][i], (0, LANES - HEADS)).reshape(1, LANES)
    oa_raw, oa, lse_a = _swa_fwd(z, a_gate, pos_col, pos_row, sink_row, B, S)
    qdn = _rms_fwd("norm_q", z, MLA_Q_LORA, Z_BQD // MLA_Q_LORA, sm['g_q'][i], T)
    kvdn = _rms_fwd("norm_kv", z, MLA_KV_LORA, Z_BKVD // MLA_KV_LORA, sm['g_kv'][i], T)
    qf, kf, vf = _mla_prep(qdn, lw['w_uq'], kvdn, lw['w_ukv'], z, tabs, T)
    ob_raw, ob, lse_b = _mla_fwd(qf, kf, vf, z, B, S)
    ua = _mm("proj_br_a", oa, lw['w_br_a'], BF16)
    ub = _mm("proj_br_b", ob, lw['w_br_b'], BF16)
    y, x1 = _merge_out(ua, ub, z, lw['w_out'], x0, T)
    hn = _rms_fwd("norm_ple", x1, D_MODEL, 0, sm['g_ple'][i], T)
    x2, u, e = _ple_fwd(x1, hn, lw['w_pg'], p_i, lw['w_pp'], T)
    saved = dict(x0=x0, h=h, z=z, a_gate=a_gate, sink_row=sink_row, oa_raw=oa_raw, oa=oa, lse_a=lse_a, qdn=qdn, kvdn=kvdn,
                 qf=qf, kf=kf, vf=vf, ob_raw=ob_raw, ob=ob, lse_b=lse_b, ua=ua, ub=ub, y=y, x1=x1, hn=hn,
                 u=u, e=e, p=p_i)
    return x2, saved


def _layer_bwd(dx2, sv, lw, sm, i, pos_col, pos_row, tabs, B, S):
    T = B * S
    z = sv['z']
    g = {}
    d_e, d_u = _ple_bwd(dx2, sv['u'], sv['e'], T)
    g['w_pp'] = _mm_tn("grad_pp", sv['p'], d_e)
    g['w_pg'] = _mm_tn("grad_pg", sv['hn'], d_u)
    dx1, g['g_ple'] = _rms_bwd("norm_ple_bwd", sv['x1'], D_MODEL, 0, sm['g_ple'][i], (d_u, lw['w_pg_t']), T, F32,
                               dres=dx2)
    g['w_out'] = _mm_tn("grad_out", sv['y'], dx1)
    dz = lax.empty((T, Z_WIDTH), BF16)
    d_ua, d_ub, dz = _merge_bwd((dx1, lw['w_out_t']), sv['ua'], sv['ub'], z, dz, T)
    g['w_br_a'] = _mm_tn("grad_br_a", sv['oa'], d_ua)
    g['w_br_b'] = _mm_tn("grad_br_b", sv['ob'], d_ub)
    dob_raw, dz, delta_b = _gate_bwd("gate_b_bwd", (d_ub, lw['w_br_b_t']), sv['ob_raw'], z, Z_BGATE // GATE_W,
                                     dz, Z_BGATE // GATE_W, T)
    delta_rows = delta_b[:, :HEADS].reshape(B, S // MLA_T, MLA_T, HEADS // MLA_HG, MLA_HG).transpose(0, 3, 1, 4, 2)
    dq, dk, dv = _mla_bwd(sv['qf'], sv['kf'], sv['vf'], dob_raw, sv['lse_b'], delta_rows, B, S)
    dq_pre, dkv_pre, dz = _mla_prep_bwd(dq, dk, dv, tabs, dz, T)
    g['w_uq'] = _mm_tn("grad_uq", sv['qdn'], dq_pre)
    g['w_ukv'] = _mm_tn("grad_ukv", sv['kvdn'], dkv_pre)
    dz, g['g_q'] = _rms_bwd("norm_q_bwd", z, MLA_Q_LORA, Z_BQD // MLA_Q_LORA, sm['g_q'][i],
                            (dq_pre, lw['w_uq_t']), T, BF16, into=(dz, Z_BQD // MLA_Q_LORA))
    dz, g['g_kv'] = _rms_bwd("norm_kv_bwd", z, MLA_KV_LORA, Z_BKVD // MLA_KV_LORA, sm['g_kv'][i],
                             (dkv_pre, lw['w_ukv_t']), T, BF16, into=(dz, Z_BKVD // MLA_KV_LORA))
    doa_raw, dz, delta_a = _gate_bwd("gate_a_bwd", (d_ua, lw['w_br_a_t']), sv['oa_raw'], sv['a_gate'], 0,
                                     dz, Z_AGATE // GATE_W, T)
    dz, d_ak, d_av, dsink = _swa_bwd(z, pos_col, pos_row, sv['sink_row'], sv['lse_a'], doa_raw, delta_a, dz, B, S)
    dz = _kv_grad_cast(d_ak, d_av, dz, T)
    g['sink'] = dsink[0, :HEADS]
    g['w_in'] = _mm_tn("grad_in", sv['h'], dz, tn=Z_WIDTH // 2)
    dx0, g['g_mix'] = _rms_bwd("norm_mix_bwd", sv['x0'], D_MODEL, 0, sm['g_mix'][i], (dz, lw['w_in_t']), T, F32,
                               dres=dx1)
    for name in ('g_ple', 'g_q', 'g_kv', 'g_mix'):
        g[name] = g[name][0]
    return dx0, g


def _local_step(x, p, positions, wfull, sm, loss_target):
    B, S, _ = x.shape
    T = B * S
    pos_col = positions.reshape(T, 1)
    pos_row = positions.reshape(T // BLOCK, 1, BLOCK)
    half = MLA_ROPE // 2
    inv = ROPE_THETA ** (-jnp.arange(0, MLA_ROPE, 2, dtype=F32) / MLA_ROPE)
    inv_lane = jnp.tile(inv, LANES // half).reshape(1, LANES)
    tabs = _rope_tables(pos_col, inv_lane, T)
    xc = x.reshape(T, D_MODEL)
    lws, saved = [], []
    for i in range(DEPTH):
        lw = _layer_weights(wfull, i)
        xc, sv = _layer_fwd(xc, p[i].reshape(T, PLE_DIM), lw, sm, i, pos_col, pos_row, tabs, B, S)
        lws.append(lw)
        saved.append(sv)
    dx, loss, dg_final = _loss_head(xc, sm['g_final'], loss_target.reshape(T, D_MODEL), T)
    layer_grads = [None] * DEPTH
    for i in reversed(range(DEPTH)):
        dx, g = _layer_bwd(dx, saved[i], lws[i], sm, i, pos_col, pos_row, tabs, B, S)
        layer_grads[i] = _unpad_grads(g)
    return loss, dx.reshape(B, S, D_MODEL), layer_grads, dg_final[0]


SMALL_ROWS = 48


def _pack_small(arrs):
    flat = jnp.concatenate([arrs[name].reshape(-1) for name in SMALL])
    return jnp.pad(flat, (0, SMALL_ROWS * LANES - flat.shape[0])).reshape(SMALL_ROWS, LANES)


def _unpack_small(block, shapes):
    flat = block.reshape(-1)
    out, off = {}, 0
    for name in SMALL:
        n = math.prod(shapes[name])
        out[name] = flat[off:off + n].reshape(shapes[name])
        off += n
    return out


def _flipped(shard_shape):
    return shard_shape[-1] % LANES != 0


def _to_slots(g, axis):
    r, c = g.shape
    if axis == 0:
        return g.reshape(N_CHIPS, r // N_CHIPS, c)
    return g.reshape(r, N_CHIPS, c // N_CHIPS).transpose(1, 0, 2)


def _div_tile(rows, cap):
    return next(t for t in range(min(cap, rows) // 8 * 8, 0, -8) if rows % t == 0)


def _units(shapes):
    units = []
    for w, shape in enumerate(shapes):
        r = shape[-2]
        n = next(n for n in (8, 7, 4, 2, 1) if r % (8 * n) == 0) if r >= 1024 else 1
        units += [(w, k * (r // n), r // n) for k in range(n)]
    return units


def _place():
    x, y, c = lax.axis_index("x"), lax.axis_index("y"), lax.axis_index("c")
    chips = [(1 - x, y), (x, 1 - y), (1 - x, 1 - y)]
    return x, y, c, chips


ANY = pl.BlockSpec(memory_space=pl.ANY)


def _remote(send_sems, recv_sems, k, src, dst, to):
    return pltpu.make_async_remote_copy(src_ref=src, dst_ref=dst, send_sem=send_sems.at[k],
                                        recv_sem=recv_sems.at[k], device_id=to, device_id_type=MESH)


def _gather_weights(shards):
    n = len(shards)
    units = _units([s.shape for s in shards])
    nu = len(units)

    def body(*refs):
        ins, outs = refs[:n], refs[n:2 * n]
        send_sems, recv_sems, local_sems = refs[2 * n:]
        x, y, c, chips = _place()
        me = 2 * x + y
        sibling = (x, y, 1 - c)
        copy = functools.partial(_remote, send_sems, recv_sems)
        keeps, sends = [], []
        for u, (w, r0, nr) in enumerate(units):
            rows = pl.ds(r0, nr)
            keeps.append(pltpu.make_async_copy(ins[w].at[:, rows, :], outs[w].at[me, :, rows, :], local_sems.at[u]))
            keeps[-1].start()
        for j, (cx, cy) in enumerate(chips):
            for u, (w, r0, nr) in enumerate(units):
                rows = pl.ds(r0, nr)
                sends.append(copy(j * nu + u, ins[w].at[c, rows, :], outs[w].at[me, c, rows, :], (cx, cy, c)))
                sends[-1].start()
        for j, (cx, cy) in enumerate(chips):
            for u, (w, r0, nr) in enumerate(units):
                landed = outs[w].at[2 * cx + cy, c, pl.ds(r0, nr), :]
                copy(j * nu + u, landed, landed, (cx, cy, c)).wait_recv()
                sends.append(copy((3 + j) * nu + u, landed, landed, sibling))
                sends[-1].start()
        for j, (cx, cy) in enumerate(chips):
            for u, (w, r0, nr) in enumerate(units):
                other = outs[w].at[2 * cx + cy, 1 - c, pl.ds(r0, nr), :]
                copy((3 + j) * nu + u, other, other, sibling).wait_recv()
        for cp in sends:
            cp.wait_send()
        for keep in keeps:
            keep.wait()

    return pl.pallas_call(
        body, name="gather_weights",
        out_shape=[jax.ShapeDtypeStruct((N_CHIPS,) + s.shape, s.dtype) for s in shards],
        in_specs=[ANY] * n, out_specs=[ANY] * n,
        scratch_shapes=[pltpu.SemaphoreType.DMA((6 * nu,)), pltpu.SemaphoreType.DMA((6 * nu,)),
                        pltpu.SemaphoreType.DMA((nu,))])(*shards)


def _pair_exchange(g0, g1):
    n = len(g0)

    def body(*refs):
        layers, outs = (refs[:n], refs[n:2 * n]), refs[2 * n:3 * n]
        send_sems, recv_sems = refs[3 * n:]
        x, y, c, _ = _place()
        copy = functools.partial(_remote, send_sems, recv_sems)
        for w in range(n):
            for q in range(N_CHIPS):
                for layer in range(DEPTH):
                    cp = copy(N_CHIPS * w + q, layers[layer][w].at[q], outs[w].at[q], (x, y, 1 - c))
                    pl.when(c == 1 - layer)(cp.start)
        for w in range(n):
            for q in range(N_CHIPS):
                copy(N_CHIPS * w + q, layers[0][w].at[q], outs[w].at[q], (x, y, 1 - c)).wait()

    return pl.pallas_call(
        body, name="pair_exchange", out_shape=[jax.ShapeDtypeStruct(g.shape, g.dtype) for g in g0],
        in_specs=[ANY] * (2 * n), out_specs=[ANY] * n,
        scratch_shapes=[pltpu.SemaphoreType.DMA((N_CHIPS * n,)), pltpu.SemaphoreType.DMA((N_CHIPS * n,))])(*g0, *g1)


def _pair_sum(name, g0, g1, theirs, cflag):
    shape = theirs.shape
    rows, width = shape[0] * shape[1], shape[2]

    def body(ins, outs, _):
        mine = jnp.where(ins[3][0:1, 0:1] == 0.0, ins[0][...], ins[1][...])
        tot = mine + ins[2][...]
        outs[0][...] = tot
        outs[1][...] = tot.astype(BF16)
    ins = [(a.reshape(rows, width), width, 0) for a in (g0, g1, theirs)] + [(cflag, None, None)]
    f32, bf16 = _ew(name, body, ins, [(width, F32), (width, BF16)], rows, tm=_div_tile(rows, ROW_TILE))
    return f32.reshape(shape), bf16.reshape(shape)


def _chip_exchange(parts):
    n = len(parts)

    def body(*refs):
        ins, outs = refs[:n], refs[n:2 * n]
        send_sems, recv_sems = refs[2 * n:]
        x, y, c, chips = _place()
        copy = functools.partial(_remote, send_sems, recv_sems)
        sends = []
        for j, (cx, cy) in enumerate(chips):
            for w in range(n):
                sends.append(copy(j * n + w, ins[w].at[2 * cx + cy], outs[w].at[j], (cx, cy, c)))
                sends[-1].start()
        for j, (cx, cy) in enumerate(chips):
            for w in range(n):
                copy(j * n + w, outs[w].at[j], outs[w].at[j], (cx, cy, c)).wait_recv()
        for cp in sends:
            cp.wait_send()

    return pl.pallas_call(
        body, name="chip_exchange",
        out_shape=[jax.ShapeDtypeStruct((3,) + a.shape[1:], a.dtype) for a in parts],
        in_specs=[ANY] * n, out_specs=[ANY] * n,
        scratch_shapes=[pltpu.SemaphoreType.DMA((3 * n,)), pltpu.SemaphoreType.DMA((3 * n,))])(*parts)


def _chip_sum(name, part, landed, chipflag):
    _, r, width = part.shape
    tm = _div_tile(r, ROW_TILE // 2)

    def kern(p_ref, l_ref, flag_ref, o_ref):
        me = flag_ref[0:1, 0:1]
        own = jnp.where(me == 0.0, p_ref[0], jnp.where(me == 1.0, p_ref[1], jnp.where(me == 2.0, p_ref[2], p_ref[3])))
        o_ref[...] = ((own + l_ref[0].astype(F32)) + l_ref[1].astype(F32)) + l_ref[2].astype(F32)

    return pl.pallas_call(
        kern, name=name, grid=(r // tm,),
        in_specs=[pl.BlockSpec((N_CHIPS, tm, width), lambda i: (0, i, 0)),
                  pl.BlockSpec((3, tm, width), lambda i: (0, i, 0)),
                  pl.BlockSpec((1, LANES), lambda i: (0, 0))],
        out_specs=pl.BlockSpec((tm, width), lambda i: (i, 0)),
        out_shape=jax.ShapeDtypeStruct((r, width), F32), compiler_params=_params(("arbitrary",)))(part, landed, chipflag)


def _pair_broadcast(mine):
    n = len(mine)
    units = _units([a.shape for a in mine])

    def body(*refs):
        ins, outs = refs[:n], refs[n:2 * n]
        send_sems, recv_sems = refs[2 * n:]
        x, y, c, _ = _place()
        copy = functools.partial(_remote, send_sems, recv_sems)
        cps = [copy(u, ins[w].at[pl.ds(r0, nr), :], outs[w].at[pl.ds(r0, nr), :], (x, y, 1 - c))
               for u, (w, r0, nr) in enumerate(units)]
        for cp in cps:
            cp.start()
        for cp in cps:
            cp.wait()

    return pl.pallas_call(
        body, name="pair_broadcast", out_shape=[jax.ShapeDtypeStruct(a.shape, a.dtype) for a in mine],
        in_specs=[ANY] * n, out_specs=[ANY] * n,
        scratch_shapes=[pltpu.SemaphoreType.DMA((len(units),)), pltpu.SemaphoreType.DMA((len(units),))])(*mine)


def _small_allreduce(v):
    offsets = [(dx, dy, dc) for dx in (0, 1) for dy in (0, 1) for dc in (0, 1)][1:]

    def body(v_ref, out_ref, recv_ref, send_sems, recv_sems):
        x, y, c, _ = _place()
        flip = lambda a, d: 1 - a if d else a
        peers = [(flip(x, dx), flip(y, dy), flip(c, dc)) for dx, dy, dc in offsets]
        copy = functools.partial(_remote, send_sems, recv_sems)
        me = 4 * x + 2 * y + c
        recv_ref[me] = v_ref[...]
        cps = [copy(k, v_ref, recv_ref.at[me], peer) for k, peer in enumerate(peers)]
        for cp in cps:
            cp.start()
        for k, (px, py, pc) in enumerate(peers):
            landed = recv_ref.at[4 * px + 2 * py + pc]
            copy(k, landed, landed, (px, py, pc)).wait_recv()
        for cp in cps:
            cp.wait_send()
        tot = recv_ref[0]
        for d in range(1, 8):
            tot = tot + recv_ref[d]
        out_ref[...] = tot

    vmem = pl.BlockSpec(memory_space=pltpu.VMEM)
    return pl.pallas_call(
        body, name="small_allreduce", out_shape=jax.ShapeDtypeStruct(v.shape, v.dtype),
        in_specs=[vmem], out_specs=vmem,
        scratch_shapes=[pltpu.VMEM((8,) + v.shape, v.dtype), pltpu.SemaphoreType.DMA((7,)),
                        pltpu.SemaphoreType.DMA((7,))])(v)


def _adam_math(gv, wv, mv, vv):
    mv = ADAM_B1 * mv + (1.0 - ADAM_B1) * gv
    vv = ADAM_B2 * vv + (1.0 - ADAM_B2) * (gv * gv)
    m_hat = mv / (1.0 - ADAM_B1 ** ADAM_STEP)
    v_hat = vv / (1.0 - ADAM_B2 ** ADAM_STEP)
    return -ADAM_LR * (m_hat / (jnp.sqrt(v_hat) + ADAM_EPS) + ADAM_WD * wv), mv, vv


def _adamw_big(name, mine, theirs, cflag, w, m, v):
    _, r, width = w.shape
    tm = _div_tile(r, ROW_TILE // 2)

    def kern(mine_ref, theirs_ref, flag_ref, w_ref, m_ref, v_ref, g_ref, d_ref, nm_ref, nv_ref):
        layer = pl.program_id(0).astype(F32)
        gv = jnp.where(flag_ref[0:1, 0:1] == layer, mine_ref[...], theirs_ref[...])
        g_ref[0] = gv
        d_ref[0], nm_ref[0], nv_ref[0] = _adam_math(gv, w_ref[0], m_ref[0], v_ref[0])

    flat = pl.BlockSpec((tm, width), lambda l, i: (i, 0))
    stacked = pl.BlockSpec((1, tm, width), lambda l, i: (l, i, 0))
    return pl.pallas_call(
        kern, name=name, grid=(DEPTH, r // tm),
        in_specs=[flat, flat, pl.BlockSpec((1, LANES), lambda l, i: (0, 0)), stacked, stacked, stacked],
        out_specs=[stacked] * 4, out_shape=[jax.ShapeDtypeStruct(w.shape, F32)] * 4,
        compiler_params=_params(("arbitrary", "arbitrary")))(mine, theirs, cflag, w, m, v)


def _adamw_small(g, w, m, v):
    def body(ins, outs, _):
        outs[0][...], outs[1][...], outs[2][...] = _adam_math(*(r[...] for r in ins))
    return _ew("adamw_small", body, [(a, LANES, 0) for a in (g, w, m, v)], [(LANES, F32)] * 3, SMALL_ROWS)


def kernel(x, p, positions, g_mix, w_in, sink, g_q, w_uq, g_kv, w_ukv, w_br_a, w_br_b, w_out, g_ple, w_ple_gate, w_ple_proj, g_final, loss_target, m_g_mix, m_w_in, m_sink, m_g_q, m_w_uq, m_g_kv, m_w_ukv, m_w_br_a, m_w_br_b, m_w_out, m_g_ple, m_w_ple_gate, m_w_ple_proj, m_g_final, v_g_mix, v_w_in, v_sink, v_g_q, v_w_uq, v_g_kv, v_w_ukv, v_w_br_a, v_w_br_b, v_w_out, v_g_ple, v_w_ple_gate, v_w_ple_proj, v_g_final):
    w = dict(g_mix=g_mix, w_in=w_in, sink=sink, g_q=g_q, w_uq=w_uq, g_kv=g_kv, w_ukv=w_ukv, w_br_a=w_br_a,
             w_br_b=w_br_b, w_out=w_out, g_ple=g_ple, w_ple_gate=w_ple_gate, w_ple_proj=w_ple_proj, g_final=g_final)
    m = dict(g_mix=m_g_mix, w_in=m_w_in, sink=m_sink, g_q=m_g_q, w_uq=m_w_uq, g_kv=m_g_kv, w_ukv=m_w_ukv,
             w_br_a=m_w_br_a, w_br_b=m_w_br_b, w_out=m_w_out, g_ple=m_g_ple, w_ple_gate=m_w_ple_gate,
             w_ple_proj=m_w_ple_proj, g_final=m_g_final)
    v = dict(g_mix=v_g_mix, w_in=v_w_in, sink=v_sink, g_q=v_g_q, w_uq=v_w_uq, g_kv=v_g_kv, w_ukv=v_w_ukv,
             w_br_a=v_w_br_a, w_br_b=v_w_br_b, w_out=v_w_out, g_ple=v_g_ple, w_ple_gate=v_w_ple_gate,
             w_ple_proj=v_w_ple_proj, g_final=v_g_final)
    wfull = _gather_full(w)
    sm = {name: w[name] for name in SMALL}
    loss_row, grad_x, layer_grads, dg_final = _local_step(x, p, positions, wfull, sm, loss_target)
    loss = lax.psum(loss_row[0, 0], ("x", "y", "c"))
    res = _update(layer_grads, dg_final, w, m, v)
    return (loss, grad_x, *[res[name][kind] for kind in range(4) for name in WEIGHT_NAMES])


def _gather_full(w):
    gathered = _gather_weights([w[name].astype(BF16) for name, _ in SHARDED])
    return {name: [jnp.concatenate([gathered[k][q, layer] for q in range(N_CHIPS)], axis=axis - 1)
                   for layer in range(DEPTH)] for k, (name, axis) in enumerate(SHARDED)}


def _update(layer_grads, dg_final, w, m, v):
    small_shapes = {name: w[name].shape for name in SMALL}
    cflag = jnp.full((1, LANES), lax.axis_index("c"), F32)
    chipflag = jnp.full((1, LANES), 2 * lax.axis_index("x") + lax.axis_index("y"), F32)

    slots = [[_to_slots(layer_grads[layer][name], axis - 1) for name, axis in SHARDED] for layer in range(DEPTH)]
    theirs = _pair_exchange(slots[0], slots[1])
    pair = [_pair_sum("pair_sum_" + name, slots[0][k], slots[1][k], theirs[k], cflag)
            for k, (name, _) in enumerate(SHARDED)]
    landed = _chip_exchange([bf16 for _, bf16 in pair])
    mine = [_chip_sum("chip_sum_" + name, pair[k][0], landed[k], chipflag) for k, (name, _) in enumerate(SHARDED)]
    other = _pair_broadcast(mine)
    res = {}
    for k, (name, _) in enumerate(SHARDED):
        flip = _flipped(w[name].shape)
        view = (lambda a: jnp.swapaxes(a, -1, -2)) if flip else (lambda a: a)
        outs = _adamw_big("adamw_" + name, view(mine[k]), view(other[k]), cflag, view(w[name]), view(m[name]),
                          view(v[name]))
        res[name] = tuple(view(a) for a in outs)

    gsmall = {name: jnp.stack([layer_grads[layer][name] for layer in range(DEPTH)]) for name in SMALL[:-1]}
    gsmall['g_final'] = dg_final
    gsum = _small_allreduce(_pack_small(gsmall))
    small = (gsum,) + tuple(_adamw_small(gsum, _pack_small(w), _pack_small(m), _pack_small(v)))
    for name, arrs in zip(SMALL, zip(*[[_unpack_small(a, small_shapes)[n] for n in SMALL] for a in small])):
        res[name] = arrs
    return res
```

```python
import functools
import math

import jax
import jax.numpy as jnp
from jax import lax
from jax.experimental import pallas as pl
from jax.experimental.pallas import tpu as pltpu

F32 = jnp.float32
BF16 = jnp.bfloat16

D_MODEL = 1024
DEPTH = 2
PLE_DIM = 256
BLOCK = 128
EPS = 1e-6
NEG = -1e30
HEADS = 8
SWA_KV_HEADS = 2
HEAD_DIM = 64
LANES = 128
HPAD = HEADS * LANES
MLA_QK = 96
MLA_ROPE = 32
MLA_Q_LORA = 256
MLA_KV_LORA = 128
ROPE_THETA = 10000.0
IN_SIZES = (512, 128, 128, 512, 256, 128, 32, 512, 1024, 1024)

Z_MA, Z_MB, Z_AQ, Z_AGATE, Z_BGATE = 0, 1024, 2048, 3072, 3584
Z_AK, Z_AV, Z_BQD, Z_BKVD, Z_BKR = 4096, 4352, 4608, 4864, 4992
Z_WIDTH = 5120
GATE_W = HEADS * HEAD_DIM

ADAM_LR, ADAM_B1, ADAM_B2, ADAM_EPS, ADAM_WD, ADAM_STEP = 0.001, 0.9, 0.999, 1e-08, 0.01, 10

VMEM_LIMIT = 56 * 1024 * 1024
MESH = pl.DeviceIdType.MESH

WEIGHT_NAMES = ('g_mix', 'w_in', 'sink', 'g_q', 'w_uq', 'g_kv', 'w_ukv', 'w_br_a', 'w_br_b',
                'w_out', 'g_ple', 'w_ple_gate', 'w_ple_proj', 'g_final')
SHARDED = (('w_in', 2), ('w_uq', 2), ('w_ukv', 2), ('w_br_a', 2), ('w_br_b', 2),
           ('w_out', 1), ('w_ple_gate', 1), ('w_ple_proj', 2))
SMALL = ('g_mix', 'sink', 'g_q', 'g_kv', 'g_ple', 'g_final')
N_CHIPS = 4


def _params(sem):
    return pltpu.CompilerParams(dimension_semantics=sem, vmem_limit_bytes=VMEM_LIMIT)


MM_TN = 512
ROW_TILE = 512
BIG_WEIGHT_BYTES = 8 * 1024 * 1024


def _row_tile(rows, weight_bytes=0):
    tm = ROW_TILE // 2 if weight_bytes > BIG_WEIGHT_BYTES else ROW_TILE
    return min(tm, rows)


def _ew(name, body, ins, outs, rows, accs=(), mms=(), tm=None):
    n_mm, n_in, n_out = len(mms), len(ins), len(outs)
    if tm is None:
        tm = _row_tile(rows, sum(b.size * b.dtype.itemsize for _, b in mms))
    in_specs, args = [], []
    for a, b in mms:
        in_specs += [pl.BlockSpec((tm, a.shape[1]), lambda i: (i, 0)), pl.BlockSpec(b.shape, lambda i: (0, 0))]
        args += [a, b]
    for arr, width, cb in ins:
        if width is None:
            in_specs.append(pl.BlockSpec(arr.shape, lambda i, nd=arr.ndim: (0,) * nd))
        else:
            in_specs.append(pl.BlockSpec((tm, width), lambda i, cb=cb: (i, cb)))
        args.append(arr)
    out_shape, out_specs, aliases = [], [], {}
    for k, out in enumerate(outs):
        if len(out) == 4:
            aliases[len(args)] = k
            in_specs.append(pl.BlockSpec(memory_space=pl.ANY))
            args.append(out[2])
            out_shape.append(jax.ShapeDtypeStruct(out[2].shape, out[2].dtype))
            out_specs.append(pl.BlockSpec((tm, out[0]), lambda i, cb=out[3]: (i, cb)))
        else:
            out_shape.append(jax.ShapeDtypeStruct((rows, out[0]), out[1]))
            out_specs.append(pl.BlockSpec((tm, out[0]), lambda i: (i, 0)))
    n_in += len(aliases)
    out_shape += [jax.ShapeDtypeStruct(s, F32) for s in accs]
    out_specs += [pl.BlockSpec(s, lambda i: (0, 0)) for s in accs]

    def kern(*refs):
        mm_refs, refs = refs[:2 * n_mm], refs[2 * n_mm:]
        in_refs, out_refs = refs[:n_in - len(aliases)], refs[n_in:n_in + n_out]
        acc_refs, prod_refs = refs[n_in + n_out:n_in + n_out + len(accs)], refs[n_in + n_out + len(accs):]
        if acc_refs:
            @pl.when(pl.program_id(0) == 0)
            def _():
                for r in acc_refs:
                    r[...] = jnp.zeros_like(r)
        for k in range(n_mm):
            a_ref, b_ref, prod = mm_refs[2 * k], mm_refs[2 * k + 1], prod_refs[k]
            av = a_ref[...].astype(BF16)
            n = b_ref.shape[1]
            tn = min(MM_TN, n)
            for j in range(n // tn):
                cols = slice(j * tn, (j + 1) * tn)
                prod[:, cols] = jnp.dot(av, b_ref[:, cols], preferred_element_type=F32)
        body(tuple(prod_refs) + tuple(in_refs), out_refs, acc_refs)

    scratch = [pltpu.VMEM((tm, b.shape[1]), F32) for _, b in mms]
    res = pl.pallas_call(kern, name=name, grid=(rows // tm,), in_specs=in_specs, out_specs=out_specs,
                         out_shape=out_shape, scratch_shapes=scratch, input_output_aliases=aliases,
                         compiler_params=_params(("arbitrary",)))(*args)
    return res


def _rms_fwd(name, x, width, cb, g, rows):
    def body(ins, outs, _):
        xv = ins[0][...].astype(F32)
        r = lax.rsqrt(jnp.mean(xv * xv, axis=-1, keepdims=True) + EPS)
        outs[0][...] = ((xv * r) * ins[1][...]).astype(BF16)
    return _ew(name, body, [(x, width, cb), (g.reshape(1, width), None, None)], [(width, BF16)], rows)[0]


def _rms_bwd(name, x, width, cb, g, dh_mm, rows, out_dtype, dres=None, into=()):
    def body(ins, outs, accs):
        dhv, xv, gv = ins[0][...], ins[1][...].astype(F32), ins[2][...]
        r = lax.rsqrt(jnp.mean(xv * xv, axis=-1, keepdims=True) + EPS)
        xhat = xv * r
        accs[0][...] += jnp.sum(dhv * xhat, axis=0, keepdims=True)
        dy = dhv * gv
        dx = r * (dy - xhat * jnp.mean(dy * xhat, axis=-1, keepdims=True))
        if dres is not None:
            dx = dx + ins[3][...]
        outs[0][...] = dx.astype(out_dtype)
    ins = [(x, width, cb), (g.reshape(1, width), None, None)]
    if dres is not None:
        ins.append((dres, width, 0))
    return _ew(name, body, ins, [(width, out_dtype) + tuple(into)], rows, accs=[(1, width)], mms=[dh_mm])


def _mm(name, a, b, out_dtype, residual=None, f32_cols=None, tn=MM_TN):
    M, K = a.shape
    N = b.shape[1]
    tm, tn = _row_tile(M, b.size * b.dtype.itemsize), min(tn, N)
    has_res = residual is not None
    c0, cw = f32_cols if f32_cols else (0, 0)

    def kern(*refs):
        a_ref, b_ref = refs[0], refs[1]
        o_ref = refs[3] if has_res else refs[2]
        av = a_ref[...].astype(BF16)
        for j in range(N // tn):
            cols = slice(j * tn, (j + 1) * tn)
            part = jnp.dot(av, b_ref[:, cols], preferred_element_type=F32)
            if has_res:
                part = part + refs[2][:, cols]
            o_ref[:, cols] = part.astype(o_ref.dtype)
            if c0 <= j * tn and (j + 1) * tn <= c0 + cw:
                refs[-1][:, j * tn - c0:(j + 1) * tn - c0] = part

    in_specs = [pl.BlockSpec((tm, K), lambda i: (i, 0)), pl.BlockSpec((K, N), lambda i: (0, 0))]
    args = [a, b]
    if has_res:
        in_specs.append(pl.BlockSpec((tm, N), lambda i: (i, 0)))
        args.append(residual)
    out_specs = [pl.BlockSpec((tm, N), lambda i: (i, 0))]
    out_shape = [jax.ShapeDtypeStruct((M, N), out_dtype)]
    if f32_cols:
        assert c0 % tn == 0 and cw % tn == 0
        out_specs.append(pl.BlockSpec((tm, cw), lambda i: (i, 0)))
        out_shape.append(jax.ShapeDtypeStruct((M, cw), F32))
    res = pl.pallas_call(kern, name=name, grid=(M // tm,), in_specs=in_specs, out_specs=out_specs,
                         out_shape=out_shape, compiler_params=_params(("parallel",)))(*args)
    return res if f32_cols else res[0]


def _mm_tn(name, a, b, tk=512, tn=2048):
    T, M = a.shape
    N = b.shape[1]
    tn, tk = min(tn, N), min(tk, T)

    def kern(a_ref, b_ref, o_ref):
        k = pl.program_id(1)
        part = _dot_tn(a_ref[...].astype(BF16), b_ref[...].astype(BF16))

        @pl.when(k == 0)
        def _():
            o_ref[...] = part

        @pl.when(k > 0)
        def _():
            o_ref[...] += part

    return pl.pallas_call(
        kern, name=name, grid=(N // tn, T // tk),
        in_specs=[pl.BlockSpec((tk, M), lambda j, k: (k, 0)), pl.BlockSpec((tk, tn), lambda j, k: (k, j))],
        out_specs=pl.BlockSpec((M, tn), lambda j, k: (0, j)),
        out_shape=jax.ShapeDtypeStruct((M, N), F32),
        compiler_params=_params(("parallel", "arbitrary")))(a, b)


def _dot_nt(a, b):
    return lax.dot_general(a, b, (((1,), (1,)), ((), ())), preferred_element_type=F32)


def _dot_tn(a, b):
    return lax.dot_general(a, b, (((0,), (0,)), ((), ())), preferred_element_type=F32)


SWA_SCALE = HEAD_DIM ** -0.5


def _swa_band(n, pq_ref, pkp_ref, pkc_ref):
    posk = jnp.concatenate([pkp_ref[...], pkc_ref[...]], axis=0)
    dist = (pq_ref[0] - posk).astype(F32)
    kj = lax.broadcasted_iota(jnp.int32, (2 * BLOCK, BLOCK), 0)
    qi = lax.broadcasted_iota(jnp.int32, (2 * BLOCK, BLOCK), 1)
    t_abs = n * BLOCK + qi
    s_abs = n * BLOCK - BLOCK + kj
    return dist, (s_abs >= 0) & (s_abs <= t_abs) & (t_abs - s_abs < BLOCK)


SWA_GROUP = HEADS // SWA_KV_HEADS


def _head_gate(gate_ref, h):
    pair = gate_ref[:, (h // 2) * LANES:(h // 2 + 1) * LANES].astype(F32)
    return pair if h % 2 == 0 else pltpu.roll(pair, HEAD_DIM, 1)


def _swa_group_q(q_all, g):
    heads = range(g * SWA_GROUP, (g + 1) * SWA_GROUP)
    return jnp.concatenate([(q_all[:, h * LANES:(h + 1) * LANES] * SWA_SCALE).astype(BF16) for h in heads], axis=0)


def _swa_mask(s, dist, valid, h):
    return jnp.where(valid, s - (2.0 ** -(h + 1)) * dist, NEG)


def _rows_to_lanes(rows):
    block = jnp.concatenate(list(rows) + [jnp.zeros((LANES - len(rows), BLOCK), F32)], axis=0)
    return block.T


def _swa_specs(nb):
    prev = lambda b, n: b * nb + jnp.maximum(n - 1, 0)
    own = lambda b, n: b * nb + n
    return [
        pl.BlockSpec((BLOCK, HPAD), lambda b, n: (own(b, n), Z_AQ // HPAD)),
        pl.BlockSpec((BLOCK, 256), lambda b, n: (prev(b, n), Z_AK // 256)),
        pl.BlockSpec((BLOCK, 256), lambda b, n: (own(b, n), Z_AK // 256)),
        pl.BlockSpec((BLOCK, 256), lambda b, n: (prev(b, n), Z_AV // 256)),
        pl.BlockSpec((BLOCK, 256), lambda b, n: (own(b, n), Z_AV // 256)),
        pl.BlockSpec((1, 1, BLOCK), lambda b, n: (own(b, n), 0, 0)),
        pl.BlockSpec((BLOCK, 1), lambda b, n: (prev(b, n), 0)),
        pl.BlockSpec((BLOCK, 1), lambda b, n: (own(b, n), 0)),
    ]


def _swa_fwd(z, gate, pos_col, pos_row, sink_row, B, S):
    nb = S // BLOCK
    T = B * S

    def kern(q_ref, kp_ref, kc_ref, vp_ref, vc_ref, pq_ref, pkp_ref, pkc_ref, gate_ref, sink_ref,
             oraw_ref, og_ref, lse_ref):
        q_all = q_ref[...]
        kb = jnp.concatenate([kp_ref[...], kc_ref[...]], axis=0).astype(BF16)
        vb = jnp.concatenate([vp_ref[...], vc_ref[...]], axis=0).astype(BF16)
        dist, valid = _swa_band(pl.program_id(1), pq_ref, pkp_ref, pkc_ref)
        lse_rows = []
        for grp in range(SWA_KV_HEADS):
            gcols = slice(grp * LANES, (grp + 1) * LANES)
            s_all = _dot_nt(kb[:, gcols], _swa_group_q(q_all, grp))
            probs = []
            for hh in range(SWA_GROUP):
                h = grp * SWA_GROUP + hh
                s = _swa_mask(s_all[:, hh * BLOCK:(hh + 1) * BLOCK], dist, valid, h)
                sink_h = sink_ref[0:1, h:h + 1]
                m = jnp.maximum(jnp.max(s, axis=0, keepdims=True), sink_h)
                e = jnp.exp(s - m)
                denom = jnp.sum(e, axis=0, keepdims=True) + jnp.exp(sink_h - m)
                probs.append((e * (1.0 / denom)).astype(BF16))
                lse_rows.append(m + jnp.log(denom))
            o_all = jnp.dot(vb[:, gcols].T, jnp.concatenate(probs, axis=1), preferred_element_type=F32)
            for hh in range(SWA_GROUP):
                h = grp * SWA_GROUP + hh
                cols = slice(h * LANES, (h + 1) * LANES)
                o = o_all[:, hh * BLOCK:(hh + 1) * BLOCK].T
                oraw_ref[:, cols] = o
                g = _head_gate(gate_ref, h)
                og_ref[:, cols] = (o * (g * jax.nn.sigmoid(g))).astype(BF16)
        lse_ref[...] = _rows_to_lanes(lse_rows)

    own = lambda b, n: b * nb + n
    in_specs = _swa_specs(nb) + [
        pl.BlockSpec((BLOCK, GATE_W), lambda b, n: (own(b, n), 0)),
        pl.BlockSpec((1, LANES), lambda b, n: (0, 0)),
    ]
    out_specs = [pl.BlockSpec((BLOCK, HPAD), lambda b, n: (own(b, n), 0)),
                 pl.BlockSpec((BLOCK, HPAD), lambda b, n: (own(b, n), 0)),
                 pl.BlockSpec((BLOCK, LANES), lambda b, n: (own(b, n), 0))]
    out_shape = [jax.ShapeDtypeStruct((T, HPAD), F32), jax.ShapeDtypeStruct((T, HPAD), BF16),
                 jax.ShapeDtypeStruct((T, LANES), F32)]
    return pl.pallas_call(kern, name="swa_fwd", grid=(B, nb), in_specs=in_specs, out_specs=out_specs,
                          out_shape=out_shape, compiler_params=_params(("parallel", "arbitrary")))(
        z, z, z, z, z, pos_row, pos_col, pos_col, gate, sink_row)


def _swa_bwd(z, pos_col, pos_row, sink_row, lse, do_raw, delta, dz, B, S):
    nb = S // BLOCK
    T = B * S

    def kern(q_ref, kp_ref, kc_ref, vp_ref, vc_ref, pq_ref, pkp_ref, pkc_ref, sink_ref, lse_ref, do_ref,
             delta_ref, dz_ref, dq_ref, dk_ref, dv_ref, dsink_ref):
        b, n = pl.program_id(0), pl.program_id(1)

        @pl.when(n == 0)
        def _():
            dk_ref[...] = jnp.zeros_like(dk_ref)
            dv_ref[...] = jnp.zeros_like(dv_ref)

        @pl.when((b == 0) & (n == 0))
        def _():
            dsink_ref[...] = jnp.zeros_like(dsink_ref)

        q_all = q_ref[...]
        kb = jnp.concatenate([kp_ref[...], kc_ref[...]], axis=0).astype(BF16)
        vb = jnp.concatenate([vp_ref[...], vc_ref[...]], axis=0).astype(BF16)
        dist, valid = _swa_band(n, pq_ref, pkp_ref, pkc_ref)
        lse_t, delta_t = lse_ref[...].T, delta_ref[...].T
        lane1 = lax.broadcasted_iota(jnp.int32, (1, LANES), 1)
        dsink = jnp.zeros((1, LANES), F32)
        dk_band, dv_band = [], []
        for grp in range(SWA_KV_HEADS):
            gcols = slice(grp * LANES, (grp + 1) * LANES)
            heads = range(grp * SWA_GROUP, (grp + 1) * SWA_GROUP)
            qg = _swa_group_q(q_all, grp)
            dog = jnp.concatenate([do_ref[:, h * LANES:(h + 1) * LANES] for h in heads], axis=0)
            s_all = _dot_nt(kb[:, gcols], qg)
            dp_all = _dot_nt(vb[:, gcols], dog)
            ps, dss = [], []
            for hh, h in enumerate(heads):
                blk = slice(hh * BLOCK, (hh + 1) * BLOCK)
                lse_h, delta_h = lse_t[h:h + 1, :], delta_t[h:h + 1, :]
                p = jnp.exp(_swa_mask(s_all[:, blk], dist, valid, h) - lse_h)
                ps.append(p.astype(BF16))
                dss.append((p * (dp_all[:, blk] - delta_h)).astype(BF16))
                psink = jnp.exp(sink_ref[0:1, h:h + 1] - lse_h)
                dsink = dsink + jnp.where(lane1 == h, -jnp.sum(psink * delta_h, axis=1, keepdims=True), 0.0)
            dsg = jnp.concatenate(dss, axis=1)
            dq_all = jnp.dot(kb[:, gcols].T, dsg, preferred_element_type=F32) * SWA_SCALE
            for hh, h in enumerate(heads):
                dq_ref[:, h * LANES:(h + 1) * LANES] = dq_all[:, hh * BLOCK:(hh + 1) * BLOCK].T.astype(BF16)
            dk_band.append(jnp.dot(dsg, qg, preferred_element_type=F32))
            dv_band.append(jnp.dot(jnp.concatenate(ps, axis=1), dog, preferred_element_type=F32))
        dsink_ref[...] += dsink
        dkb = jnp.concatenate(dk_band, axis=1)
        dvb = jnp.concatenate(dv_band, axis=1)
        r_prev = pl.ds(pl.multiple_of(jnp.maximum(n - 1, 0) * BLOCK, BLOCK), BLOCK)
        r_own = pl.ds(pl.multiple_of(n * BLOCK, BLOCK), BLOCK)
        dk_ref[r_prev, :] += dkb[:BLOCK]
        dk_ref[r_own, :] += dkb[BLOCK:]
        dv_ref[r_prev, :] += dvb[:BLOCK]
        dv_ref[r_own, :] += dvb[BLOCK:]

    own = lambda b, n: b * nb + n
    in_specs = _swa_specs(nb) + [
        pl.BlockSpec((1, LANES), lambda b, n: (0, 0)),
        pl.BlockSpec((BLOCK, LANES), lambda b, n: (own(b, n), 0)),
        pl.BlockSpec((BLOCK, HPAD), lambda b, n: (own(b, n), 0)),
        pl.BlockSpec((BLOCK, LANES), lambda b, n: (own(b, n), 0)),
        pl.BlockSpec(memory_space=pl.ANY),
    ]
    out_specs = [pl.BlockSpec((BLOCK, HPAD), lambda b, n: (own(b, n), Z_AQ // HPAD)),
                 pl.BlockSpec((S, 256), lambda b, n: (b, 0)),
                 pl.BlockSpec((S, 256), lambda b, n: (b, 0)),
                 pl.BlockSpec((1, LANES), lambda b, n: (0, 0))]
    out_shape = [jax.ShapeDtypeStruct(dz.shape, dz.dtype), jax.ShapeDtypeStruct((T, 256), F32),
                 jax.ShapeDtypeStruct((T, 256), F32), jax.ShapeDtypeStruct((1, LANES), F32)]
    return pl.pallas_call(kern, name="swa_bwd", grid=(B, nb), in_specs=in_specs, out_specs=out_specs,
                          out_shape=out_shape, input_output_aliases={len(in_specs) - 1: 0},
                          compiler_params=_params(("arbitrary", "arbitrary")))(
        z, z, z, z, z, pos_row, pos_col, pos_col, sink_row, lse, do_raw, delta, dz)


MLA_T = 256
MLA_HG = 4
MLA_W = MLA_HG * LANES
MLA_SCALE = MLA_QK ** -0.5
LOG2E = 1.4426950408889634
MLA_QSCALE = MLA_SCALE * LOG2E


def _causal_t(s):
    key = lax.broadcasted_iota(jnp.int32, s.shape, 0)
    query = lax.broadcasted_iota(jnp.int32, s.shape, 1)
    return jnp.where(key <= query, s, NEG)


def _mla_fwd(q, k, v, z, B, S):
    T = B * S
    nq = S // MLA_T

    def kern(q_ref, k_ref, v_ref, gate_ref, oraw_ref, og_ref, lse_ref):
        i = pl.program_id(2)

        def scores(j):
            rows = pl.ds(pl.multiple_of(j * MLA_T, MLA_T), MLA_T)
            return tuple(_dot_nt(k_ref[rows, hh * LANES:(hh + 1) * LANES], q_ref[:, hh * LANES:(hh + 1) * LANES])
                         for hh in range(MLA_HG))

        def update(j, ss, state):
            rows = pl.ds(pl.multiple_of(j * MLA_T, MLA_T), MLA_T)
            out = []
            for hh in range(MLA_HG):
                (m, l, acc), s = state[hh], ss[hh]
                m_new = jnp.maximum(m, jnp.max(s, axis=0, keepdims=True))
                alpha = jnp.exp2(m - m_new)
                p = jnp.exp2(s - m_new)
                l = alpha * l + jnp.sum(p, axis=0, keepdims=True)
                pv = jnp.dot(v_ref[rows, hh * LANES:(hh + 1) * LANES].T, p.astype(BF16), preferred_element_type=F32)
                out.append((m_new, l, alpha * acc + pv))
            return tuple(out)

        def body(j, carry):
            state, ss = carry
            s_next = scores(j + 1)
            return update(j, ss, state), s_next

        init = tuple((jnp.full((1, MLA_T), NEG, F32), jnp.zeros((1, MLA_T), F32), jnp.zeros((LANES, MLA_T), F32))
                     for _ in range(MLA_HG))
        state, ss = lax.fori_loop(0, i, body, (init, scores(0)))
        state = update(i, tuple(_causal_t(s) for s in ss), state)
        for hh in range(MLA_HG):
            m, l, acc = state[hh]
            cols = slice(hh * LANES, (hh + 1) * LANES)
            o = (acc * (1.0 / l)).T
            oraw_ref[:, cols] = o
            g = _head_gate(gate_ref, hh)
            og_ref[:, cols] = (o * (g * jax.nn.sigmoid(g))).astype(BF16)
            lse_ref[0, 0, 0, hh:hh + 1, :] = m + jnp.log2(l)

    blk = lambda b, h, i: (b * nq + i, h)
    in_specs = [pl.BlockSpec((MLA_T, MLA_W), blk),
                pl.BlockSpec((S, MLA_W), lambda b, h, i: (b, h)),
                pl.BlockSpec((S, MLA_W), lambda b, h, i: (b, h)),
                pl.BlockSpec((MLA_T, MLA_W // 2), lambda b, h, i: (b * nq + i, Z_BGATE // (MLA_W // 2) + h))]
    out_specs = [pl.BlockSpec((MLA_T, MLA_W), blk), pl.BlockSpec((MLA_T, MLA_W), blk),
                 pl.BlockSpec((1, 1, 1, MLA_HG, MLA_T), lambda b, h, i: (b, h, i, 0, 0))]
    out_shape = [jax.ShapeDtypeStruct((T, HPAD), F32), jax.ShapeDtypeStruct((T, HPAD), BF16),
                 jax.ShapeDtypeStruct((B, HEADS // MLA_HG, nq, MLA_HG, MLA_T), F32)]
    return pl.pallas_call(kern, name="mla_fwd", grid=(B, HEADS // MLA_HG, nq), in_specs=in_specs,
                          out_specs=out_specs, out_shape=out_shape,
                          compiler_params=_params(("parallel", "parallel", "arbitrary")))(q, k, v, z)


def _mla_bwd(q, k, v, do_raw, lse, delta, B, S):
    T = B * S
    nk = S // MLA_T

    def kern(q_ref, k_ref, v_ref, do_ref, lse_ref, delta_ref, dq_ref, dk_ref, dv_ref, dq_acc, dk_acc, dv_acc):
        j = pl.program_id(2)

        @pl.when(j == 0)
        def _():
            dq_acc[...] = jnp.zeros_like(dq_acc)

        dk_acc[...] = jnp.zeros_like(dk_acc)
        dv_acc[...] = jnp.zeros_like(dv_acc)
        kts = [k_ref[:, hh * LANES:(hh + 1) * LANES].T for hh in range(MLA_HG)]

        def step(i, masked):
            rows = pl.ds(pl.multiple_of(i * MLA_T, MLA_T), MLA_T)
            for hh in range(MLA_HG):
                cols = slice(hh * LANES, (hh + 1) * LANES)
                qv, do = q_ref[rows, cols], do_ref[rows, cols]
                st = _dot_nt(k_ref[:, cols], qv)
                if masked:
                    st = _causal_t(st)
                pt = jnp.exp2(st - lse_ref[0, 0, i, hh:hh + 1, :])
                dpt = _dot_nt(v_ref[:, cols], do)
                dst = (pt * (dpt - delta_ref[0, 0, i, hh:hh + 1, :])).astype(BF16)
                dv_acc[:, cols] += jnp.dot(pt.astype(BF16), do, preferred_element_type=F32)
                dk_acc[:, cols] += jnp.dot(dst, qv, preferred_element_type=F32)
                dq_acc[hh, i] += jnp.dot(kts[hh], dst, preferred_element_type=F32)

        step(j, True)

        def body(i, c):
            step(i, False)
            return c

        lax.fori_loop(j + 1, nk, body, 0)
        dk_ref[...] = dk_acc[...] * (1.0 / LOG2E)
        dv_ref[...] = dv_acc[...]

        @pl.when(j == nk - 1)
        def _():
            for hh in range(MLA_HG):
                for t in range(nk):
                    dq_ref[t * MLA_T:(t + 1) * MLA_T, hh * LANES:(hh + 1) * LANES] = dq_acc[hh, t].T

    whole = lambda b, h, j: (b, h)
    tile = lambda b, h, j: (b * nk + j, h)
    stats = pl.BlockSpec((1, 1, nk, MLA_HG, MLA_T), lambda b, h, j: (b, h, 0, 0, 0))
    in_specs = [pl.BlockSpec((S, MLA_W), whole), pl.BlockSpec((MLA_T, MLA_W), tile),
                pl.BlockSpec((MLA_T, MLA_W), tile), pl.BlockSpec((S, MLA_W), whole), stats, stats]
    out_specs = [pl.BlockSpec((S, MLA_W), whole), pl.BlockSpec((MLA_T, MLA_W), tile),
                 pl.BlockSpec((MLA_T, MLA_W), tile)]
    out_shape = [jax.ShapeDtypeStruct((T, HPAD), F32)] * 3
    scratch = [pltpu.VMEM((MLA_HG, nk, LANES, MLA_T), F32), pltpu.VMEM((MLA_T, MLA_W), F32),
               pltpu.VMEM((MLA_T, MLA_W), F32)]
    return pl.pallas_call(kern, name="mla_bwd", grid=(B, HEADS // MLA_HG, nk), in_specs=in_specs,
                          out_specs=out_specs, out_shape=out_shape, scratch_shapes=scratch,
                          compiler_params=_params(("parallel", "parallel", "arbitrary")))(
        q, k, v, do_raw, lse, delta)


def _rope_tables(pos_col, inv_lane, rows):
    def body(ins, outs, _):
        ang = ins[0][...].astype(F32) * ins[1][...]
        lane = lax.broadcasted_iota(jnp.int32, ang.shape, 1)
        cos, sin = jnp.cos(ang), jnp.sin(ang)
        first = (lane >= HEAD_DIM) & (lane < HEAD_DIM + MLA_ROPE // 2)
        second = (lane >= HEAD_DIM + MLA_ROPE // 2) & (lane < MLA_QK)
        outs[0][...] = jnp.where(lane < HEAD_DIM, 1.0, jnp.where(lane < MLA_QK, cos, 0.0))
        outs[1][...] = jnp.where(first, -sin, 0.0)
        outs[2][...] = jnp.where(second, sin, 0.0)
    return _ew("rope_tables", body, [(pos_col, 1, 0), (inv_lane, None, None)], [(LANES, F32)] * 3, rows)


def _rope(x, c, s1, s2):
    return x * c + pltpu.roll(x, 112, 1) * s1 + pltpu.roll(x, 16, 1) * s2


def _rope_t(d, c, s1, s2):
    return d * c + pltpu.roll(d * s1, 16, 1) + pltpu.roll(d * s2, 112, 1)


def _mla_prep(qdn, w_uq, kvdn, w_ukv, z, tabs, rows):
    def body(ins, outs, _):
        q_pre, kv_pre = ins[0], ins[1]
        c, s1, s2 = ins[3][...], ins[4][...], ins[5][...]
        kr = _rope(ins[2][...].astype(F32), c, s1, s2)
        for h in range(HEADS):
            cols = slice(h * LANES, (h + 1) * LANES)
            outs[0][:, cols] = (_rope(q_pre[:, cols], c, s1, s2) * MLA_QSCALE).astype(BF16)
            outs[1][:, cols] = (kv_pre[:, cols] + kr).astype(BF16)
        outs[2][...] = kv_pre[:, HPAD:].astype(BF16)
    ins = [(z, LANES, Z_BKR // LANES), (tabs[0], LANES, 0), (tabs[1], LANES, 0), (tabs[2], LANES, 0)]
    return _ew("mla_prep", body, ins, [(HPAD, BF16)] * 3, rows, mms=[(qdn, w_uq), (kvdn, w_ukv)])


def _mla_prep_bwd(dq, dk, dv, tabs, dz, rows):
    def body(ins, outs, _):
        c, s1, s2 = ins[3][...], ins[4][...], ins[5][...]
        lane = lax.broadcasted_iota(jnp.int32, c.shape, 1)
        dkr = jnp.zeros(c.shape, F32)
        for h in range(HEADS):
            cols = slice(h * LANES, (h + 1) * LANES)
            outs[0][:, cols] = _rope_t(ins[0][:, cols] * MLA_SCALE, c, s1, s2).astype(BF16)
            dkh = ins[1][:, cols]
            outs[1][:, cols] = jnp.where(lane < HEAD_DIM, dkh, 0.0).astype(BF16)
            dkr = dkr + dkh
        outs[1][:, HPAD:] = ins[2][...].astype(BF16)
        live = (lane >= HEAD_DIM) & (lane < MLA_QK)
        outs[2][...] = jnp.where(live, _rope_t(jnp.where(live, dkr, 0.0), c, s1, s2), 0.0).astype(BF16)
    ins = [(dq, HPAD, 0), (dk, HPAD, 0), (dv, HPAD, 0), (tabs[0], LANES, 0), (tabs[1], LANES, 0),
           (tabs[2], LANES, 0)]
    outs = [(HPAD, BF16), (2 * HPAD, BF16), (LANES, BF16, dz, Z_BKR // LANES)]
    return _ew("mla_prep_bwd", body, ins, outs, rows)


def _gate_bwd(name, d_o_mm, o_raw, gate, gate_cb, dz, dz_cb, rows):
    def body(ins, outs, _):
        lane = lax.broadcasted_iota(jnp.int32, outs[2].shape, 1)
        delta = jnp.zeros(outs[2].shape, F32)
        d_gate = [None] * HEADS
        for h in range(HEADS):
            cols = slice(h * LANES, (h + 1) * LANES)
            dog, o, g = ins[0][:, cols], ins[1][:, cols], _head_gate(ins[2], h)
            sg = jax.nn.sigmoid(g)
            do = dog * (g * sg)
            outs[0][:, cols] = do.astype(BF16)
            d_gate[h] = dog * o * (sg * (1.0 + g * (1.0 - sg)))
            delta = jnp.where(lane == h, jnp.sum(do * o, axis=-1, keepdims=True), delta)
        for pair in range(HEADS // 2):
            packed = d_gate[2 * pair] + pltpu.roll(d_gate[2 * pair + 1], HEAD_DIM, 1)
            outs[1][:, pair * LANES:(pair + 1) * LANES] = packed.astype(BF16)
        outs[2][...] = delta
    ins = [(o_raw, HPAD, 0), (gate, GATE_W, gate_cb)]
    outs = [(HPAD, BF16), (GATE_W, BF16, dz, dz_cb), (LANES, F32)]
    return _ew(name, body, ins, outs, rows, mms=[d_o_mm])


def _merge_out(ua, ub, z, w_out, x0, rows):
    tm = _row_tile(rows)

    def kern(ua_ref, ub_ref, ma_ref, mb_ref, w_ref, x0_ref, y_ref, x1_ref):
        ua_v, ub_v, m_a, m_b = (r[...].astype(F32) for r in (ua_ref, ub_ref, ma_ref, mb_ref))
        y = (jax.nn.sigmoid(m_a) * ua_v + jax.nn.sigmoid(m_b) * ub_v).astype(BF16)
        y_ref[...] = y
        for j in range(D_MODEL // MM_TN):
            cols = slice(j * MM_TN, (j + 1) * MM_TN)
            x1_ref[:, cols] = jnp.dot(y, w_ref[:, cols], preferred_element_type=F32) + x0_ref[:, cols]

    row = lambda cb: pl.BlockSpec((tm, D_MODEL), lambda i: (i, cb))
    return pl.pallas_call(
        kern, name="merge_out", grid=(rows // tm,),
        in_specs=[row(0), row(0), row(Z_MA // D_MODEL), row(Z_MB // D_MODEL),
                  pl.BlockSpec(w_out.shape, lambda i: (0, 0)), row(0)],
        out_specs=[row(0), row(0)],
        out_shape=[jax.ShapeDtypeStruct((rows, D_MODEL), BF16), jax.ShapeDtypeStruct((rows, D_MODEL), F32)],
        compiler_params=_params(("parallel",)))(ua, ub, z, z, w_out, x0)


def _merge_bwd(dy_mm, ua, ub, z, dz, rows):
    def body(ins, outs, _):
        dyv = ins[0][...]
        for idx in range(2):
            s = jax.nn.sigmoid(ins[3 + idx][...].astype(F32))
            outs[idx][...] = (dyv * s).astype(BF16)
            d_m = (dyv * ins[1 + idx][...].astype(F32) * (s * (1.0 - s))).astype(BF16)
            outs[2][:, idx * D_MODEL:(idx + 1) * D_MODEL] = d_m
    ins = [(ua, D_MODEL, 0), (ub, D_MODEL, 0), (z, D_MODEL, Z_MA // D_MODEL), (z, D_MODEL, Z_MB // D_MODEL)]
    outs = [(D_MODEL, BF16), (D_MODEL, BF16), (2 * D_MODEL, BF16, dz, Z_MA // (2 * D_MODEL))]
    return _ew("merge_bwd", body, ins, outs, rows, mms=[dy_mm])


def _kv_grad_cast(dk, dv, dz, rows):
    def body(ins, outs, _):
        outs[0][:, :256] = ins[0][...].astype(BF16)
        outs[0][:, 256:] = ins[1][...].astype(BF16)
    return _ew("kv_grad_cast", body, [(dk, 256, 0), (dv, 256, 0)], [(512, BF16, dz, Z_AK // 512)], rows)[0]


def _ple_fwd(x1, hn, w_pg, p, w_pp, rows):
    def body(ins, outs, _):
        u, e = ins[0][...], ins[1][...]
        outs[0][...] = ins[2][...] + jax.nn.sigmoid(u) * e
        outs[1][...] = u.astype(BF16)
        outs[2][...] = e.astype(BF16)
    return _ew("ple_fwd", body, [(x1, D_MODEL, 0)], [(D_MODEL, F32), (D_MODEL, BF16), (D_MODEL, BF16)], rows,
               mms=[(hn, w_pg), (p, w_pp)])


def _ple_bwd(dx2, u, e, rows):
    def body(ins, outs, _):
        d, s = ins[0][...], jax.nn.sigmoid(ins[1][...].astype(F32))
        outs[0][...] = (d * s).astype(BF16)
        outs[1][...] = (d * ins[2][...].astype(F32) * (s * (1.0 - s))).astype(BF16)
    return _ew("ple_bwd", body, [(dx2, D_MODEL, 0), (u, D_MODEL, 0), (e, D_MODEL, 0)],
               [(D_MODEL, BF16)] * 2, rows)


def _loss_head(x, g, target, rows):
    def body(ins, outs, accs):
        xv, gv = ins[0][...], ins[1][...]
        r = lax.rsqrt(jnp.mean(xv * xv, axis=-1, keepdims=True) + EPS)
        xhat = xv * r
        err = xhat * gv - ins[2][...]
        accs[0][...] += jnp.broadcast_to(0.5 * jnp.sum(jnp.mean(err * err, axis=-1, keepdims=True),
                                                       axis=0, keepdims=True), (1, LANES))
        dyv = err * (1.0 / D_MODEL)
        accs[1][...] += jnp.sum(dyv * xhat, axis=0, keepdims=True)
        dy = dyv * gv
        outs[0][...] = r * (dy - xhat * jnp.mean(dy * xhat, axis=-1, keepdims=True))
    ins = [(x, D_MODEL, 0), (g.reshape(1, D_MODEL), None, None), (target, D_MODEL, 0)]
    return _ew("loss_head", body, ins, [(D_MODEL, F32)], rows, accs=[(1, LANES), (1, D_MODEL)])


def _pad_heads_cols(w, n_heads, dim):
    k = w.shape[0]
    return jnp.pad(w.reshape(k, n_heads, dim), ((0, 0), (0, 0), (0, LANES - dim))).reshape(k, n_heads * LANES)


def _unpad_heads_cols(w, n_heads, dim):
    k = w.shape[0]
    return w.reshape(k, n_heads, LANES)[:, :, :dim].reshape(k, n_heads * dim)


def _layer_weights(w, i):
    segs = jnp.split(w['w_in'][i], list(_cumsum(IN_SIZES))[:-1], axis=1)
    a_q, a_k, a_v, a_gate, b_qd, b_kvd, b_kr, b_gate, m_a, m_b = segs
    kr = jnp.pad(b_kr, ((0, 0), (HEAD_DIM, LANES - MLA_QK)))
    w_in = jnp.concatenate([
        m_a, m_b, _pad_heads_cols(a_q, HEADS, HEAD_DIM), a_gate, b_gate, _pad_heads_cols(a_k, SWA_KV_HEADS, HEAD_DIM),
        _pad_heads_cols(a_v, SWA_KV_HEADS, HEAD_DIM), b_qd, b_kvd, kr], axis=1)
    w_uq = _pad_heads_cols(w['w_uq'][i], HEADS, MLA_QK)
    ukv = w['w_ukv'][i].reshape(MLA_KV_LORA, HEADS, 2 * HEAD_DIM)
    pad = ((0, 0), (0, 0), (0, HEAD_DIM))
    w_ukv = jnp.concatenate([jnp.pad(ukv[:, :, :HEAD_DIM], pad).reshape(MLA_KV_LORA, HPAD),
                             jnp.pad(ukv[:, :, HEAD_DIM:], pad).reshape(MLA_KV_LORA, HPAD)], axis=1)
    w_br_a = _pad_heads_cols(w['w_br_a'][i].T, HEADS, HEAD_DIM).T
    w_br_b = _pad_heads_cols(w['w_br_b'][i].T, HEADS, HEAD_DIM).T
    out = dict(w_in=w_in, w_uq=w_uq, w_ukv=w_ukv, w_br_a=w_br_a, w_br_b=w_br_b, w_out=w['w_out'][i],
               w_pg=w['w_ple_gate'][i], w_pp=w['w_ple_proj'][i])
    for name in ('w_in', 'w_uq', 'w_ukv', 'w_br_a', 'w_br_b', 'w_out', 'w_pg'):
        out[name + '_t'] = out[name].T
    return out


def _cumsum(sizes):
    acc, out = 0, []
    for s in sizes:
        acc += s
        out.append(acc)
    return out


def _unpad_grads(g):
    d = g['w_in']
    seg = lambda off, width: d[:, off:off + width]
    b_kr = seg(Z_BKR, LANES)[:, HEAD_DIM:MLA_QK]
    w_in = jnp.concatenate([
        _unpad_heads_cols(seg(Z_AQ, HPAD), HEADS, HEAD_DIM), _unpad_heads_cols(seg(Z_AK, 256), SWA_KV_HEADS, HEAD_DIM),
        _unpad_heads_cols(seg(Z_AV, 256), SWA_KV_HEADS, HEAD_DIM), seg(Z_AGATE, GATE_W),
        seg(Z_BQD, MLA_Q_LORA), seg(Z_BKVD, MLA_KV_LORA), b_kr, seg(Z_BGATE, GATE_W),
        seg(Z_MA, D_MODEL), seg(Z_MB, D_MODEL)], axis=1)
    w_uq = _unpad_heads_cols(g['w_uq'], HEADS, MLA_QK)
    ukv = g['w_ukv'].reshape(MLA_KV_LORA, 2, HEADS, LANES)[:, :, :, :HEAD_DIM]
    w_ukv = jnp.concatenate([ukv[:, 0], ukv[:, 1]], axis=-1).reshape(MLA_KV_LORA, HEADS * 2 * HEAD_DIM)
    w_br_a = _unpad_heads_cols(g['w_br_a'].T, HEADS, HEAD_DIM).T
    w_br_b = _unpad_heads_cols(g['w_br_b'].T, HEADS, HEAD_DIM).T
    return dict(w_in=w_in, w_uq=w_uq, w_ukv=w_ukv, w_br_a=w_br_a, w_br_b=w_br_b, w_out=g['w_out'],
                w_ple_gate=g['w_pg'], w_ple_proj=g['w_pp'], g_mix=g['g_mix'], sink=g['sink'], g_q=g['g_q'],
                g_kv=g['g_kv'], g_ple=g['g_ple'])


def _layer_fwd(x0, p_i, lw, sm, i, pos_col, pos_row, tabs, B, S):
    T = B * S
    h = _rms_fwd("norm_mix", x0, D_MODEL, 0, sm['g_mix'][i], T)
    z, a_gate = _mm("proj_in", h, lw['w_in'], BF16, f32_cols=(Z_AGATE, GATE_W))
    sink_row = jnp.pad(sm['sink'][i], (0, LANES - HEADS)).reshape(1, LANES)
    oa_raw, oa, lse_a = _swa_fwd(z, a_gate, pos_col, pos_row, sink_row, B, S)
    qdn = _rms_fwd("norm_q", z, MLA_Q_LORA, Z_BQD // MLA_Q_LORA, sm['g_q'][i], T)
    kvdn = _rms_fwd("norm_kv", z, MLA_KV_LORA, Z_BKVD // MLA_KV_LORA, sm['g_kv'][i], T)
    qf, kf, vf = _mla_prep(qdn, lw['w_uq'], kvdn, lw['w_ukv'], z, tabs, T)
    ob_raw, ob, lse_b = _mla_fwd(qf, kf, vf, z, B, S)
    ua = _mm("proj_br_a", oa, lw['w_br_a'], BF16)
    ub = _mm("proj_br_b", ob, lw['w_br_b'], BF16)
    y, x1 = _merge_out(ua, ub, z, lw['w_out'], x0, T)
    hn = _rms_fwd("norm_ple", x1, D_MODEL, 0, sm['g_ple'][i], T)
    x2, u, e = _ple_fwd(x1, hn, lw['w_pg'], p_i, lw['w_pp'], T)
    saved = dict(x0=x0, h=h, z=z, a_gate=a_gate, sink_row=sink_row, oa_raw=oa_raw, oa=oa, lse_a=lse_a, qdn=qdn, kvdn=kvdn,
                 qf=qf, kf=kf, vf=vf, ob_raw=ob_raw, ob=ob, lse_b=lse_b, ua=ua, ub=ub, y=y, x1=x1, hn=hn,
                 u=u, e=e, p=p_i)
    return x2, saved


def _layer_bwd(dx2, sv, lw, sm, i, pos_col, pos_row, tabs, B, S):
    T = B * S
    z = sv['z']
    g = {}
    d_e, d_u = _ple_bwd(dx2, sv['u'], sv['e'], T)
    g['w_pp'] = _mm_tn("grad_pp", sv['p'], d_e)
    g['w_pg'] = _mm_tn("grad_pg", sv['hn'], d_u)
    dx1, g['g_ple'] = _rms_bwd("norm_ple_bwd", sv['x1'], D_MODEL, 0, sm['g_ple'][i], (d_u, lw['w_pg_t']), T, F32,
                               dres=dx2)
    g['w_out'] = _mm_tn("grad_out", sv['y'], dx1)
    dz = lax.empty((T, Z_WIDTH), BF16)
    d_ua, d_ub, dz = _merge_bwd((dx1, lw['w_out_t']), sv['ua'], sv['ub'], z, dz, T)
    g['w_br_a'] = _mm_tn("grad_br_a", sv['oa'], d_ua)
    g['w_br_b'] = _mm_tn("grad_br_b", sv['ob'], d_ub)
    dob_raw, dz, delta_b = _gate_bwd("gate_b_bwd", (d_ub, lw['w_br_b_t']), sv['ob_raw'], z, Z_BGATE // GATE_W,
                                     dz, Z_BGATE // GATE_W, T)
    delta_rows = delta_b[:, :HEADS].reshape(B, S // MLA_T, MLA_T, HEADS // MLA_HG, MLA_HG).transpose(0, 3, 1, 4, 2)
    dq, dk, dv = _mla_bwd(sv['qf'], sv['kf'], sv['vf'], dob_raw, sv['lse_b'], delta_rows, B, S)
    dq_pre, dkv_pre, dz = _mla_prep_bwd(dq, dk, dv, tabs, dz, T)
    g['w_uq'] = _mm_tn("grad_uq", sv['qdn'], dq_pre)
    g['w_ukv'] = _mm_tn("grad_ukv", sv['kvdn'], dkv_pre)
    dz, g['g_q'] = _rms_bwd("norm_q_bwd", z, MLA_Q_LORA, Z_BQD // MLA_Q_LORA, sm['g_q'][i],
                            (dq_pre, lw['w_uq_t']), T, BF16, into=(dz, Z_BQD // MLA_Q_LORA))
    dz, g['g_kv'] = _rms_bwd("norm_kv_bwd", z, MLA_KV_LORA, Z_BKVD // MLA_KV_LORA, sm['g_kv'][i],
                             (dkv_pre, lw['w_ukv_t']), T, BF16, into=(dz, Z_BKVD // MLA_KV_LORA))
    doa_raw, dz, delta_a = _gate_bwd("gate_a_bwd", (d_ua, lw['w_br_a_t']), sv['oa_raw'], sv['a_gate'], 0,
                                     dz, Z_AGATE // GATE_W, T)
    dz, d_ak, d_av, dsink = _swa_bwd(z, pos_col, pos_row, sv['sink_row'], sv['lse_a'], doa_raw, delta_a, dz, B, S)
    dz = _kv_grad_cast(d_ak, d_av, dz, T)
    g['sink'] = dsink[0, :HEADS]
    g['w_in'] = _mm_tn("grad_in", sv['h'], dz, tn=Z_WIDTH // 2)
    dx0, g['g_mix'] = _rms_bwd("norm_mix_bwd", sv['x0'], D_MODEL, 0, sm['g_mix'][i], (dz, lw['w_in_t']), T, F32,
                               dres=dx1)
    for name in ('g_ple', 'g_q', 'g_kv', 'g_mix'):
        g[name] = g[name][0]
    return dx0, g


def _local_step(x, p, positions, wfull, sm, loss_target):
    B, S, _ = x.shape
    T = B * S
    pos_col = positions.reshape(T, 1)
    pos_row = positions.reshape(T // BLOCK, 1, BLOCK)
    half = MLA_ROPE // 2
    inv = ROPE_THETA ** (-jnp.arange(0, MLA_ROPE, 2, dtype=F32) / MLA_ROPE)
    inv_lane = jnp.tile(inv, LANES // half).reshape(1, LANES)
    tabs = _rope_tables(pos_col, inv_lane, T)
    xc = x.reshape(T, D_MODEL)
    lws, saved = [], []
    for i in range(DEPTH):
        lw = _layer_weights(wfull, i)
        xc, sv = _layer_fwd(xc, p[i].reshape(T, PLE_DIM), lw, sm, i, pos_col, pos_row, tabs, B, S)
        lws.append(lw)
        saved.append(sv)
    dx, loss, dg_final = _loss_head(xc, sm['g_final'], loss_target.reshape(T, D_MODEL), T)
    layer_grads = [None] * DEPTH
    for i in reversed(range(DEPTH)):
        dx, g = _layer_bwd(dx, saved[i], lws[i], sm, i, pos_col, pos_row, tabs, B, S)
        layer_grads[i] = _unpad_grads(g)
    return loss, dx.reshape(B, S, D_MODEL), layer_grads, dg_final[0]


SMALL_ROWS = 48


def _pack_small(arrs):
    flat = jnp.concatenate([arrs[name].reshape(-1) for name in SMALL])
    return jnp.pad(flat, (0, SMALL_ROWS * LANES - flat.shape[0])).reshape(SMALL_ROWS, LANES)


def _unpack_small(block, shapes):
    flat = block.reshape(-1)
    out, off = {}, 0
    for name in SMALL:
        n = math.prod(shapes[name])
        out[name] = flat[off:off + n].reshape(shapes[name])
        off += n
    return out


def _flipped(shard_shape):
    return shard_shape[-1] % LANES != 0


def _to_slots(g, axis):
    r, c = g.shape
    if axis == 0:
        return g.reshape(N_CHIPS, r // N_CHIPS, c)
    return g.reshape(r, N_CHIPS, c // N_CHIPS).transpose(1, 0, 2)


def _div_tile(rows, cap):
    return next(t for t in range(min(cap, rows) // 8 * 8, 0, -8) if rows % t == 0)


def _units(shapes):
    units = []
    for w, shape in enumerate(shapes):
        r = shape[-2]
        n = next(n for n in (8, 7, 4, 2, 1) if r % (8 * n) == 0) if r >= 1024 else 1
        units += [(w, k * (r // n), r // n) for k in range(n)]
    return units


def _place():
    x, y, c = lax.axis_index("x"), lax.axis_index("y"), lax.axis_index("c")
    chips = [(1 - x, y), (x, 1 - y), (1 - x, 1 - y)]
    return x, y, c, chips


ANY = pl.BlockSpec(memory_space=pl.ANY)


def _remote(send_sems, recv_sems, k, src, dst, to):
    return pltpu.make_async_remote_copy(src_ref=src, dst_ref=dst, send_sem=send_sems.at[k],
                                        recv_sem=recv_sems.at[k], device_id=to, device_id_type=MESH)


def _gather_weights(shards):
    n = len(shards)
    units = _units([s.shape for s in shards])
    nu = len(units)

    def body(*refs):
        ins, outs = refs[:n], refs[n:2 * n]
        send_sems, recv_sems, local_sems = refs[2 * n:]
        x, y, c, chips = _place()
        me = 2 * x + y
        sibling = (x, y, 1 - c)
        copy = functools.partial(_remote, send_sems, recv_sems)
        keeps, sends = [], []
        for u, (w, r0, nr) in enumerate(units):
            rows = pl.ds(r0, nr)
            keeps.append(pltpu.make_async_copy(ins[w].at[:, rows, :], outs[w].at[me, :, rows, :], local_sems.at[u]))
            keeps[-1].start()
        for j, (cx, cy) in enumerate(chips):
            for u, (w, r0, nr) in enumerate(units):
                rows = pl.ds(r0, nr)
                sends.append(copy(j * nu + u, ins[w].at[c, rows, :], outs[w].at[me, c, rows, :], (cx, cy, c)))
                sends[-1].start()
        for j, (cx, cy) in enumerate(chips):
            for u, (w, r0, nr) in enumerate(units):
                landed = outs[w].at[2 * cx + cy, c, pl.ds(r0, nr), :]
                copy(j * nu + u, landed, landed, (cx, cy, c)).wait_recv()
                sends.append(copy((3 + j) * nu + u, landed, landed, sibling))
                sends[-1].start()
        for j, (cx, cy) in enumerate(chips):
            for u, (w, r0, nr) in enumerate(units):
                other = outs[w].at[2 * cx + cy, 1 - c, pl.ds(r0, nr), :]
                copy((3 + j) * nu + u, other, other, sibling).wait_recv()
        for cp in sends:
            cp.wait_send()
        for keep in keeps:
            keep.wait()

    return pl.pallas_call(
        body, name="gather_weights",
        out_shape=[jax.ShapeDtypeStruct((N_CHIPS,) + s.shape, s.dtype) for s in shards],
        in_specs=[ANY] * n, out_specs=[ANY] * n,
        scratch_shapes=[pltpu.SemaphoreType.DMA((6 * nu,)), pltpu.SemaphoreType.DMA((6 * nu,)),
                        pltpu.SemaphoreType.DMA((nu,))])(*shards)


def _pair_exchange(g0, g1):
    n = len(g0)

    def body(*refs):
        layers, outs = (refs[:n], refs[n:2 * n]), refs[2 * n:3 * n]
        send_sems, recv_sems = refs[3 * n:]
        x, y, c, _ = _place()
        copy = functools.partial(_remote, send_sems, recv_sems)
        for w in range(n):
            for q in range(N_CHIPS):
                for layer in range(DEPTH):
                    cp = copy(N_CHIPS * w + q, layers[layer][w].at[q], outs[w].at[q], (x, y, 1 - c))
                    pl.when(c == 1 - layer)(cp.start)
        for w in range(n):
            for q in range(N_CHIPS):
                copy(N_CHIPS * w + q, layers[0][w].at[q], outs[w].at[q], (x, y, 1 - c)).wait()

    return pl.pallas_call(
        body, name="pair_exchange", out_shape=[jax.ShapeDtypeStruct(g.shape, g.dtype) for g in g0],
        in_specs=[ANY] * (2 * n), out_specs=[ANY] * n,
        scratch_shapes=[pltpu.SemaphoreType.DMA((N_CHIPS * n,)), pltpu.SemaphoreType.DMA((N_CHIPS * n,))])(*g0, *g1)


def _pair_sum(name, g0, g1, theirs, cflag):
    shape = theirs.shape
    rows, width = shape[0] * shape[1], shape[2]

    def body(ins, outs, _):
        mine = jnp.where(ins[3][0:1, 0:1] == 0.0, ins[0][...], ins[1][...])
        tot = mine + ins[2][...]
        outs[0][...] = tot
        outs[1][...] = tot.astype(BF16)
    ins = [(a.reshape(rows, width), width, 0) for a in (g0, g1, theirs)] + [(cflag, None, None)]
    f32, bf16 = _ew(name, body, ins, [(width, F32), (width, BF16)], rows, tm=_div_tile(rows, ROW_TILE))
    return f32.reshape(shape), bf16.reshape(shape)


def _chip_exchange(parts):
    n = len(parts)

    def body(*refs):
        ins, outs = refs[:n], refs[n:2 * n]
        send_sems, recv_sems = refs[2 * n:]
        x, y, c, chips = _place()
        copy = functools.partial(_remote, send_sems, recv_sems)
        sends = []
        for j, (cx, cy) in enumerate(chips):
            for w in range(n):
                sends.append(copy(j * n + w, ins[w].at[2 * cx + cy], outs[w].at[j], (cx, cy, c)))
                sends[-1].start()
        for j, (cx, cy) in enumerate(chips):
            for w in range(n):
                copy(j * n + w, outs[w].at[j], outs[w].at[j], (cx, cy, c)).wait_recv()
        for cp in sends:
            cp.wait_send()

    return pl.pallas_call(
        body, name="chip_exchange",
        out_shape=[jax.ShapeDtypeStruct((3,) + a.shape[1:], a.dtype) for a in parts],
        in_specs=[ANY] * n, out_specs=[ANY] * n,
        scratch_shapes=[pltpu.SemaphoreType.DMA((3 * n,)), pltpu.SemaphoreType.DMA((3 * n,))])(*parts)


def _chip_sum(name, part, landed, chipflag):
    _, r, width = part.shape
    tm = _div_tile(r, ROW_TILE // 2)

    def kern(p_ref, l_ref, flag_ref, o_ref):
        me = flag_ref[0:1, 0:1]
        own = jnp.where(me == 0.0, p_ref[0], jnp.where(me == 1.0, p_ref[1], jnp.where(me == 2.0, p_ref[2], p_ref[3])))
        o_ref[...] = ((own + l_ref[0].astype(F32)) + l_ref[1].astype(F32)) + l_ref[2].astype(F32)

    return pl.pallas_call(
        kern, name=name, grid=(r // tm,),
        in_specs=[pl.BlockSpec((N_CHIPS, tm, width), lambda i: (0, i, 0)),
                  pl.BlockSpec((3, tm, width), lambda i: (0, i, 0)),
                  pl.BlockSpec((1, LANES), lambda i: (0, 0))],
        out_specs=pl.BlockSpec((tm, width), lambda i: (i, 0)),
        out_shape=jax.ShapeDtypeStruct((r, width), F32), compiler_params=_params(("arbitrary",)))(part, landed, chipflag)


def _pair_broadcast(mine):
    n = len(mine)
    units = _units([a.shape for a in mine])

    def body(*refs):
        ins, outs = refs[:n], refs[n:2 * n]
        send_sems, recv_sems = refs[2 * n:]
        x, y, c, _ = _place()
        copy = functools.partial(_remote, send_sems, recv_sems)
        cps = [copy(u, ins[w].at[pl.ds(r0, nr), :], outs[w].at[pl.ds(r0, nr), :], (x, y, 1 - c))
               for u, (w, r0, nr) in enumerate(units)]
        for cp in cps:
            cp.start()
        for cp in cps:
            cp.wait()

    return pl.pallas_call(
        body, name="pair_broadcast", out_shape=[jax.ShapeDtypeStruct(a.shape, a.dtype) for a in mine],
        in_specs=[ANY] * n, out_specs=[ANY] * n,
        scratch_shapes=[pltpu.SemaphoreType.DMA((len(units),)), pltpu.SemaphoreType.DMA((len(units),))])(*mine)


def _small_allreduce(v):
    offsets = [(dx, dy, dc) for dx in (0, 1) for dy in (0, 1) for dc in (0, 1)][1:]

    def body(v_ref, out_ref, recv_ref, send_sems, recv_sems):
        x, y, c, _ = _place()
        flip = lambda a, d: 1 - a if d else a
        peers = [(flip(x, dx), flip(y, dy), flip(c, dc)) for dx, dy, dc in offsets]
        copy = functools.partial(_remote, send_sems, recv_sems)
        me = 4 * x + 2 * y + c
        recv_ref[me] = v_ref[...]
        cps = [copy(k, v_ref, recv_ref.at[me], peer) for k, peer in enumerate(peers)]
        for cp in cps:
            cp.start()
        for k, (px, py, pc) in enumerate(peers):
            landed = recv_ref.at[4 * px + 2 * py + pc]
            copy(k, landed, landed, (px, py, pc)).wait_recv()
        for cp in cps:
            cp.wait_send()
        tot = recv_ref[0]
        for d in range(1, 8):
            tot = tot + recv_ref[d]
        out_ref[...] = tot

    vmem = pl.BlockSpec(memory_space=pltpu.VMEM)
    return pl.pallas_call(
        body, name="small_allreduce", out_shape=jax.ShapeDtypeStruct(v.shape, v.dtype),
        in_specs=[vmem], out_specs=vmem,
        scratch_shapes=[pltpu.VMEM((8,) + v.shape, v.dtype), pltpu.SemaphoreType.DMA((7,)),
                        pltpu.SemaphoreType.DMA((7,))])(v)


def _adam_math(gv, wv, mv, vv):
    mv = ADAM_B1 * mv + (1.0 - ADAM_B1) * gv
    vv = ADAM_B2 * vv + (1.0 - ADAM_B2) * (gv * gv)
    m_hat = mv / (1.0 - ADAM_B1 ** ADAM_STEP)
    v_hat = vv / (1.0 - ADAM_B2 ** ADAM_STEP)
    return -ADAM_LR * (m_hat / (jnp.sqrt(v_hat) + ADAM_EPS) + ADAM_WD * wv), mv, vv


def _adamw_big(name, mine, theirs, cflag, w, m, v):
    _, r, width = w.shape
    tm = _div_tile(r, ROW_TILE // 2)

    def kern(mine_ref, theirs_ref, flag_ref, w_ref, m_ref, v_ref, g_ref, d_ref, nm_ref, nv_ref):
        layer = pl.program_id(0).astype(F32)
        gv = jnp.where(flag_ref[0:1, 0:1] == layer, mine_ref[...], theirs_ref[...])
        g_ref[0] = gv
        d_ref[0], nm_ref[0], nv_ref[0] = _adam_math(gv, w_ref[0], m_ref[0], v_ref[0])

    flat = pl.BlockSpec((tm, width), lambda l, i: (i, 0))
    stacked = pl.BlockSpec((1, tm, width), lambda l, i: (l, i, 0))
    return pl.pallas_call(
        kern, name=name, grid=(DEPTH, r // tm),
        in_specs=[flat, flat, pl.BlockSpec((1, LANES), lambda l, i: (0, 0)), stacked, stacked, stacked],
        out_specs=[stacked] * 4, out_shape=[jax.ShapeDtypeStruct(w.shape, F32)] * 4,
        compiler_params=_params(("arbitrary", "arbitrary")))(mine, theirs, cflag, w, m, v)


def _adamw_small(g, w, m, v):
    def body(ins, outs, _):
        outs[0][...], outs[1][...], outs[2][...] = _adam_math(*(r[...] for r in ins))
    return _ew("adamw_small", body, [(a, LANES, 0) for a in (g, w, m, v)], [(LANES, F32)] * 3, SMALL_ROWS)


def kernel(x, p, positions, g_mix, w_in, sink, g_q, w_uq, g_kv, w_ukv, w_br_a, w_br_b, w_out, g_ple, w_ple_gate, w_ple_proj, g_final, loss_target, m_g_mix, m_w_in, m_sink, m_g_q, m_w_uq, m_g_kv, m_w_ukv, m_w_br_a, m_w_br_b, m_w_out, m_g_ple, m_w_ple_gate, m_w_ple_proj, m_g_final, v_g_mix, v_w_in, v_sink, v_g_q, v_w_uq, v_g_kv, v_w_ukv, v_w_br_a, v_w_br_b, v_w_out, v_g_ple, v_w_ple_gate, v_w_ple_proj, v_g_final):
    w = dict(g_mix=g_mix, w_in=w_in, sink=sink, g_q=g_q, w_uq=w_uq, g_kv=g_kv, w_ukv=w_ukv, w_br_a=w_br_a,
             w_br_b=w_br_b, w_out=w_out, g_ple=g_ple, w_ple_gate=w_ple_gate, w_ple_proj=w_ple_proj, g_final=g_final)
    m = dict(g_mix=m_g_mix, w_in=m_w_in, sink=m_sink, g_q=m_g_q, w_uq=m_w_uq, g_kv=m_g_kv, w_ukv=m_w_ukv,
             w_br_a=m_w_br_a, w_br_b=m_w_br_b, w_out=m_w_out, g_ple=m_g_ple, w_ple_gate=m_w_ple_gate,
             w_ple_proj=m_w_ple_proj, g_final=m_g_final)
    v = dict(g_mix=v_g_mix, w_in=v_w_in, sink=v_sink, g_q=v_g_q, w_uq=v_w_uq, g_kv=v_g_kv, w_ukv=v_w_ukv,
             w_br_a=v_w_br_a, w_br_b=v_w_br_b, w_out=v_w_out, g_ple=v_g_ple, w_ple_gate=v_w_ple_gate,
             w_ple_proj=v_w_ple_proj, g_final=v_g_final)
    wfull = _gather_full(w)
    sm = {name: w[name] for name in SMALL}
    loss_row, grad_x, layer_grads, dg_final = _local_step(x, p, positions, wfull, sm, loss_target)
    loss = lax.psum(loss_row[0, 0], ("x", "y", "c"))
    res = _update(layer_grads, dg_final, w, m, v)
    return (loss, grad_x, *[res[name][kind] for kind in range(4) for name in WEIGHT_NAMES])


def _gather_full(w):
    gathered = _gather_weights([w[name].astype(BF16) for name, _ in SHARDED])
    return {name: [jnp.concatenate([gathered[k][q, layer] for q in range(N_CHIPS)], axis=axis - 1)
                   for layer in range(DEPTH)] for k, (name, axis) in enumerate(SHARDED)}


def _update(layer_grads, dg_final, w, m, v):
    small_shapes = {name: w[name].shape for name in SMALL}
    cflag = jnp.full((1, LANES), lax.axis_index("c"), F32)
    chipflag = jnp.full((1, LANES), 2 * lax.axis_index("x") + lax.axis_index("y"), F32)

    slots = [[_to_slots(layer_grads[layer][name], axis - 1) for name, axis in SHARDED] for layer in range(DEPTH)]
    theirs = _pair_exchange(slots[0], slots[1])
    pair = [_pair_sum("pair_sum_" + name, slots[0][k], slots[1][k], theirs[k], cflag)
            for k, (name, _) in enumerate(SHARDED)]
    landed = _chip_exchange([bf16 for _, bf16 in pair])
    mine = [_chip_sum("chip_sum_" + name, pair[k][0], landed[k], chipflag) for k, (name, _) in enumerate(SHARDED)]
    other = _pair_broadcast(mine)
    res = {}
    for k, (name, _) in enumerate(SHARDED):
        flip = _flipped(w[name].shape)
        view = (lambda a: jnp.swapaxes(a, -1, -2)) if flip else (lambda a: a)
        outs = _adamw_big("adamw_" + name, view(mine[k]), view(other[k]), cflag, view(w[name]), view(m[name]),
                          view(v[name]))
        res[name] = tuple(view(a) for a in outs)

    gsmall = {name: jnp.stack([layer_grads[layer][name] for layer in range(DEPTH)]) for name in SMALL[:-1]}
    gsmall['g_final'] = dg_final
    gsum = _small_allreduce(_pack_small(gsmall))
    small = (gsum,) + tuple(_adamw_small(gsum, _pack_small(w), _pack_small(m), _pack_small(v)))
    for name, arrs in zip(SMALL, zip(*[[_unpack_small(a, small_shapes)[n] for n in SMALL] for a in small])):
        res[name] = arrs
    return res
```

```python
import functools
import math

import jax
import jax.numpy as jnp
from jax import lax
from jax.experimental import pallas as pl
from jax.experimental.pallas import tpu as pltpu
from jax.experimental.pallas import tpu_sc as plsc

F32 = jnp.float32
BF16 = jnp.bfloat16

D_MODEL = 1024
DEPTH = 2
PLE_DIM = 256
BLOCK = 128
EPS = 1e-6
NEG = -1e30
HEADS = 8
SWA_KV_HEADS = 2
HEAD_DIM = 64
LANES = 128
HPAD = HEADS * LANES
MLA_QK = 96
MLA_ROPE = 32
MLA_Q_LORA = 256
MLA_KV_LORA = 128
ROPE_THETA = 10000.0
IN_SIZES = (512, 128, 128, 512, 256, 128, 32, 512, 1024, 1024)

Z_MA, Z_MB, Z_AQ, Z_AGATE, Z_BGATE = 0, 1024, 2048, 3072, 3584
Z_AK, Z_AV, Z_BQD, Z_BKVD, Z_BKR = 4096, 4352, 4608, 4864, 4992
Z_WIDTH = 5120
GATE_W = HEADS * HEAD_DIM

ADAM_LR, ADAM_B1, ADAM_B2, ADAM_EPS, ADAM_WD, ADAM_STEP = 0.001, 0.9, 0.999, 1e-08, 0.01, 10

VMEM_LIMIT = 56 * 1024 * 1024
MESH = pl.DeviceIdType.MESH

WEIGHT_NAMES = ('g_mix', 'w_in', 'sink', 'g_q', 'w_uq', 'g_kv', 'w_ukv', 'w_br_a', 'w_br_b',
                'w_out', 'g_ple', 'w_ple_gate', 'w_ple_proj', 'g_final')
SHARDED = (('w_in', 2), ('w_uq', 2), ('w_ukv', 2), ('w_br_a', 2), ('w_br_b', 2),
           ('w_out', 1), ('w_ple_gate', 1), ('w_ple_proj', 2))
SMALL = ('g_mix', 'sink', 'g_q', 'g_kv', 'g_ple', 'g_final')
N_CHIPS = 4


def _params(sem):
    return pltpu.CompilerParams(dimension_semantics=sem, vmem_limit_bytes=VMEM_LIMIT)


MM_TN = 512
ROW_TILE = 512
BIG_WEIGHT_BYTES = 8 * 1024 * 1024


def _row_tile(rows, weight_bytes=0):
    tm = ROW_TILE // 2 if weight_bytes > BIG_WEIGHT_BYTES else ROW_TILE
    return min(tm, rows)


def _ew(name, body, ins, outs, rows, accs=(), mms=(), tm=None):
    n_mm, n_in, n_out = len(mms), len(ins), len(outs)
    if tm is None:
        tm = _row_tile(rows, sum(b.size * b.dtype.itemsize for _, b in mms))
    in_specs, args = [], []
    for a, b in mms:
        in_specs += [pl.BlockSpec((tm, a.shape[1]), lambda i: (i, 0)), pl.BlockSpec(b.shape, lambda i: (0, 0))]
        args += [a, b]
    for arr, width, cb in ins:
        if width is None:
            in_specs.append(pl.BlockSpec(arr.shape, lambda i, nd=arr.ndim: (0,) * nd))
        else:
            in_specs.append(pl.BlockSpec((tm, width), lambda i, cb=cb: (i, cb)))
        args.append(arr)
    out_shape, out_specs, aliases = [], [], {}
    for k, out in enumerate(outs):
        if len(out) == 4:
            aliases[len(args)] = k
            in_specs.append(pl.BlockSpec(memory_space=pl.ANY))
            args.append(out[2])
            out_shape.append(jax.ShapeDtypeStruct(out[2].shape, out[2].dtype))
            out_specs.append(pl.BlockSpec((tm, out[0]), lambda i, cb=out[3]: (i, cb)))
        else:
            out_shape.append(jax.ShapeDtypeStruct((rows, out[0]), out[1]))
            out_specs.append(pl.BlockSpec((tm, out[0]), lambda i: (i, 0)))
    n_in += len(aliases)
    out_shape += [jax.ShapeDtypeStruct(s, F32) for s in accs]
    out_specs += [pl.BlockSpec(s, lambda i: (0, 0)) for s in accs]

    def kern(*refs):
        mm_refs, refs = refs[:2 * n_mm], refs[2 * n_mm:]
        in_refs, out_refs = refs[:n_in - len(aliases)], refs[n_in:n_in + n_out]
        acc_refs, prod_refs = refs[n_in + n_out:n_in + n_out + len(accs)], refs[n_in + n_out + len(accs):]
        if acc_refs:
            @pl.when(pl.program_id(0) == 0)
            def _():
                for r in acc_refs:
                    r[...] = jnp.zeros_like(r)
        for k in range(n_mm):
            a_ref, b_ref, prod = mm_refs[2 * k], mm_refs[2 * k + 1], prod_refs[k]
            av = a_ref[...].astype(BF16)
            n = b_ref.shape[1]
            tn = min(MM_TN, n)
            for j in range(n // tn):
                cols = slice(j * tn, (j + 1) * tn)
                prod[:, cols] = jnp.dot(av, b_ref[:, cols], preferred_element_type=F32)
        body(tuple(prod_refs) + tuple(in_refs), out_refs, acc_refs)

    scratch = [pltpu.VMEM((tm, b.shape[1]), F32) for _, b in mms]
    res = pl.pallas_call(kern, name=name, grid=(rows // tm,), in_specs=in_specs, out_specs=out_specs,
                         out_shape=out_shape, scratch_shapes=scratch, input_output_aliases=aliases,
                         compiler_params=_params(("arbitrary",)))(*args)
    return res


def _rms_fwd(name, x, width, cb, g, rows):
    def body(ins, outs, _):
        xv = ins[0][...].astype(F32)
        r = lax.rsqrt(jnp.mean(xv * xv, axis=-1, keepdims=True) + EPS)
        outs[0][...] = ((xv * r) * ins[1][...]).astype(BF16)
    return _ew(name, body, [(x, width, cb), (g.reshape(1, width), None, None)], [(width, BF16)], rows)[0]


def _rms_bwd(name, x, width, cb, g, dh_mm, rows, out_dtype, dres=None, into=()):
    def body(ins, outs, accs):
        dhv, xv, gv = ins[0][...], ins[1][...].astype(F32), ins[2][...]
        r = lax.rsqrt(jnp.mean(xv * xv, axis=-1, keepdims=True) + EPS)
        xhat = xv * r
        accs[0][...] += jnp.sum(dhv * xhat, axis=0, keepdims=True)
        dy = dhv * gv
        dx = r * (dy - xhat * jnp.mean(dy * xhat, axis=-1, keepdims=True))
        if dres is not None:
            dx = dx + ins[3][...]
        outs[0][...] = dx.astype(out_dtype)
    ins = [(x, width, cb), (g.reshape(1, width), None, None)]
    if dres is not None:
        ins.append((dres, width, 0))
    return _ew(name, body, ins, [(width, out_dtype) + tuple(into)], rows, accs=[(1, width)], mms=[dh_mm])


def _mm(name, a, b, out_dtype, residual=None, f32_cols=None, tn=MM_TN):
    M, K = a.shape
    N = b.shape[1]
    tm, tn = _row_tile(M, b.size * b.dtype.itemsize), min(tn, N)
    has_res = residual is not None
    c0, cw = f32_cols if f32_cols else (0, 0)

    def kern(*refs):
        a_ref, b_ref = refs[0], refs[1]
        o_ref = refs[3] if has_res else refs[2]
        av = a_ref[...].astype(BF16)
        for j in range(N // tn):
            cols = slice(j * tn, (j + 1) * tn)
            part = jnp.dot(av, b_ref[:, cols], preferred_element_type=F32)
            if has_res:
                part = part + refs[2][:, cols]
            o_ref[:, cols] = part.astype(o_ref.dtype)
            if c0 <= j * tn and (j + 1) * tn <= c0 + cw:
                refs[-1][:, j * tn - c0:(j + 1) * tn - c0] = part

    in_specs = [pl.BlockSpec((tm, K), lambda i: (i, 0)), pl.BlockSpec((K, N), lambda i: (0, 0))]
    args = [a, b]
    if has_res:
        in_specs.append(pl.BlockSpec((tm, N), lambda i: (i, 0)))
        args.append(residual)
    out_specs = [pl.BlockSpec((tm, N), lambda i: (i, 0))]
    out_shape = [jax.ShapeDtypeStruct((M, N), out_dtype)]
    if f32_cols:
        assert c0 % tn == 0 and cw % tn == 0
        out_specs.append(pl.BlockSpec((tm, cw), lambda i: (i, 0)))
        out_shape.append(jax.ShapeDtypeStruct((M, cw), F32))
    res = pl.pallas_call(kern, name=name, grid=(M // tm,), in_specs=in_specs, out_specs=out_specs,
                         out_shape=out_shape, compiler_params=_params(("parallel",)))(*args)
    return res if f32_cols else res[0]


def _mm_tn(name, a, b, tk=512, tn=2048):
    T, M = a.shape
    N = b.shape[1]
    tn, tk = min(tn, N), min(tk, T)

    def kern(a_ref, b_ref, o_ref):
        k = pl.program_id(1)
        part = _dot_tn(a_ref[...].astype(BF16), b_ref[...].astype(BF16))

        @pl.when(k == 0)
        def _():
            o_ref[...] = part

        @pl.when(k > 0)
        def _():
            o_ref[...] += part

    return pl.pallas_call(
        kern, name=name, grid=(N // tn, T // tk),
        in_specs=[pl.BlockSpec((tk, M), lambda j, k: (k, 0)), pl.BlockSpec((tk, tn), lambda j, k: (k, j))],
        out_specs=pl.BlockSpec((M, tn), lambda j, k: (0, j)),
        out_shape=jax.ShapeDtypeStruct((M, N), F32),
        compiler_params=_params(("parallel", "arbitrary")))(a, b)


def _dot_nt(a, b):
    return lax.dot_general(a, b, (((1,), (1,)), ((), ())), preferred_element_type=F32)


def _dot_tn(a, b):
    return lax.dot_general(a, b, (((0,), (0,)), ((), ())), preferred_element_type=F32)


SWA_SCALE = HEAD_DIM ** -0.5


def _swa_band(n, pq_ref, pkp_ref, pkc_ref):
    posk = jnp.concatenate([pkp_ref[...], pkc_ref[...]], axis=0)
    dist = (pq_ref[0] - posk).astype(F32)
    kj = lax.broadcasted_iota(jnp.int32, (2 * BLOCK, BLOCK), 0)
    qi = lax.broadcasted_iota(jnp.int32, (2 * BLOCK, BLOCK), 1)
    t_abs = n * BLOCK + qi
    s_abs = n * BLOCK - BLOCK + kj
    return dist, (s_abs >= 0) & (s_abs <= t_abs) & (t_abs - s_abs < BLOCK)


SWA_GROUP = HEADS // SWA_KV_HEADS


def _head_gate(gate_ref, h):
    pair = gate_ref[:, (h // 2) * LANES:(h // 2 + 1) * LANES].astype(F32)
    return pair if h % 2 == 0 else pltpu.roll(pair, HEAD_DIM, 1)


def _swa_group_q(q_all, g):
    heads = range(g * SWA_GROUP, (g + 1) * SWA_GROUP)
    return jnp.concatenate([(q_all[:, h * LANES:(h + 1) * LANES] * SWA_SCALE).astype(BF16) for h in heads], axis=0)


def _swa_mask(s, dist, valid, h):
    return jnp.where(valid, s - (2.0 ** -(h + 1)) * dist, NEG)


def _rows_to_lanes(rows):
    block = jnp.concatenate(list(rows) + [jnp.zeros((LANES - len(rows), BLOCK), F32)], axis=0)
    return block.T


def _swa_specs(nb):
    prev = lambda b, n: b * nb + jnp.maximum(n - 1, 0)
    own = lambda b, n: b * nb + n
    return [
        pl.BlockSpec((BLOCK, HPAD), lambda b, n: (own(b, n), Z_AQ // HPAD)),
        pl.BlockSpec((BLOCK, 256), lambda b, n: (prev(b, n), Z_AK // 256)),
        pl.BlockSpec((BLOCK, 256), lambda b, n: (own(b, n), Z_AK // 256)),
        pl.BlockSpec((BLOCK, 256), lambda b, n: (prev(b, n), Z_AV // 256)),
        pl.BlockSpec((BLOCK, 256), lambda b, n: (own(b, n), Z_AV // 256)),
        pl.BlockSpec((1, 1, BLOCK), lambda b, n: (own(b, n), 0, 0)),
        pl.BlockSpec((BLOCK, 1), lambda b, n: (prev(b, n), 0)),
        pl.BlockSpec((BLOCK, 1), lambda b, n: (own(b, n), 0)),
    ]


def _swa_fwd(z, gate, pos_col, pos_row, sink_row, B, S):
    nb = S // BLOCK
    T = B * S

    def kern(q_ref, kp_ref, kc_ref, vp_ref, vc_ref, pq_ref, pkp_ref, pkc_ref, gate_ref, sink_ref,
             oraw_ref, og_ref, lse_ref):
        q_all = q_ref[...]
        kb = jnp.concatenate([kp_ref[...], kc_ref[...]], axis=0).astype(BF16)
        vb = jnp.concatenate([vp_ref[...], vc_ref[...]], axis=0).astype(BF16)
        dist, valid = _swa_band(pl.program_id(1), pq_ref, pkp_ref, pkc_ref)
        lse_rows = []
        for grp in range(SWA_KV_HEADS):
            gcols = slice(grp * LANES, (grp + 1) * LANES)
            s_all = _dot_nt(kb[:, gcols], _swa_group_q(q_all, grp))
            probs = []
            for hh in range(SWA_GROUP):
                h = grp * SWA_GROUP + hh
                s = _swa_mask(s_all[:, hh * BLOCK:(hh + 1) * BLOCK], dist, valid, h)
                sink_h = sink_ref[0:1, h:h + 1]
                m = jnp.maximum(jnp.max(s, axis=0, keepdims=True), sink_h)
                e = jnp.exp(s - m)
                denom = jnp.sum(e, axis=0, keepdims=True) + jnp.exp(sink_h - m)
                probs.append((e * (1.0 / denom)).astype(BF16))
                lse_rows.append(m + jnp.log(denom))
            o_all = jnp.dot(vb[:, gcols].T, jnp.concatenate(probs, axis=1), preferred_element_type=F32)
            for hh in range(SWA_GROUP):
                h = grp * SWA_GROUP + hh
                cols = slice(h * LANES, (h + 1) * LANES)
                o = o_all[:, hh * BLOCK:(hh + 1) * BLOCK].T
                oraw_ref[:, cols] = o
                g = _head_gate(gate_ref, h)
                og_ref[:, cols] = (o * (g * jax.nn.sigmoid(g))).astype(BF16)
        lse_ref[...] = _rows_to_lanes(lse_rows)

    own = lambda b, n: b * nb + n
    in_specs = _swa_specs(nb) + [
        pl.BlockSpec((BLOCK, GATE_W), lambda b, n: (own(b, n), 0)),
        pl.BlockSpec((1, LANES), lambda b, n: (0, 0)),
    ]
    out_specs = [pl.BlockSpec((BLOCK, HPAD), lambda b, n: (own(b, n), 0)),
                 pl.BlockSpec((BLOCK, HPAD), lambda b, n: (own(b, n), 0)),
                 pl.BlockSpec((BLOCK, LANES), lambda b, n: (own(b, n), 0))]
    out_shape = [jax.ShapeDtypeStruct((T, HPAD), F32), jax.ShapeDtypeStruct((T, HPAD), BF16),
                 jax.ShapeDtypeStruct((T, LANES), F32)]
    return pl.pallas_call(kern, name="swa_fwd", grid=(B, nb), in_specs=in_specs, out_specs=out_specs,
                          out_shape=out_shape, compiler_params=_params(("parallel", "arbitrary")))(
        z, z, z, z, z, pos_row, pos_col, pos_col, gate, sink_row)


def _swa_bwd(z, pos_col, pos_row, sink_row, lse, do_raw, delta, dz, B, S):
    nb = S // BLOCK
    T = B * S

    def kern(q_ref, kp_ref, kc_ref, vp_ref, vc_ref, pq_ref, pkp_ref, pkc_ref, sink_ref, lse_ref, do_ref,
             delta_ref, dz_ref, dq_ref, dk_ref, dv_ref, dsink_ref):
        b, n = pl.program_id(0), pl.program_id(1)

        @pl.when(n == 0)
        def _():
            dk_ref[...] = jnp.zeros_like(dk_ref)
            dv_ref[...] = jnp.zeros_like(dv_ref)

        @pl.when((b == 0) & (n == 0))
        def _():
            dsink_ref[...] = jnp.zeros_like(dsink_ref)

        q_all = q_ref[...]
        kb = jnp.concatenate([kp_ref[...], kc_ref[...]], axis=0).astype(BF16)
        vb = jnp.concatenate([vp_ref[...], vc_ref[...]], axis=0).astype(BF16)
        dist, valid = _swa_band(n, pq_ref, pkp_ref, pkc_ref)
        lse_t, delta_t = lse_ref[...].T, delta_ref[...].T
        lane1 = lax.broadcasted_iota(jnp.int32, (1, LANES), 1)
        dsink = jnp.zeros((1, LANES), F32)
        dk_band, dv_band = [], []
        for grp in range(SWA_KV_HEADS):
            gcols = slice(grp * LANES, (grp + 1) * LANES)
            heads = range(grp * SWA_GROUP, (grp + 1) * SWA_GROUP)
            qg = _swa_group_q(q_all, grp)
            dog = jnp.concatenate([do_ref[:, h * LANES:(h + 1) * LANES] for h in heads], axis=0)
            s_all = _dot_nt(kb[:, gcols], qg)
            dp_all = _dot_nt(vb[:, gcols], dog)
            ps, dss = [], []
            for hh, h in enumerate(heads):
                blk = slice(hh * BLOCK, (hh + 1) * BLOCK)
                lse_h, delta_h = lse_t[h:h + 1, :], delta_t[h:h + 1, :]
                p = jnp.exp(_swa_mask(s_all[:, blk], dist, valid, h) - lse_h)
                ps.append(p.astype(BF16))
                dss.append((p * (dp_all[:, blk] - delta_h)).astype(BF16))
                psink = jnp.exp(sink_ref[0:1, h:h + 1] - lse_h)
                dsink = dsink + jnp.where(lane1 == h, -jnp.sum(psink * delta_h, axis=1, keepdims=True), 0.0)
            dsg = jnp.concatenate(dss, axis=1)
            dq_all = jnp.dot(kb[:, gcols].T, dsg, preferred_element_type=F32) * SWA_SCALE
            for hh, h in enumerate(heads):
                dq_ref[:, h * LANES:(h + 1) * LANES] = dq_all[:, hh * BLOCK:(hh + 1) * BLOCK].T.astype(BF16)
            dk_band.append(jnp.dot(dsg, qg, preferred_element_type=F32))
            dv_band.append(jnp.dot(jnp.concatenate(ps, axis=1), dog, preferred_element_type=F32))
        dsink_ref[...] += dsink
        dkb = jnp.concatenate(dk_band, axis=1)
        dvb = jnp.concatenate(dv_band, axis=1)
        r_prev = pl.ds(pl.multiple_of(jnp.maximum(n - 1, 0) * BLOCK, BLOCK), BLOCK)
        r_own = pl.ds(pl.multiple_of(n * BLOCK, BLOCK), BLOCK)
        dk_ref[r_prev, :] += dkb[:BLOCK]
        dk_ref[r_own, :] += dkb[BLOCK:]
        dv_ref[r_prev, :] += dvb[:BLOCK]
        dv_ref[r_own, :] += dvb[BLOCK:]

    own = lambda b, n: b * nb + n
    in_specs = _swa_specs(nb) + [
        pl.BlockSpec((1, LANES), lambda b, n: (0, 0)),
        pl.BlockSpec((BLOCK, LANES), lambda b, n: (own(b, n), 0)),
        pl.BlockSpec((BLOCK, HPAD), lambda b, n: (own(b, n), 0)),
        pl.BlockSpec((BLOCK, LANES), lambda b, n: (own(b, n), 0)),
        pl.BlockSpec(memory_space=pl.ANY),
    ]
    out_specs = [pl.BlockSpec((BLOCK, HPAD), lambda b, n: (own(b, n), Z_AQ // HPAD)),
                 pl.BlockSpec((S, 256), lambda b, n: (b, 0)),
                 pl.BlockSpec((S, 256), lambda b, n: (b, 0)),
                 pl.BlockSpec((1, LANES), lambda b, n: (0, 0))]
    out_shape = [jax.ShapeDtypeStruct(dz.shape, dz.dtype), jax.ShapeDtypeStruct((T, 256), F32),
                 jax.ShapeDtypeStruct((T, 256), F32), jax.ShapeDtypeStruct((1, LANES), F32)]
    return pl.pallas_call(kern, name="swa_bwd", grid=(B, nb), in_specs=in_specs, out_specs=out_specs,
                          out_shape=out_shape, input_output_aliases={len(in_specs) - 1: 0},
                          compiler_params=_params(("arbitrary", "arbitrary")))(
        z, z, z, z, z, pos_row, pos_col, pos_col, sink_row, lse, do_raw, delta, dz)


MLA_T = 256
MLA_HG = 4
MLA_W = MLA_HG * LANES
MLA_SCALE = MLA_QK ** -0.5
LOG2E = 1.4426950408889634
MLA_QSCALE = MLA_SCALE * LOG2E


def _causal_t(s):
    key = lax.broadcasted_iota(jnp.int32, s.shape, 0)
    query = lax.broadcasted_iota(jnp.int32, s.shape, 1)
    return jnp.where(key <= query, s, NEG)


def _mla_fwd(q, k, v, z, B, S):
    T = B * S
    nq = S // MLA_T

    def kern(q_ref, k_ref, v_ref, gate_ref, oraw_ref, og_ref, lse_ref):
        i = pl.program_id(2)

        def scores(j):
            rows = pl.ds(pl.multiple_of(j * MLA_T, MLA_T), MLA_T)
            return tuple(_dot_nt(k_ref[rows, hh * LANES:(hh + 1) * LANES], q_ref[:, hh * LANES:(hh + 1) * LANES])
                         for hh in range(MLA_HG))

        def update(j, ss, state):
            rows = pl.ds(pl.multiple_of(j * MLA_T, MLA_T), MLA_T)
            out = []
            for hh in range(MLA_HG):
                (m, l, acc), s = state[hh], ss[hh]
                m_new = jnp.maximum(m, jnp.max(s, axis=0, keepdims=True))
                alpha = jnp.exp2(m - m_new)
                p = jnp.exp2(s - m_new)
                l = alpha * l + jnp.sum(p, axis=0, keepdims=True)
                pv = jnp.dot(v_ref[rows, hh * LANES:(hh + 1) * LANES].T, p.astype(BF16), preferred_element_type=F32)
                out.append((m_new, l, alpha * acc + pv))
            return tuple(out)

        def body(j, carry):
            state, ss = carry
            s_next = scores(j + 1)
            return update(j, ss, state), s_next

        init = tuple((jnp.full((1, MLA_T), NEG, F32), jnp.zeros((1, MLA_T), F32), jnp.zeros((LANES, MLA_T), F32))
                     for _ in range(MLA_HG))
        state, ss = lax.fori_loop(0, i, body, (init, scores(0)))
        state = update(i, tuple(_causal_t(s) for s in ss), state)
        for hh in range(MLA_HG):
            m, l, acc = state[hh]
            cols = slice(hh * LANES, (hh + 1) * LANES)
            o = (acc * (1.0 / l)).T
            oraw_ref[:, cols] = o
            g = _head_gate(gate_ref, hh)
            og_ref[:, cols] = (o * (g * jax.nn.sigmoid(g))).astype(BF16)
            lse_ref[0, 0, 0, hh:hh + 1, :] = m + jnp.log2(l)

    blk = lambda b, h, i: (b * nq + i, h)
    in_specs = [pl.BlockSpec((MLA_T, MLA_W), blk),
                pl.BlockSpec((S, MLA_W), lambda b, h, i: (b, h)),
                pl.BlockSpec((S, MLA_W), lambda b, h, i: (b, h)),
                pl.BlockSpec((MLA_T, MLA_W // 2), lambda b, h, i: (b * nq + i, Z_BGATE // (MLA_W // 2) + h))]
    out_specs = [pl.BlockSpec((MLA_T, MLA_W), blk), pl.BlockSpec((MLA_T, MLA_W), blk),
                 pl.BlockSpec((1, 1, 1, MLA_HG, MLA_T), lambda b, h, i: (b, h, i, 0, 0))]
    out_shape = [jax.ShapeDtypeStruct((T, HPAD), F32), jax.ShapeDtypeStruct((T, HPAD), BF16),
                 jax.ShapeDtypeStruct((B, HEADS // MLA_HG, nq, MLA_HG, MLA_T), F32)]
    return pl.pallas_call(kern, name="mla_fwd", grid=(B, HEADS // MLA_HG, nq), in_specs=in_specs,
                          out_specs=out_specs, out_shape=out_shape,
                          compiler_params=_params(("parallel", "parallel", "arbitrary")))(q, k, v, z)


def _mla_bwd(q, k, v, do_raw, lse, delta, B, S):
    T = B * S
    nk = S // MLA_T

    def kern(q_ref, k_ref, v_ref, do_ref, lse_ref, delta_ref, dq_ref, dk_ref, dv_ref, dq_acc, dk_acc, dv_acc):
        j = pl.program_id(2)

        @pl.when(j == 0)
        def _():
            dq_acc[...] = jnp.zeros_like(dq_acc)

        dk_acc[...] = jnp.zeros_like(dk_acc)
        dv_acc[...] = jnp.zeros_like(dv_acc)
        kts = [k_ref[:, hh * LANES:(hh + 1) * LANES].T for hh in range(MLA_HG)]

        def step(i, masked):
            rows = pl.ds(pl.multiple_of(i * MLA_T, MLA_T), MLA_T)
            for hh in range(MLA_HG):
                cols = slice(hh * LANES, (hh + 1) * LANES)
                qv, do = q_ref[rows, cols], do_ref[rows, cols]
                st = _dot_nt(k_ref[:, cols], qv)
                if masked:
                    st = _causal_t(st)
                pt = jnp.exp2(st - lse_ref[0, 0, i, hh:hh + 1, :])
                dpt = _dot_nt(v_ref[:, cols], do)
                dst = (pt * (dpt - delta_ref[0, 0, i, hh:hh + 1, :])).astype(BF16)
                dv_acc[:, cols] += jnp.dot(pt.astype(BF16), do, preferred_element_type=F32)
                dk_acc[:, cols] += jnp.dot(dst, qv, preferred_element_type=F32)
                dq_acc[hh, i] += jnp.dot(kts[hh], dst, preferred_element_type=F32)

        step(j, True)

        def body(i, c):
            step(i, False)
            return c

        lax.fori_loop(j + 1, nk, body, 0)
        dk_ref[...] = dk_acc[...] * (1.0 / LOG2E)
        dv_ref[...] = dv_acc[...]

        @pl.when(j == nk - 1)
        def _():
            for hh in range(MLA_HG):
                for t in range(nk):
                    dq_ref[t * MLA_T:(t + 1) * MLA_T, hh * LANES:(hh + 1) * LANES] = dq_acc[hh, t].T

    whole = lambda b, h, j: (b, h)
    tile = lambda b, h, j: (b * nk + j, h)
    stats = pl.BlockSpec((1, 1, nk, MLA_HG, MLA_T), lambda b, h, j: (b, h, 0, 0, 0))
    in_specs = [pl.BlockSpec((S, MLA_W), whole), pl.BlockSpec((MLA_T, MLA_W), tile),
                pl.BlockSpec((MLA_T, MLA_W), tile), pl.BlockSpec((S, MLA_W), whole), stats, stats]
    out_specs = [pl.BlockSpec((S, MLA_W), whole), pl.BlockSpec((MLA_T, MLA_W), tile),
                 pl.BlockSpec((MLA_T, MLA_W), tile)]
    out_shape = [jax.ShapeDtypeStruct((T, HPAD), F32)] * 3
    scratch = [pltpu.VMEM((MLA_HG, nk, LANES, MLA_T), F32), pltpu.VMEM((MLA_T, MLA_W), F32),
               pltpu.VMEM((MLA_T, MLA_W), F32)]
    return pl.pallas_call(kern, name="mla_bwd", grid=(B, HEADS // MLA_HG, nk), in_specs=in_specs,
                          out_specs=out_specs, out_shape=out_shape, scratch_shapes=scratch,
                          compiler_params=_params(("parallel", "parallel", "arbitrary")))(
        q, k, v, do_raw, lse, delta)


def _rope_tables(pos_col, inv_lane, rows):
    def body(ins, outs, _):
        ang = ins[0][...].astype(F32) * ins[1][...]
        lane = lax.broadcasted_iota(jnp.int32, ang.shape, 1)
        cos, sin = jnp.cos(ang), jnp.sin(ang)
        first = (lane >= HEAD_DIM) & (lane < HEAD_DIM + MLA_ROPE // 2)
        second = (lane >= HEAD_DIM + MLA_ROPE // 2) & (lane < MLA_QK)
        outs[0][...] = jnp.where(lane < HEAD_DIM, 1.0, jnp.where(lane < MLA_QK, cos, 0.0))
        outs[1][...] = jnp.where(first, -sin, 0.0)
        outs[2][...] = jnp.where(second, sin, 0.0)
    return _ew("rope_tables", body, [(pos_col, 1, 0), (inv_lane, None, None)], [(LANES, F32)] * 3, rows)


def _rope(x, c, s1, s2):
    return x * c + pltpu.roll(x, 112, 1) * s1 + pltpu.roll(x, 16, 1) * s2


def _rope_t(d, c, s1, s2):
    return d * c + pltpu.roll(d * s1, 16, 1) + pltpu.roll(d * s2, 112, 1)


def _mla_prep(qdn, w_uq, kvdn, w_ukv, z, tabs, rows):
    def body(ins, outs, _):
        q_pre, kv_pre = ins[0], ins[1]
        c, s1, s2 = ins[3][...], ins[4][...], ins[5][...]
        kr = _rope(ins[2][...].astype(F32), c, s1, s2)
        for h in range(HEADS):
            cols = slice(h * LANES, (h + 1) * LANES)
            outs[0][:, cols] = (_rope(q_pre[:, cols], c, s1, s2) * MLA_QSCALE).astype(BF16)
            outs[1][:, cols] = (kv_pre[:, cols] + kr).astype(BF16)
        outs[2][...] = kv_pre[:, HPAD:].astype(BF16)
    ins = [(z, LANES, Z_BKR // LANES), (tabs[0], LANES, 0), (tabs[1], LANES, 0), (tabs[2], LANES, 0)]
    return _ew("mla_prep", body, ins, [(HPAD, BF16)] * 3, rows, mms=[(qdn, w_uq), (kvdn, w_ukv)])


def _mla_prep_bwd(dq, dk, dv, tabs, dz, rows):
    def body(ins, outs, _):
        c, s1, s2 = ins[3][...], ins[4][...], ins[5][...]
        lane = lax.broadcasted_iota(jnp.int32, c.shape, 1)
        dkr = jnp.zeros(c.shape, F32)
        for h in range(HEADS):
            cols = slice(h * LANES, (h + 1) * LANES)
            outs[0][:, cols] = _rope_t(ins[0][:, cols] * MLA_SCALE, c, s1, s2).astype(BF16)
            dkh = ins[1][:, cols]
            outs[1][:, cols] = jnp.where(lane < HEAD_DIM, dkh, 0.0).astype(BF16)
            dkr = dkr + dkh
        outs[1][:, HPAD:] = ins[2][...].astype(BF16)
        live = (lane >= HEAD_DIM) & (lane < MLA_QK)
        outs[2][...] = jnp.where(live, _rope_t(jnp.where(live, dkr, 0.0), c, s1, s2), 0.0).astype(BF16)
    ins = [(dq, HPAD, 0), (dk, HPAD, 0), (dv, HPAD, 0), (tabs[0], LANES, 0), (tabs[1], LANES, 0),
           (tabs[2], LANES, 0)]
    outs = [(HPAD, BF16), (2 * HPAD, BF16), (LANES, BF16, dz, Z_BKR // LANES)]
    return _ew("mla_prep_bwd", body, ins, outs, rows)


def _gate_bwd(name, d_o_mm, o_raw, gate, gate_cb, dz, dz_cb, rows):
    def body(ins, outs, _):
        lane = lax.broadcasted_iota(jnp.int32, outs[2].shape, 1)
        delta = jnp.zeros(outs[2].shape, F32)
        d_gate = [None] * HEADS
        for h in range(HEADS):
            cols = slice(h * LANES, (h + 1) * LANES)
            dog, o, g = ins[0][:, cols], ins[1][:, cols], _head_gate(ins[2], h)
            sg = jax.nn.sigmoid(g)
            do = dog * (g * sg)
            outs[0][:, cols] = do.astype(BF16)
            d_gate[h] = dog * o * (sg * (1.0 + g * (1.0 - sg)))
            delta = jnp.where(lane == h, jnp.sum(do * o, axis=-1, keepdims=True), delta)
        for pair in range(HEADS // 2):
            packed = d_gate[2 * pair] + pltpu.roll(d_gate[2 * pair + 1], HEAD_DIM, 1)
            outs[1][:, pair * LANES:(pair + 1) * LANES] = packed.astype(BF16)
        outs[2][...] = delta
    ins = [(o_raw, HPAD, 0), (gate, GATE_W, gate_cb)]
    outs = [(HPAD, BF16), (GATE_W, BF16, dz, dz_cb), (LANES, F32)]
    return _ew(name, body, ins, outs, rows, mms=[d_o_mm])


def _merge_out(ua, ub, z, w_out, x0, rows):
    tm = _row_tile(rows)

    def kern(ua_ref, ub_ref, ma_ref, mb_ref, w_ref, x0_ref, y_ref, x1_ref):
        ua_v, ub_v, m_a, m_b = (r[...].astype(F32) for r in (ua_ref, ub_ref, ma_ref, mb_ref))
        y = (jax.nn.sigmoid(m_a) * ua_v + jax.nn.sigmoid(m_b) * ub_v).astype(BF16)
        y_ref[...] = y
        for j in range(D_MODEL // MM_TN):
            cols = slice(j * MM_TN, (j + 1) * MM_TN)
            x1_ref[:, cols] = jnp.dot(y, w_ref[:, cols], preferred_element_type=F32) + x0_ref[:, cols]

    row = lambda cb: pl.BlockSpec((tm, D_MODEL), lambda i: (i, cb))
    return pl.pallas_call(
        kern, name="merge_out", grid=(rows // tm,),
        in_specs=[row(0), row(0), row(Z_MA // D_MODEL), row(Z_MB // D_MODEL),
                  pl.BlockSpec(w_out.shape, lambda i: (0, 0)), row(0)],
        out_specs=[row(0), row(0)],
        out_shape=[jax.ShapeDtypeStruct((rows, D_MODEL), BF16), jax.ShapeDtypeStruct((rows, D_MODEL), F32)],
        compiler_params=_params(("parallel",)))(ua, ub, z, z, w_out, x0)


def _merge_bwd(dy_mm, ua, ub, z, dz, rows):
    def body(ins, outs, _):
        dyv = ins[0][...]
        for idx in range(2):
            s = jax.nn.sigmoid(ins[3 + idx][...].astype(F32))
            outs[idx][...] = (dyv * s).astype(BF16)
            d_m = (dyv * ins[1 + idx][...].astype(F32) * (s * (1.0 - s))).astype(BF16)
            outs[2][:, idx * D_MODEL:(idx + 1) * D_MODEL] = d_m
    ins = [(ua, D_MODEL, 0), (ub, D_MODEL, 0), (z, D_MODEL, Z_MA // D_MODEL), (z, D_MODEL, Z_MB // D_MODEL)]
    outs = [(D_MODEL, BF16), (D_MODEL, BF16), (2 * D_MODEL, BF16, dz, Z_MA // (2 * D_MODEL))]
    return _ew("merge_bwd", body, ins, outs, rows, mms=[dy_mm])


def _kv_grad_cast(dk, dv, dz, rows):
    def body(ins, outs, _):
        outs[0][:, :256] = ins[0][...].astype(BF16)
        outs[0][:, 256:] = ins[1][...].astype(BF16)
    return _ew("kv_grad_cast", body, [(dk, 256, 0), (dv, 256, 0)], [(512, BF16, dz, Z_AK // 512)], rows)[0]


def _ple_fwd(x1, hn, w_pg, p, w_pp, rows):
    def body(ins, outs, _):
        u, e = ins[0][...], ins[1][...]
        outs[0][...] = ins[2][...] + jax.nn.sigmoid(u) * e
        outs[1][...] = u.astype(BF16)
        outs[2][...] = e.astype(BF16)
    return _ew("ple_fwd", body, [(x1, D_MODEL, 0)], [(D_MODEL, F32), (D_MODEL, BF16), (D_MODEL, BF16)], rows,
               mms=[(hn, w_pg), (p, w_pp)])


def _ple_bwd(dx2, u, e, rows):
    def body(ins, outs, _):
        d, s = ins[0][...], jax.nn.sigmoid(ins[1][...].astype(F32))
        outs[0][...] = (d * s).astype(BF16)
        outs[1][...] = (d * ins[2][...].astype(F32) * (s * (1.0 - s))).astype(BF16)
    return _ew("ple_bwd", body, [(dx2, D_MODEL, 0), (u, D_MODEL, 0), (e, D_MODEL, 0)],
               [(D_MODEL, BF16)] * 2, rows)


def _loss_head(x, g, target, rows):
    def body(ins, outs, accs):
        xv, gv = ins[0][...], ins[1][...]
        r = lax.rsqrt(jnp.mean(xv * xv, axis=-1, keepdims=True) + EPS)
        xhat = xv * r
        err = xhat * gv - ins[2][...]
        accs[0][...] += jnp.broadcast_to(0.5 * jnp.sum(jnp.mean(err * err, axis=-1, keepdims=True),
                                                       axis=0, keepdims=True), (1, LANES))
        dyv = err * (1.0 / D_MODEL)
        accs[1][...] += jnp.sum(dyv * xhat, axis=0, keepdims=True)
        dy = dyv * gv
        outs[0][...] = r * (dy - xhat * jnp.mean(dy * xhat, axis=-1, keepdims=True))
    ins = [(x, D_MODEL, 0), (g.reshape(1, D_MODEL), None, None), (target, D_MODEL, 0)]
    return _ew("loss_head", body, ins, [(D_MODEL, F32)], rows, accs=[(1, LANES), (1, D_MODEL)])


def _pad_heads_cols(w, n_heads, dim):
    k = w.shape[0]
    return jnp.pad(w.reshape(k, n_heads, dim), ((0, 0), (0, 0), (0, LANES - dim))).reshape(k, n_heads * LANES)


def _unpad_heads_cols(w, n_heads, dim):
    k = w.shape[0]
    return w.reshape(k, n_heads, LANES)[:, :, :dim].reshape(k, n_heads * dim)


def _layer_weights(w, i):
    segs = jnp.split(w['w_in'][i], list(_cumsum(IN_SIZES))[:-1], axis=1)
    a_q, a_k, a_v, a_gate, b_qd, b_kvd, b_kr, b_gate, m_a, m_b = segs
    kr = jnp.pad(b_kr, ((0, 0), (HEAD_DIM, LANES - MLA_QK)))
    w_in = jnp.concatenate([
        m_a, m_b, _pad_heads_cols(a_q, HEADS, HEAD_DIM), a_gate, b_gate, _pad_heads_cols(a_k, SWA_KV_HEADS, HEAD_DIM),
        _pad_heads_cols(a_v, SWA_KV_HEADS, HEAD_DIM), b_qd, b_kvd, kr], axis=1)
    w_uq = _pad_heads_cols(w['w_uq'][i], HEADS, MLA_QK)
    ukv = w['w_ukv'][i].reshape(MLA_KV_LORA, HEADS, 2 * HEAD_DIM)
    pad = ((0, 0), (0, 0), (0, HEAD_DIM))
    w_ukv = jnp.concatenate([jnp.pad(ukv[:, :, :HEAD_DIM], pad).reshape(MLA_KV_LORA, HPAD),
                             jnp.pad(ukv[:, :, HEAD_DIM:], pad).reshape(MLA_KV_LORA, HPAD)], axis=1)
    w_br_a = _pad_heads_cols(w['w_br_a'][i].T, HEADS, HEAD_DIM).T
    w_br_b = _pad_heads_cols(w['w_br_b'][i].T, HEADS, HEAD_DIM).T
    out = dict(w_in=w_in, w_uq=w_uq, w_ukv=w_ukv, w_br_a=w_br_a, w_br_b=w_br_b, w_out=w['w_out'][i],
               w_pg=w['w_ple_gate'][i], w_pp=w['w_ple_proj'][i])
    for name in ('w_in', 'w_uq', 'w_ukv', 'w_br_a', 'w_br_b', 'w_out', 'w_pg'):
        out[name + '_t'] = out[name].T
    return out


def _cumsum(sizes):
    acc, out = 0, []
    for s in sizes:
        acc += s
        out.append(acc)
    return out


def _unpad_grads(g):
    d = g['w_in']
    seg = lambda off, width: d[:, off:off + width]
    b_kr = seg(Z_BKR, LANES)[:, HEAD_DIM:MLA_QK]
    w_in = jnp.concatenate([
        _unpad_heads_cols(seg(Z_AQ, HPAD), HEADS, HEAD_DIM), _unpad_heads_cols(seg(Z_AK, 256), SWA_KV_HEADS, HEAD_DIM),
        _unpad_heads_cols(seg(Z_AV, 256), SWA_KV_HEADS, HEAD_DIM), seg(Z_AGATE, GATE_W),
        seg(Z_BQD, MLA_Q_LORA), seg(Z_BKVD, MLA_KV_LORA), b_kr, seg(Z_BGATE, GATE_W),
        seg(Z_MA, D_MODEL), seg(Z_MB, D_MODEL)], axis=1)
    w_uq = _unpad_heads_cols(g['w_uq'], HEADS, MLA_QK)
    ukv = g['w_ukv'].reshape(MLA_KV_LORA, 2, HEADS, LANES)[:, :, :, :HEAD_DIM]
    w_ukv = jnp.concatenate([ukv[:, 0], ukv[:, 1]], axis=-1).reshape(MLA_KV_LORA, HEADS * 2 * HEAD_DIM)
    w_br_a = _unpad_heads_cols(g['w_br_a'].T, HEADS, HEAD_DIM).T
    w_br_b = _unpad_heads_cols(g['w_br_b'].T, HEADS, HEAD_DIM).T
    return dict(w_in=w_in, w_uq=w_uq, w_ukv=w_ukv, w_br_a=w_br_a, w_br_b=w_br_b, w_out=g['w_out'],
                w_ple_gate=g['w_pg'], w_ple_proj=g['w_pp'], g_mix=g['g_mix'], sink=g['sink'], g_q=g['g_q'],
                g_kv=g['g_kv'], g_ple=g['g_ple'])


def _layer_fwd(x0, p_i, lw, sm, i, pos_col, pos_row, tabs, B, S):
    T = B * S
    h = _rms_fwd("norm_mix", x0, D_MODEL, 0, sm['g_mix'][i], T)
    z, a_gate = _mm("proj_in", h, lw['w_in'], BF16, f32_cols=(Z_AGATE, GATE_W))
    sink_row = jnp.pad(sm['sink'][i], (0, LANES - HEADS)).reshape(1, LANES)
    oa_raw, oa, lse_a = _swa_fwd(z, a_gate, pos_col, pos_row, sink_row, B, S)
    qdn = _rms_fwd("norm_q", z, MLA_Q_LORA, Z_BQD // MLA_Q_LORA, sm['g_q'][i], T)
    kvdn = _rms_fwd("norm_kv", z, MLA_KV_LORA, Z_BKVD // MLA_KV_LORA, sm['g_kv'][i], T)
    qf, kf, vf = _mla_prep(qdn, lw['w_uq'], kvdn, lw['w_ukv'], z, tabs, T)
    ob_raw, ob, lse_b = _mla_fwd(qf, kf, vf, z, B, S)
    ua = _mm("proj_br_a", oa, lw['w_br_a'], BF16)
    ub = _mm("proj_br_b", ob, lw['w_br_b'], BF16)
    y, x1 = _merge_out(ua, ub, z, lw['w_out'], x0, T)
    hn = _rms_fwd("norm_ple", x1, D_MODEL, 0, sm['g_ple'][i], T)
    x2, u, e = _ple_fwd(x1, hn, lw['w_pg'], p_i, lw['w_pp'], T)
    saved = dict(x0=x0, h=h, z=z, a_gate=a_gate, sink_row=sink_row, oa_raw=oa_raw, oa=oa, lse_a=lse_a, qdn=qdn, kvdn=kvdn,
                 qf=qf, kf=kf, vf=vf, ob_raw=ob_raw, ob=ob, lse_b=lse_b, ua=ua, ub=ub, y=y, x1=x1, hn=hn,
                 u=u, e=e, p=p_i)
    return x2, saved


def _layer_bwd(dx2, sv, lw, sm, i, pos_col, pos_row, tabs, B, S):
    T = B * S
    z = sv['z']
    g = {}
    d_e, d_u = _ple_bwd(dx2, sv['u'], sv['e'], T)
    g['w_pp'] = _mm_tn("grad_pp", sv['p'], d_e)
    g['w_pg'] = _mm_tn("grad_pg", sv['hn'], d_u)
    dx1, g['g_ple'] = _rms_bwd("norm_ple_bwd", sv['x1'], D_MODEL, 0, sm['g_ple'][i], (d_u, lw['w_pg_t']), T, F32,
                               dres=dx2)
    g['w_out'] = _mm_tn("grad_out", sv['y'], dx1)
    dz = lax.empty((T, Z_WIDTH), BF16)
    d_ua, d_ub, dz = _merge_bwd((dx1, lw['w_out_t']), sv['ua'], sv['ub'], z, dz, T)
    g['w_br_a'] = _mm_tn("grad_br_a", sv['oa'], d_ua)
    g['w_br_b'] = _mm_tn("grad_br_b", sv['ob'], d_ub)
    dob_raw, dz, delta_b = _gate_bwd("gate_b_bwd", (d_ub, lw['w_br_b_t']), sv['ob_raw'], z, Z_BGATE // GATE_W,
                                     dz, Z_BGATE // GATE_W, T)
    delta_rows = delta_b[:, :HEADS].reshape(B, S // MLA_T, MLA_T, HEADS // MLA_HG, MLA_HG).transpose(0, 3, 1, 4, 2)
    dq, dk, dv = _mla_bwd(sv['qf'], sv['kf'], sv['vf'], dob_raw, sv['lse_b'], delta_rows, B, S)
    dq_pre, dkv_pre, dz = _mla_prep_bwd(dq, dk, dv, tabs, dz, T)
    g['w_uq'] = _mm_tn("grad_uq", sv['qdn'], dq_pre)
    g['w_ukv'] = _mm_tn("grad_ukv", sv['kvdn'], dkv_pre)
    dz, g['g_q'] = _rms_bwd("norm_q_bwd", z, MLA_Q_LORA, Z_BQD // MLA_Q_LORA, sm['g_q'][i],
                            (dq_pre, lw['w_uq_t']), T, BF16, into=(dz, Z_BQD // MLA_Q_LORA))
    dz, g['g_kv'] = _rms_bwd("norm_kv_bwd", z, MLA_KV_LORA, Z_BKVD // MLA_KV_LORA, sm['g_kv'][i],
                             (dkv_pre, lw['w_ukv_t']), T, BF16, into=(dz, Z_BKVD // MLA_KV_LORA))
    doa_raw, dz, delta_a = _gate_bwd("gate_a_bwd", (d_ua, lw['w_br_a_t']), sv['oa_raw'], sv['a_gate'], 0,
                                     dz, Z_AGATE // GATE_W, T)
    dz, d_ak, d_av, dsink = _swa_bwd(z, pos_col, pos_row, sv['sink_row'], sv['lse_a'], doa_raw, delta_a, dz, B, S)
    dz = _kv_grad_cast(d_ak, d_av, dz, T)
    g['sink'] = dsink[0, :HEADS]
    g['w_in'] = _mm_tn("grad_in", sv['h'], dz, tn=Z_WIDTH // 2)
    dx0, g['g_mix'] = _rms_bwd("norm_mix_bwd", sv['x0'], D_MODEL, 0, sm['g_mix'][i], (dz, lw['w_in_t']), T, F32,
                               dres=dx1)
    for name in ('g_ple', 'g_q', 'g_kv', 'g_mix'):
        g[name] = g[name][0]
    return dx0, g


def _local_step(x, p, positions, wfull, sm, loss_target):
    B, S, _ = x.shape
    T = B * S
    pos_col = positions.reshape(T, 1)
    pos_row = positions.reshape(T // BLOCK, 1, BLOCK)
    half = MLA_ROPE // 2
    inv = ROPE_THETA ** (-jnp.arange(0, MLA_ROPE, 2, dtype=F32) / MLA_ROPE)
    inv_lane = jnp.tile(inv, LANES // half).reshape(1, LANES)
    tabs = _rope_tables(pos_col, inv_lane, T)
    xc = x.reshape(T, D_MODEL)
    lws, saved = [], []
    for i in range(DEPTH):
        lw = _layer_weights(wfull, i)
        xc, sv = _layer_fwd(xc, p[i].reshape(T, PLE_DIM), lw, sm, i, pos_col, pos_row, tabs, B, S)
        lws.append(lw)
        saved.append(sv)
    dx, loss, dg_final = _loss_head(xc, sm['g_final'], loss_target.reshape(T, D_MODEL), T)
    layer_grads = [None] * DEPTH
    for i in reversed(range(DEPTH)):
        dx, g = _layer_bwd(dx, saved[i], lws[i], sm, i, pos_col, pos_row, tabs, B, S)
        layer_grads[i] = _unpad_grads(g)
    return loss, dx.reshape(B, S, D_MODEL), layer_grads, dg_final[0]


SMALL_ROWS = 48


def _pack_small(arrs):
    flat = jnp.concatenate([arrs[name].reshape(-1) for name in SMALL])
    return jnp.pad(flat, (0, SMALL_ROWS * LANES - flat.shape[0])).reshape(SMALL_ROWS, LANES)


def _unpack_small(block, shapes):
    flat = block.reshape(-1)
    out, off = {}, 0
    for name in SMALL:
        n = math.prod(shapes[name])
        out[name] = flat[off:off + n].reshape(shapes[name])
        off += n
    return out


def _flipped(shard_shape):
    return shard_shape[-1] % LANES != 0


def _to_slots(g, axis):
    r, c = g.shape
    if axis == 0:
        return g.reshape(N_CHIPS, r // N_CHIPS, c)
    return g.reshape(r, N_CHIPS, c // N_CHIPS).transpose(1, 0, 2)


def _div_tile(rows, cap):
    return next(t for t in range(min(cap, rows) // 8 * 8, 0, -8) if rows % t == 0)


def _units(shapes):
    units = []
    for w, shape in enumerate(shapes):
        r = shape[-2]
        n = next(n for n in (8, 7, 4, 2, 1) if r % (8 * n) == 0) if r >= 1024 else 1
        units += [(w, k * (r // n), r // n) for k in range(n)]
    return units


def _place():
    x, y, c = lax.axis_index("x"), lax.axis_index("y"), lax.axis_index("c")
    chips = [(1 - x, y), (x, 1 - y), (1 - x, 1 - y)]
    return x, y, c, chips


ANY = pl.BlockSpec(memory_space=pl.ANY)


def _remote(send_sems, recv_sems, k, src, dst, to):
    return pltpu.make_async_remote_copy(src_ref=src, dst_ref=dst, send_sem=send_sems.at[k],
                                        recv_sem=recv_sems.at[k], device_id=to, device_id_type=MESH)


def _gather_weights(shards):
    n = len(shards)
    units = _units([s.shape for s in shards])
    nu = len(units)

    def body(*refs):
        ins, outs = refs[:n], refs[n:2 * n]
        send_sems, recv_sems, local_sems = refs[2 * n:]
        x, y, c, chips = _place()
        me = 2 * x + y
        sibling = (x, y, 1 - c)
        copy = functools.partial(_remote, send_sems, recv_sems)
        keeps, sends = [], []
        for u, (w, r0, nr) in enumerate(units):
            rows = pl.ds(r0, nr)
            keeps.append(pltpu.make_async_copy(ins[w].at[:, rows, :], outs[w].at[me, :, rows, :], local_sems.at[u]))
            keeps[-1].start()
        for j, (cx, cy) in enumerate(chips):
            for u, (w, r0, nr) in enumerate(units):
                rows = pl.ds(r0, nr)
                sends.append(copy(j * nu + u, ins[w].at[c, rows, :], outs[w].at[me, c, rows, :], (cx, cy, c)))
                sends[-1].start()
        for j, (cx, cy) in enumerate(chips):
            for u, (w, r0, nr) in enumerate(units):
                landed = outs[w].at[2 * cx + cy, c, pl.ds(r0, nr), :]
                copy(j * nu + u, landed, landed, (cx, cy, c)).wait_recv()
                sends.append(copy((3 + j) * nu + u, landed, landed, sibling))
                sends[-1].start()
        for j, (cx, cy) in enumerate(chips):
            for u, (w, r0, nr) in enumerate(units):
                other = outs[w].at[2 * cx + cy, 1 - c, pl.ds(r0, nr), :]
                copy((3 + j) * nu + u, other, other, sibling).wait_recv()
        for cp in sends:
            cp.wait_send()
        for keep in keeps:
            keep.wait()

    return pl.pallas_call(
        body, name="gather_weights",
        out_shape=[jax.ShapeDtypeStruct((N_CHIPS,) + s.shape, s.dtype) for s in shards],
        in_specs=[ANY] * n, out_specs=[ANY] * n,
        scratch_shapes=[pltpu.SemaphoreType.DMA((6 * nu,)), pltpu.SemaphoreType.DMA((6 * nu,)),
                        pltpu.SemaphoreType.DMA((nu,))])(*shards)


def _pair_exchange(g0, g1):
    n = len(g0)

    def body(*refs):
        layers, outs = (refs[:n], refs[n:2 * n]), refs[2 * n:3 * n]
        send_sems, recv_sems = refs[3 * n:]
        x, y, c, _ = _place()
        copy = functools.partial(_remote, send_sems, recv_sems)
        for w in range(n):
            for q in range(N_CHIPS):
                for layer in range(DEPTH):
                    cp = copy(N_CHIPS * w + q, layers[layer][w].at[q], outs[w].at[q], (x, y, 1 - c))
                    pl.when(c == 1 - layer)(cp.start)
        for w in range(n):
            for q in range(N_CHIPS):
                copy(N_CHIPS * w + q, layers[0][w].at[q], outs[w].at[q], (x, y, 1 - c)).wait()

    return pl.pallas_call(
        body, name="pair_exchange", out_shape=[jax.ShapeDtypeStruct(g.shape, g.dtype) for g in g0],
        in_specs=[ANY] * (2 * n), out_specs=[ANY] * n,
        scratch_shapes=[pltpu.SemaphoreType.DMA((N_CHIPS * n,)), pltpu.SemaphoreType.DMA((N_CHIPS * n,))])(*g0, *g1)


def _pair_sum(name, g0, g1, theirs, cflag):
    shape = theirs.shape
    rows, width = shape[0] * shape[1], shape[2]

    def body(ins, outs, _):
        mine = jnp.where(ins[3][0:1, 0:1] == 0.0, ins[0][...], ins[1][...])
        tot = mine + ins[2][...]
        outs[0][...] = tot
        outs[1][...] = tot.astype(BF16)
    ins = [(a.reshape(rows, width), width, 0) for a in (g0, g1, theirs)] + [(cflag, None, None)]
    f32, bf16 = _ew(name, body, ins, [(width, F32), (width, BF16)], rows, tm=_div_tile(rows, ROW_TILE))
    return f32.reshape(shape), bf16.reshape(shape)


def _chip_exchange(parts):
    n = len(parts)

    def body(*refs):
        ins, outs = refs[:n], refs[n:2 * n]
        send_sems, recv_sems = refs[2 * n:]
        x, y, c, chips = _place()
        copy = functools.partial(_remote, send_sems, recv_sems)
        sends = []
        for j, (cx, cy) in enumerate(chips):
            for w in range(n):
                sends.append(copy(j * n + w, ins[w].at[2 * cx + cy], outs[w].at[j], (cx, cy, c)))
                sends[-1].start()
        for j, (cx, cy) in enumerate(chips):
            for w in range(n):
                copy(j * n + w, outs[w].at[j], outs[w].at[j], (cx, cy, c)).wait_recv()
        for cp in sends:
            cp.wait_send()

    return pl.pallas_call(
        body, name="chip_exchange",
        out_shape=[jax.ShapeDtypeStruct((3,) + a.shape[1:], a.dtype) for a in parts],
        in_specs=[ANY] * n, out_specs=[ANY] * n,
        scratch_shapes=[pltpu.SemaphoreType.DMA((3 * n,)), pltpu.SemaphoreType.DMA((3 * n,))])(*parts)


def _chip_sum(name, part, landed, chipflag):
    _, r, width = part.shape
    tm = _div_tile(r, ROW_TILE // 2)

    def kern(p_ref, l_ref, flag_ref, o_ref):
        me = flag_ref[0:1, 0:1]
        own = jnp.where(me == 0.0, p_ref[0], jnp.where(me == 1.0, p_ref[1], jnp.where(me == 2.0, p_ref[2], p_ref[3])))
        o_ref[...] = ((own + l_ref[0].astype(F32)) + l_ref[1].astype(F32)) + l_ref[2].astype(F32)

    return pl.pallas_call(
        kern, name=name, grid=(r // tm,),
        in_specs=[pl.BlockSpec((N_CHIPS, tm, width), lambda i: (0, i, 0)),
                  pl.BlockSpec((3, tm, width), lambda i: (0, i, 0)),
                  pl.BlockSpec((1, LANES), lambda i: (0, 0))],
        out_specs=pl.BlockSpec((tm, width), lambda i: (i, 0)),
        out_shape=jax.ShapeDtypeStruct((r, width), F32), compiler_params=_params(("arbitrary",)))(part, landed, chipflag)


def _pair_broadcast(mine):
    n = len(mine)
    units = _units([a.shape for a in mine])

    def body(*refs):
        ins, outs = refs[:n], refs[n:2 * n]
        send_sems, recv_sems = refs[2 * n:]
        x, y, c, _ = _place()
        copy = functools.partial(_remote, send_sems, recv_sems)
        cps = [copy(u, ins[w].at[pl.ds(r0, nr), :], outs[w].at[pl.ds(r0, nr), :], (x, y, 1 - c))
               for u, (w, r0, nr) in enumerate(units)]
        for cp in cps:
            cp.start()
        for cp in cps:
            cp.wait()

    return pl.pallas_call(
        body, name="pair_broadcast", out_shape=[jax.ShapeDtypeStruct(a.shape, a.dtype) for a in mine],
        in_specs=[ANY] * n, out_specs=[ANY] * n,
        scratch_shapes=[pltpu.SemaphoreType.DMA((len(units),)), pltpu.SemaphoreType.DMA((len(units),))])(*mine)


def _small_allreduce(v):
    offsets = [(dx, dy, dc) for dx in (0, 1) for dy in (0, 1) for dc in (0, 1)][1:]

    def body(v_ref, out_ref, recv_ref, send_sems, recv_sems):
        x, y, c, _ = _place()
        flip = lambda a, d: 1 - a if d else a
        peers = [(flip(x, dx), flip(y, dy), flip(c, dc)) for dx, dy, dc in offsets]
        copy = functools.partial(_remote, send_sems, recv_sems)
        me = 4 * x + 2 * y + c
        recv_ref[me] = v_ref[...]
        cps = [copy(k, v_ref, recv_ref.at[me], peer) for k, peer in enumerate(peers)]
        for cp in cps:
            cp.start()
        for k, (px, py, pc) in enumerate(peers):
            landed = recv_ref.at[4 * px + 2 * py + pc]
            copy(k, landed, landed, (px, py, pc)).wait_recv()
        for cp in cps:
            cp.wait_send()
        tot = recv_ref[0]
        for d in range(1, 8):
            tot = tot + recv_ref[d]
        out_ref[...] = tot

    vmem = pl.BlockSpec(memory_space=pltpu.VMEM)
    return pl.pallas_call(
        body, name="small_allreduce", out_shape=jax.ShapeDtypeStruct(v.shape, v.dtype),
        in_specs=[vmem], out_specs=vmem,
        scratch_shapes=[pltpu.VMEM((8,) + v.shape, v.dtype), pltpu.SemaphoreType.DMA((7,)),
                        pltpu.SemaphoreType.DMA((7,))])(v)


def _adam_math(gv, wv, mv, vv):
    mv = ADAM_B1 * mv + (1.0 - ADAM_B1) * gv
    vv = ADAM_B2 * vv + (1.0 - ADAM_B2) * (gv * gv)
    m_hat = mv / (1.0 - ADAM_B1 ** ADAM_STEP)
    v_hat = vv / (1.0 - ADAM_B2 ** ADAM_STEP)
    return -ADAM_LR * (m_hat / (jnp.sqrt(v_hat) + ADAM_EPS) + ADAM_WD * wv), mv, vv


def _adamw_big(name, mine, theirs, cflag, w, m, v):
    _, r, width = w.shape
    tm = _div_tile(r, ROW_TILE // 2)

    def kern(mine_ref, theirs_ref, flag_ref, w_ref, m_ref, v_ref, g_ref, d_ref, nm_ref, nv_ref):
        layer = pl.program_id(0).astype(F32)
        gv = jnp.where(flag_ref[0:1, 0:1] == layer, mine_ref[...], theirs_ref[...])
        g_ref[0] = gv
        d_ref[0], nm_ref[0], nv_ref[0] = _adam_math(gv, w_ref[0], m_ref[0], v_ref[0])

    flat = pl.BlockSpec((tm, width), lambda l, i: (i, 0))
    stacked = pl.BlockSpec((1, tm, width), lambda l, i: (l, i, 0))
    return pl.pallas_call(
        kern, name=name, grid=(DEPTH, r // tm),
        in_specs=[flat, flat, pl.BlockSpec((1, LANES), lambda l, i: (0, 0)), stacked, stacked, stacked],
        out_specs=[stacked] * 4, out_shape=[jax.ShapeDtypeStruct(w.shape, F32)] * 4,
        compiler_params=_params(("arbitrary", "arbitrary")))(mine, theirs, cflag, w, m, v)


def _adamw_small(g, w, m, v):
    def body(ins, outs, _):
        outs[0][...], outs[1][...], outs[2][...] = _adam_math(*(r[...] for r in ins))
    return _ew("adamw_small", body, [(a, LANES, 0) for a in (g, w, m, v)], [(LANES, F32)] * 3, SMALL_ROWS)


def kernel(x, p, positions, g_mix, w_in, sink, g_q, w_uq, g_kv, w_ukv, w_br_a, w_br_b, w_out, g_ple, w_ple_gate, w_ple_proj, g_final, loss_target, m_g_mix, m_w_in, m_sink, m_g_q, m_w_uq, m_g_kv, m_w_ukv, m_w_br_a, m_w_br_b, m_w_out, m_g_ple, m_w_ple_gate, m_w_ple_proj, m_g_final, v_g_mix, v_w_in, v_sink, v_g_q, v_w_uq, v_g_kv, v_w_ukv, v_w_br_a, v_w_br_b, v_w_out, v_g_ple, v_w_ple_gate, v_w_ple_proj, v_g_final):
    w = dict(g_mix=g_mix, w_in=w_in, sink=sink, g_q=g_q, w_uq=w_uq, g_kv=g_kv, w_ukv=w_ukv, w_br_a=w_br_a,
             w_br_b=w_br_b, w_out=w_out, g_ple=g_ple, w_ple_gate=w_ple_gate, w_ple_proj=w_ple_proj, g_final=g_final)
    m = dict(g_mix=m_g_mix, w_in=m_w_in, sink=m_sink, g_q=m_g_q, w_uq=m_w_uq, g_kv=m_g_kv, w_ukv=m_w_ukv,
             w_br_a=m_w_br_a, w_br_b=m_w_br_b, w_out=m_w_out, g_ple=m_g_ple, w_ple_gate=m_w_ple_gate,
             w_ple_proj=m_w_ple_proj, g_final=m_g_final)
    v = dict(g_mix=v_g_mix, w_in=v_w_in, sink=v_sink, g_q=v_g_q, w_uq=v_w_uq, g_kv=v_g_kv, w_ukv=v_w_ukv,
             w_br_a=v_w_br_a, w_br_b=v_w_br_b, w_out=v_w_out, g_ple=v_g_ple, w_ple_gate=v_w_ple_gate,
             w_ple_proj=v_w_ple_proj, g_final=v_g_final)
    wfull = _gather_full(w)
    sm = {name: w[name] for name in SMALL}
    loss_row, grad_x, layer_grads, dg_final = _local_step(x, p, positions, wfull, sm, loss_target)
    loss = lax.psum(loss_row[0, 0], ("x", "y", "c"))
    res = _update(layer_grads, dg_final, w, m, v)
    return (loss, grad_x, *[res[name][kind] for kind in range(4) for name in WEIGHT_NAMES])


def _gather_behind(shards):
    n = len(shards)
    srcs = [jax.new_ref(s, memory_space=pltpu.MemorySpace.HBM) for s in shards]
    lands = [jax.empty_ref(jax.ShapeDtypeStruct((3,) + s.shape, s.dtype), memory_space=pltpu.MemorySpace.HBM)
             for s in shards]

    @pl.kernel(mesh=plsc.ScalarSubcoreMesh(axis_name="sequencer", num_cores=1), name="gather_behind",
               scratch_types=(pltpu.SemaphoreType.DMA((3 * n,)), pltpu.SemaphoreType.DMA((3 * n,))),
               compiler_params=pltpu.CompilerParams(collective_id=0))
    def launch(send_sems, recv_sems):
        x, y, c, chips = _place()
        barrier = pltpu.get_barrier_semaphore()
        for cx, cy in chips:
            pl.semaphore_signal(barrier, inc=1, device_id=(cx, cy, c), device_id_type=MESH)
        pl.semaphore_wait(barrier, len(chips))
        copy = functools.partial(_remote, send_sems, recv_sems)
        cps = [copy(j * n + w, srcs[w], lands[w].at[j], (cx, cy, c))
               for j, (cx, cy) in enumerate(chips) for w in range(n)]
        for cp in cps:
            cp.start()
        for cp in cps:
            cp.wait()

    launch()
    return [land[...] for land in lands]


def _gather_full(w):
    x, y = lax.axis_index("x"), lax.axis_index("y")
    shards = [w[name].astype(BF16) for name, _ in SHARDED]
    first = _gather_weights([s[0].reshape((2, s.shape[1] // 2) + s.shape[2:]) for s in shards])
    after = first[0][0, 0, 0, 0] * 0
    landed = _gather_behind([s[1] + after for s in shards])
    full = {}
    for k, (name, axis) in enumerate(SHARDED):
        layer0 = first[k].reshape((N_CHIPS,) + shards[k].shape[1:])
        layer1 = [lax.select_n((q // 2 != x) + 2 * (q % 2 != y), shards[k][1], *landed[k]) for q in range(N_CHIPS)]
        full[name] = [jnp.concatenate(list(blocks), axis=axis - 1) for blocks in (layer0, layer1)]
    return full


def _update(layer_grads, dg_final, w, m, v):
    small_shapes = {name: w[name].shape for name in SMALL}
    cflag = jnp.full((1, LANES), lax.axis_index("c"), F32)
    chipflag = jnp.full((1, LANES), 2 * lax.axis_index("x") + lax.axis_index("y"), F32)

    slots = [[_to_slots(layer_grads[layer][name], axis - 1) for name, axis in SHARDED] for layer in range(DEPTH)]
    theirs = _pair_exchange(slots[0], slots[1])
    pair = [_pair_sum("pair_sum_" + name, slots[0][k], slots[1][k], theirs[k], cflag)
            for k, (name, _) in enumerate(SHARDED)]
    landed = _chip_exchange([bf16 for _, bf16 in pair])
    mine = [_chip_sum("chip_sum_" + name, pair[k][0], landed[k], chipflag) for k, (name, _) in enumerate(SHARDED)]
    other = _pair_broadcast(mine)
    res = {}
    for k, (name, _) in enumerate(SHARDED):
        flip = _flipped(w[name].shape)
        view = (lambda a: jnp.swapaxes(a, -1, -2)) if flip else (lambda a: a)
        outs = _adamw_big("adamw_" + name, view(mine[k]), view(other[k]), cflag, view(w[name]), view(m[name]),
                          view(v[name]))
        res[name] = tuple(view(a) for a in outs)

    gsmall = {name: jnp.stack([layer_grads[layer][name] for layer in range(DEPTH)]) for name in SMALL[:-1]}
    gsmall['g_final'] = dg_final
    gsum = _small_allreduce(_pack_small(gsmall))
    small = (gsum,) + tuple(_adamw_small(gsum, _pack_small(w), _pack_small(m), _pack_small(v)))
    for name, arrs in zip(SMALL, zip(*[[_unpack_small(a, small_shapes)[n] for n in SMALL] for a in small])):
        res[name] = arrs
    return res
```

```python
import functools
import math

import jax
import jax.numpy as jnp
from jax import lax
from jax.experimental import pallas as pl
from jax.experimental.pallas import tpu as pltpu
from jax.experimental.pallas import tpu_sc as plsc

F32 = jnp.float32
BF16 = jnp.bfloat16

D_MODEL = 1024
DEPTH = 2
PLE_DIM = 256
BLOCK = 128
EPS = 1e-6
NEG = -1e30
HEADS = 8
SWA_KV_HEADS = 2
HEAD_DIM = 64
LANES = 128
HPAD = HEADS * LANES
MLA_QK = 96
MLA_ROPE = 32
MLA_Q_LORA = 256
MLA_KV_LORA = 128
ROPE_THETA = 10000.0
IN_SIZES = (512, 128, 128, 512, 256, 128, 32, 512, 1024, 1024)

Z_MA, Z_MB, Z_AQ, Z_AGATE, Z_BGATE = 0, 1024, 2048, 3072, 3584
Z_AK, Z_AV, Z_BQD, Z_BKVD, Z_BKR = 4096, 4352, 4608, 4864, 4992
Z_WIDTH = 5120
GATE_W = HEADS * HEAD_DIM

ADAM_LR, ADAM_B1, ADAM_B2, ADAM_EPS, ADAM_WD, ADAM_STEP = 0.001, 0.9, 0.999, 1e-08, 0.01, 10

VMEM_LIMIT = 56 * 1024 * 1024
MESH = pl.DeviceIdType.MESH

WEIGHT_NAMES = ('g_mix', 'w_in', 'sink', 'g_q', 'w_uq', 'g_kv', 'w_ukv', 'w_br_a', 'w_br_b',
                'w_out', 'g_ple', 'w_ple_gate', 'w_ple_proj', 'g_final')
SHARDED = (('w_in', 2), ('w_uq', 2), ('w_ukv', 2), ('w_br_a', 2), ('w_br_b', 2),
           ('w_out', 1), ('w_ple_gate', 1), ('w_ple_proj', 2))
SMALL = ('g_mix', 'sink', 'g_q', 'g_kv', 'g_ple', 'g_final')
N_CHIPS = 4


def _params(sem):
    return pltpu.CompilerParams(dimension_semantics=sem, vmem_limit_bytes=VMEM_LIMIT)


MM_TN = 512
ROW_TILE = 512
BIG_WEIGHT_BYTES = 8 * 1024 * 1024


def _row_tile(rows, weight_bytes=0):
    tm = ROW_TILE // 2 if weight_bytes > BIG_WEIGHT_BYTES else ROW_TILE
    return min(tm, rows)


def _ew(name, body, ins, outs, rows, accs=(), mms=(), tm=None):
    n_mm, n_in, n_out = len(mms), len(ins), len(outs)
    if tm is None:
        tm = _row_tile(rows, sum(b.size * b.dtype.itemsize for _, b in mms))
    in_specs, args = [], []
    for a, b in mms:
        in_specs += [pl.BlockSpec((tm, a.shape[1]), lambda i: (i, 0)), pl.BlockSpec(b.shape, lambda i: (0, 0))]
        args += [a, b]
    for arr, width, cb in ins:
        if width is None:
            in_specs.append(pl.BlockSpec(arr.shape, lambda i, nd=arr.ndim: (0,) * nd))
        else:
            in_specs.append(pl.BlockSpec((tm, width), lambda i, cb=cb: (i, cb)))
        args.append(arr)
    out_shape, out_specs, aliases = [], [], {}
    for k, out in enumerate(outs):
        if len(out) == 4:
            aliases[len(args)] = k
            in_specs.append(pl.BlockSpec(memory_space=pl.ANY))
            args.append(out[2])
            out_shape.append(jax.ShapeDtypeStruct(out[2].shape, out[2].dtype))
            out_specs.append(pl.BlockSpec((tm, out[0]), lambda i, cb=out[3]: (i, cb)))
        else:
            out_shape.append(jax.ShapeDtypeStruct((rows, out[0]), out[1]))
            out_specs.append(pl.BlockSpec((tm, out[0]), lambda i: (i, 0)))
    n_in += len(aliases)
    out_shape += [jax.ShapeDtypeStruct(s, F32) for s in accs]
    out_specs += [pl.BlockSpec(s, lambda i: (0, 0)) for s in accs]

    def kern(*refs):
        mm_refs, refs = refs[:2 * n_mm], refs[2 * n_mm:]
        in_refs, out_refs = refs[:n_in - len(aliases)], refs[n_in:n_in + n_out]
        acc_refs, prod_refs = refs[n_in + n_out:n_in + n_out + len(accs)], refs[n_in + n_out + len(accs):]
        if acc_refs:
            @pl.when(pl.program_id(0) == 0)
            def _():
                for r in acc_refs:
                    r[...] = jnp.zeros_like(r)
        for k in range(n_mm):
            a_ref, b_ref, prod = mm_refs[2 * k], mm_refs[2 * k + 1], prod_refs[k]
            av = a_ref[...].astype(BF16)
            n = b_ref.shape[1]
            tn = min(MM_TN, n)
            for j in range(n // tn):
                cols = slice(j * tn, (j + 1) * tn)
                prod[:, cols] = jnp.dot(av, b_ref[:, cols], preferred_element_type=F32)
        body(tuple(prod_refs) + tuple(in_refs), out_refs, acc_refs)

    scratch = [pltpu.VMEM((tm, b.shape[1]), F32) for _, b in mms]
    res = pl.pallas_call(kern, name=name, grid=(rows // tm,), in_specs=in_specs, out_specs=out_specs,
                         out_shape=out_shape, scratch_shapes=scratch, input_output_aliases=aliases,
                         compiler_params=_params(("arbitrary",)))(*args)
    return res


def _rms_fwd(name, x, width, cb, g, rows):
    def body(ins, outs, _):
        xv = ins[0][...].astype(F32)
        r = lax.rsqrt(jnp.mean(xv * xv, axis=-1, keepdims=True) + EPS)
        outs[0][...] = ((xv * r) * ins[1][...]).astype(BF16)
    return _ew(name, body, [(x, width, cb), (g.reshape(1, width), None, None)], [(width, BF16)], rows)[0]


def _rms_bwd(name, x, width, cb, g, dh_mm, rows, out_dtype, dres=None, into=()):
    def body(ins, outs, accs):
        dhv, xv, gv = ins[0][...], ins[1][...].astype(F32), ins[2][...]
        r = lax.rsqrt(jnp.mean(xv * xv, axis=-1, keepdims=True) + EPS)
        xhat = xv * r
        accs[0][...] += jnp.sum(dhv * xhat, axis=0, keepdims=True)
        dy = dhv * gv
        dx = r * (dy - xhat * jnp.mean(dy * xhat, axis=-1, keepdims=True))
        if dres is not None:
            dx = dx + ins[3][...]
        outs[0][...] = dx.astype(out_dtype)
    ins = [(x, width, cb), (g.reshape(1, width), None, None)]
    if dres is not None:
        ins.append((dres, width, 0))
    return _ew(name, body, ins, [(width, out_dtype) + tuple(into)], rows, accs=[(1, width)], mms=[dh_mm])


def _mm(name, a, b, out_dtype, residual=None, f32_cols=None, tn=MM_TN):
    M, K = a.shape
    N = b.shape[1]
    tm, tn = _row_tile(M, b.size * b.dtype.itemsize), min(tn, N)
    has_res = residual is not None
    c0, cw = f32_cols if f32_cols else (0, 0)

    def kern(*refs):
        a_ref, b_ref = refs[0], refs[1]
        o_ref = refs[3] if has_res else refs[2]
        av = a_ref[...].astype(BF16)
        for j in range(N // tn):
            cols = slice(j * tn, (j + 1) * tn)
            part = jnp.dot(av, b_ref[:, cols], preferred_element_type=F32)
            if has_res:
                part = part + refs[2][:, cols]
            o_ref[:, cols] = part.astype(o_ref.dtype)
            if c0 <= j * tn and (j + 1) * tn <= c0 + cw:
                refs[-1][:, j * tn - c0:(j + 1) * tn - c0] = part

    in_specs = [pl.BlockSpec((tm, K), lambda i: (i, 0)), pl.BlockSpec((K, N), lambda i: (0, 0))]
    args = [a, b]
    if has_res:
        in_specs.append(pl.BlockSpec((tm, N), lambda i: (i, 0)))
        args.append(residual)
    out_specs = [pl.BlockSpec((tm, N), lambda i: (i, 0))]
    out_shape = [jax.ShapeDtypeStruct((M, N), out_dtype)]
    if f32_cols:
        assert c0 % tn == 0 and cw % tn == 0
        out_specs.append(pl.BlockSpec((tm, cw), lambda i: (i, 0)))
        out_shape.append(jax.ShapeDtypeStruct((M, cw), F32))
    res = pl.pallas_call(kern, name=name, grid=(M // tm,), in_specs=in_specs, out_specs=out_specs,
                         out_shape=out_shape, compiler_params=_params(("parallel",)))(*args)
    return res if f32_cols else res[0]


def _mm_tn(name, a, b, tk=512, tn=2048):
    T, M = a.shape
    N = b.shape[1]
    tn, tk = min(tn, N), min(tk, T)

    def kern(a_ref, b_ref, o_ref):
        k = pl.program_id(1)
        part = _dot_tn(a_ref[...].astype(BF16), b_ref[...].astype(BF16))

        @pl.when(k == 0)
        def _():
            o_ref[...] = part

        @pl.when(k > 0)
        def _():
            o_ref[...] += part

    return pl.pallas_call(
        kern, name=name, grid=(N // tn, T // tk),
        in_specs=[pl.BlockSpec((tk, M), lambda j, k: (k, 0)), pl.BlockSpec((tk, tn), lambda j, k: (k, j))],
        out_specs=pl.BlockSpec((M, tn), lambda j, k: (0, j)),
        out_shape=jax.ShapeDtypeStruct((M, N), F32),
        compiler_params=_params(("parallel", "arbitrary")))(a, b)


def _dot_nt(a, b):
    return lax.dot_general(a, b, (((1,), (1,)), ((), ())), preferred_element_type=F32)


def _dot_tn(a, b):
    return lax.dot_general(a, b, (((0,), (0,)), ((), ())), preferred_element_type=F32)


SWA_SCALE = HEAD_DIM ** -0.5


def _swa_band(n, pq_ref, pkp_ref, pkc_ref):
    posk = jnp.concatenate([pkp_ref[...], pkc_ref[...]], axis=0)
    dist = (pq_ref[0] - posk).astype(F32)
    kj = lax.broadcasted_iota(jnp.int32, (2 * BLOCK, BLOCK), 0)
    qi = lax.broadcasted_iota(jnp.int32, (2 * BLOCK, BLOCK), 1)
    t_abs = n * BLOCK + qi
    s_abs = n * BLOCK - BLOCK + kj
    return dist, (s_abs >= 0) & (s_abs <= t_abs) & (t_abs - s_abs < BLOCK)


SWA_GROUP = HEADS // SWA_KV_HEADS


def _head_gate(gate_ref, h):
    pair = gate_ref[:, (h // 2) * LANES:(h // 2 + 1) * LANES].astype(F32)
    return pair if h % 2 == 0 else pltpu.roll(pair, HEAD_DIM, 1)


def _swa_group_q(q_all, g):
    heads = range(g * SWA_GROUP, (g + 1) * SWA_GROUP)
    return jnp.concatenate([(q_all[:, h * LANES:(h + 1) * LANES] * SWA_SCALE).astype(BF16) for h in heads], axis=0)


def _swa_mask(s, dist, valid, h):
    return jnp.where(valid, s - (2.0 ** -(h + 1)) * dist, NEG)


def _rows_to_lanes(rows):
    block = jnp.concatenate(list(rows) + [jnp.zeros((LANES - len(rows), BLOCK), F32)], axis=0)
    return block.T


def _swa_specs(nb):
    prev = lambda b, n: b * nb + jnp.maximum(n - 1, 0)
    own = lambda b, n: b * nb + n
    return [
        pl.BlockSpec((BLOCK, HPAD), lambda b, n: (own(b, n), Z_AQ // HPAD)),
        pl.BlockSpec((BLOCK, 256), lambda b, n: (prev(b, n), Z_AK // 256)),
        pl.BlockSpec((BLOCK, 256), lambda b, n: (own(b, n), Z_AK // 256)),
        pl.BlockSpec((BLOCK, 256), lambda b, n: (prev(b, n), Z_AV // 256)),
        pl.BlockSpec((BLOCK, 256), lambda b, n: (own(b, n), Z_AV // 256)),
        pl.BlockSpec((1, 1, BLOCK), lambda b, n: (own(b, n), 0, 0)),
        pl.BlockSpec((BLOCK, 1), lambda b, n: (prev(b, n), 0)),
        pl.BlockSpec((BLOCK, 1), lambda b, n: (own(b, n), 0)),
    ]


def _swa_fwd(z, gate, pos_col, pos_row, sink_row, B, S):
    nb = S // BLOCK
    T = B * S

    def kern(q_ref, kp_ref, kc_ref, vp_ref, vc_ref, pq_ref, pkp_ref, pkc_ref, gate_ref, sink_ref,
             oraw_ref, og_ref, lse_ref):
        q_all = q_ref[...]
        kb = jnp.concatenate([kp_ref[...], kc_ref[...]], axis=0).astype(BF16)
        vb = jnp.concatenate([vp_ref[...], vc_ref[...]], axis=0).astype(BF16)
        dist, valid = _swa_band(pl.program_id(1), pq_ref, pkp_ref, pkc_ref)
        lse_rows = []
        for grp in range(SWA_KV_HEADS):
            gcols = slice(grp * LANES, (grp + 1) * LANES)
            s_all = _dot_nt(kb[:, gcols], _swa_group_q(q_all, grp))
            probs = []
            for hh in range(SWA_GROUP):
                h = grp * SWA_GROUP + hh
                s = _swa_mask(s_all[:, hh * BLOCK:(hh + 1) * BLOCK], dist, valid, h)
                sink_h = sink_ref[0:1, h:h + 1]
                m = jnp.maximum(jnp.max(s, axis=0, keepdims=True), sink_h)
                e = jnp.exp(s - m)
                denom = jnp.sum(e, axis=0, keepdims=True) + jnp.exp(sink_h - m)
                probs.append((e * (1.0 / denom)).astype(BF16))
                lse_rows.append(m + jnp.log(denom))
            o_all = jnp.dot(vb[:, gcols].T, jnp.concatenate(probs, axis=1), preferred_element_type=F32)
            for hh in range(SWA_GROUP):
                h = grp * SWA_GROUP + hh
                cols = slice(h * LANES, (h + 1) * LANES)
                o = o_all[:, hh * BLOCK:(hh + 1) * BLOCK].T
                oraw_ref[:, cols] = o
                g = _head_gate(gate_ref, h)
                og_ref[:, cols] = (o * (g * jax.nn.sigmoid(g))).astype(BF16)
        lse_ref[...] = _rows_to_lanes(lse_rows)

    own = lambda b, n: b * nb + n
    in_specs = _swa_specs(nb) + [
        pl.BlockSpec((BLOCK, GATE_W), lambda b, n: (own(b, n), 0)),
        pl.BlockSpec((1, LANES), lambda b, n: (0, 0)),
    ]
    out_specs = [pl.BlockSpec((BLOCK, HPAD), lambda b, n: (own(b, n), 0)),
                 pl.BlockSpec((BLOCK, HPAD), lambda b, n: (own(b, n), 0)),
                 pl.BlockSpec((BLOCK, LANES), lambda b, n: (own(b, n), 0))]
    out_shape = [jax.ShapeDtypeStruct((T, HPAD), F32), jax.ShapeDtypeStruct((T, HPAD), BF16),
                 jax.ShapeDtypeStruct((T, LANES), F32)]
    return pl.pallas_call(kern, name="swa_fwd", grid=(B, nb), in_specs=in_specs, out_specs=out_specs,
                          out_shape=out_shape, compiler_params=_params(("parallel", "arbitrary")))(
        z, z, z, z, z, pos_row, pos_col, pos_col, gate, sink_row)


def _swa_bwd(z, pos_col, pos_row, sink_row, lse, do_raw, delta, dz, B, S):
    nb = S // BLOCK
    T = B * S

    def kern(q_ref, kp_ref, kc_ref, vp_ref, vc_ref, pq_ref, pkp_ref, pkc_ref, sink_ref, lse_ref, do_ref,
             delta_ref, dz_ref, dq_ref, dk_ref, dv_ref, dsink_ref):
        b, n = pl.program_id(0), pl.program_id(1)

        @pl.when(n == 0)
        def _():
            dk_ref[...] = jnp.zeros_like(dk_ref)
            dv_ref[...] = jnp.zeros_like(dv_ref)

        @pl.when((b == 0) & (n == 0))
        def _():
            dsink_ref[...] = jnp.zeros_like(dsink_ref)

        q_all = q_ref[...]
        kb = jnp.concatenate([kp_ref[...], kc_ref[...]], axis=0).astype(BF16)
        vb = jnp.concatenate([vp_ref[...], vc_ref[...]], axis=0).astype(BF16)
        dist, valid = _swa_band(n, pq_ref, pkp_ref, pkc_ref)
        lse_t, delta_t = lse_ref[...].T, delta_ref[...].T
        lane1 = lax.broadcasted_iota(jnp.int32, (1, LANES), 1)
        dsink = jnp.zeros((1, LANES), F32)
        dk_band, dv_band = [], []
        for grp in range(SWA_KV_HEADS):
            gcols = slice(grp * LANES, (grp + 1) * LANES)
            heads = range(grp * SWA_GROUP, (grp + 1) * SWA_GROUP)
            qg = _swa_group_q(q_all, grp)
            dog = jnp.concatenate([do_ref[:, h * LANES:(h + 1) * LANES] for h in heads], axis=0)
            s_all = _dot_nt(kb[:, gcols], qg)
            dp_all = _dot_nt(vb[:, gcols], dog)
            ps, dss = [], []
            for hh, h in enumerate(heads):
                blk = slice(hh * BLOCK, (hh + 1) * BLOCK)
                lse_h, delta_h = lse_t[h:h + 1, :], delta_t[h:h + 1, :]
                p = jnp.exp(_swa_mask(s_all[:, blk], dist, valid, h) - lse_h)
                ps.append(p.astype(BF16))
                dss.append((p * (dp_all[:, blk] - delta_h)).astype(BF16))
                psink = jnp.exp(sink_ref[0:1, h:h + 1] - lse_h)
                dsink = dsink + jnp.where(lane1 == h, -jnp.sum(psink * delta_h, axis=1, keepdims=True), 0.0)
            dsg = jnp.concatenate(dss, axis=1)
            dq_all = jnp.dot(kb[:, gcols].T, dsg, preferred_element_type=F32) * SWA_SCALE
            for hh, h in enumerate(heads):
                dq_ref[:, h * LANES:(h + 1) * LANES] = dq_all[:, hh * BLOCK:(hh + 1) * BLOCK].T.astype(BF16)
            dk_band.append(jnp.dot(dsg, qg, preferred_element_type=F32))
            dv_band.append(jnp.dot(jnp.concatenate(ps, axis=1), dog, preferred_element_type=F32))
        dsink_ref[...] += dsink
        dkb = jnp.concatenate(dk_band, axis=1)
        dvb = jnp.concatenate(dv_band, axis=1)
        r_prev = pl.ds(pl.multiple_of(jnp.maximum(n - 1, 0) * BLOCK, BLOCK), BLOCK)
        r_own = pl.ds(pl.multiple_of(n * BLOCK, BLOCK), BLOCK)
        dk_ref[r_prev, :] += dkb[:BLOCK]
        dk_ref[r_own, :] += dkb[BLOCK:]
        dv_ref[r_prev, :] += dvb[:BLOCK]
        dv_ref[r_own, :] += dvb[BLOCK:]

    own = lambda b, n: b * nb + n
    in_specs = _swa_specs(nb) + [
        pl.BlockSpec((1, LANES), lambda b, n: (0, 0)),
        pl.BlockSpec((BLOCK, LANES), lambda b, n: (own(b, n), 0)),
        pl.BlockSpec((BLOCK, HPAD), lambda b, n: (own(b, n), 0)),
        pl.BlockSpec((BLOCK, LANES), lambda b, n: (own(b, n), 0)),
        pl.BlockSpec(memory_space=pl.ANY),
    ]
    out_specs = [pl.BlockSpec((BLOCK, HPAD), lambda b, n: (own(b, n), Z_AQ // HPAD)),
                 pl.BlockSpec((S, 256), lambda b, n: (b, 0)),
                 pl.BlockSpec((S, 256), lambda b, n: (b, 0)),
                 pl.BlockSpec((1, LANES), lambda b, n: (0, 0))]
    out_shape = [jax.ShapeDtypeStruct(dz.shape, dz.dtype), jax.ShapeDtypeStruct((T, 256), F32),
                 jax.ShapeDtypeStruct((T, 256), F32), jax.ShapeDtypeStruct((1, LANES), F32)]
    return pl.pallas_call(kern, name="swa_bwd", grid=(B, nb), in_specs=in_specs, out_specs=out_specs,
                          out_shape=out_shape, input_output_aliases={len(in_specs) - 1: 0},
                          compiler_params=_params(("arbitrary", "arbitrary")))(
        z, z, z, z, z, pos_row, pos_col, pos_col, sink_row, lse, do_raw, delta, dz)


MLA_T = 256
MLA_HG = 4
MLA_W = MLA_HG * LANES
MLA_SCALE = MLA_QK ** -0.5
LOG2E = 1.4426950408889634
MLA_QSCALE = MLA_SCALE * LOG2E


def _causal_t(s):
    key = lax.broadcasted_iota(jnp.int32, s.shape, 0)
    query = lax.broadcasted_iota(jnp.int32, s.shape, 1)
    return jnp.where(key <= query, s, NEG)


def _mla_fwd(q, k, v, z, B, S):
    T = B * S
    nq = S // MLA_T

    def kern(q_ref, k_ref, v_ref, gate_ref, oraw_ref, og_ref, lse_ref):
        i = pl.program_id(2)

        def scores(j):
            rows = pl.ds(pl.multiple_of(j * MLA_T, MLA_T), MLA_T)
            return tuple(_dot_nt(k_ref[rows, hh * LANES:(hh + 1) * LANES], q_ref[:, hh * LANES:(hh + 1) * LANES])
                         for hh in range(MLA_HG))

        def update(j, ss, state):
            rows = pl.ds(pl.multiple_of(j * MLA_T, MLA_T), MLA_T)
            out = []
            for hh in range(MLA_HG):
                (m, l, acc), s = state[hh], ss[hh]
                m_new = jnp.maximum(m, jnp.max(s, axis=0, keepdims=True))
                alpha = jnp.exp2(m - m_new)
                p = jnp.exp2(s - m_new)
                l = alpha * l + jnp.sum(p, axis=0, keepdims=True)
                pv = jnp.dot(v_ref[rows, hh * LANES:(hh + 1) * LANES].T, p.astype(BF16), preferred_element_type=F32)
                out.append((m_new, l, alpha * acc + pv))
            return tuple(out)

        def body(j, carry):
            state, ss = carry
            s_next = scores(j + 1)
            return update(j, ss, state), s_next

        init = tuple((jnp.full((1, MLA_T), NEG, F32), jnp.zeros((1, MLA_T), F32), jnp.zeros((LANES, MLA_T), F32))
                     for _ in range(MLA_HG))
        state, ss = lax.fori_loop(0, i, body, (init, scores(0)))
        state = update(i, tuple(_causal_t(s) for s in ss), state)
        for hh in range(MLA_HG):
            m, l, acc = state[hh]
            cols = slice(hh * LANES, (hh + 1) * LANES)
            o = (acc * (1.0 / l)).T
            oraw_ref[:, cols] = o
            g = _head_gate(gate_ref, hh)
            og_ref[:, cols] = (o * (g * jax.nn.sigmoid(g))).astype(BF16)
            lse_ref[0, 0, 0, hh:hh + 1, :] = m + jnp.log2(l)

    blk = lambda b, h, i: (b * nq + i, h)
    in_specs = [pl.BlockSpec((MLA_T, MLA_W), blk),
                pl.BlockSpec((S, MLA_W), lambda b, h, i: (b, h)),
                pl.BlockSpec((S, MLA_W), lambda b, h, i: (b, h)),
                pl.BlockSpec((MLA_T, MLA_W // 2), lambda b, h, i: (b * nq + i, Z_BGATE // (MLA_W // 2) + h))]
    out_specs = [pl.BlockSpec((MLA_T, MLA_W), blk), pl.BlockSpec((MLA_T, MLA_W), blk),
                 pl.BlockSpec((1, 1, 1, MLA_HG, MLA_T), lambda b, h, i: (b, h, i, 0, 0))]
    out_shape = [jax.ShapeDtypeStruct((T, HPAD), F32), jax.ShapeDtypeStruct((T, HPAD), BF16),
                 jax.ShapeDtypeStruct((B, HEADS // MLA_HG, nq, MLA_HG, MLA_T), F32)]
    return pl.pallas_call(kern, name="mla_fwd", grid=(B, HEADS // MLA_HG, nq), in_specs=in_specs,
                          out_specs=out_specs, out_shape=out_shape,
                          compiler_params=_params(("parallel", "parallel", "arbitrary")))(q, k, v, z)


def _mla_bwd(q, k, v, do_raw, lse, delta, B, S):
    T = B * S
    nk = S // MLA_T

    def kern(q_ref, k_ref, v_ref, do_ref, lse_ref, delta_ref, dq_ref, dk_ref, dv_ref, dq_acc, dk_acc, dv_acc):
        j = pl.program_id(2)

        @pl.when(j == 0)
        def _():
            dq_acc[...] = jnp.zeros_like(dq_acc)

        dk_acc[...] = jnp.zeros_like(dk_acc)
        dv_acc[...] = jnp.zeros_like(dv_acc)
        kts = [k_ref[:, hh * LANES:(hh + 1) * LANES].T for hh in range(MLA_HG)]

        def step(i, masked):
            rows = pl.ds(pl.multiple_of(i * MLA_T, MLA_T), MLA_T)
            for hh in range(MLA_HG):
                cols = slice(hh * LANES, (hh + 1) * LANES)
                qv, do = q_ref[rows, cols], do_ref[rows, cols]
                st = _dot_nt(k_ref[:, cols], qv)
                if masked:
                    st = _causal_t(st)
                pt = jnp.exp2(st - lse_ref[0, 0, i, hh:hh + 1, :])
                dpt = _dot_nt(v_ref[:, cols], do)
                dst = (pt * (dpt - delta_ref[0, 0, i, hh:hh + 1, :])).astype(BF16)
                dv_acc[:, cols] += jnp.dot(pt.astype(BF16), do, preferred_element_type=F32)
                dk_acc[:, cols] += jnp.dot(dst, qv, preferred_element_type=F32)
                dq_acc[hh, i] += jnp.dot(kts[hh], dst, preferred_element_type=F32)

        step(j, True)

        def body(i, c):
            step(i, False)
            return c

        lax.fori_loop(j + 1, nk, body, 0)
        dk_ref[...] = dk_acc[...] * (1.0 / LOG2E)
        dv_ref[...] = dv_acc[...]

        @pl.when(j == nk - 1)
        def _():
            for hh in range(MLA_HG):
                for t in range(nk):
                    dq_ref[t * MLA_T:(t + 1) * MLA_T, hh * LANES:(hh + 1) * LANES] = dq_acc[hh, t].T

    whole = lambda b, h, j: (b, h)
    tile = lambda b, h, j: (b * nk + j, h)
    stats = pl.BlockSpec((1, 1, nk, MLA_HG, MLA_T), lambda b, h, j: (b, h, 0, 0, 0))
    in_specs = [pl.BlockSpec((S, MLA_W), whole), pl.BlockSpec((MLA_T, MLA_W), tile),
                pl.BlockSpec((MLA_T, MLA_W), tile), pl.BlockSpec((S, MLA_W), whole), stats, stats]
    out_specs = [pl.BlockSpec((S, MLA_W), whole), pl.BlockSpec((MLA_T, MLA_W), tile),
                 pl.BlockSpec((MLA_T, MLA_W), tile)]
    out_shape = [jax.ShapeDtypeStruct((T, HPAD), F32)] * 3
    scratch = [pltpu.VMEM((MLA_HG, nk, LANES, MLA_T), F32), pltpu.VMEM((MLA_T, MLA_W), F32),
               pltpu.VMEM((MLA_T, MLA_W), F32)]
    return pl.pallas_call(kern, name="mla_bwd", grid=(B, HEADS // MLA_HG, nk), in_specs=in_specs,
                          out_specs=out_specs, out_shape=out_shape, scratch_shapes=scratch,
                          compiler_params=_params(("parallel", "parallel", "arbitrary")))(
        q, k, v, do_raw, lse, delta)


def _rope_tables(pos_col, inv_lane, rows):
    def body(ins, outs, _):
        ang = ins[0][...].astype(F32) * ins[1][...]
        lane = lax.broadcasted_iota(jnp.int32, ang.shape, 1)
        cos, sin = jnp.cos(ang), jnp.sin(ang)
        first = (lane >= HEAD_DIM) & (lane < HEAD_DIM + MLA_ROPE // 2)
        second = (lane >= HEAD_DIM + MLA_ROPE // 2) & (lane < MLA_QK)
        outs[0][...] = jnp.where(lane < HEAD_DIM, 1.0, jnp.where(lane < MLA_QK, cos, 0.0))
        outs[1][...] = jnp.where(first, -sin, 0.0)
        outs[2][...] = jnp.where(second, sin, 0.0)
    return _ew("rope_tables", body, [(pos_col, 1, 0), (inv_lane, None, None)], [(LANES, F32)] * 3, rows)


def _rope(x, c, s1, s2):
    return x * c + pltpu.roll(x, 112, 1) * s1 + pltpu.roll(x, 16, 1) * s2


def _rope_t(d, c, s1, s2):
    return d * c + pltpu.roll(d * s1, 16, 1) + pltpu.roll(d * s2, 112, 1)


def _mla_prep(qdn, w_uq, kvdn, w_ukv, z, tabs, rows):
    def body(ins, outs, _):
        q_pre, kv_pre = ins[0], ins[1]
        c, s1, s2 = ins[3][...], ins[4][...], ins[5][...]
        kr = _rope(ins[2][...].astype(F32), c, s1, s2)
        for h in range(HEADS):
            cols = slice(h * LANES, (h + 1) * LANES)
            outs[0][:, cols] = (_rope(q_pre[:, cols], c, s1, s2) * MLA_QSCALE).astype(BF16)
            outs[1][:, cols] = (kv_pre[:, cols] + kr).astype(BF16)
        outs[2][...] = kv_pre[:, HPAD:].astype(BF16)
    ins = [(z, LANES, Z_BKR // LANES), (tabs[0], LANES, 0), (tabs[1], LANES, 0), (tabs[2], LANES, 0)]
    return _ew("mla_prep", body, ins, [(HPAD, BF16)] * 3, rows, mms=[(qdn, w_uq), (kvdn, w_ukv)])


def _mla_prep_bwd(dq, dk, dv, tabs, dz, rows):
    def body(ins, outs, _):
        c, s1, s2 = ins[3][...], ins[4][...], ins[5][...]
        lane = lax.broadcasted_iota(jnp.int32, c.shape, 1)
        dkr = jnp.zeros(c.shape, F32)
        for h in range(HEADS):
            cols = slice(h * LANES, (h + 1) * LANES)
            outs[0][:, cols] = _rope_t(ins[0][:, cols] * MLA_SCALE, c, s1, s2).astype(BF16)
            dkh = ins[1][:, cols]
            outs[1][:, cols] = jnp.where(lane < HEAD_DIM, dkh, 0.0).astype(BF16)
            dkr = dkr + dkh
        outs[1][:, HPAD:] = ins[2][...].astype(BF16)
        live = (lane >= HEAD_DIM) & (lane < MLA_QK)
        outs[2][...] = jnp.where(live, _rope_t(jnp.where(live, dkr, 0.0), c, s1, s2), 0.0).astype(BF16)
    ins = [(dq, HPAD, 0), (dk, HPAD, 0), (dv, HPAD, 0), (tabs[0], LANES, 0), (tabs[1], LANES, 0),
           (tabs[2], LANES, 0)]
    outs = [(HPAD, BF16), (2 * HPAD, BF16), (LANES, BF16, dz, Z_BKR // LANES)]
    return _ew("mla_prep_bwd", body, ins, outs, rows)


def _gate_bwd(name, d_o_mm, o_raw, gate, gate_cb, dz, dz_cb, rows):
    def body(ins, outs, _):
        lane = lax.broadcasted_iota(jnp.int32, outs[2].shape, 1)
        delta = jnp.zeros(outs[2].shape, F32)
        d_gate = [None] * HEADS
        for h in range(HEADS):
            cols = slice(h * LANES, (h + 1) * LANES)
            dog, o, g = ins[0][:, cols], ins[1][:, cols], _head_gate(ins[2], h)
            sg = jax.nn.sigmoid(g)
            do = dog * (g * sg)
            outs[0][:, cols] = do.astype(BF16)
            d_gate[h] = dog * o * (sg * (1.0 + g * (1.0 - sg)))
            delta = jnp.where(lane == h, jnp.sum(do * o, axis=-1, keepdims=True), delta)
        for pair in range(HEADS // 2):
            packed = d_gate[2 * pair] + pltpu.roll(d_gate[2 * pair + 1], HEAD_DIM, 1)
            outs[1][:, pair * LANES:(pair + 1) * LANES] = packed.astype(BF16)
        outs[2][...] = delta
    ins = [(o_raw, HPAD, 0), (gate, GATE_W, gate_cb)]
    outs = [(HPAD, BF16), (GATE_W, BF16, dz, dz_cb), (LANES, F32)]
    return _ew(name, body, ins, outs, rows, mms=[d_o_mm])


def _merge_out(ua, ub, z, w_out, x0, rows):
    tm = _row_tile(rows)

    def kern(ua_ref, ub_ref, ma_ref, mb_ref, w_ref, x0_ref, y_ref, x1_ref):
        ua_v, ub_v, m_a, m_b = (r[...].astype(F32) for r in (ua_ref, ub_ref, ma_ref, mb_ref))
        y = (jax.nn.sigmoid(m_a) * ua_v + jax.nn.sigmoid(m_b) * ub_v).astype(BF16)
        y_ref[...] = y
        for j in range(D_MODEL // MM_TN):
            cols = slice(j * MM_TN, (j + 1) * MM_TN)
            x1_ref[:, cols] = jnp.dot(y, w_ref[:, cols], preferred_element_type=F32) + x0_ref[:, cols]

    row = lambda cb: pl.BlockSpec((tm, D_MODEL), lambda i: (i, cb))
    return pl.pallas_call(
        kern, name="merge_out", grid=(rows // tm,),
        in_specs=[row(0), row(0), row(Z_MA // D_MODEL), row(Z_MB // D_MODEL),
                  pl.BlockSpec(w_out.shape, lambda i: (0, 0)), row(0)],
        out_specs=[row(0), row(0)],
        out_shape=[jax.ShapeDtypeStruct((rows, D_MODEL), BF16), jax.ShapeDtypeStruct((rows, D_MODEL), F32)],
        compiler_params=_params(("parallel",)))(ua, ub, z, z, w_out, x0)


def _merge_bwd(dy_mm, ua, ub, z, dz, rows):
    def body(ins, outs, _):
        dyv = ins[0][...]
        for idx in range(2):
            s = jax.nn.sigmoid(ins[3 + idx][...].astype(F32))
            outs[idx][...] = (dyv * s).astype(BF16)
            d_m = (dyv * ins[1 + idx][...].astype(F32) * (s * (1.0 - s))).astype(BF16)
            outs[2][:, idx * D_MODEL:(idx + 1) * D_MODEL] = d_m
    ins = [(ua, D_MODEL, 0), (ub, D_MODEL, 0), (z, D_MODEL, Z_MA // D_MODEL), (z, D_MODEL, Z_MB // D_MODEL)]
    outs = [(D_MODEL, BF16), (D_MODEL, BF16), (2 * D_MODEL, BF16, dz, Z_MA // (2 * D_MODEL))]
    return _ew("merge_bwd", body, ins, outs, rows, mms=[dy_mm])


def _kv_grad_cast(dk, dv, dz, rows):
    def body(ins, outs, _):
        outs[0][:, :256] = ins[0][...].astype(BF16)
        outs[0][:, 256:] = ins[1][...].astype(BF16)
    return _ew("kv_grad_cast", body, [(dk, 256, 0), (dv, 256, 0)], [(512, BF16, dz, Z_AK // 512)], rows)[0]


def _ple_fwd(x1, hn, w_pg, p, w_pp, rows):
    def body(ins, outs, _):
        u, e = ins[0][...], ins[1][...]
        outs[0][...] = ins[2][...] + jax.nn.sigmoid(u) * e
        outs[1][...] = u.astype(BF16)
        outs[2][...] = e.astype(BF16)
    return _ew("ple_fwd", body, [(x1, D_MODEL, 0)], [(D_MODEL, F32), (D_MODEL, BF16), (D_MODEL, BF16)], rows,
               mms=[(hn, w_pg), (p, w_pp)])


def _ple_bwd(dx2, u, e, rows):
    def body(ins, outs, _):
        d, s = ins[0][...], jax.nn.sigmoid(ins[1][...].astype(F32))
        outs[0][...] = (d * s).astype(BF16)
        outs[1][...] = (d * ins[2][...].astype(F32) * (s * (1.0 - s))).astype(BF16)
    return _ew("ple_bwd", body, [(dx2, D_MODEL, 0), (u, D_MODEL, 0), (e, D_MODEL, 0)],
               [(D_MODEL, BF16)] * 2, rows)


def _loss_head(x, g, target, rows):
    def body(ins, outs, accs):
        xv, gv = ins[0][...], ins[1][...]
        r = lax.rsqrt(jnp.mean(xv * xv, axis=-1, keepdims=True) + EPS)
        xhat = xv * r
        err = xhat * gv - ins[2][...]
        accs[0][...] += jnp.broadcast_to(0.5 * jnp.sum(jnp.mean(err * err, axis=-1, keepdims=True),
                                                       axis=0, keepdims=True), (1, LANES))
        dyv = err * (1.0 / D_MODEL)
        accs[1][...] += jnp.sum(dyv * xhat, axis=0, keepdims=True)
        dy = dyv * gv
        outs[0][...] = r * (dy - xhat * jnp.mean(dy * xhat, axis=-1, keepdims=True))
    ins = [(x, D_MODEL, 0), (g.reshape(1, D_MODEL), None, None), (target, D_MODEL, 0)]
    return _ew("loss_head", body, ins, [(D_MODEL, F32)], rows, accs=[(1, LANES), (1, D_MODEL)])


def _pad_heads_cols(w, n_heads, dim):
    k = w.shape[0]
    return jnp.pad(w.reshape(k, n_heads, dim), ((0, 0), (0, 0), (0, LANES - dim))).reshape(k, n_heads * LANES)


def _unpad_heads_cols(w, n_heads, dim):
    k = w.shape[0]
    return w.reshape(k, n_heads, LANES)[:, :, :dim].reshape(k, n_heads * dim)


def _layer_weights(w, i):
    segs = jnp.split(w['w_in'][i], list(_cumsum(IN_SIZES))[:-1], axis=1)
    a_q, a_k, a_v, a_gate, b_qd, b_kvd, b_kr, b_gate, m_a, m_b = segs
    kr = jnp.pad(b_kr, ((0, 0), (HEAD_DIM, LANES - MLA_QK)))
    w_in = jnp.concatenate([
        m_a, m_b, _pad_heads_cols(a_q, HEADS, HEAD_DIM), a_gate, b_gate, _pad_heads_cols(a_k, SWA_KV_HEADS, HEAD_DIM),
        _pad_heads_cols(a_v, SWA_KV_HEADS, HEAD_DIM), b_qd, b_kvd, kr], axis=1)
    w_uq = _pad_heads_cols(w['w_uq'][i], HEADS, MLA_QK)
    ukv = w['w_ukv'][i].reshape(MLA_KV_LORA, HEADS, 2 * HEAD_DIM)
    pad = ((0, 0), (0, 0), (0, HEAD_DIM))
    w_ukv = jnp.concatenate([jnp.pad(ukv[:, :, :HEAD_DIM], pad).reshape(MLA_KV_LORA, HPAD),
                             jnp.pad(ukv[:, :, HEAD_DIM:], pad).reshape(MLA_KV_LORA, HPAD)], axis=1)
    w_br_a = _pad_heads_cols(w['w_br_a'][i].T, HEADS, HEAD_DIM).T
    w_br_b = _pad_heads_cols(w['w_br_b'][i].T, HEADS, HEAD_DIM).T
    out = dict(w_in=w_in, w_uq=w_uq, w_ukv=w_ukv, w_br_a=w_br_a, w_br_b=w_br_b, w_out=w['w_out'][i],
               w_pg=w['w_ple_gate'][i], w_pp=w['w_ple_proj'][i])
    for name in ('w_in', 'w_uq', 'w_ukv', 'w_br_a', 'w_br_b', 'w_out', 'w_pg'):
        out[name + '_t'] = out[name].T
    return out


def _cumsum(sizes):
    acc, out = 0, []
    for s in sizes:
        acc += s
        out.append(acc)
    return out


def _unpad_grads(g):
    d = g['w_in']
    seg = lambda off, width: d[:, off:off + width]
    b_kr = seg(Z_BKR, LANES)[:, HEAD_DIM:MLA_QK]
    w_in = jnp.concatenate([
        _unpad_heads_cols(seg(Z_AQ, HPAD), HEADS, HEAD_DIM), _unpad_heads_cols(seg(Z_AK, 256), SWA_KV_HEADS, HEAD_DIM),
        _unpad_heads_cols(seg(Z_AV, 256), SWA_KV_HEADS, HEAD_DIM), seg(Z_AGATE, GATE_W),
        seg(Z_BQD, MLA_Q_LORA), seg(Z_BKVD, MLA_KV_LORA), b_kr, seg(Z_BGATE, GATE_W),
        seg(Z_MA, D_MODEL), seg(Z_MB, D_MODEL)], axis=1)
    w_uq = _unpad_heads_cols(g['w_uq'], HEADS, MLA_QK)
    ukv = g['w_ukv'].reshape(MLA_KV_LORA, 2, HEADS, LANES)[:, :, :, :HEAD_DIM]
    w_ukv = jnp.concatenate([ukv[:, 0], ukv[:, 1]], axis=-1).reshape(MLA_KV_LORA, HEADS * 2 * HEAD_DIM)
    w_br_a = _unpad_heads_cols(g['w_br_a'].T, HEADS, HEAD_DIM).T
    w_br_b = _unpad_heads_cols(g['w_br_b'].T, HEADS, HEAD_DIM).T
    return dict(w_in=w_in, w_uq=w_uq, w_ukv=w_ukv, w_br_a=w_br_a, w_br_b=w_br_b, w_out=g['w_out'],
                w_ple_gate=g['w_pg'], w_ple_proj=g['w_pp'], g_mix=g['g_mix'], sink=g['sink'], g_q=g['g_q'],
                g_kv=g['g_kv'], g_ple=g['g_ple'])


def _layer_fwd(x0, p_i, lw, sm, i, pos_col, pos_row, tabs, B, S):
    T = B * S
    h = _rms_fwd("norm_mix", x0, D_MODEL, 0, sm['g_mix'][i], T)
    z, a_gate = _mm("proj_in", h, lw['w_in'], BF16, f32_cols=(Z_AGATE, GATE_W))
    sink_row = jnp.pad(sm['sink'][i], (0, LANES - HEADS)).reshape(1, LANES)
    oa_raw, oa, lse_a = _swa_fwd(z, a_gate, pos_col, pos_row, sink_row, B, S)
    qdn = _rms_fwd("norm_q", z, MLA_Q_LORA, Z_BQD // MLA_Q_LORA, sm['g_q'][i], T)
    kvdn = _rms_fwd("norm_kv", z, MLA_KV_LORA, Z_BKVD // MLA_KV_LORA, sm['g_kv'][i], T)
    qf, kf, vf = _mla_prep(qdn, lw['w_uq'], kvdn, lw['w_ukv'], z, tabs, T)
    ob_raw, ob, lse_b = _mla_fwd(qf, kf, vf, z, B, S)
    ua = _mm("proj_br_a", oa, lw['w_br_a'], BF16)
    ub = _mm("proj_br_b", ob, lw['w_br_b'], BF16)
    y, x1 = _merge_out(ua, ub, z, lw['w_out'], x0, T)
    hn = _rms_fwd("norm_ple", x1, D_MODEL, 0, sm['g_ple'][i], T)
    x2, u, e = _ple_fwd(x1, hn, lw['w_pg'], p_i, lw['w_pp'], T)
    saved = dict(x0=x0, h=h, z=z, a_gate=a_gate, sink_row=sink_row, oa_raw=oa_raw, oa=oa, lse_a=lse_a, qdn=qdn, kvdn=kvdn,
                 qf=qf, kf=kf, vf=vf, ob_raw=ob_raw, ob=ob, lse_b=lse_b, ua=ua, ub=ub, y=y, x1=x1, hn=hn,
                 u=u, e=e, p=p_i)
    return x2, saved


def _layer_bwd(dx2, sv, lw, sm, i, pos_col, pos_row, tabs, B, S):
    T = B * S
    z = sv['z']
    g = {}
    d_e, d_u = _ple_bwd(dx2, sv['u'], sv['e'], T)
    g['w_pp'] = _mm_tn("grad_pp", sv['p'], d_e)
    g['w_pg'] = _mm_tn("grad_pg", sv['hn'], d_u)
    dx1, g['g_ple'] = _rms_bwd("norm_ple_bwd", sv['x1'], D_MODEL, 0, sm['g_ple'][i], (d_u, lw['w_pg_t']), T, F32,
                               dres=dx2)
    g['w_out'] = _mm_tn("grad_out", sv['y'], dx1)
    dz = lax.empty((T, Z_WIDTH), BF16)
    d_ua, d_ub, dz = _merge_bwd((dx1, lw['w_out_t']), sv['ua'], sv['ub'], z, dz, T)
    g['w_br_a'] = _mm_tn("grad_br_a", sv['oa'], d_ua)
    g['w_br_b'] = _mm_tn("grad_br_b", sv['ob'], d_ub)
    dob_raw, dz, delta_b = _gate_bwd("gate_b_bwd", (d_ub, lw['w_br_b_t']), sv['ob_raw'], z, Z_BGATE // GATE_W,
                                     dz, Z_BGATE // GATE_W, T)
    delta_rows = delta_b[:, :HEADS].reshape(B, S // MLA_T, MLA_T, HEADS // MLA_HG, MLA_HG).transpose(0, 3, 1, 4, 2)
    dq, dk, dv = _mla_bwd(sv['qf'], sv['kf'], sv['vf'], dob_raw, sv['lse_b'], delta_rows, B, S)
    dq_pre, dkv_pre, dz = _mla_prep_bwd(dq, dk, dv, tabs, dz, T)
    g['w_uq'] = _mm_tn("grad_uq", sv['qdn'], dq_pre)
    g['w_ukv'] = _mm_tn("grad_ukv", sv['kvdn'], dkv_pre)
    dz, g['g_q'] = _rms_bwd("norm_q_bwd", z, MLA_Q_LORA, Z_BQD // MLA_Q_LORA, sm['g_q'][i],
                            (dq_pre, lw['w_uq_t']), T, BF16, into=(dz, Z_BQD // MLA_Q_LORA))
    dz, g['g_kv'] = _rms_bwd("norm_kv_bwd", z, MLA_KV_LORA, Z_BKVD // MLA_KV_LORA, sm['g_kv'][i],
                             (dkv_pre, lw['w_ukv_t']), T, BF16, into=(dz, Z_BKVD // MLA_KV_LORA))
    doa_raw, dz, delta_a = _gate_bwd("gate_a_bwd", (d_ua, lw['w_br_a_t']), sv['oa_raw'], sv['a_gate'], 0,
                                     dz, Z_AGATE // GATE_W, T)
    dz, d_ak, d_av, dsink = _swa_bwd(z, pos_col, pos_row, sv['sink_row'], sv['lse_a'], doa_raw, delta_a, dz, B, S)
    dz = _kv_grad_cast(d_ak, d_av, dz, T)
    g['sink'] = dsink[0, :HEADS]
    g['w_in'] = _mm_tn("grad_in", sv['h'], dz, tn=Z_WIDTH // 2)
    dx0, g['g_mix'] = _rms_bwd("norm_mix_bwd", sv['x0'], D_MODEL, 0, sm['g_mix'][i], (dz, lw['w_in_t']), T, F32,
                               dres=dx1)
    for name in ('g_ple', 'g_q', 'g_kv', 'g_mix'):
        g[name] = g[name][0]
    return dx0, g


def _local_step(x, p, positions, wfull, sm, loss_target):
    B, S, _ = x.shape
    T = B * S
    pos_col = positions.reshape(T, 1)
    pos_row = positions.reshape(T // BLOCK, 1, BLOCK)
    half = MLA_ROPE // 2
    inv = ROPE_THETA ** (-jnp.arange(0, MLA_ROPE, 2, dtype=F32) / MLA_ROPE)
    inv_lane = jnp.tile(inv, LANES // half).reshape(1, LANES)
    tabs = _rope_tables(pos_col, inv_lane, T)
    xc = x.reshape(T, D_MODEL)
    lws, saved = [], []
    for i in range(DEPTH):
        lw = _layer_weights(wfull, i)
        xc, sv = _layer_fwd(xc, p[i].reshape(T, PLE_DIM), lw, sm, i, pos_col, pos_row, tabs, B, S)
        lws.append(lw)
        saved.append(sv)
    dx, loss, dg_final = _loss_head(xc, sm['g_final'], loss_target.reshape(T, D_MODEL), T)
    layer_grads = [None] * DEPTH
    for i in reversed(range(DEPTH)):
        dx, g = _layer_bwd(dx, saved[i], lws[i], sm, i, pos_col, pos_row, tabs, B, S)
        layer_grads[i] = _unpad_grads(g)
    return loss, dx.reshape(B, S, D_MODEL), layer_grads, dg_final[0]


SMALL_ROWS = 48


def _pack_small(arrs):
    flat = jnp.concatenate([arrs[name].reshape(-1) for name in SMALL])
    return jnp.pad(flat, (0, SMALL_ROWS * LANES - flat.shape[0])).reshape(SMALL_ROWS, LANES)


def _unpack_small(block, shapes):
    flat = block.reshape(-1)
    out, off = {}, 0
    for name in SMALL:
        n = math.prod(shapes[name])
        out[name] = flat[off:off + n].reshape(shapes[name])
        off += n
    return out


def _flipped(shard_shape):
    return shard_shape[-1] % LANES != 0


def _to_slots(g, axis):
    r, c = g.shape
    if axis == 0:
        return g.reshape(N_CHIPS, r // N_CHIPS, c)
    return g.reshape(r, N_CHIPS, c // N_CHIPS).transpose(1, 0, 2)


def _div_tile(rows, cap):
    return next(t for t in range(min(cap, rows) // 8 * 8, 0, -8) if rows % t == 0)


def _units(shapes):
    units = []
    for w, shape in enumerate(shapes):
        r = shape[-2]
        n = next(n for n in (8, 7, 4, 2, 1) if r % (8 * n) == 0) if r >= 1024 else 1
        units += [(w, k * (r // n), r // n) for k in range(n)]
    return units


def _place():
    x, y, c = lax.axis_index("x"), lax.axis_index("y"), lax.axis_index("c")
    chips = [(1 - x, y), (x, 1 - y), (1 - x, 1 - y)]
    return x, y, c, chips


ANY = pl.BlockSpec(memory_space=pl.ANY)


def _remote(send_sems, recv_sems, k, src, dst, to):
    return pltpu.make_async_remote_copy(src_ref=src, dst_ref=dst, send_sem=send_sems.at[k],
                                        recv_sem=recv_sems.at[k], device_id=to, device_id_type=MESH)


def _gather_weights(shards, carried):
    n, nc = len(shards), len(carried)
    units = _units([s.shape for s in shards])
    nu = len(units)

    def body(*refs):
        ins, outs = refs[:n], refs[n + nc:2 * n + nc]
        send_sems, recv_sems, local_sems = refs[2 * (n + nc):]
        x, y, c, chips = _place()
        me = 2 * x + y
        sibling = (x, y, 1 - c)
        copy = functools.partial(_remote, send_sems, recv_sems)
        keeps, sends = [], []
        for u, (w, r0, nr) in enumerate(units):
            rows = pl.ds(r0, nr)
            keeps.append(pltpu.make_async_copy(ins[w].at[:, rows, :], outs[w].at[me, :, rows, :], local_sems.at[u]))
            keeps[-1].start()
        for j, (cx, cy) in enumerate(chips):
            for u, (w, r0, nr) in enumerate(units):
                rows = pl.ds(r0, nr)
                sends.append(copy(j * nu + u, ins[w].at[c, rows, :], outs[w].at[me, c, rows, :], (cx, cy, c)))
                sends[-1].start()
        for j, (cx, cy) in enumerate(chips):
            for u, (w, r0, nr) in enumerate(units):
                landed = outs[w].at[2 * cx + cy, c, pl.ds(r0, nr), :]
                copy(j * nu + u, landed, landed, (cx, cy, c)).wait_recv()
                sends.append(copy((3 + j) * nu + u, landed, landed, sibling))
                sends[-1].start()
        for j, (cx, cy) in enumerate(chips):
            for u, (w, r0, nr) in enumerate(units):
                other = outs[w].at[2 * cx + cy, 1 - c, pl.ds(r0, nr), :]
                copy((3 + j) * nu + u, other, other, sibling).wait_recv()
        for cp in sends:
            cp.wait_send()
        for keep in keeps:
            keep.wait()

    out_shape = [jax.ShapeDtypeStruct((N_CHIPS,) + s.shape, s.dtype) for s in shards]
    out_shape += [jax.ShapeDtypeStruct(a.shape, a.dtype) for a in carried]
    res = pl.pallas_call(
        body, name="gather_weights", out_shape=out_shape,
        in_specs=[ANY] * (n + nc), out_specs=[ANY] * (n + nc),
        input_output_aliases={n + k: n + k for k in range(nc)},
        scratch_shapes=[pltpu.SemaphoreType.DMA((6 * nu,)), pltpu.SemaphoreType.DMA((6 * nu,)),
                        pltpu.SemaphoreType.DMA((nu,))])(*shards, *carried)
    return res[:n], res[n:]


def _pair_exchange(g0, g1):
    n = len(g0)

    def body(*refs):
        layers, outs = (refs[:n], refs[n:2 * n]), refs[2 * n:3 * n]
        send_sems, recv_sems = refs[3 * n:]
        x, y, c, _ = _place()
        copy = functools.partial(_remote, send_sems, recv_sems)
        for w in range(n):
            for q in range(N_CHIPS):
                for layer in range(DEPTH):
                    cp = copy(N_CHIPS * w + q, layers[layer][w].at[q], outs[w].at[q], (x, y, 1 - c))
                    pl.when(c == 1 - layer)(cp.start)
        for w in range(n):
            for q in range(N_CHIPS):
                copy(N_CHIPS * w + q, layers[0][w].at[q], outs[w].at[q], (x, y, 1 - c)).wait()

    return pl.pallas_call(
        body, name="pair_exchange", out_shape=[jax.ShapeDtypeStruct(g.shape, g.dtype) for g in g0],
        in_specs=[ANY] * (2 * n), out_specs=[ANY] * n,
        scratch_shapes=[pltpu.SemaphoreType.DMA((N_CHIPS * n,)), pltpu.SemaphoreType.DMA((N_CHIPS * n,))])(*g0, *g1)


def _pair_sum(name, g0, g1, theirs, cflag):
    shape = theirs.shape
    rows, width = shape[0] * shape[1], shape[2]

    def body(ins, outs, _):
        mine = jnp.where(ins[3][0:1, 0:1] == 0.0, ins[0][...], ins[1][...])
        tot = mine + ins[2][...]
        outs[0][...] = tot
        outs[1][...] = tot.astype(BF16)
    ins = [(a.reshape(rows, width), width, 0) for a in (g0, g1, theirs)] + [(cflag, None, None)]
    f32, bf16 = _ew(name, body, ins, [(width, F32), (width, BF16)], rows, tm=_div_tile(rows, ROW_TILE))
    return f32.reshape(shape), bf16.reshape(shape)


def _chip_exchange(parts):
    n = len(parts)

    def body(*refs):
        ins, outs = refs[:n], refs[n:2 * n]
        send_sems, recv_sems = refs[2 * n:]
        x, y, c, chips = _place()
        copy = functools.partial(_remote, send_sems, recv_sems)
        sends = []
        for j, (cx, cy) in enumerate(chips):
            for w in range(n):
                sends.append(copy(j * n + w, ins[w].at[2 * cx + cy], outs[w].at[j], (cx, cy, c)))
                sends[-1].start()
        for j, (cx, cy) in enumerate(chips):
            for w in range(n):
                copy(j * n + w, outs[w].at[j], outs[w].at[j], (cx, cy, c)).wait_recv()
        for cp in sends:
            cp.wait_send()

    return pl.pallas_call(
        body, name="chip_exchange",
        out_shape=[jax.ShapeDtypeStruct((3,) + a.shape[1:], a.dtype) for a in parts],
        in_specs=[ANY] * n, out_specs=[ANY] * n,
        scratch_shapes=[pltpu.SemaphoreType.DMA((3 * n,)), pltpu.SemaphoreType.DMA((3 * n,))])(*parts)


def _chip_sum(name, part, landed, chipflag):
    _, r, width = part.shape
    tm = _div_tile(r, ROW_TILE // 2)

    def kern(p_ref, l_ref, flag_ref, o_ref):
        me = flag_ref[0:1, 0:1]
        own = jnp.where(me == 0.0, p_ref[0], jnp.where(me == 1.0, p_ref[1], jnp.where(me == 2.0, p_ref[2], p_ref[3])))
        o_ref[...] = ((own + l_ref[0].astype(F32)) + l_ref[1].astype(F32)) + l_ref[2].astype(F32)

    return pl.pallas_call(
        kern, name=name, grid=(r // tm,),
        in_specs=[pl.BlockSpec((N_CHIPS, tm, width), lambda i: (0, i, 0)),
                  pl.BlockSpec((3, tm, width), lambda i: (0, i, 0)),
                  pl.BlockSpec((1, LANES), lambda i: (0, 0))],
        out_specs=pl.BlockSpec((tm, width), lambda i: (i, 0)),
        out_shape=jax.ShapeDtypeStruct((r, width), F32), compiler_params=_params(("arbitrary",)))(part, landed, chipflag)


def _pair_broadcast(mine):
    n = len(mine)
    units = _units([a.shape for a in mine])

    def body(*refs):
        ins, outs = refs[:n], refs[n:2 * n]
        send_sems, recv_sems = refs[2 * n:]
        x, y, c, _ = _place()
        copy = functools.partial(_remote, send_sems, recv_sems)
        cps = [copy(u, ins[w].at[pl.ds(r0, nr), :], outs[w].at[pl.ds(r0, nr), :], (x, y, 1 - c))
               for u, (w, r0, nr) in enumerate(units)]
        for cp in cps:
            cp.start()
        for cp in cps:
            cp.wait()

    return pl.pallas_call(
        body, name="pair_broadcast", out_shape=[jax.ShapeDtypeStruct(a.shape, a.dtype) for a in mine],
        in_specs=[ANY] * n, out_specs=[ANY] * n,
        scratch_shapes=[pltpu.SemaphoreType.DMA((len(units),)), pltpu.SemaphoreType.DMA((len(units),))])(*mine)


def _small_allreduce(v):
    offsets = [(dx, dy, dc) for dx in (0, 1) for dy in (0, 1) for dc in (0, 1)][1:]

    def body(v_ref, out_ref, recv_ref, send_sems, recv_sems):
        x, y, c, _ = _place()
        flip = lambda a, d: 1 - a if d else a
        peers = [(flip(x, dx), flip(y, dy), flip(c, dc)) for dx, dy, dc in offsets]
        copy = functools.partial(_remote, send_sems, recv_sems)
        me = 4 * x + 2 * y + c
        recv_ref[me] = v_ref[...]
        cps = [copy(k, v_ref, recv_ref.at[me], peer) for k, peer in enumerate(peers)]
        for cp in cps:
            cp.start()
        for k, (px, py, pc) in enumerate(peers):
            landed = recv_ref.at[4 * px + 2 * py + pc]
            copy(k, landed, landed, (px, py, pc)).wait_recv()
        for cp in cps:
            cp.wait_send()
        tot = recv_ref[0]
        for d in range(1, 8):
            tot = tot + recv_ref[d]
        out_ref[...] = tot

    vmem = pl.BlockSpec(memory_space=pltpu.VMEM)
    return pl.pallas_call(
        body, name="small_allreduce", out_shape=jax.ShapeDtypeStruct(v.shape, v.dtype),
        in_specs=[vmem], out_specs=vmem,
        scratch_shapes=[pltpu.VMEM((8,) + v.shape, v.dtype), pltpu.SemaphoreType.DMA((7,)),
                        pltpu.SemaphoreType.DMA((7,))])(v)


def _adam_math(gv, wv, mv, vv):
    mv = ADAM_B1 * mv + (1.0 - ADAM_B1) * gv
    vv = ADAM_B2 * vv + (1.0 - ADAM_B2) * (gv * gv)
    m_hat = mv / (1.0 - ADAM_B1 ** ADAM_STEP)
    v_hat = vv / (1.0 - ADAM_B2 ** ADAM_STEP)
    return -ADAM_LR * (m_hat / (jnp.sqrt(v_hat) + ADAM_EPS) + ADAM_WD * wv), mv, vv


def _adamw_big(name, mine, theirs, cflag, w, m, v):
    _, r, width = w.shape
    tm = _div_tile(r, ROW_TILE // 2)

    def kern(mine_ref, theirs_ref, flag_ref, w_ref, m_ref, v_ref, g_ref, d_ref, nm_ref, nv_ref):
        layer = pl.program_id(0).astype(F32)
        gv = jnp.where(flag_ref[0:1, 0:1] == layer, mine_ref[...], theirs_ref[...])
        g_ref[0] = gv
        d_ref[0], nm_ref[0], nv_ref[0] = _adam_math(gv, w_ref[0], m_ref[0], v_ref[0])

    flat = pl.BlockSpec((tm, width), lambda l, i: (i, 0))
    stacked = pl.BlockSpec((1, tm, width), lambda l, i: (l, i, 0))
    return pl.pallas_call(
        kern, name=name, grid=(DEPTH, r // tm),
        in_specs=[flat, flat, pl.BlockSpec((1, LANES), lambda l, i: (0, 0)), stacked, stacked, stacked],
        out_specs=[stacked] * 4, out_shape=[jax.ShapeDtypeStruct(w.shape, F32)] * 4,
        compiler_params=_params(("arbitrary", "arbitrary")))(mine, theirs, cflag, w, m, v)


def _adamw_small(g, w, m, v):
    def body(ins, outs, _):
        outs[0][...], outs[1][...], outs[2][...] = _adam_math(*(r[...] for r in ins))
    return _ew("adamw_small", body, [(a, LANES, 0) for a in (g, w, m, v)], [(LANES, F32)] * 3, SMALL_ROWS)


def kernel(x, p, positions, g_mix, w_in, sink, g_q, w_uq, g_kv, w_ukv, w_br_a, w_br_b, w_out, g_ple, w_ple_gate, w_ple_proj, g_final, loss_target, m_g_mix, m_w_in, m_sink, m_g_q, m_w_uq, m_g_kv, m_w_ukv, m_w_br_a, m_w_br_b, m_w_out, m_g_ple, m_w_ple_gate, m_w_ple_proj, m_g_final, v_g_mix, v_w_in, v_sink, v_g_q, v_w_uq, v_g_kv, v_w_ukv, v_w_br_a, v_w_br_b, v_w_out, v_g_ple, v_w_ple_gate, v_w_ple_proj, v_g_final):
    w = dict(g_mix=g_mix, w_in=w_in, sink=sink, g_q=g_q, w_uq=w_uq, g_kv=g_kv, w_ukv=w_ukv, w_br_a=w_br_a,
             w_br_b=w_br_b, w_out=w_out, g_ple=g_ple, w_ple_gate=w_ple_gate, w_ple_proj=w_ple_proj, g_final=g_final)
    m = dict(g_mix=m_g_mix, w_in=m_w_in, sink=m_sink, g_q=m_g_q, w_uq=m_w_uq, g_kv=m_g_kv, w_ukv=m_w_ukv,
             w_br_a=m_w_br_a, w_br_b=m_w_br_b, w_out=m_w_out, g_ple=m_g_ple, w_ple_gate=m_w_ple_gate,
             w_ple_proj=m_w_ple_proj, g_final=m_g_final)
    v = dict(g_mix=v_g_mix, w_in=v_w_in, sink=v_sink, g_q=v_g_q, w_uq=v_w_uq, g_kv=v_g_kv, w_ukv=v_w_ukv,
             w_br_a=v_w_br_a, w_br_b=v_w_br_b, w_out=v_w_out, g_ple=v_g_ple, w_ple_gate=v_w_ple_gate,
             w_ple_proj=v_w_ple_proj, g_final=v_g_final)
    wfull = _gather_full(w)
    sm = {name: w[name] for name in SMALL}
    loss_row, grad_x, layer_grads, dg_final = _local_step(x, p, positions, wfull, sm, loss_target)
    loss = lax.psum(loss_row[0, 0], ("x", "y", "c"))
    res = _update(layer_grads, dg_final, w, m, v)
    return (loss, grad_x, *[res[name][kind] for kind in range(4) for name in WEIGHT_NAMES])


def _gather_behind(shards):
    n = len(shards)
    srcs = [jax.new_ref(s, memory_space=pltpu.MemorySpace.HBM) for s in shards]
    lands = [jax.empty_ref(jax.ShapeDtypeStruct((N_CHIPS,) + s.shape, s.dtype), memory_space=pltpu.MemorySpace.HBM)
             for s in shards]

    @pl.kernel(mesh=plsc.ScalarSubcoreMesh(axis_name="sequencer", num_cores=1), name="gather_behind",
               scratch_types=(pltpu.SemaphoreType.DMA((3 * n,)), pltpu.SemaphoreType.DMA((3 * n,)),
                              pltpu.SemaphoreType.DMA((n,))),
               compiler_params=pltpu.CompilerParams(collective_id=0))
    def launch(send_sems, recv_sems, local_sems):
        x, y, c, chips = _place()
        me = 2 * x + y
        barrier = pltpu.get_barrier_semaphore()
        for cx, cy in chips:
            pl.semaphore_signal(barrier, inc=1, device_id=(cx, cy, c), device_id_type=MESH)
        pl.semaphore_wait(barrier, len(chips))
        copy = functools.partial(_remote, send_sems, recv_sems)
        keeps = [pltpu.make_async_copy(srcs[w], lands[w].at[me], local_sems.at[w]) for w in range(n)]
        cps = [copy(j * n + w, srcs[w], lands[w].at[me], (cx, cy, c))
               for j, (cx, cy) in enumerate(chips) for w in range(n)]
        for cp in keeps + cps:
            cp.start()
        for cp in keeps + cps:
            cp.wait()

    launch()
    return [land[...] for land in lands]


def _gather_full(w):
    shards = [w[name].astype(BF16) for name, _ in SHARDED]
    first, later = _gather_weights([s[0].reshape((2, s.shape[1] // 2) + s.shape[2:]) for s in shards],
                                   [s[1] for s in shards])
    second = _gather_behind(later)
    full = {}
    for k, (name, axis) in enumerate(SHARDED):
        layer0 = first[k].reshape((N_CHIPS,) + shards[k].shape[1:])
        full[name] = [jnp.concatenate(list(blocks), axis=axis - 1) for blocks in (layer0, second[k])]
    return full


def _update(layer_grads, dg_final, w, m, v):
    small_shapes = {name: w[name].shape for name in SMALL}
    cflag = jnp.full((1, LANES), lax.axis_index("c"), F32)
    chipflag = jnp.full((1, LANES), 2 * lax.axis_index("x") + lax.axis_index("y"), F32)

    slots = [[_to_slots(layer_grads[layer][name], axis - 1) for name, axis in SHARDED] for layer in range(DEPTH)]
    theirs = _pair_exchange(slots[0], slots[1])
    pair = [_pair_sum("pair_sum_" + name, slots[0][k], slots[1][k], theirs[k], cflag)
            for k, (name, _) in enumerate(SHARDED)]
    landed = _chip_exchange([bf16 for _, bf16 in pair])
    mine = [_chip_sum("chip_sum_" + name, pair[k][0], landed[k], chipflag) for k, (name, _) in enumerate(SHARDED)]
    other = _pair_broadcast(mine)
    res = {}
    for k, (name, _) in enumerate(SHARDED):
        flip = _flipped(w[name].shape)
        view = (lambda a: jnp.swapaxes(a, -1, -2)) if flip else (lambda a: a)
        outs = _adamw_big("adamw_" + name, view(mine[k]), view(other[k]), cflag, view(w[name]), view(m[name]),
                          view(v[name]))
        res[name] = tuple(view(a) for a in outs)

    gsmall = {name: jnp.stack([layer_grads[layer][name] for layer in range(DEPTH)]) for name in SMALL[:-1]}
    gsmall['g_final'] = dg_final
    gsum = _small_allreduce(_pack_small(gsmall))
    small = (gsum,) + tuple(_adamw_small(gsum, _pack_small(w), _pack_small(m), _pack_small(v)))
    for name, arrs in zip(SMALL, zip(*[[_unpack_small(a, small_shapes)[n] for n in SMALL] for a in small])):
        res[name] = arrs
    return res
```

```python
import functools
import math

import jax
import jax.numpy as jnp
from jax import lax
from jax.experimental import pallas as pl
from jax.experimental.pallas import tpu as pltpu
from jax.experimental.pallas import tpu_sc as plsc

F32 = jnp.float32
BF16 = jnp.bfloat16

D_MODEL = 1024
DEPTH = 2
PLE_DIM = 256
BLOCK = 128
EPS = 1e-6
NEG = -1e30
HEADS = 8
SWA_KV_HEADS = 2
HEAD_DIM = 64
LANES = 128
HPAD = HEADS * LANES
MLA_QK = 96
MLA_ROPE = 32
MLA_Q_LORA = 256
MLA_KV_LORA = 128
ROPE_THETA = 10000.0
IN_SIZES = (512, 128, 128, 512, 256, 128, 32, 512, 1024, 1024)

Z_MA, Z_MB, Z_AQ, Z_AGATE, Z_BGATE = 0, 1024, 2048, 3072, 3584
Z_AK, Z_AV, Z_BQD, Z_BKVD, Z_BKR = 4096, 4352, 4608, 4864, 4992
Z_WIDTH = 5120
GATE_W = HEADS * HEAD_DIM

ADAM_LR, ADAM_B1, ADAM_B2, ADAM_EPS, ADAM_WD, ADAM_STEP = 0.001, 0.9, 0.999, 1e-08, 0.01, 10

VMEM_LIMIT = 56 * 1024 * 1024
MESH = pl.DeviceIdType.MESH

WEIGHT_NAMES = ('g_mix', 'w_in', 'sink', 'g_q', 'w_uq', 'g_kv', 'w_ukv', 'w_br_a', 'w_br_b',
                'w_out', 'g_ple', 'w_ple_gate', 'w_ple_proj', 'g_final')
SHARDED = (('w_in', 2), ('w_uq', 2), ('w_ukv', 2), ('w_br_a', 2), ('w_br_b', 2),
           ('w_out', 1), ('w_ple_gate', 1), ('w_ple_proj', 2))
SMALL = ('g_mix', 'sink', 'g_q', 'g_kv', 'g_ple', 'g_final')
N_CHIPS = 4


def _params(sem):
    return pltpu.CompilerParams(dimension_semantics=sem, vmem_limit_bytes=VMEM_LIMIT)


MM_TN = 512
ROW_TILE = 512
BIG_WEIGHT_BYTES = 8 * 1024 * 1024


def _row_tile(rows, weight_bytes=0):
    tm = ROW_TILE // 2 if weight_bytes > BIG_WEIGHT_BYTES else ROW_TILE
    return min(tm, rows)


def _ew(name, body, ins, outs, rows, accs=(), mms=(), tm=None):
    n_mm, n_in, n_out = len(mms), len(ins), len(outs)
    if tm is None:
        tm = _row_tile(rows, sum(b.size * b.dtype.itemsize for _, b in mms))
    in_specs, args = [], []
    for a, b in mms:
        in_specs += [pl.BlockSpec((tm, a.shape[1]), lambda i: (i, 0)), pl.BlockSpec(b.shape, lambda i: (0, 0))]
        args += [a, b]
    for arr, width, cb in ins:
        if width is None:
            in_specs.append(pl.BlockSpec(arr.shape, lambda i, nd=arr.ndim: (0,) * nd))
        else:
            in_specs.append(pl.BlockSpec((tm, width), lambda i, cb=cb: (i, cb)))
        args.append(arr)
    out_shape, out_specs, aliases = [], [], {}
    for k, out in enumerate(outs):
        if len(out) == 4:
            aliases[len(args)] = k
            in_specs.append(pl.BlockSpec(memory_space=pl.ANY))
            args.append(out[2])
            out_shape.append(jax.ShapeDtypeStruct(out[2].shape, out[2].dtype))
            out_specs.append(pl.BlockSpec((tm, out[0]), lambda i, cb=out[3]: (i, cb)))
        else:
            out_shape.append(jax.ShapeDtypeStruct((rows, out[0]), out[1]))
            out_specs.append(pl.BlockSpec((tm, out[0]), lambda i: (i, 0)))
    n_in += len(aliases)
    out_shape += [jax.ShapeDtypeStruct(s, F32) for s in accs]
    out_specs += [pl.BlockSpec(s, lambda i: (0, 0)) for s in accs]

    def kern(*refs):
        mm_refs, refs = refs[:2 * n_mm], refs[2 * n_mm:]
        in_refs, out_refs = refs[:n_in - len(aliases)], refs[n_in:n_in + n_out]
        acc_refs, prod_refs = refs[n_in + n_out:n_in + n_out + len(accs)], refs[n_in + n_out + len(accs):]
        if acc_refs:
            @pl.when(pl.program_id(0) == 0)
            def _():
                for r in acc_refs:
                    r[...] = jnp.zeros_like(r)
        for k in range(n_mm):
            a_ref, b_ref, prod = mm_refs[2 * k], mm_refs[2 * k + 1], prod_refs[k]
            av = a_ref[...].astype(BF16)
            n = b_ref.shape[1]
            tn = min(MM_TN, n)
            for j in range(n // tn):
                cols = slice(j * tn, (j + 1) * tn)
                prod[:, cols] = jnp.dot(av, b_ref[:, cols], preferred_element_type=F32)
        body(tuple(prod_refs) + tuple(in_refs), out_refs, acc_refs)

    scratch = [pltpu.VMEM((tm, b.shape[1]), F32) for _, b in mms]
    res = pl.pallas_call(kern, name=name, grid=(rows // tm,), in_specs=in_specs, out_specs=out_specs,
                         out_shape=out_shape, scratch_shapes=scratch, input_output_aliases=aliases,
                         compiler_params=_params(("arbitrary",)))(*args)
    return res


def _rms_fwd(name, x, width, cb, g, rows):
    def body(ins, outs, _):
        xv = ins[0][...].astype(F32)
        r = lax.rsqrt(jnp.mean(xv * xv, axis=-1, keepdims=True) + EPS)
        outs[0][...] = ((xv * r) * ins[1][...]).astype(BF16)
    return _ew(name, body, [(x, width, cb), (g.reshape(1, width), None, None)], [(width, BF16)], rows)[0]


def _rms_bwd(name, x, width, cb, g, dh_mm, rows, out_dtype, dres=None, into=()):
    def body(ins, outs, accs):
        dhv, xv, gv = ins[0][...], ins[1][...].astype(F32), ins[2][...]
        r = lax.rsqrt(jnp.mean(xv * xv, axis=-1, keepdims=True) + EPS)
        xhat = xv * r
        accs[0][...] += jnp.sum(dhv * xhat, axis=0, keepdims=True)
        dy = dhv * gv
        dx = r * (dy - xhat * jnp.mean(dy * xhat, axis=-1, keepdims=True))
        if dres is not None:
            dx = dx + ins[3][...]
        outs[0][...] = dx.astype(out_dtype)
    ins = [(x, width, cb), (g.reshape(1, width), None, None)]
    if dres is not None:
        ins.append((dres, width, 0))
    return _ew(name, body, ins, [(width, out_dtype) + tuple(into)], rows, accs=[(1, width)], mms=[dh_mm])


def _mm(name, a, b, out_dtype, residual=None, f32_cols=None, tn=MM_TN):
    M, K = a.shape
    N = b.shape[1]
    tm, tn = _row_tile(M, b.size * b.dtype.itemsize), min(tn, N)
    has_res = residual is not None
    c0, cw = f32_cols if f32_cols else (0, 0)

    def kern(*refs):
        a_ref, b_ref = refs[0], refs[1]
        o_ref = refs[3] if has_res else refs[2]
        av = a_ref[...].astype(BF16)
        for j in range(N // tn):
            cols = slice(j * tn, (j + 1) * tn)
            part = jnp.dot(av, b_ref[:, cols], preferred_element_type=F32)
            if has_res:
                part = part + refs[2][:, cols]
            o_ref[:, cols] = part.astype(o_ref.dtype)
            if c0 <= j * tn and (j + 1) * tn <= c0 + cw:
                refs[-1][:, j * tn - c0:(j + 1) * tn - c0] = part

    in_specs = [pl.BlockSpec((tm, K), lambda i: (i, 0)), pl.BlockSpec((K, N), lambda i: (0, 0))]
    args = [a, b]
    if has_res:
        in_specs.append(pl.BlockSpec((tm, N), lambda i: (i, 0)))
        args.append(residual)
    out_specs = [pl.BlockSpec((tm, N), lambda i: (i, 0))]
    out_shape = [jax.ShapeDtypeStruct((M, N), out_dtype)]
    if f32_cols:
        assert c0 % tn == 0 and cw % tn == 0
        out_specs.append(pl.BlockSpec((tm, cw), lambda i: (i, 0)))
        out_shape.append(jax.ShapeDtypeStruct((M, cw), F32))
    res = pl.pallas_call(kern, name=name, grid=(M // tm,), in_specs=in_specs, out_specs=out_specs,
                         out_shape=out_shape, compiler_params=_params(("parallel",)))(*args)
    return res if f32_cols else res[0]


def _mm_tn(name, a, b, tk=1024, tn=2048):
    T, M = a.shape
    N = b.shape[1]
    tn, tk = min(tn, N), min(tk, T)

    def kern(a_ref, b_ref, o_ref):
        k = pl.program_id(1)
        part = _dot_tn(a_ref[...].astype(BF16), b_ref[...].astype(BF16))

        @pl.when(k == 0)
        def _():
            o_ref[...] = part

        @pl.when(k > 0)
        def _():
            o_ref[...] += part

    return pl.pallas_call(
        kern, name=name, grid=(N // tn, T // tk),
        in_specs=[pl.BlockSpec((tk, M), lambda j, k: (k, 0)), pl.BlockSpec((tk, tn), lambda j, k: (k, j))],
        out_specs=pl.BlockSpec((M, tn), lambda j, k: (0, j)),
        out_shape=jax.ShapeDtypeStruct((M, N), F32),
        compiler_params=_params(("parallel", "arbitrary")))(a, b)


def _dot_nt(a, b):
    return lax.dot_general(a, b, (((1,), (1,)), ((), ())), preferred_element_type=F32)


def _dot_tn(a, b):
    return lax.dot_general(a, b, (((0,), (0,)), ((), ())), preferred_element_type=F32)


SWA_SCALE = HEAD_DIM ** -0.5


def _swa_band(n, pq_ref, pkp_ref, pkc_ref):
    posk = jnp.concatenate([pkp_ref[...], pkc_ref[...]], axis=0)
    dist = (pq_ref[0] - posk).astype(F32)
    kj = lax.broadcasted_iota(jnp.int32, (2 * BLOCK, BLOCK), 0)
    qi = lax.broadcasted_iota(jnp.int32, (2 * BLOCK, BLOCK), 1)
    t_abs = n * BLOCK + qi
    s_abs = n * BLOCK - BLOCK + kj
    return dist, (s_abs >= 0) & (s_abs <= t_abs) & (t_abs - s_abs < BLOCK)


SWA_GROUP = HEADS // SWA_KV_HEADS


def _head_gate(gate_ref, h):
    pair = gate_ref[:, (h // 2) * LANES:(h // 2 + 1) * LANES].astype(F32)
    return pair if h % 2 == 0 else pltpu.roll(pair, HEAD_DIM, 1)


def _swa_group_q(q_all, g):
    heads = range(g * SWA_GROUP, (g + 1) * SWA_GROUP)
    return jnp.concatenate([(q_all[:, h * LANES:(h + 1) * LANES] * SWA_SCALE).astype(BF16) for h in heads], axis=0)


def _swa_mask(s, dist, valid, h):
    return jnp.where(valid, s - (2.0 ** -(h + 1)) * dist, NEG)


def _rows_to_lanes(rows):
    block = jnp.concatenate(list(rows) + [jnp.zeros((LANES - len(rows), BLOCK), F32)], axis=0)
    return block.T


def _swa_specs(nb):
    prev = lambda b, n: b * nb + jnp.maximum(n - 1, 0)
    own = lambda b, n: b * nb + n
    return [
        pl.BlockSpec((BLOCK, HPAD), lambda b, n: (own(b, n), Z_AQ // HPAD)),
        pl.BlockSpec((BLOCK, 256), lambda b, n: (prev(b, n), Z_AK // 256)),
        pl.BlockSpec((BLOCK, 256), lambda b, n: (own(b, n), Z_AK // 256)),
        pl.BlockSpec((BLOCK, 256), lambda b, n: (prev(b, n), Z_AV // 256)),
        pl.BlockSpec((BLOCK, 256), lambda b, n: (own(b, n), Z_AV // 256)),
        pl.BlockSpec((1, 1, BLOCK), lambda b, n: (own(b, n), 0, 0)),
        pl.BlockSpec((BLOCK, 1), lambda b, n: (prev(b, n), 0)),
        pl.BlockSpec((BLOCK, 1), lambda b, n: (own(b, n), 0)),
    ]


def _swa_fwd(z, gate, pos_col, pos_row, sink_row, B, S):
    nb = S // BLOCK
    T = B * S

    def kern(q_ref, kp_ref, kc_ref, vp_ref, vc_ref, pq_ref, pkp_ref, pkc_ref, gate_ref, sink_ref,
             oraw_ref, og_ref, lse_ref):
        q_all = q_ref[...]
        kb = jnp.concatenate([kp_ref[...], kc_ref[...]], axis=0).astype(BF16)
        vb = jnp.concatenate([vp_ref[...], vc_ref[...]], axis=0).astype(BF16)
        dist, valid = _swa_band(pl.program_id(1), pq_ref, pkp_ref, pkc_ref)
        lse_rows = []
        for grp in range(SWA_KV_HEADS):
            gcols = slice(grp * LANES, (grp + 1) * LANES)
            s_all = _dot_nt(kb[:, gcols], _swa_group_q(q_all, grp))
            probs = []
            for hh in range(SWA_GROUP):
                h = grp * SWA_GROUP + hh
                s = _swa_mask(s_all[:, hh * BLOCK:(hh + 1) * BLOCK], dist, valid, h)
                sink_h = sink_ref[0:1, h:h + 1]
                m = jnp.maximum(jnp.max(s, axis=0, keepdims=True), sink_h)
                e = jnp.exp(s - m)
                denom = jnp.sum(e, axis=0, keepdims=True) + jnp.exp(sink_h - m)
                probs.append((e * (1.0 / denom)).astype(BF16))
                lse_rows.append(m + jnp.log(denom))
            o_all = jnp.dot(vb[:, gcols].T, jnp.concatenate(probs, axis=1), preferred_element_type=F32)
            for hh in range(SWA_GROUP):
                h = grp * SWA_GROUP + hh
                cols = slice(h * LANES, (h + 1) * LANES)
                o = o_all[:, hh * BLOCK:(hh + 1) * BLOCK].T
                oraw_ref[:, cols] = o
                g = _head_gate(gate_ref, h)
                og_ref[:, cols] = (o * (g * jax.nn.sigmoid(g))).astype(BF16)
        lse_ref[...] = _rows_to_lanes(lse_rows)

    own = lambda b, n: b * nb + n
    in_specs = _swa_specs(nb) + [
        pl.BlockSpec((BLOCK, GATE_W), lambda b, n: (own(b, n), 0)),
        pl.BlockSpec((1, LANES), lambda b, n: (0, 0)),
    ]
    out_specs = [pl.BlockSpec((BLOCK, HPAD), lambda b, n: (own(b, n), 0)),
                 pl.BlockSpec((BLOCK, HPAD), lambda b, n: (own(b, n), 0)),
                 pl.BlockSpec((BLOCK, LANES), lambda b, n: (own(b, n), 0))]
    out_shape = [jax.ShapeDtypeStruct((T, HPAD), F32), jax.ShapeDtypeStruct((T, HPAD), BF16),
                 jax.ShapeDtypeStruct((T, LANES), F32)]
    return pl.pallas_call(kern, name="swa_fwd", grid=(B, nb), in_specs=in_specs, out_specs=out_specs,
                          out_shape=out_shape, compiler_params=_params(("parallel", "arbitrary")))(
        z, z, z, z, z, pos_row, pos_col, pos_col, gate, sink_row)


def _swa_bwd(z, pos_col, pos_row, sink_row, lse, do_raw, delta, dz, B, S):
    nb = S // BLOCK
    T = B * S

    def kern(q_ref, kp_ref, kc_ref, vp_ref, vc_ref, pq_ref, pkp_ref, pkc_ref, sink_ref, lse_ref, do_ref,
             delta_ref, dz_ref, dq_ref, dk_ref, dv_ref, dsink_ref):
        b, n = pl.program_id(0), pl.program_id(1)

        @pl.when(n == 0)
        def _():
            dk_ref[...] = jnp.zeros_like(dk_ref)
            dv_ref[...] = jnp.zeros_like(dv_ref)

        @pl.when((b == 0) & (n == 0))
        def _():
            dsink_ref[...] = jnp.zeros_like(dsink_ref)

        q_all = q_ref[...]
        kb = jnp.concatenate([kp_ref[...], kc_ref[...]], axis=0).astype(BF16)
        vb = jnp.concatenate([vp_ref[...], vc_ref[...]], axis=0).astype(BF16)
        dist, valid = _swa_band(n, pq_ref, pkp_ref, pkc_ref)
        lse_t, delta_t = lse_ref[...].T, delta_ref[...].T
        lane1 = lax.broadcasted_iota(jnp.int32, (1, LANES), 1)
        dsink = jnp.zeros((1, LANES), F32)
        dk_band, dv_band = [], []
        for grp in range(SWA_KV_HEADS):
            gcols = slice(grp * LANES, (grp + 1) * LANES)
            heads = range(grp * SWA_GROUP, (grp + 1) * SWA_GROUP)
            qg = _swa_group_q(q_all, grp)
            dog = jnp.concatenate([do_ref[:, h * LANES:(h + 1) * LANES] for h in heads], axis=0)
            s_all = _dot_nt(kb[:, gcols], qg)
            dp_all = _dot_nt(vb[:, gcols], dog)
            ps, dss = [], []
            for hh, h in enumerate(heads):
                blk = slice(hh * BLOCK, (hh + 1) * BLOCK)
                lse_h, delta_h = lse_t[h:h + 1, :], delta_t[h:h + 1, :]
                p = jnp.exp(_swa_mask(s_all[:, blk], dist, valid, h) - lse_h)
                ps.append(p.astype(BF16))
                dss.append((p * (dp_all[:, blk] - delta_h)).astype(BF16))
                psink = jnp.exp(sink_ref[0:1, h:h + 1] - lse_h)
                dsink = dsink + jnp.where(lane1 == h, -jnp.sum(psink * delta_h, axis=1, keepdims=True), 0.0)
            dsg = jnp.concatenate(dss, axis=1)
            dq_all = jnp.dot(kb[:, gcols].T, dsg, preferred_element_type=F32) * SWA_SCALE
            for hh, h in enumerate(heads):
                dq_ref[:, h * LANES:(h + 1) * LANES] = dq_all[:, hh * BLOCK:(hh + 1) * BLOCK].T.astype(BF16)
            dk_band.append(jnp.dot(dsg, qg, preferred_element_type=F32))
            dv_band.append(jnp.dot(jnp.concatenate(ps, axis=1), dog, preferred_element_type=F32))
        dsink_ref[...] += dsink
        dkb = jnp.concatenate(dk_band, axis=1)
        dvb = jnp.concatenate(dv_band, axis=1)
        r_prev = pl.ds(pl.multiple_of(jnp.maximum(n - 1, 0) * BLOCK, BLOCK), BLOCK)
        r_own = pl.ds(pl.multiple_of(n * BLOCK, BLOCK), BLOCK)
        dk_ref[r_prev, :] += dkb[:BLOCK]
        dk_ref[r_own, :] += dkb[BLOCK:]
        dv_ref[r_prev, :] += dvb[:BLOCK]
        dv_ref[r_own, :] += dvb[BLOCK:]

    own = lambda b, n: b * nb + n
    in_specs = _swa_specs(nb) + [
        pl.BlockSpec((1, LANES), lambda b, n: (0, 0)),
        pl.BlockSpec((BLOCK, LANES), lambda b, n: (own(b, n), 0)),
        pl.BlockSpec((BLOCK, HPAD), lambda b, n: (own(b, n), 0)),
        pl.BlockSpec((BLOCK, LANES), lambda b, n: (own(b, n), 0)),
        pl.BlockSpec(memory_space=pl.ANY),
    ]
    out_specs = [pl.BlockSpec((BLOCK, HPAD), lambda b, n: (own(b, n), Z_AQ // HPAD)),
                 pl.BlockSpec((S, 256), lambda b, n: (b, 0)),
                 pl.BlockSpec((S, 256), lambda b, n: (b, 0)),
                 pl.BlockSpec((1, LANES), lambda b, n: (0, 0))]
    out_shape = [jax.ShapeDtypeStruct(dz.shape, dz.dtype), jax.ShapeDtypeStruct((T, 256), F32),
                 jax.ShapeDtypeStruct((T, 256), F32), jax.ShapeDtypeStruct((1, LANES), F32)]
    return pl.pallas_call(kern, name="swa_bwd", grid=(B, nb), in_specs=in_specs, out_specs=out_specs,
                          out_shape=out_shape, input_output_aliases={len(in_specs) - 1: 0},
                          compiler_params=_params(("arbitrary", "arbitrary")))(
        z, z, z, z, z, pos_row, pos_col, pos_col, sink_row, lse, do_raw, delta, dz)


MLA_T = 256
MLA_HG = 4
MLA_W = MLA_HG * LANES
MLA_SCALE = MLA_QK ** -0.5
LOG2E = 1.4426950408889634
MLA_QSCALE = MLA_SCALE * LOG2E


def _causal_t(s):
    key = lax.broadcasted_iota(jnp.int32, s.shape, 0)
    query = lax.broadcasted_iota(jnp.int32, s.shape, 1)
    return jnp.where(key <= query, s, NEG)


def _mla_fwd(q, k, v, z, B, S):
    T = B * S
    nq = S // MLA_T

    def kern(q_ref, k_ref, v_ref, gate_ref, oraw_ref, og_ref, lse_ref):
        i = pl.program_id(2)

        def scores(j):
            rows = pl.ds(pl.multiple_of(j * MLA_T, MLA_T), MLA_T)
            return tuple(_dot_nt(k_ref[rows, hh * LANES:(hh + 1) * LANES], q_ref[:, hh * LANES:(hh + 1) * LANES])
                         for hh in range(MLA_HG))

        def update(j, ss, state):
            rows = pl.ds(pl.multiple_of(j * MLA_T, MLA_T), MLA_T)
            out = []
            for hh in range(MLA_HG):
                (m, l, acc), s = state[hh], ss[hh]
                m_new = jnp.maximum(m, jnp.max(s, axis=0, keepdims=True))
                alpha = jnp.exp2(m - m_new)
                p = jnp.exp2(s - m_new)
                l = alpha * l + jnp.sum(p, axis=0, keepdims=True)
                pv = jnp.dot(v_ref[rows, hh * LANES:(hh + 1) * LANES].T, p.astype(BF16), preferred_element_type=F32)
                out.append((m_new, l, alpha * acc + pv))
            return tuple(out)

        def body(j, carry):
            state, ss = carry
            s_next = scores(j + 1)
            return update(j, ss, state), s_next

        init = tuple((jnp.full((1, MLA_T), NEG, F32), jnp.zeros((1, MLA_T), F32), jnp.zeros((LANES, MLA_T), F32))
                     for _ in range(MLA_HG))
        state, ss = lax.fori_loop(0, i, body, (init, scores(0)))
        state = update(i, tuple(_causal_t(s) for s in ss), state)
        for hh in range(MLA_HG):
            m, l, acc = state[hh]
            cols = slice(hh * LANES, (hh + 1) * LANES)
            o = (acc * (1.0 / l)).T
            oraw_ref[:, cols] = o
            g = _head_gate(gate_ref, hh)
            og_ref[:, cols] = (o * (g * jax.nn.sigmoid(g))).astype(BF16)
            lse_ref[0, 0, 0, hh:hh + 1, :] = m + jnp.log2(l)

    blk = lambda b, h, i: (b * nq + i, h)
    in_specs = [pl.BlockSpec((MLA_T, MLA_W), blk),
                pl.BlockSpec((S, MLA_W), lambda b, h, i: (b, h)),
                pl.BlockSpec((S, MLA_W), lambda b, h, i: (b, h)),
                pl.BlockSpec((MLA_T, MLA_W // 2), lambda b, h, i: (b * nq + i, Z_BGATE // (MLA_W // 2) + h))]
    out_specs = [pl.BlockSpec((MLA_T, MLA_W), blk), pl.BlockSpec((MLA_T, MLA_W), blk),
                 pl.BlockSpec((1, 1, 1, MLA_HG, MLA_T), lambda b, h, i: (b, h, i, 0, 0))]
    out_shape = [jax.ShapeDtypeStruct((T, HPAD), F32), jax.ShapeDtypeStruct((T, HPAD), BF16),
                 jax.ShapeDtypeStruct((B, HEADS // MLA_HG, nq, MLA_HG, MLA_T), F32)]
    return pl.pallas_call(kern, name="mla_fwd", grid=(B, HEADS // MLA_HG, nq), in_specs=in_specs,
                          out_specs=out_specs, out_shape=out_shape,
                          compiler_params=_params(("parallel", "parallel", "arbitrary")))(q, k, v, z)


def _mla_bwd(q, k, v, do_raw, lse, delta, B, S):
    T = B * S
    nk = S // MLA_T

    def kern(q_ref, k_ref, v_ref, do_ref, lse_ref, delta_ref, dq_ref, dk_ref, dv_ref, dq_acc, dk_acc, dv_acc):
        j = pl.program_id(2)

        @pl.when(j == 0)
        def _():
            dq_acc[...] = jnp.zeros_like(dq_acc)

        dk_acc[...] = jnp.zeros_like(dk_acc)
        dv_acc[...] = jnp.zeros_like(dv_acc)
        kts = [k_ref[:, hh * LANES:(hh + 1) * LANES].T for hh in range(MLA_HG)]

        def step(i, masked):
            rows = pl.ds(pl.multiple_of(i * MLA_T, MLA_T), MLA_T)
            for hh in range(MLA_HG):
                cols = slice(hh * LANES, (hh + 1) * LANES)
                qv, do = q_ref[rows, cols], do_ref[rows, cols]
                st = _dot_nt(k_ref[:, cols], qv)
                if masked:
                    st = _causal_t(st)
                pt = jnp.exp2(st - lse_ref[0, 0, i, hh:hh + 1, :])
                dpt = _dot_nt(v_ref[:, cols], do)
                dst = (pt * (dpt - delta_ref[0, 0, i, hh:hh + 1, :])).astype(BF16)
                dv_acc[:, cols] += jnp.dot(pt.astype(BF16), do, preferred_element_type=F32)
                dk_acc[:, cols] += jnp.dot(dst, qv, preferred_element_type=F32)
                dq_acc[hh, i] += jnp.dot(kts[hh], dst, preferred_element_type=F32)

        step(j, True)

        def body(i, c):
            step(i, False)
            return c

        lax.fori_loop(j + 1, nk, body, 0)
        dk_ref[...] = (dk_acc[...] * (1.0 / LOG2E)).astype(BF16)
        dv_ref[...] = dv_acc[...].astype(BF16)

        @pl.when(j == nk - 1)
        def _():
            for hh in range(MLA_HG):
                for t in range(nk):
                    dq_ref[t * MLA_T:(t + 1) * MLA_T, hh * LANES:(hh + 1) * LANES] = dq_acc[hh, t].T.astype(BF16)

    whole = lambda b, h, j: (b, h)
    tile = lambda b, h, j: (b * nk + j, h)
    stats = pl.BlockSpec((1, 1, nk, MLA_HG, MLA_T), lambda b, h, j: (b, h, 0, 0, 0))
    in_specs = [pl.BlockSpec((S, MLA_W), whole), pl.BlockSpec((MLA_T, MLA_W), tile),
                pl.BlockSpec((MLA_T, MLA_W), tile), pl.BlockSpec((S, MLA_W), whole), stats, stats]
    out_specs = [pl.BlockSpec((S, MLA_W), whole), pl.BlockSpec((MLA_T, MLA_W), tile),
                 pl.BlockSpec((MLA_T, MLA_W), tile)]
    out_shape = [jax.ShapeDtypeStruct((T, HPAD), BF16)] * 3
    scratch = [pltpu.VMEM((MLA_HG, nk, LANES, MLA_T), F32), pltpu.VMEM((MLA_T, MLA_W), F32),
               pltpu.VMEM((MLA_T, MLA_W), F32)]
    return pl.pallas_call(kern, name="mla_bwd", grid=(B, HEADS // MLA_HG, nk), in_specs=in_specs,
                          out_specs=out_specs, out_shape=out_shape, scratch_shapes=scratch,
                          compiler_params=_params(("parallel", "parallel", "arbitrary")))(
        q, k, v, do_raw, lse, delta)


def _rope_tables(pos_col, inv_lane, rows):
    def body(ins, outs, _):
        ang = ins[0][...].astype(F32) * ins[1][...]
        lane = lax.broadcasted_iota(jnp.int32, ang.shape, 1)
        cos, sin = jnp.cos(ang), jnp.sin(ang)
        first = (lane >= HEAD_DIM) & (lane < HEAD_DIM + MLA_ROPE // 2)
        second = (lane >= HEAD_DIM + MLA_ROPE // 2) & (lane < MLA_QK)
        outs[0][...] = jnp.where(lane < HEAD_DIM, 1.0, jnp.where(lane < MLA_QK, cos, 0.0))
        outs[1][...] = jnp.where(first, -sin, 0.0)
        outs[2][...] = jnp.where(second, sin, 0.0)
    return _ew("rope_tables", body, [(pos_col, 1, 0), (inv_lane, None, None)], [(LANES, F32)] * 3, rows)


def _rope(x, c, s1, s2):
    return x * c + pltpu.roll(x, 112, 1) * s1 + pltpu.roll(x, 16, 1) * s2


def _rope_t(d, c, s1, s2):
    return d * c + pltpu.roll(d * s1, 16, 1) + pltpu.roll(d * s2, 112, 1)


def _mla_prep(qdn, w_uq, kvdn, w_ukv, z, tabs, rows):
    def body(ins, outs, _):
        q_pre, kv_pre = ins[0], ins[1]
        c, s1, s2 = ins[3][...], ins[4][...], ins[5][...]
        kr = _rope(ins[2][...].astype(F32), c, s1, s2)
        for h in range(HEADS):
            cols = slice(h * LANES, (h + 1) * LANES)
            outs[0][:, cols] = (_rope(q_pre[:, cols], c, s1, s2) * MLA_QSCALE).astype(BF16)
            outs[1][:, cols] = (kv_pre[:, cols] + kr).astype(BF16)
        outs[2][...] = kv_pre[:, HPAD:].astype(BF16)
    ins = [(z, LANES, Z_BKR // LANES), (tabs[0], LANES, 0), (tabs[1], LANES, 0), (tabs[2], LANES, 0)]
    return _ew("mla_prep", body, ins, [(HPAD, BF16)] * 3, rows, mms=[(qdn, w_uq), (kvdn, w_ukv)])


def _mla_prep_bwd(dq, dk, dv, tabs, dz, rows):
    def body(ins, outs, _):
        c, s1, s2 = ins[3][...], ins[4][...], ins[5][...]
        lane = lax.broadcasted_iota(jnp.int32, c.shape, 1)
        dkr = jnp.zeros(c.shape, F32)
        for h in range(HEADS):
            cols = slice(h * LANES, (h + 1) * LANES)
            outs[0][:, cols] = _rope_t(ins[0][:, cols].astype(F32) * MLA_SCALE, c, s1, s2).astype(BF16)
            dkh = ins[1][:, cols].astype(F32)
            outs[1][:, cols] = jnp.where(lane < HEAD_DIM, dkh, 0.0).astype(BF16)
            dkr = dkr + dkh
        outs[1][:, HPAD:] = ins[2][...].astype(BF16)
        live = (lane >= HEAD_DIM) & (lane < MLA_QK)
        outs[2][...] = jnp.where(live, _rope_t(jnp.where(live, dkr, 0.0), c, s1, s2), 0.0).astype(BF16)
    ins = [(dq, HPAD, 0), (dk, HPAD, 0), (dv, HPAD, 0), (tabs[0], LANES, 0), (tabs[1], LANES, 0),
           (tabs[2], LANES, 0)]
    outs = [(HPAD, BF16), (2 * HPAD, BF16), (LANES, BF16, dz, Z_BKR // LANES)]
    return _ew("mla_prep_bwd", body, ins, outs, rows)


def _gate_bwd(name, d_o_mm, o_raw, gate, gate_cb, dz, dz_cb, rows):
    def body(ins, outs, _):
        lane = lax.broadcasted_iota(jnp.int32, outs[2].shape, 1)
        delta = jnp.zeros(outs[2].shape, F32)
        d_gate = [None] * HEADS
        for h in range(HEADS):
            cols = slice(h * LANES, (h + 1) * LANES)
            dog, o, g = ins[0][:, cols], ins[1][:, cols], _head_gate(ins[2], h)
            sg = jax.nn.sigmoid(g)
            do = dog * (g * sg)
            outs[0][:, cols] = do.astype(BF16)
            d_gate[h] = dog * o * (sg * (1.0 + g * (1.0 - sg)))
            delta = jnp.where(lane == h, jnp.sum(do * o, axis=-1, keepdims=True), delta)
        for pair in range(HEADS // 2):
            packed = d_gate[2 * pair] + pltpu.roll(d_gate[2 * pair + 1], HEAD_DIM, 1)
            outs[1][:, pair * LANES:(pair + 1) * LANES] = packed.astype(BF16)
        outs[2][...] = delta
    ins = [(o_raw, HPAD, 0), (gate, GATE_W, gate_cb)]
    outs = [(HPAD, BF16), (GATE_W, BF16, dz, dz_cb), (LANES, F32)]
    return _ew(name, body, ins, outs, rows, mms=[d_o_mm])


def _merge_out(ua, ub, z, w_out, x0, rows):
    tm = _row_tile(rows)

    def kern(ua_ref, ub_ref, ma_ref, mb_ref, w_ref, x0_ref, y_ref, x1_ref):
        ua_v, ub_v, m_a, m_b = (r[...].astype(F32) for r in (ua_ref, ub_ref, ma_ref, mb_ref))
        y = (jax.nn.sigmoid(m_a) * ua_v + jax.nn.sigmoid(m_b) * ub_v).astype(BF16)
        y_ref[...] = y
        for j in range(D_MODEL // MM_TN):
            cols = slice(j * MM_TN, (j + 1) * MM_TN)
            x1_ref[:, cols] = jnp.dot(y, w_ref[:, cols], preferred_element_type=F32) + x0_ref[:, cols]

    row = lambda cb: pl.BlockSpec((tm, D_MODEL), lambda i: (i, cb))
    return pl.pallas_call(
        kern, name="merge_out", grid=(rows // tm,),
        in_specs=[row(0), row(0), row(Z_MA // D_MODEL), row(Z_MB // D_MODEL),
                  pl.BlockSpec(w_out.shape, lambda i: (0, 0)), row(0)],
        out_specs=[row(0), row(0)],
        out_shape=[jax.ShapeDtypeStruct((rows, D_MODEL), BF16), jax.ShapeDtypeStruct((rows, D_MODEL), F32)],
        compiler_params=_params(("parallel",)))(ua, ub, z, z, w_out, x0)


def _merge_bwd(dy_mm, ua, ub, z, dz, rows):
    def body(ins, outs, _):
        dyv = ins[0][...]
        for idx in range(2):
            s = jax.nn.sigmoid(ins[3 + idx][...].astype(F32))
            outs[idx][...] = (dyv * s).astype(BF16)
            d_m = (dyv * ins[1 + idx][...].astype(F32) * (s * (1.0 - s))).astype(BF16)
            outs[2][:, idx * D_MODEL:(idx + 1) * D_MODEL] = d_m
    ins = [(ua, D_MODEL, 0), (ub, D_MODEL, 0), (z, D_MODEL, Z_MA // D_MODEL), (z, D_MODEL, Z_MB // D_MODEL)]
    outs = [(D_MODEL, BF16), (D_MODEL, BF16), (2 * D_MODEL, BF16, dz, Z_MA // (2 * D_MODEL))]
    return _ew("merge_bwd", body, ins, outs, rows, mms=[dy_mm])


def _kv_grad_cast(dk, dv, dz, rows):
    def body(ins, outs, _):
        outs[0][:, :256] = ins[0][...].astype(BF16)
        outs[0][:, 256:] = ins[1][...].astype(BF16)
    return _ew("kv_grad_cast", body, [(dk, 256, 0), (dv, 256, 0)], [(512, BF16, dz, Z_AK // 512)], rows)[0]


def _ple_fwd(x1, hn, w_pg, p, w_pp, rows):
    def body(ins, outs, _):
        u, e = ins[0][...], ins[1][...]
        outs[0][...] = ins[2][...] + jax.nn.sigmoid(u) * e
        outs[1][...] = u.astype(BF16)
        outs[2][...] = e.astype(BF16)
    return _ew("ple_fwd", body, [(x1, D_MODEL, 0)], [(D_MODEL, F32), (D_MODEL, BF16), (D_MODEL, BF16)], rows,
               mms=[(hn, w_pg), (p, w_pp)])


def _ple_bwd(dx2, u, e, rows):
    def body(ins, outs, _):
        d, s = ins[0][...], jax.nn.sigmoid(ins[1][...].astype(F32))
        outs[0][...] = (d * s).astype(BF16)
        outs[1][...] = (d * ins[2][...].astype(F32) * (s * (1.0 - s))).astype(BF16)
    return _ew("ple_bwd", body, [(dx2, D_MODEL, 0), (u, D_MODEL, 0), (e, D_MODEL, 0)],
               [(D_MODEL, BF16)] * 2, rows)


def _loss_head(x, g, target, rows):
    def body(ins, outs, accs):
        xv, gv = ins[0][...], ins[1][...]
        r = lax.rsqrt(jnp.mean(xv * xv, axis=-1, keepdims=True) + EPS)
        xhat = xv * r
        err = xhat * gv - ins[2][...]
        accs[0][...] += jnp.broadcast_to(0.5 * jnp.sum(jnp.mean(err * err, axis=-1, keepdims=True),
                                                       axis=0, keepdims=True), (1, LANES))
        dyv = err * (1.0 / D_MODEL)
        accs[1][...] += jnp.sum(dyv * xhat, axis=0, keepdims=True)
        dy = dyv * gv
        outs[0][...] = r * (dy - xhat * jnp.mean(dy * xhat, axis=-1, keepdims=True))
    ins = [(x, D_MODEL, 0), (g.reshape(1, D_MODEL), None, None), (target, D_MODEL, 0)]
    return _ew("loss_head", body, ins, [(D_MODEL, F32)], rows, accs=[(1, LANES), (1, D_MODEL)])


def _pad_heads_cols(w, n_heads, dim):
    k = w.shape[0]
    return jnp.pad(w.reshape(k, n_heads, dim), ((0, 0), (0, 0), (0, LANES - dim))).reshape(k, n_heads * LANES)


def _unpad_heads_cols(w, n_heads, dim):
    k = w.shape[0]
    return w.reshape(k, n_heads, LANES)[:, :, :dim].reshape(k, n_heads * dim)


def _layer_weights(w, i):
    segs = jnp.split(w['w_in'][i], list(_cumsum(IN_SIZES))[:-1], axis=1)
    a_q, a_k, a_v, a_gate, b_qd, b_kvd, b_kr, b_gate, m_a, m_b = segs
    kr = jnp.pad(b_kr, ((0, 0), (HEAD_DIM, LANES - MLA_QK)))
    w_in = jnp.concatenate([
        m_a, m_b, _pad_heads_cols(a_q, HEADS, HEAD_DIM), a_gate, b_gate, _pad_heads_cols(a_k, SWA_KV_HEADS, HEAD_DIM),
        _pad_heads_cols(a_v, SWA_KV_HEADS, HEAD_DIM), b_qd, b_kvd, kr], axis=1)
    w_uq = _pad_heads_cols(w['w_uq'][i], HEADS, MLA_QK)
    ukv = w['w_ukv'][i].reshape(MLA_KV_LORA, HEADS, 2 * HEAD_DIM)
    pad = ((0, 0), (0, 0), (0, HEAD_DIM))
    w_ukv = jnp.concatenate([jnp.pad(ukv[:, :, :HEAD_DIM], pad).reshape(MLA_KV_LORA, HPAD),
                             jnp.pad(ukv[:, :, HEAD_DIM:], pad).reshape(MLA_KV_LORA, HPAD)], axis=1)
    w_br_a = _pad_heads_cols(w['w_br_a'][i].T, HEADS, HEAD_DIM).T
    w_br_b = _pad_heads_cols(w['w_br_b'][i].T, HEADS, HEAD_DIM).T
    out = dict(w_in=w_in, w_uq=w_uq, w_ukv=w_ukv, w_br_a=w_br_a, w_br_b=w_br_b, w_out=w['w_out'][i],
               w_pg=w['w_ple_gate'][i], w_pp=w['w_ple_proj'][i])
    for name in ('w_in', 'w_uq', 'w_ukv', 'w_br_a', 'w_br_b', 'w_out', 'w_pg'):
        out[name + '_t'] = out[name].T
    return out


def _cumsum(sizes):
    acc, out = 0, []
    for s in sizes:
        acc += s
        out.append(acc)
    return out


def _unpad_grads(g):
    d = g['w_in']
    seg = lambda off, width: d[:, off:off + width]
    b_kr = seg(Z_BKR, LANES)[:, HEAD_DIM:MLA_QK]
    w_in = jnp.concatenate([
        _unpad_heads_cols(seg(Z_AQ, HPAD), HEADS, HEAD_DIM), _unpad_heads_cols(seg(Z_AK, 256), SWA_KV_HEADS, HEAD_DIM),
        _unpad_heads_cols(seg(Z_AV, 256), SWA_KV_HEADS, HEAD_DIM), seg(Z_AGATE, GATE_W),
        seg(Z_BQD, MLA_Q_LORA), seg(Z_BKVD, MLA_KV_LORA), b_kr, seg(Z_BGATE, GATE_W),
        seg(Z_MA, D_MODEL), seg(Z_MB, D_MODEL)], axis=1)
    w_uq = _unpad_heads_cols(g['w_uq'], HEADS, MLA_QK)
    ukv = g['w_ukv'].reshape(MLA_KV_LORA, 2, HEADS, LANES)[:, :, :, :HEAD_DIM]
    w_ukv = jnp.concatenate([ukv[:, 0], ukv[:, 1]], axis=-1).reshape(MLA_KV_LORA, HEADS * 2 * HEAD_DIM)
    w_br_a = _unpad_heads_cols(g['w_br_a'].T, HEADS, HEAD_DIM).T
    w_br_b = _unpad_heads_cols(g['w_br_b'].T, HEADS, HEAD_DIM).T
    return dict(w_in=w_in, w_uq=w_uq, w_ukv=w_ukv, w_br_a=w_br_a, w_br_b=w_br_b, w_out=g['w_out'],
                w_ple_gate=g['w_pg'], w_ple_proj=g['w_pp'], g_mix=g['g_mix'], sink=g['sink'], g_q=g['g_q'],
                g_kv=g['g_kv'], g_ple=g['g_ple'])


def _layer_fwd(x0, p_i, lw, sm, i, pos_col, pos_row, tabs, B, S):
    T = B * S
    h = _rms_fwd("norm_mix", x0, D_MODEL, 0, sm['g_mix'][i], T)
    z, a_gate = _mm("proj_in", h, lw['w_in'], BF16, f32_cols=(Z_AGATE, GATE_W))
    sink_row = jnp.pad(sm['sink'][i], (0, LANES - HEADS)).reshape(1, LANES)
    oa_raw, oa, lse_a = _swa_fwd(z, a_gate, pos_col, pos_row, sink_row, B, S)
    qdn = _rms_fwd("norm_q", z, MLA_Q_LORA, Z_BQD // MLA_Q_LORA, sm['g_q'][i], T)
    kvdn = _rms_fwd("norm_kv", z, MLA_KV_LORA, Z_BKVD // MLA_KV_LORA, sm['g_kv'][i], T)
    qf, kf, vf = _mla_prep(qdn, lw['w_uq'], kvdn, lw['w_ukv'], z, tabs, T)
    ob_raw, ob, lse_b = _mla_fwd(qf, kf, vf, z, B, S)
    ua = _mm("proj_br_a", oa, lw['w_br_a'], BF16)
    ub = _mm("proj_br_b", ob, lw['w_br_b'], BF16)
    y, x1 = _merge_out(ua, ub, z, lw['w_out'], x0, T)
    hn = _rms_fwd("norm_ple", x1, D_MODEL, 0, sm['g_ple'][i], T)
    x2, u, e = _ple_fwd(x1, hn, lw['w_pg'], p_i, lw['w_pp'], T)
    saved = dict(x0=x0, h=h, z=z, a_gate=a_gate, sink_row=sink_row, oa_raw=oa_raw, oa=oa, lse_a=lse_a, qdn=qdn, kvdn=kvdn,
                 qf=qf, kf=kf, vf=vf, ob_raw=ob_raw, ob=ob, lse_b=lse_b, ua=ua, ub=ub, y=y, x1=x1, hn=hn,
                 u=u, e=e, p=p_i)
    return x2, saved


def _layer_bwd(dx2, sv, lw, sm, i, pos_col, pos_row, tabs, B, S):
    T = B * S
    z = sv['z']
    g = {}
    d_e, d_u = _ple_bwd(dx2, sv['u'], sv['e'], T)
    g['w_pp'] = _mm_tn("grad_pp", sv['p'], d_e)
    g['w_pg'] = _mm_tn("grad_pg", sv['hn'], d_u)
    dx1, g['g_ple'] = _rms_bwd("norm_ple_bwd", sv['x1'], D_MODEL, 0, sm['g_ple'][i], (d_u, lw['w_pg_t']), T, F32,
                               dres=dx2)
    g['w_out'] = _mm_tn("grad_out", sv['y'], dx1)
    dz = lax.empty((T, Z_WIDTH), BF16)
    d_ua, d_ub, dz = _merge_bwd((dx1, lw['w_out_t']), sv['ua'], sv['ub'], z, dz, T)
    g['w_br_a'] = _mm_tn("grad_br_a", sv['oa'], d_ua)
    g['w_br_b'] = _mm_tn("grad_br_b", sv['ob'], d_ub)
    dob_raw, dz, delta_b = _gate_bwd("gate_b_bwd", (d_ub, lw['w_br_b_t']), sv['ob_raw'], z, Z_BGATE // GATE_W,
                                     dz, Z_BGATE // GATE_W, T)
    delta_rows = delta_b[:, :HEADS].reshape(B, S // MLA_T, MLA_T, HEADS // MLA_HG, MLA_HG).transpose(0, 3, 1, 4, 2)
    dq, dk, dv = _mla_bwd(sv['qf'], sv['kf'], sv['vf'], dob_raw, sv['lse_b'], delta_rows, B, S)
    dq_pre, dkv_pre, dz = _mla_prep_bwd(dq, dk, dv, tabs, dz, T)
    g['w_uq'] = _mm_tn("grad_uq", sv['qdn'], dq_pre)
    g['w_ukv'] = _mm_tn("grad_ukv", sv['kvdn'], dkv_pre)
    dz, g['g_q'] = _rms_bwd("norm_q_bwd", z, MLA_Q_LORA, Z_BQD // MLA_Q_LORA, sm['g_q'][i],
                            (dq_pre, lw['w_uq_t']), T, BF16, into=(dz, Z_BQD // MLA_Q_LORA))
    dz, g['g_kv'] = _rms_bwd("norm_kv_bwd", z, MLA_KV_LORA, Z_BKVD // MLA_KV_LORA, sm['g_kv'][i],
                             (dkv_pre, lw['w_ukv_t']), T, BF16, into=(dz, Z_BKVD // MLA_KV_LORA))
    doa_raw, dz, delta_a = _gate_bwd("gate_a_bwd", (d_ua, lw['w_br_a_t']), sv['oa_raw'], sv['a_gate'], 0,
                                     dz, Z_AGATE // GATE_W, T)
    dz, d_ak, d_av, dsink = _swa_bwd(z, pos_col, pos_row, sv['sink_row'], sv['lse_a'], doa_raw, delta_a, dz, B, S)
    dz = _kv_grad_cast(d_ak, d_av, dz, T)
    g['sink'] = dsink[0, :HEADS]
    g['w_in'] = _mm_tn("grad_in", sv['h'], dz, tn=Z_WIDTH // 2)
    dx0, g['g_mix'] = _rms_bwd("norm_mix_bwd", sv['x0'], D_MODEL, 0, sm['g_mix'][i], (dz, lw['w_in_t']), T, F32,
                               dres=dx1)
    for name in ('g_ple', 'g_q', 'g_kv', 'g_mix'):
        g[name] = g[name][0]
    return dx0, g


def _local_step(x, p, positions, wfull, sm, loss_target):
    B, S, _ = x.shape
    T = B * S
    pos_col = positions.reshape(T, 1)
    pos_row = positions.reshape(T // BLOCK, 1, BLOCK)
    half = MLA_ROPE // 2
    inv = ROPE_THETA ** (-jnp.arange(0, MLA_ROPE, 2, dtype=F32) / MLA_ROPE)
    inv_lane = jnp.tile(inv, LANES // half).reshape(1, LANES)
    tabs = _rope_tables(pos_col, inv_lane, T)
    xc = x.reshape(T, D_MODEL)
    lws, saved = [], []
    for i in range(DEPTH):
        lw = _layer_weights(wfull, i)
        xc, sv = _layer_fwd(xc, p[i].reshape(T, PLE_DIM), lw, sm, i, pos_col, pos_row, tabs, B, S)
        lws.append(lw)
        saved.append(sv)
    dx, loss, dg_final = _loss_head(xc, sm['g_final'], loss_target.reshape(T, D_MODEL), T)
    layer_grads = [None] * DEPTH
    for i in reversed(range(DEPTH)):
        dx, g = _layer_bwd(dx, saved[i], lws[i], sm, i, pos_col, pos_row, tabs, B, S)
        layer_grads[i] = _unpad_grads(g)
    return loss, dx.reshape(B, S, D_MODEL), layer_grads, dg_final[0]


SMALL_ROWS = 48


def _pack_small(arrs):
    flat = jnp.concatenate([arrs[name].reshape(-1) for name in SMALL])
    return jnp.pad(flat, (0, SMALL_ROWS * LANES - flat.shape[0])).reshape(SMALL_ROWS, LANES)


def _unpack_small(block, shapes):
    flat = block.reshape(-1)
    out, off = {}, 0
    for name in SMALL:
        n = math.prod(shapes[name])
        out[name] = flat[off:off + n].reshape(shapes[name])
        off += n
    return out


def _flipped(shard_shape):
    return shard_shape[-1] % LANES != 0


def _to_slots(g, axis):
    r, c = g.shape
    if axis == 0:
        return g.reshape(N_CHIPS, r // N_CHIPS, c)
    return g.reshape(r, N_CHIPS, c // N_CHIPS).transpose(1, 0, 2)


def _div_tile(rows, cap):
    return next(t for t in range(min(cap, rows) // 8 * 8, 0, -8) if rows % t == 0)


def _units(shapes):
    units = []
    for w, shape in enumerate(shapes):
        r = shape[-2]
        n = next(n for n in (8, 7, 4, 2, 1) if r % (8 * n) == 0) if r >= 1024 else 1
        units += [(w, k * (r // n), r // n) for k in range(n)]
    return units


def _place():
    x, y, c = lax.axis_index("x"), lax.axis_index("y"), lax.axis_index("c")
    chips = [(1 - x, y), (x, 1 - y), (1 - x, 1 - y)]
    return x, y, c, chips


ANY = pl.BlockSpec(memory_space=pl.ANY)


def _remote(send_sems, recv_sems, k, src, dst, to):
    return pltpu.make_async_remote_copy(src_ref=src, dst_ref=dst, send_sem=send_sems.at[k],
                                        recv_sem=recv_sems.at[k], device_id=to, device_id_type=MESH)


def _gather_weights(shards, carried):
    n, nc = len(shards), len(carried)
    units = _units([s.shape for s in shards])
    nu = len(units)

    def body(*refs):
        ins, outs = refs[:n], refs[n + nc:2 * n + nc]
        send_sems, recv_sems, local_sems = refs[2 * (n + nc):]
        x, y, c, chips = _place()
        me = 2 * x + y
        sibling = (x, y, 1 - c)
        copy = functools.partial(_remote, send_sems, recv_sems)
        keeps, sends = [], []
        for u, (w, r0, nr) in enumerate(units):
            rows = pl.ds(r0, nr)
            keeps.append(pltpu.make_async_copy(ins[w].at[:, rows, :], outs[w].at[me, :, rows, :], local_sems.at[u]))
            keeps[-1].start()
        for j, (cx, cy) in enumerate(chips):
            for u, (w, r0, nr) in enumerate(units):
                rows = pl.ds(r0, nr)
                sends.append(copy(j * nu + u, ins[w].at[c, rows, :], outs[w].at[me, c, rows, :], (cx, cy, c)))
                sends[-1].start()
        for j, (cx, cy) in enumerate(chips):
            for u, (w, r0, nr) in enumerate(units):
                landed = outs[w].at[2 * cx + cy, c, pl.ds(r0, nr), :]
                copy(j * nu + u, landed, landed, (cx, cy, c)).wait_recv()
                sends.append(copy((3 + j) * nu + u, landed, landed, sibling))
                sends[-1].start()
        for j, (cx, cy) in enumerate(chips):
            for u, (w, r0, nr) in enumerate(units):
                other = outs[w].at[2 * cx + cy, 1 - c, pl.ds(r0, nr), :]
                copy((3 + j) * nu + u, other, other, sibling).wait_recv()
        for cp in sends:
            cp.wait_send()
        for keep in keeps:
            keep.wait()

    out_shape = [jax.ShapeDtypeStruct((N_CHIPS,) + s.shape, s.dtype) for s in shards]
    out_shape += [jax.ShapeDtypeStruct(a.shape, a.dtype) for a in carried]
    res = pl.pallas_call(
        body, name="gather_weights", out_shape=out_shape,
        in_specs=[ANY] * (n + nc), out_specs=[ANY] * (n + nc),
        input_output_aliases={n + k: n + k for k in range(nc)},
        scratch_shapes=[pltpu.SemaphoreType.DMA((6 * nu,)), pltpu.SemaphoreType.DMA((6 * nu,)),
                        pltpu.SemaphoreType.DMA((nu,))])(*shards, *carried)
    return res[:n], res[n:]


def _pair_exchange(g0, g1):
    n = len(g0)

    def body(*refs):
        layers, outs = (refs[:n], refs[n:2 * n]), refs[2 * n:3 * n]
        send_sems, recv_sems = refs[3 * n:]
        x, y, c, _ = _place()
        copy = functools.partial(_remote, send_sems, recv_sems)
        for w in range(n):
            for q in range(N_CHIPS):
                for layer in range(DEPTH):
                    cp = copy(N_CHIPS * w + q, layers[layer][w].at[q], outs[w].at[q], (x, y, 1 - c))
                    pl.when(c == 1 - layer)(cp.start)
        for w in range(n):
            for q in range(N_CHIPS):
                copy(N_CHIPS * w + q, layers[0][w].at[q], outs[w].at[q], (x, y, 1 - c)).wait()

    return pl.pallas_call(
        body, name="pair_exchange", out_shape=[jax.ShapeDtypeStruct(g.shape, g.dtype) for g in g0],
        in_specs=[ANY] * (2 * n), out_specs=[ANY] * n,
        scratch_shapes=[pltpu.SemaphoreType.DMA((N_CHIPS * n,)), pltpu.SemaphoreType.DMA((N_CHIPS * n,))])(*g0, *g1)


def _pair_sum(name, g0, g1, theirs, cflag):
    shape = theirs.shape
    rows, width = shape[0] * shape[1], shape[2]

    def body(ins, outs, _):
        mine = jnp.where(ins[3][0:1, 0:1] == 0.0, ins[0][...], ins[1][...])
        tot = mine + ins[2][...]
        outs[0][...] = tot
        outs[1][...] = tot.astype(BF16)
    ins = [(a.reshape(rows, width), width, 0) for a in (g0, g1, theirs)] + [(cflag, None, None)]
    f32, bf16 = _ew(name, body, ins, [(width, F32), (width, BF16)], rows, tm=_div_tile(rows, ROW_TILE))
    return f32.reshape(shape), bf16.reshape(shape)


def _chip_exchange(parts):
    n = len(parts)

    def body(*refs):
        ins, outs = refs[:n], refs[n:2 * n]
        send_sems, recv_sems = refs[2 * n:]
        x, y, c, chips = _place()
        copy = functools.partial(_remote, send_sems, recv_sems)
        sends = []
        for j, (cx, cy) in enumerate(chips):
            for w in range(n):
                sends.append(copy(j * n + w, ins[w].at[2 * cx + cy], outs[w].at[j], (cx, cy, c)))
                sends[-1].start()
        for j, (cx, cy) in enumerate(chips):
            for w in range(n):
                copy(j * n + w, outs[w].at[j], outs[w].at[j], (cx, cy, c)).wait_recv()
        for cp in sends:
            cp.wait_send()

    return pl.pallas_call(
        body, name="chip_exchange",
        out_shape=[jax.ShapeDtypeStruct((3,) + a.shape[1:], a.dtype) for a in parts],
        in_specs=[ANY] * n, out_specs=[ANY] * n,
        scratch_shapes=[pltpu.SemaphoreType.DMA((3 * n,)), pltpu.SemaphoreType.DMA((3 * n,))])(*parts)


def _chip_sum(name, part, landed, chipflag):
    _, r, width = part.shape
    tm = _div_tile(r, ROW_TILE // 2)

    def kern(p_ref, l_ref, flag_ref, o_ref):
        me = flag_ref[0:1, 0:1]
        own = jnp.where(me == 0.0, p_ref[0], jnp.where(me == 1.0, p_ref[1], jnp.where(me == 2.0, p_ref[2], p_ref[3])))
        o_ref[...] = ((own + l_ref[0].astype(F32)) + l_ref[1].astype(F32)) + l_ref[2].astype(F32)

    return pl.pallas_call(
        kern, name=name, grid=(r // tm,),
        in_specs=[pl.BlockSpec((N_CHIPS, tm, width), lambda i: (0, i, 0)),
                  pl.BlockSpec((3, tm, width), lambda i: (0, i, 0)),
                  pl.BlockSpec((1, LANES), lambda i: (0, 0))],
        out_specs=pl.BlockSpec((tm, width), lambda i: (i, 0)),
        out_shape=jax.ShapeDtypeStruct((r, width), F32), compiler_params=_params(("arbitrary",)))(part, landed, chipflag)


def _pair_broadcast(mine):
    n = len(mine)
    units = _units([a.shape for a in mine])

    def body(*refs):
        ins, outs = refs[:n], refs[n:2 * n]
        send_sems, recv_sems = refs[2 * n:]
        x, y, c, _ = _place()
        copy = functools.partial(_remote, send_sems, recv_sems)
        cps = [copy(u, ins[w].at[pl.ds(r0, nr), :], outs[w].at[pl.ds(r0, nr), :], (x, y, 1 - c))
               for u, (w, r0, nr) in enumerate(units)]
        for cp in cps:
            cp.start()
        for cp in cps:
            cp.wait()

    return pl.pallas_call(
        body, name="pair_broadcast", out_shape=[jax.ShapeDtypeStruct(a.shape, a.dtype) for a in mine],
        in_specs=[ANY] * n, out_specs=[ANY] * n,
        scratch_shapes=[pltpu.SemaphoreType.DMA((len(units),)), pltpu.SemaphoreType.DMA((len(units),))])(*mine)


def _small_allreduce(v):
    offsets = [(dx, dy, dc) for dx in (0, 1) for dy in (0, 1) for dc in (0, 1)][1:]

    def body(v_ref, out_ref, recv_ref, send_sems, recv_sems):
        x, y, c, _ = _place()
        flip = lambda a, d: 1 - a if d else a
        peers = [(flip(x, dx), flip(y, dy), flip(c, dc)) for dx, dy, dc in offsets]
        copy = functools.partial(_remote, send_sems, recv_sems)
        me = 4 * x + 2 * y + c
        recv_ref[me] = v_ref[...]
        cps = [copy(k, v_ref, recv_ref.at[me], peer) for k, peer in enumerate(peers)]
        for cp in cps:
            cp.start()
        for k, (px, py, pc) in enumerate(peers):
            landed = recv_ref.at[4 * px + 2 * py + pc]
            copy(k, landed, landed, (px, py, pc)).wait_recv()
        for cp in cps:
            cp.wait_send()
        tot = recv_ref[0]
        for d in range(1, 8):
            tot = tot + recv_ref[d]
        out_ref[...] = tot

    vmem = pl.BlockSpec(memory_space=pltpu.VMEM)
    return pl.pallas_call(
        body, name="small_allreduce", out_shape=jax.ShapeDtypeStruct(v.shape, v.dtype),
        in_specs=[vmem], out_specs=vmem,
        scratch_shapes=[pltpu.VMEM((8,) + v.shape, v.dtype), pltpu.SemaphoreType.DMA((7,)),
                        pltpu.SemaphoreType.DMA((7,))])(v)


def _adam_math(gv, wv, mv, vv):
    mv = ADAM_B1 * mv + (1.0 - ADAM_B1) * gv
    vv = ADAM_B2 * vv + (1.0 - ADAM_B2) * (gv * gv)
    m_hat = mv / (1.0 - ADAM_B1 ** ADAM_STEP)
    v_hat = vv / (1.0 - ADAM_B2 ** ADAM_STEP)
    return -ADAM_LR * (m_hat / (jnp.sqrt(v_hat) + ADAM_EPS) + ADAM_WD * wv), mv, vv


def _adamw_big(name, mine, theirs, cflag, w, m, v):
    _, r, width = w.shape
    tm = _div_tile(r, ROW_TILE // 2)

    def kern(mine_ref, theirs_ref, flag_ref, w_ref, m_ref, v_ref, g_ref, d_ref, nm_ref, nv_ref):
        layer = pl.program_id(0).astype(F32)
        gv = jnp.where(flag_ref[0:1, 0:1] == layer, mine_ref[...], theirs_ref[...])
        g_ref[0] = gv
        d_ref[0], nm_ref[0], nv_ref[0] = _adam_math(gv, w_ref[0], m_ref[0], v_ref[0])

    flat = pl.BlockSpec((tm, width), lambda l, i: (i, 0))
    stacked = pl.BlockSpec((1, tm, width), lambda l, i: (l, i, 0))
    return pl.pallas_call(
        kern, name=name, grid=(DEPTH, r // tm),
        in_specs=[flat, flat, pl.BlockSpec((1, LANES), lambda l, i: (0, 0)), stacked, stacked, stacked],
        out_specs=[stacked] * 4, out_shape=[jax.ShapeDtypeStruct(w.shape, F32)] * 4,
        compiler_params=_params(("arbitrary", "arbitrary")))(mine, theirs, cflag, w, m, v)


def _adamw_small(g, w, m, v):
    def body(ins, outs, _):
        outs[0][...], outs[1][...], outs[2][...] = _adam_math(*(r[...] for r in ins))
    return _ew("adamw_small", body, [(a, LANES, 0) for a in (g, w, m, v)], [(LANES, F32)] * 3, SMALL_ROWS)


def kernel(x, p, positions, g_mix, w_in, sink, g_q, w_uq, g_kv, w_ukv, w_br_a, w_br_b, w_out, g_ple, w_ple_gate, w_ple_proj, g_final, loss_target, m_g_mix, m_w_in, m_sink, m_g_q, m_w_uq, m_g_kv, m_w_ukv, m_w_br_a, m_w_br_b, m_w_out, m_g_ple, m_w_ple_gate, m_w_ple_proj, m_g_final, v_g_mix, v_w_in, v_sink, v_g_q, v_w_uq, v_g_kv, v_w_ukv, v_w_br_a, v_w_br_b, v_w_out, v_g_ple, v_w_ple_gate, v_w_ple_proj, v_g_final):
    w = dict(g_mix=g_mix, w_in=w_in, sink=sink, g_q=g_q, w_uq=w_uq, g_kv=g_kv, w_ukv=w_ukv, w_br_a=w_br_a,
             w_br_b=w_br_b, w_out=w_out, g_ple=g_ple, w_ple_gate=w_ple_gate, w_ple_proj=w_ple_proj, g_final=g_final)
    m = dict(g_mix=m_g_mix, w_in=m_w_in, sink=m_sink, g_q=m_g_q, w_uq=m_w_uq, g_kv=m_g_kv, w_ukv=m_w_ukv,
             w_br_a=m_w_br_a, w_br_b=m_w_br_b, w_out=m_w_out, g_ple=m_g_ple, w_ple_gate=m_w_ple_gate,
             w_ple_proj=m_w_ple_proj, g_final=m_g_final)
    v = dict(g_mix=v_g_mix, w_in=v_w_in, sink=v_sink, g_q=v_g_q, w_uq=v_w_uq, g_kv=v_g_kv, w_ukv=v_w_ukv,
             w_br_a=v_w_br_a, w_br_b=v_w_br_b, w_out=v_w_out, g_ple=v_g_ple, w_ple_gate=v_w_ple_gate,
             w_ple_proj=v_w_ple_proj, g_final=v_g_final)
    wfull = _gather_full(w)
    sm = {name: w[name] for name in SMALL}
    loss_row, grad_x, layer_grads, dg_final = _local_step(x, p, positions, wfull, sm, loss_target)
    loss = lax.psum(loss_row[0, 0], ("x", "y", "c"))
    res = _update(layer_grads, dg_final, w, m, v)
    return (loss, grad_x, *[res[name][kind] for kind in range(4) for name in WEIGHT_NAMES])


def _gather_behind(shards):
    n = len(shards)
    srcs = [jax.new_ref(s, memory_space=pltpu.MemorySpace.HBM) for s in shards]
    lands = [jax.empty_ref(jax.ShapeDtypeStruct((N_CHIPS,) + s.shape, s.dtype), memory_space=pltpu.MemorySpace.HBM)
             for s in shards]

    @pl.kernel(mesh=plsc.ScalarSubcoreMesh(axis_name="sequencer", num_cores=1), name="gather_behind",
               scratch_types=(pltpu.SemaphoreType.DMA((3 * n,)), pltpu.SemaphoreType.DMA((3 * n,)),
                              pltpu.SemaphoreType.DMA((n,))),
               compiler_params=pltpu.CompilerParams(collective_id=0))
    def launch(send_sems, recv_sems, local_sems):
        x, y, c, chips = _place()
        me = 2 * x + y
        barrier = pltpu.get_barrier_semaphore()
        for cx, cy in chips:
            pl.semaphore_signal(barrier, inc=1, device_id=(cx, cy, c), device_id_type=MESH)
        pl.semaphore_wait(barrier, len(chips))
        copy = functools.partial(_remote, send_sems, recv_sems)
        keeps = [pltpu.make_async_copy(srcs[w], lands[w].at[me], local_sems.at[w]) for w in range(n)]
        cps = [copy(j * n + w, srcs[w], lands[w].at[me], (cx, cy, c))
               for j, (cx, cy) in enumerate(chips) for w in range(n)]
        for cp in keeps + cps:
            cp.start()
        for cp in keeps + cps:
            cp.wait()

    launch()
    return [land[...] for land in lands]


def _gather_full(w):
    shards = [w[name].astype(BF16) for name, _ in SHARDED]
    first, later = _gather_weights([s[0].reshape((2, s.shape[1] // 2) + s.shape[2:]) for s in shards],
                                   [s[1] for s in shards])
    second = _gather_behind(later)
    full = {}
    for k, (name, axis) in enumerate(SHARDED):
        layer0 = first[k].reshape((N_CHIPS,) + shards[k].shape[1:])
        full[name] = [jnp.concatenate(list(blocks), axis=axis - 1) for blocks in (layer0, second[k])]
    return full


def _update(layer_grads, dg_final, w, m, v):
    small_shapes = {name: w[name].shape for name in SMALL}
    cflag = jnp.full((1, LANES), lax.axis_index("c"), F32)
    chipflag = jnp.full((1, LANES), 2 * lax.axis_index("x") + lax.axis_index("y"), F32)

    slots = [[_to_slots(layer_grads[layer][name], axis - 1) for name, axis in SHARDED] for layer in range(DEPTH)]
    theirs = _pair_exchange(slots[0], slots[1])
    pair = [_pair_sum("pair_sum_" + name, slots[0][k], slots[1][k], theirs[k], cflag)
            for k, (name, _) in enumerate(SHARDED)]
    landed = _chip_exchange([bf16 for _, bf16 in pair])
    mine = [_chip_sum("chip_sum_" + name, pair[k][0], landed[k], chipflag) for k, (name, _) in enumerate(SHARDED)]
    other = _pair_broadcast(mine)
    res = {}
    for k, (name, _) in enumerate(SHARDED):
        flip = _flipped(w[name].shape)
        view = (lambda a: jnp.swapaxes(a, -1, -2)) if flip else (lambda a: a)
        outs = _adamw_big("adamw_" + name, view(mine[k]), view(other[k]), cflag, view(w[name]), view(m[name]),
                          view(v[name]))
        res[name] = tuple(view(a) for a in outs)

    gsmall = {name: jnp.stack([layer_grads[layer][name] for layer in range(DEPTH)]) for name in SMALL[:-1]}
    gsmall['g_final'] = dg_final
    gsum = _small_allreduce(_pack_small(gsmall))
    small = (gsum,) + tuple(_adamw_small(gsum, _pack_small(w), _pack_small(m), _pack_small(v)))
    for name, arrs in zip(SMALL, zip(*[[_unpack_small(a, small_shapes)[n] for n in SMALL] for a in small])):
        res[name] = arrs
    return res
```

```python
import functools
import math

import jax
import jax.numpy as jnp
from jax import lax
from jax.experimental import pallas as pl
from jax.experimental.pallas import tpu as pltpu
from jax.experimental.pallas import tpu_sc as plsc

F32 = jnp.float32
BF16 = jnp.bfloat16

D_MODEL = 1024
DEPTH = 2
PLE_DIM = 256
BLOCK = 128
EPS = 1e-6
NEG = -1e30
HEADS = 8
SWA_KV_HEADS = 2
HEAD_DIM = 64
LANES = 128
HPAD = HEADS * LANES
MLA_QK = 96
MLA_ROPE = 32
MLA_Q_LORA = 256
MLA_KV_LORA = 128
ROPE_THETA = 10000.0
IN_SIZES = (512, 128, 128, 512, 256, 128, 32, 512, 1024, 1024)

Z_MA, Z_MB, Z_AQ, Z_AGATE, Z_BGATE = 0, 1024, 2048, 3072, 3584
Z_AK, Z_AV, Z_BQD, Z_BKVD, Z_BKR = 4096, 4352, 4608, 4864, 4992
Z_WIDTH = 5120
GATE_W = HEADS * HEAD_DIM

ADAM_LR, ADAM_B1, ADAM_B2, ADAM_EPS, ADAM_WD, ADAM_STEP = 0.001, 0.9, 0.999, 1e-08, 0.01, 10

VMEM_LIMIT = 56 * 1024 * 1024
MESH = pl.DeviceIdType.MESH

WEIGHT_NAMES = ('g_mix', 'w_in', 'sink', 'g_q', 'w_uq', 'g_kv', 'w_ukv', 'w_br_a', 'w_br_b',
                'w_out', 'g_ple', 'w_ple_gate', 'w_ple_proj', 'g_final')
SHARDED = (('w_in', 2), ('w_uq', 2), ('w_ukv', 2), ('w_br_a', 2), ('w_br_b', 2),
           ('w_out', 1), ('w_ple_gate', 1), ('w_ple_proj', 2))
SMALL = ('g_mix', 'sink', 'g_q', 'g_kv', 'g_ple', 'g_final')
N_CHIPS = 4


def _params(sem):
    return pltpu.CompilerParams(dimension_semantics=sem, vmem_limit_bytes=VMEM_LIMIT)


MM_TN = 512
ROW_TILE = 512
BIG_WEIGHT_BYTES = 8 * 1024 * 1024


def _row_tile(rows, weight_bytes=0):
    tm = ROW_TILE // 2 if weight_bytes > BIG_WEIGHT_BYTES else ROW_TILE
    return min(tm, rows)


def _ew(name, body, ins, outs, rows, accs=(), mms=(), tm=None):
    n_mm, n_in, n_out = len(mms), len(ins), len(outs)
    if tm is None:
        tm = _row_tile(rows, sum(b.size * b.dtype.itemsize for _, b in mms))
    in_specs, args = [], []
    for a, b in mms:
        in_specs += [pl.BlockSpec((tm, a.shape[1]), lambda i: (i, 0)), pl.BlockSpec(b.shape, lambda i: (0, 0))]
        args += [a, b]
    for arr, width, cb in ins:
        if width is None:
            in_specs.append(pl.BlockSpec(arr.shape, lambda i, nd=arr.ndim: (0,) * nd))
        else:
            in_specs.append(pl.BlockSpec((tm, width), lambda i, cb=cb: (i, cb)))
        args.append(arr)
    out_shape, out_specs, aliases = [], [], {}
    for k, out in enumerate(outs):
        if len(out) == 4:
            aliases[len(args)] = k
            in_specs.append(pl.BlockSpec(memory_space=pl.ANY))
            args.append(out[2])
            out_shape.append(jax.ShapeDtypeStruct(out[2].shape, out[2].dtype))
            out_specs.append(pl.BlockSpec((tm, out[0]), lambda i, cb=out[3]: (i, cb)))
        else:
            out_shape.append(jax.ShapeDtypeStruct((rows, out[0]), out[1]))
            out_specs.append(pl.BlockSpec((tm, out[0]), lambda i: (i, 0)))
    n_in += len(aliases)
    out_shape += [jax.ShapeDtypeStruct(s, F32) for s in accs]
    out_specs += [pl.BlockSpec(s, lambda i: (0, 0)) for s in accs]

    def kern(*refs):
        mm_refs, refs = refs[:2 * n_mm], refs[2 * n_mm:]
        in_refs, out_refs = refs[:n_in - len(aliases)], refs[n_in:n_in + n_out]
        acc_refs, prod_refs = refs[n_in + n_out:n_in + n_out + len(accs)], refs[n_in + n_out + len(accs):]
        if acc_refs:
            @pl.when(pl.program_id(0) == 0)
            def _():
                for r in acc_refs:
                    r[...] = jnp.zeros_like(r)
        for k in range(n_mm):
            a_ref, b_ref, prod = mm_refs[2 * k], mm_refs[2 * k + 1], prod_refs[k]
            av = a_ref[...].astype(BF16)
            n = b_ref.shape[1]
            tn = min(MM_TN, n)
            for j in range(n // tn):
                cols = slice(j * tn, (j + 1) * tn)
                prod[:, cols] = jnp.dot(av, b_ref[:, cols], preferred_element_type=F32)
        body(tuple(prod_refs) + tuple(in_refs), out_refs, acc_refs)

    scratch = [pltpu.VMEM((tm, b.shape[1]), F32) for _, b in mms]
    res = pl.pallas_call(kern, name=name, grid=(rows // tm,), in_specs=in_specs, out_specs=out_specs,
                         out_shape=out_shape, scratch_shapes=scratch, input_output_aliases=aliases,
                         compiler_params=_params(("arbitrary",)))(*args)
    return res


def _rms_fwd(name, x, width, cb, g, rows):
    def body(ins, outs, _):
        xv = ins[0][...].astype(F32)
        r = lax.rsqrt(jnp.mean(xv * xv, axis=-1, keepdims=True) + EPS)
        outs[0][...] = ((xv * r) * ins[1][...]).astype(BF16)
    return _ew(name, body, [(x, width, cb), (g.reshape(1, width), None, None)], [(width, BF16)], rows)[0]


def _rms_bwd(name, x, width, cb, g, dh_mm, rows, out_dtype, dres=None, into=()):
    def body(ins, outs, accs):
        dhv, xv, gv = ins[0][...], ins[1][...].astype(F32), ins[2][...]
        r = lax.rsqrt(jnp.mean(xv * xv, axis=-1, keepdims=True) + EPS)
        xhat = xv * r
        accs[0][...] += jnp.sum(dhv * xhat, axis=0, keepdims=True)
        dy = dhv * gv
        dx = r * (dy - xhat * jnp.mean(dy * xhat, axis=-1, keepdims=True))
        if dres is not None:
            dx = dx + ins[3][...]
        outs[0][...] = dx.astype(out_dtype)
    ins = [(x, width, cb), (g.reshape(1, width), None, None)]
    if dres is not None:
        ins.append((dres, width, 0))
    return _ew(name, body, ins, [(width, out_dtype) + tuple(into)], rows, accs=[(1, width)], mms=[dh_mm])


def _mm(name, a, b, out_dtype, residual=None, f32_cols=None, tn=MM_TN):
    M, K = a.shape
    N = b.shape[1]
    tm, tn = _row_tile(M, b.size * b.dtype.itemsize), min(tn, N)
    has_res = residual is not None
    c0, cw = f32_cols if f32_cols else (0, 0)

    def kern(*refs):
        a_ref, b_ref = refs[0], refs[1]
        o_ref = refs[3] if has_res else refs[2]
        av = a_ref[...].astype(BF16)
        for j in range(N // tn):
            cols = slice(j * tn, (j + 1) * tn)
            part = jnp.dot(av, b_ref[:, cols], preferred_element_type=F32)
            if has_res:
                part = part + refs[2][:, cols]
            o_ref[:, cols] = part.astype(o_ref.dtype)
            if c0 <= j * tn and (j + 1) * tn <= c0 + cw:
                refs[-1][:, j * tn - c0:(j + 1) * tn - c0] = part

    in_specs = [pl.BlockSpec((tm, K), lambda i: (i, 0)), pl.BlockSpec((K, N), lambda i: (0, 0))]
    args = [a, b]
    if has_res:
        in_specs.append(pl.BlockSpec((tm, N), lambda i: (i, 0)))
        args.append(residual)
    out_specs = [pl.BlockSpec((tm, N), lambda i: (i, 0))]
    out_shape = [jax.ShapeDtypeStruct((M, N), out_dtype)]
    if f32_cols:
        assert c0 % tn == 0 and cw % tn == 0
        out_specs.append(pl.BlockSpec((tm, cw), lambda i: (i, 0)))
        out_shape.append(jax.ShapeDtypeStruct((M, cw), F32))
    res = pl.pallas_call(kern, name=name, grid=(M // tm,), in_specs=in_specs, out_specs=out_specs,
                         out_shape=out_shape, compiler_params=_params(("parallel",)))(*args)
    return res if f32_cols else res[0]


def _mm_tn(name, a, b, tk=2048, tn=2048):
    T, M = a.shape
    N = b.shape[1]
    tn, tk = min(tn, N), min(tk, T)

    def kern(a_ref, b_ref, o_ref):
        k = pl.program_id(1)
        part = _dot_tn(a_ref[...].astype(BF16), b_ref[...].astype(BF16))

        @pl.when(k == 0)
        def _():
            o_ref[...] = part

        @pl.when(k > 0)
        def _():
            o_ref[...] += part

    return pl.pallas_call(
        kern, name=name, grid=(N // tn, T // tk),
        in_specs=[pl.BlockSpec((tk, M), lambda j, k: (k, 0)), pl.BlockSpec((tk, tn), lambda j, k: (k, j))],
        out_specs=pl.BlockSpec((M, tn), lambda j, k: (0, j)),
        out_shape=jax.ShapeDtypeStruct((M, N), F32),
        compiler_params=_params(("parallel", "arbitrary")))(a, b)


def _dot_nt(a, b):
    return lax.dot_general(a, b, (((1,), (1,)), ((), ())), preferred_element_type=F32)


def _dot_tn(a, b):
    return lax.dot_general(a, b, (((0,), (0,)), ((), ())), preferred_element_type=F32)


SWA_SCALE = HEAD_DIM ** -0.5


def _swa_band(n, pq_ref, pkp_ref, pkc_ref):
    posk = jnp.concatenate([pkp_ref[...], pkc_ref[...]], axis=0)
    dist = (pq_ref[0] - posk).astype(F32)
    kj = lax.broadcasted_iota(jnp.int32, (2 * BLOCK, BLOCK), 0)
    qi = lax.broadcasted_iota(jnp.int32, (2 * BLOCK, BLOCK), 1)
    t_abs = n * BLOCK + qi
    s_abs = n * BLOCK - BLOCK + kj
    return dist, (s_abs >= 0) & (s_abs <= t_abs) & (t_abs - s_abs < BLOCK)


SWA_GROUP = HEADS // SWA_KV_HEADS


def _head_gate(gate_ref, h):
    pair = gate_ref[:, (h // 2) * LANES:(h // 2 + 1) * LANES].astype(F32)
    return pair if h % 2 == 0 else pltpu.roll(pair, HEAD_DIM, 1)


def _swa_group_q(q_all, g):
    heads = range(g * SWA_GROUP, (g + 1) * SWA_GROUP)
    return jnp.concatenate([(q_all[:, h * LANES:(h + 1) * LANES] * SWA_SCALE).astype(BF16) for h in heads], axis=0)


def _swa_mask(s, dist, valid, h):
    return jnp.where(valid, s - (2.0 ** -(h + 1)) * dist, NEG)


def _rows_to_lanes(rows):
    block = jnp.concatenate(list(rows) + [jnp.zeros((LANES - len(rows), BLOCK), F32)], axis=0)
    return block.T


def _swa_specs(nb):
    prev = lambda b, n: b * nb + jnp.maximum(n - 1, 0)
    own = lambda b, n: b * nb + n
    return [
        pl.BlockSpec((BLOCK, HPAD), lambda b, n: (own(b, n), Z_AQ // HPAD)),
        pl.BlockSpec((BLOCK, 256), lambda b, n: (prev(b, n), Z_AK // 256)),
        pl.BlockSpec((BLOCK, 256), lambda b, n: (own(b, n), Z_AK // 256)),
        pl.BlockSpec((BLOCK, 256), lambda b, n: (prev(b, n), Z_AV // 256)),
        pl.BlockSpec((BLOCK, 256), lambda b, n: (own(b, n), Z_AV // 256)),
        pl.BlockSpec((1, 1, BLOCK), lambda b, n: (own(b, n), 0, 0)),
        pl.BlockSpec((BLOCK, 1), lambda b, n: (prev(b, n), 0)),
        pl.BlockSpec((BLOCK, 1), lambda b, n: (own(b, n), 0)),
    ]


def _swa_fwd(z, gate, pos_col, pos_row, sink_row, B, S):
    nb = S // BLOCK
    T = B * S

    def kern(q_ref, kp_ref, kc_ref, vp_ref, vc_ref, pq_ref, pkp_ref, pkc_ref, gate_ref, sink_ref,
             oraw_ref, og_ref, lse_ref):
        q_all = q_ref[...]
        kb = jnp.concatenate([kp_ref[...], kc_ref[...]], axis=0).astype(BF16)
        vb = jnp.concatenate([vp_ref[...], vc_ref[...]], axis=0).astype(BF16)
        dist, valid = _swa_band(pl.program_id(1), pq_ref, pkp_ref, pkc_ref)
        lse_rows = []
        for grp in range(SWA_KV_HEADS):
            gcols = slice(grp * LANES, (grp + 1) * LANES)
            s_all = _dot_nt(kb[:, gcols], _swa_group_q(q_all, grp))
            probs = []
            for hh in range(SWA_GROUP):
                h = grp * SWA_GROUP + hh
                s = _swa_mask(s_all[:, hh * BLOCK:(hh + 1) * BLOCK], dist, valid, h)
                sink_h = sink_ref[0:1, h:h + 1]
                m = jnp.maximum(jnp.max(s, axis=0, keepdims=True), sink_h)
                e = jnp.exp(s - m)
                denom = jnp.sum(e, axis=0, keepdims=True) + jnp.exp(sink_h - m)
                probs.append((e * (1.0 / denom)).astype(BF16))
                lse_rows.append(m + jnp.log(denom))
            o_all = jnp.dot(vb[:, gcols].T, jnp.concatenate(probs, axis=1), preferred_element_type=F32)
            for hh in range(SWA_GROUP):
                h = grp * SWA_GROUP + hh
                cols = slice(h * LANES, (h + 1) * LANES)
                o = o_all[:, hh * BLOCK:(hh + 1) * BLOCK].T
                oraw_ref[:, cols] = o
                g = _head_gate(gate_ref, h)
                og_ref[:, cols] = (o * (g * jax.nn.sigmoid(g))).astype(BF16)
        lse_ref[...] = _rows_to_lanes(lse_rows)

    own = lambda b, n: b * nb + n
    in_specs = _swa_specs(nb) + [
        pl.BlockSpec((BLOCK, GATE_W), lambda b, n: (own(b, n), 0)),
        pl.BlockSpec((1, LANES), lambda b, n: (0, 0)),
    ]
    out_specs = [pl.BlockSpec((BLOCK, HPAD), lambda b, n: (own(b, n), 0)),
                 pl.BlockSpec((BLOCK, HPAD), lambda b, n: (own(b, n), 0)),
                 pl.BlockSpec((BLOCK, LANES), lambda b, n: (own(b, n), 0))]
    out_shape = [jax.ShapeDtypeStruct((T, HPAD), F32), jax.ShapeDtypeStruct((T, HPAD), BF16),
                 jax.ShapeDtypeStruct((T, LANES), F32)]
    return pl.pallas_call(kern, name="swa_fwd", grid=(B, nb), in_specs=in_specs, out_specs=out_specs,
                          out_shape=out_shape, compiler_params=_params(("parallel", "arbitrary")))(
        z, z, z, z, z, pos_row, pos_col, pos_col, gate, sink_row)


def _swa_bwd(z, pos_col, pos_row, sink_row, lse, do_raw, delta, dz, B, S):
    nb = S // BLOCK
    T = B * S

    def kern(q_ref, kp_ref, kc_ref, vp_ref, vc_ref, pq_ref, pkp_ref, pkc_ref, sink_ref, lse_ref, do_ref,
             delta_ref, dz_ref, dq_ref, dk_ref, dv_ref, dsink_ref):
        b, n = pl.program_id(0), pl.program_id(1)

        @pl.when(n == 0)
        def _():
            dk_ref[...] = jnp.zeros_like(dk_ref)
            dv_ref[...] = jnp.zeros_like(dv_ref)

        @pl.when((b == 0) & (n == 0))
        def _():
            dsink_ref[...] = jnp.zeros_like(dsink_ref)

        q_all = q_ref[...]
        kb = jnp.concatenate([kp_ref[...], kc_ref[...]], axis=0).astype(BF16)
        vb = jnp.concatenate([vp_ref[...], vc_ref[...]], axis=0).astype(BF16)
        dist, valid = _swa_band(n, pq_ref, pkp_ref, pkc_ref)
        lse_t, delta_t = lse_ref[...].T, delta_ref[...].T
        lane1 = lax.broadcasted_iota(jnp.int32, (1, LANES), 1)
        dsink = jnp.zeros((1, LANES), F32)
        dk_band, dv_band = [], []
        for grp in range(SWA_KV_HEADS):
            gcols = slice(grp * LANES, (grp + 1) * LANES)
            heads = range(grp * SWA_GROUP, (grp + 1) * SWA_GROUP)
            qg = _swa_group_q(q_all, grp)
            dog = jnp.concatenate([do_ref[:, h * LANES:(h + 1) * LANES] for h in heads], axis=0)
            s_all = _dot_nt(kb[:, gcols], qg)
            dp_all = _dot_nt(vb[:, gcols], dog)
            ps, dss = [], []
            for hh, h in enumerate(heads):
                blk = slice(hh * BLOCK, (hh + 1) * BLOCK)
                lse_h, delta_h = lse_t[h:h + 1, :], delta_t[h:h + 1, :]
                p = jnp.exp(_swa_mask(s_all[:, blk], dist, valid, h) - lse_h)
                ps.append(p.astype(BF16))
                dss.append((p * (dp_all[:, blk] - delta_h)).astype(BF16))
                psink = jnp.exp(sink_ref[0:1, h:h + 1] - lse_h)
                dsink = dsink + jnp.where(lane1 == h, -jnp.sum(psink * delta_h, axis=1, keepdims=True), 0.0)
            dsg = jnp.concatenate(dss, axis=1)
            dq_all = jnp.dot(kb[:, gcols].T, dsg, preferred_element_type=F32) * SWA_SCALE
            for hh, h in enumerate(heads):
                dq_ref[:, h * LANES:(h + 1) * LANES] = dq_all[:, hh * BLOCK:(hh + 1) * BLOCK].T.astype(BF16)
            dk_band.append(jnp.dot(dsg, qg, preferred_element_type=F32))
            dv_band.append(jnp.dot(jnp.concatenate(ps, axis=1), dog, preferred_element_type=F32))
        dsink_ref[...] += dsink
        dkb = jnp.concatenate(dk_band, axis=1)
        dvb = jnp.concatenate(dv_band, axis=1)
        r_prev = pl.ds(pl.multiple_of(jnp.maximum(n - 1, 0) * BLOCK, BLOCK), BLOCK)
        r_own = pl.ds(pl.multiple_of(n * BLOCK, BLOCK), BLOCK)
        dk_ref[r_prev, :] += dkb[:BLOCK]
        dk_ref[r_own, :] += dkb[BLOCK:]
        dv_ref[r_prev, :] += dvb[:BLOCK]
        dv_ref[r_own, :] += dvb[BLOCK:]

    own = lambda b, n: b * nb + n
    in_specs = _swa_specs(nb) + [
        pl.BlockSpec((1, LANES), lambda b, n: (0, 0)),
        pl.BlockSpec((BLOCK, LANES), lambda b, n: (own(b, n), 0)),
        pl.BlockSpec((BLOCK, HPAD), lambda b, n: (own(b, n), 0)),
        pl.BlockSpec((BLOCK, LANES), lambda b, n: (own(b, n), 0)),
        pl.BlockSpec(memory_space=pl.ANY),
    ]
    out_specs = [pl.BlockSpec((BLOCK, HPAD), lambda b, n: (own(b, n), Z_AQ // HPAD)),
                 pl.BlockSpec((S, 256), lambda b, n: (b, 0)),
                 pl.BlockSpec((S, 256), lambda b, n: (b, 0)),
                 pl.BlockSpec((1, LANES), lambda b, n: (0, 0))]
    out_shape = [jax.ShapeDtypeStruct(dz.shape, dz.dtype), jax.ShapeDtypeStruct((T, 256), F32),
                 jax.ShapeDtypeStruct((T, 256), F32), jax.ShapeDtypeStruct((1, LANES), F32)]
    return pl.pallas_call(kern, name="swa_bwd", grid=(B, nb), in_specs=in_specs, out_specs=out_specs,
                          out_shape=out_shape, input_output_aliases={len(in_specs) - 1: 0},
                          compiler_params=_params(("arbitrary", "arbitrary")))(
        z, z, z, z, z, pos_row, pos_col, pos_col, sink_row, lse, do_raw, delta, dz)


MLA_T = 256
MLA_HG = 4
MLA_W = MLA_HG * LANES
MLA_SCALE = MLA_QK ** -0.5
LOG2E = 1.4426950408889634
MLA_QSCALE = MLA_SCALE * LOG2E


def _causal_t(s):
    key = lax.broadcasted_iota(jnp.int32, s.shape, 0)
    query = lax.broadcasted_iota(jnp.int32, s.shape, 1)
    return jnp.where(key <= query, s, NEG)


def _mla_fwd(q, k, v, z, B, S):
    T = B * S
    nq = S // MLA_T

    def kern(q_ref, k_ref, v_ref, gate_ref, oraw_ref, og_ref, lse_ref):
        i = pl.program_id(2)

        def scores(j):
            rows = pl.ds(pl.multiple_of(j * MLA_T, MLA_T), MLA_T)
            return tuple(_dot_nt(k_ref[rows, hh * LANES:(hh + 1) * LANES], q_ref[:, hh * LANES:(hh + 1) * LANES])
                         for hh in range(MLA_HG))

        def update(j, ss, state):
            rows = pl.ds(pl.multiple_of(j * MLA_T, MLA_T), MLA_T)
            out = []
            for hh in range(MLA_HG):
                (m, l, acc), s = state[hh], ss[hh]
                m_new = jnp.maximum(m, jnp.max(s, axis=0, keepdims=True))
                alpha = jnp.exp2(m - m_new)
                p = jnp.exp2(s - m_new)
                l = alpha * l + jnp.sum(p, axis=0, keepdims=True)
                pv = jnp.dot(v_ref[rows, hh * LANES:(hh + 1) * LANES].T, p.astype(BF16), preferred_element_type=F32)
                out.append((m_new, l, alpha * acc + pv))
            return tuple(out)

        def body(j, carry):
            state, ss = carry
            s_next = scores(j + 1)
            return update(j, ss, state), s_next

        init = tuple((jnp.full((1, MLA_T), NEG, F32), jnp.zeros((1, MLA_T), F32), jnp.zeros((LANES, MLA_T), F32))
                     for _ in range(MLA_HG))
        state, ss = lax.fori_loop(0, i, body, (init, scores(0)))
        state = update(i, tuple(_causal_t(s) for s in ss), state)
        for hh in range(MLA_HG):
            m, l, acc = state[hh]
            cols = slice(hh * LANES, (hh + 1) * LANES)
            o = (acc * (1.0 / l)).T
            oraw_ref[:, cols] = o.astype(BF16)
            g = _head_gate(gate_ref, hh)
            og_ref[:, cols] = (o * (g * jax.nn.sigmoid(g))).astype(BF16)
            lse_ref[0, 0, 0, hh:hh + 1, :] = m + jnp.log2(l)

    blk = lambda b, h, i: (b * nq + i, h)
    in_specs = [pl.BlockSpec((MLA_T, MLA_W), blk),
                pl.BlockSpec((S, MLA_W), lambda b, h, i: (b, h)),
                pl.BlockSpec((S, MLA_W), lambda b, h, i: (b, h)),
                pl.BlockSpec((MLA_T, MLA_W // 2), lambda b, h, i: (b * nq + i, Z_BGATE // (MLA_W // 2) + h))]
    out_specs = [pl.BlockSpec((MLA_T, MLA_W), blk), pl.BlockSpec((MLA_T, MLA_W), blk),
                 pl.BlockSpec((1, 1, 1, MLA_HG, MLA_T), lambda b, h, i: (b, h, i, 0, 0))]
    out_shape = [jax.ShapeDtypeStruct((T, HPAD), BF16), jax.ShapeDtypeStruct((T, HPAD), BF16),
                 jax.ShapeDtypeStruct((B, HEADS // MLA_HG, nq, MLA_HG, MLA_T), F32)]
    return pl.pallas_call(kern, name="mla_fwd", grid=(B, HEADS // MLA_HG, nq), in_specs=in_specs,
                          out_specs=out_specs, out_shape=out_shape,
                          compiler_params=_params(("parallel", "parallel", "arbitrary")))(q, k, v, z)


def _mla_bwd(q, k, v, do_raw, lse, delta, B, S):
    T = B * S
    nk = S // MLA_T

    def kern(q_ref, k_ref, v_ref, do_ref, lse_ref, delta_ref, dq_ref, dk_ref, dv_ref, dq_acc, dk_acc, dv_acc):
        j = pl.program_id(2)

        @pl.when(j == 0)
        def _():
            dq_acc[...] = jnp.zeros_like(dq_acc)

        dk_acc[...] = jnp.zeros_like(dk_acc)
        dv_acc[...] = jnp.zeros_like(dv_acc)
        kts = [k_ref[:, hh * LANES:(hh + 1) * LANES].T for hh in range(MLA_HG)]

        def step(i, masked):
            rows = pl.ds(pl.multiple_of(i * MLA_T, MLA_T), MLA_T)
            for hh in range(MLA_HG):
                cols = slice(hh * LANES, (hh + 1) * LANES)
                qv, do = q_ref[rows, cols], do_ref[rows, cols]
                st = _dot_nt(k_ref[:, cols], qv)
                if masked:
                    st = _causal_t(st)
                pt = jnp.exp2(st - lse_ref[0, 0, i, hh:hh + 1, :])
                dpt = _dot_nt(v_ref[:, cols], do)
                dst = (pt * (dpt - delta_ref[0, 0, i, hh:hh + 1, :])).astype(BF16)
                dv_acc[:, cols] += jnp.dot(pt.astype(BF16), do, preferred_element_type=F32)
                dk_acc[:, cols] += jnp.dot(dst, qv, preferred_element_type=F32)
                dq_acc[hh, i] += jnp.dot(kts[hh], dst, preferred_element_type=F32)

        step(j, True)

        def body(i, c):
            step(i, False)
            return c

        lax.fori_loop(j + 1, nk, body, 0)
        dk_ref[...] = (dk_acc[...] * (1.0 / LOG2E)).astype(BF16)
        dv_ref[...] = dv_acc[...].astype(BF16)

        @pl.when(j == nk - 1)
        def _():
            for hh in range(MLA_HG):
                for t in range(nk):
                    dq_ref[t * MLA_T:(t + 1) * MLA_T, hh * LANES:(hh + 1) * LANES] = dq_acc[hh, t].T.astype(BF16)

    whole = lambda b, h, j: (b, h)
    tile = lambda b, h, j: (b * nk + j, h)
    stats = pl.BlockSpec((1, 1, nk, MLA_HG, MLA_T), lambda b, h, j: (b, h, 0, 0, 0))
    in_specs = [pl.BlockSpec((S, MLA_W), whole), pl.BlockSpec((MLA_T, MLA_W), tile),
                pl.BlockSpec((MLA_T, MLA_W), tile), pl.BlockSpec((S, MLA_W), whole), stats, stats]
    out_specs = [pl.BlockSpec((S, MLA_W), whole), pl.BlockSpec((MLA_T, MLA_W), tile),
                 pl.BlockSpec((MLA_T, MLA_W), tile)]
    out_shape = [jax.ShapeDtypeStruct((T, HPAD), BF16)] * 3
    scratch = [pltpu.VMEM((MLA_HG, nk, LANES, MLA_T), F32), pltpu.VMEM((MLA_T, MLA_W), F32),
               pltpu.VMEM((MLA_T, MLA_W), F32)]
    return pl.pallas_call(kern, name="mla_bwd", grid=(B, HEADS // MLA_HG, nk), in_specs=in_specs,
                          out_specs=out_specs, out_shape=out_shape, scratch_shapes=scratch,
                          compiler_params=_params(("parallel", "parallel", "arbitrary")))(
        q, k, v, do_raw, lse, delta)


def _rope_tables(pos_col, inv_lane, rows):
    def body(ins, outs, _):
        ang = ins[0][...].astype(F32) * ins[1][...]
        lane = lax.broadcasted_iota(jnp.int32, ang.shape, 1)
        cos, sin = jnp.cos(ang), jnp.sin(ang)
        first = (lane >= HEAD_DIM) & (lane < HEAD_DIM + MLA_ROPE // 2)
        second = (lane >= HEAD_DIM + MLA_ROPE // 2) & (lane < MLA_QK)
        outs[0][...] = jnp.where(lane < HEAD_DIM, 1.0, jnp.where(lane < MLA_QK, cos, 0.0))
        outs[1][...] = jnp.where(first, -sin, 0.0)
        outs[2][...] = jnp.where(second, sin, 0.0)
    return _ew("rope_tables", body, [(pos_col, 1, 0), (inv_lane, None, None)], [(LANES, F32)] * 3, rows)


def _rope(x, c, s1, s2):
    return x * c + pltpu.roll(x, 112, 1) * s1 + pltpu.roll(x, 16, 1) * s2


def _rope_t(d, c, s1, s2):
    return d * c + pltpu.roll(d * s1, 16, 1) + pltpu.roll(d * s2, 112, 1)


def _mla_prep(qdn, w_uq, kvdn, w_ukv, z, tabs, rows):
    def body(ins, outs, _):
        q_pre, kv_pre = ins[0], ins[1]
        c, s1, s2 = ins[3][...], ins[4][...], ins[5][...]
        kr = _rope(ins[2][...].astype(F32), c, s1, s2)
        for h in range(HEADS):
            cols = slice(h * LANES, (h + 1) * LANES)
            outs[0][:, cols] = (_rope(q_pre[:, cols], c, s1, s2) * MLA_QSCALE).astype(BF16)
            outs[1][:, cols] = (kv_pre[:, cols] + kr).astype(BF16)
        outs[2][...] = kv_pre[:, HPAD:].astype(BF16)
    ins = [(z, LANES, Z_BKR // LANES), (tabs[0], LANES, 0), (tabs[1], LANES, 0), (tabs[2], LANES, 0)]
    return _ew("mla_prep", body, ins, [(HPAD, BF16)] * 3, rows, mms=[(qdn, w_uq), (kvdn, w_ukv)])


def _mla_prep_bwd(dq, dk, dv, tabs, dz, rows):
    def body(ins, outs, _):
        c, s1, s2 = ins[3][...], ins[4][...], ins[5][...]
        lane = lax.broadcasted_iota(jnp.int32, c.shape, 1)
        dkr = jnp.zeros(c.shape, F32)
        for h in range(HEADS):
            cols = slice(h * LANES, (h + 1) * LANES)
            outs[0][:, cols] = _rope_t(ins[0][:, cols].astype(F32) * MLA_SCALE, c, s1, s2).astype(BF16)
            dkh = ins[1][:, cols].astype(F32)
            outs[1][:, cols] = jnp.where(lane < HEAD_DIM, dkh, 0.0).astype(BF16)
            dkr = dkr + dkh
        outs[1][:, HPAD:] = ins[2][...].astype(BF16)
        live = (lane >= HEAD_DIM) & (lane < MLA_QK)
        outs[2][...] = jnp.where(live, _rope_t(jnp.where(live, dkr, 0.0), c, s1, s2), 0.0).astype(BF16)
    ins = [(dq, HPAD, 0), (dk, HPAD, 0), (dv, HPAD, 0), (tabs[0], LANES, 0), (tabs[1], LANES, 0),
           (tabs[2], LANES, 0)]
    outs = [(HPAD, BF16), (2 * HPAD, BF16), (LANES, BF16, dz, Z_BKR // LANES)]
    return _ew("mla_prep_bwd", body, ins, outs, rows)


def _gate_bwd(name, d_o_mm, o_raw, gate, gate_cb, dz, dz_cb, rows):
    def body(ins, outs, _):
        lane = lax.broadcasted_iota(jnp.int32, outs[2].shape, 1)
        delta = jnp.zeros(outs[2].shape, F32)
        d_gate = [None] * HEADS
        for h in range(HEADS):
            cols = slice(h * LANES, (h + 1) * LANES)
            dog, o, g = ins[0][:, cols], ins[1][:, cols].astype(F32), _head_gate(ins[2], h)
            sg = jax.nn.sigmoid(g)
            do = dog * (g * sg)
            outs[0][:, cols] = do.astype(BF16)
            d_gate[h] = dog * o * (sg * (1.0 + g * (1.0 - sg)))
            delta = jnp.where(lane == h, jnp.sum(do * o, axis=-1, keepdims=True), delta)
        for pair in range(HEADS // 2):
            packed = d_gate[2 * pair] + pltpu.roll(d_gate[2 * pair + 1], HEAD_DIM, 1)
            outs[1][:, pair * LANES:(pair + 1) * LANES] = packed.astype(BF16)
        outs[2][...] = delta
    ins = [(o_raw, HPAD, 0), (gate, GATE_W, gate_cb)]
    outs = [(HPAD, BF16), (GATE_W, BF16, dz, dz_cb), (LANES, F32)]
    return _ew(name, body, ins, outs, rows, mms=[d_o_mm])


def _merge_out(ua, ub, z, w_out, x0, rows):
    tm = _row_tile(rows)

    def kern(ua_ref, ub_ref, ma_ref, mb_ref, w_ref, x0_ref, y_ref, x1_ref):
        ua_v, ub_v, m_a, m_b = (r[...].astype(F32) for r in (ua_ref, ub_ref, ma_ref, mb_ref))
        y = (jax.nn.sigmoid(m_a) * ua_v + jax.nn.sigmoid(m_b) * ub_v).astype(BF16)
        y_ref[...] = y
        for j in range(D_MODEL // MM_TN):
            cols = slice(j * MM_TN, (j + 1) * MM_TN)
            x1_ref[:, cols] = jnp.dot(y, w_ref[:, cols], preferred_element_type=F32) + x0_ref[:, cols]

    row = lambda cb: pl.BlockSpec((tm, D_MODEL), lambda i: (i, cb))
    return pl.pallas_call(
        kern, name="merge_out", grid=(rows // tm,),
        in_specs=[row(0), row(0), row(Z_MA // D_MODEL), row(Z_MB // D_MODEL),
                  pl.BlockSpec(w_out.shape, lambda i: (0, 0)), row(0)],
        out_specs=[row(0), row(0)],
        out_shape=[jax.ShapeDtypeStruct((rows, D_MODEL), BF16), jax.ShapeDtypeStruct((rows, D_MODEL), F32)],
        compiler_params=_params(("parallel",)))(ua, ub, z, z, w_out, x0)


def _merge_bwd(dy_mm, ua, ub, z, dz, rows):
    def body(ins, outs, _):
        dyv = ins[0][...]
        for idx in range(2):
            s = jax.nn.sigmoid(ins[3 + idx][...].astype(F32))
            outs[idx][...] = (dyv * s).astype(BF16)
            d_m = (dyv * ins[1 + idx][...].astype(F32) * (s * (1.0 - s))).astype(BF16)
            outs[2][:, idx * D_MODEL:(idx + 1) * D_MODEL] = d_m
    ins = [(ua, D_MODEL, 0), (ub, D_MODEL, 0), (z, D_MODEL, Z_MA // D_MODEL), (z, D_MODEL, Z_MB // D_MODEL)]
    outs = [(D_MODEL, BF16), (D_MODEL, BF16), (2 * D_MODEL, BF16, dz, Z_MA // (2 * D_MODEL))]
    return _ew("merge_bwd", body, ins, outs, rows, mms=[dy_mm])


def _kv_grad_cast(dk, dv, dz, rows):
    def body(ins, outs, _):
        outs[0][:, :256] = ins[0][...].astype(BF16)
        outs[0][:, 256:] = ins[1][...].astype(BF16)
    return _ew("kv_grad_cast", body, [(dk, 256, 0), (dv, 256, 0)], [(512, BF16, dz, Z_AK // 512)], rows)[0]


def _ple_fwd(x1, hn, w_pg, p, w_pp, rows):
    def body(ins, outs, _):
        u, e = ins[0][...], ins[1][...]
        outs[0][...] = ins[2][...] + jax.nn.sigmoid(u) * e
        outs[1][...] = u.astype(BF16)
        outs[2][...] = e.astype(BF16)
    return _ew("ple_fwd", body, [(x1, D_MODEL, 0)], [(D_MODEL, F32), (D_MODEL, BF16), (D_MODEL, BF16)], rows,
               mms=[(hn, w_pg), (p, w_pp)])


def _ple_bwd(dx2, u, e, rows):
    def body(ins, outs, _):
        d, s = ins[0][...], jax.nn.sigmoid(ins[1][...].astype(F32))
        outs[0][...] = (d * s).astype(BF16)
        outs[1][...] = (d * ins[2][...].astype(F32) * (s * (1.0 - s))).astype(BF16)
    return _ew("ple_bwd", body, [(dx2, D_MODEL, 0), (u, D_MODEL, 0), (e, D_MODEL, 0)],
               [(D_MODEL, BF16)] * 2, rows)


def _loss_head(x, g, target, rows):
    def body(ins, outs, accs):
        xv, gv = ins[0][...], ins[1][...]
        r = lax.rsqrt(jnp.mean(xv * xv, axis=-1, keepdims=True) + EPS)
        xhat = xv * r
        err = xhat * gv - ins[2][...]
        accs[0][...] += jnp.broadcast_to(0.5 * jnp.sum(jnp.mean(err * err, axis=-1, keepdims=True),
                                                       axis=0, keepdims=True), (1, LANES))
        dyv = err * (1.0 / D_MODEL)
        accs[1][...] += jnp.sum(dyv * xhat, axis=0, keepdims=True)
        dy = dyv * gv
        outs[0][...] = r * (dy - xhat * jnp.mean(dy * xhat, axis=-1, keepdims=True))
    ins = [(x, D_MODEL, 0), (g.reshape(1, D_MODEL), None, None), (target, D_MODEL, 0)]
    return _ew("loss_head", body, ins, [(D_MODEL, F32)], rows, accs=[(1, LANES), (1, D_MODEL)])


def _pad_heads_cols(w, n_heads, dim):
    k = w.shape[0]
    return jnp.pad(w.reshape(k, n_heads, dim), ((0, 0), (0, 0), (0, LANES - dim))).reshape(k, n_heads * LANES)


def _unpad_heads_cols(w, n_heads, dim):
    k = w.shape[0]
    return w.reshape(k, n_heads, LANES)[:, :, :dim].reshape(k, n_heads * dim)


def _layer_weights(w, i):
    segs = jnp.split(w['w_in'][i], list(_cumsum(IN_SIZES))[:-1], axis=1)
    a_q, a_k, a_v, a_gate, b_qd, b_kvd, b_kr, b_gate, m_a, m_b = segs
    kr = jnp.pad(b_kr, ((0, 0), (HEAD_DIM, LANES - MLA_QK)))
    w_in = jnp.concatenate([
        m_a, m_b, _pad_heads_cols(a_q, HEADS, HEAD_DIM), a_gate, b_gate, _pad_heads_cols(a_k, SWA_KV_HEADS, HEAD_DIM),
        _pad_heads_cols(a_v, SWA_KV_HEADS, HEAD_DIM), b_qd, b_kvd, kr], axis=1)
    w_uq = _pad_heads_cols(w['w_uq'][i], HEADS, MLA_QK)
    ukv = w['w_ukv'][i].reshape(MLA_KV_LORA, HEADS, 2 * HEAD_DIM)
    pad = ((0, 0), (0, 0), (0, HEAD_DIM))
    w_ukv = jnp.concatenate([jnp.pad(ukv[:, :, :HEAD_DIM], pad).reshape(MLA_KV_LORA, HPAD),
                             jnp.pad(ukv[:, :, HEAD_DIM:], pad).reshape(MLA_KV_LORA, HPAD)], axis=1)
    w_br_a = _pad_heads_cols(w['w_br_a'][i].T, HEADS, HEAD_DIM).T
    w_br_b = _pad_heads_cols(w['w_br_b'][i].T, HEADS, HEAD_DIM).T
    out = dict(w_in=w_in, w_uq=w_uq, w_ukv=w_ukv, w_br_a=w_br_a, w_br_b=w_br_b, w_out=w['w_out'][i],
               w_pg=w['w_ple_gate'][i], w_pp=w['w_ple_proj'][i])
    for name in ('w_in', 'w_uq', 'w_ukv', 'w_br_a', 'w_br_b', 'w_out', 'w_pg'):
        out[name + '_t'] = out[name].T
    return out


def _cumsum(sizes):
    acc, out = 0, []
    for s in sizes:
        acc += s
        out.append(acc)
    return out


def _unpad_grads(g):
    d = g['w_in']
    seg = lambda off, width: d[:, off:off + width]
    b_kr = seg(Z_BKR, LANES)[:, HEAD_DIM:MLA_QK]
    w_in = jnp.concatenate([
        _unpad_heads_cols(seg(Z_AQ, HPAD), HEADS, HEAD_DIM), _unpad_heads_cols(seg(Z_AK, 256), SWA_KV_HEADS, HEAD_DIM),
        _unpad_heads_cols(seg(Z_AV, 256), SWA_KV_HEADS, HEAD_DIM), seg(Z_AGATE, GATE_W),
        seg(Z_BQD, MLA_Q_LORA), seg(Z_BKVD, MLA_KV_LORA), b_kr, seg(Z_BGATE, GATE_W),
        seg(Z_MA, D_MODEL), seg(Z_MB, D_MODEL)], axis=1)
    w_uq = _unpad_heads_cols(g['w_uq'], HEADS, MLA_QK)
    ukv = g['w_ukv'].reshape(MLA_KV_LORA, 2, HEADS, LANES)[:, :, :, :HEAD_DIM]
    w_ukv = jnp.concatenate([ukv[:, 0], ukv[:, 1]], axis=-1).reshape(MLA_KV_LORA, HEADS * 2 * HEAD_DIM)
    w_br_a = _unpad_heads_cols(g['w_br_a'].T, HEADS, HEAD_DIM).T
    w_br_b = _unpad_heads_cols(g['w_br_b'].T, HEADS, HEAD_DIM).T
    return dict(w_in=w_in, w_uq=w_uq, w_ukv=w_ukv, w_br_a=w_br_a, w_br_b=w_br_b, w_out=g['w_out'],
                w_ple_gate=g['w_pg'], w_ple_proj=g['w_pp'], g_mix=g['g_mix'], sink=g['sink'], g_q=g['g_q'],
                g_kv=g['g_kv'], g_ple=g['g_ple'])


def _layer_fwd(x0, p_i, lw, sm, i, pos_col, pos_row, tabs, B, S):
    T = B * S
    h = _rms_fwd("norm_mix", x0, D_MODEL, 0, sm['g_mix'][i], T)
    z, a_gate = _mm("proj_in", h, lw['w_in'], BF16, f32_cols=(Z_AGATE, GATE_W))
    sink_row = jnp.pad(sm['sink'][i], (0, LANES - HEADS)).reshape(1, LANES)
    oa_raw, oa, lse_a = _swa_fwd(z, a_gate, pos_col, pos_row, sink_row, B, S)
    qdn = _rms_fwd("norm_q", z, MLA_Q_LORA, Z_BQD // MLA_Q_LORA, sm['g_q'][i], T)
    kvdn = _rms_fwd("norm_kv", z, MLA_KV_LORA, Z_BKVD // MLA_KV_LORA, sm['g_kv'][i], T)
    qf, kf, vf = _mla_prep(qdn, lw['w_uq'], kvdn, lw['w_ukv'], z, tabs, T)
    ob_raw, ob, lse_b = _mla_fwd(qf, kf, vf, z, B, S)
    ua = _mm("proj_br_a", oa, lw['w_br_a'], BF16)
    ub = _mm("proj_br_b", ob, lw['w_br_b'], BF16)
    y, x1 = _merge_out(ua, ub, z, lw['w_out'], x0, T)
    hn = _rms_fwd("norm_ple", x1, D_MODEL, 0, sm['g_ple'][i], T)
    x2, u, e = _ple_fwd(x1, hn, lw['w_pg'], p_i, lw['w_pp'], T)
    saved = dict(x0=x0, h=h, z=z, a_gate=a_gate, sink_row=sink_row, oa_raw=oa_raw, oa=oa, lse_a=lse_a, qdn=qdn, kvdn=kvdn,
                 qf=qf, kf=kf, vf=vf, ob_raw=ob_raw, ob=ob, lse_b=lse_b, ua=ua, ub=ub, y=y, x1=x1, hn=hn,
                 u=u, e=e, p=p_i)
    return x2, saved


def _layer_bwd(dx2, sv, lw, sm, i, pos_col, pos_row, tabs, B, S):
    T = B * S
    z = sv['z']
    g = {}
    d_e, d_u = _ple_bwd(dx2, sv['u'], sv['e'], T)
    g['w_pp'] = _mm_tn("grad_pp", sv['p'], d_e)
    g['w_pg'] = _mm_tn("grad_pg", sv['hn'], d_u)
    dx1, g['g_ple'] = _rms_bwd("norm_ple_bwd", sv['x1'], D_MODEL, 0, sm['g_ple'][i], (d_u, lw['w_pg_t']), T, F32,
                               dres=dx2)
    g['w_out'] = _mm_tn("grad_out", sv['y'], dx1)
    dz = lax.empty((T, Z_WIDTH), BF16)
    d_ua, d_ub, dz = _merge_bwd((dx1, lw['w_out_t']), sv['ua'], sv['ub'], z, dz, T)
    g['w_br_a'] = _mm_tn("grad_br_a", sv['oa'], d_ua)
    g['w_br_b'] = _mm_tn("grad_br_b", sv['ob'], d_ub)
    dob_raw, dz, delta_b = _gate_bwd("gate_b_bwd", (d_ub, lw['w_br_b_t']), sv['ob_raw'], z, Z_BGATE // GATE_W,
                                     dz, Z_BGATE // GATE_W, T)
    delta_rows = delta_b[:, :HEADS].reshape(B, S // MLA_T, MLA_T, HEADS // MLA_HG, MLA_HG).transpose(0, 3, 1, 4, 2)
    dq, dk, dv = _mla_bwd(sv['qf'], sv['kf'], sv['vf'], dob_raw, sv['lse_b'], delta_rows, B, S)
    dq_pre, dkv_pre, dz = _mla_prep_bwd(dq, dk, dv, tabs, dz, T)
    g['w_uq'] = _mm_tn("grad_uq", sv['qdn'], dq_pre)
    g['w_ukv'] = _mm_tn("grad_ukv", sv['kvdn'], dkv_pre)
    dz, g['g_q'] = _rms_bwd("norm_q_bwd", z, MLA_Q_LORA, Z_BQD // MLA_Q_LORA, sm['g_q'][i],
                            (dq_pre, lw['w_uq_t']), T, BF16, into=(dz, Z_BQD // MLA_Q_LORA))
    dz, g['g_kv'] = _rms_bwd("norm_kv_bwd", z, MLA_KV_LORA, Z_BKVD // MLA_KV_LORA, sm['g_kv'][i],
                             (dkv_pre, lw['w_ukv_t']), T, BF16, into=(dz, Z_BKVD // MLA_KV_LORA))
    doa_raw, dz, delta_a = _gate_bwd("gate_a_bwd", (d_ua, lw['w_br_a_t']), sv['oa_raw'], sv['a_gate'], 0,
                                     dz, Z_AGATE // GATE_W, T)
    dz, d_ak, d_av, dsink = _swa_bwd(z, pos_col, pos_row, sv['sink_row'], sv['lse_a'], doa_raw, delta_a, dz, B, S)
    dz = _kv_grad_cast(d_ak, d_av, dz, T)
    g['sink'] = dsink[0, :HEADS]
    g['w_in'] = _mm_tn("grad_in", sv['h'], dz, tk=1024, tn=Z_WIDTH // 2)
    dx0, g['g_mix'] = _rms_bwd("norm_mix_bwd", sv['x0'], D_MODEL, 0, sm['g_mix'][i], (dz, lw['w_in_t']), T, F32,
                               dres=dx1)
    for name in ('g_ple', 'g_q', 'g_kv', 'g_mix'):
        g[name] = g[name][0]
    return dx0, g


def _local_step(x, p, positions, wfull, sm, loss_target):
    B, S, _ = x.shape
    T = B * S
    pos_col = positions.reshape(T, 1)
    pos_row = positions.reshape(T // BLOCK, 1, BLOCK)
    half = MLA_ROPE // 2
    inv = ROPE_THETA ** (-jnp.arange(0, MLA_ROPE, 2, dtype=F32) / MLA_ROPE)
    inv_lane = jnp.tile(inv, LANES // half).reshape(1, LANES)
    tabs = _rope_tables(pos_col, inv_lane, T)
    xc = x.reshape(T, D_MODEL)
    lws, saved = [], []
    for i in range(DEPTH):
        lw = _layer_weights(wfull, i)
        xc, sv = _layer_fwd(xc, p[i].reshape(T, PLE_DIM), lw, sm, i, pos_col, pos_row, tabs, B, S)
        lws.append(lw)
        saved.append(sv)
    dx, loss, dg_final = _loss_head(xc, sm['g_final'], loss_target.reshape(T, D_MODEL), T)
    layer_grads = [None] * DEPTH
    for i in reversed(range(DEPTH)):
        dx, g = _layer_bwd(dx, saved[i], lws[i], sm, i, pos_col, pos_row, tabs, B, S)
        layer_grads[i] = _unpad_grads(g)
    return loss, dx.reshape(B, S, D_MODEL), layer_grads, dg_final[0]


SMALL_ROWS = 48


SMALL_SIZE = 2 * (2 * D_MODEL + HEADS + MLA_Q_LORA + MLA_KV_LORA) + D_MODEL


def _pack_small(arrs, tail=()):
    flat = jnp.concatenate([arrs[name].reshape(-1) for name in SMALL] + [t.reshape(1) for t in tail])
    return jnp.pad(flat, (0, SMALL_ROWS * LANES - flat.shape[0])).reshape(SMALL_ROWS, LANES)


def _unpack_small(block, shapes):
    flat = block.reshape(-1)
    out, off = {}, 0
    for name in SMALL:
        n = math.prod(shapes[name])
        out[name] = flat[off:off + n].reshape(shapes[name])
        off += n
    return out


def _flipped(shard_shape):
    return shard_shape[-1] % LANES != 0


def _to_slots(g, axis):
    r, c = g.shape
    if axis == 0:
        return g.reshape(N_CHIPS, r // N_CHIPS, c)
    return g.reshape(r, N_CHIPS, c // N_CHIPS).transpose(1, 0, 2)


def _div_tile(rows, cap):
    return next(t for t in range(min(cap, rows) // 8 * 8, 0, -8) if rows % t == 0)


def _units(shapes):
    units = []
    for w, shape in enumerate(shapes):
        r = shape[-2]
        n = next(n for n in (8, 7, 4, 2, 1) if r % (8 * n) == 0) if r >= 1024 else 1
        units += [(w, k * (r // n), r // n) for k in range(n)]
    return units


def _place():
    x, y, c = lax.axis_index("x"), lax.axis_index("y"), lax.axis_index("c")
    chips = [(1 - x, y), (x, 1 - y), (1 - x, 1 - y)]
    return x, y, c, chips


ANY = pl.BlockSpec(memory_space=pl.ANY)


def _remote(send_sems, recv_sems, k, src, dst, to):
    return pltpu.make_async_remote_copy(src_ref=src, dst_ref=dst, send_sem=send_sems.at[k],
                                        recv_sem=recv_sems.at[k], device_id=to, device_id_type=MESH)


def _gather_weights(shards, carried):
    n, nc = len(shards), len(carried)
    units = _units([s.shape for s in shards])
    nu = len(units)

    def body(*refs):
        ins, outs = refs[:n], refs[n + nc:2 * n + nc]
        send_sems, recv_sems, local_sems = refs[2 * (n + nc):]
        x, y, c, chips = _place()
        me = 2 * x + y
        sibling = (x, y, 1 - c)
        copy = functools.partial(_remote, send_sems, recv_sems)
        keeps, sends = [], []
        for u, (w, r0, nr) in enumerate(units):
            rows = pl.ds(r0, nr)
            keeps.append(pltpu.make_async_copy(ins[w].at[:, rows, :], outs[w].at[me, :, rows, :], local_sems.at[u]))
            keeps[-1].start()
        for j, (cx, cy) in enumerate(chips):
            for u, (w, r0, nr) in enumerate(units):
                rows = pl.ds(r0, nr)
                sends.append(copy(j * nu + u, ins[w].at[c, rows, :], outs[w].at[me, c, rows, :], (cx, cy, c)))
                sends[-1].start()
        for j, (cx, cy) in enumerate(chips):
            for u, (w, r0, nr) in enumerate(units):
                landed = outs[w].at[2 * cx + cy, c, pl.ds(r0, nr), :]
                copy(j * nu + u, landed, landed, (cx, cy, c)).wait_recv()
                sends.append(copy((3 + j) * nu + u, landed, landed, sibling))
                sends[-1].start()
        for j, (cx, cy) in enumerate(chips):
            for u, (w, r0, nr) in enumerate(units):
                other = outs[w].at[2 * cx + cy, 1 - c, pl.ds(r0, nr), :]
                copy((3 + j) * nu + u, other, other, sibling).wait_recv()
        for cp in sends:
            cp.wait_send()
        for keep in keeps:
            keep.wait()

    out_shape = [jax.ShapeDtypeStruct((N_CHIPS,) + s.shape, s.dtype) for s in shards]
    out_shape += [jax.ShapeDtypeStruct(a.shape, a.dtype) for a in carried]
    res = pl.pallas_call(
        body, name="gather_weights", out_shape=out_shape,
        in_specs=[ANY] * (n + nc), out_specs=[ANY] * (n + nc),
        input_output_aliases={n + k: n + k for k in range(nc)},
        scratch_shapes=[pltpu.SemaphoreType.DMA((6 * nu,)), pltpu.SemaphoreType.DMA((6 * nu,)),
                        pltpu.SemaphoreType.DMA((nu,))])(*shards, *carried)
    return res[:n], res[n:]


def _pair_exchange(g0, g1):
    n = len(g0)

    def body(*refs):
        layers, outs = (refs[:n], refs[n:2 * n]), refs[2 * n:3 * n]
        send_sems, recv_sems = refs[3 * n:]
        x, y, c, _ = _place()
        copy = functools.partial(_remote, send_sems, recv_sems)
        for w in range(n):
            for q in range(N_CHIPS):
                for layer in range(DEPTH):
                    cp = copy(N_CHIPS * w + q, layers[layer][w].at[q], outs[w].at[q], (x, y, 1 - c))
                    pl.when(c == 1 - layer)(cp.start)
        for w in range(n):
            for q in range(N_CHIPS):
                copy(N_CHIPS * w + q, layers[0][w].at[q], outs[w].at[q], (x, y, 1 - c)).wait()

    return pl.pallas_call(
        body, name="pair_exchange", out_shape=[jax.ShapeDtypeStruct(g.shape, g.dtype) for g in g0],
        in_specs=[ANY] * (2 * n), out_specs=[ANY] * n,
        scratch_shapes=[pltpu.SemaphoreType.DMA((N_CHIPS * n,)), pltpu.SemaphoreType.DMA((N_CHIPS * n,))])(*g0, *g1)


def _pair_sum(name, g0, g1, theirs, cflag):
    shape = theirs.shape
    rows, width = shape[0] * shape[1], shape[2]

    def body(ins, outs, _):
        mine = jnp.where(ins[3][0:1, 0:1] == 0.0, ins[0][...], ins[1][...])
        tot = mine + ins[2][...]
        outs[0][...] = tot
        outs[1][...] = tot.astype(BF16)
    ins = [(a.reshape(rows, width), width, 0) for a in (g0, g1, theirs)] + [(cflag, None, None)]
    f32, bf16 = _ew(name, body, ins, [(width, F32), (width, BF16)], rows, tm=_div_tile(rows, ROW_TILE))
    return f32.reshape(shape), bf16.reshape(shape)


def _chip_exchange(parts):
    n = len(parts)

    def body(*refs):
        ins, outs = refs[:n], refs[n:2 * n]
        send_sems, recv_sems = refs[2 * n:]
        x, y, c, chips = _place()
        copy = functools.partial(_remote, send_sems, recv_sems)
        sends = []
        for j, (cx, cy) in enumerate(chips):
            for w in range(n):
                sends.append(copy(j * n + w, ins[w].at[2 * cx + cy], outs[w].at[j], (cx, cy, c)))
                sends[-1].start()
        for j, (cx, cy) in enumerate(chips):
            for w in range(n):
                copy(j * n + w, outs[w].at[j], outs[w].at[j], (cx, cy, c)).wait_recv()
        for cp in sends:
            cp.wait_send()

    return pl.pallas_call(
        body, name="chip_exchange",
        out_shape=[jax.ShapeDtypeStruct((3,) + a.shape[1:], a.dtype) for a in parts],
        in_specs=[ANY] * n, out_specs=[ANY] * n,
        scratch_shapes=[pltpu.SemaphoreType.DMA((3 * n,)), pltpu.SemaphoreType.DMA((3 * n,))])(*parts)


def _chip_sum(name, part, landed, chipflag):
    _, r, width = part.shape
    tm = _div_tile(r, ROW_TILE // 2)

    def kern(p_ref, l_ref, flag_ref, o_ref):
        me = flag_ref[0:1, 0:1]
        own = jnp.where(me == 0.0, p_ref[0], jnp.where(me == 1.0, p_ref[1], jnp.where(me == 2.0, p_ref[2], p_ref[3])))
        o_ref[...] = ((own + l_ref[0].astype(F32)) + l_ref[1].astype(F32)) + l_ref[2].astype(F32)

    return pl.pallas_call(
        kern, name=name, grid=(r // tm,),
        in_specs=[pl.BlockSpec((N_CHIPS, tm, width), lambda i: (0, i, 0)),
                  pl.BlockSpec((3, tm, width), lambda i: (0, i, 0)),
                  pl.BlockSpec((1, LANES), lambda i: (0, 0))],
        out_specs=pl.BlockSpec((tm, width), lambda i: (i, 0)),
        out_shape=jax.ShapeDtypeStruct((r, width), F32), compiler_params=_params(("arbitrary",)))(part, landed, chipflag)


def _pair_broadcast(mine):
    n = len(mine)
    units = _units([a.shape for a in mine])

    def body(*refs):
        ins, outs = refs[:n], refs[n:2 * n]
        send_sems, recv_sems = refs[2 * n:]
        x, y, c, _ = _place()
        copy = functools.partial(_remote, send_sems, recv_sems)
        cps = [copy(u, ins[w].at[pl.ds(r0, nr), :], outs[w].at[pl.ds(r0, nr), :], (x, y, 1 - c))
               for u, (w, r0, nr) in enumerate(units)]
        for cp in cps:
            cp.start()
        for cp in cps:
            cp.wait()

    return pl.pallas_call(
        body, name="pair_broadcast", out_shape=[jax.ShapeDtypeStruct(a.shape, a.dtype) for a in mine],
        in_specs=[ANY] * n, out_specs=[ANY] * n,
        scratch_shapes=[pltpu.SemaphoreType.DMA((len(units),)), pltpu.SemaphoreType.DMA((len(units),))])(*mine)


def _small_allreduce(v):
    offsets = [(dx, dy, dc) for dx in (0, 1) for dy in (0, 1) for dc in (0, 1)][1:]

    def body(v_ref, out_ref, recv_ref, send_sems, recv_sems):
        x, y, c, _ = _place()
        flip = lambda a, d: 1 - a if d else a
        peers = [(flip(x, dx), flip(y, dy), flip(c, dc)) for dx, dy, dc in offsets]
        copy = functools.partial(_remote, send_sems, recv_sems)
        me = 4 * x + 2 * y + c
        recv_ref[me] = v_ref[...]
        cps = [copy(k, v_ref, recv_ref.at[me], peer) for k, peer in enumerate(peers)]
        for cp in cps:
            cp.start()
        for k, (px, py, pc) in enumerate(peers):
            landed = recv_ref.at[4 * px + 2 * py + pc]
            copy(k, landed, landed, (px, py, pc)).wait_recv()
        for cp in cps:
            cp.wait_send()
        tot = recv_ref[0]
        for d in range(1, 8):
            tot = tot + recv_ref[d]
        out_ref[...] = tot

    vmem = pl.BlockSpec(memory_space=pltpu.VMEM)
    return pl.pallas_call(
        body, name="small_allreduce", out_shape=jax.ShapeDtypeStruct(v.shape, v.dtype),
        in_specs=[vmem], out_specs=vmem,
        scratch_shapes=[pltpu.VMEM((8,) + v.shape, v.dtype), pltpu.SemaphoreType.DMA((7,)),
                        pltpu.SemaphoreType.DMA((7,))])(v)


def _adam_math(gv, wv, mv, vv):
    mv = ADAM_B1 * mv + (1.0 - ADAM_B1) * gv
    vv = ADAM_B2 * vv + (1.0 - ADAM_B2) * (gv * gv)
    m_hat = mv / (1.0 - ADAM_B1 ** ADAM_STEP)
    v_hat = vv / (1.0 - ADAM_B2 ** ADAM_STEP)
    return -ADAM_LR * (m_hat / (jnp.sqrt(v_hat) + ADAM_EPS) + ADAM_WD * wv), mv, vv


def _adamw_big(name, mine, theirs, cflag, w, m, v):
    _, r, width = w.shape
    tm = _div_tile(r, ROW_TILE // 2)

    def kern(mine_ref, theirs_ref, flag_ref, w_ref, m_ref, v_ref, g_ref, d_ref, nm_ref, nv_ref):
        layer = pl.program_id(0).astype(F32)
        gv = jnp.where(flag_ref[0:1, 0:1] == layer, mine_ref[...], theirs_ref[...])
        g_ref[0] = gv
        d_ref[0], nm_ref[0], nv_ref[0] = _adam_math(gv, w_ref[0], m_ref[0], v_ref[0])

    flat = pl.BlockSpec((tm, width), lambda l, i: (i, 0))
    stacked = pl.BlockSpec((1, tm, width), lambda l, i: (l, i, 0))
    return pl.pallas_call(
        kern, name=name, grid=(DEPTH, r // tm),
        in_specs=[flat, flat, pl.BlockSpec((1, LANES), lambda l, i: (0, 0)), stacked, stacked, stacked],
        out_specs=[stacked] * 4, out_shape=[jax.ShapeDtypeStruct(w.shape, F32)] * 4,
        compiler_params=_params(("arbitrary", "arbitrary")))(mine, theirs, cflag, w, m, v)


def _adamw_small(g, w, m, v):
    def body(ins, outs, _):
        outs[0][...], outs[1][...], outs[2][...] = _adam_math(*(r[...] for r in ins))
    return _ew("adamw_small", body, [(a, LANES, 0) for a in (g, w, m, v)], [(LANES, F32)] * 3, SMALL_ROWS)


def kernel(x, p, positions, g_mix, w_in, sink, g_q, w_uq, g_kv, w_ukv, w_br_a, w_br_b, w_out, g_ple, w_ple_gate, w_ple_proj, g_final, loss_target, m_g_mix, m_w_in, m_sink, m_g_q, m_w_uq, m_g_kv, m_w_ukv, m_w_br_a, m_w_br_b, m_w_out, m_g_ple, m_w_ple_gate, m_w_ple_proj, m_g_final, v_g_mix, v_w_in, v_sink, v_g_q, v_w_uq, v_g_kv, v_w_ukv, v_w_br_a, v_w_br_b, v_w_out, v_g_ple, v_w_ple_gate, v_w_ple_proj, v_g_final):
    w = dict(g_mix=g_mix, w_in=w_in, sink=sink, g_q=g_q, w_uq=w_uq, g_kv=g_kv, w_ukv=w_ukv, w_br_a=w_br_a,
             w_br_b=w_br_b, w_out=w_out, g_ple=g_ple, w_ple_gate=w_ple_gate, w_ple_proj=w_ple_proj, g_final=g_final)
    m = dict(g_mix=m_g_mix, w_in=m_w_in, sink=m_sink, g_q=m_g_q, w_uq=m_w_uq, g_kv=m_g_kv, w_ukv=m_w_ukv,
             w_br_a=m_w_br_a, w_br_b=m_w_br_b, w_out=m_w_out, g_ple=m_g_ple, w_ple_gate=m_w_ple_gate,
             w_ple_proj=m_w_ple_proj, g_final=m_g_final)
    v = dict(g_mix=v_g_mix, w_in=v_w_in, sink=v_sink, g_q=v_g_q, w_uq=v_w_uq, g_kv=v_g_kv, w_ukv=v_w_ukv,
             w_br_a=v_w_br_a, w_br_b=v_w_br_b, w_out=v_w_out, g_ple=v_g_ple, w_ple_gate=v_w_ple_gate,
             w_ple_proj=v_w_ple_proj, g_final=v_g_final)
    wfull = _gather_full(w)
    sm = {name: w[name] for name in SMALL}
    loss_row, grad_x, layer_grads, dg_final = _local_step(x, p, positions, wfull, sm, loss_target)
    res, loss = _update(layer_grads, dg_final, loss_row[0, 0], w, m, v)
    return (loss, grad_x, *[res[name][kind] for kind in range(4) for name in WEIGHT_NAMES])


def _gather_behind(shards):
    n = len(shards)
    srcs = [jax.new_ref(s, memory_space=pltpu.MemorySpace.HBM) for s in shards]
    lands = [jax.empty_ref(jax.ShapeDtypeStruct((N_CHIPS,) + s.shape, s.dtype), memory_space=pltpu.MemorySpace.HBM)
             for s in shards]

    @pl.kernel(mesh=plsc.ScalarSubcoreMesh(axis_name="sequencer", num_cores=1), name="gather_behind",
               scratch_types=(pltpu.SemaphoreType.DMA((3 * n,)), pltpu.SemaphoreType.DMA((3 * n,)),
                              pltpu.SemaphoreType.DMA((n,))),
               compiler_params=pltpu.CompilerParams(collective_id=0))
    def launch(send_sems, recv_sems, local_sems):
        x, y, c, chips = _place()
        me = 2 * x + y
        barrier = pltpu.get_barrier_semaphore()
        for cx, cy in chips:
            pl.semaphore_signal(barrier, inc=1, device_id=(cx, cy, c), device_id_type=MESH)
        pl.semaphore_wait(barrier, len(chips))
        copy = functools.partial(_remote, send_sems, recv_sems)
        keeps = [pltpu.make_async_copy(srcs[w], lands[w].at[me], local_sems.at[w]) for w in range(n)]
        cps = [copy(j * n + w, srcs[w], lands[w].at[me], (cx, cy, c))
               for j, (cx, cy) in enumerate(chips) for w in range(n)]
        for cp in keeps + cps:
            cp.start()
        for cp in keeps + cps:
            cp.wait()

    launch()
    return [land[...] for land in lands]


def _gather_full(w):
    shards = [w[name].astype(BF16) for name, _ in SHARDED]
    first, later = _gather_weights([s[0].reshape((2, s.shape[1] // 2) + s.shape[2:]) for s in shards],
                                   [s[1] for s in shards])
    second = _gather_behind(later)
    full = {}
    for k, (name, axis) in enumerate(SHARDED):
        layer0 = first[k].reshape((N_CHIPS,) + shards[k].shape[1:])
        full[name] = [jnp.concatenate(list(blocks), axis=axis - 1) for blocks in (layer0, second[k])]
    return full


def _update(layer_grads, dg_final, loss_local, w, m, v):
    small_shapes = {name: w[name].shape for name in SMALL}
    cflag = jnp.full((1, LANES), lax.axis_index("c"), F32)
    chipflag = jnp.full((1, LANES), 2 * lax.axis_index("x") + lax.axis_index("y"), F32)

    slots = [[_to_slots(layer_grads[layer][name], axis - 1) for name, axis in SHARDED] for layer in range(DEPTH)]
    theirs = _pair_exchange(slots[0], slots[1])
    pair = [_pair_sum("pair_sum_" + name, slots[0][k], slots[1][k], theirs[k], cflag)
            for k, (name, _) in enumerate(SHARDED)]
    landed = _chip_exchange([bf16 for _, bf16 in pair])
    mine = [_chip_sum("chip_sum_" + name, pair[k][0], landed[k], chipflag) for k, (name, _) in enumerate(SHARDED)]
    other = _pair_broadcast(mine)
    res = {}
    for k, (name, _) in enumerate(SHARDED):
        flip = _flipped(w[name].shape)
        view = (lambda a: jnp.swapaxes(a, -1, -2)) if flip else (lambda a: a)
        outs = _adamw_big("adamw_" + name, view(mine[k]), view(other[k]), cflag, view(w[name]), view(m[name]),
                          view(v[name]))
        res[name] = tuple(view(a) for a in outs)

    gsmall = {name: jnp.stack([layer_grads[layer][name] for layer in range(DEPTH)]) for name in SMALL[:-1]}
    gsmall['g_final'] = dg_final
    gsum = _small_allreduce(_pack_small(gsmall, tail=[loss_local]))
    small = (gsum,) + tuple(_adamw_small(gsum, _pack_small(w), _pack_small(m), _pack_small(v)))
    for name, arrs in zip(SMALL, zip(*[[_unpack_small(a, small_shapes)[n] for n in SMALL] for a in small])):
        res[name] = arrs
    return res, gsum.reshape(-1)[SMALL_SIZE]
```

```python
import functools
import math

import jax
import jax.numpy as jnp
from jax import lax
from jax.experimental import pallas as pl
from jax.experimental.pallas import tpu as pltpu
from jax.experimental.pallas import tpu_sc as plsc

F32 = jnp.float32
BF16 = jnp.bfloat16

D_MODEL = 1024
DEPTH = 2
PLE_DIM = 256
BLOCK = 128
EPS = 1e-6
NEG = -1e30
HEADS = 8
SWA_KV_HEADS = 2
HEAD_DIM = 64
LANES = 128
HPAD = HEADS * LANES
MLA_QK = 96
MLA_ROPE = 32
MLA_Q_LORA = 256
MLA_KV_LORA = 128
ROPE_THETA = 10000.0
IN_SIZES = (512, 128, 128, 512, 256, 128, 32, 512, 1024, 1024)

Z_MA, Z_MB, Z_AQ, Z_AGATE, Z_BGATE = 0, 1024, 2048, 3072, 3584
Z_AK, Z_AV, Z_BQD, Z_BKVD, Z_BKR = 4096, 4352, 4608, 4864, 4992
Z_WIDTH = 5120
GATE_W = HEADS * HEAD_DIM
KV_W = SWA_KV_HEADS * LANES

ADAM_LR, ADAM_B1, ADAM_B2, ADAM_EPS, ADAM_WD, ADAM_STEP = 0.001, 0.9, 0.999, 1e-08, 0.01, 10

VMEM_LIMIT = 56 * 1024 * 1024
MESH = pl.DeviceIdType.MESH

WEIGHT_NAMES = ('g_mix', 'w_in', 'sink', 'g_q', 'w_uq', 'g_kv', 'w_ukv', 'w_br_a', 'w_br_b',
                'w_out', 'g_ple', 'w_ple_gate', 'w_ple_proj', 'g_final')
SHARDED = (('w_in', 2), ('w_uq', 2), ('w_ukv', 2), ('w_br_a', 2), ('w_br_b', 2),
           ('w_out', 1), ('w_ple_gate', 1), ('w_ple_proj', 2))
SMALL = ('g_mix', 'sink', 'g_q', 'g_kv', 'g_ple', 'g_final')
N_CHIPS = 4


def _params(sem):
    return pltpu.CompilerParams(dimension_semantics=sem, vmem_limit_bytes=VMEM_LIMIT)


MM_TN = 512
ROW_TILE = 512
BIG_WEIGHT_BYTES = 8 * 1024 * 1024


def _row_tile(rows, weight_bytes=0):
    tm = ROW_TILE // 2 if weight_bytes > BIG_WEIGHT_BYTES else ROW_TILE
    return min(tm, rows)


def _ew(name, body, ins, outs, rows, accs=(), mms=(), tm=None):
    n_mm, n_in, n_out = len(mms), len(ins), len(outs)
    if tm is None:
        tm = _row_tile(rows, sum(b.size * b.dtype.itemsize for _, b in mms))
    in_specs, args = [], []
    for a, b in mms:
        in_specs += [pl.BlockSpec((tm, a.shape[1]), lambda i: (i, 0)), pl.BlockSpec(b.shape, lambda i: (0, 0))]
        args += [a, b]
    for arr, width, cb in ins:
        if width is None:
            in_specs.append(pl.BlockSpec(arr.shape, lambda i, nd=arr.ndim: (0,) * nd))
        else:
            in_specs.append(pl.BlockSpec((tm, width), lambda i, cb=cb: (i, cb)))
        args.append(arr)
    out_shape, out_specs, aliases = [], [], {}
    for k, out in enumerate(outs):
        if len(out) == 4:
            aliases[len(args)] = k
            in_specs.append(pl.BlockSpec(memory_space=pl.ANY))
            args.append(out[2])
            out_shape.append(jax.ShapeDtypeStruct(out[2].shape, out[2].dtype))
            out_specs.append(pl.BlockSpec((tm, out[0]), lambda i, cb=out[3]: (i, cb)))
        else:
            out_shape.append(jax.ShapeDtypeStruct((rows, out[0]), out[1]))
            out_specs.append(pl.BlockSpec((tm, out[0]), lambda i: (i, 0)))
    n_in += len(aliases)
    out_shape += [jax.ShapeDtypeStruct(s, F32) for s in accs]
    out_specs += [pl.BlockSpec(s, lambda i: (0, 0)) for s in accs]

    def kern(*refs):
        mm_refs, refs = refs[:2 * n_mm], refs[2 * n_mm:]
        in_refs, out_refs = refs[:n_in - len(aliases)], refs[n_in:n_in + n_out]
        acc_refs, prod_refs = refs[n_in + n_out:n_in + n_out + len(accs)], refs[n_in + n_out + len(accs):]
        if acc_refs:
            @pl.when(pl.program_id(0) == 0)
            def _():
                for r in acc_refs:
                    r[...] = jnp.zeros_like(r)
        for k in range(n_mm):
            a_ref, b_ref, prod = mm_refs[2 * k], mm_refs[2 * k + 1], prod_refs[k]
            av = a_ref[...].astype(BF16)
            n = b_ref.shape[1]
            tn = min(MM_TN, n)
            for j in range(n // tn):
                cols = slice(j * tn, (j + 1) * tn)
                prod[:, cols] = jnp.dot(av, b_ref[:, cols], preferred_element_type=F32)
        body(tuple(prod_refs) + tuple(in_refs), out_refs, acc_refs)

    scratch = [pltpu.VMEM((tm, b.shape[1]), F32) for _, b in mms]
    res = pl.pallas_call(kern, name=name, grid=(rows // tm,), in_specs=in_specs, out_specs=out_specs,
                         out_shape=out_shape, scratch_shapes=scratch, input_output_aliases=aliases,
                         compiler_params=_params(("arbitrary",)))(*args)
    return res


def _rms(xv, gv):
    r = lax.rsqrt(jnp.mean(xv * xv, axis=-1, keepdims=True) + EPS)
    return ((xv * r) * gv).astype(BF16)


def _rms_fwd(name, x, width, cb, g, rows):
    def body(ins, outs, _):
        outs[0][...] = _rms(ins[0][...].astype(F32), ins[1][...])
    return _ew(name, body, [(x, width, cb), (g.reshape(1, width), None, None)], [(width, BF16)], rows)[0]


def _rms_bwd(name, x, width, cb, g, dh_mm, rows, out_dtype, dres=None, into=()):
    def body(ins, outs, accs):
        dhv, xv, gv = ins[0][...], ins[1][...].astype(F32), ins[2][...]
        r = lax.rsqrt(jnp.mean(xv * xv, axis=-1, keepdims=True) + EPS)
        xhat = xv * r
        accs[0][...] += jnp.sum(dhv * xhat, axis=0, keepdims=True)
        dy = dhv * gv
        dx = r * (dy - xhat * jnp.mean(dy * xhat, axis=-1, keepdims=True))
        if dres is not None:
            dx = dx + ins[3][...]
        outs[0][...] = dx.astype(out_dtype)
    ins = [(x, width, cb), (g.reshape(1, width), None, None)]
    if dres is not None:
        ins.append((dres, width, 0))
    return _ew(name, body, ins, [(width, out_dtype) + tuple(into)], rows, accs=[(1, width)], mms=[dh_mm])


def _mm(name, a, b, out_dtype, f32_cols=None, norms=(), tn=MM_TN):
    M, K = a.shape
    N = b.shape[1]
    tm, tn = _row_tile(M, b.size * b.dtype.itemsize), min(tn, N)
    c0, cw = f32_cols if f32_cols else (0, 0)
    n_norm, n_f32 = len(norms), 1 if f32_cols else 0
    assert c0 % tn == 0 and cw % tn == 0
    assert all(nc // tn == (nc + g.shape[-1] - 1) // tn for nc, g in norms)

    def kern(*refs):
        a_ref, b_ref, g_refs = refs[0], refs[1], refs[2:2 + n_norm]
        o_ref, extra = refs[2 + n_norm], refs[3 + n_norm:]
        av = a_ref[...].astype(BF16)
        for j in range(N // tn):
            cols = slice(j * tn, (j + 1) * tn)
            part = jnp.dot(av, b_ref[:, cols], preferred_element_type=F32)
            o_ref[:, cols] = part.astype(o_ref.dtype)
            if f32_cols and c0 <= j * tn and (j + 1) * tn <= c0 + cw:
                extra[0][:, j * tn - c0:(j + 1) * tn - c0] = part
            for k, (nc, g) in enumerate(norms):
                if nc // tn == j:
                    seg = part[:, nc - j * tn:nc - j * tn + g.shape[-1]]
                    extra[n_f32 + k][...] = _rms(seg, g_refs[k][...])

    in_specs = [pl.BlockSpec((tm, K), lambda i: (i, 0)), pl.BlockSpec((K, N), lambda i: (0, 0))]
    in_specs += [pl.BlockSpec((1, g.shape[-1]), lambda i: (0, 0)) for _, g in norms]
    widths = [(N, out_dtype)] + ([(cw, F32)] if f32_cols else []) + [(g.shape[-1], BF16) for _, g in norms]
    return pl.pallas_call(
        kern, name=name, grid=(M // tm,), in_specs=in_specs,
        out_specs=[pl.BlockSpec((tm, w), lambda i: (i, 0)) for w, _ in widths],
        out_shape=[jax.ShapeDtypeStruct((M, w), dt) for w, dt in widths],
        compiler_params=_params(("parallel",)))(a, b, *[g.reshape(1, -1) for _, g in norms])


def _mm_tn(name, a, b, tk=2048, tn=2048):
    T, M = a.shape
    N = b.shape[1]
    tn, tk = min(tn, N), min(tk, T)

    def kern(a_ref, b_ref, o_ref):
        k = pl.program_id(1)
        part = _dot_tn(a_ref[...].astype(BF16), b_ref[...].astype(BF16))

        @pl.when(k == 0)
        def _():
            o_ref[...] = part

        @pl.when(k > 0)
        def _():
            o_ref[...] += part

    return pl.pallas_call(
        kern, name=name, grid=(N // tn, T // tk),
        in_specs=[pl.BlockSpec((tk, M), lambda j, k: (k, 0)), pl.BlockSpec((tk, tn), lambda j, k: (k, j))],
        out_specs=pl.BlockSpec((M, tn), lambda j, k: (0, j)),
        out_shape=jax.ShapeDtypeStruct((M, N), F32),
        compiler_params=_params(("parallel", "arbitrary")))(a, b)


def _dot_nt(a, b):
    return lax.dot_general(a, b, (((1,), (1,)), ((), ())), preferred_element_type=F32)


def _dot_tn(a, b):
    return lax.dot_general(a, b, (((0,), (0,)), ((), ())), preferred_element_type=F32)


SWA_SCALE = HEAD_DIM ** -0.5


def _swa_band(n, pq_ref, pkp_ref, pkc_ref):
    posk = jnp.concatenate([pkp_ref[...], pkc_ref[...]], axis=0)
    dist = (pq_ref[0] - posk).astype(F32)
    kj = lax.broadcasted_iota(jnp.int32, (2 * BLOCK, BLOCK), 0)
    qi = lax.broadcasted_iota(jnp.int32, (2 * BLOCK, BLOCK), 1)
    t_abs = n * BLOCK + qi
    s_abs = n * BLOCK - BLOCK + kj
    return dist, (s_abs >= 0) & (s_abs <= t_abs) & (t_abs - s_abs < BLOCK)


SWA_GROUP = HEADS // SWA_KV_HEADS


def _head_gate(gate_ref, h):
    pair = gate_ref[:, (h // 2) * LANES:(h // 2 + 1) * LANES].astype(F32)
    return pair if h % 2 == 0 else pltpu.roll(pair, HEAD_DIM, 1)


def _swa_group_q(q_all, g):
    heads = range(g * SWA_GROUP, (g + 1) * SWA_GROUP)
    return jnp.concatenate([(q_all[:, h * LANES:(h + 1) * LANES] * SWA_SCALE).astype(BF16) for h in heads], axis=0)


def _swa_mask(s, dist, valid, h):
    return jnp.where(valid, s - (2.0 ** -(h + 1)) * dist, NEG)


def _rows_to_lanes(rows):
    block = jnp.concatenate(list(rows) + [jnp.zeros((LANES - len(rows), BLOCK), F32)], axis=0)
    return block.T


def _swa_specs(nb):
    prev = lambda b, n: b * nb + jnp.maximum(n - 1, 0)
    own = lambda b, n: b * nb + n
    return [
        pl.BlockSpec((BLOCK, HPAD), lambda b, n: (own(b, n), Z_AQ // HPAD)),
        pl.BlockSpec((BLOCK, KV_W), lambda b, n: (prev(b, n), Z_AK // KV_W)),
        pl.BlockSpec((BLOCK, KV_W), lambda b, n: (own(b, n), Z_AK // KV_W)),
        pl.BlockSpec((BLOCK, KV_W), lambda b, n: (prev(b, n), Z_AV // KV_W)),
        pl.BlockSpec((BLOCK, KV_W), lambda b, n: (own(b, n), Z_AV // KV_W)),
        pl.BlockSpec((1, 1, BLOCK), lambda b, n: (own(b, n), 0, 0)),
        pl.BlockSpec((BLOCK, 1), lambda b, n: (prev(b, n), 0)),
        pl.BlockSpec((BLOCK, 1), lambda b, n: (own(b, n), 0)),
    ]


def _swa_fwd(z, gate, pos_col, pos_row, sink_row, B, S):
    nb = S // BLOCK
    T = B * S

    def kern(q_ref, kp_ref, kc_ref, vp_ref, vc_ref, pq_ref, pkp_ref, pkc_ref, gate_ref, sink_ref,
             oraw_ref, og_ref, lse_ref):
        q_all = q_ref[...]
        kb = jnp.concatenate([kp_ref[...], kc_ref[...]], axis=0).astype(BF16)
        vb = jnp.concatenate([vp_ref[...], vc_ref[...]], axis=0).astype(BF16)
        dist, valid = _swa_band(pl.program_id(1), pq_ref, pkp_ref, pkc_ref)
        lse_rows = []
        for grp in range(SWA_KV_HEADS):
            gcols = slice(grp * LANES, (grp + 1) * LANES)
            s_all = _dot_nt(kb[:, gcols], _swa_group_q(q_all, grp))
            probs = []
            for hh in range(SWA_GROUP):
                h = grp * SWA_GROUP + hh
                s = _swa_mask(s_all[:, hh * BLOCK:(hh + 1) * BLOCK], dist, valid, h)
                sink_h = sink_ref[0:1, h:h + 1]
                m = jnp.maximum(jnp.max(s, axis=0, keepdims=True), sink_h)
                e = jnp.exp(s - m)
                denom = jnp.sum(e, axis=0, keepdims=True) + jnp.exp(sink_h - m)
                probs.append((e * (1.0 / denom)).astype(BF16))
                lse_rows.append(m + jnp.log(denom))
            o_all = jnp.dot(vb[:, gcols].T, jnp.concatenate(probs, axis=1), preferred_element_type=F32)
            for hh in range(SWA_GROUP):
                h = grp * SWA_GROUP + hh
                cols = slice(h * LANES, (h + 1) * LANES)
                o = o_all[:, hh * BLOCK:(hh + 1) * BLOCK].T
                oraw_ref[:, cols] = o
                g = _head_gate(gate_ref, h)
                og_ref[:, cols] = (o * (g * jax.nn.sigmoid(g))).astype(BF16)
        lse_ref[...] = _rows_to_lanes(lse_rows)

    own = lambda b, n: b * nb + n
    in_specs = _swa_specs(nb) + [
        pl.BlockSpec((BLOCK, GATE_W), lambda b, n: (own(b, n), 0)),
        pl.BlockSpec((1, LANES), lambda b, n: (0, 0)),
    ]
    out_specs = [pl.BlockSpec((BLOCK, HPAD), lambda b, n: (own(b, n), 0)),
                 pl.BlockSpec((BLOCK, HPAD), lambda b, n: (own(b, n), 0)),
                 pl.BlockSpec((BLOCK, LANES), lambda b, n: (own(b, n), 0))]
    out_shape = [jax.ShapeDtypeStruct((T, HPAD), F32), jax.ShapeDtypeStruct((T, HPAD), BF16),
                 jax.ShapeDtypeStruct((T, LANES), F32)]
    return pl.pallas_call(kern, name="swa_fwd", grid=(B, nb), in_specs=in_specs, out_specs=out_specs,
                          out_shape=out_shape, compiler_params=_params(("parallel", "arbitrary")))(
        z, z, z, z, z, pos_row, pos_col, pos_col, gate, sink_row)


def _swa_bwd(z, pos_col, pos_row, sink_row, lse, do_raw, delta, dz, B, S):
    nb = S // BLOCK
    T = B * S

    def kern(q_ref, kp_ref, kc_ref, vp_ref, vc_ref, pq_ref, pkp_ref, pkc_ref, sink_ref, lse_ref, do_ref,
             delta_ref, dz_ref, dq_ref, dk_ref, dv_ref, dsink_ref):
        b, n = pl.program_id(0), pl.program_id(1)

        @pl.when(n == 0)
        def _():
            dk_ref[...] = jnp.zeros_like(dk_ref)
            dv_ref[...] = jnp.zeros_like(dv_ref)

        @pl.when((b == 0) & (n == 0))
        def _():
            dsink_ref[...] = jnp.zeros_like(dsink_ref)

        q_all = q_ref[...]
        kb = jnp.concatenate([kp_ref[...], kc_ref[...]], axis=0).astype(BF16)
        vb = jnp.concatenate([vp_ref[...], vc_ref[...]], axis=0).astype(BF16)
        dist, valid = _swa_band(n, pq_ref, pkp_ref, pkc_ref)
        lse_t, delta_t = lse_ref[...].T, delta_ref[...].T
        lane1 = lax.broadcasted_iota(jnp.int32, (1, LANES), 1)
        dsink = jnp.zeros((1, LANES), F32)
        dk_band, dv_band = [], []
        for grp in range(SWA_KV_HEADS):
            gcols = slice(grp * LANES, (grp + 1) * LANES)
            heads = range(grp * SWA_GROUP, (grp + 1) * SWA_GROUP)
            qg = _swa_group_q(q_all, grp)
            dog = jnp.concatenate([do_ref[:, h * LANES:(h + 1) * LANES] for h in heads], axis=0)
            s_all = _dot_nt(kb[:, gcols], qg)
            dp_all = _dot_nt(vb[:, gcols], dog)
            ps, dss = [], []
            for hh, h in enumerate(heads):
                blk = slice(hh * BLOCK, (hh + 1) * BLOCK)
                lse_h, delta_h = lse_t[h:h + 1, :], delta_t[h:h + 1, :]
                p = jnp.exp(_swa_mask(s_all[:, blk], dist, valid, h) - lse_h)
                ps.append(p.astype(BF16))
                dss.append((p * (dp_all[:, blk] - delta_h)).astype(BF16))
                psink = jnp.exp(sink_ref[0:1, h:h + 1] - lse_h)
                dsink = dsink + jnp.where(lane1 == h, -jnp.sum(psink * delta_h, axis=1, keepdims=True), 0.0)
            dsg = jnp.concatenate(dss, axis=1)
            dq_all = jnp.dot(kb[:, gcols].T, dsg, preferred_element_type=F32) * SWA_SCALE
            for hh, h in enumerate(heads):
                dq_ref[:, h * LANES:(h + 1) * LANES] = dq_all[:, hh * BLOCK:(hh + 1) * BLOCK].T.astype(BF16)
            dk_band.append(jnp.dot(dsg, qg, preferred_element_type=F32))
            dv_band.append(jnp.dot(jnp.concatenate(ps, axis=1), dog, preferred_element_type=F32))
        dsink_ref[...] += dsink
        dkb = jnp.concatenate(dk_band, axis=1)
        dvb = jnp.concatenate(dv_band, axis=1)
        r_prev = pl.ds(pl.multiple_of(jnp.maximum(n - 1, 0) * BLOCK, BLOCK), BLOCK)
        r_own = pl.ds(pl.multiple_of(n * BLOCK, BLOCK), BLOCK)
        dk_ref[r_prev, :] += dkb[:BLOCK]
        dk_ref[r_own, :] += dkb[BLOCK:]
        dv_ref[r_prev, :] += dvb[:BLOCK]
        dv_ref[r_own, :] += dvb[BLOCK:]

    own = lambda b, n: b * nb + n
    in_specs = _swa_specs(nb) + [
        pl.BlockSpec((1, LANES), lambda b, n: (0, 0)),
        pl.BlockSpec((BLOCK, LANES), lambda b, n: (own(b, n), 0)),
        pl.BlockSpec((BLOCK, HPAD), lambda b, n: (own(b, n), 0)),
        pl.BlockSpec((BLOCK, LANES), lambda b, n: (own(b, n), 0)),
        pl.BlockSpec(memory_space=pl.ANY),
    ]
    out_specs = [pl.BlockSpec((BLOCK, HPAD), lambda b, n: (own(b, n), Z_AQ // HPAD)),
                 pl.BlockSpec((S, KV_W), lambda b, n: (b, 0)),
                 pl.BlockSpec((S, KV_W), lambda b, n: (b, 0)),
                 pl.BlockSpec((1, LANES), lambda b, n: (0, 0))]
    out_shape = [jax.ShapeDtypeStruct(dz.shape, dz.dtype), jax.ShapeDtypeStruct((T, KV_W), F32),
                 jax.ShapeDtypeStruct((T, KV_W), F32), jax.ShapeDtypeStruct((1, LANES), F32)]
    return pl.pallas_call(kern, name="swa_bwd", grid=(B, nb), in_specs=in_specs, out_specs=out_specs,
                          out_shape=out_shape, input_output_aliases={len(in_specs) - 1: 0},
                          compiler_params=_params(("arbitrary", "arbitrary")))(
        z, z, z, z, z, pos_row, pos_col, pos_col, sink_row, lse, do_raw, delta, dz)


MLA_T = 256
MLA_HG = 4
MLA_W = MLA_HG * LANES
MLA_SCALE = MLA_QK ** -0.5
LOG2E = 1.4426950408889634
MLA_QSCALE = MLA_SCALE * LOG2E


def _causal_t(s):
    key = lax.broadcasted_iota(jnp.int32, s.shape, 0)
    query = lax.broadcasted_iota(jnp.int32, s.shape, 1)
    return jnp.where(key <= query, s, NEG)


def _mla_fwd(q, k, v, z, B, S):
    T = B * S
    nq = S // MLA_T

    def kern(q_ref, k_ref, v_ref, gate_ref, oraw_ref, og_ref, lse_ref):
        i = pl.program_id(2)

        def scores(j):
            rows = pl.ds(pl.multiple_of(j * MLA_T, MLA_T), MLA_T)
            return tuple(_dot_nt(k_ref[rows, hh * LANES:(hh + 1) * LANES], q_ref[:, hh * LANES:(hh + 1) * LANES])
                         for hh in range(MLA_HG))

        def update(j, ss, state):
            rows = pl.ds(pl.multiple_of(j * MLA_T, MLA_T), MLA_T)
            out = []
            for hh in range(MLA_HG):
                (m, l, acc), s = state[hh], ss[hh]
                m_new = jnp.maximum(m, jnp.max(s, axis=0, keepdims=True))
                alpha = jnp.exp2(m - m_new)
                p = jnp.exp2(s - m_new)
                l = alpha * l + jnp.sum(p, axis=0, keepdims=True)
                pv = jnp.dot(v_ref[rows, hh * LANES:(hh + 1) * LANES].T, p.astype(BF16), preferred_element_type=F32)
                out.append((m_new, l, alpha * acc + pv))
            return tuple(out)

        def body(j, carry):
            state, ss = carry
            s_next = scores(j + 1)
            return update(j, ss, state), s_next

        init = tuple((jnp.full((1, MLA_T), NEG, F32), jnp.zeros((1, MLA_T), F32), jnp.zeros((LANES, MLA_T), F32))
                     for _ in range(MLA_HG))
        state, ss = lax.fori_loop(0, i, body, (init, scores(0)))
        state = update(i, tuple(_causal_t(s) for s in ss), state)
        for hh in range(MLA_HG):
            m, l, acc = state[hh]
            cols = slice(hh * LANES, (hh + 1) * LANES)
            o = (acc * (1.0 / l)).T
            oraw_ref[:, cols] = o.astype(BF16)
            g = _head_gate(gate_ref, hh)
            og_ref[:, cols] = (o * (g * jax.nn.sigmoid(g))).astype(BF16)
            lse_ref[0, 0, 0, hh:hh + 1, :] = m + jnp.log2(l)

    blk = lambda b, h, i: (b * nq + i, h)
    in_specs = [pl.BlockSpec((MLA_T, MLA_W), blk),
                pl.BlockSpec((S, MLA_W), lambda b, h, i: (b, h)),
                pl.BlockSpec((S, MLA_W), lambda b, h, i: (b, h)),
                pl.BlockSpec((MLA_T, MLA_W // 2), lambda b, h, i: (b * nq + i, Z_BGATE // (MLA_W // 2) + h))]
    out_specs = [pl.BlockSpec((MLA_T, MLA_W), blk), pl.BlockSpec((MLA_T, MLA_W), blk),
                 pl.BlockSpec((1, 1, 1, MLA_HG, MLA_T), lambda b, h, i: (b, h, i, 0, 0))]
    out_shape = [jax.ShapeDtypeStruct((T, HPAD), BF16), jax.ShapeDtypeStruct((T, HPAD), BF16),
                 jax.ShapeDtypeStruct((B, HEADS // MLA_HG, nq, MLA_HG, MLA_T), F32)]
    return pl.pallas_call(kern, name="mla_fwd", grid=(B, HEADS // MLA_HG, nq), in_specs=in_specs,
                          out_specs=out_specs, out_shape=out_shape,
                          compiler_params=_params(("parallel", "parallel", "arbitrary")))(q, k, v, z)


def _mla_bwd(q, k, v, do_raw, lse, delta, B, S):
    T = B * S
    nk = S // MLA_T

    def kern(q_ref, k_ref, v_ref, do_ref, lse_ref, delta_ref, dq_ref, dk_ref, dv_ref, dq_acc, dk_acc, dv_acc):
        j = pl.program_id(2)

        @pl.when(j == 0)
        def _():
            dq_acc[...] = jnp.zeros_like(dq_acc)

        dk_acc[...] = jnp.zeros_like(dk_acc)
        dv_acc[...] = jnp.zeros_like(dv_acc)
        kts = [k_ref[:, hh * LANES:(hh + 1) * LANES].T for hh in range(MLA_HG)]

        def step(i, masked):
            rows = pl.ds(pl.multiple_of(i * MLA_T, MLA_T), MLA_T)
            for hh in range(MLA_HG):
                cols = slice(hh * LANES, (hh + 1) * LANES)
                qv, do = q_ref[rows, cols], do_ref[rows, cols]
                st = _dot_nt(k_ref[:, cols], qv)
                if masked:
                    st = _causal_t(st)
                pt = jnp.exp2(st - lse_ref[0, 0, i, hh:hh + 1, :])
                dpt = _dot_nt(v_ref[:, cols], do)
                dst = (pt * (dpt - delta_ref[0, 0, i, hh:hh + 1, :])).astype(BF16)
                dv_acc[:, cols] += jnp.dot(pt.astype(BF16), do, preferred_element_type=F32)
                dk_acc[:, cols] += jnp.dot(dst, qv, preferred_element_type=F32)
                dq_acc[hh, i] += jnp.dot(kts[hh], dst, preferred_element_type=F32)

        step(j, True)

        def body(i, c):
            step(i, False)
            return c

        lax.fori_loop(j + 1, nk, body, 0)
        dk_ref[...] = (dk_acc[...] * (1.0 / LOG2E)).astype(BF16)
        dv_ref[...] = dv_acc[...].astype(BF16)

        @pl.when(j == nk - 1)
        def _():
            for hh in range(MLA_HG):
                for t in range(nk):
                    dq_ref[t * MLA_T:(t + 1) * MLA_T, hh * LANES:(hh + 1) * LANES] = dq_acc[hh, t].T.astype(BF16)

    whole = lambda b, h, j: (b, h)
    tile = lambda b, h, j: (b * nk + j, h)
    stats = pl.BlockSpec((1, 1, nk, MLA_HG, MLA_T), lambda b, h, j: (b, h, 0, 0, 0))
    in_specs = [pl.BlockSpec((S, MLA_W), whole), pl.BlockSpec((MLA_T, MLA_W), tile),
                pl.BlockSpec((MLA_T, MLA_W), tile), pl.BlockSpec((S, MLA_W), whole), stats, stats]
    out_specs = [pl.BlockSpec((S, MLA_W), whole), pl.BlockSpec((MLA_T, MLA_W), tile),
                 pl.BlockSpec((MLA_T, MLA_W), tile)]
    out_shape = [jax.ShapeDtypeStruct((T, HPAD), BF16)] * 3
    scratch = [pltpu.VMEM((MLA_HG, nk, LANES, MLA_T), F32), pltpu.VMEM((MLA_T, MLA_W), F32),
               pltpu.VMEM((MLA_T, MLA_W), F32)]
    return pl.pallas_call(kern, name="mla_bwd", grid=(B, HEADS // MLA_HG, nk), in_specs=in_specs,
                          out_specs=out_specs, out_shape=out_shape, scratch_shapes=scratch,
                          compiler_params=_params(("parallel", "parallel", "arbitrary")))(
        q, k, v, do_raw, lse, delta)


def _rope_tables(pos_col, inv_lane, rows):
    def body(ins, outs, _):
        ang = ins[0][...].astype(F32) * ins[1][...]
        lane = lax.broadcasted_iota(jnp.int32, ang.shape, 1)
        cos, sin = jnp.cos(ang), jnp.sin(ang)
        first = (lane >= HEAD_DIM) & (lane < HEAD_DIM + MLA_ROPE // 2)
        second = (lane >= HEAD_DIM + MLA_ROPE // 2) & (lane < MLA_QK)
        outs[0][...] = jnp.where(lane < HEAD_DIM, 1.0, jnp.where(lane < MLA_QK, cos, 0.0))
        outs[1][...] = jnp.where(first, -sin, 0.0)
        outs[2][...] = jnp.where(second, sin, 0.0)
    return _ew("rope_tables", body, [(pos_col, 1, 0), (inv_lane, None, None)], [(LANES, F32)] * 3, rows)


def _rope(x, c, s1, s2):
    return x * c + pltpu.roll(x, 112, 1) * s1 + pltpu.roll(x, 16, 1) * s2


def _rope_t(d, c, s1, s2):
    return d * c + pltpu.roll(d * s1, 16, 1) + pltpu.roll(d * s2, 112, 1)


def _mla_prep(qdn, w_uq, kvdn, w_ukv, z, tabs, rows):
    def body(ins, outs, _):
        q_pre, kv_pre = ins[0], ins[1]
        c, s1, s2 = ins[3][...], ins[4][...], ins[5][...]
        kr = _rope(ins[2][...].astype(F32), c, s1, s2)
        for h in range(HEADS):
            cols = slice(h * LANES, (h + 1) * LANES)
            outs[0][:, cols] = (_rope(q_pre[:, cols], c, s1, s2) * MLA_QSCALE).astype(BF16)
            outs[1][:, cols] = (kv_pre[:, cols] + kr).astype(BF16)
        outs[2][...] = kv_pre[:, HPAD:].astype(BF16)
    ins = [(z, LANES, Z_BKR // LANES), (tabs[0], LANES, 0), (tabs[1], LANES, 0), (tabs[2], LANES, 0)]
    return _ew("mla_prep", body, ins, [(HPAD, BF16)] * 3, rows, mms=[(qdn, w_uq), (kvdn, w_ukv)])


def _mla_prep_bwd(dq, dk, dv, tabs, dz, rows):
    def body(ins, outs, _):
        c, s1, s2 = ins[3][...], ins[4][...], ins[5][...]
        lane = lax.broadcasted_iota(jnp.int32, c.shape, 1)
        dkr = jnp.zeros(c.shape, F32)
        for h in range(HEADS):
            cols = slice(h * LANES, (h + 1) * LANES)
            outs[0][:, cols] = _rope_t(ins[0][:, cols].astype(F32) * MLA_SCALE, c, s1, s2).astype(BF16)
            dkh = ins[1][:, cols].astype(F32)
            outs[1][:, cols] = jnp.where(lane < HEAD_DIM, dkh, 0.0).astype(BF16)
            dkr = dkr + dkh
        outs[1][:, HPAD:] = ins[2][...].astype(BF16)
        live = (lane >= HEAD_DIM) & (lane < MLA_QK)
        outs[2][...] = jnp.where(live, _rope_t(jnp.where(live, dkr, 0.0), c, s1, s2), 0.0).astype(BF16)
    ins = [(dq, HPAD, 0), (dk, HPAD, 0), (dv, HPAD, 0), (tabs[0], LANES, 0), (tabs[1], LANES, 0),
           (tabs[2], LANES, 0)]
    outs = [(HPAD, BF16), (2 * HPAD, BF16), (LANES, BF16, dz, Z_BKR // LANES)]
    return _ew("mla_prep_bwd", body, ins, outs, rows)


def _gate_bwd(name, d_o_mm, o_raw, gate, gate_cb, dz, dz_cb, rows):
    def body(ins, outs, _):
        lane = lax.broadcasted_iota(jnp.int32, outs[2].shape, 1)
        delta = jnp.zeros(outs[2].shape, F32)
        d_gate = [None] * HEADS
        for h in range(HEADS):
            cols = slice(h * LANES, (h + 1) * LANES)
            dog, o, g = ins[0][:, cols], ins[1][:, cols].astype(F32), _head_gate(ins[2], h)
            sg = jax.nn.sigmoid(g)
            do = dog * (g * sg)
            outs[0][:, cols] = do.astype(BF16)
            d_gate[h] = dog * o * (sg * (1.0 + g * (1.0 - sg)))
            delta = jnp.where(lane == h, jnp.sum(do * o, axis=-1, keepdims=True), delta)
        for pair in range(HEADS // 2):
            packed = d_gate[2 * pair] + pltpu.roll(d_gate[2 * pair + 1], HEAD_DIM, 1)
            outs[1][:, pair * LANES:(pair + 1) * LANES] = packed.astype(BF16)
        outs[2][...] = delta
    ins = [(o_raw, HPAD, 0), (gate, GATE_W, gate_cb)]
    outs = [(HPAD, BF16), (GATE_W, BF16, dz, dz_cb), (LANES, F32)]
    return _ew(name, body, ins, outs, rows, mms=[d_o_mm])


def _merge_out(ua, ub, z, w_out, x0, g_next, rows):
    tm = _row_tile(rows)

    def kern(ua_ref, ub_ref, ma_ref, mb_ref, w_ref, x0_ref, g_ref, y_ref, x1_ref, hn_ref):
        ua_v, ub_v, m_a, m_b = (r[...].astype(F32) for r in (ua_ref, ub_ref, ma_ref, mb_ref))
        y = (jax.nn.sigmoid(m_a) * ua_v + jax.nn.sigmoid(m_b) * ub_v).astype(BF16)
        y_ref[...] = y
        for j in range(D_MODEL // MM_TN):
            cols = slice(j * MM_TN, (j + 1) * MM_TN)
            x1_ref[:, cols] = jnp.dot(y, w_ref[:, cols], preferred_element_type=F32) + x0_ref[:, cols]
        hn_ref[...] = _rms(x1_ref[...], g_ref[...])

    row = lambda cb: pl.BlockSpec((tm, D_MODEL), lambda i: (i, cb))
    return pl.pallas_call(
        kern, name="merge_out", grid=(rows // tm,),
        in_specs=[row(0), row(0), row(Z_MA // D_MODEL), row(Z_MB // D_MODEL),
                  pl.BlockSpec(w_out.shape, lambda i: (0, 0)), row(0), pl.BlockSpec((1, D_MODEL), lambda i: (0, 0))],
        out_specs=[row(0), row(0), row(0)],
        out_shape=[jax.ShapeDtypeStruct((rows, D_MODEL), BF16), jax.ShapeDtypeStruct((rows, D_MODEL), F32),
                   jax.ShapeDtypeStruct((rows, D_MODEL), BF16)],
        compiler_params=_params(("parallel",)))(ua, ub, z, z, w_out, x0, g_next.reshape(1, D_MODEL))


def _merge_bwd(dy_mm, ua, ub, z, dz, rows):
    def body(ins, outs, _):
        dyv = ins[0][...]
        for idx in range(2):
            s = jax.nn.sigmoid(ins[3 + idx][...].astype(F32))
            outs[idx][...] = (dyv * s).astype(BF16)
            d_m = (dyv * ins[1 + idx][...].astype(F32) * (s * (1.0 - s))).astype(BF16)
            outs[2][:, idx * D_MODEL:(idx + 1) * D_MODEL] = d_m
    ins = [(ua, D_MODEL, 0), (ub, D_MODEL, 0), (z, D_MODEL, Z_MA // D_MODEL), (z, D_MODEL, Z_MB // D_MODEL)]
    outs = [(D_MODEL, BF16), (D_MODEL, BF16), (2 * D_MODEL, BF16, dz, Z_MA // (2 * D_MODEL))]
    return _ew("merge_bwd", body, ins, outs, rows, mms=[dy_mm])


def _kv_grad_cast(dk, dv, dz, rows):
    def body(ins, outs, _):
        outs[0][:, :KV_W] = ins[0][...].astype(BF16)
        outs[0][:, KV_W:] = ins[1][...].astype(BF16)
    outs = [(2 * KV_W, BF16, dz, Z_AK // (2 * KV_W))]
    return _ew("kv_grad_cast", body, [(dk, KV_W, 0), (dv, KV_W, 0)], outs, rows)[0]


def _ple_fwd(x1, hn, w_pg, p, w_pp, g_next, rows):
    def body(ins, outs, _):
        u, e = ins[0][...], ins[1][...]
        x2 = ins[2][...] + jax.nn.sigmoid(u) * e
        outs[0][...] = x2
        outs[1][...] = u.astype(BF16)
        outs[2][...] = e.astype(BF16)
        if g_next is not None:
            outs[3][...] = _rms(x2, ins[3][...])
    ins = [(x1, D_MODEL, 0)] + ([(g_next.reshape(1, D_MODEL), None, None)] if g_next is not None else [])
    outs = [(D_MODEL, F32), (D_MODEL, BF16), (D_MODEL, BF16)] + ([(D_MODEL, BF16)] if g_next is not None else [])
    return _ew("ple_fwd", body, ins, outs, rows, mms=[(hn, w_pg), (p, w_pp)])


def _ple_bwd(dx2, u, e, rows):
    def body(ins, outs, _):
        d, s = ins[0][...], jax.nn.sigmoid(ins[1][...].astype(F32))
        outs[0][...] = (d * s).astype(BF16)
        outs[1][...] = (d * ins[2][...].astype(F32) * (s * (1.0 - s))).astype(BF16)
    return _ew("ple_bwd", body, [(dx2, D_MODEL, 0), (u, D_MODEL, 0), (e, D_MODEL, 0)],
               [(D_MODEL, BF16)] * 2, rows)


def _loss_head(x, g, target, rows):
    def body(ins, outs, accs):
        xv, gv = ins[0][...], ins[1][...]
        r = lax.rsqrt(jnp.mean(xv * xv, axis=-1, keepdims=True) + EPS)
        xhat = xv * r
        err = xhat * gv - ins[2][...]
        accs[0][...] += jnp.broadcast_to(0.5 * jnp.sum(jnp.mean(err * err, axis=-1, keepdims=True),
                                                       axis=0, keepdims=True), (1, LANES))
        dyv = err * (1.0 / D_MODEL)
        accs[1][...] += jnp.sum(dyv * xhat, axis=0, keepdims=True)
        dy = dyv * gv
        outs[0][...] = r * (dy - xhat * jnp.mean(dy * xhat, axis=-1, keepdims=True))
    ins = [(x, D_MODEL, 0), (g.reshape(1, D_MODEL), None, None), (target, D_MODEL, 0)]
    return _ew("loss_head", body, ins, [(D_MODEL, F32)], rows, accs=[(1, LANES), (1, D_MODEL)])


def _pad_heads_cols(w, n_heads, dim):
    k = w.shape[0]
    return jnp.pad(w.reshape(k, n_heads, dim), ((0, 0), (0, 0), (0, LANES - dim))).reshape(k, n_heads * LANES)


def _unpad_heads_cols(w, n_heads, dim):
    k = w.shape[0]
    return w.reshape(k, n_heads, LANES)[:, :, :dim].reshape(k, n_heads * dim)


def _layer_weights(w, i):
    segs = jnp.split(w['w_in'][i], list(_cumsum(IN_SIZES))[:-1], axis=1)
    a_q, a_k, a_v, a_gate, b_qd, b_kvd, b_kr, b_gate, m_a, m_b = segs
    kr = jnp.pad(b_kr, ((0, 0), (HEAD_DIM, LANES - MLA_QK)))
    w_in = jnp.concatenate([
        m_a, m_b, _pad_heads_cols(a_q, HEADS, HEAD_DIM), a_gate, b_gate, _pad_heads_cols(a_k, SWA_KV_HEADS, HEAD_DIM),
        _pad_heads_cols(a_v, SWA_KV_HEADS, HEAD_DIM), b_qd, b_kvd, kr], axis=1)
    w_uq = _pad_heads_cols(w['w_uq'][i], HEADS, MLA_QK)
    ukv = w['w_ukv'][i].reshape(MLA_KV_LORA, HEADS, 2 * HEAD_DIM)
    pad = ((0, 0), (0, 0), (0, HEAD_DIM))
    w_ukv = jnp.concatenate([jnp.pad(ukv[:, :, :HEAD_DIM], pad).reshape(MLA_KV_LORA, HPAD),
                             jnp.pad(ukv[:, :, HEAD_DIM:], pad).reshape(MLA_KV_LORA, HPAD)], axis=1)
    w_br_a = _pad_heads_cols(w['w_br_a'][i].T, HEADS, HEAD_DIM).T
    w_br_b = _pad_heads_cols(w['w_br_b'][i].T, HEADS, HEAD_DIM).T
    out = dict(w_in=w_in, w_uq=w_uq, w_ukv=w_ukv, w_br_a=w_br_a, w_br_b=w_br_b, w_out=w['w_out'][i],
               w_pg=w['w_ple_gate'][i], w_pp=w['w_ple_proj'][i])
    for name in ('w_in', 'w_uq', 'w_ukv', 'w_br_a', 'w_br_b', 'w_out', 'w_pg'):
        out[name + '_t'] = out[name].T
    return out


def _cumsum(sizes):
    acc, out = 0, []
    for s in sizes:
        acc += s
        out.append(acc)
    return out


def _unpad_grads(g):
    d = g['w_in']
    seg = lambda off, width: d[:, off:off + width]
    b_kr = seg(Z_BKR, LANES)[:, HEAD_DIM:MLA_QK]
    w_in = jnp.concatenate([
        _unpad_heads_cols(seg(Z_AQ, HPAD), HEADS, HEAD_DIM), _unpad_heads_cols(seg(Z_AK, KV_W), SWA_KV_HEADS, HEAD_DIM),
        _unpad_heads_cols(seg(Z_AV, KV_W), SWA_KV_HEADS, HEAD_DIM), seg(Z_AGATE, GATE_W),
        seg(Z_BQD, MLA_Q_LORA), seg(Z_BKVD, MLA_KV_LORA), b_kr, seg(Z_BGATE, GATE_W),
        seg(Z_MA, D_MODEL), seg(Z_MB, D_MODEL)], axis=1)
    w_uq = _unpad_heads_cols(g['w_uq'], HEADS, MLA_QK)
    ukv = g['w_ukv'].reshape(MLA_KV_LORA, 2, HEADS, LANES)[:, :, :, :HEAD_DIM]
    w_ukv = jnp.concatenate([ukv[:, 0], ukv[:, 1]], axis=-1).reshape(MLA_KV_LORA, HEADS * 2 * HEAD_DIM)
    w_br_a = _unpad_heads_cols(g['w_br_a'].T, HEADS, HEAD_DIM).T
    w_br_b = _unpad_heads_cols(g['w_br_b'].T, HEADS, HEAD_DIM).T
    return dict(w_in=w_in, w_uq=w_uq, w_ukv=w_ukv, w_br_a=w_br_a, w_br_b=w_br_b, w_out=g['w_out'],
                w_ple_gate=g['w_pg'], w_ple_proj=g['w_pp'], g_mix=g['g_mix'], sink=g['sink'], g_q=g['g_q'],
                g_kv=g['g_kv'], g_ple=g['g_ple'])


def _layer_fwd(x0, h, p_i, lw, sm, i, pos_col, pos_row, tabs, B, S):
    T = B * S
    z, a_gate, qdn, kvdn = _mm("proj_in", h, lw['w_in'], BF16, f32_cols=(Z_AGATE, GATE_W),
                               norms=[(Z_BQD, sm['g_q'][i]), (Z_BKVD, sm['g_kv'][i])])
    sink_row = jnp.pad(sm['sink'][i], (0, LANES - HEADS)).reshape(1, LANES)
    oa_raw, oa, lse_a = _swa_fwd(z, a_gate, pos_col, pos_row, sink_row, B, S)
    qf, kf, vf = _mla_prep(qdn, lw['w_uq'], kvdn, lw['w_ukv'], z, tabs, T)
    ob_raw, ob, lse_b = _mla_fwd(qf, kf, vf, z, B, S)
    ua, = _mm("proj_br_a", oa, lw['w_br_a'], BF16)
    ub, = _mm("proj_br_b", ob, lw['w_br_b'], BF16)
    y, x1, hn = _merge_out(ua, ub, z, lw['w_out'], x0, sm['g_ple'][i], T)
    g_next = sm['g_mix'][i + 1] if i + 1 < DEPTH else None
    x2, u, e, *h_next = _ple_fwd(x1, hn, lw['w_pg'], p_i, lw['w_pp'], g_next, T)
    saved = dict(x0=x0, h=h, z=z, a_gate=a_gate, sink_row=sink_row, oa_raw=oa_raw, oa=oa, lse_a=lse_a, qdn=qdn, kvdn=kvdn,
                 qf=qf, kf=kf, vf=vf, ob_raw=ob_raw, ob=ob, lse_b=lse_b, ua=ua, ub=ub, y=y, x1=x1, hn=hn,
                 u=u, e=e, p=p_i)
    return x2, (h_next[0] if h_next else None), saved


def _layer_bwd(dx2, sv, lw, sm, i, pos_col, pos_row, tabs, B, S):
    T = B * S
    z = sv['z']
    g = {}
    d_e, d_u = _ple_bwd(dx2, sv['u'], sv['e'], T)
    g['w_pp'] = _mm_tn("grad_pp", sv['p'], d_e)
    g['w_pg'] = _mm_tn("grad_pg", sv['hn'], d_u)
    dx1, g['g_ple'] = _rms_bwd("norm_ple_bwd", sv['x1'], D_MODEL, 0, sm['g_ple'][i], (d_u, lw['w_pg_t']), T, F32,
                               dres=dx2)
    g['w_out'] = _mm_tn("grad_out", sv['y'], dx1)
    dz = lax.empty((T, Z_WIDTH), BF16)
    d_ua, d_ub, dz = _merge_bwd((dx1, lw['w_out_t']), sv['ua'], sv['ub'], z, dz, T)
    g['w_br_a'] = _mm_tn("grad_br_a", sv['oa'], d_ua)
    g['w_br_b'] = _mm_tn("grad_br_b", sv['ob'], d_ub)
    dob_raw, dz, delta_b = _gate_bwd("gate_b_bwd", (d_ub, lw['w_br_b_t']), sv['ob_raw'], z, Z_BGATE // GATE_W,
                                     dz, Z_BGATE // GATE_W, T)
    delta_rows = delta_b[:, :HEADS].reshape(B, S // MLA_T, MLA_T, HEADS // MLA_HG, MLA_HG).transpose(0, 3, 1, 4, 2)
    dq, dk, dv = _mla_bwd(sv['qf'], sv['kf'], sv['vf'], dob_raw, sv['lse_b'], delta_rows, B, S)
    dq_pre, dkv_pre, dz = _mla_prep_bwd(dq, dk, dv, tabs, dz, T)
    g['w_uq'] = _mm_tn("grad_uq", sv['qdn'], dq_pre)
    g['w_ukv'] = _mm_tn("grad_ukv", sv['kvdn'], dkv_pre)
    dz, g['g_q'] = _rms_bwd("norm_q_bwd", z, MLA_Q_LORA, Z_BQD // MLA_Q_LORA, sm['g_q'][i],
                            (dq_pre, lw['w_uq_t']), T, BF16, into=(dz, Z_BQD // MLA_Q_LORA))
    dz, g['g_kv'] = _rms_bwd("norm_kv_bwd", z, MLA_KV_LORA, Z_BKVD // MLA_KV_LORA, sm['g_kv'][i],
                             (dkv_pre, lw['w_ukv_t']), T, BF16, into=(dz, Z_BKVD // MLA_KV_LORA))
    doa_raw, dz, delta_a = _gate_bwd("gate_a_bwd", (d_ua, lw['w_br_a_t']), sv['oa_raw'], sv['a_gate'], 0,
                                     dz, Z_AGATE // GATE_W, T)
    dz, d_ak, d_av, dsink = _swa_bwd(z, pos_col, pos_row, sv['sink_row'], sv['lse_a'], doa_raw, delta_a, dz, B, S)
    dz = _kv_grad_cast(d_ak, d_av, dz, T)
    g['sink'] = dsink[0, :HEADS]
    g['w_in'] = _mm_tn("grad_in", sv['h'], dz, tk=1024, tn=Z_WIDTH // 2)
    dx0, g['g_mix'] = _rms_bwd("norm_mix_bwd", sv['x0'], D_MODEL, 0, sm['g_mix'][i], (dz, lw['w_in_t']), T, F32,
                               dres=dx1)
    for name in ('g_ple', 'g_q', 'g_kv', 'g_mix'):
        g[name] = g[name][0]
    return dx0, g


def _local_step(x, p, positions, wfull, sm, loss_target):
    B, S, _ = x.shape
    T = B * S
    pos_col = positions.reshape(T, 1)
    pos_row = positions.reshape(T // BLOCK, 1, BLOCK)
    half = MLA_ROPE // 2
    inv = ROPE_THETA ** (-jnp.arange(0, MLA_ROPE, 2, dtype=F32) / MLA_ROPE)
    inv_lane = jnp.tile(inv, LANES // half).reshape(1, LANES)
    tabs = _rope_tables(pos_col, inv_lane, T)
    xc = x.reshape(T, D_MODEL)
    h = _rms_fwd("norm_mix", xc, D_MODEL, 0, sm['g_mix'][0], T)
    lws, saved = [], []
    for i in range(DEPTH):
        lw = _layer_weights(wfull, i)
        xc, h, sv = _layer_fwd(xc, h, p[i].reshape(T, PLE_DIM), lw, sm, i, pos_col, pos_row, tabs, B, S)
        lws.append(lw)
        saved.append(sv)
    dx, loss, dg_final = _loss_head(xc, sm['g_final'], loss_target.reshape(T, D_MODEL), T)
    layer_grads = [None] * DEPTH
    for i in reversed(range(DEPTH)):
        dx, g = _layer_bwd(dx, saved[i], lws[i], sm, i, pos_col, pos_row, tabs, B, S)
        layer_grads[i] = _unpad_grads(g)
    return loss, dx.reshape(B, S, D_MODEL), layer_grads, dg_final[0]


SMALL_ROWS = 48


SMALL_SIZE = 2 * (2 * D_MODEL + HEADS + MLA_Q_LORA + MLA_KV_LORA) + D_MODEL


def _pack_small(arrs, tail=()):
    flat = jnp.concatenate([arrs[name].reshape(-1) for name in SMALL] + [t.reshape(1) for t in tail])
    return jnp.pad(flat, (0, SMALL_ROWS * LANES - flat.shape[0])).reshape(SMALL_ROWS, LANES)


def _unpack_small(block, shapes):
    flat = block.reshape(-1)
    out, off = {}, 0
    for name in SMALL:
        n = math.prod(shapes[name])
        out[name] = flat[off:off + n].reshape(shapes[name])
        off += n
    return out


def _flipped(shard_shape):
    return shard_shape[-1] % LANES != 0


def _to_slots(g, axis):
    r, c = g.shape
    if axis == 0:
        return g.reshape(N_CHIPS, r // N_CHIPS, c)
    return g.reshape(r, N_CHIPS, c // N_CHIPS).transpose(1, 0, 2)


def _div_tile(rows, cap):
    return next(t for t in range(min(cap, rows) // 8 * 8, 0, -8) if rows % t == 0)


def _units(shapes):
    units = []
    for w, shape in enumerate(shapes):
        r = shape[-2]
        n = next(n for n in (8, 7, 4, 2, 1) if r % (8 * n) == 0) if r >= 1024 else 1
        units += [(w, k * (r // n), r // n) for k in range(n)]
    return units


def _place():
    x, y, c = lax.axis_index("x"), lax.axis_index("y"), lax.axis_index("c")
    chips = [(1 - x, y), (x, 1 - y), (1 - x, 1 - y)]
    return x, y, c, chips


ANY = pl.BlockSpec(memory_space=pl.ANY)


def _remote(send_sems, recv_sems, k, src, dst, to):
    return pltpu.make_async_remote_copy(src_ref=src, dst_ref=dst, send_sem=send_sems.at[k],
                                        recv_sem=recv_sems.at[k], device_id=to, device_id_type=MESH)


def _gather_weights(shards, carried):
    n, nc = len(shards), len(carried)
    units = _units([s.shape for s in shards])
    nu = len(units)

    def body(*refs):
        ins, outs = refs[:n], refs[n + nc:2 * n + nc]
        send_sems, recv_sems, local_sems = refs[2 * (n + nc):]
        x, y, c, chips = _place()
        me = 2 * x + y
        sibling = (x, y, 1 - c)
        copy = functools.partial(_remote, send_sems, recv_sems)
        keeps, sends = [], []
        for u, (w, r0, nr) in enumerate(units):
            rows = pl.ds(r0, nr)
            keeps.append(pltpu.make_async_copy(ins[w].at[:, rows, :], outs[w].at[me, :, rows, :], local_sems.at[u]))
            keeps[-1].start()
        for j, (cx, cy) in enumerate(chips):
            for u, (w, r0, nr) in enumerate(units):
                rows = pl.ds(r0, nr)
                sends.append(copy(j * nu + u, ins[w].at[c, rows, :], outs[w].at[me, c, rows, :], (cx, cy, c)))
                sends[-1].start()
        for j, (cx, cy) in enumerate(chips):
            for u, (w, r0, nr) in enumerate(units):
                landed = outs[w].at[2 * cx + cy, c, pl.ds(r0, nr), :]
                copy(j * nu + u, landed, landed, (cx, cy, c)).wait_recv()
                sends.append(copy((3 + j) * nu + u, landed, landed, sibling))
                sends[-1].start()
        for j, (cx, cy) in enumerate(chips):
            for u, (w, r0, nr) in enumerate(units):
                other = outs[w].at[2 * cx + cy, 1 - c, pl.ds(r0, nr), :]
                copy((3 + j) * nu + u, other, other, sibling).wait_recv()
        for cp in sends:
            cp.wait_send()
        for keep in keeps:
            keep.wait()

    out_shape = [jax.ShapeDtypeStruct((N_CHIPS,) + s.shape, s.dtype) for s in shards]
    out_shape += [jax.ShapeDtypeStruct(a.shape, a.dtype) for a in carried]
    res = pl.pallas_call(
        body, name="gather_weights", out_shape=out_shape,
        in_specs=[ANY] * (n + nc), out_specs=[ANY] * (n + nc),
        input_output_aliases={n + k: n + k for k in range(nc)},
        scratch_shapes=[pltpu.SemaphoreType.DMA((6 * nu,)), pltpu.SemaphoreType.DMA((6 * nu,)),
                        pltpu.SemaphoreType.DMA((nu,))])(*shards, *carried)
    return res[:n], res[n:]


def _pair_exchange(g0, g1):
    n = len(g0)

    def body(*refs):
        layers, outs = (refs[:n], refs[n:2 * n]), refs[2 * n:3 * n]
        send_sems, recv_sems = refs[3 * n:]
        x, y, c, _ = _place()
        copy = functools.partial(_remote, send_sems, recv_sems)
        for w in range(n):
            for q in range(N_CHIPS):
                for layer in range(DEPTH):
                    cp = copy(N_CHIPS * w + q, layers[layer][w].at[q], outs[w].at[q], (x, y, 1 - c))
                    pl.when(c == 1 - layer)(cp.start)
        for w in range(n):
            for q in range(N_CHIPS):
                copy(N_CHIPS * w + q, layers[0][w].at[q], outs[w].at[q], (x, y, 1 - c)).wait()

    return pl.pallas_call(
        body, name="pair_exchange", out_shape=[jax.ShapeDtypeStruct(g.shape, g.dtype) for g in g0],
        in_specs=[ANY] * (2 * n), out_specs=[ANY] * n,
        scratch_shapes=[pltpu.SemaphoreType.DMA((N_CHIPS * n,)), pltpu.SemaphoreType.DMA((N_CHIPS * n,))])(*g0, *g1)


def _pair_sum(name, g0, g1, theirs, cflag):
    shape = theirs.shape
    rows, width = shape[0] * shape[1], shape[2]

    def body(ins, outs, _):
        mine = jnp.where(ins[3][0:1, 0:1] == 0.0, ins[0][...], ins[1][...])
        tot = mine + ins[2][...]
        outs[0][...] = tot
        outs[1][...] = tot.astype(BF16)
    ins = [(a.reshape(rows, width), width, 0) for a in (g0, g1, theirs)] + [(cflag, None, None)]
    f32, bf16 = _ew(name, body, ins, [(width, F32), (width, BF16)], rows, tm=_div_tile(rows, ROW_TILE))
    return f32.reshape(shape), bf16.reshape(shape)


def _chip_exchange(parts):
    n = len(parts)

    def body(*refs):
        ins, outs = refs[:n], refs[n:2 * n]
        send_sems, recv_sems = refs[2 * n:]
        x, y, c, chips = _place()
        copy = functools.partial(_remote, send_sems, recv_sems)
        sends = []
        for j, (cx, cy) in enumerate(chips):
            for w in range(n):
                sends.append(copy(j * n + w, ins[w].at[2 * cx + cy], outs[w].at[j], (cx, cy, c)))
                sends[-1].start()
        for j, (cx, cy) in enumerate(chips):
            for w in range(n):
                copy(j * n + w, outs[w].at[j], outs[w].at[j], (cx, cy, c)).wait_recv()
        for cp in sends:
            cp.wait_send()

    return pl.pallas_call(
        body, name="chip_exchange",
        out_shape=[jax.ShapeDtypeStruct((3,) + a.shape[1:], a.dtype) for a in parts],
        in_specs=[ANY] * n, out_specs=[ANY] * n,
        scratch_shapes=[pltpu.SemaphoreType.DMA((3 * n,)), pltpu.SemaphoreType.DMA((3 * n,))])(*parts)


def _chip_sum(name, part, landed, chipflag):
    _, r, width = part.shape
    tm = _div_tile(r, ROW_TILE // 2)

    def kern(p_ref, l_ref, flag_ref, o_ref):
        me = flag_ref[0:1, 0:1]
        own = jnp.where(me == 0.0, p_ref[0], jnp.where(me == 1.0, p_ref[1], jnp.where(me == 2.0, p_ref[2], p_ref[3])))
        o_ref[...] = ((own + l_ref[0].astype(F32)) + l_ref[1].astype(F32)) + l_ref[2].astype(F32)

    return pl.pallas_call(
        kern, name=name, grid=(r // tm,),
        in_specs=[pl.BlockSpec((N_CHIPS, tm, width), lambda i: (0, i, 0)),
                  pl.BlockSpec((3, tm, width), lambda i: (0, i, 0)),
                  pl.BlockSpec((1, LANES), lambda i: (0, 0))],
        out_specs=pl.BlockSpec((tm, width), lambda i: (i, 0)),
        out_shape=jax.ShapeDtypeStruct((r, width), F32), compiler_params=_params(("arbitrary",)))(part, landed, chipflag)


def _pair_broadcast(mine):
    n = len(mine)
    units = _units([a.shape for a in mine])

    def body(*refs):
        ins, outs = refs[:n], refs[n:2 * n]
        send_sems, recv_sems = refs[2 * n:]
        x, y, c, _ = _place()
        copy = functools.partial(_remote, send_sems, recv_sems)
        cps = [copy(u, ins[w].at[pl.ds(r0, nr), :], outs[w].at[pl.ds(r0, nr), :], (x, y, 1 - c))
               for u, (w, r0, nr) in enumerate(units)]
        for cp in cps:
            cp.start()
        for cp in cps:
            cp.wait()

    return pl.pallas_call(
        body, name="pair_broadcast", out_shape=[jax.ShapeDtypeStruct(a.shape, a.dtype) for a in mine],
        in_specs=[ANY] * n, out_specs=[ANY] * n,
        scratch_shapes=[pltpu.SemaphoreType.DMA((len(units),)), pltpu.SemaphoreType.DMA((len(units),))])(*mine)


def _small_allreduce(v):
    offsets = [(dx, dy, dc) for dx in (0, 1) for dy in (0, 1) for dc in (0, 1)][1:]

    def body(v_ref, out_ref, recv_ref, send_sems, recv_sems):
        x, y, c, _ = _place()
        flip = lambda a, d: 1 - a if d else a
        peers = [(flip(x, dx), flip(y, dy), flip(c, dc)) for dx, dy, dc in offsets]
        copy = functools.partial(_remote, send_sems, recv_sems)
        me = 4 * x + 2 * y + c
        recv_ref[me] = v_ref[...]
        cps = [copy(k, v_ref, recv_ref.at[me], peer) for k, peer in enumerate(peers)]
        for cp in cps:
            cp.start()
        for k, (px, py, pc) in enumerate(peers):
            landed = recv_ref.at[4 * px + 2 * py + pc]
            copy(k, landed, landed, (px, py, pc)).wait_recv()
        for cp in cps:
            cp.wait_send()
        tot = recv_ref[0]
        for d in range(1, 8):
            tot = tot + recv_ref[d]
        out_ref[...] = tot

    vmem = pl.BlockSpec(memory_space=pltpu.VMEM)
    return pl.pallas_call(
        body, name="small_allreduce", out_shape=jax.ShapeDtypeStruct(v.shape, v.dtype),
        in_specs=[vmem], out_specs=vmem,
        scratch_shapes=[pltpu.VMEM((8,) + v.shape, v.dtype), pltpu.SemaphoreType.DMA((7,)),
                        pltpu.SemaphoreType.DMA((7,))])(v)


def _adam_math(gv, wv, mv, vv):
    mv = ADAM_B1 * mv + (1.0 - ADAM_B1) * gv
    vv = ADAM_B2 * vv + (1.0 - ADAM_B2) * (gv * gv)
    m_hat = mv / (1.0 - ADAM_B1 ** ADAM_STEP)
    v_hat = vv / (1.0 - ADAM_B2 ** ADAM_STEP)
    return -ADAM_LR * (m_hat / (jnp.sqrt(v_hat) + ADAM_EPS) + ADAM_WD * wv), mv, vv


def _adamw_big(name, mine, theirs, cflag, w, m, v):
    _, r, width = w.shape
    tm = _div_tile(r, ROW_TILE // 2)

    def kern(mine_ref, theirs_ref, flag_ref, w_ref, m_ref, v_ref, g_ref, d_ref, nm_ref, nv_ref):
        layer = pl.program_id(0).astype(F32)
        gv = jnp.where(flag_ref[0:1, 0:1] == layer, mine_ref[...], theirs_ref[...])
        g_ref[0] = gv
        d_ref[0], nm_ref[0], nv_ref[0] = _adam_math(gv, w_ref[0], m_ref[0], v_ref[0])

    flat = pl.BlockSpec((tm, width), lambda l, i: (i, 0))
    stacked = pl.BlockSpec((1, tm, width), lambda l, i: (l, i, 0))
    return pl.pallas_call(
        kern, name=name, grid=(DEPTH, r // tm),
        in_specs=[flat, flat, pl.BlockSpec((1, LANES), lambda l, i: (0, 0)), stacked, stacked, stacked],
        out_specs=[stacked] * 4, out_shape=[jax.ShapeDtypeStruct(w.shape, F32)] * 4,
        compiler_params=_params(("arbitrary", "arbitrary")))(mine, theirs, cflag, w, m, v)


def _adamw_small(g, w, m, v):
    def body(ins, outs, _):
        outs[0][...], outs[1][...], outs[2][...] = _adam_math(*(r[...] for r in ins))
    return _ew("adamw_small", body, [(a, LANES, 0) for a in (g, w, m, v)], [(LANES, F32)] * 3, SMALL_ROWS)


def kernel(x, p, positions, g_mix, w_in, sink, g_q, w_uq, g_kv, w_ukv, w_br_a, w_br_b, w_out, g_ple, w_ple_gate, w_ple_proj, g_final, loss_target, m_g_mix, m_w_in, m_sink, m_g_q, m_w_uq, m_g_kv, m_w_ukv, m_w_br_a, m_w_br_b, m_w_out, m_g_ple, m_w_ple_gate, m_w_ple_proj, m_g_final, v_g_mix, v_w_in, v_sink, v_g_q, v_w_uq, v_g_kv, v_w_ukv, v_w_br_a, v_w_br_b, v_w_out, v_g_ple, v_w_ple_gate, v_w_ple_proj, v_g_final):
    w = dict(g_mix=g_mix, w_in=w_in, sink=sink, g_q=g_q, w_uq=w_uq, g_kv=g_kv, w_ukv=w_ukv, w_br_a=w_br_a,
             w_br_b=w_br_b, w_out=w_out, g_ple=g_ple, w_ple_gate=w_ple_gate, w_ple_proj=w_ple_proj, g_final=g_final)
    m = dict(g_mix=m_g_mix, w_in=m_w_in, sink=m_sink, g_q=m_g_q, w_uq=m_w_uq, g_kv=m_g_kv, w_ukv=m_w_ukv,
             w_br_a=m_w_br_a, w_br_b=m_w_br_b, w_out=m_w_out, g_ple=m_g_ple, w_ple_gate=m_w_ple_gate,
             w_ple_proj=m_w_ple_proj, g_final=m_g_final)
    v = dict(g_mix=v_g_mix, w_in=v_w_in, sink=v_sink, g_q=v_g_q, w_uq=v_w_uq, g_kv=v_g_kv, w_ukv=v_w_ukv,
             w_br_a=v_w_br_a, w_br_b=v_w_br_b, w_out=v_w_out, g_ple=v_g_ple, w_ple_gate=v_w_ple_gate,
             w_ple_proj=v_w_ple_proj, g_final=v_g_final)
    wfull = _gather_full(w)
    sm = {name: w[name] for name in SMALL}
    loss_row, grad_x, layer_grads, dg_final = _local_step(x, p, positions, wfull, sm, loss_target)
    res, loss = _update(layer_grads, dg_final, loss_row[0, 0], w, m, v)
    return (loss, grad_x, *[res[name][kind] for kind in range(4) for name in WEIGHT_NAMES])


def _gather_behind(shards):
    n = len(shards)
    srcs = [jax.new_ref(s, memory_space=pltpu.MemorySpace.HBM) for s in shards]
    lands = [jax.empty_ref(jax.ShapeDtypeStruct((N_CHIPS,) + s.shape, s.dtype), memory_space=pltpu.MemorySpace.HBM)
             for s in shards]

    @pl.kernel(mesh=plsc.ScalarSubcoreMesh(axis_name="sequencer", num_cores=1), name="gather_behind",
               scratch_types=(pltpu.SemaphoreType.DMA((3 * n,)), pltpu.SemaphoreType.DMA((3 * n,)),
                              pltpu.SemaphoreType.DMA((n,))),
               compiler_params=pltpu.CompilerParams(collective_id=0))
    def launch(send_sems, recv_sems, local_sems):
        x, y, c, chips = _place()
        me = 2 * x + y
        barrier = pltpu.get_barrier_semaphore()
        for cx, cy in chips:
            pl.semaphore_signal(barrier, inc=1, device_id=(cx, cy, c), device_id_type=MESH)
        pl.semaphore_wait(barrier, len(chips))
        copy = functools.partial(_remote, send_sems, recv_sems)
        keeps = [pltpu.make_async_copy(srcs[w], lands[w].at[me], local_sems.at[w]) for w in range(n)]
        cps = [copy(j * n + w, srcs[w], lands[w].at[me], (cx, cy, c))
               for j, (cx, cy) in enumerate(chips) for w in range(n)]
        for cp in keeps + cps:
            cp.start()
        for cp in keeps + cps:
            cp.wait()

    launch()
    return [land[...] for land in lands]


def _gather_full(w):
    shards = [w[name].astype(BF16) for name, _ in SHARDED]
    first, later = _gather_weights([s[0].reshape((2, s.shape[1] // 2) + s.shape[2:]) for s in shards],
                                   [s[1] for s in shards])
    second = _gather_behind(later)
    full = {}
    for k, (name, axis) in enumerate(SHARDED):
        layer0 = first[k].reshape((N_CHIPS,) + shards[k].shape[1:])
        full[name] = [jnp.concatenate(list(blocks), axis=axis - 1) for blocks in (layer0, second[k])]
    return full


def _update(layer_grads, dg_final, loss_local, w, m, v):
    small_shapes = {name: w[name].shape for name in SMALL}
    cflag = jnp.full((1, LANES), lax.axis_index("c"), F32)
    chipflag = jnp.full((1, LANES), 2 * lax.axis_index("x") + lax.axis_index("y"), F32)

    slots = [[_to_slots(layer_grads[layer][name], axis - 1) for name, axis in SHARDED] for layer in range(DEPTH)]
    theirs = _pair_exchange(slots[0], slots[1])
    pair = [_pair_sum("pair_sum_" + name, slots[0][k], slots[1][k], theirs[k], cflag)
            for k, (name, _) in enumerate(SHARDED)]
    landed = _chip_exchange([bf16 for _, bf16 in pair])
    mine = [_chip_sum("chip_sum_" + name, pair[k][0], landed[k], chipflag) for k, (name, _) in enumerate(SHARDED)]
    other = _pair_broadcast(mine)
    res = {}
    for k, (name, _) in enumerate(SHARDED):
        flip = _flipped(w[name].shape)
        view = (lambda a: jnp.swapaxes(a, -1, -2)) if flip else (lambda a: a)
        outs = _adamw_big("adamw_" + name, view(mine[k]), view(other[k]), cflag, view(w[name]), view(m[name]),
                          view(v[name]))
        res[name] = tuple(view(a) for a in outs)

    gsmall = {name: jnp.stack([layer_grads[layer][name] for layer in range(DEPTH)]) for name in SMALL[:-1]}
    gsmall['g_final'] = dg_final
    gsum = _small_allreduce(_pack_small(gsmall, tail=[loss_local]))
    small = (gsum,) + tuple(_adamw_small(gsum, _pack_small(w), _pack_small(m), _pack_small(v)))
    for name, arrs in zip(SMALL, zip(*[[_unpack_small(a, small_shapes)[n] for n in SMALL] for a in small])):
        res[name] = arrs
    return res, gsum.reshape(-1)[SMALL_SIZE]
```

```python
import functools
import math

import jax
import jax.numpy as jnp
from jax import lax
from jax.experimental import pallas as pl
from jax.experimental.pallas import tpu as pltpu
from jax.experimental.pallas import tpu_sc as plsc

F32 = jnp.float32
BF16 = jnp.bfloat16

D_MODEL = 1024
DEPTH = 2
PLE_DIM = 256
BLOCK = 128
EPS = 1e-6
NEG = -1e30
HEADS = 8
SWA_KV_HEADS = 2
HEAD_DIM = 64
LANES = 128
HPAD = HEADS * LANES
MLA_QK = 96
MLA_ROPE = 32
MLA_Q_LORA = 256
MLA_KV_LORA = 128
ROPE_THETA = 10000.0
IN_SIZES = (512, 128, 128, 512, 256, 128, 32, 512, 1024, 1024)

Z_MA, Z_MB, Z_AQ, Z_AGATE, Z_BGATE = 0, 1024, 2048, 3072, 3584
Z_AK, Z_AV, Z_BQD, Z_BKVD, Z_BKR = 4096, 4352, 4608, 4864, 4992
Z_WIDTH = 5120
GATE_W = HEADS * HEAD_DIM
KV_W = SWA_KV_HEADS * LANES

ADAM_LR, ADAM_B1, ADAM_B2, ADAM_EPS, ADAM_WD, ADAM_STEP = 0.001, 0.9, 0.999, 1e-08, 0.01, 10

VMEM_LIMIT = 56 * 1024 * 1024
MESH = pl.DeviceIdType.MESH

WEIGHT_NAMES = ('g_mix', 'w_in', 'sink', 'g_q', 'w_uq', 'g_kv', 'w_ukv', 'w_br_a', 'w_br_b',
                'w_out', 'g_ple', 'w_ple_gate', 'w_ple_proj', 'g_final')
SHARDED = (('w_in', 2), ('w_uq', 2), ('w_ukv', 2), ('w_br_a', 2), ('w_br_b', 2),
           ('w_out', 1), ('w_ple_gate', 1), ('w_ple_proj', 2))
SMALL = ('g_mix', 'sink', 'g_q', 'g_kv', 'g_ple', 'g_final')
N_CHIPS = 4


def _params(sem):
    return pltpu.CompilerParams(dimension_semantics=sem, vmem_limit_bytes=VMEM_LIMIT)


MM_TN = 512
ROW_TILE = 512
BIG_WEIGHT_BYTES = 8 * 1024 * 1024


def _row_tile(rows, weight_bytes=0):
    tm = ROW_TILE // 2 if weight_bytes > BIG_WEIGHT_BYTES else ROW_TILE
    return min(tm, rows)


def _ew(name, body, ins, outs, rows, accs=(), mms=(), tm=None):
    n_mm, n_in, n_out = len(mms), len(ins), len(outs)
    if tm is None:
        tm = _row_tile(rows, sum(b.size * b.dtype.itemsize for _, b in mms))
    in_specs, args = [], []
    for a, b in mms:
        in_specs += [pl.BlockSpec((tm, a.shape[1]), lambda i: (i, 0)), pl.BlockSpec(b.shape, lambda i: (0, 0))]
        args += [a, b]
    for arr, width, cb in ins:
        if width is None:
            in_specs.append(pl.BlockSpec(arr.shape, lambda i, nd=arr.ndim: (0,) * nd))
        else:
            in_specs.append(pl.BlockSpec((tm, width), lambda i, cb=cb: (i, cb)))
        args.append(arr)
    out_shape, out_specs, aliases = [], [], {}
    for k, out in enumerate(outs):
        if len(out) == 4:
            aliases[len(args)] = k
            in_specs.append(pl.BlockSpec(memory_space=pl.ANY))
            args.append(out[2])
            out_shape.append(jax.ShapeDtypeStruct(out[2].shape, out[2].dtype))
            out_specs.append(pl.BlockSpec((tm, out[0]), lambda i, cb=out[3]: (i, cb)))
        else:
            out_shape.append(jax.ShapeDtypeStruct((rows, out[0]), out[1]))
            out_specs.append(pl.BlockSpec((tm, out[0]), lambda i: (i, 0)))
    n_in += len(aliases)
    out_shape += [jax.ShapeDtypeStruct(s, F32) for s in accs]
    out_specs += [pl.BlockSpec(s, lambda i: (0, 0)) for s in accs]

    def kern(*refs):
        mm_refs, refs = refs[:2 * n_mm], refs[2 * n_mm:]
        in_refs, out_refs = refs[:n_in - len(aliases)], refs[n_in:n_in + n_out]
        acc_refs, prod_refs = refs[n_in + n_out:n_in + n_out + len(accs)], refs[n_in + n_out + len(accs):]
        if acc_refs:
            @pl.when(pl.program_id(0) == 0)
            def _():
                for r in acc_refs:
                    r[...] = jnp.zeros_like(r)
        for k in range(n_mm):
            a_ref, b_ref, prod = mm_refs[2 * k], mm_refs[2 * k + 1], prod_refs[k]
            av = a_ref[...].astype(BF16)
            n = b_ref.shape[1]
            tn = min(MM_TN, n)
            for j in range(n // tn):
                cols = slice(j * tn, (j + 1) * tn)
                prod[:, cols] = jnp.dot(av, b_ref[:, cols], preferred_element_type=F32)
        body(tuple(prod_refs) + tuple(in_refs), out_refs, acc_refs)

    scratch = [pltpu.VMEM((tm, b.shape[1]), F32) for _, b in mms]
    res = pl.pallas_call(kern, name=name, grid=(rows // tm,), in_specs=in_specs, out_specs=out_specs,
                         out_shape=out_shape, scratch_shapes=scratch, input_output_aliases=aliases,
                         compiler_params=_params(("arbitrary",)))(*args)
    return res


def _rms(xv, gv):
    r = lax.rsqrt(jnp.mean(xv * xv, axis=-1, keepdims=True) + EPS)
    return ((xv * r) * gv).astype(BF16)


def _rms_fwd(name, x, width, cb, g, rows):
    def body(ins, outs, _):
        outs[0][...] = _rms(ins[0][...].astype(F32), ins[1][...])
    return _ew(name, body, [(x, width, cb), (g.reshape(1, width), None, None)], [(width, BF16)], rows)[0]


def _rms_bwd(name, x, width, cb, g, dh_mm, rows, out_dtype, dres=None, into=()):
    def body(ins, outs, accs):
        dhv, xv, gv = ins[0][...], ins[1][...].astype(F32), ins[2][...]
        r = lax.rsqrt(jnp.mean(xv * xv, axis=-1, keepdims=True) + EPS)
        xhat = xv * r
        accs[0][...] += jnp.sum(dhv * xhat, axis=0, keepdims=True)
        dy = dhv * gv
        dx = r * (dy - xhat * jnp.mean(dy * xhat, axis=-1, keepdims=True))
        if dres is not None:
            dx = dx + ins[3][...]
        outs[0][...] = dx.astype(out_dtype)
    ins = [(x, width, cb), (g.reshape(1, width), None, None)]
    if dres is not None:
        ins.append((dres, width, 0))
    return _ew(name, body, ins, [(width, out_dtype) + tuple(into)], rows, accs=[(1, width)], mms=[dh_mm])


def _mm(name, a, b, out_dtype, f32_cols=None, norms=(), tn=MM_TN):
    M, K = a.shape
    N = b.shape[1]
    tm, tn = _row_tile(M, b.size * b.dtype.itemsize), min(tn, N)
    c0, cw = f32_cols if f32_cols else (0, 0)
    n_norm, n_f32 = len(norms), 1 if f32_cols else 0
    assert c0 % tn == 0 and cw % tn == 0
    assert all(nc // tn == (nc + g.shape[-1] - 1) // tn for nc, g in norms)

    def kern(*refs):
        a_ref, b_ref, g_refs = refs[0], refs[1], refs[2:2 + n_norm]
        o_ref, extra = refs[2 + n_norm], refs[3 + n_norm:]
        av = a_ref[...].astype(BF16)
        for j in range(N // tn):
            cols = slice(j * tn, (j + 1) * tn)
            part = jnp.dot(av, b_ref[:, cols], preferred_element_type=F32)
            o_ref[:, cols] = part.astype(o_ref.dtype)
            if f32_cols and c0 <= j * tn and (j + 1) * tn <= c0 + cw:
                extra[0][:, j * tn - c0:(j + 1) * tn - c0] = part
            for k, (nc, g) in enumerate(norms):
                if nc // tn == j:
                    seg = part[:, nc - j * tn:nc - j * tn + g.shape[-1]]
                    extra[n_f32 + k][...] = _rms(seg, g_refs[k][...])

    in_specs = [pl.BlockSpec((tm, K), lambda i: (i, 0)), pl.BlockSpec((K, N), lambda i: (0, 0))]
    in_specs += [pl.BlockSpec((1, g.shape[-1]), lambda i: (0, 0)) for _, g in norms]
    widths = [(N, out_dtype)] + ([(cw, F32)] if f32_cols else []) + [(g.shape[-1], BF16) for _, g in norms]
    return pl.pallas_call(
        kern, name=name, grid=(M // tm,), in_specs=in_specs,
        out_specs=[pl.BlockSpec((tm, w), lambda i: (i, 0)) for w, _ in widths],
        out_shape=[jax.ShapeDtypeStruct((M, w), dt) for w, dt in widths],
        compiler_params=_params(("parallel",)))(a, b, *[g.reshape(1, -1) for _, g in norms])


def _mm_tn(name, a, b, tk=2048, tn=2048):
    T, M = a.shape
    N = b.shape[1]
    tn, tk = min(tn, N), min(tk, T)

    def kern(a_ref, b_ref, o_ref):
        k = pl.program_id(1)
        part = _dot_tn(a_ref[...].astype(BF16), b_ref[...].astype(BF16))

        @pl.when(k == 0)
        def _():
            o_ref[...] = part

        @pl.when(k > 0)
        def _():
            o_ref[...] += part

    return pl.pallas_call(
        kern, name=name, grid=(N // tn, T // tk),
        in_specs=[pl.BlockSpec((tk, M), lambda j, k: (k, 0)), pl.BlockSpec((tk, tn), lambda j, k: (k, j))],
        out_specs=pl.BlockSpec((M, tn), lambda j, k: (0, j)),
        out_shape=jax.ShapeDtypeStruct((M, N), F32),
        compiler_params=_params(("parallel", "arbitrary")))(a, b)


def _dot_nt(a, b):
    return lax.dot_general(a, b, (((1,), (1,)), ((), ())), preferred_element_type=F32)


def _dot_tn(a, b):
    return lax.dot_general(a, b, (((0,), (0,)), ((), ())), preferred_element_type=F32)


SWA_SCALE = HEAD_DIM ** -0.5


def _swa_band(n, pq_ref, pkp_ref, pkc_ref):
    posk = jnp.concatenate([pkp_ref[...], pkc_ref[...]], axis=0)
    dist = (pq_ref[0] - posk).astype(F32)
    kj = lax.broadcasted_iota(jnp.int32, (2 * BLOCK, BLOCK), 0)
    qi = lax.broadcasted_iota(jnp.int32, (2 * BLOCK, BLOCK), 1)
    t_abs = n * BLOCK + qi
    s_abs = n * BLOCK - BLOCK + kj
    return dist, (s_abs >= 0) & (s_abs <= t_abs) & (t_abs - s_abs < BLOCK)


SWA_GROUP = HEADS // SWA_KV_HEADS


def _head_gate(gate_ref, h):
    pair = gate_ref[:, (h // 2) * LANES:(h // 2 + 1) * LANES].astype(F32)
    return pair if h % 2 == 0 else pltpu.roll(pair, HEAD_DIM, 1)


def _swa_group_q(q_all, g):
    heads = range(g * SWA_GROUP, (g + 1) * SWA_GROUP)
    return jnp.concatenate([(q_all[:, h * LANES:(h + 1) * LANES] * SWA_SCALE).astype(BF16) for h in heads], axis=0)


def _swa_mask(s, dist, valid, h):
    return jnp.where(valid, s - (2.0 ** -(h + 1)) * dist, NEG)


def _rows_to_lanes(rows):
    block = jnp.concatenate(list(rows) + [jnp.zeros((LANES - len(rows), BLOCK), F32)], axis=0)
    return block.T


def _swa_specs(nb):
    prev = lambda b, n: b * nb + jnp.maximum(n - 1, 0)
    own = lambda b, n: b * nb + n
    return [
        pl.BlockSpec((BLOCK, HPAD), lambda b, n: (own(b, n), Z_AQ // HPAD)),
        pl.BlockSpec((BLOCK, KV_W), lambda b, n: (prev(b, n), Z_AK // KV_W)),
        pl.BlockSpec((BLOCK, KV_W), lambda b, n: (own(b, n), Z_AK // KV_W)),
        pl.BlockSpec((BLOCK, KV_W), lambda b, n: (prev(b, n), Z_AV // KV_W)),
        pl.BlockSpec((BLOCK, KV_W), lambda b, n: (own(b, n), Z_AV // KV_W)),
        pl.BlockSpec((1, 1, BLOCK), lambda b, n: (own(b, n), 0, 0)),
        pl.BlockSpec((BLOCK, 1), lambda b, n: (prev(b, n), 0)),
        pl.BlockSpec((BLOCK, 1), lambda b, n: (own(b, n), 0)),
    ]


def _swa_fwd(z, gate, pos_col, pos_row, sink_row, B, S):
    nb = S // BLOCK
    T = B * S

    def kern(q_ref, kp_ref, kc_ref, vp_ref, vc_ref, pq_ref, pkp_ref, pkc_ref, gate_ref, sink_ref,
             oraw_ref, og_ref, lse_ref):
        q_all = q_ref[...]
        kb = jnp.concatenate([kp_ref[...], kc_ref[...]], axis=0).astype(BF16)
        vb = jnp.concatenate([vp_ref[...], vc_ref[...]], axis=0).astype(BF16)
        dist, valid = _swa_band(pl.program_id(1), pq_ref, pkp_ref, pkc_ref)
        lse_rows = []
        for grp in range(SWA_KV_HEADS):
            gcols = slice(grp * LANES, (grp + 1) * LANES)
            s_all = _dot_nt(kb[:, gcols], _swa_group_q(q_all, grp))
            probs = []
            for hh in range(SWA_GROUP):
                h = grp * SWA_GROUP + hh
                s = _swa_mask(s_all[:, hh * BLOCK:(hh + 1) * BLOCK], dist, valid, h)
                sink_h = sink_ref[0:1, h:h + 1]
                m = jnp.maximum(jnp.max(s, axis=0, keepdims=True), sink_h)
                e = jnp.exp(s - m)
                denom = jnp.sum(e, axis=0, keepdims=True) + jnp.exp(sink_h - m)
                probs.append((e * (1.0 / denom)).astype(BF16))
                lse_rows.append(m + jnp.log(denom))
            o_all = jnp.dot(vb[:, gcols].T, jnp.concatenate(probs, axis=1), preferred_element_type=F32)
            for hh in range(SWA_GROUP):
                h = grp * SWA_GROUP + hh
                cols = slice(h * LANES, (h + 1) * LANES)
                o = o_all[:, hh * BLOCK:(hh + 1) * BLOCK].T
                oraw_ref[:, cols] = o
                g = _head_gate(gate_ref, h)
                og_ref[:, cols] = (o * (g * jax.nn.sigmoid(g))).astype(BF16)
        lse_ref[...] = _rows_to_lanes(lse_rows)

    own = lambda b, n: b * nb + n
    in_specs = _swa_specs(nb) + [
        pl.BlockSpec((BLOCK, GATE_W), lambda b, n: (own(b, n), 0)),
        pl.BlockSpec((1, LANES), lambda b, n: (0, 0)),
    ]
    out_specs = [pl.BlockSpec((BLOCK, HPAD), lambda b, n: (own(b, n), 0)),
                 pl.BlockSpec((BLOCK, HPAD), lambda b, n: (own(b, n), 0)),
                 pl.BlockSpec((BLOCK, LANES), lambda b, n: (own(b, n), 0))]
    out_shape = [jax.ShapeDtypeStruct((T, HPAD), F32), jax.ShapeDtypeStruct((T, HPAD), BF16),
                 jax.ShapeDtypeStruct((T, LANES), F32)]
    return pl.pallas_call(kern, name="swa_fwd", grid=(B, nb), in_specs=in_specs, out_specs=out_specs,
                          out_shape=out_shape, compiler_params=_params(("parallel", "arbitrary")))(
        z, z, z, z, z, pos_row, pos_col, pos_col, gate, sink_row)


def _swa_bwd(z, pos_col, pos_row, sink_row, lse, do_raw, delta, dz, B, S):
    nb = S // BLOCK
    T = B * S

    def kern(q_ref, kp_ref, kc_ref, vp_ref, vc_ref, pq_ref, pkp_ref, pkc_ref, sink_ref, lse_ref, do_ref,
             delta_ref, dz_ref, dq_ref, dk_ref, dv_ref, dsink_ref):
        b, n = pl.program_id(0), pl.program_id(1)

        @pl.when(n == 0)
        def _():
            dk_ref[...] = jnp.zeros_like(dk_ref)
            dv_ref[...] = jnp.zeros_like(dv_ref)

        @pl.when((b == 0) & (n == 0))
        def _():
            dsink_ref[...] = jnp.zeros_like(dsink_ref)

        q_all = q_ref[...]
        kb = jnp.concatenate([kp_ref[...], kc_ref[...]], axis=0).astype(BF16)
        vb = jnp.concatenate([vp_ref[...], vc_ref[...]], axis=0).astype(BF16)
        dist, valid = _swa_band(n, pq_ref, pkp_ref, pkc_ref)
        lse_t, delta_t = lse_ref[...].T, delta_ref[...].T
        lane1 = lax.broadcasted_iota(jnp.int32, (1, LANES), 1)
        dsink = jnp.zeros((1, LANES), F32)
        dk_band, dv_band = [], []
        for grp in range(SWA_KV_HEADS):
            gcols = slice(grp * LANES, (grp + 1) * LANES)
            heads = range(grp * SWA_GROUP, (grp + 1) * SWA_GROUP)
            qg = _swa_group_q(q_all, grp)
            dog = jnp.concatenate([do_ref[:, h * LANES:(h + 1) * LANES] for h in heads], axis=0)
            s_all = _dot_nt(kb[:, gcols], qg)
            dp_all = _dot_nt(vb[:, gcols], dog)
            ps, dss = [], []
            for hh, h in enumerate(heads):
                blk = slice(hh * BLOCK, (hh + 1) * BLOCK)
                lse_h, delta_h = lse_t[h:h + 1, :], delta_t[h:h + 1, :]
                p = jnp.exp(_swa_mask(s_all[:, blk], dist, valid, h) - lse_h)
                ps.append(p.astype(BF16))
                dss.append((p * (dp_all[:, blk] - delta_h)).astype(BF16))
                psink = jnp.exp(sink_ref[0:1, h:h + 1] - lse_h)
                dsink = dsink + jnp.where(lane1 == h, -jnp.sum(psink * delta_h, axis=1, keepdims=True), 0.0)
            dsg = jnp.concatenate(dss, axis=1)
            dq_all = jnp.dot(kb[:, gcols].T, dsg, preferred_element_type=F32) * SWA_SCALE
            for hh, h in enumerate(heads):
                dq_ref[:, h * LANES:(h + 1) * LANES] = dq_all[:, hh * BLOCK:(hh + 1) * BLOCK].T.astype(BF16)
            dk_band.append(jnp.dot(dsg, qg, preferred_element_type=F32))
            dv_band.append(jnp.dot(jnp.concatenate(ps, axis=1), dog, preferred_element_type=F32))
        dsink_ref[...] += dsink
        dkb = jnp.concatenate(dk_band, axis=1)
        dvb = jnp.concatenate(dv_band, axis=1)
        r_prev = pl.ds(pl.multiple_of(jnp.maximum(n - 1, 0) * BLOCK, BLOCK), BLOCK)
        r_own = pl.ds(pl.multiple_of(n * BLOCK, BLOCK), BLOCK)
        dk_ref[r_prev, :] += dkb[:BLOCK]
        dk_ref[r_own, :] += dkb[BLOCK:]
        dv_ref[r_prev, :] += dvb[:BLOCK]
        dv_ref[r_own, :] += dvb[BLOCK:]

    own = lambda b, n: b * nb + n
    in_specs = _swa_specs(nb) + [
        pl.BlockSpec((1, LANES), lambda b, n: (0, 0)),
        pl.BlockSpec((BLOCK, LANES), lambda b, n: (own(b, n), 0)),
        pl.BlockSpec((BLOCK, HPAD), lambda b, n: (own(b, n), 0)),
        pl.BlockSpec((BLOCK, LANES), lambda b, n: (own(b, n), 0)),
        pl.BlockSpec(memory_space=pl.ANY),
    ]
    out_specs = [pl.BlockSpec((BLOCK, HPAD), lambda b, n: (own(b, n), Z_AQ // HPAD)),
                 pl.BlockSpec((S, KV_W), lambda b, n: (b, 0)),
                 pl.BlockSpec((S, KV_W), lambda b, n: (b, 0)),
                 pl.BlockSpec((1, LANES), lambda b, n: (0, 0))]
    out_shape = [jax.ShapeDtypeStruct(dz.shape, dz.dtype), jax.ShapeDtypeStruct((T, KV_W), F32),
                 jax.ShapeDtypeStruct((T, KV_W), F32), jax.ShapeDtypeStruct((1, LANES), F32)]
    return pl.pallas_call(kern, name="swa_bwd", grid=(B, nb), in_specs=in_specs, out_specs=out_specs,
                          out_shape=out_shape, input_output_aliases={len(in_specs) - 1: 0},
                          compiler_params=_params(("arbitrary", "arbitrary")))(
        z, z, z, z, z, pos_row, pos_col, pos_col, sink_row, lse, do_raw, delta, dz)


MLA_T = 256
MLA_HG = 4
MLA_W = MLA_HG * LANES
MLA_SCALE = MLA_QK ** -0.5
LOG2E = 1.4426950408889634
MLA_QSCALE = MLA_SCALE * LOG2E


def _causal_t(s):
    key = lax.broadcasted_iota(jnp.int32, s.shape, 0)
    query = lax.broadcasted_iota(jnp.int32, s.shape, 1)
    return jnp.where(key <= query, s, NEG)


def _mla_fwd(q, k, v, z, B, S):
    T = B * S
    nq = S // MLA_T

    def kern(q_ref, k_ref, v_ref, gate_ref, oraw_ref, og_ref, lse_ref):
        i = pl.program_id(2)

        def scores(j):
            rows = pl.ds(pl.multiple_of(j * MLA_T, MLA_T), MLA_T)
            return tuple(_dot_nt(k_ref[rows, hh * LANES:(hh + 1) * LANES], q_ref[:, hh * LANES:(hh + 1) * LANES])
                         for hh in range(MLA_HG))

        def update(j, ss, state):
            rows = pl.ds(pl.multiple_of(j * MLA_T, MLA_T), MLA_T)
            out = []
            for hh in range(MLA_HG):
                (m, l, acc), s = state[hh], ss[hh]
                m_new = jnp.maximum(m, jnp.max(s, axis=0, keepdims=True))
                alpha = jnp.exp2(m - m_new)
                p = jnp.exp2(s - m_new)
                l = alpha * l + jnp.sum(p, axis=0, keepdims=True)
                pv = jnp.dot(v_ref[rows, hh * LANES:(hh + 1) * LANES].T, p.astype(BF16), preferred_element_type=F32)
                out.append((m_new, l, alpha * acc + pv))
            return tuple(out)

        def body(pair, state):
            j = 2 * pair
            s0, s1 = scores(j), scores(j + 1)
            return update(j + 1, s1, update(j, s0, state))

        init = tuple((jnp.full((1, MLA_T), NEG, F32), jnp.zeros((1, MLA_T), F32), jnp.zeros((LANES, MLA_T), F32))
                     for _ in range(MLA_HG))
        state = lax.fori_loop(0, i // 2, body, init)
        state = lax.cond(i % 2 == 1, lambda st: update(i - 1, scores(i - 1), st), lambda st: st, state)
        state = update(i, tuple(_causal_t(s) for s in scores(i)), state)
        for hh in range(MLA_HG):
            m, l, acc = state[hh]
            cols = slice(hh * LANES, (hh + 1) * LANES)
            o = (acc * (1.0 / l)).T
            oraw_ref[:, cols] = o.astype(BF16)
            g = _head_gate(gate_ref, hh)
            og_ref[:, cols] = (o * (g * jax.nn.sigmoid(g))).astype(BF16)
            lse_ref[0, 0, 0, hh:hh + 1, :] = m + jnp.log2(l)

    blk = lambda b, h, i: (b * nq + i, h)
    in_specs = [pl.BlockSpec((MLA_T, MLA_W), blk),
                pl.BlockSpec((S, MLA_W), lambda b, h, i: (b, h)),
                pl.BlockSpec((S, MLA_W), lambda b, h, i: (b, h)),
                pl.BlockSpec((MLA_T, MLA_W // 2), lambda b, h, i: (b * nq + i, Z_BGATE // (MLA_W // 2) + h))]
    out_specs = [pl.BlockSpec((MLA_T, MLA_W), blk), pl.BlockSpec((MLA_T, MLA_W), blk),
                 pl.BlockSpec((1, 1, 1, MLA_HG, MLA_T), lambda b, h, i: (b, h, i, 0, 0))]
    out_shape = [jax.ShapeDtypeStruct((T, HPAD), BF16), jax.ShapeDtypeStruct((T, HPAD), BF16),
                 jax.ShapeDtypeStruct((B, HEADS // MLA_HG, nq, MLA_HG, MLA_T), F32)]
    return pl.pallas_call(kern, name="mla_fwd", grid=(B, HEADS // MLA_HG, nq), in_specs=in_specs,
                          out_specs=out_specs, out_shape=out_shape,
                          compiler_params=_params(("parallel", "parallel", "arbitrary")))(q, k, v, z)


def _mla_bwd(q, k, v, do_raw, lse, delta, B, S):
    T = B * S
    nk = S // MLA_T

    def kern(q_ref, k_ref, v_ref, do_ref, lse_ref, delta_ref, dq_ref, dk_ref, dv_ref, dq_acc, dk_acc, dv_acc):
        j = pl.program_id(2)

        @pl.when(j == 0)
        def _():
            dq_acc[...] = jnp.zeros_like(dq_acc)

        dk_acc[...] = jnp.zeros_like(dk_acc)
        dv_acc[...] = jnp.zeros_like(dv_acc)
        kts = [k_ref[:, hh * LANES:(hh + 1) * LANES].T for hh in range(MLA_HG)]

        def step(i, masked):
            rows = pl.ds(pl.multiple_of(i * MLA_T, MLA_T), MLA_T)
            for hh in range(MLA_HG):
                cols = slice(hh * LANES, (hh + 1) * LANES)
                qv, do = q_ref[rows, cols], do_ref[rows, cols]
                st = _dot_nt(k_ref[:, cols], qv)
                if masked:
                    st = _causal_t(st)
                pt = jnp.exp2(st - lse_ref[0, 0, i, hh:hh + 1, :])
                dpt = _dot_nt(v_ref[:, cols], do)
                dst = (pt * (dpt - delta_ref[0, 0, i, hh:hh + 1, :])).astype(BF16)
                dv_acc[:, cols] += jnp.dot(pt.astype(BF16), do, preferred_element_type=F32)
                dk_acc[:, cols] += jnp.dot(dst, qv, preferred_element_type=F32)
                dq_acc[hh, i] += jnp.dot(kts[hh], dst, preferred_element_type=F32)

        step(j, True)

        def body(i, c):
            step(i, False)
            return c

        lax.fori_loop(j + 1, nk, body, 0)
        dk_ref[...] = (dk_acc[...] * (1.0 / LOG2E)).astype(BF16)
        dv_ref[...] = dv_acc[...].astype(BF16)

        @pl.when(j == nk - 1)
        def _():
            for hh in range(MLA_HG):
                for t in range(nk):
                    dq_ref[t * MLA_T:(t + 1) * MLA_T, hh * LANES:(hh + 1) * LANES] = dq_acc[hh, t].T.astype(BF16)

    whole = lambda b, h, j: (b, h)
    tile = lambda b, h, j: (b * nk + j, h)
    stats = pl.BlockSpec((1, 1, nk, MLA_HG, MLA_T), lambda b, h, j: (b, h, 0, 0, 0))
    in_specs = [pl.BlockSpec((S, MLA_W), whole), pl.BlockSpec((MLA_T, MLA_W), tile),
                pl.BlockSpec((MLA_T, MLA_W), tile), pl.BlockSpec((S, MLA_W), whole), stats, stats]
    out_specs = [pl.BlockSpec((S, MLA_W), whole), pl.BlockSpec((MLA_T, MLA_W), tile),
                 pl.BlockSpec((MLA_T, MLA_W), tile)]
    out_shape = [jax.ShapeDtypeStruct((T, HPAD), BF16)] * 3
    scratch = [pltpu.VMEM((MLA_HG, nk, LANES, MLA_T), F32), pltpu.VMEM((MLA_T, MLA_W), F32),
               pltpu.VMEM((MLA_T, MLA_W), F32)]
    return pl.pallas_call(kern, name="mla_bwd", grid=(B, HEADS // MLA_HG, nk), in_specs=in_specs,
                          out_specs=out_specs, out_shape=out_shape, scratch_shapes=scratch,
                          compiler_params=_params(("parallel", "parallel", "arbitrary")))(
        q, k, v, do_raw, lse, delta)


def _rope_tables(pos_col, inv_lane, rows):
    def body(ins, outs, _):
        ang = ins[0][...].astype(F32) * ins[1][...]
        lane = lax.broadcasted_iota(jnp.int32, ang.shape, 1)
        cos, sin = jnp.cos(ang), jnp.sin(ang)
        first = (lane >= HEAD_DIM) & (lane < HEAD_DIM + MLA_ROPE // 2)
        second = (lane >= HEAD_DIM + MLA_ROPE // 2) & (lane < MLA_QK)
        outs[0][...] = jnp.where(lane < HEAD_DIM, 1.0, jnp.where(lane < MLA_QK, cos, 0.0))
        outs[1][...] = jnp.where(first, -sin, 0.0)
        outs[2][...] = jnp.where(second, sin, 0.0)
    return _ew("rope_tables", body, [(pos_col, 1, 0), (inv_lane, None, None)], [(LANES, F32)] * 3, rows)


def _rope(x, c, s1, s2):
    return x * c + pltpu.roll(x, 112, 1) * s1 + pltpu.roll(x, 16, 1) * s2


def _rope_t(d, c, s1, s2):
    return d * c + pltpu.roll(d * s1, 16, 1) + pltpu.roll(d * s2, 112, 1)


def _mla_prep(qdn, w_uq, kvdn, w_ukv, z, tabs, rows):
    def body(ins, outs, _):
        q_pre, kv_pre = ins[0], ins[1]
        c, s1, s2 = ins[3][...], ins[4][...], ins[5][...]
        kr = _rope(ins[2][...].astype(F32), c, s1, s2)
        for h in range(HEADS):
            cols = slice(h * LANES, (h + 1) * LANES)
            outs[0][:, cols] = (_rope(q_pre[:, cols], c, s1, s2) * MLA_QSCALE).astype(BF16)
            outs[1][:, cols] = (kv_pre[:, cols] + kr).astype(BF16)
        outs[2][...] = kv_pre[:, HPAD:].astype(BF16)
    ins = [(z, LANES, Z_BKR // LANES), (tabs[0], LANES, 0), (tabs[1], LANES, 0), (tabs[2], LANES, 0)]
    return _ew("mla_prep", body, ins, [(HPAD, BF16)] * 3, rows, mms=[(qdn, w_uq), (kvdn, w_ukv)])


def _mla_prep_bwd(dq, dk, dv, tabs, dz, rows):
    def body(ins, outs, _):
        c, s1, s2 = ins[3][...], ins[4][...], ins[5][...]
        lane = lax.broadcasted_iota(jnp.int32, c.shape, 1)
        dkr = jnp.zeros(c.shape, F32)
        for h in range(HEADS):
            cols = slice(h * LANES, (h + 1) * LANES)
            outs[0][:, cols] = _rope_t(ins[0][:, cols].astype(F32) * MLA_SCALE, c, s1, s2).astype(BF16)
            dkh = ins[1][:, cols].astype(F32)
            outs[1][:, cols] = jnp.where(lane < HEAD_DIM, dkh, 0.0).astype(BF16)
            dkr = dkr + dkh
        outs[1][:, HPAD:] = ins[2][...].astype(BF16)
        live = (lane >= HEAD_DIM) & (lane < MLA_QK)
        outs[2][...] = jnp.where(live, _rope_t(jnp.where(live, dkr, 0.0), c, s1, s2), 0.0).astype(BF16)
    ins = [(dq, HPAD, 0), (dk, HPAD, 0), (dv, HPAD, 0), (tabs[0], LANES, 0), (tabs[1], LANES, 0),
           (tabs[2], LANES, 0)]
    outs = [(HPAD, BF16), (2 * HPAD, BF16), (LANES, BF16, dz, Z_BKR // LANES)]
    return _ew("mla_prep_bwd", body, ins, outs, rows)


def _gate_bwd(name, d_o_mm, o_raw, gate, gate_cb, dz, dz_cb, rows):
    def body(ins, outs, _):
        lane = lax.broadcasted_iota(jnp.int32, outs[2].shape, 1)
        delta = jnp.zeros(outs[2].shape, F32)
        d_gate = [None] * HEADS
        for h in range(HEADS):
            cols = slice(h * LANES, (h + 1) * LANES)
            dog, o, g = ins[0][:, cols], ins[1][:, cols].astype(F32), _head_gate(ins[2], h)
            sg = jax.nn.sigmoid(g)
            do = dog * (g * sg)
            outs[0][:, cols] = do.astype(BF16)
            d_gate[h] = dog * o * (sg * (1.0 + g * (1.0 - sg)))
            delta = jnp.where(lane == h, jnp.sum(do * o, axis=-1, keepdims=True), delta)
        for pair in range(HEADS // 2):
            packed = d_gate[2 * pair] + pltpu.roll(d_gate[2 * pair + 1], HEAD_DIM, 1)
            outs[1][:, pair * LANES:(pair + 1) * LANES] = packed.astype(BF16)
        outs[2][...] = delta
    ins = [(o_raw, HPAD, 0), (gate, GATE_W, gate_cb)]
    outs = [(HPAD, BF16), (GATE_W, BF16, dz, dz_cb), (LANES, F32)]
    return _ew(name, body, ins, outs, rows, mms=[d_o_mm])


def _merge_out(ua, ub, z, w_out, x0, g_next, rows):
    tm = _row_tile(rows)

    def kern(ua_ref, ub_ref, ma_ref, mb_ref, w_ref, x0_ref, g_ref, y_ref, x1_ref, hn_ref):
        ua_v, ub_v, m_a, m_b = (r[...].astype(F32) for r in (ua_ref, ub_ref, ma_ref, mb_ref))
        y = (jax.nn.sigmoid(m_a) * ua_v + jax.nn.sigmoid(m_b) * ub_v).astype(BF16)
        y_ref[...] = y
        for j in range(D_MODEL // MM_TN):
            cols = slice(j * MM_TN, (j + 1) * MM_TN)
            x1_ref[:, cols] = jnp.dot(y, w_ref[:, cols], preferred_element_type=F32) + x0_ref[:, cols]
        hn_ref[...] = _rms(x1_ref[...], g_ref[...])

    row = lambda cb: pl.BlockSpec((tm, D_MODEL), lambda i: (i, cb))
    return pl.pallas_call(
        kern, name="merge_out", grid=(rows // tm,),
        in_specs=[row(0), row(0), row(Z_MA // D_MODEL), row(Z_MB // D_MODEL),
                  pl.BlockSpec(w_out.shape, lambda i: (0, 0)), row(0), pl.BlockSpec((1, D_MODEL), lambda i: (0, 0))],
        out_specs=[row(0), row(0), row(0)],
        out_shape=[jax.ShapeDtypeStruct((rows, D_MODEL), BF16), jax.ShapeDtypeStruct((rows, D_MODEL), F32),
                   jax.ShapeDtypeStruct((rows, D_MODEL), BF16)],
        compiler_params=_params(("parallel",)))(ua, ub, z, z, w_out, x0, g_next.reshape(1, D_MODEL))


def _merge_bwd(dy_mm, ua, ub, z, dz, rows):
    def body(ins, outs, _):
        dyv = ins[0][...]
        for idx in range(2):
            s = jax.nn.sigmoid(ins[3 + idx][...].astype(F32))
            outs[idx][...] = (dyv * s).astype(BF16)
            d_m = (dyv * ins[1 + idx][...].astype(F32) * (s * (1.0 - s))).astype(BF16)
            outs[2][:, idx * D_MODEL:(idx + 1) * D_MODEL] = d_m
    ins = [(ua, D_MODEL, 0), (ub, D_MODEL, 0), (z, D_MODEL, Z_MA // D_MODEL), (z, D_MODEL, Z_MB // D_MODEL)]
    outs = [(D_MODEL, BF16), (D_MODEL, BF16), (2 * D_MODEL, BF16, dz, Z_MA // (2 * D_MODEL))]
    return _ew("merge_bwd", body, ins, outs, rows, mms=[dy_mm])


def _kv_grad_cast(dk, dv, dz, rows):
    def body(ins, outs, _):
        outs[0][:, :KV_W] = ins[0][...].astype(BF16)
        outs[0][:, KV_W:] = ins[1][...].astype(BF16)
    outs = [(2 * KV_W, BF16, dz, Z_AK // (2 * KV_W))]
    return _ew("kv_grad_cast", body, [(dk, KV_W, 0), (dv, KV_W, 0)], outs, rows)[0]


def _ple_fwd(x1, hn, w_pg, p, w_pp, g_next, rows):
    def body(ins, outs, _):
        u, e = ins[0][...], ins[1][...]
        x2 = ins[2][...] + jax.nn.sigmoid(u) * e
        outs[0][...] = x2
        outs[1][...] = u.astype(BF16)
        outs[2][...] = e.astype(BF16)
        if g_next is not None:
            outs[3][...] = _rms(x2, ins[3][...])
    ins = [(x1, D_MODEL, 0)] + ([(g_next.reshape(1, D_MODEL), None, None)] if g_next is not None else [])
    outs = [(D_MODEL, F32), (D_MODEL, BF16), (D_MODEL, BF16)] + ([(D_MODEL, BF16)] if g_next is not None else [])
    return _ew("ple_fwd", body, ins, outs, rows, mms=[(hn, w_pg), (p, w_pp)])


def _ple_bwd(dx2, u, e, rows):
    def body(ins, outs, _):
        d, s = ins[0][...], jax.nn.sigmoid(ins[1][...].astype(F32))
        outs[0][...] = (d * s).astype(BF16)
        outs[1][...] = (d * ins[2][...].astype(F32) * (s * (1.0 - s))).astype(BF16)
    return _ew("ple_bwd", body, [(dx2, D_MODEL, 0), (u, D_MODEL, 0), (e, D_MODEL, 0)],
               [(D_MODEL, BF16)] * 2, rows)


def _loss_head(x, g, target, rows):
    def body(ins, outs, accs):
        xv, gv = ins[0][...], ins[1][...]
        r = lax.rsqrt(jnp.mean(xv * xv, axis=-1, keepdims=True) + EPS)
        xhat = xv * r
        err = xhat * gv - ins[2][...]
        accs[0][...] += jnp.broadcast_to(0.5 * jnp.sum(jnp.mean(err * err, axis=-1, keepdims=True),
                                                       axis=0, keepdims=True), (1, LANES))
        dyv = err * (1.0 / D_MODEL)
        accs[1][...] += jnp.sum(dyv * xhat, axis=0, keepdims=True)
        dy = dyv * gv
        outs[0][...] = r * (dy - xhat * jnp.mean(dy * xhat, axis=-1, keepdims=True))
    ins = [(x, D_MODEL, 0), (g.reshape(1, D_MODEL), None, None), (target, D_MODEL, 0)]
    return _ew("loss_head", body, ins, [(D_MODEL, F32)], rows, accs=[(1, LANES), (1, D_MODEL)])


def _pad_heads_cols(w, n_heads, dim):
    k = w.shape[0]
    return jnp.pad(w.reshape(k, n_heads, dim), ((0, 0), (0, 0), (0, LANES - dim))).reshape(k, n_heads * LANES)


def _unpad_heads_cols(w, n_heads, dim):
    k = w.shape[0]
    return w.reshape(k, n_heads, LANES)[:, :, :dim].reshape(k, n_heads * dim)


def _layer_weights(w, i):
    segs = jnp.split(w['w_in'][i], list(_cumsum(IN_SIZES))[:-1], axis=1)
    a_q, a_k, a_v, a_gate, b_qd, b_kvd, b_kr, b_gate, m_a, m_b = segs
    kr = jnp.pad(b_kr, ((0, 0), (HEAD_DIM, LANES - MLA_QK)))
    w_in = jnp.concatenate([
        m_a, m_b, _pad_heads_cols(a_q, HEADS, HEAD_DIM), a_gate, b_gate, _pad_heads_cols(a_k, SWA_KV_HEADS, HEAD_DIM),
        _pad_heads_cols(a_v, SWA_KV_HEADS, HEAD_DIM), b_qd, b_kvd, kr], axis=1)
    w_uq = _pad_heads_cols(w['w_uq'][i], HEADS, MLA_QK)
    ukv = w['w_ukv'][i].reshape(MLA_KV_LORA, HEADS, 2 * HEAD_DIM)
    pad = ((0, 0), (0, 0), (0, HEAD_DIM))
    w_ukv = jnp.concatenate([jnp.pad(ukv[:, :, :HEAD_DIM], pad).reshape(MLA_KV_LORA, HPAD),
                             jnp.pad(ukv[:, :, HEAD_DIM:], pad).reshape(MLA_KV_LORA, HPAD)], axis=1)
    w_br_a = _pad_heads_cols(w['w_br_a'][i].T, HEADS, HEAD_DIM).T
    w_br_b = _pad_heads_cols(w['w_br_b'][i].T, HEADS, HEAD_DIM).T
    out = dict(w_in=w_in, w_uq=w_uq, w_ukv=w_ukv, w_br_a=w_br_a, w_br_b=w_br_b, w_out=w['w_out'][i],
               w_pg=w['w_ple_gate'][i], w_pp=w['w_ple_proj'][i])
    for name in ('w_in', 'w_uq', 'w_ukv', 'w_br_a', 'w_br_b', 'w_out', 'w_pg'):
        out[name + '_t'] = out[name].T
    return out


def _cumsum(sizes):
    acc, out = 0, []
    for s in sizes:
        acc += s
        out.append(acc)
    return out


def _unpad_grads(g):
    d = g['w_in']
    seg = lambda off, width: d[:, off:off + width]
    b_kr = seg(Z_BKR, LANES)[:, HEAD_DIM:MLA_QK]
    w_in = jnp.concatenate([
        _unpad_heads_cols(seg(Z_AQ, HPAD), HEADS, HEAD_DIM), _unpad_heads_cols(seg(Z_AK, KV_W), SWA_KV_HEADS, HEAD_DIM),
        _unpad_heads_cols(seg(Z_AV, KV_W), SWA_KV_HEADS, HEAD_DIM), seg(Z_AGATE, GATE_W),
        seg(Z_BQD, MLA_Q_LORA), seg(Z_BKVD, MLA_KV_LORA), b_kr, seg(Z_BGATE, GATE_W),
        seg(Z_MA, D_MODEL), seg(Z_MB, D_MODEL)], axis=1)
    w_uq = _unpad_heads_cols(g['w_uq'], HEADS, MLA_QK)
    ukv = g['w_ukv'].reshape(MLA_KV_LORA, 2, HEADS, LANES)[:, :, :, :HEAD_DIM]
    w_ukv = jnp.concatenate([ukv[:, 0], ukv[:, 1]], axis=-1).reshape(MLA_KV_LORA, HEADS * 2 * HEAD_DIM)
    w_br_a = _unpad_heads_cols(g['w_br_a'].T, HEADS, HEAD_DIM).T
    w_br_b = _unpad_heads_cols(g['w_br_b'].T, HEADS, HEAD_DIM).T
    return dict(w_in=w_in, w_uq=w_uq, w_ukv=w_ukv, w_br_a=w_br_a, w_br_b=w_br_b, w_out=g['w_out'],
                w_ple_gate=g['w_pg'], w_ple_proj=g['w_pp'], g_mix=g['g_mix'], sink=g['sink'], g_q=g['g_q'],
                g_kv=g['g_kv'], g_ple=g['g_ple'])


def _layer_fwd(x0, h, p_i, lw, sm, i, pos_col, pos_row, tabs, B, S):
    T = B * S
    z, a_gate, qdn, kvdn = _mm("proj_in", h, lw['w_in'], BF16, f32_cols=(Z_AGATE, GATE_W),
                               norms=[(Z_BQD, sm['g_q'][i]), (Z_BKVD, sm['g_kv'][i])])
    sink_row = jnp.pad(sm['sink'][i], (0, LANES - HEADS)).reshape(1, LANES)
    oa_raw, oa, lse_a = _swa_fwd(z, a_gate, pos_col, pos_row, sink_row, B, S)
    qf, kf, vf = _mla_prep(qdn, lw['w_uq'], kvdn, lw['w_ukv'], z, tabs, T)
    ob_raw, ob, lse_b = _mla_fwd(qf, kf, vf, z, B, S)
    ua, = _mm("proj_br_a", oa, lw['w_br_a'], BF16)
    ub, = _mm("proj_br_b", ob, lw['w_br_b'], BF16)
    y, x1, hn = _merge_out(ua, ub, z, lw['w_out'], x0, sm['g_ple'][i], T)
    g_next = sm['g_mix'][i + 1] if i + 1 < DEPTH else None
    x2, u, e, *h_next = _ple_fwd(x1, hn, lw['w_pg'], p_i, lw['w_pp'], g_next, T)
    saved = dict(x0=x0, h=h, z=z, a_gate=a_gate, sink_row=sink_row, oa_raw=oa_raw, oa=oa, lse_a=lse_a, qdn=qdn, kvdn=kvdn,
                 qf=qf, kf=kf, vf=vf, ob_raw=ob_raw, ob=ob, lse_b=lse_b, ua=ua, ub=ub, y=y, x1=x1, hn=hn,
                 u=u, e=e, p=p_i)
    return x2, (h_next[0] if h_next else None), saved


def _layer_bwd(dx2, sv, lw, sm, i, pos_col, pos_row, tabs, B, S):
    T = B * S
    z = sv['z']
    g = {}
    d_e, d_u = _ple_bwd(dx2, sv['u'], sv['e'], T)
    g['w_pp'] = _mm_tn("grad_pp", sv['p'], d_e)
    g['w_pg'] = _mm_tn("grad_pg", sv['hn'], d_u)
    dx1, g['g_ple'] = _rms_bwd("norm_ple_bwd", sv['x1'], D_MODEL, 0, sm['g_ple'][i], (d_u, lw['w_pg_t']), T, F32,
                               dres=dx2)
    g['w_out'] = _mm_tn("grad_out", sv['y'], dx1)
    dz = lax.empty((T, Z_WIDTH), BF16)
    d_ua, d_ub, dz = _merge_bwd((dx1, lw['w_out_t']), sv['ua'], sv['ub'], z, dz, T)
    g['w_br_a'] = _mm_tn("grad_br_a", sv['oa'], d_ua)
    g['w_br_b'] = _mm_tn("grad_br_b", sv['ob'], d_ub)
    dob_raw, dz, delta_b = _gate_bwd("gate_b_bwd", (d_ub, lw['w_br_b_t']), sv['ob_raw'], z, Z_BGATE // GATE_W,
                                     dz, Z_BGATE // GATE_W, T)
    delta_rows = delta_b[:, :HEADS].reshape(B, S // MLA_T, MLA_T, HEADS // MLA_HG, MLA_HG).transpose(0, 3, 1, 4, 2)
    dq, dk, dv = _mla_bwd(sv['qf'], sv['kf'], sv['vf'], dob_raw, sv['lse_b'], delta_rows, B, S)
    dq_pre, dkv_pre, dz = _mla_prep_bwd(dq, dk, dv, tabs, dz, T)
    g['w_uq'] = _mm_tn("grad_uq", sv['qdn'], dq_pre)
    g['w_ukv'] = _mm_tn("grad_ukv", sv['kvdn'], dkv_pre)
    dz, g['g_q'] = _rms_bwd("norm_q_bwd", z, MLA_Q_LORA, Z_BQD // MLA_Q_LORA, sm['g_q'][i],
                            (dq_pre, lw['w_uq_t']), T, BF16, into=(dz, Z_BQD // MLA_Q_LORA))
    dz, g['g_kv'] = _rms_bwd("norm_kv_bwd", z, MLA_KV_LORA, Z_BKVD // MLA_KV_LORA, sm['g_kv'][i],
                             (dkv_pre, lw['w_ukv_t']), T, BF16, into=(dz, Z_BKVD // MLA_KV_LORA))
    doa_raw, dz, delta_a = _gate_bwd("gate_a_bwd", (d_ua, lw['w_br_a_t']), sv['oa_raw'], sv['a_gate'], 0,
                                     dz, Z_AGATE // GATE_W, T)
    dz, d_ak, d_av, dsink = _swa_bwd(z, pos_col, pos_row, sv['sink_row'], sv['lse_a'], doa_raw, delta_a, dz, B, S)
    dz = _kv_grad_cast(d_ak, d_av, dz, T)
    g['sink'] = dsink[0, :HEADS]
    g['w_in'] = _mm_tn("grad_in", sv['h'], dz, tk=1024, tn=Z_WIDTH // 2)
    dx0, g['g_mix'] = _rms_bwd("norm_mix_bwd", sv['x0'], D_MODEL, 0, sm['g_mix'][i], (dz, lw['w_in_t']), T, F32,
                               dres=dx1)
    for name in ('g_ple', 'g_q', 'g_kv', 'g_mix'):
        g[name] = g[name][0]
    return dx0, g


def _local_step(x, p, positions, wfull, sm, loss_target):
    B, S, _ = x.shape
    T = B * S
    pos_col = positions.reshape(T, 1)
    pos_row = positions.reshape(T // BLOCK, 1, BLOCK)
    half = MLA_ROPE // 2
    inv = ROPE_THETA ** (-jnp.arange(0, MLA_ROPE, 2, dtype=F32) / MLA_ROPE)
    inv_lane = jnp.tile(inv, LANES // half).reshape(1, LANES)
    tabs = _rope_tables(pos_col, inv_lane, T)
    xc = x.reshape(T, D_MODEL)
    h = _rms_fwd("norm_mix", xc, D_MODEL, 0, sm['g_mix'][0], T)
    lws, saved = [], []
    for i in range(DEPTH):
        lw = _layer_weights(wfull, i)
        xc, h, sv = _layer_fwd(xc, h, p[i].reshape(T, PLE_DIM), lw, sm, i, pos_col, pos_row, tabs, B, S)
        lws.append(lw)
        saved.append(sv)
    dx, loss, dg_final = _loss_head(xc, sm['g_final'], loss_target.reshape(T, D_MODEL), T)
    layer_grads = [None] * DEPTH
    for i in reversed(range(DEPTH)):
        dx, g = _layer_bwd(dx, saved[i], lws[i], sm, i, pos_col, pos_row, tabs, B, S)
        layer_grads[i] = _unpad_grads(g)
    return loss, dx.reshape(B, S, D_MODEL), layer_grads, dg_final[0]


SMALL_ROWS = 48


SMALL_SIZE = 2 * (2 * D_MODEL + HEADS + MLA_Q_LORA + MLA_KV_LORA) + D_MODEL


def _pack_small(arrs, tail=()):
    flat = jnp.concatenate([arrs[name].reshape(-1) for name in SMALL] + [t.reshape(1) for t in tail])
    return jnp.pad(flat, (0, SMALL_ROWS * LANES - flat.shape[0])).reshape(SMALL_ROWS, LANES)


def _unpack_small(block, shapes):
    flat = block.reshape(-1)
    out, off = {}, 0
    for name in SMALL:
        n = math.prod(shapes[name])
        out[name] = flat[off:off + n].reshape(shapes[name])
        off += n
    return out


def _flipped(shard_shape):
    return shard_shape[-1] % LANES != 0


def _to_slots(g, axis):
    r, c = g.shape
    if axis == 0:
        return g.reshape(N_CHIPS, r // N_CHIPS, c)
    return g.reshape(r, N_CHIPS, c // N_CHIPS).transpose(1, 0, 2)


def _div_tile(rows, cap):
    return next(t for t in range(min(cap, rows) // 8 * 8, 0, -8) if rows % t == 0)


def _units(shapes):
    units = []
    for w, shape in enumerate(shapes):
        r = shape[-2]
        n = next(n for n in (8, 7, 4, 2, 1) if r % (8 * n) == 0) if r >= 1024 else 1
        units += [(w, k * (r // n), r // n) for k in range(n)]
    return units


def _place():
    x, y, c = lax.axis_index("x"), lax.axis_index("y"), lax.axis_index("c")
    chips = [(1 - x, y), (x, 1 - y), (1 - x, 1 - y)]
    return x, y, c, chips


ANY = pl.BlockSpec(memory_space=pl.ANY)


def _remote(send_sems, recv_sems, k, src, dst, to):
    return pltpu.make_async_remote_copy(src_ref=src, dst_ref=dst, send_sem=send_sems.at[k],
                                        recv_sem=recv_sems.at[k], device_id=to, device_id_type=MESH)


def _gather_weights(shards, carried):
    n, nc = len(shards), len(carried)
    units = _units([s.shape for s in shards])
    nu = len(units)

    def body(*refs):
        ins, outs = refs[:n], refs[n + nc:2 * n + nc]
        send_sems, recv_sems, local_sems = refs[2 * (n + nc):]
        x, y, c, chips = _place()
        me = 2 * x + y
        sibling = (x, y, 1 - c)
        copy = functools.partial(_remote, send_sems, recv_sems)
        keeps, sends = [], []
        for u, (w, r0, nr) in enumerate(units):
            rows = pl.ds(r0, nr)
            keeps.append(pltpu.make_async_copy(ins[w].at[:, rows, :], outs[w].at[me, :, rows, :], local_sems.at[u]))
            keeps[-1].start()
        for j, (cx, cy) in enumerate(chips):
            for u, (w, r0, nr) in enumerate(units):
                rows = pl.ds(r0, nr)
                sends.append(copy(j * nu + u, ins[w].at[c, rows, :], outs[w].at[me, c, rows, :], (cx, cy, c)))
                sends[-1].start()
        for j, (cx, cy) in enumerate(chips):
            for u, (w, r0, nr) in enumerate(units):
                landed = outs[w].at[2 * cx + cy, c, pl.ds(r0, nr), :]
                copy(j * nu + u, landed, landed, (cx, cy, c)).wait_recv()
                sends.append(copy((3 + j) * nu + u, landed, landed, sibling))
                sends[-1].start()
        for j, (cx, cy) in enumerate(chips):
            for u, (w, r0, nr) in enumerate(units):
                other = outs[w].at[2 * cx + cy, 1 - c, pl.ds(r0, nr), :]
                copy((3 + j) * nu + u, other, other, sibling).wait_recv()
        for cp in sends:
            cp.wait_send()
        for keep in keeps:
            keep.wait()

    out_shape = [jax.ShapeDtypeStruct((N_CHIPS,) + s.shape, s.dtype) for s in shards]
    out_shape += [jax.ShapeDtypeStruct(a.shape, a.dtype) for a in carried]
    res = pl.pallas_call(
        body, name="gather_weights", out_shape=out_shape,
        in_specs=[ANY] * (n + nc), out_specs=[ANY] * (n + nc),
        input_output_aliases={n + k: n + k for k in range(nc)},
        scratch_shapes=[pltpu.SemaphoreType.DMA((6 * nu,)), pltpu.SemaphoreType.DMA((6 * nu,)),
                        pltpu.SemaphoreType.DMA((nu,))])(*shards, *carried)
    return res[:n], res[n:]


def _pair_exchange(g0, g1):
    n = len(g0)

    def body(*refs):
        layers, outs = (refs[:n], refs[n:2 * n]), refs[2 * n:3 * n]
        send_sems, recv_sems = refs[3 * n:]
        x, y, c, _ = _place()
        copy = functools.partial(_remote, send_sems, recv_sems)
        for w in range(n):
            for q in range(N_CHIPS):
                for layer in range(DEPTH):
                    cp = copy(N_CHIPS * w + q, layers[layer][w].at[q], outs[w].at[q], (x, y, 1 - c))
                    pl.when(c == 1 - layer)(cp.start)
        for w in range(n):
            for q in range(N_CHIPS):
                copy(N_CHIPS * w + q, layers[0][w].at[q], outs[w].at[q], (x, y, 1 - c)).wait()

    return pl.pallas_call(
        body, name="pair_exchange", out_shape=[jax.ShapeDtypeStruct(g.shape, g.dtype) for g in g0],
        in_specs=[ANY] * (2 * n), out_specs=[ANY] * n,
        scratch_shapes=[pltpu.SemaphoreType.DMA((N_CHIPS * n,)), pltpu.SemaphoreType.DMA((N_CHIPS * n,))])(*g0, *g1)


def _pair_sum(name, g0, g1, theirs, cflag):
    shape = theirs.shape
    rows, width = shape[0] * shape[1], shape[2]

    def body(ins, outs, _):
        mine = jnp.where(ins[3][0:1, 0:1] == 0.0, ins[0][...], ins[1][...])
        tot = mine + ins[2][...]
        outs[0][...] = tot
        outs[1][...] = tot.astype(BF16)
    ins = [(a.reshape(rows, width), width, 0) for a in (g0, g1, theirs)] + [(cflag, None, None)]
    f32, bf16 = _ew(name, body, ins, [(width, F32), (width, BF16)], rows, tm=_div_tile(rows, ROW_TILE))
    return f32.reshape(shape), bf16.reshape(shape)


def _chip_exchange(parts):
    n = len(parts)

    def body(*refs):
        ins, outs = refs[:n], refs[n:2 * n]
        send_sems, recv_sems = refs[2 * n:]
        x, y, c, chips = _place()
        copy = functools.partial(_remote, send_sems, recv_sems)
        sends = []
        for j, (cx, cy) in enumerate(chips):
            for w in range(n):
                sends.append(copy(j * n + w, ins[w].at[2 * cx + cy], outs[w].at[j], (cx, cy, c)))
                sends[-1].start()
        for j, (cx, cy) in enumerate(chips):
            for w in range(n):
                copy(j * n + w, outs[w].at[j], outs[w].at[j], (cx, cy, c)).wait_recv()
        for cp in sends:
            cp.wait_send()

    return pl.pallas_call(
        body, name="chip_exchange",
        out_shape=[jax.ShapeDtypeStruct((3,) + a.shape[1:], a.dtype) for a in parts],
        in_specs=[ANY] * n, out_specs=[ANY] * n,
        scratch_shapes=[pltpu.SemaphoreType.DMA((3 * n,)), pltpu.SemaphoreType.DMA((3 * n,))])(*parts)


def _chip_sum(name, part, landed, chipflag):
    _, r, width = part.shape
    tm = _div_tile(r, ROW_TILE // 2)

    def kern(p_ref, l_ref, flag_ref, o_ref):
        me = flag_ref[0:1, 0:1]
        own = jnp.where(me == 0.0, p_ref[0], jnp.where(me == 1.0, p_ref[1], jnp.where(me == 2.0, p_ref[2], p_ref[3])))
        o_ref[...] = ((own + l_ref[0].astype(F32)) + l_ref[1].astype(F32)) + l_ref[2].astype(F32)

    return pl.pallas_call(
        kern, name=name, grid=(r // tm,),
        in_specs=[pl.BlockSpec((N_CHIPS, tm, width), lambda i: (0, i, 0)),
                  pl.BlockSpec((3, tm, width), lambda i: (0, i, 0)),
                  pl.BlockSpec((1, LANES), lambda i: (0, 0))],
        out_specs=pl.BlockSpec((tm, width), lambda i: (i, 0)),
        out_shape=jax.ShapeDtypeStruct((r, width), F32), compiler_params=_params(("arbitrary",)))(part, landed, chipflag)


def _pair_broadcast(mine):
    n = len(mine)
    units = _units([a.shape for a in mine])

    def body(*refs):
        ins, outs = refs[:n], refs[n:2 * n]
        send_sems, recv_sems = refs[2 * n:]
        x, y, c, _ = _place()
        copy = functools.partial(_remote, send_sems, recv_sems)
        cps = [copy(u, ins[w].at[pl.ds(r0, nr), :], outs[w].at[pl.ds(r0, nr), :], (x, y, 1 - c))
               for u, (w, r0, nr) in enumerate(units)]
        for cp in cps:
            cp.start()
        for cp in cps:
            cp.wait()

    return pl.pallas_call(
        body, name="pair_broadcast", out_shape=[jax.ShapeDtypeStruct(a.shape, a.dtype) for a in mine],
        in_specs=[ANY] * n, out_specs=[ANY] * n,
        scratch_shapes=[pltpu.SemaphoreType.DMA((len(units),)), pltpu.SemaphoreType.DMA((len(units),))])(*mine)


def _small_allreduce(v):
    offsets = [(dx, dy, dc) for dx in (0, 1) for dy in (0, 1) for dc in (0, 1)][1:]

    def body(v_ref, out_ref, recv_ref, send_sems, recv_sems):
        x, y, c, _ = _place()
        flip = lambda a, d: 1 - a if d else a
        peers = [(flip(x, dx), flip(y, dy), flip(c, dc)) for dx, dy, dc in offsets]
        copy = functools.partial(_remote, send_sems, recv_sems)
        me = 4 * x + 2 * y + c
        recv_ref[me] = v_ref[...]
        cps = [copy(k, v_ref, recv_ref.at[me], peer) for k, peer in enumerate(peers)]
        for cp in cps:
            cp.start()
        for k, (px, py, pc) in enumerate(peers):
            landed = recv_ref.at[4 * px + 2 * py + pc]
            copy(k, landed, landed, (px, py, pc)).wait_recv()
        for cp in cps:
            cp.wait_send()
        tot = recv_ref[0]
        for d in range(1, 8):
            tot = tot + recv_ref[d]
        out_ref[...] = tot

    vmem = pl.BlockSpec(memory_space=pltpu.VMEM)
    return pl.pallas_call(
        body, name="small_allreduce", out_shape=jax.ShapeDtypeStruct(v.shape, v.dtype),
        in_specs=[vmem], out_specs=vmem,
        scratch_shapes=[pltpu.VMEM((8,) + v.shape, v.dtype), pltpu.SemaphoreType.DMA((7,)),
                        pltpu.SemaphoreType.DMA((7,))])(v)


def _adam_math(gv, wv, mv, vv):
    mv = ADAM_B1 * mv + (1.0 - ADAM_B1) * gv
    vv = ADAM_B2 * vv + (1.0 - ADAM_B2) * (gv * gv)
    m_hat = mv / (1.0 - ADAM_B1 ** ADAM_STEP)
    v_hat = vv / (1.0 - ADAM_B2 ** ADAM_STEP)
    return -ADAM_LR * (m_hat / (jnp.sqrt(v_hat) + ADAM_EPS) + ADAM_WD * wv), mv, vv


def _adamw_big(name, mine, theirs, cflag, w, m, v):
    _, r, width = w.shape
    tm = _div_tile(r, ROW_TILE // 2)

    def kern(mine_ref, theirs_ref, flag_ref, w_ref, m_ref, v_ref, g_ref, d_ref, nm_ref, nv_ref):
        layer = pl.program_id(0).astype(F32)
        gv = jnp.where(flag_ref[0:1, 0:1] == layer, mine_ref[...], theirs_ref[...])
        g_ref[0] = gv
        d_ref[0], nm_ref[0], nv_ref[0] = _adam_math(gv, w_ref[0], m_ref[0], v_ref[0])

    flat = pl.BlockSpec((tm, width), lambda l, i: (i, 0))
    stacked = pl.BlockSpec((1, tm, width), lambda l, i: (l, i, 0))
    return pl.pallas_call(
        kern, name=name, grid=(DEPTH, r // tm),
        in_specs=[flat, flat, pl.BlockSpec((1, LANES), lambda l, i: (0, 0)), stacked, stacked, stacked],
        out_specs=[stacked] * 4, out_shape=[jax.ShapeDtypeStruct(w.shape, F32)] * 4,
        compiler_params=_params(("arbitrary", "arbitrary")))(mine, theirs, cflag, w, m, v)


def _adamw_small(g, w, m, v):
    def body(ins, outs, _):
        outs[0][...], outs[1][...], outs[2][...] = _adam_math(*(r[...] for r in ins))
    return _ew("adamw_small", body, [(a, LANES, 0) for a in (g, w, m, v)], [(LANES, F32)] * 3, SMALL_ROWS)


def kernel(x, p, positions, g_mix, w_in, sink, g_q, w_uq, g_kv, w_ukv, w_br_a, w_br_b, w_out, g_ple, w_ple_gate, w_ple_proj, g_final, loss_target, m_g_mix, m_w_in, m_sink, m_g_q, m_w_uq, m_g_kv, m_w_ukv, m_w_br_a, m_w_br_b, m_w_out, m_g_ple, m_w_ple_gate, m_w_ple_proj, m_g_final, v_g_mix, v_w_in, v_sink, v_g_q, v_w_uq, v_g_kv, v_w_ukv, v_w_br_a, v_w_br_b, v_w_out, v_g_ple, v_w_ple_gate, v_w_ple_proj, v_g_final):
    w = dict(g_mix=g_mix, w_in=w_in, sink=sink, g_q=g_q, w_uq=w_uq, g_kv=g_kv, w_ukv=w_ukv, w_br_a=w_br_a,
             w_br_b=w_br_b, w_out=w_out, g_ple=g_ple, w_ple_gate=w_ple_gate, w_ple_proj=w_ple_proj, g_final=g_final)
    m = dict(g_mix=m_g_mix, w_in=m_w_in, sink=m_sink, g_q=m_g_q, w_uq=m_w_uq, g_kv=m_g_kv, w_ukv=m_w_ukv,
             w_br_a=m_w_br_a, w_br_b=m_w_br_b, w_out=m_w_out, g_ple=m_g_ple, w_ple_gate=m_w_ple_gate,
             w_ple_proj=m_w_ple_proj, g_final=m_g_final)
    v = dict(g_mix=v_g_mix, w_in=v_w_in, sink=v_sink, g_q=v_g_q, w_uq=v_w_uq, g_kv=v_g_kv, w_ukv=v_w_ukv,
             w_br_a=v_w_br_a, w_br_b=v_w_br_b, w_out=v_w_out, g_ple=v_g_ple, w_ple_gate=v_w_ple_gate,
             w_ple_proj=v_w_ple_proj, g_final=v_g_final)
    wfull = _gather_full(w)
    sm = {name: w[name] for name in SMALL}
    loss_row, grad_x, layer_grads, dg_final = _local_step(x, p, positions, wfull, sm, loss_target)
    res, loss = _update(layer_grads, dg_final, loss_row[0, 0], w, m, v)
    return (loss, grad_x, *[res[name][kind] for kind in range(4) for name in WEIGHT_NAMES])


def _gather_behind(shards):
    n = len(shards)
    srcs = [jax.new_ref(s, memory_space=pltpu.MemorySpace.HBM) for s in shards]
    lands = [jax.empty_ref(jax.ShapeDtypeStruct((N_CHIPS,) + s.shape, s.dtype), memory_space=pltpu.MemorySpace.HBM)
             for s in shards]

    @pl.kernel(mesh=plsc.ScalarSubcoreMesh(axis_name="sequencer", num_cores=1), name="gather_behind",
               scratch_types=(pltpu.SemaphoreType.DMA((3 * n,)), pltpu.SemaphoreType.DMA((3 * n,)),
                              pltpu.SemaphoreType.DMA((n,))),
               compiler_params=pltpu.CompilerParams(collective_id=0))
    def launch(send_sems, recv_sems, local_sems):
        x, y, c, chips = _place()
        me = 2 * x + y
        barrier = pltpu.get_barrier_semaphore()
        for cx, cy in chips:
            pl.semaphore_signal(barrier, inc=1, device_id=(cx, cy, c), device_id_type=MESH)
        pl.semaphore_wait(barrier, len(chips))
        copy = functools.partial(_remote, send_sems, recv_sems)
        keeps = [pltpu.make_async_copy(srcs[w], lands[w].at[me], local_sems.at[w]) for w in range(n)]
        cps = [copy(j * n + w, srcs[w], lands[w].at[me], (cx, cy, c))
               for j, (cx, cy) in enumerate(chips) for w in range(n)]
        for cp in keeps + cps:
            cp.start()
        for cp in keeps + cps:
            cp.wait()

    launch()
    return [land[...] for land in lands]


def _gather_full(w):
    shards = [w[name].astype(BF16) for name, _ in SHARDED]
    first, later = _gather_weights([s[0].reshape((2, s.shape[1] // 2) + s.shape[2:]) for s in shards],
                                   [s[1] for s in shards])
    second = _gather_behind(later)
    full = {}
    for k, (name, axis) in enumerate(SHARDED):
        layer0 = first[k].reshape((N_CHIPS,) + shards[k].shape[1:])
        full[name] = [jnp.concatenate(list(blocks), axis=axis - 1) for blocks in (layer0, second[k])]
    return full


def _update(layer_grads, dg_final, loss_local, w, m, v):
    small_shapes = {name: w[name].shape for name in SMALL}
    cflag = jnp.full((1, LANES), lax.axis_index("c"), F32)
    chipflag = jnp.full((1, LANES), 2 * lax.axis_index("x") + lax.axis_index("y"), F32)

    slots = [[_to_slots(layer_grads[layer][name], axis - 1) for name, axis in SHARDED] for layer in range(DEPTH)]
    theirs = _pair_exchange(slots[0], slots[1])
    pair = [_pair_sum("pair_sum_" + name, slots[0][k], slots[1][k], theirs[k], cflag)
            for k, (name, _) in enumerate(SHARDED)]
    landed = _chip_exchange([bf16 for _, bf16 in pair])
    mine = [_chip_sum("chip_sum_" + name, pair[k][0], landed[k], chipflag) for k, (name, _) in enumerate(SHARDED)]
    other = _pair_broadcast(mine)
    res = {}
    for k, (name, _) in enumerate(SHARDED):
        flip = _flipped(w[name].shape)
        view = (lambda a: jnp.swapaxes(a, -1, -2)) if flip else (lambda a: a)
        outs = _adamw_big("adamw_" + name, view(mine[k]), view(other[k]), cflag, view(w[name]), view(m[name]),
                          view(v[name]))
        res[name] = tuple(view(a) for a in outs)

    gsmall = {name: jnp.stack([layer_grads[layer][name] for layer in range(DEPTH)]) for name in SMALL[:-1]}
    gsmall['g_final'] = dg_final
    gsum = _small_allreduce(_pack_small(gsmall, tail=[loss_local]))
    small = (gsum,) + tuple(_adamw_small(gsum, _pack_small(w), _pack_small(m), _pack_small(v)))
    for name, arrs in zip(SMALL, zip(*[[_unpack_small(a, small_shapes)[n] for n in SMALL] for a in small])):
        res[name] = arrs
    return res, gsum.reshape(-1)[SMALL_SIZE]
```

```python
import functools
import math

import jax
import jax.numpy as jnp
from jax import lax
from jax.experimental import pallas as pl
from jax.experimental.pallas import tpu as pltpu
from jax.experimental.pallas import tpu_sc as plsc

F32 = jnp.float32
BF16 = jnp.bfloat16

D_MODEL = 1024
DEPTH = 2
PLE_DIM = 256
BLOCK = 128
EPS = 1e-6
NEG = -1e30
HEADS = 8
SWA_KV_HEADS = 2
HEAD_DIM = 64
LANES = 128
HPAD = HEADS * LANES
MLA_QK = 96
MLA_ROPE = 32
MLA_Q_LORA = 256
MLA_KV_LORA = 128
ROPE_THETA = 10000.0
IN_SIZES = (512, 128, 128, 512, 256, 128, 32, 512, 1024, 1024)

Z_MA, Z_MB, Z_AQ, Z_AGATE, Z_BGATE = 0, 1024, 2048, 3072, 3584
Z_AK, Z_AV, Z_BQD, Z_BKVD, Z_BKR = 4096, 4352, 4608, 4864, 4992
Z_WIDTH = 5120
GATE_W = HEADS * HEAD_DIM
KV_W = SWA_KV_HEADS * LANES

ADAM_LR, ADAM_B1, ADAM_B2, ADAM_EPS, ADAM_WD, ADAM_STEP = 0.001, 0.9, 0.999, 1e-08, 0.01, 10

VMEM_LIMIT = 56 * 1024 * 1024
MESH = pl.DeviceIdType.MESH

WEIGHT_NAMES = ('g_mix', 'w_in', 'sink', 'g_q', 'w_uq', 'g_kv', 'w_ukv', 'w_br_a', 'w_br_b',
                'w_out', 'g_ple', 'w_ple_gate', 'w_ple_proj', 'g_final')
SHARDED = (('w_in', 2), ('w_uq', 2), ('w_ukv', 2), ('w_br_a', 2), ('w_br_b', 2),
           ('w_out', 1), ('w_ple_gate', 1), ('w_ple_proj', 2))
SMALL = ('g_mix', 'sink', 'g_q', 'g_kv', 'g_ple', 'g_final')
N_CHIPS = 4


def _params(sem):
    return pltpu.CompilerParams(dimension_semantics=sem, vmem_limit_bytes=VMEM_LIMIT)


MM_TN = 512
ROW_TILE = 512
BIG_WEIGHT_BYTES = 16 * 1024 * 1024


def _row_tile(rows, weight_bytes=0):
    tm = ROW_TILE // 2 if weight_bytes > BIG_WEIGHT_BYTES else ROW_TILE
    return min(tm, rows)


def _ew(name, body, ins, outs, rows, accs=(), mms=(), tm=None):
    n_mm, n_in, n_out = len(mms), len(ins), len(outs)
    if tm is None:
        tm = _row_tile(rows, sum(b.size * b.dtype.itemsize for _, b in mms))
    in_specs, args = [], []
    for a, b in mms:
        in_specs += [pl.BlockSpec((tm, a.shape[1]), lambda i: (i, 0)), pl.BlockSpec(b.shape, lambda i: (0, 0))]
        args += [a, b]
    for arr, width, cb in ins:
        if width is None:
            in_specs.append(pl.BlockSpec(arr.shape, lambda i, nd=arr.ndim: (0,) * nd))
        else:
            in_specs.append(pl.BlockSpec((tm, width), lambda i, cb=cb: (i, cb)))
        args.append(arr)
    out_shape, out_specs, aliases = [], [], {}
    for k, out in enumerate(outs):
        if len(out) == 4:
            aliases[len(args)] = k
            in_specs.append(pl.BlockSpec(memory_space=pl.ANY))
            args.append(out[2])
            out_shape.append(jax.ShapeDtypeStruct(out[2].shape, out[2].dtype))
            out_specs.append(pl.BlockSpec((tm, out[0]), lambda i, cb=out[3]: (i, cb)))
        else:
            out_shape.append(jax.ShapeDtypeStruct((rows, out[0]), out[1]))
            out_specs.append(pl.BlockSpec((tm, out[0]), lambda i: (i, 0)))
    n_in += len(aliases)
    out_shape += [jax.ShapeDtypeStruct(s, F32) for s in accs]
    out_specs += [pl.BlockSpec(s, lambda i: (0, 0)) for s in accs]

    def kern(*refs):
        mm_refs, refs = refs[:2 * n_mm], refs[2 * n_mm:]
        in_refs, out_refs = refs[:n_in - len(aliases)], refs[n_in:n_in + n_out]
        acc_refs, prod_refs = refs[n_in + n_out:n_in + n_out + len(accs)], refs[n_in + n_out + len(accs):]
        if acc_refs:
            @pl.when(pl.program_id(0) == 0)
            def _():
                for r in acc_refs:
                    r[...] = jnp.zeros_like(r)
        for k in range(n_mm):
            a_ref, b_ref, prod = mm_refs[2 * k], mm_refs[2 * k + 1], prod_refs[k]
            av = a_ref[...].astype(BF16)
            n = b_ref.shape[1]
            tn = min(MM_TN, n)
            for j in range(n // tn):
                cols = slice(j * tn, (j + 1) * tn)
                prod[:, cols] = jnp.dot(av, b_ref[:, cols], preferred_element_type=F32)
        body(tuple(prod_refs) + tuple(in_refs), out_refs, acc_refs)

    scratch = [pltpu.VMEM((tm, b.shape[1]), F32) for _, b in mms]
    res = pl.pallas_call(kern, name=name, grid=(rows // tm,), in_specs=in_specs, out_specs=out_specs,
                         out_shape=out_shape, scratch_shapes=scratch, input_output_aliases=aliases,
                         compiler_params=_params(("arbitrary",)))(*args)
    return res


def _rms(xv, gv):
    r = lax.rsqrt(jnp.mean(xv * xv, axis=-1, keepdims=True) + EPS)
    return ((xv * r) * gv).astype(BF16)


def _rms_fwd(name, x, width, cb, g, rows):
    def body(ins, outs, _):
        outs[0][...] = _rms(ins[0][...].astype(F32), ins[1][...])
    return _ew(name, body, [(x, width, cb), (g.reshape(1, width), None, None)], [(width, BF16)], rows)[0]


def _rms_bwd(name, x, width, cb, g, dh_mm, rows, out_dtype, dres=None, into=()):
    def body(ins, outs, accs):
        dhv, xv, gv = ins[0][...], ins[1][...].astype(F32), ins[2][...]
        r = lax.rsqrt(jnp.mean(xv * xv, axis=-1, keepdims=True) + EPS)
        xhat = xv * r
        accs[0][...] += jnp.sum(dhv * xhat, axis=0, keepdims=True)
        dy = dhv * gv
        dx = r * (dy - xhat * jnp.mean(dy * xhat, axis=-1, keepdims=True))
        if dres is not None:
            dx = dx + ins[3][...]
        outs[0][...] = dx.astype(out_dtype)
    ins = [(x, width, cb), (g.reshape(1, width), None, None)]
    if dres is not None:
        ins.append((dres, width, 0))
    return _ew(name, body, ins, [(width, out_dtype) + tuple(into)], rows, accs=[(1, width)], mms=[dh_mm])


def _mm(name, a, b, out_dtype, f32_cols=None, norms=(), tn=MM_TN):
    M, K = a.shape
    N = b.shape[1]
    tm, tn = _row_tile(M, b.size * b.dtype.itemsize), min(tn, N)
    c0, cw = f32_cols if f32_cols else (0, 0)
    n_norm, n_f32 = len(norms), 1 if f32_cols else 0
    assert c0 % tn == 0 and cw % tn == 0
    assert all(nc // tn == (nc + g.shape[-1] - 1) // tn for nc, g in norms)

    def kern(*refs):
        a_ref, b_ref, g_refs = refs[0], refs[1], refs[2:2 + n_norm]
        o_ref, extra = refs[2 + n_norm], refs[3 + n_norm:]
        av = a_ref[...].astype(BF16)
        for j in range(N // tn):
            cols = slice(j * tn, (j + 1) * tn)
            part = jnp.dot(av, b_ref[:, cols], preferred_element_type=F32)
            o_ref[:, cols] = part.astype(o_ref.dtype)
            if f32_cols and c0 <= j * tn and (j + 1) * tn <= c0 + cw:
                extra[0][:, j * tn - c0:(j + 1) * tn - c0] = part
            for k, (nc, g) in enumerate(norms):
                if nc // tn == j:
                    seg = part[:, nc - j * tn:nc - j * tn + g.shape[-1]]
                    extra[n_f32 + k][...] = _rms(seg, g_refs[k][...])

    in_specs = [pl.BlockSpec((tm, K), lambda i: (i, 0)), pl.BlockSpec((K, N), lambda i: (0, 0))]
    in_specs += [pl.BlockSpec((1, g.shape[-1]), lambda i: (0, 0)) for _, g in norms]
    widths = [(N, out_dtype)] + ([(cw, F32)] if f32_cols else []) + [(g.shape[-1], BF16) for _, g in norms]
    return pl.pallas_call(
        kern, name=name, grid=(M // tm,), in_specs=in_specs,
        out_specs=[pl.BlockSpec((tm, w), lambda i: (i, 0)) for w, _ in widths],
        out_shape=[jax.ShapeDtypeStruct((M, w), dt) for w, dt in widths],
        compiler_params=_params(("parallel",)))(a, b, *[g.reshape(1, -1) for _, g in norms])


def _mm_tn(name, a, b, tk=2048, tn=2048):
    T, M = a.shape
    N = b.shape[1]
    tn, tk = min(tn, N), min(tk, T)

    def kern(a_ref, b_ref, o_ref):
        k = pl.program_id(1)
        part = _dot_tn(a_ref[...].astype(BF16), b_ref[...].astype(BF16))

        @pl.when(k == 0)
        def _():
            o_ref[...] = part

        @pl.when(k > 0)
        def _():
            o_ref[...] += part

    return pl.pallas_call(
        kern, name=name, grid=(N // tn, T // tk),
        in_specs=[pl.BlockSpec((tk, M), lambda j, k: (k, 0)), pl.BlockSpec((tk, tn), lambda j, k: (k, j))],
        out_specs=pl.BlockSpec((M, tn), lambda j, k: (0, j)),
        out_shape=jax.ShapeDtypeStruct((M, N), F32),
        compiler_params=_params(("parallel", "arbitrary")))(a, b)


def _dot_nt(a, b):
    return lax.dot_general(a, b, (((1,), (1,)), ((), ())), preferred_element_type=F32)


def _dot_tn(a, b):
    return lax.dot_general(a, b, (((0,), (0,)), ((), ())), preferred_element_type=F32)


SWA_SCALE = HEAD_DIM ** -0.5


def _swa_band(n, pq_ref, pkp_ref, pkc_ref):
    posk = jnp.concatenate([pkp_ref[...], pkc_ref[...]], axis=0)
    dist = (pq_ref[0] - posk).astype(F32)
    kj = lax.broadcasted_iota(jnp.int32, (2 * BLOCK, BLOCK), 0)
    qi = lax.broadcasted_iota(jnp.int32, (2 * BLOCK, BLOCK), 1)
    t_abs = n * BLOCK + qi
    s_abs = n * BLOCK - BLOCK + kj
    return dist, (s_abs >= 0) & (s_abs <= t_abs) & (t_abs - s_abs < BLOCK)


SWA_GROUP = HEADS // SWA_KV_HEADS


def _head_gate(gate_ref, h):
    pair = gate_ref[:, (h // 2) * LANES:(h // 2 + 1) * LANES].astype(F32)
    return pair if h % 2 == 0 else pltpu.roll(pair, HEAD_DIM, 1)


def _swa_group_q(q_all, g):
    heads = range(g * SWA_GROUP, (g + 1) * SWA_GROUP)
    return jnp.concatenate([(q_all[:, h * LANES:(h + 1) * LANES] * SWA_SCALE).astype(BF16) for h in heads], axis=0)


def _swa_mask(s, dist, valid, h):
    return jnp.where(valid, s - (2.0 ** -(h + 1)) * dist, NEG)


def _rows_to_lanes(rows):
    block = jnp.concatenate(list(rows) + [jnp.zeros((LANES - len(rows), BLOCK), F32)], axis=0)
    return block.T


def _swa_specs(nb):
    prev = lambda b, n: b * nb + jnp.maximum(n - 1, 0)
    own = lambda b, n: b * nb + n
    return [
        pl.BlockSpec((BLOCK, HPAD), lambda b, n: (own(b, n), Z_AQ // HPAD)),
        pl.BlockSpec((BLOCK, KV_W), lambda b, n: (prev(b, n), Z_AK // KV_W)),
        pl.BlockSpec((BLOCK, KV_W), lambda b, n: (own(b, n), Z_AK // KV_W)),
        pl.BlockSpec((BLOCK, KV_W), lambda b, n: (prev(b, n), Z_AV // KV_W)),
        pl.BlockSpec((BLOCK, KV_W), lambda b, n: (own(b, n), Z_AV // KV_W)),
        pl.BlockSpec((1, 1, BLOCK), lambda b, n: (own(b, n), 0, 0)),
        pl.BlockSpec((BLOCK, 1), lambda b, n: (prev(b, n), 0)),
        pl.BlockSpec((BLOCK, 1), lambda b, n: (own(b, n), 0)),
    ]


def _swa_fwd(z, gate, pos_col, pos_row, sink_row, B, S):
    nb = S // BLOCK
    T = B * S

    def kern(q_ref, kp_ref, kc_ref, vp_ref, vc_ref, pq_ref, pkp_ref, pkc_ref, gate_ref, sink_ref,
             oraw_ref, og_ref, lse_ref):
        q_all = q_ref[...]
        kb = jnp.concatenate([kp_ref[...], kc_ref[...]], axis=0).astype(BF16)
        vb = jnp.concatenate([vp_ref[...], vc_ref[...]], axis=0).astype(BF16)
        dist, valid = _swa_band(pl.program_id(1), pq_ref, pkp_ref, pkc_ref)
        lse_rows = []
        for grp in range(SWA_KV_HEADS):
            gcols = slice(grp * LANES, (grp + 1) * LANES)
            s_all = _dot_nt(kb[:, gcols], _swa_group_q(q_all, grp))
            probs = []
            for hh in range(SWA_GROUP):
                h = grp * SWA_GROUP + hh
                s = _swa_mask(s_all[:, hh * BLOCK:(hh + 1) * BLOCK], dist, valid, h)
                sink_h = sink_ref[0:1, h:h + 1]
                m = jnp.maximum(jnp.max(s, axis=0, keepdims=True), sink_h)
                e = jnp.exp(s - m)
                denom = jnp.sum(e, axis=0, keepdims=True) + jnp.exp(sink_h - m)
                probs.append((e * (1.0 / denom)).astype(BF16))
                lse_rows.append(m + jnp.log(denom))
            o_all = jnp.dot(vb[:, gcols].T, jnp.concatenate(probs, axis=1), preferred_element_type=F32)
            for hh in range(SWA_GROUP):
                h = grp * SWA_GROUP + hh
                cols = slice(h * LANES, (h + 1) * LANES)
                o = o_all[:, hh * BLOCK:(hh + 1) * BLOCK].T
                oraw_ref[:, cols] = o
                g = _head_gate(gate_ref, h)
                og_ref[:, cols] = (o * (g * jax.nn.sigmoid(g))).astype(BF16)
        lse_ref[...] = _rows_to_lanes(lse_rows)

    own = lambda b, n: b * nb + n
    in_specs = _swa_specs(nb) + [
        pl.BlockSpec((BLOCK, GATE_W), lambda b, n: (own(b, n), 0)),
        pl.BlockSpec((1, LANES), lambda b, n: (0, 0)),
    ]
    out_specs = [pl.BlockSpec((BLOCK, HPAD), lambda b, n: (own(b, n), 0)),
                 pl.BlockSpec((BLOCK, HPAD), lambda b, n: (own(b, n), 0)),
                 pl.BlockSpec((BLOCK, LANES), lambda b, n: (own(b, n), 0))]
    out_shape = [jax.ShapeDtypeStruct((T, HPAD), F32), jax.ShapeDtypeStruct((T, HPAD), BF16),
                 jax.ShapeDtypeStruct((T, LANES), F32)]
    return pl.pallas_call(kern, name="swa_fwd", grid=(B, nb), in_specs=in_specs, out_specs=out_specs,
                          out_shape=out_shape, compiler_params=_params(("parallel", "arbitrary")))(
        z, z, z, z, z, pos_row, pos_col, pos_col, gate, sink_row)


def _swa_bwd(z, pos_col, pos_row, sink_row, lse, do_raw, delta, dz, B, S):
    nb = S // BLOCK
    T = B * S

    def kern(q_ref, kp_ref, kc_ref, vp_ref, vc_ref, pq_ref, pkp_ref, pkc_ref, sink_ref, lse_ref, do_ref,
             delta_ref, dz_ref, dq_ref, dk_ref, dv_ref, dsink_ref):
        b, n = pl.program_id(0), pl.program_id(1)

        @pl.when(n == 0)
        def _():
            dk_ref[...] = jnp.zeros_like(dk_ref)
            dv_ref[...] = jnp.zeros_like(dv_ref)

        @pl.when((b == 0) & (n == 0))
        def _():
            dsink_ref[...] = jnp.zeros_like(dsink_ref)

        q_all = q_ref[...]
        kb = jnp.concatenate([kp_ref[...], kc_ref[...]], axis=0).astype(BF16)
        vb = jnp.concatenate([vp_ref[...], vc_ref[...]], axis=0).astype(BF16)
        dist, valid = _swa_band(n, pq_ref, pkp_ref, pkc_ref)
        lse_t, delta_t = lse_ref[...].T, delta_ref[...].T
        lane1 = lax.broadcasted_iota(jnp.int32, (1, LANES), 1)
        dsink = jnp.zeros((1, LANES), F32)
        dk_band, dv_band = [], []
        for grp in range(SWA_KV_HEADS):
            gcols = slice(grp * LANES, (grp + 1) * LANES)
            heads = range(grp * SWA_GROUP, (grp + 1) * SWA_GROUP)
            qg = _swa_group_q(q_all, grp)
            dog = jnp.concatenate([do_ref[:, h * LANES:(h + 1) * LANES] for h in heads], axis=0)
            s_all = _dot_nt(kb[:, gcols], qg)
            dp_all = _dot_nt(vb[:, gcols], dog)
            ps, dss = [], []
            for hh, h in enumerate(heads):
                blk = slice(hh * BLOCK, (hh + 1) * BLOCK)
                lse_h, delta_h = lse_t[h:h + 1, :], delta_t[h:h + 1, :]
                p = jnp.exp(_swa_mask(s_all[:, blk], dist, valid, h) - lse_h)
                ps.append(p.astype(BF16))
                dss.append((p * (dp_all[:, blk] - delta_h)).astype(BF16))
                psink = jnp.exp(sink_ref[0:1, h:h + 1] - lse_h)
                dsink = dsink + jnp.where(lane1 == h, -jnp.sum(psink * delta_h, axis=1, keepdims=True), 0.0)
            dsg = jnp.concatenate(dss, axis=1)
            dq_all = jnp.dot(kb[:, gcols].T, dsg, preferred_element_type=F32) * SWA_SCALE
            for hh, h in enumerate(heads):
                dq_ref[:, h * LANES:(h + 1) * LANES] = dq_all[:, hh * BLOCK:(hh + 1) * BLOCK].T.astype(BF16)
            dk_band.append(jnp.dot(dsg, qg, preferred_element_type=F32))
            dv_band.append(jnp.dot(jnp.concatenate(ps, axis=1), dog, preferred_element_type=F32))
        dsink_ref[...] += dsink
        dkb = jnp.concatenate(dk_band, axis=1)
        dvb = jnp.concatenate(dv_band, axis=1)
        r_prev = pl.ds(pl.multiple_of(jnp.maximum(n - 1, 0) * BLOCK, BLOCK), BLOCK)
        r_own = pl.ds(pl.multiple_of(n * BLOCK, BLOCK), BLOCK)
        dk_ref[r_prev, :] += dkb[:BLOCK]
        dk_ref[r_own, :] += dkb[BLOCK:]
        dv_ref[r_prev, :] += dvb[:BLOCK]
        dv_ref[r_own, :] += dvb[BLOCK:]

    own = lambda b, n: b * nb + n
    in_specs = _swa_specs(nb) + [
        pl.BlockSpec((1, LANES), lambda b, n: (0, 0)),
        pl.BlockSpec((BLOCK, LANES), lambda b, n: (own(b, n), 0)),
        pl.BlockSpec((BLOCK, HPAD), lambda b, n: (own(b, n), 0)),
        pl.BlockSpec((BLOCK, LANES), lambda b, n: (own(b, n), 0)),
        pl.BlockSpec(memory_space=pl.ANY),
    ]
    out_specs = [pl.BlockSpec((BLOCK, HPAD), lambda b, n: (own(b, n), Z_AQ // HPAD)),
                 pl.BlockSpec((S, KV_W), lambda b, n: (b, 0)),
                 pl.BlockSpec((S, KV_W), lambda b, n: (b, 0)),
                 pl.BlockSpec((1, LANES), lambda b, n: (0, 0))]
    out_shape = [jax.ShapeDtypeStruct(dz.shape, dz.dtype), jax.ShapeDtypeStruct((T, KV_W), F32),
                 jax.ShapeDtypeStruct((T, KV_W), F32), jax.ShapeDtypeStruct((1, LANES), F32)]
    return pl.pallas_call(kern, name="swa_bwd", grid=(B, nb), in_specs=in_specs, out_specs=out_specs,
                          out_shape=out_shape, input_output_aliases={len(in_specs) - 1: 0},
                          compiler_params=_params(("arbitrary", "arbitrary")))(
        z, z, z, z, z, pos_row, pos_col, pos_col, sink_row, lse, do_raw, delta, dz)


MLA_T = 256
MLA_HG = 4
MLA_W = MLA_HG * LANES
MLA_HGB = 8
MLA_WB = MLA_HGB * LANES
MLA_SCALE = MLA_QK ** -0.5
LOG2E = 1.4426950408889634
MLA_QSCALE = MLA_SCALE * LOG2E


def _causal_t(s):
    key = lax.broadcasted_iota(jnp.int32, s.shape, 0)
    query = lax.broadcasted_iota(jnp.int32, s.shape, 1)
    return jnp.where(key <= query, s, NEG)


def _mla_fwd(q, k, v, z, B, S):
    T = B * S
    nq = S // MLA_T

    def kern(q_ref, k_ref, v_ref, gate_ref, oraw_ref, og_ref, lse_ref):
        i = pl.program_id(2)

        def scores(j):
            rows = pl.ds(pl.multiple_of(j * MLA_T, MLA_T), MLA_T)
            return tuple(_dot_nt(k_ref[rows, hh * LANES:(hh + 1) * LANES], q_ref[:, hh * LANES:(hh + 1) * LANES])
                         for hh in range(MLA_HG))

        def update(j, ss, state):
            rows = pl.ds(pl.multiple_of(j * MLA_T, MLA_T), MLA_T)
            out = []
            for hh in range(MLA_HG):
                (m, l, acc), s = state[hh], ss[hh]
                m_new = jnp.maximum(m, jnp.max(s, axis=0, keepdims=True))
                alpha = jnp.exp2(m - m_new)
                p = jnp.exp2(s - m_new)
                l = alpha * l + jnp.sum(p, axis=0, keepdims=True)
                pv = jnp.dot(v_ref[rows, hh * LANES:(hh + 1) * LANES].T, p.astype(BF16), preferred_element_type=F32)
                out.append((m_new, l, alpha * acc + pv))
            return tuple(out)

        def body(pair, state):
            j = 2 * pair
            s0, s1 = scores(j), scores(j + 1)
            return update(j + 1, s1, update(j, s0, state))

        init = tuple((jnp.full((1, MLA_T), NEG, F32), jnp.zeros((1, MLA_T), F32), jnp.zeros((LANES, MLA_T), F32))
                     for _ in range(MLA_HG))
        state = lax.fori_loop(0, i // 2, body, init)
        state = lax.cond(i % 2 == 1, lambda st: update(i - 1, scores(i - 1), st), lambda st: st, state)
        state = update(i, tuple(_causal_t(s) for s in scores(i)), state)
        for hh in range(MLA_HG):
            m, l, acc = state[hh]
            cols = slice(hh * LANES, (hh + 1) * LANES)
            o = (acc * (1.0 / l)).T
            oraw_ref[:, cols] = o.astype(BF16)
            g = _head_gate(gate_ref, hh)
            og_ref[:, cols] = (o * (g * jax.nn.sigmoid(g))).astype(BF16)
            lse_ref[0, 0, 0, hh:hh + 1, :] = m + jnp.log2(l)

    blk = lambda b, h, i: (b * nq + i, h)
    in_specs = [pl.BlockSpec((MLA_T, MLA_W), blk),
                pl.BlockSpec((S, MLA_W), lambda b, h, i: (b, h)),
                pl.BlockSpec((S, MLA_W), lambda b, h, i: (b, h)),
                pl.BlockSpec((MLA_T, MLA_W // 2), lambda b, h, i: (b * nq + i, Z_BGATE // (MLA_W // 2) + h))]
    out_specs = [pl.BlockSpec((MLA_T, MLA_W), blk), pl.BlockSpec((MLA_T, MLA_W), blk),
                 pl.BlockSpec((1, 1, 1, MLA_HG, MLA_T), lambda b, h, i: (b, h, i, 0, 0))]
    out_shape = [jax.ShapeDtypeStruct((T, HPAD), BF16), jax.ShapeDtypeStruct((T, HPAD), BF16),
                 jax.ShapeDtypeStruct((B, HEADS // MLA_HG, nq, MLA_HG, MLA_T), F32)]
    return pl.pallas_call(kern, name="mla_fwd", grid=(B, HEADS // MLA_HG, nq), in_specs=in_specs,
                          out_specs=out_specs, out_shape=out_shape,
                          compiler_params=_params(("parallel", "parallel", "arbitrary")))(q, k, v, z)


def _mla_bwd(q, k, v, do_raw, lse, delta, B, S):
    T = B * S
    nk = S // MLA_T

    def kern(q_ref, k_ref, v_ref, do_ref, lse_ref, delta_ref, dq_ref, dk_ref, dv_ref, dq_acc, dk_acc, dv_acc):
        j = pl.program_id(2)

        @pl.when(j == 0)
        def _():
            dq_acc[...] = jnp.zeros_like(dq_acc)

        dk_acc[...] = jnp.zeros_like(dk_acc)
        dv_acc[...] = jnp.zeros_like(dv_acc)
        kts = [k_ref[:, hh * LANES:(hh + 1) * LANES].T for hh in range(MLA_HGB)]

        def step(i, masked):
            rows = pl.ds(pl.multiple_of(i * MLA_T, MLA_T), MLA_T)
            for hh in range(MLA_HGB):
                cols = slice(hh * LANES, (hh + 1) * LANES)
                qv, do = q_ref[rows, cols], do_ref[rows, cols]
                st = _dot_nt(k_ref[:, cols], qv)
                if masked:
                    st = _causal_t(st)
                pt = jnp.exp2(st - lse_ref[0, 0, i, hh:hh + 1, :])
                dpt = _dot_nt(v_ref[:, cols], do)
                dst = (pt * (dpt - delta_ref[0, 0, i, hh:hh + 1, :])).astype(BF16)
                dv_acc[:, cols] += jnp.dot(pt.astype(BF16), do, preferred_element_type=F32)
                dk_acc[:, cols] += jnp.dot(dst, qv, preferred_element_type=F32)
                dq_acc[hh, i] += jnp.dot(kts[hh], dst, preferred_element_type=F32)

        step(j, True)

        def body(i, c):
            step(i, False)
            return c

        lax.fori_loop(j + 1, nk, body, 0)
        dk_ref[...] = (dk_acc[...] * (1.0 / LOG2E)).astype(BF16)
        dv_ref[...] = dv_acc[...].astype(BF16)

        @pl.when(j == nk - 1)
        def _():
            for hh in range(MLA_HGB):
                for t in range(nk):
                    dq_ref[t * MLA_T:(t + 1) * MLA_T, hh * LANES:(hh + 1) * LANES] = dq_acc[hh, t].T.astype(BF16)

    whole = lambda b, h, j: (b, h)
    tile = lambda b, h, j: (b * nk + j, h)
    stats = pl.BlockSpec((1, 1, nk, MLA_HGB, MLA_T), lambda b, h, j: (b, h, 0, 0, 0))
    in_specs = [pl.BlockSpec((S, MLA_WB), whole), pl.BlockSpec((MLA_T, MLA_WB), tile),
                pl.BlockSpec((MLA_T, MLA_WB), tile), pl.BlockSpec((S, MLA_WB), whole), stats, stats]
    out_specs = [pl.BlockSpec((S, MLA_WB), whole), pl.BlockSpec((MLA_T, MLA_WB), tile),
                 pl.BlockSpec((MLA_T, MLA_WB), tile)]
    out_shape = [jax.ShapeDtypeStruct((T, HPAD), BF16)] * 3
    scratch = [pltpu.VMEM((MLA_HGB, nk, LANES, MLA_T), F32), pltpu.VMEM((MLA_T, MLA_WB), F32),
               pltpu.VMEM((MLA_T, MLA_WB), F32)]
    return pl.pallas_call(kern, name="mla_bwd", grid=(B, HEADS // MLA_HGB, nk), in_specs=in_specs,
                          out_specs=out_specs, out_shape=out_shape, scratch_shapes=scratch,
                          compiler_params=_params(("parallel", "parallel", "arbitrary")))(
        q, k, v, do_raw, lse, delta)


def _rope_tables(pos_col, inv_lane, rows):
    def body(ins, outs, _):
        ang = ins[0][...].astype(F32) * ins[1][...]
        lane = lax.broadcasted_iota(jnp.int32, ang.shape, 1)
        cos, sin = jnp.cos(ang), jnp.sin(ang)
        first = (lane >= HEAD_DIM) & (lane < HEAD_DIM + MLA_ROPE // 2)
        second = (lane >= HEAD_DIM + MLA_ROPE // 2) & (lane < MLA_QK)
        outs[0][...] = jnp.where(lane < HEAD_DIM, 1.0, jnp.where(lane < MLA_QK, cos, 0.0))
        outs[1][...] = jnp.where(first, -sin, 0.0)
        outs[2][...] = jnp.where(second, sin, 0.0)
    return _ew("rope_tables", body, [(pos_col, 1, 0), (inv_lane, None, None)], [(LANES, F32)] * 3, rows)


def _rope(x, c, s1, s2):
    return x * c + pltpu.roll(x, 112, 1) * s1 + pltpu.roll(x, 16, 1) * s2


def _rope_t(d, c, s1, s2):
    return d * c + pltpu.roll(d * s1, 16, 1) + pltpu.roll(d * s2, 112, 1)


def _mla_prep(qdn, w_uq, kvdn, w_ukv, z, tabs, rows):
    def body(ins, outs, _):
        q_pre, kv_pre = ins[0], ins[1]
        c, s1, s2 = ins[3][...], ins[4][...], ins[5][...]
        kr = _rope(ins[2][...].astype(F32), c, s1, s2)
        for h in range(HEADS):
            cols = slice(h * LANES, (h + 1) * LANES)
            outs[0][:, cols] = (_rope(q_pre[:, cols], c, s1, s2) * MLA_QSCALE).astype(BF16)
            outs[1][:, cols] = (kv_pre[:, cols] + kr).astype(BF16)
        outs[2][...] = kv_pre[:, HPAD:].astype(BF16)
    ins = [(z, LANES, Z_BKR // LANES), (tabs[0], LANES, 0), (tabs[1], LANES, 0), (tabs[2], LANES, 0)]
    return _ew("mla_prep", body, ins, [(HPAD, BF16)] * 3, rows, mms=[(qdn, w_uq), (kvdn, w_ukv)])


def _mla_prep_bwd(dq, dk, dv, tabs, dz, rows):
    def body(ins, outs, _):
        c, s1, s2 = ins[3][...], ins[4][...], ins[5][...]
        lane = lax.broadcasted_iota(jnp.int32, c.shape, 1)
        dkr = jnp.zeros(c.shape, F32)
        for h in range(HEADS):
            cols = slice(h * LANES, (h + 1) * LANES)
            outs[0][:, cols] = _rope_t(ins[0][:, cols].astype(F32) * MLA_SCALE, c, s1, s2).astype(BF16)
            dkh = ins[1][:, cols].astype(F32)
            outs[1][:, cols] = jnp.where(lane < HEAD_DIM, dkh, 0.0).astype(BF16)
            dkr = dkr + dkh
        outs[1][:, HPAD:] = ins[2][...].astype(BF16)
        live = (lane >= HEAD_DIM) & (lane < MLA_QK)
        outs[2][...] = jnp.where(live, _rope_t(jnp.where(live, dkr, 0.0), c, s1, s2), 0.0).astype(BF16)
    ins = [(dq, HPAD, 0), (dk, HPAD, 0), (dv, HPAD, 0), (tabs[0], LANES, 0), (tabs[1], LANES, 0),
           (tabs[2], LANES, 0)]
    outs = [(HPAD, BF16), (2 * HPAD, BF16), (LANES, BF16, dz, Z_BKR // LANES)]
    return _ew("mla_prep_bwd", body, ins, outs, rows)


def _gate_bwd(name, d_o_mm, o_raw, gate, gate_cb, dz, dz_cb, rows):
    def body(ins, outs, _):
        lane = lax.broadcasted_iota(jnp.int32, outs[2].shape, 1)
        delta = jnp.zeros(outs[2].shape, F32)
        d_gate = [None] * HEADS
        for h in range(HEADS):
            cols = slice(h * LANES, (h + 1) * LANES)
            dog, o, g = ins[0][:, cols], ins[1][:, cols].astype(F32), _head_gate(ins[2], h)
            sg = jax.nn.sigmoid(g)
            do = dog * (g * sg)
            outs[0][:, cols] = do.astype(BF16)
            d_gate[h] = dog * o * (sg * (1.0 + g * (1.0 - sg)))
            delta = jnp.where(lane == h, jnp.sum(do * o, axis=-1, keepdims=True), delta)
        for pair in range(HEADS // 2):
            packed = d_gate[2 * pair] + pltpu.roll(d_gate[2 * pair + 1], HEAD_DIM, 1)
            outs[1][:, pair * LANES:(pair + 1) * LANES] = packed.astype(BF16)
        outs[2][...] = delta
    ins = [(o_raw, HPAD, 0), (gate, GATE_W, gate_cb)]
    outs = [(HPAD, BF16), (GATE_W, BF16, dz, dz_cb), (LANES, F32)]
    return _ew(name, body, ins, outs, rows, mms=[d_o_mm])


def _merge_out(ua, ub, z, w_out, x0, g_next, rows):
    tm = _row_tile(rows)

    def kern(ua_ref, ub_ref, ma_ref, mb_ref, w_ref, x0_ref, g_ref, y_ref, x1_ref, hn_ref):
        ua_v, ub_v, m_a, m_b = (r[...].astype(F32) for r in (ua_ref, ub_ref, ma_ref, mb_ref))
        y = (jax.nn.sigmoid(m_a) * ua_v + jax.nn.sigmoid(m_b) * ub_v).astype(BF16)
        y_ref[...] = y
        for j in range(D_MODEL // MM_TN):
            cols = slice(j * MM_TN, (j + 1) * MM_TN)
            x1_ref[:, cols] = jnp.dot(y, w_ref[:, cols], preferred_element_type=F32) + x0_ref[:, cols]
        hn_ref[...] = _rms(x1_ref[...], g_ref[...])

    row = lambda cb: pl.BlockSpec((tm, D_MODEL), lambda i: (i, cb))
    return pl.pallas_call(
        kern, name="merge_out", grid=(rows // tm,),
        in_specs=[row(0), row(0), row(Z_MA // D_MODEL), row(Z_MB // D_MODEL),
                  pl.BlockSpec(w_out.shape, lambda i: (0, 0)), row(0), pl.BlockSpec((1, D_MODEL), lambda i: (0, 0))],
        out_specs=[row(0), row(0), row(0)],
        out_shape=[jax.ShapeDtypeStruct((rows, D_MODEL), BF16), jax.ShapeDtypeStruct((rows, D_MODEL), F32),
                   jax.ShapeDtypeStruct((rows, D_MODEL), BF16)],
        compiler_params=_params(("parallel",)))(ua, ub, z, z, w_out, x0, g_next.reshape(1, D_MODEL))


def _merge_bwd(dy_mm, ua, ub, z, dz, rows):
    def body(ins, outs, _):
        dyv = ins[0][...]
        for idx in range(2):
            s = jax.nn.sigmoid(ins[3 + idx][...].astype(F32))
            outs[idx][...] = (dyv * s).astype(BF16)
            d_m = (dyv * ins[1 + idx][...].astype(F32) * (s * (1.0 - s))).astype(BF16)
            outs[2][:, idx * D_MODEL:(idx + 1) * D_MODEL] = d_m
    ins = [(ua, D_MODEL, 0), (ub, D_MODEL, 0), (z, D_MODEL, Z_MA // D_MODEL), (z, D_MODEL, Z_MB // D_MODEL)]
    outs = [(D_MODEL, BF16), (D_MODEL, BF16), (2 * D_MODEL, BF16, dz, Z_MA // (2 * D_MODEL))]
    return _ew("merge_bwd", body, ins, outs, rows, mms=[dy_mm])


def _kv_grad_cast(dk, dv, dz, rows):
    def body(ins, outs, _):
        outs[0][:, :KV_W] = ins[0][...].astype(BF16)
        outs[0][:, KV_W:] = ins[1][...].astype(BF16)
    outs = [(2 * KV_W, BF16, dz, Z_AK // (2 * KV_W))]
    return _ew("kv_grad_cast", body, [(dk, KV_W, 0), (dv, KV_W, 0)], outs, rows)[0]


def _ple_fwd(x1, hn, w_pg, p, w_pp, g_next, rows):
    def body(ins, outs, _):
        u, e = ins[0][...], ins[1][...]
        x2 = ins[2][...] + jax.nn.sigmoid(u) * e
        outs[0][...] = x2
        outs[1][...] = u.astype(BF16)
        outs[2][...] = e.astype(BF16)
        if g_next is not None:
            outs[3][...] = _rms(x2, ins[3][...])
    ins = [(x1, D_MODEL, 0)] + ([(g_next.reshape(1, D_MODEL), None, None)] if g_next is not None else [])
    outs = [(D_MODEL, F32), (D_MODEL, BF16), (D_MODEL, BF16)] + ([(D_MODEL, BF16)] if g_next is not None else [])
    return _ew("ple_fwd", body, ins, outs, rows, mms=[(hn, w_pg), (p, w_pp)])


def _ple_bwd(dx2, u, e, rows):
    def body(ins, outs, _):
        d, s = ins[0][...], jax.nn.sigmoid(ins[1][...].astype(F32))
        outs[0][...] = (d * s).astype(BF16)
        outs[1][...] = (d * ins[2][...].astype(F32) * (s * (1.0 - s))).astype(BF16)
    return _ew("ple_bwd", body, [(dx2, D_MODEL, 0), (u, D_MODEL, 0), (e, D_MODEL, 0)],
               [(D_MODEL, BF16)] * 2, rows)


def _loss_head(x, g, target, rows):
    def body(ins, outs, accs):
        xv, gv = ins[0][...], ins[1][...]
        r = lax.rsqrt(jnp.mean(xv * xv, axis=-1, keepdims=True) + EPS)
        xhat = xv * r
        err = xhat * gv - ins[2][...]
        accs[0][...] += jnp.broadcast_to(0.5 * jnp.sum(jnp.mean(err * err, axis=-1, keepdims=True),
                                                       axis=0, keepdims=True), (1, LANES))
        dyv = err * (1.0 / D_MODEL)
        accs[1][...] += jnp.sum(dyv * xhat, axis=0, keepdims=True)
        dy = dyv * gv
        outs[0][...] = r * (dy - xhat * jnp.mean(dy * xhat, axis=-1, keepdims=True))
    ins = [(x, D_MODEL, 0), (g.reshape(1, D_MODEL), None, None), (target, D_MODEL, 0)]
    return _ew("loss_head", body, ins, [(D_MODEL, F32)], rows, accs=[(1, LANES), (1, D_MODEL)])


def _pad_heads_cols(w, n_heads, dim):
    k = w.shape[0]
    return jnp.pad(w.reshape(k, n_heads, dim), ((0, 0), (0, 0), (0, LANES - dim))).reshape(k, n_heads * LANES)


def _unpad_heads_cols(w, n_heads, dim):
    k = w.shape[0]
    return w.reshape(k, n_heads, LANES)[:, :, :dim].reshape(k, n_heads * dim)


def _layer_weights(w, i):
    segs = jnp.split(w['w_in'][i], list(_cumsum(IN_SIZES))[:-1], axis=1)
    a_q, a_k, a_v, a_gate, b_qd, b_kvd, b_kr, b_gate, m_a, m_b = segs
    kr = jnp.pad(b_kr, ((0, 0), (HEAD_DIM, LANES - MLA_QK)))
    w_in = jnp.concatenate([
        m_a, m_b, _pad_heads_cols(a_q, HEADS, HEAD_DIM), a_gate, b_gate, _pad_heads_cols(a_k, SWA_KV_HEADS, HEAD_DIM),
        _pad_heads_cols(a_v, SWA_KV_HEADS, HEAD_DIM), b_qd, b_kvd, kr], axis=1)
    w_uq = _pad_heads_cols(w['w_uq'][i], HEADS, MLA_QK)
    ukv = w['w_ukv'][i].reshape(MLA_KV_LORA, HEADS, 2 * HEAD_DIM)
    pad = ((0, 0), (0, 0), (0, HEAD_DIM))
    w_ukv = jnp.concatenate([jnp.pad(ukv[:, :, :HEAD_DIM], pad).reshape(MLA_KV_LORA, HPAD),
                             jnp.pad(ukv[:, :, HEAD_DIM:], pad).reshape(MLA_KV_LORA, HPAD)], axis=1)
    w_br_a = _pad_heads_cols(w['w_br_a'][i].T, HEADS, HEAD_DIM).T
    w_br_b = _pad_heads_cols(w['w_br_b'][i].T, HEADS, HEAD_DIM).T
    out = dict(w_in=w_in, w_uq=w_uq, w_ukv=w_ukv, w_br_a=w_br_a, w_br_b=w_br_b, w_out=w['w_out'][i],
               w_pg=w['w_ple_gate'][i], w_pp=w['w_ple_proj'][i])
    for name in ('w_in', 'w_uq', 'w_ukv', 'w_br_a', 'w_br_b', 'w_out', 'w_pg'):
        out[name + '_t'] = out[name].T
    return out


def _cumsum(sizes):
    acc, out = 0, []
    for s in sizes:
        acc += s
        out.append(acc)
    return out


def _unpad_grads(g):
    d = g['w_in']
    seg = lambda off, width: d[:, off:off + width]
    b_kr = seg(Z_BKR, LANES)[:, HEAD_DIM:MLA_QK]
    w_in = jnp.concatenate([
        _unpad_heads_cols(seg(Z_AQ, HPAD), HEADS, HEAD_DIM), _unpad_heads_cols(seg(Z_AK, KV_W), SWA_KV_HEADS, HEAD_DIM),
        _unpad_heads_cols(seg(Z_AV, KV_W), SWA_KV_HEADS, HEAD_DIM), seg(Z_AGATE, GATE_W),
        seg(Z_BQD, MLA_Q_LORA), seg(Z_BKVD, MLA_KV_LORA), b_kr, seg(Z_BGATE, GATE_W),
        seg(Z_MA, D_MODEL), seg(Z_MB, D_MODEL)], axis=1)
    w_uq = _unpad_heads_cols(g['w_uq'], HEADS, MLA_QK)
    ukv = g['w_ukv'].reshape(MLA_KV_LORA, 2, HEADS, LANES)[:, :, :, :HEAD_DIM]
    w_ukv = jnp.concatenate([ukv[:, 0], ukv[:, 1]], axis=-1).reshape(MLA_KV_LORA, HEADS * 2 * HEAD_DIM)
    w_br_a = _unpad_heads_cols(g['w_br_a'].T, HEADS, HEAD_DIM).T
    w_br_b = _unpad_heads_cols(g['w_br_b'].T, HEADS, HEAD_DIM).T
    return dict(w_in=w_in, w_uq=w_uq, w_ukv=w_ukv, w_br_a=w_br_a, w_br_b=w_br_b, w_out=g['w_out'],
                w_ple_gate=g['w_pg'], w_ple_proj=g['w_pp'], g_mix=g['g_mix'], sink=g['sink'], g_q=g['g_q'],
                g_kv=g['g_kv'], g_ple=g['g_ple'])


def _layer_fwd(x0, h, p_i, lw, sm, i, pos_col, pos_row, tabs, B, S):
    T = B * S
    z, a_gate, qdn, kvdn = _mm("proj_in", h, lw['w_in'], BF16, f32_cols=(Z_AGATE, GATE_W),
                               norms=[(Z_BQD, sm['g_q'][i]), (Z_BKVD, sm['g_kv'][i])])
    sink_row = jnp.pad(sm['sink'][i], (0, LANES - HEADS)).reshape(1, LANES)
    oa_raw, oa, lse_a = _swa_fwd(z, a_gate, pos_col, pos_row, sink_row, B, S)
    qf, kf, vf = _mla_prep(qdn, lw['w_uq'], kvdn, lw['w_ukv'], z, tabs, T)
    ob_raw, ob, lse_b = _mla_fwd(qf, kf, vf, z, B, S)
    ua, = _mm("proj_br_a", oa, lw['w_br_a'], BF16)
    ub, = _mm("proj_br_b", ob, lw['w_br_b'], BF16)
    y, x1, hn = _merge_out(ua, ub, z, lw['w_out'], x0, sm['g_ple'][i], T)
    g_next = sm['g_mix'][i + 1] if i + 1 < DEPTH else None
    x2, u, e, *h_next = _ple_fwd(x1, hn, lw['w_pg'], p_i, lw['w_pp'], g_next, T)
    saved = dict(x0=x0, h=h, z=z, a_gate=a_gate, sink_row=sink_row, oa_raw=oa_raw, oa=oa, lse_a=lse_a, qdn=qdn, kvdn=kvdn,
                 qf=qf, kf=kf, vf=vf, ob_raw=ob_raw, ob=ob, lse_b=lse_b, ua=ua, ub=ub, y=y, x1=x1, hn=hn,
                 u=u, e=e, p=p_i)
    return x2, (h_next[0] if h_next else None), saved


def _layer_bwd(dx2, sv, lw, sm, i, pos_col, pos_row, tabs, B, S):
    T = B * S
    z = sv['z']
    g = {}
    d_e, d_u = _ple_bwd(dx2, sv['u'], sv['e'], T)
    g['w_pp'] = _mm_tn("grad_pp", sv['p'], d_e)
    g['w_pg'] = _mm_tn("grad_pg", sv['hn'], d_u)
    dx1, g['g_ple'] = _rms_bwd("norm_ple_bwd", sv['x1'], D_MODEL, 0, sm['g_ple'][i], (d_u, lw['w_pg_t']), T, F32,
                               dres=dx2)
    g['w_out'] = _mm_tn("grad_out", sv['y'], dx1)
    dz = lax.empty((T, Z_WIDTH), BF16)
    d_ua, d_ub, dz = _merge_bwd((dx1, lw['w_out_t']), sv['ua'], sv['ub'], z, dz, T)
    g['w_br_a'] = _mm_tn("grad_br_a", sv['oa'], d_ua)
    g['w_br_b'] = _mm_tn("grad_br_b", sv['ob'], d_ub)
    dob_raw, dz, delta_b = _gate_bwd("gate_b_bwd", (d_ub, lw['w_br_b_t']), sv['ob_raw'], z, Z_BGATE // GATE_W,
                                     dz, Z_BGATE // GATE_W, T)
    nq, groups = S // MLA_T, HEADS // MLA_HGB
    delta_rows = delta_b[:, :HEADS].reshape(B, nq, MLA_T, groups, MLA_HGB).transpose(0, 3, 1, 4, 2)
    lse_rows = sv['lse_b'].transpose(0, 2, 1, 3, 4).reshape(B, nq, groups, MLA_HGB, MLA_T).transpose(0, 2, 1, 3, 4)
    dq, dk, dv = _mla_bwd(sv['qf'], sv['kf'], sv['vf'], dob_raw, lse_rows, delta_rows, B, S)
    dq_pre, dkv_pre, dz = _mla_prep_bwd(dq, dk, dv, tabs, dz, T)
    g['w_uq'] = _mm_tn("grad_uq", sv['qdn'], dq_pre)
    g['w_ukv'] = _mm_tn("grad_ukv", sv['kvdn'], dkv_pre)
    dz, g['g_q'] = _rms_bwd("norm_q_bwd", z, MLA_Q_LORA, Z_BQD // MLA_Q_LORA, sm['g_q'][i],
                            (dq_pre, lw['w_uq_t']), T, BF16, into=(dz, Z_BQD // MLA_Q_LORA))
    dz, g['g_kv'] = _rms_bwd("norm_kv_bwd", z, MLA_KV_LORA, Z_BKVD // MLA_KV_LORA, sm['g_kv'][i],
                             (dkv_pre, lw['w_ukv_t']), T, BF16, into=(dz, Z_BKVD // MLA_KV_LORA))
    doa_raw, dz, delta_a = _gate_bwd("gate_a_bwd", (d_ua, lw['w_br_a_t']), sv['oa_raw'], sv['a_gate'], 0,
                                     dz, Z_AGATE // GATE_W, T)
    dz, d_ak, d_av, dsink = _swa_bwd(z, pos_col, pos_row, sv['sink_row'], sv['lse_a'], doa_raw, delta_a, dz, B, S)
    dz = _kv_grad_cast(d_ak, d_av, dz, T)
    g['sink'] = dsink[0, :HEADS]
    g['w_in'] = _mm_tn("grad_in", sv['h'], dz, tk=1024, tn=Z_WIDTH // 2)
    dx0, g['g_mix'] = _rms_bwd("norm_mix_bwd", sv['x0'], D_MODEL, 0, sm['g_mix'][i], (dz, lw['w_in_t']), T, F32,
                               dres=dx1)
    for name in ('g_ple', 'g_q', 'g_kv', 'g_mix'):
        g[name] = g[name][0]
    return dx0, g


def _local_step(x, p, positions, wfull, sm, loss_target):
    B, S, _ = x.shape
    T = B * S
    pos_col = positions.reshape(T, 1)
    pos_row = positions.reshape(T // BLOCK, 1, BLOCK)
    half = MLA_ROPE // 2
    inv = ROPE_THETA ** (-jnp.arange(0, MLA_ROPE, 2, dtype=F32) / MLA_ROPE)
    inv_lane = jnp.tile(inv, LANES // half).reshape(1, LANES)
    tabs = _rope_tables(pos_col, inv_lane, T)
    xc = x.reshape(T, D_MODEL)
    h = _rms_fwd("norm_mix", xc, D_MODEL, 0, sm['g_mix'][0], T)
    lws, saved = [], []
    for i in range(DEPTH):
        lw = _layer_weights(wfull, i)
        xc, h, sv = _layer_fwd(xc, h, p[i].reshape(T, PLE_DIM), lw, sm, i, pos_col, pos_row, tabs, B, S)
        lws.append(lw)
        saved.append(sv)
    dx, loss, dg_final = _loss_head(xc, sm['g_final'], loss_target.reshape(T, D_MODEL), T)
    layer_grads = [None] * DEPTH
    for i in reversed(range(DEPTH)):
        dx, g = _layer_bwd(dx, saved[i], lws[i], sm, i, pos_col, pos_row, tabs, B, S)
        layer_grads[i] = _unpad_grads(g)
    return loss, dx.reshape(B, S, D_MODEL), layer_grads, dg_final[0]


SMALL_ROWS = 48


SMALL_SIZE = 2 * (2 * D_MODEL + HEADS + MLA_Q_LORA + MLA_KV_LORA) + D_MODEL


def _pack_small(arrs, tail=()):
    flat = jnp.concatenate([arrs[name].reshape(-1) for name in SMALL] + [t.reshape(1) for t in tail])
    return jnp.pad(flat, (0, SMALL_ROWS * LANES - flat.shape[0])).reshape(SMALL_ROWS, LANES)


def _unpack_small(block, shapes):
    flat = block.reshape(-1)
    out, off = {}, 0
    for name in SMALL:
        n = math.prod(shapes[name])
        out[name] = flat[off:off + n].reshape(shapes[name])
        off += n
    return out


def _flipped(shard_shape):
    return shard_shape[-1] % LANES != 0


def _to_slots(g, axis):
    r, c = g.shape
    if axis == 0:
        return g.reshape(N_CHIPS, r // N_CHIPS, c)
    return g.reshape(r, N_CHIPS, c // N_CHIPS).transpose(1, 0, 2)


def _div_tile(rows, cap):
    return next(t for t in range(min(cap, rows) // 8 * 8, 0, -8) if rows % t == 0)


def _units(shapes):
    units = []
    for w, shape in enumerate(shapes):
        r = shape[-2]
        n = next(n for n in (8, 7, 4, 2, 1) if r % (8 * n) == 0) if r >= 1024 else 1
        units += [(w, k * (r // n), r // n) for k in range(n)]
    return units


def _place():
    x, y, c = lax.axis_index("x"), lax.axis_index("y"), lax.axis_index("c")
    chips = [(1 - x, y), (x, 1 - y), (1 - x, 1 - y)]
    return x, y, c, chips


ANY = pl.BlockSpec(memory_space=pl.ANY)


def _remote(send_sems, recv_sems, k, src, dst, to):
    return pltpu.make_async_remote_copy(src_ref=src, dst_ref=dst, send_sem=send_sems.at[k],
                                        recv_sem=recv_sems.at[k], device_id=to, device_id_type=MESH)


def _gather_weights(shards, carried):
    n, nc = len(shards), len(carried)
    units = _units([s.shape for s in shards])
    nu = len(units)

    def body(*refs):
        ins, outs = refs[:n], refs[n + nc:2 * n + nc]
        send_sems, recv_sems, local_sems = refs[2 * (n + nc):]
        x, y, c, chips = _place()
        me = 2 * x + y
        sibling = (x, y, 1 - c)
        copy = functools.partial(_remote, send_sems, recv_sems)
        keeps, sends = [], []
        for u, (w, r0, nr) in enumerate(units):
            rows = pl.ds(r0, nr)
            keeps.append(pltpu.make_async_copy(ins[w].at[:, rows, :], outs[w].at[me, :, rows, :], local_sems.at[u]))
            keeps[-1].start()
        for j, (cx, cy) in enumerate(chips):
            for u, (w, r0, nr) in enumerate(units):
                rows = pl.ds(r0, nr)
                sends.append(copy(j * nu + u, ins[w].at[c, rows, :], outs[w].at[me, c, rows, :], (cx, cy, c)))
                sends[-1].start()
        for j, (cx, cy) in enumerate(chips):
            for u, (w, r0, nr) in enumerate(units):
                landed = outs[w].at[2 * cx + cy, c, pl.ds(r0, nr), :]
                copy(j * nu + u, landed, landed, (cx, cy, c)).wait_recv()
                sends.append(copy((3 + j) * nu + u, landed, landed, sibling))
                sends[-1].start()
        for j, (cx, cy) in enumerate(chips):
            for u, (w, r0, nr) in enumerate(units):
                other = outs[w].at[2 * cx + cy, 1 - c, pl.ds(r0, nr), :]
                copy((3 + j) * nu + u, other, other, sibling).wait_recv()
        for cp in sends:
            cp.wait_send()
        for keep in keeps:
            keep.wait()

    out_shape = [jax.ShapeDtypeStruct((N_CHIPS,) + s.shape, s.dtype) for s in shards]
    out_shape += [jax.ShapeDtypeStruct(a.shape, a.dtype) for a in carried]
    res = pl.pallas_call(
        body, name="gather_weights", out_shape=out_shape,
        in_specs=[ANY] * (n + nc), out_specs=[ANY] * (n + nc),
        input_output_aliases={n + k: n + k for k in range(nc)},
        scratch_shapes=[pltpu.SemaphoreType.DMA((6 * nu,)), pltpu.SemaphoreType.DMA((6 * nu,)),
                        pltpu.SemaphoreType.DMA((nu,))])(*shards, *carried)
    return res[:n], res[n:]


def _pair_exchange(g0, g1):
    n = len(g0)

    def body(*refs):
        layers, outs = (refs[:n], refs[n:2 * n]), refs[2 * n:3 * n]
        send_sems, recv_sems = refs[3 * n:]
        x, y, c, _ = _place()
        copy = functools.partial(_remote, send_sems, recv_sems)
        for w in range(n):
            for q in range(N_CHIPS):
                for layer in range(DEPTH):
                    cp = copy(N_CHIPS * w + q, layers[layer][w].at[q], outs[w].at[q], (x, y, 1 - c))
                    pl.when(c == 1 - layer)(cp.start)
        for w in range(n):
            for q in range(N_CHIPS):
                copy(N_CHIPS * w + q, layers[0][w].at[q], outs[w].at[q], (x, y, 1 - c)).wait()

    return pl.pallas_call(
        body, name="pair_exchange", out_shape=[jax.ShapeDtypeStruct(g.shape, g.dtype) for g in g0],
        in_specs=[ANY] * (2 * n), out_specs=[ANY] * n,
        scratch_shapes=[pltpu.SemaphoreType.DMA((N_CHIPS * n,)), pltpu.SemaphoreType.DMA((N_CHIPS * n,))])(*g0, *g1)


def _pair_sum(name, g0, g1, theirs, cflag):
    shape = theirs.shape
    rows, width = shape[0] * shape[1], shape[2]

    def body(ins, outs, _):
        mine = jnp.where(ins[3][0:1, 0:1] == 0.0, ins[0][...], ins[1][...])
        tot = mine + ins[2][...]
        outs[0][...] = tot
        outs[1][...] = tot.astype(BF16)
    ins = [(a.reshape(rows, width), width, 0) for a in (g0, g1, theirs)] + [(cflag, None, None)]
    f32, bf16 = _ew(name, body, ins, [(width, F32), (width, BF16)], rows, tm=_div_tile(rows, ROW_TILE))
    return f32.reshape(shape), bf16.reshape(shape)


def _chip_exchange(parts):
    n = len(parts)

    def body(*refs):
        ins, outs = refs[:n], refs[n:2 * n]
        send_sems, recv_sems = refs[2 * n:]
        x, y, c, chips = _place()
        copy = functools.partial(_remote, send_sems, recv_sems)
        sends = []
        for j, (cx, cy) in enumerate(chips):
            for w in range(n):
                sends.append(copy(j * n + w, ins[w].at[2 * cx + cy], outs[w].at[j], (cx, cy, c)))
                sends[-1].start()
        for j, (cx, cy) in enumerate(chips):
            for w in range(n):
                copy(j * n + w, outs[w].at[j], outs[w].at[j], (cx, cy, c)).wait_recv()
        for cp in sends:
            cp.wait_send()

    return pl.pallas_call(
        body, name="chip_exchange",
        out_shape=[jax.ShapeDtypeStruct((3,) + a.shape[1:], a.dtype) for a in parts],
        in_specs=[ANY] * n, out_specs=[ANY] * n,
        scratch_shapes=[pltpu.SemaphoreType.DMA((3 * n,)), pltpu.SemaphoreType.DMA((3 * n,))])(*parts)


def _chip_sum(name, part, landed, chipflag):
    _, r, width = part.shape
    tm = _div_tile(r, ROW_TILE // 2)

    def kern(p_ref, l_ref, flag_ref, o_ref):
        me = flag_ref[0:1, 0:1]
        own = jnp.where(me == 0.0, p_ref[0], jnp.where(me == 1.0, p_ref[1], jnp.where(me == 2.0, p_ref[2], p_ref[3])))
        o_ref[...] = ((own + l_ref[0].astype(F32)) + l_ref[1].astype(F32)) + l_ref[2].astype(F32)

    return pl.pallas_call(
        kern, name=name, grid=(r // tm,),
        in_specs=[pl.BlockSpec((N_CHIPS, tm, width), lambda i: (0, i, 0)),
                  pl.BlockSpec((3, tm, width), lambda i: (0, i, 0)),
                  pl.BlockSpec((1, LANES), lambda i: (0, 0))],
        out_specs=pl.BlockSpec((tm, width), lambda i: (i, 0)),
        out_shape=jax.ShapeDtypeStruct((r, width), F32), compiler_params=_params(("arbitrary",)))(part, landed, chipflag)


def _pair_broadcast(mine):
    n = len(mine)
    units = _units([a.shape for a in mine])

    def body(*refs):
        ins, outs = refs[:n], refs[n:2 * n]
        send_sems, recv_sems = refs[2 * n:]
        x, y, c, _ = _place()
        copy = functools.partial(_remote, send_sems, recv_sems)
        cps = [copy(u, ins[w].at[pl.ds(r0, nr), :], outs[w].at[pl.ds(r0, nr), :], (x, y, 1 - c))
               for u, (w, r0, nr) in enumerate(units)]
        for cp in cps:
            cp.start()
        for cp in cps:
            cp.wait()

    return pl.pallas_call(
        body, name="pair_broadcast", out_shape=[jax.ShapeDtypeStruct(a.shape, a.dtype) for a in mine],
        in_specs=[ANY] * n, out_specs=[ANY] * n,
        scratch_shapes=[pltpu.SemaphoreType.DMA((len(units),)), pltpu.SemaphoreType.DMA((len(units),))])(*mine)


def _small_allreduce(v):
    offsets = [(dx, dy, dc) for dx in (0, 1) for dy in (0, 1) for dc in (0, 1)][1:]

    def body(v_ref, out_ref, recv_ref, send_sems, recv_sems):
        x, y, c, _ = _place()
        flip = lambda a, d: 1 - a if d else a
        peers = [(flip(x, dx), flip(y, dy), flip(c, dc)) for dx, dy, dc in offsets]
        copy = functools.partial(_remote, send_sems, recv_sems)
        me = 4 * x + 2 * y + c
        recv_ref[me] = v_ref[...]
        cps = [copy(k, v_ref, recv_ref.at[me], peer) for k, peer in enumerate(peers)]
        for cp in cps:
            cp.start()
        for k, (px, py, pc) in enumerate(peers):
            landed = recv_ref.at[4 * px + 2 * py + pc]
            copy(k, landed, landed, (px, py, pc)).wait_recv()
        for cp in cps:
            cp.wait_send()
        tot = recv_ref[0]
        for d in range(1, 8):
            tot = tot + recv_ref[d]
        out_ref[...] = tot

    vmem = pl.BlockSpec(memory_space=pltpu.VMEM)
    return pl.pallas_call(
        body, name="small_allreduce", out_shape=jax.ShapeDtypeStruct(v.shape, v.dtype),
        in_specs=[vmem], out_specs=vmem,
        scratch_shapes=[pltpu.VMEM((8,) + v.shape, v.dtype), pltpu.SemaphoreType.DMA((7,)),
                        pltpu.SemaphoreType.DMA((7,))])(v)


def _adam_math(gv, wv, mv, vv):
    mv = ADAM_B1 * mv + (1.0 - ADAM_B1) * gv
    vv = ADAM_B2 * vv + (1.0 - ADAM_B2) * (gv * gv)
    m_hat = mv / (1.0 - ADAM_B1 ** ADAM_STEP)
    v_hat = vv / (1.0 - ADAM_B2 ** ADAM_STEP)
    return -ADAM_LR * (m_hat / (jnp.sqrt(v_hat) + ADAM_EPS) + ADAM_WD * wv), mv, vv


def _adamw_big(name, mine, theirs, cflag, w, m, v):
    _, r, width = w.shape
    tm = _div_tile(r, ROW_TILE // 2)

    def kern(mine_ref, theirs_ref, flag_ref, w_ref, m_ref, v_ref, g_ref, d_ref, nm_ref, nv_ref):
        layer = pl.program_id(0).astype(F32)
        gv = jnp.where(flag_ref[0:1, 0:1] == layer, mine_ref[...], theirs_ref[...])
        g_ref[0] = gv
        d_ref[0], nm_ref[0], nv_ref[0] = _adam_math(gv, w_ref[0], m_ref[0], v_ref[0])

    flat = pl.BlockSpec((tm, width), lambda l, i: (i, 0))
    stacked = pl.BlockSpec((1, tm, width), lambda l, i: (l, i, 0))
    return pl.pallas_call(
        kern, name=name, grid=(DEPTH, r // tm),
        in_specs=[flat, flat, pl.BlockSpec((1, LANES), lambda l, i: (0, 0)), stacked, stacked, stacked],
        out_specs=[stacked] * 4, out_shape=[jax.ShapeDtypeStruct(w.shape, F32)] * 4,
        compiler_params=_params(("arbitrary", "arbitrary")))(mine, theirs, cflag, w, m, v)


def _adamw_small(g, w, m, v):
    def body(ins, outs, _):
        outs[0][...], outs[1][...], outs[2][...] = _adam_math(*(r[...] for r in ins))
    return _ew("adamw_small", body, [(a, LANES, 0) for a in (g, w, m, v)], [(LANES, F32)] * 3, SMALL_ROWS)


def kernel(x, p, positions, g_mix, w_in, sink, g_q, w_uq, g_kv, w_ukv, w_br_a, w_br_b, w_out, g_ple, w_ple_gate, w_ple_proj, g_final, loss_target, m_g_mix, m_w_in, m_sink, m_g_q, m_w_uq, m_g_kv, m_w_ukv, m_w_br_a, m_w_br_b, m_w_out, m_g_ple, m_w_ple_gate, m_w_ple_proj, m_g_final, v_g_mix, v_w_in, v_sink, v_g_q, v_w_uq, v_g_kv, v_w_ukv, v_w_br_a, v_w_br_b, v_w_out, v_g_ple, v_w_ple_gate, v_w_ple_proj, v_g_final):
    w = dict(g_mix=g_mix, w_in=w_in, sink=sink, g_q=g_q, w_uq=w_uq, g_kv=g_kv, w_ukv=w_ukv, w_br_a=w_br_a,
             w_br_b=w_br_b, w_out=w_out, g_ple=g_ple, w_ple_gate=w_ple_gate, w_ple_proj=w_ple_proj, g_final=g_final)
    m = dict(g_mix=m_g_mix, w_in=m_w_in, sink=m_sink, g_q=m_g_q, w_uq=m_w_uq, g_kv=m_g_kv, w_ukv=m_w_ukv,
             w_br_a=m_w_br_a, w_br_b=m_w_br_b, w_out=m_w_out, g_ple=m_g_ple, w_ple_gate=m_w_ple_gate,
             w_ple_proj=m_w_ple_proj, g_final=m_g_final)
    v = dict(g_mix=v_g_mix, w_in=v_w_in, sink=v_sink, g_q=v_g_q, w_uq=v_w_uq, g_kv=v_g_kv, w_ukv=v_w_ukv,
             w_br_a=v_w_br_a, w_br_b=v_w_br_b, w_out=v_w_out, g_ple=v_g_ple, w_ple_gate=v_w_ple_gate,
             w_ple_proj=v_w_ple_proj, g_final=v_g_final)
    wfull = _gather_full(w)
    sm = {name: w[name] for name in SMALL}
    loss_row, grad_x, layer_grads, dg_final = _local_step(x, p, positions, wfull, sm, loss_target)
    res, loss = _update(layer_grads, dg_final, loss_row[0, 0], w, m, v)
    return (loss, grad_x, *[res[name][kind] for kind in range(4) for name in WEIGHT_NAMES])


def _gather_behind(shards):
    n = len(shards)
    srcs = [jax.new_ref(s, memory_space=pltpu.MemorySpace.HBM) for s in shards]
    lands = [jax.empty_ref(jax.ShapeDtypeStruct((N_CHIPS,) + s.shape, s.dtype), memory_space=pltpu.MemorySpace.HBM)
             for s in shards]

    @pl.kernel(mesh=plsc.ScalarSubcoreMesh(axis_name="sequencer", num_cores=1), name="gather_behind",
               scratch_types=(pltpu.SemaphoreType.DMA((3 * n,)), pltpu.SemaphoreType.DMA((3 * n,)),
                              pltpu.SemaphoreType.DMA((n,))),
               compiler_params=pltpu.CompilerParams(collective_id=0))
    def launch(send_sems, recv_sems, local_sems):
        x, y, c, chips = _place()
        me = 2 * x + y
        barrier = pltpu.get_barrier_semaphore()
        for cx, cy in chips:
            pl.semaphore_signal(barrier, inc=1, device_id=(cx, cy, c), device_id_type=MESH)
        pl.semaphore_wait(barrier, len(chips))
        copy = functools.partial(_remote, send_sems, recv_sems)
        keeps = [pltpu.make_async_copy(srcs[w], lands[w].at[me], local_sems.at[w]) for w in range(n)]
        cps = [copy(j * n + w, srcs[w], lands[w].at[me], (cx, cy, c))
               for j, (cx, cy) in enumerate(chips) for w in range(n)]
        for cp in keeps + cps:
            cp.start()
        for cp in keeps + cps:
            cp.wait()

    launch()
    return [land[...] for land in lands]


def _gather_full(w):
    shards = [w[name].astype(BF16) for name, _ in SHARDED]
    first, later = _gather_weights([s[0].reshape((2, s.shape[1] // 2) + s.shape[2:]) for s in shards],
                                   [s[1] for s in shards])
    second = _gather_behind(later)
    full = {}
    for k, (name, axis) in enumerate(SHARDED):
        layer0 = first[k].reshape((N_CHIPS,) + shards[k].shape[1:])
        full[name] = [jnp.concatenate(list(blocks), axis=axis - 1) for blocks in (layer0, second[k])]
    return full


def _update(layer_grads, dg_final, loss_local, w, m, v):
    small_shapes = {name: w[name].shape for name in SMALL}
    cflag = jnp.full((1, LANES), lax.axis_index("c"), F32)
    chipflag = jnp.full((1, LANES), 2 * lax.axis_index("x") + lax.axis_index("y"), F32)

    slots = [[_to_slots(layer_grads[layer][name], axis - 1) for name, axis in SHARDED] for layer in range(DEPTH)]
    theirs = _pair_exchange(slots[0], slots[1])
    pair = [_pair_sum("pair_sum_" + name, slots[0][k], slots[1][k], theirs[k], cflag)
            for k, (name, _) in enumerate(SHARDED)]
    landed = _chip_exchange([bf16 for _, bf16 in pair])
    mine = [_chip_sum("chip_sum_" + name, pair[k][0], landed[k], chipflag) for k, (name, _) in enumerate(SHARDED)]
    other = _pair_broadcast(mine)
    res = {}
    for k, (name, _) in enumerate(SHARDED):
        flip = _flipped(w[name].shape)
        view = (lambda a: jnp.swapaxes(a, -1, -2)) if flip else (lambda a: a)
        outs = _adamw_big("adamw_" + name, view(mine[k]), view(other[k]), cflag, view(w[name]), view(m[name]),
                          view(v[name]))
        res[name] = tuple(view(a) for a in outs)

    gsmall = {name: jnp.stack([layer_grads[layer][name] for layer in range(DEPTH)]) for name in SMALL[:-1]}
    gsmall['g_final'] = dg_final
    gsum = _small_allreduce(_pack_small(gsmall, tail=[loss_local]))
    small = (gsum,) + tuple(_adamw_small(gsum, _pack_small(w), _pack_small(m), _pack_small(v)))
    for name, arrs in zip(SMALL, zip(*[[_unpack_small(a, small_shapes)[n] for n in SMALL] for a in small])):
        res[name] = arrs
    return res, gsum.reshape(-1)[SMALL_SIZE]
```

```python
import functools
import math

import jax
import jax.numpy as jnp
from jax import lax
from jax.experimental import pallas as pl
from jax.experimental.pallas import tpu as pltpu
from jax.experimental.pallas import tpu_sc as plsc

F32 = jnp.float32
BF16 = jnp.bfloat16

D_MODEL = 1024
DEPTH = 2
PLE_DIM = 256
BLOCK = 128
EPS = 1e-6
NEG = -1e30
HEADS = 8
SWA_KV_HEADS = 2
HEAD_DIM = 64
LANES = 128
HPAD = HEADS * LANES
MLA_QK = 96
MLA_ROPE = 32
MLA_Q_LORA = 256
MLA_KV_LORA = 128
ROPE_THETA = 10000.0
IN_SIZES = (512, 128, 128, 512, 256, 128, 32, 512, 1024, 1024)

Z_MA, Z_MB, Z_AQ, Z_AGATE, Z_BGATE = 0, 1024, 2048, 3072, 3584
Z_AK, Z_AV, Z_BQD, Z_BKVD, Z_BKR = 4096, 4352, 4608, 4864, 4992
Z_WIDTH = 5120
GATE_W = HEADS * HEAD_DIM
KV_W = SWA_KV_HEADS * LANES

ADAM_LR, ADAM_B1, ADAM_B2, ADAM_EPS, ADAM_WD, ADAM_STEP = 0.001, 0.9, 0.999, 1e-08, 0.01, 10

VMEM_LIMIT = 56 * 1024 * 1024
MESH = pl.DeviceIdType.MESH

WEIGHT_NAMES = ('g_mix', 'w_in', 'sink', 'g_q', 'w_uq', 'g_kv', 'w_ukv', 'w_br_a', 'w_br_b',
                'w_out', 'g_ple', 'w_ple_gate', 'w_ple_proj', 'g_final')
SHARDED = (('w_in', 2), ('w_uq', 2), ('w_ukv', 2), ('w_br_a', 2), ('w_br_b', 2),
           ('w_out', 1), ('w_ple_gate', 1), ('w_ple_proj', 2))
SMALL = ('g_mix', 'sink', 'g_q', 'g_kv', 'g_ple', 'g_final')
N_CHIPS = 4


def _params(sem):
    return pltpu.CompilerParams(dimension_semantics=sem, vmem_limit_bytes=VMEM_LIMIT)


MM_TN = 512
ROW_TILE = 512
BIG_WEIGHT_BYTES = 16 * 1024 * 1024


def _row_tile(rows, weight_bytes=0):
    tm = ROW_TILE // 2 if weight_bytes > BIG_WEIGHT_BYTES else ROW_TILE
    return min(tm, rows)


def _ew(name, body, ins, outs, rows, accs=(), mms=(), tm=None):
    n_mm, n_in, n_out = len(mms), len(ins), len(outs)
    if tm is None:
        tm = _row_tile(rows, sum(b.size * b.dtype.itemsize for _, b in mms))
    in_specs, args = [], []
    for a, b in mms:
        in_specs += [pl.BlockSpec((tm, a.shape[1]), lambda i: (i, 0)), pl.BlockSpec(b.shape, lambda i: (0, 0))]
        args += [a, b]
    for arr, width, cb in ins:
        if width is None:
            in_specs.append(pl.BlockSpec(arr.shape, lambda i, nd=arr.ndim: (0,) * nd))
        else:
            in_specs.append(pl.BlockSpec((tm, width), lambda i, cb=cb: (i, cb)))
        args.append(arr)
    out_shape, out_specs, aliases = [], [], {}
    for k, out in enumerate(outs):
        if len(out) == 4:
            aliases[len(args)] = k
            in_specs.append(pl.BlockSpec(memory_space=pl.ANY))
            args.append(out[2])
            out_shape.append(jax.ShapeDtypeStruct(out[2].shape, out[2].dtype))
            out_specs.append(pl.BlockSpec((tm, out[0]), lambda i, cb=out[3]: (i, cb)))
        else:
            out_shape.append(jax.ShapeDtypeStruct((rows, out[0]), out[1]))
            out_specs.append(pl.BlockSpec((tm, out[0]), lambda i: (i, 0)))
    n_in += len(aliases)
    out_shape += [jax.ShapeDtypeStruct(s, F32) for s in accs]
    out_specs += [pl.BlockSpec(s, lambda i: (0, 0)) for s in accs]

    def kern(*refs):
        mm_refs, refs = refs[:2 * n_mm], refs[2 * n_mm:]
        in_refs, out_refs = refs[:n_in - len(aliases)], refs[n_in:n_in + n_out]
        acc_refs, prod_refs = refs[n_in + n_out:n_in + n_out + len(accs)], refs[n_in + n_out + len(accs):]
        if acc_refs:
            @pl.when(pl.program_id(0) == 0)
            def _():
                for r in acc_refs:
                    r[...] = jnp.zeros_like(r)
        for k in range(n_mm):
            a_ref, b_ref, prod = mm_refs[2 * k], mm_refs[2 * k + 1], prod_refs[k]
            av = a_ref[...].astype(BF16)
            n = b_ref.shape[1]
            tn = min(MM_TN, n)
            for j in range(n // tn):
                cols = slice(j * tn, (j + 1) * tn)
                prod[:, cols] = jnp.dot(av, b_ref[:, cols], preferred_element_type=F32)
        body(tuple(prod_refs) + tuple(in_refs), out_refs, acc_refs)

    scratch = [pltpu.VMEM((tm, b.shape[1]), F32) for _, b in mms]
    res = pl.pallas_call(kern, name=name, grid=(rows // tm,), in_specs=in_specs, out_specs=out_specs,
                         out_shape=out_shape, scratch_shapes=scratch, input_output_aliases=aliases,
                         compiler_params=_params(("arbitrary",)))(*args)
    return res


def _rms(xv, gv):
    r = lax.rsqrt(jnp.mean(xv * xv, axis=-1, keepdims=True) + EPS)
    return ((xv * r) * gv).astype(BF16)


def _rms_fwd(name, x, width, cb, g, rows):
    def body(ins, outs, _):
        outs[0][...] = _rms(ins[0][...].astype(F32), ins[1][...])
    return _ew(name, body, [(x, width, cb), (g.reshape(1, width), None, None)], [(width, BF16)], rows)[0]


def _rms_bwd(name, x, width, cb, g, dh_mm, rows, out_dtype, dres=None, into=()):
    def body(ins, outs, accs):
        dhv, xv, gv = ins[0][...], ins[1][...].astype(F32), ins[2][...]
        r = lax.rsqrt(jnp.mean(xv * xv, axis=-1, keepdims=True) + EPS)
        xhat = xv * r
        accs[0][...] += jnp.sum(dhv * xhat, axis=0, keepdims=True)
        dy = dhv * gv
        dx = r * (dy - xhat * jnp.mean(dy * xhat, axis=-1, keepdims=True))
        if dres is not None:
            dx = dx + ins[3][...]
        outs[0][...] = dx.astype(out_dtype)
    ins = [(x, width, cb), (g.reshape(1, width), None, None)]
    if dres is not None:
        ins.append((dres, width, 0))
    return _ew(name, body, ins, [(width, out_dtype) + tuple(into)], rows, accs=[(1, width)], mms=[dh_mm])


def _mm(name, a, b, out_dtype, f32_cols=None, norms=(), tn=MM_TN):
    M, K = a.shape
    N = b.shape[1]
    tm, tn = _row_tile(M, b.size * b.dtype.itemsize), min(tn, N)
    c0, cw = f32_cols if f32_cols else (0, 0)
    n_norm, n_f32 = len(norms), 1 if f32_cols else 0
    assert c0 % tn == 0 and cw % tn == 0
    assert all(nc // tn == (nc + g.shape[-1] - 1) // tn for nc, g in norms)

    def kern(*refs):
        a_ref, b_ref, g_refs = refs[0], refs[1], refs[2:2 + n_norm]
        o_ref, extra = refs[2 + n_norm], refs[3 + n_norm:]
        av = a_ref[...].astype(BF16)
        for j in range(N // tn):
            cols = slice(j * tn, (j + 1) * tn)
            part = jnp.dot(av, b_ref[:, cols], preferred_element_type=F32)
            o_ref[:, cols] = part.astype(o_ref.dtype)
            if f32_cols and c0 <= j * tn and (j + 1) * tn <= c0 + cw:
                extra[0][:, j * tn - c0:(j + 1) * tn - c0] = part
            for k, (nc, g) in enumerate(norms):
                if nc // tn == j:
                    seg = part[:, nc - j * tn:nc - j * tn + g.shape[-1]]
                    extra[n_f32 + k][...] = _rms(seg, g_refs[k][...])

    in_specs = [pl.BlockSpec((tm, K), lambda i: (i, 0)), pl.BlockSpec((K, N), lambda i: (0, 0))]
    in_specs += [pl.BlockSpec((1, g.shape[-1]), lambda i: (0, 0)) for _, g in norms]
    widths = [(N, out_dtype)] + ([(cw, F32)] if f32_cols else []) + [(g.shape[-1], BF16) for _, g in norms]
    return pl.pallas_call(
        kern, name=name, grid=(M // tm,), in_specs=in_specs,
        out_specs=[pl.BlockSpec((tm, w), lambda i: (i, 0)) for w, _ in widths],
        out_shape=[jax.ShapeDtypeStruct((M, w), dt) for w, dt in widths],
        compiler_params=_params(("parallel",)))(a, b, *[g.reshape(1, -1) for _, g in norms])


def _mm_tn(name, a, b, tk=2048, tn=2048):
    T, M = a.shape
    N = b.shape[1]
    tn, tk = min(tn, N), min(tk, T)

    def kern(a_ref, b_ref, o_ref):
        k = pl.program_id(1)
        part = _dot_tn(a_ref[...].astype(BF16), b_ref[...].astype(BF16))

        @pl.when(k == 0)
        def _():
            o_ref[...] = part

        @pl.when(k > 0)
        def _():
            o_ref[...] += part

    return pl.pallas_call(
        kern, name=name, grid=(N // tn, T // tk),
        in_specs=[pl.BlockSpec((tk, M), lambda j, k: (k, 0)), pl.BlockSpec((tk, tn), lambda j, k: (k, j))],
        out_specs=pl.BlockSpec((M, tn), lambda j, k: (0, j)),
        out_shape=jax.ShapeDtypeStruct((M, N), F32),
        compiler_params=_params(("parallel", "arbitrary")))(a, b)


def _dot_nt(a, b):
    return lax.dot_general(a, b, (((1,), (1,)), ((), ())), preferred_element_type=F32)


def _dot_tn(a, b):
    return lax.dot_general(a, b, (((0,), (0,)), ((), ())), preferred_element_type=F32)


SWA_SCALE = HEAD_DIM ** -0.5


def _swa_band(n, pq_ref, pkp_ref, pkc_ref):
    posk = jnp.concatenate([pkp_ref[...], pkc_ref[...]], axis=0)
    dist = (pq_ref[0] - posk).astype(F32)
    kj = lax.broadcasted_iota(jnp.int32, (2 * BLOCK, BLOCK), 0)
    qi = lax.broadcasted_iota(jnp.int32, (2 * BLOCK, BLOCK), 1)
    t_abs = n * BLOCK + qi
    s_abs = n * BLOCK - BLOCK + kj
    return dist, (s_abs >= 0) & (s_abs <= t_abs) & (t_abs - s_abs < BLOCK)


SWA_GROUP = HEADS // SWA_KV_HEADS


def _head_gate(gate_ref, h):
    pair = gate_ref[:, (h // 2) * LANES:(h // 2 + 1) * LANES].astype(F32)
    return pair if h % 2 == 0 else pltpu.roll(pair, HEAD_DIM, 1)


def _swa_group_q(q_all, g):
    heads = range(g * SWA_GROUP, (g + 1) * SWA_GROUP)
    return jnp.concatenate([(q_all[:, h * LANES:(h + 1) * LANES] * SWA_SCALE).astype(BF16) for h in heads], axis=0)


def _swa_mask(s, dist, valid, h):
    return jnp.where(valid, s - (2.0 ** -(h + 1)) * dist, NEG)


def _rows_to_lanes(rows):
    block = jnp.concatenate(list(rows) + [jnp.zeros((LANES - len(rows), BLOCK), F32)], axis=0)
    return block.T


def _swa_specs(nb):
    prev = lambda b, n: b * nb + jnp.maximum(n - 1, 0)
    own = lambda b, n: b * nb + n
    return [
        pl.BlockSpec((BLOCK, HPAD), lambda b, n: (own(b, n), Z_AQ // HPAD)),
        pl.BlockSpec((BLOCK, KV_W), lambda b, n: (prev(b, n), Z_AK // KV_W)),
        pl.BlockSpec((BLOCK, KV_W), lambda b, n: (own(b, n), Z_AK // KV_W)),
        pl.BlockSpec((BLOCK, KV_W), lambda b, n: (prev(b, n), Z_AV // KV_W)),
        pl.BlockSpec((BLOCK, KV_W), lambda b, n: (own(b, n), Z_AV // KV_W)),
        pl.BlockSpec((1, 1, BLOCK), lambda b, n: (own(b, n), 0, 0)),
        pl.BlockSpec((BLOCK, 1), lambda b, n: (prev(b, n), 0)),
        pl.BlockSpec((BLOCK, 1), lambda b, n: (own(b, n), 0)),
    ]


def _swa_fwd(z, gate, pos_col, pos_row, sink_row, B, S):
    nb = S // BLOCK
    T = B * S

    def kern(q_ref, kp_ref, kc_ref, vp_ref, vc_ref, pq_ref, pkp_ref, pkc_ref, gate_ref, sink_ref,
             oraw_ref, og_ref, lse_ref):
        q_all = q_ref[...]
        kb = jnp.concatenate([kp_ref[...], kc_ref[...]], axis=0).astype(BF16)
        vb = jnp.concatenate([vp_ref[...], vc_ref[...]], axis=0).astype(BF16)
        dist, valid = _swa_band(pl.program_id(1), pq_ref, pkp_ref, pkc_ref)
        lse_rows = []
        for grp in range(SWA_KV_HEADS):
            gcols = slice(grp * LANES, (grp + 1) * LANES)
            s_all = _dot_nt(kb[:, gcols], _swa_group_q(q_all, grp))
            probs = []
            for hh in range(SWA_GROUP):
                h = grp * SWA_GROUP + hh
                s = _swa_mask(s_all[:, hh * BLOCK:(hh + 1) * BLOCK], dist, valid, h)
                sink_h = sink_ref[0:1, h:h + 1]
                m = jnp.maximum(jnp.max(s, axis=0, keepdims=True), sink_h)
                e = jnp.exp(s - m)
                denom = jnp.sum(e, axis=0, keepdims=True) + jnp.exp(sink_h - m)
                probs.append((e * (1.0 / denom)).astype(BF16))
                lse_rows.append(m + jnp.log(denom))
            o_all = jnp.dot(vb[:, gcols].T, jnp.concatenate(probs, axis=1), preferred_element_type=F32)
            for hh in range(SWA_GROUP):
                h = grp * SWA_GROUP + hh
                cols = slice(h * LANES, (h + 1) * LANES)
                o = o_all[:, hh * BLOCK:(hh + 1) * BLOCK].T
                oraw_ref[:, cols] = o
                g = _head_gate(gate_ref, h)
                og_ref[:, cols] = (o * (g * jax.nn.sigmoid(g))).astype(BF16)
        lse_ref[...] = _rows_to_lanes(lse_rows)

    own = lambda b, n: b * nb + n
    in_specs = _swa_specs(nb) + [
        pl.BlockSpec((BLOCK, GATE_W), lambda b, n: (own(b, n), 0)),
        pl.BlockSpec((1, LANES), lambda b, n: (0, 0)),
    ]
    out_specs = [pl.BlockSpec((BLOCK, HPAD), lambda b, n: (own(b, n), 0)),
                 pl.BlockSpec((BLOCK, HPAD), lambda b, n: (own(b, n), 0)),
                 pl.BlockSpec((BLOCK, LANES), lambda b, n: (own(b, n), 0))]
    out_shape = [jax.ShapeDtypeStruct((T, HPAD), F32), jax.ShapeDtypeStruct((T, HPAD), BF16),
                 jax.ShapeDtypeStruct((T, LANES), F32)]
    return pl.pallas_call(kern, name="swa_fwd", grid=(B, nb), in_specs=in_specs, out_specs=out_specs,
                          out_shape=out_shape, compiler_params=_params(("parallel", "arbitrary")))(
        z, z, z, z, z, pos_row, pos_col, pos_col, gate, sink_row)


def _swa_bwd(z, pos_col, pos_row, sink_row, lse, do_raw, delta, dz, B, S):
    nb = S // BLOCK
    T = B * S

    def kern(q_ref, kp_ref, kc_ref, vp_ref, vc_ref, pq_ref, pkp_ref, pkc_ref, sink_ref, lse_ref, do_ref,
             delta_ref, dz_ref, dq_ref, dk_ref, dv_ref, dsink_ref):
        b, n = pl.program_id(0), pl.program_id(1)

        @pl.when(n == 0)
        def _():
            dk_ref[...] = jnp.zeros_like(dk_ref)
            dv_ref[...] = jnp.zeros_like(dv_ref)

        @pl.when((b == 0) & (n == 0))
        def _():
            dsink_ref[...] = jnp.zeros_like(dsink_ref)

        q_all = q_ref[...]
        kb = jnp.concatenate([kp_ref[...], kc_ref[...]], axis=0).astype(BF16)
        vb = jnp.concatenate([vp_ref[...], vc_ref[...]], axis=0).astype(BF16)
        dist, valid = _swa_band(n, pq_ref, pkp_ref, pkc_ref)
        lse_t, delta_t = lse_ref[...].T, delta_ref[...].T
        lane1 = lax.broadcasted_iota(jnp.int32, (1, LANES), 1)
        dsink = jnp.zeros((1, LANES), F32)
        dk_band, dv_band = [], []
        for grp in range(SWA_KV_HEADS):
            gcols = slice(grp * LANES, (grp + 1) * LANES)
            heads = range(grp * SWA_GROUP, (grp + 1) * SWA_GROUP)
            qg = _swa_group_q(q_all, grp)
            dog = jnp.concatenate([do_ref[:, h * LANES:(h + 1) * LANES] for h in heads], axis=0)
            s_all = _dot_nt(kb[:, gcols], qg)
            dp_all = _dot_nt(vb[:, gcols], dog)
            ps, dss = [], []
            for hh, h in enumerate(heads):
                blk = slice(hh * BLOCK, (hh + 1) * BLOCK)
                lse_h, delta_h = lse_t[h:h + 1, :], delta_t[h:h + 1, :]
                p = jnp.exp(_swa_mask(s_all[:, blk], dist, valid, h) - lse_h)
                ps.append(p.astype(BF16))
                dss.append((p * (dp_all[:, blk] - delta_h)).astype(BF16))
                psink = jnp.exp(sink_ref[0:1, h:h + 1] - lse_h)
                dsink = dsink + jnp.where(lane1 == h, -jnp.sum(psink * delta_h, axis=1, keepdims=True), 0.0)
            dsg = jnp.concatenate(dss, axis=1)
            dq_all = jnp.dot(kb[:, gcols].T, dsg, preferred_element_type=F32) * SWA_SCALE
            for hh, h in enumerate(heads):
                dq_ref[:, h * LANES:(h + 1) * LANES] = dq_all[:, hh * BLOCK:(hh + 1) * BLOCK].T.astype(BF16)
            dk_band.append(jnp.dot(dsg, qg, preferred_element_type=F32))
            dv_band.append(jnp.dot(jnp.concatenate(ps, axis=1), dog, preferred_element_type=F32))
        dsink_ref[...] += dsink
        dkb = jnp.concatenate(dk_band, axis=1)
        dvb = jnp.concatenate(dv_band, axis=1)
        r_prev = pl.ds(pl.multiple_of(jnp.maximum(n - 1, 0) * BLOCK, BLOCK), BLOCK)
        r_own = pl.ds(pl.multiple_of(n * BLOCK, BLOCK), BLOCK)
        dk_ref[r_prev, :] += dkb[:BLOCK]
        dk_ref[r_own, :] += dkb[BLOCK:]
        dv_ref[r_prev, :] += dvb[:BLOCK]
        dv_ref[r_own, :] += dvb[BLOCK:]

    own = lambda b, n: b * nb + n
    in_specs = _swa_specs(nb) + [
        pl.BlockSpec((1, LANES), lambda b, n: (0, 0)),
        pl.BlockSpec((BLOCK, LANES), lambda b, n: (own(b, n), 0)),
        pl.BlockSpec((BLOCK, HPAD), lambda b, n: (own(b, n), 0)),
        pl.BlockSpec((BLOCK, LANES), lambda b, n: (own(b, n), 0)),
        pl.BlockSpec(memory_space=pl.ANY),
    ]
    out_specs = [pl.BlockSpec((BLOCK, HPAD), lambda b, n: (own(b, n), Z_AQ // HPAD)),
                 pl.BlockSpec((S, KV_W), lambda b, n: (b, 0)),
                 pl.BlockSpec((S, KV_W), lambda b, n: (b, 0)),
                 pl.BlockSpec((1, LANES), lambda b, n: (0, 0))]
    out_shape = [jax.ShapeDtypeStruct(dz.shape, dz.dtype), jax.ShapeDtypeStruct((T, KV_W), F32),
                 jax.ShapeDtypeStruct((T, KV_W), F32), jax.ShapeDtypeStruct((1, LANES), F32)]
    return pl.pallas_call(kern, name="swa_bwd", grid=(B, nb), in_specs=in_specs, out_specs=out_specs,
                          out_shape=out_shape, input_output_aliases={len(in_specs) - 1: 0},
                          compiler_params=_params(("arbitrary", "arbitrary")))(
        z, z, z, z, z, pos_row, pos_col, pos_col, sink_row, lse, do_raw, delta, dz)


MLA_T = 256
MLA_HG = 4
MLA_W = MLA_HG * LANES
MLA_HGB = 8
MLA_WB = MLA_HGB * LANES
MLA_SCALE = MLA_QK ** -0.5
LOG2E = 1.4426950408889634
MLA_QSCALE = MLA_SCALE * LOG2E


def _causal_t(s):
    key = lax.broadcasted_iota(jnp.int32, s.shape, 0)
    query = lax.broadcasted_iota(jnp.int32, s.shape, 1)
    return jnp.where(key <= query, s, NEG)


def _mla_fwd(q, k, v, z, B, S):
    T = B * S
    nq = S // MLA_T

    def kern(q_ref, k_ref, v_ref, gate_ref, oraw_ref, og_ref, lse_ref):
        i = pl.program_id(2)

        def scores(j):
            rows = pl.ds(pl.multiple_of(j * MLA_T, MLA_T), MLA_T)
            return tuple(_dot_nt(k_ref[rows, hh * LANES:(hh + 1) * LANES], q_ref[:, hh * LANES:(hh + 1) * LANES])
                         for hh in range(MLA_HG))

        def update(j, ss, state):
            rows = pl.ds(pl.multiple_of(j * MLA_T, MLA_T), MLA_T)
            out = []
            for hh in range(MLA_HG):
                (m, l, acc), s = state[hh], ss[hh]
                m_new = jnp.maximum(m, jnp.max(s, axis=0, keepdims=True))
                alpha = jnp.exp2(m - m_new)
                p = jnp.exp2(s - m_new)
                l = alpha * l + jnp.sum(p, axis=0, keepdims=True)
                pv = jnp.dot(v_ref[rows, hh * LANES:(hh + 1) * LANES].T, p.astype(BF16), preferred_element_type=F32)
                out.append((m_new, l, alpha * acc + pv))
            return tuple(out)

        def body(pair, state):
            j = 2 * pair
            s0, s1 = scores(j), scores(j + 1)
            return update(j + 1, s1, update(j, s0, state))

        init = tuple((jnp.full((1, MLA_T), NEG, F32), jnp.zeros((1, MLA_T), F32), jnp.zeros((LANES, MLA_T), F32))
                     for _ in range(MLA_HG))
        state = lax.fori_loop(0, i // 2, body, init)
        state = lax.cond(i % 2 == 1, lambda st: update(i - 1, scores(i - 1), st), lambda st: st, state)
        state = update(i, tuple(_causal_t(s) for s in scores(i)), state)
        for hh in range(MLA_HG):
            m, l, acc = state[hh]
            cols = slice(hh * LANES, (hh + 1) * LANES)
            o = (acc * (1.0 / l)).T
            oraw_ref[:, cols] = o.astype(BF16)
            g = _head_gate(gate_ref, hh)
            og_ref[:, cols] = (o * (g * jax.nn.sigmoid(g))).astype(BF16)
            lse_ref[0, 0, 0, hh:hh + 1, :] = m + jnp.log2(l)

    blk = lambda b, h, i: (b * nq + i, h)
    in_specs = [pl.BlockSpec((MLA_T, MLA_W), blk),
                pl.BlockSpec((S, MLA_W), lambda b, h, i: (b, h)),
                pl.BlockSpec((S, MLA_W), lambda b, h, i: (b, h)),
                pl.BlockSpec((MLA_T, MLA_W // 2), lambda b, h, i: (b * nq + i, Z_BGATE // (MLA_W // 2) + h))]
    out_specs = [pl.BlockSpec((MLA_T, MLA_W), blk), pl.BlockSpec((MLA_T, MLA_W), blk),
                 pl.BlockSpec((1, 1, 1, MLA_HG, MLA_T), lambda b, h, i: (b, h, i, 0, 0))]
    out_shape = [jax.ShapeDtypeStruct((T, HPAD), BF16), jax.ShapeDtypeStruct((T, HPAD), BF16),
                 jax.ShapeDtypeStruct((B, HEADS // MLA_HG, nq, MLA_HG, MLA_T), F32)]
    return pl.pallas_call(kern, name="mla_fwd", grid=(B, HEADS // MLA_HG, nq), in_specs=in_specs,
                          out_specs=out_specs, out_shape=out_shape,
                          compiler_params=_params(("parallel", "parallel", "arbitrary")))(q, k, v, z)


def _mla_bwd(q, k, v, do_raw, lse, delta, B, S):
    T = B * S
    nk = S // MLA_T

    def kern(q_ref, k_ref, v_ref, do_ref, lse_ref, delta_ref, dq_ref, dk_ref, dv_ref, dq_acc, dk_acc, dv_acc):
        j = pl.program_id(2)

        @pl.when(j == 0)
        def _():
            dq_acc[...] = jnp.zeros_like(dq_acc)

        dk_acc[...] = jnp.zeros_like(dk_acc)
        dv_acc[...] = jnp.zeros_like(dv_acc)
        kts = [k_ref[:, hh * LANES:(hh + 1) * LANES].T for hh in range(MLA_HGB)]

        def step(i, masked):
            rows = pl.ds(pl.multiple_of(i * MLA_T, MLA_T), MLA_T)
            for hh in range(MLA_HGB):
                cols = slice(hh * LANES, (hh + 1) * LANES)
                qv, do = q_ref[rows, cols], do_ref[rows, cols]
                st = _dot_nt(k_ref[:, cols], qv)
                if masked:
                    st = _causal_t(st)
                pt = jnp.exp2(st - lse_ref[0, 0, i, hh:hh + 1, :])
                dpt = _dot_nt(v_ref[:, cols], do)
                dst = (pt * (dpt - delta_ref[0, 0, i, hh:hh + 1, :])).astype(BF16)
                dv_acc[:, cols] += jnp.dot(pt.astype(BF16), do, preferred_element_type=F32)
                dk_acc[:, cols] += jnp.dot(dst, qv, preferred_element_type=F32)
                dq_acc[hh, i] += jnp.dot(kts[hh], dst, preferred_element_type=F32)

        step(j, True)

        def body(i, c):
            step(i, False)
            return c

        lax.fori_loop(j + 1, nk, body, 0)
        dk_ref[...] = (dk_acc[...] * (1.0 / LOG2E)).astype(BF16)
        dv_ref[...] = dv_acc[...].astype(BF16)

        @pl.when(j == nk - 1)
        def _():
            for hh in range(MLA_HGB):
                for t in range(nk):
                    dq_ref[t * MLA_T:(t + 1) * MLA_T, hh * LANES:(hh + 1) * LANES] = dq_acc[hh, t].T.astype(BF16)

    whole = lambda b, h, j: (b, h)
    tile = lambda b, h, j: (b * nk + j, h)
    stats = pl.BlockSpec((1, 1, nk, MLA_HGB, MLA_T), lambda b, h, j: (b, h, 0, 0, 0))
    in_specs = [pl.BlockSpec((S, MLA_WB), whole), pl.BlockSpec((MLA_T, MLA_WB), tile),
                pl.BlockSpec((MLA_T, MLA_WB), tile), pl.BlockSpec((S, MLA_WB), whole), stats, stats]
    out_specs = [pl.BlockSpec((S, MLA_WB), whole), pl.BlockSpec((MLA_T, MLA_WB), tile),
                 pl.BlockSpec((MLA_T, MLA_WB), tile)]
    out_shape = [jax.ShapeDtypeStruct((T, HPAD), BF16)] * 3
    scratch = [pltpu.VMEM((MLA_HGB, nk, LANES, MLA_T), F32), pltpu.VMEM((MLA_T, MLA_WB), F32),
               pltpu.VMEM((MLA_T, MLA_WB), F32)]
    return pl.pallas_call(kern, name="mla_bwd", grid=(B, HEADS // MLA_HGB, nk), in_specs=in_specs,
                          out_specs=out_specs, out_shape=out_shape, scratch_shapes=scratch,
                          compiler_params=_params(("parallel", "parallel", "arbitrary")))(
        q, k, v, do_raw, lse, delta)


def _rope_tables(pos_col, inv_lane, rows):
    def body(ins, outs, _):
        ang = ins[0][...].astype(F32) * ins[1][...]
        lane = lax.broadcasted_iota(jnp.int32, ang.shape, 1)
        cos, sin = jnp.cos(ang), jnp.sin(ang)
        first = (lane >= HEAD_DIM) & (lane < HEAD_DIM + MLA_ROPE // 2)
        second = (lane >= HEAD_DIM + MLA_ROPE // 2) & (lane < MLA_QK)
        outs[0][...] = jnp.where(lane < HEAD_DIM, 1.0, jnp.where(lane < MLA_QK, cos, 0.0))
        outs[1][...] = jnp.where(first, -sin, 0.0)
        outs[2][...] = jnp.where(second, sin, 0.0)
    return _ew("rope_tables", body, [(pos_col, 1, 0), (inv_lane, None, None)], [(LANES, F32)] * 3, rows)


def _rope(x, c, s1, s2):
    return x * c + pltpu.roll(x, 112, 1) * s1 + pltpu.roll(x, 16, 1) * s2


def _rope_t(d, c, s1, s2):
    return d * c + pltpu.roll(d * s1, 16, 1) + pltpu.roll(d * s2, 112, 1)


def _mla_prep(qdn, w_uq, kvdn, w_ukv, z, tabs, rows):
    def body(ins, outs, _):
        q_pre, kv_pre = ins[0], ins[1]
        c, s1, s2 = ins[3][...], ins[4][...], ins[5][...]
        kr = _rope(ins[2][...].astype(F32), c, s1, s2)
        for h in range(HEADS):
            cols = slice(h * LANES, (h + 1) * LANES)
            outs[0][:, cols] = (_rope(q_pre[:, cols], c, s1, s2) * MLA_QSCALE).astype(BF16)
            outs[1][:, cols] = (kv_pre[:, cols] + kr).astype(BF16)
        outs[2][...] = kv_pre[:, HPAD:].astype(BF16)
    ins = [(z, LANES, Z_BKR // LANES), (tabs[0], LANES, 0), (tabs[1], LANES, 0), (tabs[2], LANES, 0)]
    return _ew("mla_prep", body, ins, [(HPAD, BF16)] * 3, rows, mms=[(qdn, w_uq), (kvdn, w_ukv)])


def _mla_prep_bwd(dq, dk, dv, tabs, dz, rows):
    def body(ins, outs, _):
        c, s1, s2 = ins[3][...], ins[4][...], ins[5][...]
        lane = lax.broadcasted_iota(jnp.int32, c.shape, 1)
        dkr = jnp.zeros(c.shape, F32)
        for h in range(HEADS):
            cols = slice(h * LANES, (h + 1) * LANES)
            outs[0][:, cols] = _rope_t(ins[0][:, cols].astype(F32) * MLA_SCALE, c, s1, s2).astype(BF16)
            dkh = ins[1][:, cols].astype(F32)
            outs[1][:, cols] = jnp.where(lane < HEAD_DIM, dkh, 0.0).astype(BF16)
            dkr = dkr + dkh
        outs[1][:, HPAD:] = ins[2][...].astype(BF16)
        live = (lane >= HEAD_DIM) & (lane < MLA_QK)
        outs[2][...] = jnp.where(live, _rope_t(jnp.where(live, dkr, 0.0), c, s1, s2), 0.0).astype(BF16)
    ins = [(dq, HPAD, 0), (dk, HPAD, 0), (dv, HPAD, 0), (tabs[0], LANES, 0), (tabs[1], LANES, 0),
           (tabs[2], LANES, 0)]
    outs = [(HPAD, BF16), (2 * HPAD, BF16), (LANES, BF16, dz, Z_BKR // LANES)]
    return _ew("mla_prep_bwd", body, ins, outs, rows)


def _gate_bwd(name, d_o_mm, o_raw, gate, gate_cb, dz, dz_cb, rows):
    def body(ins, outs, _):
        lane = lax.broadcasted_iota(jnp.int32, outs[2].shape, 1)
        delta = jnp.zeros(outs[2].shape, F32)
        d_gate = [None] * HEADS
        for h in range(HEADS):
            cols = slice(h * LANES, (h + 1) * LANES)
            dog, o, g = ins[0][:, cols], ins[1][:, cols].astype(F32), _head_gate(ins[2], h)
            sg = jax.nn.sigmoid(g)
            do = dog * (g * sg)
            outs[0][:, cols] = do.astype(BF16)
            d_gate[h] = dog * o * (sg * (1.0 + g * (1.0 - sg)))
            delta = jnp.where(lane == h, jnp.sum(do * o, axis=-1, keepdims=True), delta)
        for pair in range(HEADS // 2):
            packed = d_gate[2 * pair] + pltpu.roll(d_gate[2 * pair + 1], HEAD_DIM, 1)
            outs[1][:, pair * LANES:(pair + 1) * LANES] = packed.astype(BF16)
        outs[2][...] = delta
    ins = [(o_raw, HPAD, 0), (gate, GATE_W, gate_cb)]
    outs = [(HPAD, BF16), (GATE_W, BF16, dz, dz_cb), (LANES, F32)]
    return _ew(name, body, ins, outs, rows, mms=[d_o_mm])


def _merge_out(ua, ub, z, w_out, x0, g_next, rows):
    tm = _row_tile(rows)

    def kern(ua_ref, ub_ref, ma_ref, mb_ref, w_ref, x0_ref, g_ref, y_ref, x1_ref, hn_ref):
        ua_v, ub_v, m_a, m_b = (r[...].astype(F32) for r in (ua_ref, ub_ref, ma_ref, mb_ref))
        y = (jax.nn.sigmoid(m_a) * ua_v + jax.nn.sigmoid(m_b) * ub_v).astype(BF16)
        y_ref[...] = y
        for j in range(D_MODEL // MM_TN):
            cols = slice(j * MM_TN, (j + 1) * MM_TN)
            x1_ref[:, cols] = jnp.dot(y, w_ref[:, cols], preferred_element_type=F32) + x0_ref[:, cols]
        hn_ref[...] = _rms(x1_ref[...], g_ref[...])

    row = lambda cb: pl.BlockSpec((tm, D_MODEL), lambda i: (i, cb))
    return pl.pallas_call(
        kern, name="merge_out", grid=(rows // tm,),
        in_specs=[row(0), row(0), row(Z_MA // D_MODEL), row(Z_MB // D_MODEL),
                  pl.BlockSpec(w_out.shape, lambda i: (0, 0)), row(0), pl.BlockSpec((1, D_MODEL), lambda i: (0, 0))],
        out_specs=[row(0), row(0), row(0)],
        out_shape=[jax.ShapeDtypeStruct((rows, D_MODEL), BF16), jax.ShapeDtypeStruct((rows, D_MODEL), F32),
                   jax.ShapeDtypeStruct((rows, D_MODEL), BF16)],
        compiler_params=_params(("parallel",)))(ua, ub, z, z, w_out, x0, g_next.reshape(1, D_MODEL))


def _merge_bwd(dy_mm, ua, ub, z, dz, rows):
    def body(ins, outs, _):
        dyv = ins[0][...]
        for idx in range(2):
            s = jax.nn.sigmoid(ins[3 + idx][...].astype(F32))
            outs[idx][...] = (dyv * s).astype(BF16)
            d_m = (dyv * ins[1 + idx][...].astype(F32) * (s * (1.0 - s))).astype(BF16)
            outs[2][:, idx * D_MODEL:(idx + 1) * D_MODEL] = d_m
    ins = [(ua, D_MODEL, 0), (ub, D_MODEL, 0), (z, D_MODEL, Z_MA // D_MODEL), (z, D_MODEL, Z_MB // D_MODEL)]
    outs = [(D_MODEL, BF16), (D_MODEL, BF16), (2 * D_MODEL, BF16, dz, Z_MA // (2 * D_MODEL))]
    return _ew("merge_bwd", body, ins, outs, rows, mms=[dy_mm])


def _kv_grad_cast(dk, dv, dz, rows):
    def body(ins, outs, _):
        outs[0][:, :KV_W] = ins[0][...].astype(BF16)
        outs[0][:, KV_W:] = ins[1][...].astype(BF16)
    outs = [(2 * KV_W, BF16, dz, Z_AK // (2 * KV_W))]
    return _ew("kv_grad_cast", body, [(dk, KV_W, 0), (dv, KV_W, 0)], outs, rows)[0]


def _ple_fwd(x1, hn, w_pg, p, w_pp, g_next, rows):
    def body(ins, outs, _):
        u, e = ins[0][...], ins[1][...]
        x2 = ins[2][...] + jax.nn.sigmoid(u) * e
        outs[0][...] = x2
        outs[1][...] = u.astype(BF16)
        outs[2][...] = e.astype(BF16)
        if g_next is not None:
            outs[3][...] = _rms(x2, ins[3][...])
    ins = [(x1, D_MODEL, 0)] + ([(g_next.reshape(1, D_MODEL), None, None)] if g_next is not None else [])
    outs = [(D_MODEL, F32), (D_MODEL, BF16), (D_MODEL, BF16)] + ([(D_MODEL, BF16)] if g_next is not None else [])
    return _ew("ple_fwd", body, ins, outs, rows, mms=[(hn, w_pg), (p, w_pp)])


def _ple_bwd(dx2, u, e, rows):
    def body(ins, outs, _):
        d, s = ins[0][...], jax.nn.sigmoid(ins[1][...].astype(F32))
        outs[0][...] = (d * s).astype(BF16)
        outs[1][...] = (d * ins[2][...].astype(F32) * (s * (1.0 - s))).astype(BF16)
    return _ew("ple_bwd", body, [(dx2, D_MODEL, 0), (u, D_MODEL, 0), (e, D_MODEL, 0)],
               [(D_MODEL, BF16)] * 2, rows)


def _loss_head(x, g, target, rows):
    def body(ins, outs, accs):
        xv, gv = ins[0][...], ins[1][...]
        r = lax.rsqrt(jnp.mean(xv * xv, axis=-1, keepdims=True) + EPS)
        xhat = xv * r
        err = xhat * gv - ins[2][...]
        accs[0][...] += jnp.broadcast_to(0.5 * jnp.sum(jnp.mean(err * err, axis=-1, keepdims=True),
                                                       axis=0, keepdims=True), (1, LANES))
        dyv = err * (1.0 / D_MODEL)
        accs[1][...] += jnp.sum(dyv * xhat, axis=0, keepdims=True)
        dy = dyv * gv
        outs[0][...] = r * (dy - xhat * jnp.mean(dy * xhat, axis=-1, keepdims=True))
    ins = [(x, D_MODEL, 0), (g.reshape(1, D_MODEL), None, None), (target, D_MODEL, 0)]
    return _ew("loss_head", body, ins, [(D_MODEL, F32)], rows, accs=[(1, LANES), (1, D_MODEL)])


def _pad_heads_cols(w, n_heads, dim):
    k = w.shape[0]
    return jnp.pad(w.reshape(k, n_heads, dim), ((0, 0), (0, 0), (0, LANES - dim))).reshape(k, n_heads * LANES)


def _unpad_heads_cols(w, n_heads, dim):
    k = w.shape[0]
    return w.reshape(k, n_heads, LANES)[:, :, :dim].reshape(k, n_heads * dim)


def _layer_weights(w, i):
    segs = jnp.split(w['w_in'][i], list(_cumsum(IN_SIZES))[:-1], axis=1)
    a_q, a_k, a_v, a_gate, b_qd, b_kvd, b_kr, b_gate, m_a, m_b = segs
    kr = jnp.pad(b_kr, ((0, 0), (HEAD_DIM, LANES - MLA_QK)))
    w_in = jnp.concatenate([
        m_a, m_b, _pad_heads_cols(a_q, HEADS, HEAD_DIM), a_gate, b_gate, _pad_heads_cols(a_k, SWA_KV_HEADS, HEAD_DIM),
        _pad_heads_cols(a_v, SWA_KV_HEADS, HEAD_DIM), b_qd, b_kvd, kr], axis=1)
    w_uq = _pad_heads_cols(w['w_uq'][i], HEADS, MLA_QK)
    ukv = w['w_ukv'][i].reshape(MLA_KV_LORA, HEADS, 2 * HEAD_DIM)
    pad = ((0, 0), (0, 0), (0, HEAD_DIM))
    w_ukv = jnp.concatenate([jnp.pad(ukv[:, :, :HEAD_DIM], pad).reshape(MLA_KV_LORA, HPAD),
                             jnp.pad(ukv[:, :, HEAD_DIM:], pad).reshape(MLA_KV_LORA, HPAD)], axis=1)
    w_br_a = _pad_heads_cols(w['w_br_a'][i].T, HEADS, HEAD_DIM).T
    w_br_b = _pad_heads_cols(w['w_br_b'][i].T, HEADS, HEAD_DIM).T
    out = dict(w_in=w_in, w_uq=w_uq, w_ukv=w_ukv, w_br_a=w_br_a, w_br_b=w_br_b, w_out=w['w_out'][i],
               w_pg=w['w_ple_gate'][i], w_pp=w['w_ple_proj'][i])
    for name in ('w_in', 'w_uq', 'w_ukv', 'w_br_a', 'w_br_b', 'w_out', 'w_pg'):
        out[name + '_t'] = out[name].T
    return out


def _cumsum(sizes):
    acc, out = 0, []
    for s in sizes:
        acc += s
        out.append(acc)
    return out


def _unpad_grads(g):
    d = g['w_in']
    seg = lambda off, width: d[:, off:off + width]
    b_kr = seg(Z_BKR, LANES)[:, HEAD_DIM:MLA_QK]
    w_in = jnp.concatenate([
        _unpad_heads_cols(seg(Z_AQ, HPAD), HEADS, HEAD_DIM), _unpad_heads_cols(seg(Z_AK, KV_W), SWA_KV_HEADS, HEAD_DIM),
        _unpad_heads_cols(seg(Z_AV, KV_W), SWA_KV_HEADS, HEAD_DIM), seg(Z_AGATE, GATE_W),
        seg(Z_BQD, MLA_Q_LORA), seg(Z_BKVD, MLA_KV_LORA), b_kr, seg(Z_BGATE, GATE_W),
        seg(Z_MA, D_MODEL), seg(Z_MB, D_MODEL)], axis=1)
    w_uq = _unpad_heads_cols(g['w_uq'], HEADS, MLA_QK)
    ukv = g['w_ukv'].reshape(MLA_KV_LORA, 2, HEADS, LANES)[:, :, :, :HEAD_DIM]
    w_ukv = jnp.concatenate([ukv[:, 0], ukv[:, 1]], axis=-1).reshape(MLA_KV_LORA, HEADS * 2 * HEAD_DIM)
    w_br_a = _unpad_heads_cols(g['w_br_a'].T, HEADS, HEAD_DIM).T
    w_br_b = _unpad_heads_cols(g['w_br_b'].T, HEADS, HEAD_DIM).T
    return dict(w_in=w_in, w_uq=w_uq, w_ukv=w_ukv, w_br_a=w_br_a, w_br_b=w_br_b, w_out=g['w_out'],
                w_ple_gate=g['w_pg'], w_ple_proj=g['w_pp'], g_mix=g['g_mix'], sink=g['sink'], g_q=g['g_q'],
                g_kv=g['g_kv'], g_ple=g['g_ple'])


def _layer_fwd(x0, h, p_i, lw, sm, i, pos_col, pos_row, tabs, B, S):
    T = B * S
    z, a_gate, qdn, kvdn = _mm("proj_in", h, lw['w_in'], BF16, f32_cols=(Z_AGATE, GATE_W),
                               norms=[(Z_BQD, sm['g_q'][i]), (Z_BKVD, sm['g_kv'][i])])
    sink_row = jnp.pad(sm['sink'][i], (0, LANES - HEADS)).reshape(1, LANES)
    oa_raw, oa, lse_a = _swa_fwd(z, a_gate, pos_col, pos_row, sink_row, B, S)
    qf, kf, vf = _mla_prep(qdn, lw['w_uq'], kvdn, lw['w_ukv'], z, tabs, T)
    ob_raw, ob, lse_b = _mla_fwd(qf, kf, vf, z, B, S)
    ua, = _mm("proj_br_a", oa, lw['w_br_a'], BF16)
    ub, = _mm("proj_br_b", ob, lw['w_br_b'], BF16)
    y, x1, hn = _merge_out(ua, ub, z, lw['w_out'], x0, sm['g_ple'][i], T)
    g_next = sm['g_mix'][i + 1] if i + 1 < DEPTH else None
    x2, u, e, *h_next = _ple_fwd(x1, hn, lw['w_pg'], p_i, lw['w_pp'], g_next, T)
    saved = dict(x0=x0, h=h, z=z, a_gate=a_gate, sink_row=sink_row, oa_raw=oa_raw, oa=oa, lse_a=lse_a, qdn=qdn, kvdn=kvdn,
                 qf=qf, kf=kf, vf=vf, ob_raw=ob_raw, ob=ob, lse_b=lse_b, ua=ua, ub=ub, y=y, x1=x1, hn=hn,
                 u=u, e=e, p=p_i)
    return x2, (h_next[0] if h_next else None), saved


def _layer_bwd(dx2, sv, lw, sm, i, pos_col, pos_row, tabs, B, S):
    T = B * S
    z = sv['z']
    g = {}
    d_e, d_u = _ple_bwd(dx2, sv['u'], sv['e'], T)
    g['w_pp'] = _mm_tn("grad_pp", sv['p'], d_e)
    g['w_pg'] = _mm_tn("grad_pg", sv['hn'], d_u)
    dx1, g['g_ple'] = _rms_bwd("norm_ple_bwd", sv['x1'], D_MODEL, 0, sm['g_ple'][i], (d_u, lw['w_pg_t']), T, F32,
                               dres=dx2)
    g['w_out'] = _mm_tn("grad_out", sv['y'], dx1)
    dz = lax.empty((T, Z_WIDTH), BF16)
    d_ua, d_ub, dz = _merge_bwd((dx1, lw['w_out_t']), sv['ua'], sv['ub'], z, dz, T)
    g['w_br_a'] = _mm_tn("grad_br_a", sv['oa'], d_ua)
    g['w_br_b'] = _mm_tn("grad_br_b", sv['ob'], d_ub)
    dob_raw, dz, delta_b = _gate_bwd("gate_b_bwd", (d_ub, lw['w_br_b_t']), sv['ob_raw'], z, Z_BGATE // GATE_W,
                                     dz, Z_BGATE // GATE_W, T)
    nq, groups = S // MLA_T, HEADS // MLA_HGB
    delta_rows = delta_b[:, :HEADS].reshape(B, nq, MLA_T, groups, MLA_HGB).transpose(0, 3, 1, 4, 2)
    lse_rows = sv['lse_b'].transpose(0, 2, 1, 3, 4).reshape(B, nq, groups, MLA_HGB, MLA_T).transpose(0, 2, 1, 3, 4)
    dq, dk, dv = _mla_bwd(sv['qf'], sv['kf'], sv['vf'], dob_raw, lse_rows, delta_rows, B, S)
    dq_pre, dkv_pre, dz = _mla_prep_bwd(dq, dk, dv, tabs, dz, T)
    g['w_uq'] = _mm_tn("grad_uq", sv['qdn'], dq_pre)
    g['w_ukv'] = _mm_tn("grad_ukv", sv['kvdn'], dkv_pre)
    dz, g['g_q'] = _rms_bwd("norm_q_bwd", z, MLA_Q_LORA, Z_BQD // MLA_Q_LORA, sm['g_q'][i],
                            (dq_pre, lw['w_uq_t']), T, BF16, into=(dz, Z_BQD // MLA_Q_LORA))
    dz, g['g_kv'] = _rms_bwd("norm_kv_bwd", z, MLA_KV_LORA, Z_BKVD // MLA_KV_LORA, sm['g_kv'][i],
                             (dkv_pre, lw['w_ukv_t']), T, BF16, into=(dz, Z_BKVD // MLA_KV_LORA))
    doa_raw, dz, delta_a = _gate_bwd("gate_a_bwd", (d_ua, lw['w_br_a_t']), sv['oa_raw'], sv['a_gate'], 0,
                                     dz, Z_AGATE // GATE_W, T)
    dz, d_ak, d_av, dsink = _swa_bwd(z, pos_col, pos_row, sv['sink_row'], sv['lse_a'], doa_raw, delta_a, dz, B, S)
    dz = _kv_grad_cast(d_ak, d_av, dz, T)
    g['sink'] = dsink[0, :HEADS]
    g['w_in'] = _mm_tn("grad_in", sv['h'], dz, tk=1024, tn=Z_WIDTH // 2)
    dx0, g['g_mix'] = _rms_bwd("norm_mix_bwd", sv['x0'], D_MODEL, 0, sm['g_mix'][i], (dz, lw['w_in_t']), T, F32,
                               dres=dx1)
    for name in ('g_ple', 'g_q', 'g_kv', 'g_mix'):
        g[name] = g[name][0]
    return dx0, g


def _local_step(x, p, positions, wfull, sm, loss_target):
    B, S, _ = x.shape
    T = B * S
    pos_col = positions.reshape(T, 1)
    pos_row = positions.reshape(T // BLOCK, 1, BLOCK)
    half = MLA_ROPE // 2
    inv = ROPE_THETA ** (-jnp.arange(0, MLA_ROPE, 2, dtype=F32) / MLA_ROPE)
    inv_lane = jnp.tile(inv, LANES // half).reshape(1, LANES)
    tabs = _rope_tables(pos_col, inv_lane, T)
    xc = x.reshape(T, D_MODEL)
    h = _rms_fwd("norm_mix", xc, D_MODEL, 0, sm['g_mix'][0], T)
    lws, saved = [], []
    for i in range(DEPTH):
        lw = _layer_weights(wfull, i)
        xc, h, sv = _layer_fwd(xc, h, p[i].reshape(T, PLE_DIM), lw, sm, i, pos_col, pos_row, tabs, B, S)
        lws.append(lw)
        saved.append(sv)
    dx, loss, dg_final = _loss_head(xc, sm['g_final'], loss_target.reshape(T, D_MODEL), T)
    layer_grads = [None] * DEPTH
    for i in reversed(range(DEPTH)):
        dx, g = _layer_bwd(dx, saved[i], lws[i], sm, i, pos_col, pos_row, tabs, B, S)
        layer_grads[i] = _unpad_grads(g)
    return loss, dx.reshape(B, S, D_MODEL), layer_grads, dg_final[0]


SMALL_ROWS = 48


SMALL_SIZE = 2 * (2 * D_MODEL + HEADS + MLA_Q_LORA + MLA_KV_LORA) + D_MODEL


def _pack_small(arrs, tail=()):
    flat = jnp.concatenate([arrs[name].reshape(-1) for name in SMALL] + [t.reshape(1) for t in tail])
    return jnp.pad(flat, (0, SMALL_ROWS * LANES - flat.shape[0])).reshape(SMALL_ROWS, LANES)


def _unpack_small(block, shapes):
    flat = block.reshape(-1)
    out, off = {}, 0
    for name in SMALL:
        n = math.prod(shapes[name])
        out[name] = flat[off:off + n].reshape(shapes[name])
        off += n
    return out


def _flipped(shard_shape):
    return shard_shape[-1] % LANES != 0


def _to_slots(g, axis):
    r, c = g.shape
    if axis == 0:
        return g.reshape(N_CHIPS, r // N_CHIPS, c)
    return g.reshape(r, N_CHIPS, c // N_CHIPS).transpose(1, 0, 2)


def _div_tile(rows, cap):
    return next(t for t in range(min(cap, rows) // 8 * 8, 0, -8) if rows % t == 0)


def _units(shapes):
    units = []
    for w, shape in enumerate(shapes):
        r = shape[-2]
        n = next((n for n in (8, 4, 2) if r % (16 * n) == 0), 1) if r >= 512 else 1
        units += [(w, k * (r // n), r // n) for k in range(n)]
    return units


def _place():
    x, y, c = lax.axis_index("x"), lax.axis_index("y"), lax.axis_index("c")
    chips = [(1 - x, y), (x, 1 - y), (1 - x, 1 - y)]
    return x, y, c, chips


ANY = pl.BlockSpec(memory_space=pl.ANY)


def _remote(send_sems, recv_sems, k, src, dst, to):
    return pltpu.make_async_remote_copy(src_ref=src, dst_ref=dst, send_sem=send_sems.at[k],
                                        recv_sem=recv_sems.at[k], device_id=to, device_id_type=MESH)


def _gather_weights(shards, carried):
    n, nc = len(shards), len(carried)
    units = _units([s.shape for s in shards])
    nu = len(units)

    def body(*refs):
        ins, outs = refs[:n], refs[n + nc:2 * n + nc]
        send_sems, recv_sems, local_sems = refs[2 * (n + nc):]
        x, y, c, chips = _place()
        me = 2 * x + y
        sibling = (x, y, 1 - c)
        copy = functools.partial(_remote, send_sems, recv_sems)
        keeps, sends = [], []
        for u, (w, r0, nr) in enumerate(units):
            rows = pl.ds(r0, nr)
            keeps.append(pltpu.make_async_copy(ins[w].at[:, rows, :], outs[w].at[me, :, rows, :], local_sems.at[u]))
            keeps[-1].start()
        for j, (cx, cy) in enumerate(chips):
            for u, (w, r0, nr) in enumerate(units):
                rows = pl.ds(r0, nr)
                sends.append(copy(j * nu + u, ins[w].at[c, rows, :], outs[w].at[me, c, rows, :], (cx, cy, c)))
                sends[-1].start()
        for j, (cx, cy) in enumerate(chips):
            for u, (w, r0, nr) in enumerate(units):
                landed = outs[w].at[2 * cx + cy, c, pl.ds(r0, nr), :]
                copy(j * nu + u, landed, landed, (cx, cy, c)).wait_recv()
                sends.append(copy((3 + j) * nu + u, landed, landed, sibling))
                sends[-1].start()
        for j, (cx, cy) in enumerate(chips):
            for u, (w, r0, nr) in enumerate(units):
                other = outs[w].at[2 * cx + cy, 1 - c, pl.ds(r0, nr), :]
                copy((3 + j) * nu + u, other, other, sibling).wait_recv()
        for cp in sends:
            cp.wait_send()
        for keep in keeps:
            keep.wait()

    out_shape = [jax.ShapeDtypeStruct((N_CHIPS,) + s.shape, s.dtype) for s in shards]
    out_shape += [jax.ShapeDtypeStruct(a.shape, a.dtype) for a in carried]
    res = pl.pallas_call(
        body, name="gather_weights", out_shape=out_shape,
        in_specs=[ANY] * (n + nc), out_specs=[ANY] * (n + nc),
        input_output_aliases={n + k: n + k for k in range(nc)},
        scratch_shapes=[pltpu.SemaphoreType.DMA((6 * nu,)), pltpu.SemaphoreType.DMA((6 * nu,)),
                        pltpu.SemaphoreType.DMA((nu,))])(*shards, *carried)
    return res[:n], res[n:]


def _pair_exchange(g0, g1):
    n = len(g0)

    def body(*refs):
        layers, outs = (refs[:n], refs[n:2 * n]), refs[2 * n:3 * n]
        send_sems, recv_sems = refs[3 * n:]
        x, y, c, _ = _place()
        copy = functools.partial(_remote, send_sems, recv_sems)
        for w in range(n):
            for q in range(N_CHIPS):
                for layer in range(DEPTH):
                    cp = copy(N_CHIPS * w + q, layers[layer][w].at[q], outs[w].at[q], (x, y, 1 - c))
                    pl.when(c == 1 - layer)(cp.start)
        for w in range(n):
            for q in range(N_CHIPS):
                copy(N_CHIPS * w + q, layers[0][w].at[q], outs[w].at[q], (x, y, 1 - c)).wait()

    return pl.pallas_call(
        body, name="pair_exchange", out_shape=[jax.ShapeDtypeStruct(g.shape, g.dtype) for g in g0],
        in_specs=[ANY] * (2 * n), out_specs=[ANY] * n,
        scratch_shapes=[pltpu.SemaphoreType.DMA((N_CHIPS * n,)), pltpu.SemaphoreType.DMA((N_CHIPS * n,))])(*g0, *g1)


def _pair_sum(name, g0, g1, theirs, cflag):
    shape = theirs.shape
    rows, width = shape[0] * shape[1], shape[2]

    def body(ins, outs, _):
        mine = jnp.where(ins[3][0:1, 0:1] == 0.0, ins[0][...], ins[1][...])
        tot = mine + ins[2][...]
        outs[0][...] = tot
        outs[1][...] = tot.astype(BF16)
    ins = [(a.reshape(rows, width), width, 0) for a in (g0, g1, theirs)] + [(cflag, None, None)]
    f32, bf16 = _ew(name, body, ins, [(width, F32), (width, BF16)], rows, tm=_div_tile(rows, ROW_TILE))
    return f32.reshape(shape), bf16.reshape(shape)


def _chip_exchange(parts):
    n = len(parts)

    def body(*refs):
        ins, outs = refs[:n], refs[n:2 * n]
        send_sems, recv_sems = refs[2 * n:]
        x, y, c, chips = _place()
        copy = functools.partial(_remote, send_sems, recv_sems)
        sends = []
        for j, (cx, cy) in enumerate(chips):
            for w in range(n):
                sends.append(copy(j * n + w, ins[w].at[2 * cx + cy], outs[w].at[j], (cx, cy, c)))
                sends[-1].start()
        for j, (cx, cy) in enumerate(chips):
            for w in range(n):
                copy(j * n + w, outs[w].at[j], outs[w].at[j], (cx, cy, c)).wait_recv()
        for cp in sends:
            cp.wait_send()

    return pl.pallas_call(
        body, name="chip_exchange",
        out_shape=[jax.ShapeDtypeStruct((3,) + a.shape[1:], a.dtype) for a in parts],
        in_specs=[ANY] * n, out_specs=[ANY] * n,
        scratch_shapes=[pltpu.SemaphoreType.DMA((3 * n,)), pltpu.SemaphoreType.DMA((3 * n,))])(*parts)


def _chip_sum(name, part, landed, chipflag):
    _, r, width = part.shape
    tm = _div_tile(r, ROW_TILE // 2)

    def kern(p_ref, l_ref, flag_ref, o_ref):
        me = flag_ref[0:1, 0:1]
        own = jnp.where(me == 0.0, p_ref[0], jnp.where(me == 1.0, p_ref[1], jnp.where(me == 2.0, p_ref[2], p_ref[3])))
        o_ref[...] = ((own + l_ref[0].astype(F32)) + l_ref[1].astype(F32)) + l_ref[2].astype(F32)

    return pl.pallas_call(
        kern, name=name, grid=(r // tm,),
        in_specs=[pl.BlockSpec((N_CHIPS, tm, width), lambda i: (0, i, 0)),
                  pl.BlockSpec((3, tm, width), lambda i: (0, i, 0)),
                  pl.BlockSpec((1, LANES), lambda i: (0, 0))],
        out_specs=pl.BlockSpec((tm, width), lambda i: (i, 0)),
        out_shape=jax.ShapeDtypeStruct((r, width), F32), compiler_params=_params(("arbitrary",)))(part, landed, chipflag)


def _pair_broadcast(mine):
    n = len(mine)
    units = _units([a.shape for a in mine])

    def body(*refs):
        ins, outs = refs[:n], refs[n:2 * n]
        send_sems, recv_sems = refs[2 * n:]
        x, y, c, _ = _place()
        copy = functools.partial(_remote, send_sems, recv_sems)
        cps = [copy(u, ins[w].at[pl.ds(r0, nr), :], outs[w].at[pl.ds(r0, nr), :], (x, y, 1 - c))
               for u, (w, r0, nr) in enumerate(units)]
        for cp in cps:
            cp.start()
        for cp in cps:
            cp.wait()

    return pl.pallas_call(
        body, name="pair_broadcast", out_shape=[jax.ShapeDtypeStruct(a.shape, a.dtype) for a in mine],
        in_specs=[ANY] * n, out_specs=[ANY] * n,
        scratch_shapes=[pltpu.SemaphoreType.DMA((len(units),)), pltpu.SemaphoreType.DMA((len(units),))])(*mine)


def _small_allreduce(v):
    offsets = [(dx, dy, dc) for dx in (0, 1) for dy in (0, 1) for dc in (0, 1)][1:]

    def body(v_ref, out_ref, recv_ref, send_sems, recv_sems):
        x, y, c, _ = _place()
        flip = lambda a, d: 1 - a if d else a
        peers = [(flip(x, dx), flip(y, dy), flip(c, dc)) for dx, dy, dc in offsets]
        copy = functools.partial(_remote, send_sems, recv_sems)
        me = 4 * x + 2 * y + c
        recv_ref[me] = v_ref[...]
        cps = [copy(k, v_ref, recv_ref.at[me], peer) for k, peer in enumerate(peers)]
        for cp in cps:
            cp.start()
        for k, (px, py, pc) in enumerate(peers):
            landed = recv_ref.at[4 * px + 2 * py + pc]
            copy(k, landed, landed, (px, py, pc)).wait_recv()
        for cp in cps:
            cp.wait_send()
        tot = recv_ref[0]
        for d in range(1, 8):
            tot = tot + recv_ref[d]
        out_ref[...] = tot

    vmem = pl.BlockSpec(memory_space=pltpu.VMEM)
    return pl.pallas_call(
        body, name="small_allreduce", out_shape=jax.ShapeDtypeStruct(v.shape, v.dtype),
        in_specs=[vmem], out_specs=vmem,
        scratch_shapes=[pltpu.VMEM((8,) + v.shape, v.dtype), pltpu.SemaphoreType.DMA((7,)),
                        pltpu.SemaphoreType.DMA((7,))])(v)


def _adam_math(gv, wv, mv, vv):
    mv = ADAM_B1 * mv + (1.0 - ADAM_B1) * gv
    vv = ADAM_B2 * vv + (1.0 - ADAM_B2) * (gv * gv)
    m_hat = mv / (1.0 - ADAM_B1 ** ADAM_STEP)
    v_hat = vv / (1.0 - ADAM_B2 ** ADAM_STEP)
    return -ADAM_LR * (m_hat / (jnp.sqrt(v_hat) + ADAM_EPS) + ADAM_WD * wv), mv, vv


def _adamw_big(name, mine, theirs, cflag, w, m, v):
    _, r, width = w.shape
    tm = _div_tile(r, ROW_TILE // 2)

    def kern(mine_ref, theirs_ref, flag_ref, w_ref, m_ref, v_ref, g_ref, d_ref, nm_ref, nv_ref):
        layer = pl.program_id(0).astype(F32)
        gv = jnp.where(flag_ref[0:1, 0:1] == layer, mine_ref[...], theirs_ref[...])
        g_ref[0] = gv
        d_ref[0], nm_ref[0], nv_ref[0] = _adam_math(gv, w_ref[0], m_ref[0], v_ref[0])

    flat = pl.BlockSpec((tm, width), lambda l, i: (i, 0))
    stacked = pl.BlockSpec((1, tm, width), lambda l, i: (l, i, 0))
    return pl.pallas_call(
        kern, name=name, grid=(DEPTH, r // tm),
        in_specs=[flat, flat, pl.BlockSpec((1, LANES), lambda l, i: (0, 0)), stacked, stacked, stacked],
        out_specs=[stacked] * 4, out_shape=[jax.ShapeDtypeStruct(w.shape, F32)] * 4,
        compiler_params=_params(("arbitrary", "arbitrary")))(mine, theirs, cflag, w, m, v)


def _adamw_small(g, w, m, v):
    def body(ins, outs, _):
        outs[0][...], outs[1][...], outs[2][...] = _adam_math(*(r[...] for r in ins))
    return _ew("adamw_small", body, [(a, LANES, 0) for a in (g, w, m, v)], [(LANES, F32)] * 3, SMALL_ROWS)


def kernel(x, p, positions, g_mix, w_in, sink, g_q, w_uq, g_kv, w_ukv, w_br_a, w_br_b, w_out, g_ple, w_ple_gate, w_ple_proj, g_final, loss_target, m_g_mix, m_w_in, m_sink, m_g_q, m_w_uq, m_g_kv, m_w_ukv, m_w_br_a, m_w_br_b, m_w_out, m_g_ple, m_w_ple_gate, m_w_ple_proj, m_g_final, v_g_mix, v_w_in, v_sink, v_g_q, v_w_uq, v_g_kv, v_w_ukv, v_w_br_a, v_w_br_b, v_w_out, v_g_ple, v_w_ple_gate, v_w_ple_proj, v_g_final):
    w = dict(g_mix=g_mix, w_in=w_in, sink=sink, g_q=g_q, w_uq=w_uq, g_kv=g_kv, w_ukv=w_ukv, w_br_a=w_br_a,
             w_br_b=w_br_b, w_out=w_out, g_ple=g_ple, w_ple_gate=w_ple_gate, w_ple_proj=w_ple_proj, g_final=g_final)
    m = dict(g_mix=m_g_mix, w_in=m_w_in, sink=m_sink, g_q=m_g_q, w_uq=m_w_uq, g_kv=m_g_kv, w_ukv=m_w_ukv,
             w_br_a=m_w_br_a, w_br_b=m_w_br_b, w_out=m_w_out, g_ple=m_g_ple, w_ple_gate=m_w_ple_gate,
             w_ple_proj=m_w_ple_proj, g_final=m_g_final)
    v = dict(g_mix=v_g_mix, w_in=v_w_in, sink=v_sink, g_q=v_g_q, w_uq=v_w_uq, g_kv=v_g_kv, w_ukv=v_w_ukv,
             w_br_a=v_w_br_a, w_br_b=v_w_br_b, w_out=v_w_out, g_ple=v_g_ple, w_ple_gate=v_w_ple_gate,
             w_ple_proj=v_w_ple_proj, g_final=v_g_final)
    wfull = _gather_full(w)
    sm = {name: w[name] for name in SMALL}
    loss_row, grad_x, layer_grads, dg_final = _local_step(x, p, positions, wfull, sm, loss_target)
    res, loss = _update(layer_grads, dg_final, loss_row[0, 0], w, m, v)
    return (loss, grad_x, *[res[name][kind] for kind in range(4) for name in WEIGHT_NAMES])


def _gather_behind(name, collective_id, shards):
    n = len(shards)
    srcs = [jax.new_ref(s, memory_space=pltpu.MemorySpace.HBM) for s in shards]
    lands = [jax.empty_ref(jax.ShapeDtypeStruct((N_CHIPS,) + s.shape, s.dtype), memory_space=pltpu.MemorySpace.HBM)
             for s in shards]

    @pl.kernel(mesh=plsc.ScalarSubcoreMesh(axis_name="sequencer", num_cores=1), name=name,
               scratch_types=(pltpu.SemaphoreType.DMA((3 * n,)), pltpu.SemaphoreType.DMA((3 * n,)),
                              pltpu.SemaphoreType.DMA((n,))),
               compiler_params=pltpu.CompilerParams(collective_id=collective_id))
    def launch(send_sems, recv_sems, local_sems):
        x, y, c, chips = _place()
        me = 2 * x + y
        barrier = pltpu.get_barrier_semaphore()
        for cx, cy in chips:
            pl.semaphore_signal(barrier, inc=1, device_id=(cx, cy, c), device_id_type=MESH)
        pl.semaphore_wait(barrier, len(chips))
        copy = functools.partial(_remote, send_sems, recv_sems)
        keeps = [pltpu.make_async_copy(srcs[w], lands[w].at[me], local_sems.at[w]) for w in range(n)]
        cps = [copy(j * n + w, srcs[w], lands[w].at[me], (cx, cy, c))
               for j, (cx, cy) in enumerate(chips) for w in range(n)]
        for cp in keeps + cps:
            cp.start()
        for cp in keeps + cps:
            cp.wait()

    launch()
    return [land[...] for land in lands]


def _gather_full(w):
    shards = [w[name].astype(BF16) for name, _ in SHARDED]
    w_in0 = shards[0][0]
    first, later = _gather_weights([w_in0.reshape((2, w_in0.shape[0] // 2) + w_in0.shape[1:])],
                                   [s[0] for s in shards[1:]] + [s[1] for s in shards])
    n_rest = len(shards) - 1
    layer0 = [first[0].reshape((N_CHIPS,) + w_in0.shape)] + _gather_behind("gather_rest", 0, later[:n_rest])
    layer1 = _gather_behind("gather_next", 1, later[n_rest:])
    return {name: [jnp.concatenate(list(blocks[k]), axis=axis - 1) for blocks in (layer0, layer1)]
            for k, (name, axis) in enumerate(SHARDED)}


def _update(layer_grads, dg_final, loss_local, w, m, v):
    small_shapes = {name: w[name].shape for name in SMALL}
    cflag = jnp.full((1, LANES), lax.axis_index("c"), F32)
    chipflag = jnp.full((1, LANES), 2 * lax.axis_index("x") + lax.axis_index("y"), F32)

    slots = [[_to_slots(layer_grads[layer][name], axis - 1) for name, axis in SHARDED] for layer in range(DEPTH)]
    theirs = _pair_exchange(slots[0], slots[1])
    pair = [_pair_sum("pair_sum_" + name, slots[0][k], slots[1][k], theirs[k], cflag)
            for k, (name, _) in enumerate(SHARDED)]
    landed = _chip_exchange([bf16 for _, bf16 in pair])
    mine = [_chip_sum("chip_sum_" + name, pair[k][0], landed[k], chipflag) for k, (name, _) in enumerate(SHARDED)]
    other = _pair_broadcast(mine)
    res = {}
    for k, (name, _) in enumerate(SHARDED):
        flip = _flipped(w[name].shape)
        view = (lambda a: jnp.swapaxes(a, -1, -2)) if flip else (lambda a: a)
        outs = _adamw_big("adamw_" + name, view(mine[k]), view(other[k]), cflag, view(w[name]), view(m[name]),
                          view(v[name]))
        res[name] = tuple(view(a) for a in outs)

    gsmall = {name: jnp.stack([layer_grads[layer][name] for layer in range(DEPTH)]) for name in SMALL[:-1]}
    gsmall['g_final'] = dg_final
    gsum = _small_allreduce(_pack_small(gsmall, tail=[loss_local]))
    small = (gsum,) + tuple(_adamw_small(gsum, _pack_small(w), _pack_small(m), _pack_small(v)))
    for name, arrs in zip(SMALL, zip(*[[_unpack_small(a, small_shapes)[n] for n in SMALL] for a in small])):
        res[name] = arrs
    return res, gsum.reshape(-1)[SMALL_SIZE]
```

```python
import functools
import math

import jax
import jax.numpy as jnp
from jax import lax
from jax.experimental import pallas as pl
from jax.experimental.pallas import tpu as pltpu
from jax.experimental.pallas import tpu_sc as plsc

F32 = jnp.float32
BF16 = jnp.bfloat16

D_MODEL = 1024
DEPTH = 2
PLE_DIM = 256
BLOCK = 128
EPS = 1e-6
NEG = -1e30
HEADS = 8
SWA_KV_HEADS = 2
HEAD_DIM = 64
LANES = 128
HPAD = HEADS * LANES
MLA_QK = 96
MLA_ROPE = 32
MLA_Q_LORA = 256
MLA_KV_LORA = 128
ROPE_THETA = 10000.0
IN_SIZES = (512, 128, 128, 512, 256, 128, 32, 512, 1024, 1024)

Z_MA, Z_MB, Z_AQ, Z_AGATE, Z_BGATE = 0, 1024, 2048, 3072, 3584
Z_AK, Z_AV, Z_BQD, Z_BKVD, Z_BKR = 4096, 4352, 4608, 4864, 4992
Z_WIDTH = 5120
GATE_W = HEADS * HEAD_DIM
KV_W = SWA_KV_HEADS * LANES

ADAM_LR, ADAM_B1, ADAM_B2, ADAM_EPS, ADAM_WD, ADAM_STEP = 0.001, 0.9, 0.999, 1e-08, 0.01, 10

VMEM_LIMIT = 56 * 1024 * 1024
MESH = pl.DeviceIdType.MESH

WEIGHT_NAMES = ('g_mix', 'w_in', 'sink', 'g_q', 'w_uq', 'g_kv', 'w_ukv', 'w_br_a', 'w_br_b',
                'w_out', 'g_ple', 'w_ple_gate', 'w_ple_proj', 'g_final')
SHARDED = (('w_in', 2), ('w_uq', 2), ('w_ukv', 2), ('w_br_a', 2), ('w_br_b', 2),
           ('w_out', 1), ('w_ple_gate', 1), ('w_ple_proj', 2))
SMALL = ('g_mix', 'sink', 'g_q', 'g_kv', 'g_ple', 'g_final')
N_CHIPS = 4


def _params(sem):
    return pltpu.CompilerParams(dimension_semantics=sem, vmem_limit_bytes=VMEM_LIMIT)


MM_TN = 512
ROW_TILE = 512
BIG_WEIGHT_BYTES = 16 * 1024 * 1024


def _row_tile(rows, weight_bytes=0):
    tm = ROW_TILE // 2 if weight_bytes > BIG_WEIGHT_BYTES else ROW_TILE
    return min(tm, rows)


def _ew(name, body, ins, outs, rows, accs=(), mms=(), tm=None):
    n_mm, n_in, n_out = len(mms), len(ins), len(outs)
    if tm is None:
        tm = _row_tile(rows, sum(b.size * b.dtype.itemsize for _, b in mms))
    in_specs, args = [], []
    for a, b in mms:
        in_specs += [pl.BlockSpec((tm, a.shape[1]), lambda i: (i, 0)), pl.BlockSpec(b.shape, lambda i: (0, 0))]
        args += [a, b]
    for arr, width, cb in ins:
        if width is None:
            in_specs.append(pl.BlockSpec(arr.shape, lambda i, nd=arr.ndim: (0,) * nd))
        else:
            in_specs.append(pl.BlockSpec((tm, width), lambda i, cb=cb: (i, cb)))
        args.append(arr)
    out_shape, out_specs, aliases = [], [], {}
    for k, out in enumerate(outs):
        if len(out) == 4:
            aliases[len(args)] = k
            in_specs.append(pl.BlockSpec(memory_space=pl.ANY))
            args.append(out[2])
            out_shape.append(jax.ShapeDtypeStruct(out[2].shape, out[2].dtype))
            out_specs.append(pl.BlockSpec((tm, out[0]), lambda i, cb=out[3]: (i, cb)))
        else:
            out_shape.append(jax.ShapeDtypeStruct((rows, out[0]), out[1]))
            out_specs.append(pl.BlockSpec((tm, out[0]), lambda i: (i, 0)))
    n_in += len(aliases)
    out_shape += [jax.ShapeDtypeStruct(s, F32) for s in accs]
    out_specs += [pl.BlockSpec(s, lambda i: (0, 0)) for s in accs]

    def kern(*refs):
        mm_refs, refs = refs[:2 * n_mm], refs[2 * n_mm:]
        in_refs, out_refs = refs[:n_in - len(aliases)], refs[n_in:n_in + n_out]
        acc_refs, prod_refs = refs[n_in + n_out:n_in + n_out + len(accs)], refs[n_in + n_out + len(accs):]
        if acc_refs:
            @pl.when(pl.program_id(0) == 0)
            def _():
                for r in acc_refs:
                    r[...] = jnp.zeros_like(r)
        for k in range(n_mm):
            a_ref, b_ref, prod = mm_refs[2 * k], mm_refs[2 * k + 1], prod_refs[k]
            av = a_ref[...].astype(BF16)
            n = b_ref.shape[1]
            tn = min(MM_TN, n)
            for j in range(n // tn):
                cols = slice(j * tn, (j + 1) * tn)
                prod[:, cols] = jnp.dot(av, b_ref[:, cols], preferred_element_type=F32)
        body(tuple(prod_refs) + tuple(in_refs), out_refs, acc_refs)

    scratch = [pltpu.VMEM((tm, b.shape[1]), F32) for _, b in mms]
    res = pl.pallas_call(kern, name=name, grid=(rows // tm,), in_specs=in_specs, out_specs=out_specs,
                         out_shape=out_shape, scratch_shapes=scratch, input_output_aliases=aliases,
                         compiler_params=_params(("arbitrary",)))(*args)
    return res


def _rms(xv, gv):
    r = lax.rsqrt(jnp.mean(xv * xv, axis=-1, keepdims=True) + EPS)
    return ((xv * r) * gv).astype(BF16)


def _rms_fwd(name, x, width, cb, g, rows):
    def body(ins, outs, _):
        outs[0][...] = _rms(ins[0][...].astype(F32), ins[1][...])
    return _ew(name, body, [(x, width, cb), (g.reshape(1, width), None, None)], [(width, BF16)], rows)[0]


def _rms_bwd(name, x, width, cb, g, dh_mm, rows, out_dtype, dres=None, into=()):
    def body(ins, outs, accs):
        dhv, xv, gv = ins[0][...], ins[1][...].astype(F32), ins[2][...]
        r = lax.rsqrt(jnp.mean(xv * xv, axis=-1, keepdims=True) + EPS)
        xhat = xv * r
        accs[0][...] += jnp.sum(dhv * xhat, axis=0, keepdims=True)
        dy = dhv * gv
        dx = r * (dy - xhat * jnp.mean(dy * xhat, axis=-1, keepdims=True))
        if dres is not None:
            dx = dx + ins[3][...]
        outs[0][...] = dx.astype(out_dtype)
    ins = [(x, width, cb), (g.reshape(1, width), None, None)]
    if dres is not None:
        ins.append((dres, width, 0))
    return _ew(name, body, ins, [(width, out_dtype) + tuple(into)], rows, accs=[(1, width)], mms=[dh_mm])


def _mm(name, a, b, out_dtype, f32_cols=None, norms=(), tn=MM_TN):
    M, K = a.shape
    N = b.shape[1]
    tm, tn = _row_tile(M, b.size * b.dtype.itemsize), min(tn, N)
    c0, cw = f32_cols if f32_cols else (0, 0)
    n_norm, n_f32 = len(norms), 1 if f32_cols else 0
    assert c0 % tn == 0 and cw % tn == 0
    assert all(nc // tn == (nc + g.shape[-1] - 1) // tn for nc, g in norms)

    def kern(*refs):
        a_ref, b_ref, g_refs = refs[0], refs[1], refs[2:2 + n_norm]
        o_ref, extra = refs[2 + n_norm], refs[3 + n_norm:]
        av = a_ref[...].astype(BF16)
        for j in range(N // tn):
            cols = slice(j * tn, (j + 1) * tn)
            part = jnp.dot(av, b_ref[:, cols], preferred_element_type=F32)
            o_ref[:, cols] = part.astype(o_ref.dtype)
            if f32_cols and c0 <= j * tn and (j + 1) * tn <= c0 + cw:
                extra[0][:, j * tn - c0:(j + 1) * tn - c0] = part
            for k, (nc, g) in enumerate(norms):
                if nc // tn == j:
                    seg = part[:, nc - j * tn:nc - j * tn + g.shape[-1]]
                    extra[n_f32 + k][...] = _rms(seg, g_refs[k][...])

    in_specs = [pl.BlockSpec((tm, K), lambda i: (i, 0)), pl.BlockSpec((K, N), lambda i: (0, 0))]
    in_specs += [pl.BlockSpec((1, g.shape[-1]), lambda i: (0, 0)) for _, g in norms]
    widths = [(N, out_dtype)] + ([(cw, F32)] if f32_cols else []) + [(g.shape[-1], BF16) for _, g in norms]
    return pl.pallas_call(
        kern, name=name, grid=(M // tm,), in_specs=in_specs,
        out_specs=[pl.BlockSpec((tm, w), lambda i: (i, 0)) for w, _ in widths],
        out_shape=[jax.ShapeDtypeStruct((M, w), dt) for w, dt in widths],
        compiler_params=_params(("parallel",)))(a, b, *[g.reshape(1, -1) for _, g in norms])


def _mm_tn(name, a, b, tk=2048, tn=2048):
    T, M = a.shape
    N = b.shape[1]
    tn, tk = min(tn, N), min(tk, T)

    def kern(a_ref, b_ref, o_ref):
        k = pl.program_id(1)
        part = _dot_tn(a_ref[...].astype(BF16), b_ref[...].astype(BF16))

        @pl.when(k == 0)
        def _():
            o_ref[...] = part

        @pl.when(k > 0)
        def _():
            o_ref[...] += part

    return pl.pallas_call(
        kern, name=name, grid=(N // tn, T // tk),
        in_specs=[pl.BlockSpec((tk, M), lambda j, k: (k, 0)), pl.BlockSpec((tk, tn), lambda j, k: (k, j))],
        out_specs=pl.BlockSpec((M, tn), lambda j, k: (0, j)),
        out_shape=jax.ShapeDtypeStruct((M, N), F32),
        compiler_params=_params(("parallel", "arbitrary")))(a, b)


def _dot_nt(a, b):
    return lax.dot_general(a, b, (((1,), (1,)), ((), ())), preferred_element_type=F32)


def _dot_tn(a, b):
    return lax.dot_general(a, b, (((0,), (0,)), ((), ())), preferred_element_type=F32)


SWA_SCALE = HEAD_DIM ** -0.5


def _swa_band(n, pq_ref, pkp_ref, pkc_ref):
    posk = jnp.concatenate([pkp_ref[...], pkc_ref[...]], axis=0)
    dist = (pq_ref[0] - posk).astype(F32)
    kj = lax.broadcasted_iota(jnp.int32, (2 * BLOCK, BLOCK), 0)
    qi = lax.broadcasted_iota(jnp.int32, (2 * BLOCK, BLOCK), 1)
    t_abs = n * BLOCK + qi
    s_abs = n * BLOCK - BLOCK + kj
    return dist, (s_abs >= 0) & (s_abs <= t_abs) & (t_abs - s_abs < BLOCK)


SWA_GROUP = HEADS // SWA_KV_HEADS


def _head_gate(gate_ref, h):
    pair = gate_ref[:, (h // 2) * LANES:(h // 2 + 1) * LANES].astype(F32)
    return pair if h % 2 == 0 else pltpu.roll(pair, HEAD_DIM, 1)


def _swa_group_q(q_all, g):
    heads = range(g * SWA_GROUP, (g + 1) * SWA_GROUP)
    return jnp.concatenate([(q_all[:, h * LANES:(h + 1) * LANES] * SWA_SCALE).astype(BF16) for h in heads], axis=0)


def _swa_mask(s, dist, valid, h):
    return jnp.where(valid, s - (2.0 ** -(h + 1)) * dist, NEG)


def _rows_to_lanes(rows):
    block = jnp.concatenate(list(rows) + [jnp.zeros((LANES - len(rows), BLOCK), F32)], axis=0)
    return block.T


def _swa_specs(nb):
    prev = lambda b, n: b * nb + jnp.maximum(n - 1, 0)
    own = lambda b, n: b * nb + n
    return [
        pl.BlockSpec((BLOCK, HPAD), lambda b, n: (own(b, n), Z_AQ // HPAD)),
        pl.BlockSpec((BLOCK, KV_W), lambda b, n: (prev(b, n), Z_AK // KV_W)),
        pl.BlockSpec((BLOCK, KV_W), lambda b, n: (own(b, n), Z_AK // KV_W)),
        pl.BlockSpec((BLOCK, KV_W), lambda b, n: (prev(b, n), Z_AV // KV_W)),
        pl.BlockSpec((BLOCK, KV_W), lambda b, n: (own(b, n), Z_AV // KV_W)),
        pl.BlockSpec((1, 1, BLOCK), lambda b, n: (own(b, n), 0, 0)),
        pl.BlockSpec((BLOCK, 1), lambda b, n: (prev(b, n), 0)),
        pl.BlockSpec((BLOCK, 1), lambda b, n: (own(b, n), 0)),
    ]


def _swa_fwd(z, gate, pos_col, pos_row, sink_row, B, S):
    nb = S // BLOCK
    T = B * S

    def kern(q_ref, kp_ref, kc_ref, vp_ref, vc_ref, pq_ref, pkp_ref, pkc_ref, gate_ref, sink_ref,
             oraw_ref, og_ref, lse_ref):
        q_all = q_ref[...]
        kb = jnp.concatenate([kp_ref[...], kc_ref[...]], axis=0).astype(BF16)
        vb = jnp.concatenate([vp_ref[...], vc_ref[...]], axis=0).astype(BF16)
        dist, valid = _swa_band(pl.program_id(1), pq_ref, pkp_ref, pkc_ref)
        lse_rows = []
        for grp in range(SWA_KV_HEADS):
            gcols = slice(grp * LANES, (grp + 1) * LANES)
            s_all = _dot_nt(kb[:, gcols], _swa_group_q(q_all, grp))
            probs = []
            for hh in range(SWA_GROUP):
                h = grp * SWA_GROUP + hh
                s = _swa_mask(s_all[:, hh * BLOCK:(hh + 1) * BLOCK], dist, valid, h)
                sink_h = sink_ref[0:1, h:h + 1]
                m = jnp.maximum(jnp.max(s, axis=0, keepdims=True), sink_h)
                e = jnp.exp(s - m)
                denom = jnp.sum(e, axis=0, keepdims=True) + jnp.exp(sink_h - m)
                probs.append((e * (1.0 / denom)).astype(BF16))
                lse_rows.append(m + jnp.log(denom))
            o_all = jnp.dot(vb[:, gcols].T, jnp.concatenate(probs, axis=1), preferred_element_type=F32)
            for hh in range(SWA_GROUP):
                h = grp * SWA_GROUP + hh
                cols = slice(h * LANES, (h + 1) * LANES)
                o = o_all[:, hh * BLOCK:(hh + 1) * BLOCK].T
                oraw_ref[:, cols] = o
                g = _head_gate(gate_ref, h)
                og_ref[:, cols] = (o * (g * jax.nn.sigmoid(g))).astype(BF16)
        lse_ref[...] = _rows_to_lanes(lse_rows)

    own = lambda b, n: b * nb + n
    in_specs = _swa_specs(nb) + [
        pl.BlockSpec((BLOCK, GATE_W), lambda b, n: (own(b, n), 0)),
        pl.BlockSpec((1, LANES), lambda b, n: (0, 0)),
    ]
    out_specs = [pl.BlockSpec((BLOCK, HPAD), lambda b, n: (own(b, n), 0)),
                 pl.BlockSpec((BLOCK, HPAD), lambda b, n: (own(b, n), 0)),
                 pl.BlockSpec((BLOCK, LANES), lambda b, n: (own(b, n), 0))]
    out_shape = [jax.ShapeDtypeStruct((T, HPAD), F32), jax.ShapeDtypeStruct((T, HPAD), BF16),
                 jax.ShapeDtypeStruct((T, LANES), F32)]
    return pl.pallas_call(kern, name="swa_fwd", grid=(B, nb), in_specs=in_specs, out_specs=out_specs,
                          out_shape=out_shape, compiler_params=_params(("parallel", "arbitrary")))(
        z, z, z, z, z, pos_row, pos_col, pos_col, gate, sink_row)


def _swa_bwd(z, pos_col, pos_row, sink_row, lse, do_raw, delta, dz, B, S):
    nb = S // BLOCK
    T = B * S

    def kern(q_ref, kp_ref, kc_ref, vp_ref, vc_ref, pq_ref, pkp_ref, pkc_ref, sink_ref, lse_ref, do_ref,
             delta_ref, dz_ref, dq_ref, dk_ref, dv_ref, dsink_ref):
        b, n = pl.program_id(0), pl.program_id(1)

        @pl.when(n == 0)
        def _():
            dk_ref[...] = jnp.zeros_like(dk_ref)
            dv_ref[...] = jnp.zeros_like(dv_ref)

        @pl.when((b == 0) & (n == 0))
        def _():
            dsink_ref[...] = jnp.zeros_like(dsink_ref)

        q_all = q_ref[...]
        kb = jnp.concatenate([kp_ref[...], kc_ref[...]], axis=0).astype(BF16)
        vb = jnp.concatenate([vp_ref[...], vc_ref[...]], axis=0).astype(BF16)
        dist, valid = _swa_band(n, pq_ref, pkp_ref, pkc_ref)
        lse_t, delta_t = lse_ref[...].T, delta_ref[...].T
        lane1 = lax.broadcasted_iota(jnp.int32, (1, LANES), 1)
        dsink = jnp.zeros((1, LANES), F32)
        dk_band, dv_band = [], []
        for grp in range(SWA_KV_HEADS):
            gcols = slice(grp * LANES, (grp + 1) * LANES)
            heads = range(grp * SWA_GROUP, (grp + 1) * SWA_GROUP)
            qg = _swa_group_q(q_all, grp)
            dog = jnp.concatenate([do_ref[:, h * LANES:(h + 1) * LANES] for h in heads], axis=0)
            s_all = _dot_nt(kb[:, gcols], qg)
            dp_all = _dot_nt(vb[:, gcols], dog)
            ps, dss = [], []
            for hh, h in enumerate(heads):
                blk = slice(hh * BLOCK, (hh + 1) * BLOCK)
                lse_h, delta_h = lse_t[h:h + 1, :], delta_t[h:h + 1, :]
                p = jnp.exp(_swa_mask(s_all[:, blk], dist, valid, h) - lse_h)
                ps.append(p.astype(BF16))
                dss.append((p * (dp_all[:, blk] - delta_h)).astype(BF16))
                psink = jnp.exp(sink_ref[0:1, h:h + 1] - lse_h)
                dsink = dsink + jnp.where(lane1 == h, -jnp.sum(psink * delta_h, axis=1, keepdims=True), 0.0)
            dsg = jnp.concatenate(dss, axis=1)
            dq_all = jnp.dot(kb[:, gcols].T, dsg, preferred_element_type=F32) * SWA_SCALE
            for hh, h in enumerate(heads):
                dq_ref[:, h * LANES:(h + 1) * LANES] = dq_all[:, hh * BLOCK:(hh + 1) * BLOCK].T.astype(BF16)
            dk_band.append(jnp.dot(dsg, qg, preferred_element_type=F32))
            dv_band.append(jnp.dot(jnp.concatenate(ps, axis=1), dog, preferred_element_type=F32))
        dsink_ref[...] += dsink
        dkb = jnp.concatenate(dk_band, axis=1)
        dvb = jnp.concatenate(dv_band, axis=1)
        r_prev = pl.ds(pl.multiple_of(jnp.maximum(n - 1, 0) * BLOCK, BLOCK), BLOCK)
        r_own = pl.ds(pl.multiple_of(n * BLOCK, BLOCK), BLOCK)
        dk_ref[r_prev, :] += dkb[:BLOCK]
        dk_ref[r_own, :] += dkb[BLOCK:]
        dv_ref[r_prev, :] += dvb[:BLOCK]
        dv_ref[r_own, :] += dvb[BLOCK:]

    own = lambda b, n: b * nb + n
    in_specs = _swa_specs(nb) + [
        pl.BlockSpec((1, LANES), lambda b, n: (0, 0)),
        pl.BlockSpec((BLOCK, LANES), lambda b, n: (own(b, n), 0)),
        pl.BlockSpec((BLOCK, HPAD), lambda b, n: (own(b, n), 0)),
        pl.BlockSpec((BLOCK, LANES), lambda b, n: (own(b, n), 0)),
        pl.BlockSpec(memory_space=pl.ANY),
    ]
    out_specs = [pl.BlockSpec((BLOCK, HPAD), lambda b, n: (own(b, n), Z_AQ // HPAD)),
                 pl.BlockSpec((S, KV_W), lambda b, n: (b, 0)),
                 pl.BlockSpec((S, KV_W), lambda b, n: (b, 0)),
                 pl.BlockSpec((1, LANES), lambda b, n: (0, 0))]
    out_shape = [jax.ShapeDtypeStruct(dz.shape, dz.dtype), jax.ShapeDtypeStruct((T, KV_W), F32),
                 jax.ShapeDtypeStruct((T, KV_W), F32), jax.ShapeDtypeStruct((1, LANES), F32)]
    return pl.pallas_call(kern, name="swa_bwd", grid=(B, nb), in_specs=in_specs, out_specs=out_specs,
                          out_shape=out_shape, input_output_aliases={len(in_specs) - 1: 0},
                          compiler_params=_params(("arbitrary", "arbitrary")))(
        z, z, z, z, z, pos_row, pos_col, pos_col, sink_row, lse, do_raw, delta, dz)


MLA_T = 256
MLA_HG = 4
MLA_W = MLA_HG * LANES
MLA_HGB = 8
MLA_WB = MLA_HGB * LANES
MLA_SCALE = MLA_QK ** -0.5
LOG2E = 1.4426950408889634
MLA_QSCALE = MLA_SCALE * LOG2E


def _causal_t(s):
    key = lax.broadcasted_iota(jnp.int32, s.shape, 0)
    query = lax.broadcasted_iota(jnp.int32, s.shape, 1)
    return jnp.where(key <= query, s, NEG)


def _mla_fwd(q, k, v, z, B, S):
    T = B * S
    nq = S // MLA_T

    def kern(q_ref, k_ref, v_ref, gate_ref, oraw_ref, og_ref, lse_ref):
        i = pl.program_id(2)

        def scores(j):
            rows = pl.ds(pl.multiple_of(j * MLA_T, MLA_T), MLA_T)
            return tuple(_dot_nt(k_ref[rows, hh * LANES:(hh + 1) * LANES], q_ref[:, hh * LANES:(hh + 1) * LANES])
                         for hh in range(MLA_HG))

        def update(j, ss, state):
            rows = pl.ds(pl.multiple_of(j * MLA_T, MLA_T), MLA_T)
            out = []
            for hh in range(MLA_HG):
                (m, l, acc), s = state[hh], ss[hh]
                m_new = jnp.maximum(m, jnp.max(s, axis=0, keepdims=True))
                alpha = jnp.exp2(m - m_new)
                p = jnp.exp2(s - m_new)
                l = alpha * l + jnp.sum(p, axis=0, keepdims=True)
                pv = jnp.dot(v_ref[rows, hh * LANES:(hh + 1) * LANES].T, p.astype(BF16), preferred_element_type=F32)
                out.append((m_new, l, alpha * acc + pv))
            return tuple(out)

        def body(pair, state):
            j = 2 * pair
            s0, s1 = scores(j), scores(j + 1)
            return update(j + 1, s1, update(j, s0, state))

        init = tuple((jnp.full((1, MLA_T), NEG, F32), jnp.zeros((1, MLA_T), F32), jnp.zeros((LANES, MLA_T), F32))
                     for _ in range(MLA_HG))
        state = lax.fori_loop(0, i // 2, body, init)
        state = lax.cond(i % 2 == 1, lambda st: update(i - 1, scores(i - 1), st), lambda st: st, state)
        state = update(i, tuple(_causal_t(s) for s in scores(i)), state)
        for hh in range(MLA_HG):
            m, l, acc = state[hh]
            cols = slice(hh * LANES, (hh + 1) * LANES)
            o = (acc * (1.0 / l)).T
            oraw_ref[:, cols] = o.astype(BF16)
            g = _head_gate(gate_ref, hh)
            og_ref[:, cols] = (o * (g * jax.nn.sigmoid(g))).astype(BF16)
            lse_ref[0, 0, 0, hh:hh + 1, :] = m + jnp.log2(l)

    blk = lambda b, h, i: (b * nq + i, h)
    in_specs = [pl.BlockSpec((MLA_T, MLA_W), blk),
                pl.BlockSpec((S, MLA_W), lambda b, h, i: (b, h)),
                pl.BlockSpec((S, MLA_W), lambda b, h, i: (b, h)),
                pl.BlockSpec((MLA_T, MLA_W // 2), lambda b, h, i: (b * nq + i, Z_BGATE // (MLA_W // 2) + h))]
    out_specs = [pl.BlockSpec((MLA_T, MLA_W), blk), pl.BlockSpec((MLA_T, MLA_W), blk),
                 pl.BlockSpec((1, 1, 1, MLA_HG, MLA_T), lambda b, h, i: (b, h, i, 0, 0))]
    out_shape = [jax.ShapeDtypeStruct((T, HPAD), BF16), jax.ShapeDtypeStruct((T, HPAD), BF16),
                 jax.ShapeDtypeStruct((B, HEADS // MLA_HG, nq, MLA_HG, MLA_T), F32)]
    return pl.pallas_call(kern, name="mla_fwd", grid=(B, HEADS // MLA_HG, nq), in_specs=in_specs,
                          out_specs=out_specs, out_shape=out_shape,
                          compiler_params=_params(("parallel", "parallel", "arbitrary")))(q, k, v, z)


def _mla_bwd(q, k, v, do_raw, lse, delta, B, S):
    T = B * S
    nk = S // MLA_T

    def kern(q_ref, k_ref, v_ref, do_ref, lse_ref, delta_ref, dq_ref, dk_ref, dv_ref, dq_acc, dk_acc, dv_acc):
        j = pl.program_id(2)

        @pl.when(j == 0)
        def _():
            dq_acc[...] = jnp.zeros_like(dq_acc)

        dk_acc[...] = jnp.zeros_like(dk_acc)
        dv_acc[...] = jnp.zeros_like(dv_acc)
        kts = [k_ref[:, hh * LANES:(hh + 1) * LANES].T for hh in range(MLA_HGB)]

        def step(i, masked):
            rows = pl.ds(pl.multiple_of(i * MLA_T, MLA_T), MLA_T)
            for hh in range(MLA_HGB):
                cols = slice(hh * LANES, (hh + 1) * LANES)
                qv, do = q_ref[rows, cols], do_ref[rows, cols]
                st = _dot_nt(k_ref[:, cols], qv)
                if masked:
                    st = _causal_t(st)
                pt = jnp.exp2(st - lse_ref[0, 0, i, hh:hh + 1, :])
                dpt = _dot_nt(v_ref[:, cols], do)
                dst = (pt * (dpt - delta_ref[0, 0, i, hh:hh + 1, :])).astype(BF16)
                dv_acc[:, cols] += jnp.dot(pt.astype(BF16), do, preferred_element_type=F32)
                dk_acc[:, cols] += jnp.dot(dst, qv, preferred_element_type=F32)
                dq_acc[hh, i] += jnp.dot(kts[hh], dst, preferred_element_type=F32)

        step(j, True)

        def body(i, c):
            step(i, False)
            return c

        lax.fori_loop(j + 1, nk, body, 0)
        dk_ref[...] = (dk_acc[...] * (1.0 / LOG2E)).astype(BF16)
        dv_ref[...] = dv_acc[...].astype(BF16)

        @pl.when(j == nk - 1)
        def _():
            for hh in range(MLA_HGB):
                for t in range(nk):
                    dq_ref[t * MLA_T:(t + 1) * MLA_T, hh * LANES:(hh + 1) * LANES] = dq_acc[hh, t].T.astype(BF16)

    whole = lambda b, h, j: (b, h)
    tile = lambda b, h, j: (b * nk + j, h)
    stats = pl.BlockSpec((1, 1, nk, MLA_HGB, MLA_T), lambda b, h, j: (b, h, 0, 0, 0))
    in_specs = [pl.BlockSpec((S, MLA_WB), whole), pl.BlockSpec((MLA_T, MLA_WB), tile),
                pl.BlockSpec((MLA_T, MLA_WB), tile), pl.BlockSpec((S, MLA_WB), whole), stats, stats]
    out_specs = [pl.BlockSpec((S, MLA_WB), whole), pl.BlockSpec((MLA_T, MLA_WB), tile),
                 pl.BlockSpec((MLA_T, MLA_WB), tile)]
    out_shape = [jax.ShapeDtypeStruct((T, HPAD), BF16)] * 3
    scratch = [pltpu.VMEM((MLA_HGB, nk, LANES, MLA_T), F32), pltpu.VMEM((MLA_T, MLA_WB), F32),
               pltpu.VMEM((MLA_T, MLA_WB), F32)]
    return pl.pallas_call(kern, name="mla_bwd", grid=(B, HEADS // MLA_HGB, nk), in_specs=in_specs,
                          out_specs=out_specs, out_shape=out_shape, scratch_shapes=scratch,
                          compiler_params=_params(("parallel", "parallel", "arbitrary")))(
        q, k, v, do_raw, lse, delta)


def _rope_tables(pos_col, inv_lane, rows):
    def body(ins, outs, _):
        ang = ins[0][...].astype(F32) * ins[1][...]
        lane = lax.broadcasted_iota(jnp.int32, ang.shape, 1)
        cos, sin = jnp.cos(ang), jnp.sin(ang)
        first = (lane >= HEAD_DIM) & (lane < HEAD_DIM + MLA_ROPE // 2)
        second = (lane >= HEAD_DIM + MLA_ROPE // 2) & (lane < MLA_QK)
        outs[0][...] = jnp.where(lane < HEAD_DIM, 1.0, jnp.where(lane < MLA_QK, cos, 0.0))
        outs[1][...] = jnp.where(first, -sin, 0.0)
        outs[2][...] = jnp.where(second, sin, 0.0)
    return _ew("rope_tables", body, [(pos_col, 1, 0), (inv_lane, None, None)], [(LANES, F32)] * 3, rows)


def _rope(x, c, s1, s2):
    return x * c + pltpu.roll(x, 112, 1) * s1 + pltpu.roll(x, 16, 1) * s2


def _rope_t(d, c, s1, s2):
    return d * c + pltpu.roll(d * s1, 16, 1) + pltpu.roll(d * s2, 112, 1)


def _mla_prep(qdn, w_uq, kvdn, w_ukv, z, tabs, rows):
    def body(ins, outs, _):
        q_pre, kv_pre = ins[0], ins[1]
        c, s1, s2 = ins[3][...], ins[4][...], ins[5][...]
        kr = _rope(ins[2][...].astype(F32), c, s1, s2)
        for h in range(HEADS):
            cols = slice(h * LANES, (h + 1) * LANES)
            outs[0][:, cols] = (_rope(q_pre[:, cols], c, s1, s2) * MLA_QSCALE).astype(BF16)
            outs[1][:, cols] = (kv_pre[:, cols] + kr).astype(BF16)
        outs[2][...] = kv_pre[:, HPAD:].astype(BF16)
    ins = [(z, LANES, Z_BKR // LANES), (tabs[0], LANES, 0), (tabs[1], LANES, 0), (tabs[2], LANES, 0)]
    return _ew("mla_prep", body, ins, [(HPAD, BF16)] * 3, rows, mms=[(qdn, w_uq), (kvdn, w_ukv)])


def _mla_prep_bwd(dq, dk, dv, tabs, dz, rows):
    def body(ins, outs, _):
        c, s1, s2 = ins[3][...], ins[4][...], ins[5][...]
        lane = lax.broadcasted_iota(jnp.int32, c.shape, 1)
        dkr = jnp.zeros(c.shape, F32)
        for h in range(HEADS):
            cols = slice(h * LANES, (h + 1) * LANES)
            outs[0][:, cols] = _rope_t(ins[0][:, cols].astype(F32) * MLA_SCALE, c, s1, s2).astype(BF16)
            dkh = ins[1][:, cols].astype(F32)
            outs[1][:, cols] = jnp.where(lane < HEAD_DIM, dkh, 0.0).astype(BF16)
            dkr = dkr + dkh
        outs[1][:, HPAD:] = ins[2][...].astype(BF16)
        live = (lane >= HEAD_DIM) & (lane < MLA_QK)
        outs[2][...] = jnp.where(live, _rope_t(jnp.where(live, dkr, 0.0), c, s1, s2), 0.0).astype(BF16)
    ins = [(dq, HPAD, 0), (dk, HPAD, 0), (dv, HPAD, 0), (tabs[0], LANES, 0), (tabs[1], LANES, 0),
           (tabs[2], LANES, 0)]
    outs = [(HPAD, BF16), (2 * HPAD, BF16), (LANES, BF16, dz, Z_BKR // LANES)]
    return _ew("mla_prep_bwd", body, ins, outs, rows)


def _gate_bwd(name, d_o_mm, o_raw, gate, gate_cb, dz, dz_cb, rows):
    def body(ins, outs, _):
        lane = lax.broadcasted_iota(jnp.int32, outs[2].shape, 1)
        delta = jnp.zeros(outs[2].shape, F32)
        d_gate = [None] * HEADS
        for h in range(HEADS):
            cols = slice(h * LANES, (h + 1) * LANES)
            dog, o, g = ins[0][:, cols], ins[1][:, cols].astype(F32), _head_gate(ins[2], h)
            sg = jax.nn.sigmoid(g)
            do = dog * (g * sg)
            outs[0][:, cols] = do.astype(BF16)
            d_gate[h] = dog * o * (sg * (1.0 + g * (1.0 - sg)))
            delta = jnp.where(lane == h, jnp.sum(do * o, axis=-1, keepdims=True), delta)
        for pair in range(HEADS // 2):
            packed = d_gate[2 * pair] + pltpu.roll(d_gate[2 * pair + 1], HEAD_DIM, 1)
            outs[1][:, pair * LANES:(pair + 1) * LANES] = packed.astype(BF16)
        outs[2][...] = delta
    ins = [(o_raw, HPAD, 0), (gate, GATE_W, gate_cb)]
    outs = [(HPAD, BF16), (GATE_W, BF16, dz, dz_cb), (LANES, F32)]
    return _ew(name, body, ins, outs, rows, mms=[d_o_mm])


def _merge_out(oa, ob, w_br_a, w_br_b, z, w_out, x0, g_next, rows):
    tm = _row_tile(rows)

    def kern(oa_ref, ob_ref, wa_ref, wb_ref, ma_ref, mb_ref, w_ref, x0_ref, g_ref,
             ua_ref, ub_ref, y_ref, x1_ref, hn_ref):
        oa_v, ob_v = oa_ref[...], ob_ref[...]
        for j in range(D_MODEL // MM_TN):
            cols = slice(j * MM_TN, (j + 1) * MM_TN)
            ua = jnp.dot(oa_v, wa_ref[:, cols], preferred_element_type=F32)
            ub = jnp.dot(ob_v, wb_ref[:, cols], preferred_element_type=F32)
            ua_ref[:, cols] = ua.astype(BF16)
            ub_ref[:, cols] = ub.astype(BF16)
            m_a, m_b = ma_ref[:, cols].astype(F32), mb_ref[:, cols].astype(F32)
            y_ref[:, cols] = (jax.nn.sigmoid(m_a) * ua + jax.nn.sigmoid(m_b) * ub).astype(BF16)
        y = y_ref[...]
        for j in range(D_MODEL // MM_TN):
            cols = slice(j * MM_TN, (j + 1) * MM_TN)
            x1_ref[:, cols] = jnp.dot(y, w_ref[:, cols], preferred_element_type=F32) + x0_ref[:, cols]
        hn_ref[...] = _rms(x1_ref[...], g_ref[...])

    row = lambda cb: pl.BlockSpec((tm, D_MODEL), lambda i: (i, cb))
    whole = lambda a: pl.BlockSpec(a.shape, lambda i: (0, 0))
    bf16, f32 = jax.ShapeDtypeStruct((rows, D_MODEL), BF16), jax.ShapeDtypeStruct((rows, D_MODEL), F32)
    return pl.pallas_call(
        kern, name="merge_out", grid=(rows // tm,),
        in_specs=[row(0), row(0), whole(w_br_a), whole(w_br_b), row(Z_MA // D_MODEL), row(Z_MB // D_MODEL),
                  whole(w_out), row(0), pl.BlockSpec((1, D_MODEL), lambda i: (0, 0))],
        out_specs=[row(0)] * 5, out_shape=[bf16, bf16, bf16, f32, bf16],
        compiler_params=_params(("parallel",)))(oa, ob, w_br_a, w_br_b, z, z, w_out, x0, g_next.reshape(1, D_MODEL))


def _merge_bwd(dy_mm, ua, ub, z, dz, rows):
    def body(ins, outs, _):
        dyv = ins[0][...]
        for idx in range(2):
            s = jax.nn.sigmoid(ins[3 + idx][...].astype(F32))
            outs[idx][...] = (dyv * s).astype(BF16)
            d_m = (dyv * ins[1 + idx][...].astype(F32) * (s * (1.0 - s))).astype(BF16)
            outs[2][:, idx * D_MODEL:(idx + 1) * D_MODEL] = d_m
    ins = [(ua, D_MODEL, 0), (ub, D_MODEL, 0), (z, D_MODEL, Z_MA // D_MODEL), (z, D_MODEL, Z_MB // D_MODEL)]
    outs = [(D_MODEL, BF16), (D_MODEL, BF16), (2 * D_MODEL, BF16, dz, Z_MA // (2 * D_MODEL))]
    return _ew("merge_bwd", body, ins, outs, rows, mms=[dy_mm])


def _kv_grad_cast(dk, dv, dz, rows):
    def body(ins, outs, _):
        outs[0][:, :KV_W] = ins[0][...].astype(BF16)
        outs[0][:, KV_W:] = ins[1][...].astype(BF16)
    outs = [(2 * KV_W, BF16, dz, Z_AK // (2 * KV_W))]
    return _ew("kv_grad_cast", body, [(dk, KV_W, 0), (dv, KV_W, 0)], outs, rows)[0]


def _ple_fwd(x1, hn, w_pg, p, w_pp, g_next, rows):
    def body(ins, outs, _):
        u, e = ins[0][...], ins[1][...]
        x2 = ins[2][...] + jax.nn.sigmoid(u) * e
        outs[0][...] = x2
        outs[1][...] = u.astype(BF16)
        outs[2][...] = e.astype(BF16)
        if g_next is not None:
            outs[3][...] = _rms(x2, ins[3][...])
    ins = [(x1, D_MODEL, 0)] + ([(g_next.reshape(1, D_MODEL), None, None)] if g_next is not None else [])
    outs = [(D_MODEL, F32), (D_MODEL, BF16), (D_MODEL, BF16)] + ([(D_MODEL, BF16)] if g_next is not None else [])
    return _ew("ple_fwd", body, ins, outs, rows, mms=[(hn, w_pg), (p, w_pp)])


def _ple_bwd(dx2, u, e, rows):
    def body(ins, outs, _):
        d, s = ins[0][...], jax.nn.sigmoid(ins[1][...].astype(F32))
        outs[0][...] = (d * s).astype(BF16)
        outs[1][...] = (d * ins[2][...].astype(F32) * (s * (1.0 - s))).astype(BF16)
    return _ew("ple_bwd", body, [(dx2, D_MODEL, 0), (u, D_MODEL, 0), (e, D_MODEL, 0)],
               [(D_MODEL, BF16)] * 2, rows)


def _loss_head(x, g, target, rows):
    def body(ins, outs, accs):
        xv, gv = ins[0][...], ins[1][...]
        r = lax.rsqrt(jnp.mean(xv * xv, axis=-1, keepdims=True) + EPS)
        xhat = xv * r
        err = xhat * gv - ins[2][...]
        accs[0][...] += jnp.broadcast_to(0.5 * jnp.sum(jnp.mean(err * err, axis=-1, keepdims=True),
                                                       axis=0, keepdims=True), (1, LANES))
        dyv = err * (1.0 / D_MODEL)
        accs[1][...] += jnp.sum(dyv * xhat, axis=0, keepdims=True)
        dy = dyv * gv
        outs[0][...] = r * (dy - xhat * jnp.mean(dy * xhat, axis=-1, keepdims=True))
    ins = [(x, D_MODEL, 0), (g.reshape(1, D_MODEL), None, None), (target, D_MODEL, 0)]
    return _ew("loss_head", body, ins, [(D_MODEL, F32)], rows, accs=[(1, LANES), (1, D_MODEL)])


def _pad_heads_cols(w, n_heads, dim):
    k = w.shape[0]
    return jnp.pad(w.reshape(k, n_heads, dim), ((0, 0), (0, 0), (0, LANES - dim))).reshape(k, n_heads * LANES)


def _unpad_heads_cols(w, n_heads, dim):
    k = w.shape[0]
    return w.reshape(k, n_heads, LANES)[:, :, :dim].reshape(k, n_heads * dim)


def _layer_weights(w, i):
    segs = jnp.split(w['w_in'][i], list(_cumsum(IN_SIZES))[:-1], axis=1)
    a_q, a_k, a_v, a_gate, b_qd, b_kvd, b_kr, b_gate, m_a, m_b = segs
    kr = jnp.pad(b_kr, ((0, 0), (HEAD_DIM, LANES - MLA_QK)))
    w_in = jnp.concatenate([
        m_a, m_b, _pad_heads_cols(a_q, HEADS, HEAD_DIM), a_gate, b_gate, _pad_heads_cols(a_k, SWA_KV_HEADS, HEAD_DIM),
        _pad_heads_cols(a_v, SWA_KV_HEADS, HEAD_DIM), b_qd, b_kvd, kr], axis=1)
    w_uq = _pad_heads_cols(w['w_uq'][i], HEADS, MLA_QK)
    ukv = w['w_ukv'][i].reshape(MLA_KV_LORA, HEADS, 2 * HEAD_DIM)
    pad = ((0, 0), (0, 0), (0, HEAD_DIM))
    w_ukv = jnp.concatenate([jnp.pad(ukv[:, :, :HEAD_DIM], pad).reshape(MLA_KV_LORA, HPAD),
                             jnp.pad(ukv[:, :, HEAD_DIM:], pad).reshape(MLA_KV_LORA, HPAD)], axis=1)
    w_br_a = _pad_heads_cols(w['w_br_a'][i].T, HEADS, HEAD_DIM).T
    w_br_b = _pad_heads_cols(w['w_br_b'][i].T, HEADS, HEAD_DIM).T
    out = dict(w_in=w_in, w_uq=w_uq, w_ukv=w_ukv, w_br_a=w_br_a, w_br_b=w_br_b, w_out=w['w_out'][i],
               w_pg=w['w_ple_gate'][i], w_pp=w['w_ple_proj'][i])
    for name in ('w_in', 'w_uq', 'w_ukv', 'w_br_a', 'w_br_b', 'w_out', 'w_pg'):
        out[name + '_t'] = out[name].T
    return out


def _cumsum(sizes):
    acc, out = 0, []
    for s in sizes:
        acc += s
        out.append(acc)
    return out


def _unpad_grads(g):
    d = g['w_in']
    seg = lambda off, width: d[:, off:off + width]
    b_kr = seg(Z_BKR, LANES)[:, HEAD_DIM:MLA_QK]
    w_in = jnp.concatenate([
        _unpad_heads_cols(seg(Z_AQ, HPAD), HEADS, HEAD_DIM), _unpad_heads_cols(seg(Z_AK, KV_W), SWA_KV_HEADS, HEAD_DIM),
        _unpad_heads_cols(seg(Z_AV, KV_W), SWA_KV_HEADS, HEAD_DIM), seg(Z_AGATE, GATE_W),
        seg(Z_BQD, MLA_Q_LORA), seg(Z_BKVD, MLA_KV_LORA), b_kr, seg(Z_BGATE, GATE_W),
        seg(Z_MA, D_MODEL), seg(Z_MB, D_MODEL)], axis=1)
    w_uq = _unpad_heads_cols(g['w_uq'], HEADS, MLA_QK)
    ukv = g['w_ukv'].reshape(MLA_KV_LORA, 2, HEADS, LANES)[:, :, :, :HEAD_DIM]
    w_ukv = jnp.concatenate([ukv[:, 0], ukv[:, 1]], axis=-1).reshape(MLA_KV_LORA, HEADS * 2 * HEAD_DIM)
    w_br_a = _unpad_heads_cols(g['w_br_a'].T, HEADS, HEAD_DIM).T
    w_br_b = _unpad_heads_cols(g['w_br_b'].T, HEADS, HEAD_DIM).T
    return dict(w_in=w_in, w_uq=w_uq, w_ukv=w_ukv, w_br_a=w_br_a, w_br_b=w_br_b, w_out=g['w_out'],
                w_ple_gate=g['w_pg'], w_ple_proj=g['w_pp'], g_mix=g['g_mix'], sink=g['sink'], g_q=g['g_q'],
                g_kv=g['g_kv'], g_ple=g['g_ple'])


def _layer_fwd(x0, h, p_i, lw, sm, i, pos_col, pos_row, tabs, B, S):
    T = B * S
    z, a_gate, qdn, kvdn = _mm("proj_in", h, lw['w_in'], BF16, f32_cols=(Z_AGATE, GATE_W),
                               norms=[(Z_BQD, sm['g_q'][i]), (Z_BKVD, sm['g_kv'][i])])
    sink_row = jnp.pad(sm['sink'][i], (0, LANES - HEADS)).reshape(1, LANES)
    oa_raw, oa, lse_a = _swa_fwd(z, a_gate, pos_col, pos_row, sink_row, B, S)
    qf, kf, vf = _mla_prep(qdn, lw['w_uq'], kvdn, lw['w_ukv'], z, tabs, T)
    ob_raw, ob, lse_b = _mla_fwd(qf, kf, vf, z, B, S)
    ua, ub, y, x1, hn = _merge_out(oa, ob, lw['w_br_a'], lw['w_br_b'], z, lw['w_out'], x0, sm['g_ple'][i], T)
    g_next = sm['g_mix'][i + 1] if i + 1 < DEPTH else None
    x2, u, e, *h_next = _ple_fwd(x1, hn, lw['w_pg'], p_i, lw['w_pp'], g_next, T)
    saved = dict(x0=x0, h=h, z=z, a_gate=a_gate, sink_row=sink_row, oa_raw=oa_raw, oa=oa, lse_a=lse_a, qdn=qdn, kvdn=kvdn,
                 qf=qf, kf=kf, vf=vf, ob_raw=ob_raw, ob=ob, lse_b=lse_b, ua=ua, ub=ub, y=y, x1=x1, hn=hn,
                 u=u, e=e, p=p_i)
    return x2, (h_next[0] if h_next else None), saved


def _layer_bwd(dx2, sv, lw, sm, i, pos_col, pos_row, tabs, B, S):
    T = B * S
    z = sv['z']
    g = {}
    d_e, d_u = _ple_bwd(dx2, sv['u'], sv['e'], T)
    g['w_pp'] = _mm_tn("grad_pp", sv['p'], d_e)
    g['w_pg'] = _mm_tn("grad_pg", sv['hn'], d_u)
    dx1, g['g_ple'] = _rms_bwd("norm_ple_bwd", sv['x1'], D_MODEL, 0, sm['g_ple'][i], (d_u, lw['w_pg_t']), T, F32,
                               dres=dx2)
    g['w_out'] = _mm_tn("grad_out", sv['y'], dx1)
    dz = lax.empty((T, Z_WIDTH), BF16)
    d_ua, d_ub, dz = _merge_bwd((dx1, lw['w_out_t']), sv['ua'], sv['ub'], z, dz, T)
    g['w_br_a'] = _mm_tn("grad_br_a", sv['oa'], d_ua)
    g['w_br_b'] = _mm_tn("grad_br_b", sv['ob'], d_ub)
    dob_raw, dz, delta_b = _gate_bwd("gate_b_bwd", (d_ub, lw['w_br_b_t']), sv['ob_raw'], z, Z_BGATE // GATE_W,
                                     dz, Z_BGATE // GATE_W, T)
    nq, groups = S // MLA_T, HEADS // MLA_HGB
    delta_rows = delta_b[:, :HEADS].reshape(B, nq, MLA_T, groups, MLA_HGB).transpose(0, 3, 1, 4, 2)
    lse_rows = sv['lse_b'].transpose(0, 2, 1, 3, 4).reshape(B, nq, groups, MLA_HGB, MLA_T).transpose(0, 2, 1, 3, 4)
    dq, dk, dv = _mla_bwd(sv['qf'], sv['kf'], sv['vf'], dob_raw, lse_rows, delta_rows, B, S)
    dq_pre, dkv_pre, dz = _mla_prep_bwd(dq, dk, dv, tabs, dz, T)
    g['w_uq'] = _mm_tn("grad_uq", sv['qdn'], dq_pre)
    g['w_ukv'] = _mm_tn("grad_ukv", sv['kvdn'], dkv_pre)
    dz, g['g_q'] = _rms_bwd("norm_q_bwd", z, MLA_Q_LORA, Z_BQD // MLA_Q_LORA, sm['g_q'][i],
                            (dq_pre, lw['w_uq_t']), T, BF16, into=(dz, Z_BQD // MLA_Q_LORA))
    dz, g['g_kv'] = _rms_bwd("norm_kv_bwd", z, MLA_KV_LORA, Z_BKVD // MLA_KV_LORA, sm['g_kv'][i],
                             (dkv_pre, lw['w_ukv_t']), T, BF16, into=(dz, Z_BKVD // MLA_KV_LORA))
    doa_raw, dz, delta_a = _gate_bwd("gate_a_bwd", (d_ua, lw['w_br_a_t']), sv['oa_raw'], sv['a_gate'], 0,
                                     dz, Z_AGATE // GATE_W, T)
    dz, d_ak, d_av, dsink = _swa_bwd(z, pos_col, pos_row, sv['sink_row'], sv['lse_a'], doa_raw, delta_a, dz, B, S)
    dz = _kv_grad_cast(d_ak, d_av, dz, T)
    g['sink'] = dsink[0, :HEADS]
    g['w_in'] = _mm_tn("grad_in", sv['h'], dz, tk=1024, tn=Z_WIDTH // 2)
    dx0, g['g_mix'] = _rms_bwd("norm_mix_bwd", sv['x0'], D_MODEL, 0, sm['g_mix'][i], (dz, lw['w_in_t']), T, F32,
                               dres=dx1)
    for name in ('g_ple', 'g_q', 'g_kv', 'g_mix'):
        g[name] = g[name][0]
    return dx0, g


def _local_step(x, p, positions, wfull, sm, loss_target):
    B, S, _ = x.shape
    T = B * S
    pos_col = positions.reshape(T, 1)
    pos_row = positions.reshape(T // BLOCK, 1, BLOCK)
    half = MLA_ROPE // 2
    inv = ROPE_THETA ** (-jnp.arange(0, MLA_ROPE, 2, dtype=F32) / MLA_ROPE)
    inv_lane = jnp.tile(inv, LANES // half).reshape(1, LANES)
    tabs = _rope_tables(pos_col, inv_lane, T)
    xc = x.reshape(T, D_MODEL)
    h = _rms_fwd("norm_mix", xc, D_MODEL, 0, sm['g_mix'][0], T)
    lws, saved = [], []
    for i in range(DEPTH):
        lw = _layer_weights(wfull, i)
        xc, h, sv = _layer_fwd(xc, h, p[i].reshape(T, PLE_DIM), lw, sm, i, pos_col, pos_row, tabs, B, S)
        lws.append(lw)
        saved.append(sv)
    dx, loss, dg_final = _loss_head(xc, sm['g_final'], loss_target.reshape(T, D_MODEL), T)
    layer_grads = [None] * DEPTH
    for i in reversed(range(DEPTH)):
        dx, g = _layer_bwd(dx, saved[i], lws[i], sm, i, pos_col, pos_row, tabs, B, S)
        layer_grads[i] = _unpad_grads(g)
    return loss, dx.reshape(B, S, D_MODEL), layer_grads, dg_final[0]


SMALL_ROWS = 48


SMALL_SIZE = 2 * (2 * D_MODEL + HEADS + MLA_Q_LORA + MLA_KV_LORA) + D_MODEL


def _pack_small(arrs, tail=()):
    flat = jnp.concatenate([arrs[name].reshape(-1) for name in SMALL] + [t.reshape(1) for t in tail])
    return jnp.pad(flat, (0, SMALL_ROWS * LANES - flat.shape[0])).reshape(SMALL_ROWS, LANES)


def _unpack_small(block, shapes):
    flat = block.reshape(-1)
    out, off = {}, 0
    for name in SMALL:
        n = math.prod(shapes[name])
        out[name] = flat[off:off + n].reshape(shapes[name])
        off += n
    return out


def _flipped(shard_shape):
    return shard_shape[-1] % LANES != 0


def _to_slots(g, axis):
    r, c = g.shape
    if axis == 0:
        return g.reshape(N_CHIPS, r // N_CHIPS, c)
    return g.reshape(r, N_CHIPS, c // N_CHIPS).transpose(1, 0, 2)


def _div_tile(rows, cap):
    return next(t for t in range(min(cap, rows) // 8 * 8, 0, -8) if rows % t == 0)


def _units(shapes):
    units = []
    for w, shape in enumerate(shapes):
        r = shape[-2]
        n = next((n for n in (8, 4, 2) if r % (16 * n) == 0), 1) if r >= 512 else 1
        units += [(w, k * (r // n), r // n) for k in range(n)]
    return units


def _place():
    x, y, c = lax.axis_index("x"), lax.axis_index("y"), lax.axis_index("c")
    chips = [(1 - x, y), (x, 1 - y), (1 - x, 1 - y)]
    return x, y, c, chips


ANY = pl.BlockSpec(memory_space=pl.ANY)


def _remote(send_sems, recv_sems, k, src, dst, to):
    return pltpu.make_async_remote_copy(src_ref=src, dst_ref=dst, send_sem=send_sems.at[k],
                                        recv_sem=recv_sems.at[k], device_id=to, device_id_type=MESH)


def _gather_weights(shards, carried):
    n, nc = len(shards), len(carried)
    units = _units([s.shape for s in shards])
    nu = len(units)

    def body(*refs):
        ins, outs = refs[:n], refs[n + nc:2 * n + nc]
        send_sems, recv_sems, local_sems = refs[2 * (n + nc):]
        x, y, c, chips = _place()
        me = 2 * x + y
        sibling = (x, y, 1 - c)
        copy = functools.partial(_remote, send_sems, recv_sems)
        keeps, sends = [], []
        for u, (w, r0, nr) in enumerate(units):
            rows = pl.ds(r0, nr)
            keeps.append(pltpu.make_async_copy(ins[w].at[:, rows, :], outs[w].at[me, :, rows, :], local_sems.at[u]))
            keeps[-1].start()
        for j, (cx, cy) in enumerate(chips):
            for u, (w, r0, nr) in enumerate(units):
                rows = pl.ds(r0, nr)
                sends.append(copy(j * nu + u, ins[w].at[c, rows, :], outs[w].at[me, c, rows, :], (cx, cy, c)))
                sends[-1].start()
        for j, (cx, cy) in enumerate(chips):
            for u, (w, r0, nr) in enumerate(units):
                landed = outs[w].at[2 * cx + cy, c, pl.ds(r0, nr), :]
                copy(j * nu + u, landed, landed, (cx, cy, c)).wait_recv()
                sends.append(copy((3 + j) * nu + u, landed, landed, sibling))
                sends[-1].start()
        for j, (cx, cy) in enumerate(chips):
            for u, (w, r0, nr) in enumerate(units):
                other = outs[w].at[2 * cx + cy, 1 - c, pl.ds(r0, nr), :]
                copy((3 + j) * nu + u, other, other, sibling).wait_recv()
        for cp in sends:
            cp.wait_send()
        for keep in keeps:
            keep.wait()

    out_shape = [jax.ShapeDtypeStruct((N_CHIPS,) + s.shape, s.dtype) for s in shards]
    out_shape += [jax.ShapeDtypeStruct(a.shape, a.dtype) for a in carried]
    res = pl.pallas_call(
        body, name="gather_weights", out_shape=out_shape,
        in_specs=[ANY] * (n + nc), out_specs=[ANY] * (n + nc),
        input_output_aliases={n + k: n + k for k in range(nc)},
        scratch_shapes=[pltpu.SemaphoreType.DMA((6 * nu,)), pltpu.SemaphoreType.DMA((6 * nu,)),
                        pltpu.SemaphoreType.DMA((nu,))])(*shards, *carried)
    return res[:n], res[n:]


def _pair_exchange(g0, g1):
    n = len(g0)

    def body(*refs):
        layers, outs = (refs[:n], refs[n:2 * n]), refs[2 * n:3 * n]
        send_sems, recv_sems = refs[3 * n:]
        x, y, c, _ = _place()
        copy = functools.partial(_remote, send_sems, recv_sems)
        for w in range(n):
            for q in range(N_CHIPS):
                for layer in range(DEPTH):
                    cp = copy(N_CHIPS * w + q, layers[layer][w].at[q], outs[w].at[q], (x, y, 1 - c))
                    pl.when(c == 1 - layer)(cp.start)
        for w in range(n):
            for q in range(N_CHIPS):
                copy(N_CHIPS * w + q, layers[0][w].at[q], outs[w].at[q], (x, y, 1 - c)).wait()

    return pl.pallas_call(
        body, name="pair_exchange", out_shape=[jax.ShapeDtypeStruct(g.shape, g.dtype) for g in g0],
        in_specs=[ANY] * (2 * n), out_specs=[ANY] * n,
        scratch_shapes=[pltpu.SemaphoreType.DMA((N_CHIPS * n,)), pltpu.SemaphoreType.DMA((N_CHIPS * n,))])(*g0, *g1)


def _pair_sum(name, g0, g1, theirs, cflag):
    shape = theirs.shape
    rows, width = shape[0] * shape[1], shape[2]

    def body(ins, outs, _):
        mine = jnp.where(ins[3][0:1, 0:1] == 0.0, ins[0][...], ins[1][...])
        tot = mine + ins[2][...]
        outs[0][...] = tot
        outs[1][...] = tot.astype(BF16)
    ins = [(a.reshape(rows, width), width, 0) for a in (g0, g1, theirs)] + [(cflag, None, None)]
    f32, bf16 = _ew(name, body, ins, [(width, F32), (width, BF16)], rows, tm=_div_tile(rows, ROW_TILE))
    return f32.reshape(shape), bf16.reshape(shape)


def _chip_exchange(parts):
    n = len(parts)

    def body(*refs):
        ins, outs = refs[:n], refs[n:2 * n]
        send_sems, recv_sems = refs[2 * n:]
        x, y, c, chips = _place()
        copy = functools.partial(_remote, send_sems, recv_sems)
        sends = []
        for j, (cx, cy) in enumerate(chips):
            for w in range(n):
                sends.append(copy(j * n + w, ins[w].at[2 * cx + cy], outs[w].at[j], (cx, cy, c)))
                sends[-1].start()
        for j, (cx, cy) in enumerate(chips):
            for w in range(n):
                copy(j * n + w, outs[w].at[j], outs[w].at[j], (cx, cy, c)).wait_recv()
        for cp in sends:
            cp.wait_send()

    return pl.pallas_call(
        body, name="chip_exchange",
        out_shape=[jax.ShapeDtypeStruct((3,) + a.shape[1:], a.dtype) for a in parts],
        in_specs=[ANY] * n, out_specs=[ANY] * n,
        scratch_shapes=[pltpu.SemaphoreType.DMA((3 * n,)), pltpu.SemaphoreType.DMA((3 * n,))])(*parts)


def _chip_sum(name, part, landed, chipflag):
    _, r, width = part.shape
    tm = _div_tile(r, ROW_TILE // 2)

    def kern(p_ref, l_ref, flag_ref, o_ref):
        me = flag_ref[0:1, 0:1]
        own = jnp.where(me == 0.0, p_ref[0], jnp.where(me == 1.0, p_ref[1], jnp.where(me == 2.0, p_ref[2], p_ref[3])))
        o_ref[...] = ((own + l_ref[0].astype(F32)) + l_ref[1].astype(F32)) + l_ref[2].astype(F32)

    return pl.pallas_call(
        kern, name=name, grid=(r // tm,),
        in_specs=[pl.BlockSpec((N_CHIPS, tm, width), lambda i: (0, i, 0)),
                  pl.BlockSpec((3, tm, width), lambda i: (0, i, 0)),
                  pl.BlockSpec((1, LANES), lambda i: (0, 0))],
        out_specs=pl.BlockSpec((tm, width), lambda i: (i, 0)),
        out_shape=jax.ShapeDtypeStruct((r, width), F32), compiler_params=_params(("arbitrary",)))(part, landed, chipflag)


def _pair_broadcast(mine):
    n = len(mine)
    units = _units([a.shape for a in mine])

    def body(*refs):
        ins, outs = refs[:n], refs[n:2 * n]
        send_sems, recv_sems = refs[2 * n:]
        x, y, c, _ = _place()
        copy = functools.partial(_remote, send_sems, recv_sems)
        cps = [copy(u, ins[w].at[pl.ds(r0, nr), :], outs[w].at[pl.ds(r0, nr), :], (x, y, 1 - c))
               for u, (w, r0, nr) in enumerate(units)]
        for cp in cps:
            cp.start()
        for cp in cps:
            cp.wait()

    return pl.pallas_call(
        body, name="pair_broadcast", out_shape=[jax.ShapeDtypeStruct(a.shape, a.dtype) for a in mine],
        in_specs=[ANY] * n, out_specs=[ANY] * n,
        scratch_shapes=[pltpu.SemaphoreType.DMA((len(units),)), pltpu.SemaphoreType.DMA((len(units),))])(*mine)


def _small_allreduce(v):
    offsets = [(dx, dy, dc) for dx in (0, 1) for dy in (0, 1) for dc in (0, 1)][1:]

    def body(v_ref, out_ref, recv_ref, send_sems, recv_sems):
        x, y, c, _ = _place()
        flip = lambda a, d: 1 - a if d else a
        peers = [(flip(x, dx), flip(y, dy), flip(c, dc)) for dx, dy, dc in offsets]
        copy = functools.partial(_remote, send_sems, recv_sems)
        me = 4 * x + 2 * y + c
        recv_ref[me] = v_ref[...]
        cps = [copy(k, v_ref, recv_ref.at[me], peer) for k, peer in enumerate(peers)]
        for cp in cps:
            cp.start()
        for k, (px, py, pc) in enumerate(peers):
            landed = recv_ref.at[4 * px + 2 * py + pc]
            copy(k, landed, landed, (px, py, pc)).wait_recv()
        for cp in cps:
            cp.wait_send()
        tot = recv_ref[0]
        for d in range(1, 8):
            tot = tot + recv_ref[d]
        out_ref[...] = tot

    vmem = pl.BlockSpec(memory_space=pltpu.VMEM)
    return pl.pallas_call(
        body, name="small_allreduce", out_shape=jax.ShapeDtypeStruct(v.shape, v.dtype),
        in_specs=[vmem], out_specs=vmem,
        scratch_shapes=[pltpu.VMEM((8,) + v.shape, v.dtype), pltpu.SemaphoreType.DMA((7,)),
                        pltpu.SemaphoreType.DMA((7,))])(v)


def _adam_math(gv, wv, mv, vv):
    mv = ADAM_B1 * mv + (1.0 - ADAM_B1) * gv
    vv = ADAM_B2 * vv + (1.0 - ADAM_B2) * (gv * gv)
    m_hat = mv / (1.0 - ADAM_B1 ** ADAM_STEP)
    v_hat = vv / (1.0 - ADAM_B2 ** ADAM_STEP)
    return -ADAM_LR * (m_hat / (jnp.sqrt(v_hat) + ADAM_EPS) + ADAM_WD * wv), mv, vv


def _adamw_big(name, mine, theirs, cflag, w, m, v):
    _, r, width = w.shape
    tm = _div_tile(r, ROW_TILE // 2)

    def kern(mine_ref, theirs_ref, flag_ref, w_ref, m_ref, v_ref, g_ref, d_ref, nm_ref, nv_ref):
        layer = pl.program_id(0).astype(F32)
        gv = jnp.where(flag_ref[0:1, 0:1] == layer, mine_ref[...], theirs_ref[...])
        g_ref[0] = gv
        d_ref[0], nm_ref[0], nv_ref[0] = _adam_math(gv, w_ref[0], m_ref[0], v_ref[0])

    flat = pl.BlockSpec((tm, width), lambda l, i: (i, 0))
    stacked = pl.BlockSpec((1, tm, width), lambda l, i: (l, i, 0))
    return pl.pallas_call(
        kern, name=name, grid=(DEPTH, r // tm),
        in_specs=[flat, flat, pl.BlockSpec((1, LANES), lambda l, i: (0, 0)), stacked, stacked, stacked],
        out_specs=[stacked] * 4, out_shape=[jax.ShapeDtypeStruct(w.shape, F32)] * 4,
        compiler_params=_params(("arbitrary", "arbitrary")))(mine, theirs, cflag, w, m, v)


def _adamw_small(g, w, m, v):
    def body(ins, outs, _):
        outs[0][...], outs[1][...], outs[2][...] = _adam_math(*(r[...] for r in ins))
    return _ew("adamw_small", body, [(a, LANES, 0) for a in (g, w, m, v)], [(LANES, F32)] * 3, SMALL_ROWS)


def kernel(x, p, positions, g_mix, w_in, sink, g_q, w_uq, g_kv, w_ukv, w_br_a, w_br_b, w_out, g_ple, w_ple_gate, w_ple_proj, g_final, loss_target, m_g_mix, m_w_in, m_sink, m_g_q, m_w_uq, m_g_kv, m_w_ukv, m_w_br_a, m_w_br_b, m_w_out, m_g_ple, m_w_ple_gate, m_w_ple_proj, m_g_final, v_g_mix, v_w_in, v_sink, v_g_q, v_w_uq, v_g_kv, v_w_ukv, v_w_br_a, v_w_br_b, v_w_out, v_g_ple, v_w_ple_gate, v_w_ple_proj, v_g_final):
    w = dict(g_mix=g_mix, w_in=w_in, sink=sink, g_q=g_q, w_uq=w_uq, g_kv=g_kv, w_ukv=w_ukv, w_br_a=w_br_a,
             w_br_b=w_br_b, w_out=w_out, g_ple=g_ple, w_ple_gate=w_ple_gate, w_ple_proj=w_ple_proj, g_final=g_final)
    m = dict(g_mix=m_g_mix, w_in=m_w_in, sink=m_sink, g_q=m_g_q, w_uq=m_w_uq, g_kv=m_g_kv, w_ukv=m_w_ukv,
             w_br_a=m_w_br_a, w_br_b=m_w_br_b, w_out=m_w_out, g_ple=m_g_ple, w_ple_gate=m_w_ple_gate,
             w_ple_proj=m_w_ple_proj, g_final=m_g_final)
    v = dict(g_mix=v_g_mix, w_in=v_w_in, sink=v_sink, g_q=v_g_q, w_uq=v_w_uq, g_kv=v_g_kv, w_ukv=v_w_ukv,
             w_br_a=v_w_br_a, w_br_b=v_w_br_b, w_out=v_w_out, g_ple=v_g_ple, w_ple_gate=v_w_ple_gate,
             w_ple_proj=v_w_ple_proj, g_final=v_g_final)
    wfull = _gather_full(w)
    sm = {name: w[name] for name in SMALL}
    loss_row, grad_x, layer_grads, dg_final = _local_step(x, p, positions, wfull, sm, loss_target)
    res, loss = _update(layer_grads, dg_final, loss_row[0, 0], w, m, v)
    return (loss, grad_x, *[res[name][kind] for kind in range(4) for name in WEIGHT_NAMES])


def _gather_behind(name, collective_id, shards):
    n = len(shards)
    srcs = [jax.new_ref(s, memory_space=pltpu.MemorySpace.HBM) for s in shards]
    lands = [jax.empty_ref(jax.ShapeDtypeStruct((N_CHIPS,) + s.shape, s.dtype), memory_space=pltpu.MemorySpace.HBM)
             for s in shards]

    @pl.kernel(mesh=plsc.ScalarSubcoreMesh(axis_name="sequencer", num_cores=1), name=name,
               scratch_types=(pltpu.SemaphoreType.DMA((3 * n,)), pltpu.SemaphoreType.DMA((3 * n,)),
                              pltpu.SemaphoreType.DMA((n,))),
               compiler_params=pltpu.CompilerParams(collective_id=collective_id))
    def launch(send_sems, recv_sems, local_sems):
        x, y, c, chips = _place()
        me = 2 * x + y
        barrier = pltpu.get_barrier_semaphore()
        for cx, cy in chips:
            pl.semaphore_signal(barrier, inc=1, device_id=(cx, cy, c), device_id_type=MESH)
        pl.semaphore_wait(barrier, len(chips))
        copy = functools.partial(_remote, send_sems, recv_sems)
        keeps = [pltpu.make_async_copy(srcs[w], lands[w].at[me], local_sems.at[w]) for w in range(n)]
        cps = [copy(j * n + w, srcs[w], lands[w].at[me], (cx, cy, c))
               for j, (cx, cy) in enumerate(chips) for w in range(n)]
        for cp in keeps + cps:
            cp.start()
        for cp in keeps + cps:
            cp.wait()

    launch()
    return [land[...] for land in lands]


def _gather_full(w):
    shards = [w[name].astype(BF16) for name, _ in SHARDED]
    w_in0 = shards[0][0]
    first, later = _gather_weights([w_in0.reshape((2, w_in0.shape[0] // 2) + w_in0.shape[1:])],
                                   [s[0] for s in shards[1:]] + [s[1] for s in shards])
    n_rest = len(shards) - 1
    layer0 = [first[0].reshape((N_CHIPS,) + w_in0.shape)] + _gather_behind("gather_rest", 0, later[:n_rest])
    layer1 = _gather_behind("gather_next", 1, later[n_rest:])
    return {name: [jnp.concatenate(list(blocks[k]), axis=axis - 1) for blocks in (layer0, layer1)]
            for k, (name, axis) in enumerate(SHARDED)}


def _update(layer_grads, dg_final, loss_local, w, m, v):
    small_shapes = {name: w[name].shape for name in SMALL}
    cflag = jnp.full((1, LANES), lax.axis_index("c"), F32)
    chipflag = jnp.full((1, LANES), 2 * lax.axis_index("x") + lax.axis_index("y"), F32)

    slots = [[_to_slots(layer_grads[layer][name], axis - 1) for name, axis in SHARDED] for layer in range(DEPTH)]
    theirs = _pair_exchange(slots[0], slots[1])
    pair = [_pair_sum("pair_sum_" + name, slots[0][k], slots[1][k], theirs[k], cflag)
            for k, (name, _) in enumerate(SHARDED)]
    landed = _chip_exchange([bf16 for _, bf16 in pair])
    mine = [_chip_sum("chip_sum_" + name, pair[k][0], landed[k], chipflag) for k, (name, _) in enumerate(SHARDED)]
    other = _pair_broadcast(mine)
    res = {}
    for k, (name, _) in enumerate(SHARDED):
        flip = _flipped(w[name].shape)
        view = (lambda a: jnp.swapaxes(a, -1, -2)) if flip else (lambda a: a)
        outs = _adamw_big("adamw_" + name, view(mine[k]), view(other[k]), cflag, view(w[name]), view(m[name]),
                          view(v[name]))
        res[name] = tuple(view(a) for a in outs)

    gsmall = {name: jnp.stack([layer_grads[layer][name] for layer in range(DEPTH)]) for name in SMALL[:-1]}
    gsmall['g_final'] = dg_final
    gsum = _small_allreduce(_pack_small(gsmall, tail=[loss_local]))
    small = (gsum,) + tuple(_adamw_small(gsum, _pack_small(w), _pack_small(m), _pack_small(v)))
    for name, arrs in zip(SMALL, zip(*[[_unpack_small(a, small_shapes)[n] for n in SMALL] for a in small])):
        res[name] = arrs
    return res, gsum.reshape(-1)[SMALL_SIZE]
```

```python
import functools
import math

import jax
import jax.numpy as jnp
from jax import lax
from jax.experimental import pallas as pl
from jax.experimental.pallas import tpu as pltpu
from jax.experimental.pallas import tpu_sc as plsc

F32 = jnp.float32
BF16 = jnp.bfloat16

D_MODEL = 1024
DEPTH = 2
PLE_DIM = 256
BLOCK = 128
EPS = 1e-6
NEG = -1e30
HEADS = 8
SWA_KV_HEADS = 2
HEAD_DIM = 64
LANES = 128
HPAD = HEADS * LANES
MLA_QK = 96
MLA_ROPE = 32
MLA_Q_LORA = 256
MLA_KV_LORA = 128
ROPE_THETA = 10000.0
IN_SIZES = (512, 128, 128, 512, 256, 128, 32, 512, 1024, 1024)

Z_MA, Z_MB, Z_AQ, Z_AGATE, Z_BGATE = 0, 1024, 2048, 3072, 3584
Z_AK, Z_AV, Z_BQD, Z_BKVD, Z_BKR = 4096, 4352, 4608, 4864, 4992
Z_WIDTH = 5120
GATE_W = HEADS * HEAD_DIM
KV_W = SWA_KV_HEADS * LANES

ADAM_LR, ADAM_B1, ADAM_B2, ADAM_EPS, ADAM_WD, ADAM_STEP = 0.001, 0.9, 0.999, 1e-08, 0.01, 10

VMEM_LIMIT = 56 * 1024 * 1024
MESH = pl.DeviceIdType.MESH

WEIGHT_NAMES = ('g_mix', 'w_in', 'sink', 'g_q', 'w_uq', 'g_kv', 'w_ukv', 'w_br_a', 'w_br_b',
                'w_out', 'g_ple', 'w_ple_gate', 'w_ple_proj', 'g_final')
SHARDED = (('w_in', 2), ('w_uq', 2), ('w_ukv', 2), ('w_br_a', 2), ('w_br_b', 2),
           ('w_out', 1), ('w_ple_gate', 1), ('w_ple_proj', 2))
SMALL = ('g_mix', 'sink', 'g_q', 'g_kv', 'g_ple', 'g_final')
N_CHIPS = 4


def _params(sem):
    return pltpu.CompilerParams(dimension_semantics=sem, vmem_limit_bytes=VMEM_LIMIT)


MM_TN = 512
ROW_TILE = 512
BIG_WEIGHT_BYTES = 16 * 1024 * 1024


def _row_tile(rows, weight_bytes=0):
    tm = ROW_TILE // 2 if weight_bytes > BIG_WEIGHT_BYTES else ROW_TILE
    return min(tm, rows)


def _ew(name, body, ins, outs, rows, accs=(), mms=(), tm=None):
    n_mm, n_in, n_out = len(mms), len(ins), len(outs)
    if tm is None:
        tm = _row_tile(rows, sum(b.size * b.dtype.itemsize for _, b in mms))
    in_specs, args = [], []
    for a, b in mms:
        in_specs += [pl.BlockSpec((tm, a.shape[1]), lambda i: (i, 0)), pl.BlockSpec(b.shape, lambda i: (0, 0))]
        args += [a, b]
    for arr, width, cb in ins:
        if width is None:
            in_specs.append(pl.BlockSpec(arr.shape, lambda i, nd=arr.ndim: (0,) * nd))
        else:
            in_specs.append(pl.BlockSpec((tm, width), lambda i, cb=cb: (i, cb)))
        args.append(arr)
    out_shape, out_specs, aliases = [], [], {}
    for k, out in enumerate(outs):
        if len(out) == 4:
            aliases[len(args)] = k
            in_specs.append(pl.BlockSpec(memory_space=pl.ANY))
            args.append(out[2])
            out_shape.append(jax.ShapeDtypeStruct(out[2].shape, out[2].dtype))
            out_specs.append(pl.BlockSpec((tm, out[0]), lambda i, cb=out[3]: (i, cb)))
        else:
            out_shape.append(jax.ShapeDtypeStruct((rows, out[0]), out[1]))
            out_specs.append(pl.BlockSpec((tm, out[0]), lambda i: (i, 0)))
    n_in += len(aliases)
    out_shape += [jax.ShapeDtypeStruct(s, F32) for s in accs]
    out_specs += [pl.BlockSpec(s, lambda i: (0, 0)) for s in accs]

    def kern(*refs):
        mm_refs, refs = refs[:2 * n_mm], refs[2 * n_mm:]
        in_refs, out_refs = refs[:n_in - len(aliases)], refs[n_in:n_in + n_out]
        acc_refs, prod_refs = refs[n_in + n_out:n_in + n_out + len(accs)], refs[n_in + n_out + len(accs):]
        if acc_refs:
            @pl.when(pl.program_id(0) == 0)
            def _():
                for r in acc_refs:
                    r[...] = jnp.zeros_like(r)
        for k in range(n_mm):
            a_ref, b_ref, prod = mm_refs[2 * k], mm_refs[2 * k + 1], prod_refs[k]
            av = a_ref[...].astype(BF16)
            n = b_ref.shape[1]
            tn = min(MM_TN, n)
            for j in range(n // tn):
                cols = slice(j * tn, (j + 1) * tn)
                prod[:, cols] = jnp.dot(av, b_ref[:, cols], preferred_element_type=F32)
        body(tuple(prod_refs) + tuple(in_refs), out_refs, acc_refs)

    scratch = [pltpu.VMEM((tm, b.shape[1]), F32) for _, b in mms]
    res = pl.pallas_call(kern, name=name, grid=(rows // tm,), in_specs=in_specs, out_specs=out_specs,
                         out_shape=out_shape, scratch_shapes=scratch, input_output_aliases=aliases,
                         compiler_params=_params(("arbitrary",)))(*args)
    return res


def _rms(xv, gv):
    r = lax.rsqrt(jnp.mean(xv * xv, axis=-1, keepdims=True) + EPS)
    return ((xv * r) * gv).astype(BF16)


def _rms_fwd(name, x, width, cb, g, rows):
    def body(ins, outs, _):
        outs[0][...] = _rms(ins[0][...].astype(F32), ins[1][...])
    return _ew(name, body, [(x, width, cb), (g.reshape(1, width), None, None)], [(width, BF16)], rows)[0]


def _ple_grads(d, u, e):
    s = jax.nn.sigmoid(u.astype(F32))
    return (d * s).astype(BF16), (d * e.astype(F32) * (s * (1.0 - s))).astype(BF16)


def _rms_bwd(name, x, width, cb, g, dh_mm, rows, out_dtype, dres=None, into=(), ple=None):
    def body(ins, outs, accs):
        dhv, xv, gv = ins[0][...], ins[1][...].astype(F32), ins[2][...]
        r = lax.rsqrt(jnp.mean(xv * xv, axis=-1, keepdims=True) + EPS)
        xhat = xv * r
        accs[0][...] += jnp.sum(dhv * xhat, axis=0, keepdims=True)
        dy = dhv * gv
        dx = r * (dy - xhat * jnp.mean(dy * xhat, axis=-1, keepdims=True))
        if dres is not None:
            dx = dx + ins[3][...]
        outs[0][...] = dx.astype(out_dtype)
        if ple is not None:
            outs[1][...], outs[2][...] = _ple_grads(dx, ins[-2][...], ins[-1][...])
    ins = [(x, width, cb), (g.reshape(1, width), None, None)]
    if dres is not None:
        ins.append((dres, width, 0))
    outs = [(width, out_dtype) + tuple(into)]
    if ple is not None:
        ins += [(ple[0], width, 0), (ple[1], width, 0)]
        outs += [(width, BF16), (width, BF16)]
    return _ew(name, body, ins, outs, rows, accs=[(1, width)], mms=[dh_mm])


def _mm(name, a, b, out_dtype, f32_cols=None, norms=(), tn=MM_TN):
    M, K = a.shape
    N = b.shape[1]
    tm, tn = _row_tile(M, b.size * b.dtype.itemsize), min(tn, N)
    c0, cw = f32_cols if f32_cols else (0, 0)
    n_norm, n_f32 = len(norms), 1 if f32_cols else 0
    assert c0 % tn == 0 and cw % tn == 0
    assert all(nc // tn == (nc + g.shape[-1] - 1) // tn for nc, g in norms)

    def kern(*refs):
        a_ref, b_ref, g_refs = refs[0], refs[1], refs[2:2 + n_norm]
        o_ref, extra = refs[2 + n_norm], refs[3 + n_norm:]
        av = a_ref[...].astype(BF16)
        for j in range(N // tn):
            cols = slice(j * tn, (j + 1) * tn)
            part = jnp.dot(av, b_ref[:, cols], preferred_element_type=F32)
            o_ref[:, cols] = part.astype(o_ref.dtype)
            if f32_cols and c0 <= j * tn and (j + 1) * tn <= c0 + cw:
                extra[0][:, j * tn - c0:(j + 1) * tn - c0] = part
            for k, (nc, g) in enumerate(norms):
                if nc // tn == j:
                    seg = part[:, nc - j * tn:nc - j * tn + g.shape[-1]]
                    extra[n_f32 + k][...] = _rms(seg, g_refs[k][...])

    in_specs = [pl.BlockSpec((tm, K), lambda i: (i, 0)), pl.BlockSpec((K, N), lambda i: (0, 0))]
    in_specs += [pl.BlockSpec((1, g.shape[-1]), lambda i: (0, 0)) for _, g in norms]
    widths = [(N, out_dtype)] + ([(cw, F32)] if f32_cols else []) + [(g.shape[-1], BF16) for _, g in norms]
    return pl.pallas_call(
        kern, name=name, grid=(M // tm,), in_specs=in_specs,
        out_specs=[pl.BlockSpec((tm, w), lambda i: (i, 0)) for w, _ in widths],
        out_shape=[jax.ShapeDtypeStruct((M, w), dt) for w, dt in widths],
        compiler_params=_params(("parallel",)))(a, b, *[g.reshape(1, -1) for _, g in norms])


def _mm_tn(name, a, b, tk=2048, tn=2048):
    T, M = a.shape
    N = b.shape[1]
    tn, tk = min(tn, N), min(tk, T)

    def kern(a_ref, b_ref, o_ref):
        k = pl.program_id(1)
        part = _dot_tn(a_ref[...].astype(BF16), b_ref[...].astype(BF16))

        @pl.when(k == 0)
        def _():
            o_ref[...] = part

        @pl.when(k > 0)
        def _():
            o_ref[...] += part

    return pl.pallas_call(
        kern, name=name, grid=(N // tn, T // tk),
        in_specs=[pl.BlockSpec((tk, M), lambda j, k: (k, 0)), pl.BlockSpec((tk, tn), lambda j, k: (k, j))],
        out_specs=pl.BlockSpec((M, tn), lambda j, k: (0, j)),
        out_shape=jax.ShapeDtypeStruct((M, N), F32),
        compiler_params=_params(("parallel", "arbitrary")))(a, b)


def _dot_nt(a, b):
    return lax.dot_general(a, b, (((1,), (1,)), ((), ())), preferred_element_type=F32)


def _dot_tn(a, b):
    return lax.dot_general(a, b, (((0,), (0,)), ((), ())), preferred_element_type=F32)


SWA_SCALE = HEAD_DIM ** -0.5


def _swa_band(n, pq_ref, pkp_ref, pkc_ref):
    posk = jnp.concatenate([pkp_ref[...], pkc_ref[...]], axis=0)
    dist = (pq_ref[0] - posk).astype(F32)
    kj = lax.broadcasted_iota(jnp.int32, (2 * BLOCK, BLOCK), 0)
    qi = lax.broadcasted_iota(jnp.int32, (2 * BLOCK, BLOCK), 1)
    t_abs = n * BLOCK + qi
    s_abs = n * BLOCK - BLOCK + kj
    return dist, (s_abs >= 0) & (s_abs <= t_abs) & (t_abs - s_abs < BLOCK)


SWA_GROUP = HEADS // SWA_KV_HEADS


def _head_gate(gate_ref, h):
    pair = gate_ref[:, (h // 2) * LANES:(h // 2 + 1) * LANES].astype(F32)
    return pair if h % 2 == 0 else pltpu.roll(pair, HEAD_DIM, 1)


def _swa_group_q(q_all, g):
    heads = range(g * SWA_GROUP, (g + 1) * SWA_GROUP)
    return jnp.concatenate([(q_all[:, h * LANES:(h + 1) * LANES] * SWA_SCALE).astype(BF16) for h in heads], axis=0)


def _swa_mask(s, dist, valid, h):
    return jnp.where(valid, s - (2.0 ** -(h + 1)) * dist, NEG)


def _rows_to_lanes(rows):
    block = jnp.concatenate(list(rows) + [jnp.zeros((LANES - len(rows), BLOCK), F32)], axis=0)
    return block.T


def _swa_specs(nb):
    prev = lambda b, n: b * nb + jnp.maximum(n - 1, 0)
    own = lambda b, n: b * nb + n
    return [
        pl.BlockSpec((BLOCK, HPAD), lambda b, n: (own(b, n), Z_AQ // HPAD)),
        pl.BlockSpec((BLOCK, KV_W), lambda b, n: (prev(b, n), Z_AK // KV_W)),
        pl.BlockSpec((BLOCK, KV_W), lambda b, n: (own(b, n), Z_AK // KV_W)),
        pl.BlockSpec((BLOCK, KV_W), lambda b, n: (prev(b, n), Z_AV // KV_W)),
        pl.BlockSpec((BLOCK, KV_W), lambda b, n: (own(b, n), Z_AV // KV_W)),
        pl.BlockSpec((1, 1, BLOCK), lambda b, n: (own(b, n), 0, 0)),
        pl.BlockSpec((BLOCK, 1), lambda b, n: (prev(b, n), 0)),
        pl.BlockSpec((BLOCK, 1), lambda b, n: (own(b, n), 0)),
    ]


def _swa_fwd(z, gate, pos_col, pos_row, sink_row, B, S):
    nb = S // BLOCK
    T = B * S

    def kern(q_ref, kp_ref, kc_ref, vp_ref, vc_ref, pq_ref, pkp_ref, pkc_ref, gate_ref, sink_ref,
             oraw_ref, og_ref, lse_ref):
        q_all = q_ref[...]
        kb = jnp.concatenate([kp_ref[...], kc_ref[...]], axis=0).astype(BF16)
        vb = jnp.concatenate([vp_ref[...], vc_ref[...]], axis=0).astype(BF16)
        dist, valid = _swa_band(pl.program_id(1), pq_ref, pkp_ref, pkc_ref)
        lse_rows = []
        for grp in range(SWA_KV_HEADS):
            gcols = slice(grp * LANES, (grp + 1) * LANES)
            s_all = _dot_nt(kb[:, gcols], _swa_group_q(q_all, grp))
            probs = []
            for hh in range(SWA_GROUP):
                h = grp * SWA_GROUP + hh
                s = _swa_mask(s_all[:, hh * BLOCK:(hh + 1) * BLOCK], dist, valid, h)
                sink_h = sink_ref[0:1, h:h + 1]
                m = jnp.maximum(jnp.max(s, axis=0, keepdims=True), sink_h)
                e = jnp.exp(s - m)
                denom = jnp.sum(e, axis=0, keepdims=True) + jnp.exp(sink_h - m)
                probs.append((e * (1.0 / denom)).astype(BF16))
                lse_rows.append(m + jnp.log(denom))
            o_all = jnp.dot(vb[:, gcols].T, jnp.concatenate(probs, axis=1), preferred_element_type=F32)
            for hh in range(SWA_GROUP):
                h = grp * SWA_GROUP + hh
                cols = slice(h * LANES, (h + 1) * LANES)
                o = o_all[:, hh * BLOCK:(hh + 1) * BLOCK].T
                oraw_ref[:, cols] = o
                g = _head_gate(gate_ref, h)
                og_ref[:, cols] = (o * (g * jax.nn.sigmoid(g))).astype(BF16)
        lse_ref[...] = _rows_to_lanes(lse_rows)

    own = lambda b, n: b * nb + n
    in_specs = _swa_specs(nb) + [
        pl.BlockSpec((BLOCK, GATE_W), lambda b, n: (own(b, n), 0)),
        pl.BlockSpec((1, LANES), lambda b, n: (0, 0)),
    ]
    out_specs = [pl.BlockSpec((BLOCK, HPAD), lambda b, n: (own(b, n), 0)),
                 pl.BlockSpec((BLOCK, HPAD), lambda b, n: (own(b, n), 0)),
                 pl.BlockSpec((BLOCK, LANES), lambda b, n: (own(b, n), 0))]
    out_shape = [jax.ShapeDtypeStruct((T, HPAD), F32), jax.ShapeDtypeStruct((T, HPAD), BF16),
                 jax.ShapeDtypeStruct((T, LANES), F32)]
    return pl.pallas_call(kern, name="swa_fwd", grid=(B, nb), in_specs=in_specs, out_specs=out_specs,
                          out_shape=out_shape, compiler_params=_params(("parallel", "arbitrary")))(
        z, z, z, z, z, pos_row, pos_col, pos_col, gate, sink_row)


def _swa_bwd(z, pos_col, pos_row, sink_row, lse, do_raw, delta, dz, B, S):
    nb = S // BLOCK
    T = B * S

    def kern(q_ref, kp_ref, kc_ref, vp_ref, vc_ref, pq_ref, pkp_ref, pkc_ref, sink_ref, lse_ref, do_ref,
             delta_ref, dz_ref, dq_ref, dk_ref, dv_ref, dsink_ref):
        b, n = pl.program_id(0), pl.program_id(1)

        @pl.when(n == 0)
        def _():
            dk_ref[...] = jnp.zeros_like(dk_ref)
            dv_ref[...] = jnp.zeros_like(dv_ref)

        @pl.when((b == 0) & (n == 0))
        def _():
            dsink_ref[...] = jnp.zeros_like(dsink_ref)

        q_all = q_ref[...]
        kb = jnp.concatenate([kp_ref[...], kc_ref[...]], axis=0).astype(BF16)
        vb = jnp.concatenate([vp_ref[...], vc_ref[...]], axis=0).astype(BF16)
        dist, valid = _swa_band(n, pq_ref, pkp_ref, pkc_ref)
        lse_t, delta_t = lse_ref[...].T, delta_ref[...].T
        lane1 = lax.broadcasted_iota(jnp.int32, (1, LANES), 1)
        dsink = jnp.zeros((1, LANES), F32)
        dk_band, dv_band = [], []
        for grp in range(SWA_KV_HEADS):
            gcols = slice(grp * LANES, (grp + 1) * LANES)
            heads = range(grp * SWA_GROUP, (grp + 1) * SWA_GROUP)
            qg = _swa_group_q(q_all, grp)
            dog = jnp.concatenate([do_ref[:, h * LANES:(h + 1) * LANES] for h in heads], axis=0)
            s_all = _dot_nt(kb[:, gcols], qg)
            dp_all = _dot_nt(vb[:, gcols], dog)
            ps, dss = [], []
            for hh, h in enumerate(heads):
                blk = slice(hh * BLOCK, (hh + 1) * BLOCK)
                lse_h, delta_h = lse_t[h:h + 1, :], delta_t[h:h + 1, :]
                p = jnp.exp(_swa_mask(s_all[:, blk], dist, valid, h) - lse_h)
                ps.append(p.astype(BF16))
                dss.append((p * (dp_all[:, blk] - delta_h)).astype(BF16))
                psink = jnp.exp(sink_ref[0:1, h:h + 1] - lse_h)
                dsink = dsink + jnp.where(lane1 == h, -jnp.sum(psink * delta_h, axis=1, keepdims=True), 0.0)
            dsg = jnp.concatenate(dss, axis=1)
            dq_all = jnp.dot(kb[:, gcols].T, dsg, preferred_element_type=F32) * SWA_SCALE
            for hh, h in enumerate(heads):
                dq_ref[:, h * LANES:(h + 1) * LANES] = dq_all[:, hh * BLOCK:(hh + 1) * BLOCK].T.astype(BF16)
            dk_band.append(jnp.dot(dsg, qg, preferred_element_type=F32))
            dv_band.append(jnp.dot(jnp.concatenate(ps, axis=1), dog, preferred_element_type=F32))
        dsink_ref[...] += dsink
        dkb = jnp.concatenate(dk_band, axis=1)
        dvb = jnp.concatenate(dv_band, axis=1)
        r_prev = pl.ds(pl.multiple_of(jnp.maximum(n - 1, 0) * BLOCK, BLOCK), BLOCK)
        r_own = pl.ds(pl.multiple_of(n * BLOCK, BLOCK), BLOCK)
        dk_ref[r_prev, :] += dkb[:BLOCK]
        dk_ref[r_own, :] += dkb[BLOCK:]
        dv_ref[r_prev, :] += dvb[:BLOCK]
        dv_ref[r_own, :] += dvb[BLOCK:]

    own = lambda b, n: b * nb + n
    in_specs = _swa_specs(nb) + [
        pl.BlockSpec((1, LANES), lambda b, n: (0, 0)),
        pl.BlockSpec((BLOCK, LANES), lambda b, n: (own(b, n), 0)),
        pl.BlockSpec((BLOCK, HPAD), lambda b, n: (own(b, n), 0)),
        pl.BlockSpec((BLOCK, LANES), lambda b, n: (own(b, n), 0)),
        pl.BlockSpec(memory_space=pl.ANY),
    ]
    out_specs = [pl.BlockSpec((BLOCK, HPAD), lambda b, n: (own(b, n), Z_AQ // HPAD)),
                 pl.BlockSpec((S, KV_W), lambda b, n: (b, 0)),
                 pl.BlockSpec((S, KV_W), lambda b, n: (b, 0)),
                 pl.BlockSpec((1, LANES), lambda b, n: (0, 0))]
    out_shape = [jax.ShapeDtypeStruct(dz.shape, dz.dtype), jax.ShapeDtypeStruct((T, KV_W), F32),
                 jax.ShapeDtypeStruct((T, KV_W), F32), jax.ShapeDtypeStruct((1, LANES), F32)]
    return pl.pallas_call(kern, name="swa_bwd", grid=(B, nb), in_specs=in_specs, out_specs=out_specs,
                          out_shape=out_shape, input_output_aliases={len(in_specs) - 1: 0},
                          compiler_params=_params(("arbitrary", "arbitrary")))(
        z, z, z, z, z, pos_row, pos_col, pos_col, sink_row, lse, do_raw, delta, dz)


MLA_T = 256
MLA_HG = 4
MLA_W = MLA_HG * LANES
MLA_HGB = 8
MLA_WB = MLA_HGB * LANES
MLA_SCALE = MLA_QK ** -0.5
LOG2E = 1.4426950408889634
MLA_QSCALE = MLA_SCALE * LOG2E


def _causal_t(s):
    key = lax.broadcasted_iota(jnp.int32, s.shape, 0)
    query = lax.broadcasted_iota(jnp.int32, s.shape, 1)
    return jnp.where(key <= query, s, NEG)


def _mla_fwd(q, k, v, z, B, S):
    T = B * S
    nq = S // MLA_T

    def kern(q_ref, k_ref, v_ref, gate_ref, oraw_ref, og_ref, lse_ref):
        i = pl.program_id(2)

        def scores(j):
            rows = pl.ds(pl.multiple_of(j * MLA_T, MLA_T), MLA_T)
            return tuple(_dot_nt(k_ref[rows, hh * LANES:(hh + 1) * LANES], q_ref[:, hh * LANES:(hh + 1) * LANES])
                         for hh in range(MLA_HG))

        def update(j, ss, state):
            rows = pl.ds(pl.multiple_of(j * MLA_T, MLA_T), MLA_T)
            out = []
            for hh in range(MLA_HG):
                (m, l, acc), s = state[hh], ss[hh]
                m_new = jnp.maximum(m, jnp.max(s, axis=0, keepdims=True))
                alpha = jnp.exp2(m - m_new)
                p = jnp.exp2(s - m_new)
                l = alpha * l + jnp.sum(p, axis=0, keepdims=True)
                pv = jnp.dot(v_ref[rows, hh * LANES:(hh + 1) * LANES].T, p.astype(BF16), preferred_element_type=F32)
                out.append((m_new, l, alpha * acc + pv))
            return tuple(out)

        def body(pair, state):
            j = 2 * pair
            s0, s1 = scores(j), scores(j + 1)
            return update(j + 1, s1, update(j, s0, state))

        init = tuple((jnp.full((1, MLA_T), NEG, F32), jnp.zeros((1, MLA_T), F32), jnp.zeros((LANES, MLA_T), F32))
                     for _ in range(MLA_HG))
        state = lax.fori_loop(0, i // 2, body, init)
        state = lax.cond(i % 2 == 1, lambda st: update(i - 1, scores(i - 1), st), lambda st: st, state)
        state = update(i, tuple(_causal_t(s) for s in scores(i)), state)
        for hh in range(MLA_HG):
            m, l, acc = state[hh]
            cols = slice(hh * LANES, (hh + 1) * LANES)
            o = (acc * (1.0 / l)).T
            oraw_ref[:, cols] = o.astype(BF16)
            g = _head_gate(gate_ref, hh)
            og_ref[:, cols] = (o * (g * jax.nn.sigmoid(g))).astype(BF16)
            lse_ref[0, 0, 0, hh:hh + 1, :] = m + jnp.log2(l)

    blk = lambda b, h, i: (b * nq + i, h)
    in_specs = [pl.BlockSpec((MLA_T, MLA_W), blk),
                pl.BlockSpec((S, MLA_W), lambda b, h, i: (b, h)),
                pl.BlockSpec((S, MLA_W), lambda b, h, i: (b, h)),
                pl.BlockSpec((MLA_T, MLA_W // 2), lambda b, h, i: (b * nq + i, Z_BGATE // (MLA_W // 2) + h))]
    out_specs = [pl.BlockSpec((MLA_T, MLA_W), blk), pl.BlockSpec((MLA_T, MLA_W), blk),
                 pl.BlockSpec((1, 1, 1, MLA_HG, MLA_T), lambda b, h, i: (b, h, i, 0, 0))]
    out_shape = [jax.ShapeDtypeStruct((T, HPAD), BF16), jax.ShapeDtypeStruct((T, HPAD), BF16),
                 jax.ShapeDtypeStruct((B, HEADS // MLA_HG, nq, MLA_HG, MLA_T), F32)]
    return pl.pallas_call(kern, name="mla_fwd", grid=(B, HEADS // MLA_HG, nq), in_specs=in_specs,
                          out_specs=out_specs, out_shape=out_shape,
                          compiler_params=_params(("parallel", "parallel", "arbitrary")))(q, k, v, z)


def _mla_bwd(q, k, v, do_raw, lse, delta, B, S):
    T = B * S
    nk = S // MLA_T

    def kern(q_ref, k_ref, v_ref, do_ref, lse_ref, delta_ref, dq_ref, dk_ref, dv_ref, dq_acc, dk_acc, dv_acc):
        j = pl.program_id(2)

        @pl.when(j == 0)
        def _():
            dq_acc[...] = jnp.zeros_like(dq_acc)

        dk_acc[...] = jnp.zeros_like(dk_acc)
        dv_acc[...] = jnp.zeros_like(dv_acc)
        kts = [k_ref[:, hh * LANES:(hh + 1) * LANES].T for hh in range(MLA_HGB)]

        def step(i, masked):
            rows = pl.ds(pl.multiple_of(i * MLA_T, MLA_T), MLA_T)
            for hh in range(MLA_HGB):
                cols = slice(hh * LANES, (hh + 1) * LANES)
                qv, do = q_ref[rows, cols], do_ref[rows, cols]
                st = _dot_nt(k_ref[:, cols], qv)
                if masked:
                    st = _causal_t(st)
                pt = jnp.exp2(st - lse_ref[0, 0, i, hh:hh + 1, :])
                dpt = _dot_nt(v_ref[:, cols], do)
                dst = (pt * (dpt - delta_ref[0, 0, i, hh:hh + 1, :])).astype(BF16)
                dv_acc[:, cols] += jnp.dot(pt.astype(BF16), do, preferred_element_type=F32)
                dk_acc[:, cols] += jnp.dot(dst, qv, preferred_element_type=F32)
                dq_acc[hh, i] += jnp.dot(kts[hh], dst, preferred_element_type=F32)

        step(j, True)

        def body(i, c):
            step(i, False)
            return c

        lax.fori_loop(j + 1, nk, body, 0)
        dk_ref[...] = (dk_acc[...] * (1.0 / LOG2E)).astype(BF16)
        dv_ref[...] = dv_acc[...].astype(BF16)

        @pl.when(j == nk - 1)
        def _():
            for hh in range(MLA_HGB):
                for t in range(nk):
                    dq_ref[t * MLA_T:(t + 1) * MLA_T, hh * LANES:(hh + 1) * LANES] = dq_acc[hh, t].T.astype(BF16)

    whole = lambda b, h, j: (b, h)
    tile = lambda b, h, j: (b * nk + j, h)
    stats = pl.BlockSpec((1, 1, nk, MLA_HGB, MLA_T), lambda b, h, j: (b, h, 0, 0, 0))
    in_specs = [pl.BlockSpec((S, MLA_WB), whole), pl.BlockSpec((MLA_T, MLA_WB), tile),
                pl.BlockSpec((MLA_T, MLA_WB), tile), pl.BlockSpec((S, MLA_WB), whole), stats, stats]
    out_specs = [pl.BlockSpec((S, MLA_WB), whole), pl.BlockSpec((MLA_T, MLA_WB), tile),
                 pl.BlockSpec((MLA_T, MLA_WB), tile)]
    out_shape = [jax.ShapeDtypeStruct((T, HPAD), BF16)] * 3
    scratch = [pltpu.VMEM((MLA_HGB, nk, LANES, MLA_T), F32), pltpu.VMEM((MLA_T, MLA_WB), F32),
               pltpu.VMEM((MLA_T, MLA_WB), F32)]
    return pl.pallas_call(kern, name="mla_bwd", grid=(B, HEADS // MLA_HGB, nk), in_specs=in_specs,
                          out_specs=out_specs, out_shape=out_shape, scratch_shapes=scratch,
                          compiler_params=_params(("parallel", "parallel", "arbitrary")))(
        q, k, v, do_raw, lse, delta)


def _rope_tables(pos_col, inv_lane, rows):
    def body(ins, outs, _):
        ang = ins[0][...].astype(F32) * ins[1][...]
        lane = lax.broadcasted_iota(jnp.int32, ang.shape, 1)
        cos, sin = jnp.cos(ang), jnp.sin(ang)
        first = (lane >= HEAD_DIM) & (lane < HEAD_DIM + MLA_ROPE // 2)
        second = (lane >= HEAD_DIM + MLA_ROPE // 2) & (lane < MLA_QK)
        outs[0][...] = jnp.where(lane < HEAD_DIM, 1.0, jnp.where(lane < MLA_QK, cos, 0.0))
        outs[1][...] = jnp.where(first, -sin, 0.0)
        outs[2][...] = jnp.where(second, sin, 0.0)
    return _ew("rope_tables", body, [(pos_col, 1, 0), (inv_lane, None, None)], [(LANES, F32)] * 3, rows)


def _rope(x, c, s1, s2):
    return x * c + pltpu.roll(x, 112, 1) * s1 + pltpu.roll(x, 16, 1) * s2


def _rope_t(d, c, s1, s2):
    return d * c + pltpu.roll(d * s1, 16, 1) + pltpu.roll(d * s2, 112, 1)


def _mla_prep(qdn, w_uq, kvdn, w_ukv, z, tabs, rows):
    def body(ins, outs, _):
        q_pre, kv_pre = ins[0], ins[1]
        c, s1, s2 = ins[3][...], ins[4][...], ins[5][...]
        kr = _rope(ins[2][...].astype(F32), c, s1, s2)
        for h in range(HEADS):
            cols = slice(h * LANES, (h + 1) * LANES)
            outs[0][:, cols] = (_rope(q_pre[:, cols], c, s1, s2) * MLA_QSCALE).astype(BF16)
            outs[1][:, cols] = (kv_pre[:, cols] + kr).astype(BF16)
        outs[2][...] = kv_pre[:, HPAD:].astype(BF16)
    ins = [(z, LANES, Z_BKR // LANES), (tabs[0], LANES, 0), (tabs[1], LANES, 0), (tabs[2], LANES, 0)]
    return _ew("mla_prep", body, ins, [(HPAD, BF16)] * 3, rows, mms=[(qdn, w_uq), (kvdn, w_ukv)])


def _mla_prep_bwd(dq, dk, dv, tabs, dz, rows):
    def body(ins, outs, _):
        c, s1, s2 = ins[3][...], ins[4][...], ins[5][...]
        lane = lax.broadcasted_iota(jnp.int32, c.shape, 1)
        dkr = jnp.zeros(c.shape, F32)
        for h in range(HEADS):
            cols = slice(h * LANES, (h + 1) * LANES)
            outs[0][:, cols] = _rope_t(ins[0][:, cols].astype(F32) * MLA_SCALE, c, s1, s2).astype(BF16)
            dkh = ins[1][:, cols].astype(F32)
            outs[1][:, cols] = jnp.where(lane < HEAD_DIM, dkh, 0.0).astype(BF16)
            dkr = dkr + dkh
        outs[1][:, HPAD:] = ins[2][...].astype(BF16)
        live = (lane >= HEAD_DIM) & (lane < MLA_QK)
        outs[2][...] = jnp.where(live, _rope_t(jnp.where(live, dkr, 0.0), c, s1, s2), 0.0).astype(BF16)
    ins = [(dq, HPAD, 0), (dk, HPAD, 0), (dv, HPAD, 0), (tabs[0], LANES, 0), (tabs[1], LANES, 0),
           (tabs[2], LANES, 0)]
    outs = [(HPAD, BF16), (2 * HPAD, BF16), (LANES, BF16, dz, Z_BKR // LANES)]
    return _ew("mla_prep_bwd", body, ins, outs, rows)


def _gate_bwd(name, d_o_mm, o_raw, gate, gate_cb, dz, dz_cb, rows):
    def body(ins, outs, _):
        lane = lax.broadcasted_iota(jnp.int32, outs[2].shape, 1)
        delta = jnp.zeros(outs[2].shape, F32)
        d_gate = [None] * HEADS
        for h in range(HEADS):
            cols = slice(h * LANES, (h + 1) * LANES)
            dog, o, g = ins[0][:, cols], ins[1][:, cols].astype(F32), _head_gate(ins[2], h)
            sg = jax.nn.sigmoid(g)
            do = dog * (g * sg)
            outs[0][:, cols] = do.astype(BF16)
            d_gate[h] = dog * o * (sg * (1.0 + g * (1.0 - sg)))
            delta = jnp.where(lane == h, jnp.sum(do * o, axis=-1, keepdims=True), delta)
        for pair in range(HEADS // 2):
            packed = d_gate[2 * pair] + pltpu.roll(d_gate[2 * pair + 1], HEAD_DIM, 1)
            outs[1][:, pair * LANES:(pair + 1) * LANES] = packed.astype(BF16)
        outs[2][...] = delta
    ins = [(o_raw, HPAD, 0), (gate, GATE_W, gate_cb)]
    outs = [(HPAD, BF16), (GATE_W, BF16, dz, dz_cb), (LANES, F32)]
    return _ew(name, body, ins, outs, rows, mms=[d_o_mm])


def _merge_out(oa, ob, w_br_a, w_br_b, z, w_out, x0, g_next, rows):
    tm = _row_tile(rows)

    def kern(oa_ref, ob_ref, wa_ref, wb_ref, ma_ref, mb_ref, w_ref, x0_ref, g_ref,
             ua_ref, ub_ref, y_ref, x1_ref, hn_ref):
        oa_v, ob_v = oa_ref[...], ob_ref[...]
        for j in range(D_MODEL // MM_TN):
            cols = slice(j * MM_TN, (j + 1) * MM_TN)
            ua = jnp.dot(oa_v, wa_ref[:, cols], preferred_element_type=F32)
            ub = jnp.dot(ob_v, wb_ref[:, cols], preferred_element_type=F32)
            ua_ref[:, cols] = ua.astype(BF16)
            ub_ref[:, cols] = ub.astype(BF16)
            m_a, m_b = ma_ref[:, cols].astype(F32), mb_ref[:, cols].astype(F32)
            y_ref[:, cols] = (jax.nn.sigmoid(m_a) * ua + jax.nn.sigmoid(m_b) * ub).astype(BF16)
        y = y_ref[...]
        for j in range(D_MODEL // MM_TN):
            cols = slice(j * MM_TN, (j + 1) * MM_TN)
            x1_ref[:, cols] = jnp.dot(y, w_ref[:, cols], preferred_element_type=F32) + x0_ref[:, cols]
        hn_ref[...] = _rms(x1_ref[...], g_ref[...])

    row = lambda cb: pl.BlockSpec((tm, D_MODEL), lambda i: (i, cb))
    whole = lambda a: pl.BlockSpec(a.shape, lambda i: (0, 0))
    bf16, f32 = jax.ShapeDtypeStruct((rows, D_MODEL), BF16), jax.ShapeDtypeStruct((rows, D_MODEL), F32)
    return pl.pallas_call(
        kern, name="merge_out", grid=(rows // tm,),
        in_specs=[row(0), row(0), whole(w_br_a), whole(w_br_b), row(Z_MA // D_MODEL), row(Z_MB // D_MODEL),
                  whole(w_out), row(0), pl.BlockSpec((1, D_MODEL), lambda i: (0, 0))],
        out_specs=[row(0)] * 5, out_shape=[bf16, bf16, bf16, f32, bf16],
        compiler_params=_params(("parallel",)))(oa, ob, w_br_a, w_br_b, z, z, w_out, x0, g_next.reshape(1, D_MODEL))


def _merge_bwd(dy_mm, ua, ub, z, dz, rows):
    def body(ins, outs, _):
        dyv = ins[0][...]
        for idx in range(2):
            s = jax.nn.sigmoid(ins[3 + idx][...].astype(F32))
            outs[idx][...] = (dyv * s).astype(BF16)
            d_m = (dyv * ins[1 + idx][...].astype(F32) * (s * (1.0 - s))).astype(BF16)
            outs[2][:, idx * D_MODEL:(idx + 1) * D_MODEL] = d_m
    ins = [(ua, D_MODEL, 0), (ub, D_MODEL, 0), (z, D_MODEL, Z_MA // D_MODEL), (z, D_MODEL, Z_MB // D_MODEL)]
    outs = [(D_MODEL, BF16), (D_MODEL, BF16), (2 * D_MODEL, BF16, dz, Z_MA // (2 * D_MODEL))]
    return _ew("merge_bwd", body, ins, outs, rows, mms=[dy_mm])


def _kv_grad_cast(dk, dv, dz, rows):
    def body(ins, outs, _):
        outs[0][:, :KV_W] = ins[0][...].astype(BF16)
        outs[0][:, KV_W:] = ins[1][...].astype(BF16)
    outs = [(2 * KV_W, BF16, dz, Z_AK // (2 * KV_W))]
    return _ew("kv_grad_cast", body, [(dk, KV_W, 0), (dv, KV_W, 0)], outs, rows)[0]


def _ple_fwd(x1, hn, w_pg, p, w_pp, g_next, rows):
    def body(ins, outs, _):
        u, e = ins[0][...], ins[1][...]
        x2 = ins[2][...] + jax.nn.sigmoid(u) * e
        outs[0][...] = x2
        outs[1][...] = u.astype(BF16)
        outs[2][...] = e.astype(BF16)
        if g_next is not None:
            outs[3][...] = _rms(x2, ins[3][...])
    ins = [(x1, D_MODEL, 0)] + ([(g_next.reshape(1, D_MODEL), None, None)] if g_next is not None else [])
    outs = [(D_MODEL, F32), (D_MODEL, BF16), (D_MODEL, BF16)] + ([(D_MODEL, BF16)] if g_next is not None else [])
    return _ew("ple_fwd", body, ins, outs, rows, mms=[(hn, w_pg), (p, w_pp)])


def _loss_head(x, g, target, ple, rows):
    def body(ins, outs, accs):
        xv, gv = ins[0][...], ins[1][...]
        r = lax.rsqrt(jnp.mean(xv * xv, axis=-1, keepdims=True) + EPS)
        xhat = xv * r
        err = xhat * gv - ins[2][...]
        accs[0][...] += jnp.broadcast_to(0.5 * jnp.sum(jnp.mean(err * err, axis=-1, keepdims=True),
                                                       axis=0, keepdims=True), (1, LANES))
        dyv = err * (1.0 / D_MODEL)
        accs[1][...] += jnp.sum(dyv * xhat, axis=0, keepdims=True)
        dy = dyv * gv
        dx = r * (dy - xhat * jnp.mean(dy * xhat, axis=-1, keepdims=True))
        outs[0][...] = dx
        outs[1][...], outs[2][...] = _ple_grads(dx, ins[3][...], ins[4][...])
    ins = [(x, D_MODEL, 0), (g.reshape(1, D_MODEL), None, None), (target, D_MODEL, 0),
           (ple[0], D_MODEL, 0), (ple[1], D_MODEL, 0)]
    outs = [(D_MODEL, F32), (D_MODEL, BF16), (D_MODEL, BF16)]
    return _ew("loss_head", body, ins, outs, rows, accs=[(1, LANES), (1, D_MODEL)])


def _pad_heads_cols(w, n_heads, dim):
    k = w.shape[0]
    return jnp.pad(w.reshape(k, n_heads, dim), ((0, 0), (0, 0), (0, LANES - dim))).reshape(k, n_heads * LANES)


def _unpad_heads_cols(w, n_heads, dim):
    k = w.shape[0]
    return w.reshape(k, n_heads, LANES)[:, :, :dim].reshape(k, n_heads * dim)


def _layer_weights(w, i):
    segs = jnp.split(w['w_in'][i], list(_cumsum(IN_SIZES))[:-1], axis=1)
    a_q, a_k, a_v, a_gate, b_qd, b_kvd, b_kr, b_gate, m_a, m_b = segs
    kr = jnp.pad(b_kr, ((0, 0), (HEAD_DIM, LANES - MLA_QK)))
    w_in = jnp.concatenate([
        m_a, m_b, _pad_heads_cols(a_q, HEADS, HEAD_DIM), a_gate, b_gate, _pad_heads_cols(a_k, SWA_KV_HEADS, HEAD_DIM),
        _pad_heads_cols(a_v, SWA_KV_HEADS, HEAD_DIM), b_qd, b_kvd, kr], axis=1)
    w_uq = _pad_heads_cols(w['w_uq'][i], HEADS, MLA_QK)
    ukv = w['w_ukv'][i].reshape(MLA_KV_LORA, HEADS, 2 * HEAD_DIM)
    pad = ((0, 0), (0, 0), (0, HEAD_DIM))
    w_ukv = jnp.concatenate([jnp.pad(ukv[:, :, :HEAD_DIM], pad).reshape(MLA_KV_LORA, HPAD),
                             jnp.pad(ukv[:, :, HEAD_DIM:], pad).reshape(MLA_KV_LORA, HPAD)], axis=1)
    w_br_a = _pad_heads_cols(w['w_br_a'][i].T, HEADS, HEAD_DIM).T
    w_br_b = _pad_heads_cols(w['w_br_b'][i].T, HEADS, HEAD_DIM).T
    out = dict(w_in=w_in, w_uq=w_uq, w_ukv=w_ukv, w_br_a=w_br_a, w_br_b=w_br_b, w_out=w['w_out'][i],
               w_pg=w['w_ple_gate'][i], w_pp=w['w_ple_proj'][i])
    for name in ('w_in', 'w_uq', 'w_ukv', 'w_br_a', 'w_br_b', 'w_out', 'w_pg'):
        out[name + '_t'] = out[name].T
    return out


def _cumsum(sizes):
    acc, out = 0, []
    for s in sizes:
        acc += s
        out.append(acc)
    return out


def _unpad_grads(g):
    d = g['w_in']
    seg = lambda off, width: d[:, off:off + width]
    b_kr = seg(Z_BKR, LANES)[:, HEAD_DIM:MLA_QK]
    w_in = jnp.concatenate([
        _unpad_heads_cols(seg(Z_AQ, HPAD), HEADS, HEAD_DIM), _unpad_heads_cols(seg(Z_AK, KV_W), SWA_KV_HEADS, HEAD_DIM),
        _unpad_heads_cols(seg(Z_AV, KV_W), SWA_KV_HEADS, HEAD_DIM), seg(Z_AGATE, GATE_W),
        seg(Z_BQD, MLA_Q_LORA), seg(Z_BKVD, MLA_KV_LORA), b_kr, seg(Z_BGATE, GATE_W),
        seg(Z_MA, D_MODEL), seg(Z_MB, D_MODEL)], axis=1)
    w_uq = _unpad_heads_cols(g['w_uq'], HEADS, MLA_QK)
    ukv = g['w_ukv'].reshape(MLA_KV_LORA, 2, HEADS, LANES)[:, :, :, :HEAD_DIM]
    w_ukv = jnp.concatenate([ukv[:, 0], ukv[:, 1]], axis=-1).reshape(MLA_KV_LORA, HEADS * 2 * HEAD_DIM)
    w_br_a = _unpad_heads_cols(g['w_br_a'].T, HEADS, HEAD_DIM).T
    w_br_b = _unpad_heads_cols(g['w_br_b'].T, HEADS, HEAD_DIM).T
    return dict(w_in=w_in, w_uq=w_uq, w_ukv=w_ukv, w_br_a=w_br_a, w_br_b=w_br_b, w_out=g['w_out'],
                w_ple_gate=g['w_pg'], w_ple_proj=g['w_pp'], g_mix=g['g_mix'], sink=g['sink'], g_q=g['g_q'],
                g_kv=g['g_kv'], g_ple=g['g_ple'])


def _layer_fwd(x0, h, p_i, lw, sm, i, pos_col, pos_row, tabs, B, S):
    T = B * S
    z, a_gate, qdn, kvdn = _mm("proj_in", h, lw['w_in'], BF16, f32_cols=(Z_AGATE, GATE_W),
                               norms=[(Z_BQD, sm['g_q'][i]), (Z_BKVD, sm['g_kv'][i])])
    sink_row = jnp.pad(sm['sink'][i], (0, LANES - HEADS)).reshape(1, LANES)
    oa_raw, oa, lse_a = _swa_fwd(z, a_gate, pos_col, pos_row, sink_row, B, S)
    qf, kf, vf = _mla_prep(qdn, lw['w_uq'], kvdn, lw['w_ukv'], z, tabs, T)
    ob_raw, ob, lse_b = _mla_fwd(qf, kf, vf, z, B, S)
    ua, ub, y, x1, hn = _merge_out(oa, ob, lw['w_br_a'], lw['w_br_b'], z, lw['w_out'], x0, sm['g_ple'][i], T)
    g_next = sm['g_mix'][i + 1] if i + 1 < DEPTH else None
    x2, u, e, *h_next = _ple_fwd(x1, hn, lw['w_pg'], p_i, lw['w_pp'], g_next, T)
    saved = dict(x0=x0, h=h, z=z, a_gate=a_gate, sink_row=sink_row, oa_raw=oa_raw, oa=oa, lse_a=lse_a, qdn=qdn, kvdn=kvdn,
                 qf=qf, kf=kf, vf=vf, ob_raw=ob_raw, ob=ob, lse_b=lse_b, ua=ua, ub=ub, y=y, x1=x1, hn=hn,
                 u=u, e=e, p=p_i)
    return x2, (h_next[0] if h_next else None), saved


def _layer_bwd(dx2, d_e, d_u, sv, below, lw, sm, i, pos_col, pos_row, tabs, B, S):
    T = B * S
    z = sv['z']
    g = {}
    g['w_pp'] = _mm_tn("grad_pp", sv['p'], d_e)
    g['w_pg'] = _mm_tn("grad_pg", sv['hn'], d_u)
    dx1, g['g_ple'] = _rms_bwd("norm_ple_bwd", sv['x1'], D_MODEL, 0, sm['g_ple'][i], (d_u, lw['w_pg_t']), T, F32,
                               dres=dx2)
    g['w_out'] = _mm_tn("grad_out", sv['y'], dx1)
    dz = lax.empty((T, Z_WIDTH), BF16)
    d_ua, d_ub, dz = _merge_bwd((dx1, lw['w_out_t']), sv['ua'], sv['ub'], z, dz, T)
    g['w_br_a'] = _mm_tn("grad_br_a", sv['oa'], d_ua)
    g['w_br_b'] = _mm_tn("grad_br_b", sv['ob'], d_ub)
    dob_raw, dz, delta_b = _gate_bwd("gate_b_bwd", (d_ub, lw['w_br_b_t']), sv['ob_raw'], z, Z_BGATE // GATE_W,
                                     dz, Z_BGATE // GATE_W, T)
    nq, groups = S // MLA_T, HEADS // MLA_HGB
    delta_rows = delta_b[:, :HEADS].reshape(B, nq, MLA_T, groups, MLA_HGB).transpose(0, 3, 1, 4, 2)
    lse_rows = sv['lse_b'].transpose(0, 2, 1, 3, 4).reshape(B, nq, groups, MLA_HGB, MLA_T).transpose(0, 2, 1, 3, 4)
    dq, dk, dv = _mla_bwd(sv['qf'], sv['kf'], sv['vf'], dob_raw, lse_rows, delta_rows, B, S)
    dq_pre, dkv_pre, dz = _mla_prep_bwd(dq, dk, dv, tabs, dz, T)
    g['w_uq'] = _mm_tn("grad_uq", sv['qdn'], dq_pre)
    g['w_ukv'] = _mm_tn("grad_ukv", sv['kvdn'], dkv_pre)
    dz, g['g_q'] = _rms_bwd("norm_q_bwd", z, MLA_Q_LORA, Z_BQD // MLA_Q_LORA, sm['g_q'][i],
                            (dq_pre, lw['w_uq_t']), T, BF16, into=(dz, Z_BQD // MLA_Q_LORA))
    dz, g['g_kv'] = _rms_bwd("norm_kv_bwd", z, MLA_KV_LORA, Z_BKVD // MLA_KV_LORA, sm['g_kv'][i],
                             (dkv_pre, lw['w_ukv_t']), T, BF16, into=(dz, Z_BKVD // MLA_KV_LORA))
    doa_raw, dz, delta_a = _gate_bwd("gate_a_bwd", (d_ua, lw['w_br_a_t']), sv['oa_raw'], sv['a_gate'], 0,
                                     dz, Z_AGATE // GATE_W, T)
    dz, d_ak, d_av, dsink = _swa_bwd(z, pos_col, pos_row, sv['sink_row'], sv['lse_a'], doa_raw, delta_a, dz, B, S)
    dz = _kv_grad_cast(d_ak, d_av, dz, T)
    g['sink'] = dsink[0, :HEADS]
    g['w_in'] = _mm_tn("grad_in", sv['h'], dz, tk=1024, tn=Z_WIDTH // 2)
    *down, g['g_mix'] = _rms_bwd("norm_mix_bwd", sv['x0'], D_MODEL, 0, sm['g_mix'][i], (dz, lw['w_in_t']), T, F32,
                                 dres=dx1, ple=below)
    for name in ('g_ple', 'g_q', 'g_kv', 'g_mix'):
        g[name] = g[name][0]
    return down, g


def _local_step(x, p, positions, wfull, sm, loss_target):
    B, S, _ = x.shape
    T = B * S
    pos_col = positions.reshape(T, 1)
    pos_row = positions.reshape(T // BLOCK, 1, BLOCK)
    half = MLA_ROPE // 2
    inv = ROPE_THETA ** (-jnp.arange(0, MLA_ROPE, 2, dtype=F32) / MLA_ROPE)
    inv_lane = jnp.tile(inv, LANES // half).reshape(1, LANES)
    tabs = _rope_tables(pos_col, inv_lane, T)
    xc = x.reshape(T, D_MODEL)
    h = _rms_fwd("norm_mix", xc, D_MODEL, 0, sm['g_mix'][0], T)
    lws, saved = [], []
    for i in range(DEPTH):
        lw = _layer_weights(wfull, i)
        xc, h, sv = _layer_fwd(xc, h, p[i].reshape(T, PLE_DIM), lw, sm, i, pos_col, pos_row, tabs, B, S)
        lws.append(lw)
        saved.append(sv)
    ple = [(sv['u'], sv['e']) for sv in saved]
    *down, loss, dg_final = _loss_head(xc, sm['g_final'], loss_target.reshape(T, D_MODEL), ple[-1], T)
    layer_grads = [None] * DEPTH
    for i in reversed(range(DEPTH)):
        down, g = _layer_bwd(*down, saved[i], ple[i - 1] if i else None, lws[i], sm, i, pos_col, pos_row, tabs, B, S)
        layer_grads[i] = _unpad_grads(g)
    return loss, down[0].reshape(B, S, D_MODEL), layer_grads, dg_final[0]


SMALL_ROWS = 48


SMALL_SIZE = 2 * (2 * D_MODEL + HEADS + MLA_Q_LORA + MLA_KV_LORA) + D_MODEL


def _pack_small(arrs, tail=()):
    flat = jnp.concatenate([arrs[name].reshape(-1) for name in SMALL] + [t.reshape(1) for t in tail])
    return jnp.pad(flat, (0, SMALL_ROWS * LANES - flat.shape[0])).reshape(SMALL_ROWS, LANES)


def _unpack_small(block, shapes):
    flat = block.reshape(-1)
    out, off = {}, 0
    for name in SMALL:
        n = math.prod(shapes[name])
        out[name] = flat[off:off + n].reshape(shapes[name])
        off += n
    return out


def _flipped(shard_shape):
    return shard_shape[-1] % LANES != 0


def _to_slots(g, axis):
    r, c = g.shape
    if axis == 0:
        return g.reshape(N_CHIPS, r // N_CHIPS, c)
    return g.reshape(r, N_CHIPS, c // N_CHIPS).transpose(1, 0, 2)


def _div_tile(rows, cap):
    return next(t for t in range(min(cap, rows) // 8 * 8, 0, -8) if rows % t == 0)


def _units(shapes):
    units = []
    for w, shape in enumerate(shapes):
        r = shape[-2]
        n = next((n for n in (8, 4, 2) if r % (16 * n) == 0), 1) if r >= 512 else 1
        units += [(w, k * (r // n), r // n) for k in range(n)]
    return units


def _place():
    x, y, c = lax.axis_index("x"), lax.axis_index("y"), lax.axis_index("c")
    chips = [(1 - x, y), (x, 1 - y), (1 - x, 1 - y)]
    return x, y, c, chips


ANY = pl.BlockSpec(memory_space=pl.ANY)


def _remote(send_sems, recv_sems, k, src, dst, to):
    return pltpu.make_async_remote_copy(src_ref=src, dst_ref=dst, send_sem=send_sems.at[k],
                                        recv_sem=recv_sems.at[k], device_id=to, device_id_type=MESH)


def _gather_weights(shards, carried):
    n, nc = len(shards), len(carried)
    units = _units([s.shape for s in shards])
    nu = len(units)

    def body(*refs):
        ins, outs = refs[:n], refs[n + nc:2 * n + nc]
        send_sems, recv_sems, local_sems = refs[2 * (n + nc):]
        x, y, c, chips = _place()
        me = 2 * x + y
        sibling = (x, y, 1 - c)
        copy = functools.partial(_remote, send_sems, recv_sems)
        keeps, sends = [], []
        for u, (w, r0, nr) in enumerate(units):
            rows = pl.ds(r0, nr)
            keeps.append(pltpu.make_async_copy(ins[w].at[:, rows, :], outs[w].at[me, :, rows, :], local_sems.at[u]))
            keeps[-1].start()
        for j, (cx, cy) in enumerate(chips):
            for u, (w, r0, nr) in enumerate(units):
                rows = pl.ds(r0, nr)
                sends.append(copy(j * nu + u, ins[w].at[c, rows, :], outs[w].at[me, c, rows, :], (cx, cy, c)))
                sends[-1].start()
        for j, (cx, cy) in enumerate(chips):
            for u, (w, r0, nr) in enumerate(units):
                landed = outs[w].at[2 * cx + cy, c, pl.ds(r0, nr), :]
                copy(j * nu + u, landed, landed, (cx, cy, c)).wait_recv()
                sends.append(copy((3 + j) * nu + u, landed, landed, sibling))
                sends[-1].start()
        for j, (cx, cy) in enumerate(chips):
            for u, (w, r0, nr) in enumerate(units):
                other = outs[w].at[2 * cx + cy, 1 - c, pl.ds(r0, nr), :]
                copy((3 + j) * nu + u, other, other, sibling).wait_recv()
        for cp in sends:
            cp.wait_send()
        for keep in keeps:
            keep.wait()

    out_shape = [jax.ShapeDtypeStruct((N_CHIPS,) + s.shape, s.dtype) for s in shards]
    out_shape += [jax.ShapeDtypeStruct(a.shape, a.dtype) for a in carried]
    res = pl.pallas_call(
        body, name="gather_weights", out_shape=out_shape,
        in_specs=[ANY] * (n + nc), out_specs=[ANY] * (n + nc),
        input_output_aliases={n + k: n + k for k in range(nc)},
        scratch_shapes=[pltpu.SemaphoreType.DMA((6 * nu,)), pltpu.SemaphoreType.DMA((6 * nu,)),
                        pltpu.SemaphoreType.DMA((nu,))])(*shards, *carried)
    return res[:n], res[n:]


def _pair_exchange(g0, g1):
    n = len(g0)

    def body(*refs):
        layers, outs = (refs[:n], refs[n:2 * n]), refs[2 * n:3 * n]
        send_sems, recv_sems = refs[3 * n:]
        x, y, c, _ = _place()
        copy = functools.partial(_remote, send_sems, recv_sems)
        for w in range(n):
            for q in range(N_CHIPS):
                for layer in range(DEPTH):
                    cp = copy(N_CHIPS * w + q, layers[layer][w].at[q], outs[w].at[q], (x, y, 1 - c))
                    pl.when(c == 1 - layer)(cp.start)
        for w in range(n):
            for q in range(N_CHIPS):
                copy(N_CHIPS * w + q, layers[0][w].at[q], outs[w].at[q], (x, y, 1 - c)).wait()

    return pl.pallas_call(
        body, name="pair_exchange", out_shape=[jax.ShapeDtypeStruct(g.shape, g.dtype) for g in g0],
        in_specs=[ANY] * (2 * n), out_specs=[ANY] * n,
        scratch_shapes=[pltpu.SemaphoreType.DMA((N_CHIPS * n,)), pltpu.SemaphoreType.DMA((N_CHIPS * n,))])(*g0, *g1)


def _pair_sum(name, g0, g1, theirs, cflag):
    shape = theirs.shape
    rows, width = shape[0] * shape[1], shape[2]

    def body(ins, outs, _):
        mine = jnp.where(ins[3][0:1, 0:1] == 0.0, ins[0][...], ins[1][...])
        tot = mine + ins[2][...]
        outs[0][...] = tot
        outs[1][...] = tot.astype(BF16)
    ins = [(a.reshape(rows, width), width, 0) for a in (g0, g1, theirs)] + [(cflag, None, None)]
    f32, bf16 = _ew(name, body, ins, [(width, F32), (width, BF16)], rows, tm=_div_tile(rows, ROW_TILE))
    return f32.reshape(shape), bf16.reshape(shape)


def _chip_exchange(parts):
    n = len(parts)

    def body(*refs):
        ins, outs = refs[:n], refs[n:2 * n]
        send_sems, recv_sems = refs[2 * n:]
        x, y, c, chips = _place()
        copy = functools.partial(_remote, send_sems, recv_sems)
        sends = []
        for j, (cx, cy) in enumerate(chips):
            for w in range(n):
                sends.append(copy(j * n + w, ins[w].at[2 * cx + cy], outs[w].at[j], (cx, cy, c)))
                sends[-1].start()
        for j, (cx, cy) in enumerate(chips):
            for w in range(n):
                copy(j * n + w, outs[w].at[j], outs[w].at[j], (cx, cy, c)).wait_recv()
        for cp in sends:
            cp.wait_send()

    return pl.pallas_call(
        body, name="chip_exchange",
        out_shape=[jax.ShapeDtypeStruct((3,) + a.shape[1:], a.dtype) for a in parts],
        in_specs=[ANY] * n, out_specs=[ANY] * n,
        scratch_shapes=[pltpu.SemaphoreType.DMA((3 * n,)), pltpu.SemaphoreType.DMA((3 * n,))])(*parts)


def _chip_sum(name, part, landed, chipflag):
    _, r, width = part.shape
    tm = _div_tile(r, ROW_TILE // 2)

    def kern(p_ref, l_ref, flag_ref, o_ref):
        me = flag_ref[0:1, 0:1]
        own = jnp.where(me == 0.0, p_ref[0], jnp.where(me == 1.0, p_ref[1], jnp.where(me == 2.0, p_ref[2], p_ref[3])))
        o_ref[...] = ((own + l_ref[0].astype(F32)) + l_ref[1].astype(F32)) + l_ref[2].astype(F32)

    return pl.pallas_call(
        kern, name=name, grid=(r // tm,),
        in_specs=[pl.BlockSpec((N_CHIPS, tm, width), lambda i: (0, i, 0)),
                  pl.BlockSpec((3, tm, width), lambda i: (0, i, 0)),
                  pl.BlockSpec((1, LANES), lambda i: (0, 0))],
        out_specs=pl.BlockSpec((tm, width), lambda i: (i, 0)),
        out_shape=jax.ShapeDtypeStruct((r, width), F32), compiler_params=_params(("arbitrary",)))(part, landed, chipflag)


def _pair_broadcast(mine):
    n = len(mine)
    units = _units([a.shape for a in mine])

    def body(*refs):
        ins, outs = refs[:n], refs[n:2 * n]
        send_sems, recv_sems = refs[2 * n:]
        x, y, c, _ = _place()
        copy = functools.partial(_remote, send_sems, recv_sems)
        cps = [copy(u, ins[w].at[pl.ds(r0, nr), :], outs[w].at[pl.ds(r0, nr), :], (x, y, 1 - c))
               for u, (w, r0, nr) in enumerate(units)]
        for cp in cps:
            cp.start()
        for cp in cps:
            cp.wait()

    return pl.pallas_call(
        body, name="pair_broadcast", out_shape=[jax.ShapeDtypeStruct(a.shape, a.dtype) for a in mine],
        in_specs=[ANY] * n, out_specs=[ANY] * n,
        scratch_shapes=[pltpu.SemaphoreType.DMA((len(units),)), pltpu.SemaphoreType.DMA((len(units),))])(*mine)


def _small_allreduce(v):
    offsets = [(dx, dy, dc) for dx in (0, 1) for dy in (0, 1) for dc in (0, 1)][1:]

    def body(v_ref, out_ref, recv_ref, send_sems, recv_sems):
        x, y, c, _ = _place()
        flip = lambda a, d: 1 - a if d else a
        peers = [(flip(x, dx), flip(y, dy), flip(c, dc)) for dx, dy, dc in offsets]
        copy = functools.partial(_remote, send_sems, recv_sems)
        me = 4 * x + 2 * y + c
        recv_ref[me] = v_ref[...]
        cps = [copy(k, v_ref, recv_ref.at[me], peer) for k, peer in enumerate(peers)]
        for cp in cps:
            cp.start()
        for k, (px, py, pc) in enumerate(peers):
            landed = recv_ref.at[4 * px + 2 * py + pc]
            copy(k, landed, landed, (px, py, pc)).wait_recv()
        for cp in cps:
            cp.wait_send()
        tot = recv_ref[0]
        for d in range(1, 8):
            tot = tot + recv_ref[d]
        out_ref[...] = tot

    vmem = pl.BlockSpec(memory_space=pltpu.VMEM)
    return pl.pallas_call(
        body, name="small_allreduce", out_shape=jax.ShapeDtypeStruct(v.shape, v.dtype),
        in_specs=[vmem], out_specs=vmem,
        scratch_shapes=[pltpu.VMEM((8,) + v.shape, v.dtype), pltpu.SemaphoreType.DMA((7,)),
                        pltpu.SemaphoreType.DMA((7,))])(v)


def _adam_math(gv, wv, mv, vv):
    mv = ADAM_B1 * mv + (1.0 - ADAM_B1) * gv
    vv = ADAM_B2 * vv + (1.0 - ADAM_B2) * (gv * gv)
    m_hat = mv / (1.0 - ADAM_B1 ** ADAM_STEP)
    v_hat = vv / (1.0 - ADAM_B2 ** ADAM_STEP)
    return -ADAM_LR * (m_hat / (jnp.sqrt(v_hat) + ADAM_EPS) + ADAM_WD * wv), mv, vv


def _adamw_big(name, mine, theirs, cflag, w, m, v):
    _, r, width = w.shape
    tm = _div_tile(r, ROW_TILE // 2)

    def kern(mine_ref, theirs_ref, flag_ref, w_ref, m_ref, v_ref, g_ref, d_ref, nm_ref, nv_ref):
        layer = pl.program_id(0).astype(F32)
        gv = jnp.where(flag_ref[0:1, 0:1] == layer, mine_ref[...], theirs_ref[...])
        g_ref[0] = gv
        d_ref[0], nm_ref[0], nv_ref[0] = _adam_math(gv, w_ref[0], m_ref[0], v_ref[0])

    flat = pl.BlockSpec((tm, width), lambda l, i: (i, 0))
    stacked = pl.BlockSpec((1, tm, width), lambda l, i: (l, i, 0))
    return pl.pallas_call(
        kern, name=name, grid=(DEPTH, r // tm),
        in_specs=[flat, flat, pl.BlockSpec((1, LANES), lambda l, i: (0, 0)), stacked, stacked, stacked],
        out_specs=[stacked] * 4, out_shape=[jax.ShapeDtypeStruct(w.shape, F32)] * 4,
        compiler_params=_params(("arbitrary", "arbitrary")))(mine, theirs, cflag, w, m, v)


def _adamw_small(g, w, m, v):
    def body(ins, outs, _):
        outs[0][...], outs[1][...], outs[2][...] = _adam_math(*(r[...] for r in ins))
    return _ew("adamw_small", body, [(a, LANES, 0) for a in (g, w, m, v)], [(LANES, F32)] * 3, SMALL_ROWS)


def kernel(x, p, positions, g_mix, w_in, sink, g_q, w_uq, g_kv, w_ukv, w_br_a, w_br_b, w_out, g_ple, w_ple_gate, w_ple_proj, g_final, loss_target, m_g_mix, m_w_in, m_sink, m_g_q, m_w_uq, m_g_kv, m_w_ukv, m_w_br_a, m_w_br_b, m_w_out, m_g_ple, m_w_ple_gate, m_w_ple_proj, m_g_final, v_g_mix, v_w_in, v_sink, v_g_q, v_w_uq, v_g_kv, v_w_ukv, v_w_br_a, v_w_br_b, v_w_out, v_g_ple, v_w_ple_gate, v_w_ple_proj, v_g_final):
    w = dict(g_mix=g_mix, w_in=w_in, sink=sink, g_q=g_q, w_uq=w_uq, g_kv=g_kv, w_ukv=w_ukv, w_br_a=w_br_a,
             w_br_b=w_br_b, w_out=w_out, g_ple=g_ple, w_ple_gate=w_ple_gate, w_ple_proj=w_ple_proj, g_final=g_final)
    m = dict(g_mix=m_g_mix, w_in=m_w_in, sink=m_sink, g_q=m_g_q, w_uq=m_w_uq, g_kv=m_g_kv, w_ukv=m_w_ukv,
             w_br_a=m_w_br_a, w_br_b=m_w_br_b, w_out=m_w_out, g_ple=m_g_ple, w_ple_gate=m_w_ple_gate,
             w_ple_proj=m_w_ple_proj, g_final=m_g_final)
    v = dict(g_mix=v_g_mix, w_in=v_w_in, sink=v_sink, g_q=v_g_q, w_uq=v_w_uq, g_kv=v_g_kv, w_ukv=v_w_ukv,
             w_br_a=v_w_br_a, w_br_b=v_w_br_b, w_out=v_w_out, g_ple=v_g_ple, w_ple_gate=v_w_ple_gate,
             w_ple_proj=v_w_ple_proj, g_final=v_g_final)
    wfull = _gather_full(w)
    sm = {name: w[name] for name in SMALL}
    loss_row, grad_x, layer_grads, dg_final = _local_step(x, p, positions, wfull, sm, loss_target)
    res, loss = _update(layer_grads, dg_final, loss_row[0, 0], w, m, v)
    return (loss, grad_x, *[res[name][kind] for kind in range(4) for name in WEIGHT_NAMES])


def _gather_behind(name, collective_id, shards):
    n = len(shards)
    srcs = [jax.new_ref(s, memory_space=pltpu.MemorySpace.HBM) for s in shards]
    lands = [jax.empty_ref(jax.ShapeDtypeStruct((N_CHIPS,) + s.shape, s.dtype), memory_space=pltpu.MemorySpace.HBM)
             for s in shards]

    @pl.kernel(mesh=plsc.ScalarSubcoreMesh(axis_name="sequencer", num_cores=1), name=name,
               scratch_types=(pltpu.SemaphoreType.DMA((3 * n,)), pltpu.SemaphoreType.DMA((3 * n,)),
                              pltpu.SemaphoreType.DMA((n,))),
               compiler_params=pltpu.CompilerParams(collective_id=collective_id))
    def launch(send_sems, recv_sems, local_sems):
        x, y, c, chips = _place()
        me = 2 * x + y
        barrier = pltpu.get_barrier_semaphore()
        for cx, cy in chips:
            pl.semaphore_signal(barrier, inc=1, device_id=(cx, cy, c), device_id_type=MESH)
        pl.semaphore_wait(barrier, len(chips))
        copy = functools.partial(_remote, send_sems, recv_sems)
        keeps = [pltpu.make_async_copy(srcs[w], lands[w].at[me], local_sems.at[w]) for w in range(n)]
        cps = [copy(j * n + w, srcs[w], lands[w].at[me], (cx, cy, c))
               for j, (cx, cy) in enumerate(chips) for w in range(n)]
        for cp in keeps + cps:
            cp.start()
        for cp in keeps + cps:
            cp.wait()

    launch()
    return [land[...] for land in lands]


def _gather_full(w):
    shards = [w[name].astype(BF16) for name, _ in SHARDED]
    w_in0 = shards[0][0]
    first, later = _gather_weights([w_in0.reshape((2, w_in0.shape[0] // 2) + w_in0.shape[1:])],
                                   [s[0] for s in shards[1:]] + [s[1] for s in shards])
    n_rest = len(shards) - 1
    layer0 = [first[0].reshape((N_CHIPS,) + w_in0.shape)] + _gather_behind("gather_rest", 0, later[:n_rest])
    layer1 = _gather_behind("gather_next", 1, later[n_rest:])
    return {name: [jnp.concatenate(list(blocks[k]), axis=axis - 1) for blocks in (layer0, layer1)]
            for k, (name, axis) in enumerate(SHARDED)}


def _update(layer_grads, dg_final, loss_local, w, m, v):
    small_shapes = {name: w[name].shape for name in SMALL}
    cflag = jnp.full((1, LANES), lax.axis_index("c"), F32)
    chipflag = jnp.full((1, LANES), 2 * lax.axis_index("x") + lax.axis_index("y"), F32)

    slots = [[_to_slots(layer_grads[layer][name], axis - 1) for name, axis in SHARDED] for layer in range(DEPTH)]
    theirs = _pair_exchange(slots[0], slots[1])
    pair = [_pair_sum("pair_sum_" + name, slots[0][k], slots[1][k], theirs[k], cflag)
            for k, (name, _) in enumerate(SHARDED)]
    landed = _chip_exchange([bf16 for _, bf16 in pair])
    mine = [_chip_sum("chip_sum_" + name, pair[k][0], landed[k], chipflag) for k, (name, _) in enumerate(SHARDED)]
    other = _pair_broadcast(mine)
    res = {}
    for k, (name, _) in enumerate(SHARDED):
        flip = _flipped(w[name].shape)
        view = (lambda a: jnp.swapaxes(a, -1, -2)) if flip else (lambda a: a)
        outs = _adamw_big("adamw_" + name, view(mine[k]), view(other[k]), cflag, view(w[name]), view(m[name]),
                          view(v[name]))
        res[name] = tuple(view(a) for a in outs)

    gsmall = {name: jnp.stack([layer_grads[layer][name] for layer in range(DEPTH)]) for name in SMALL[:-1]}
    gsmall['g_final'] = dg_final
    gsum = _small_allreduce(_pack_small(gsmall, tail=[loss_local]))
    small = (gsum,) + tuple(_adamw_small(gsum, _pack_small(w), _pack_small(m), _pack_small(v)))
    for name, arrs in zip(SMALL, zip(*[[_unpack_small(a, small_shapes)[n] for n in SMALL] for a in small])):
        res[name] = arrs
    return res, gsum.reshape(-1)[SMALL_SIZE]
```

```python
import functools
import math

import jax
import jax.numpy as jnp
from jax import lax
from jax.experimental import pallas as pl
from jax.experimental.pallas import tpu as pltpu
from jax.experimental.pallas import tpu_sc as plsc

F32 = jnp.float32
BF16 = jnp.bfloat16

D_MODEL = 1024
DEPTH = 2
PLE_DIM = 256
BLOCK = 128
EPS = 1e-6
NEG = -1e30
HEADS = 8
SWA_KV_HEADS = 2
HEAD_DIM = 64
LANES = 128
HPAD = HEADS * LANES
MLA_QK = 96
MLA_ROPE = 32
MLA_Q_LORA = 256
MLA_KV_LORA = 128
ROPE_THETA = 10000.0
IN_SIZES = (512, 128, 128, 512, 256, 128, 32, 512, 1024, 1024)

Z_MA, Z_MB, Z_AQ, Z_AGATE, Z_BGATE = 0, 1024, 2048, 3072, 3584
Z_AK, Z_AV, Z_BQD, Z_BKVD, Z_BKR = 4096, 4352, 4608, 4864, 4992
Z_WIDTH = 5120
GATE_W = HEADS * HEAD_DIM
KV_W = SWA_KV_HEADS * LANES

ADAM_LR, ADAM_B1, ADAM_B2, ADAM_EPS, ADAM_WD, ADAM_STEP = 0.001, 0.9, 0.999, 1e-08, 0.01, 10

VMEM_LIMIT = 56 * 1024 * 1024
MESH = pl.DeviceIdType.MESH

WEIGHT_NAMES = ('g_mix', 'w_in', 'sink', 'g_q', 'w_uq', 'g_kv', 'w_ukv', 'w_br_a', 'w_br_b',
                'w_out', 'g_ple', 'w_ple_gate', 'w_ple_proj', 'g_final')
SHARDED = (('w_in', 2), ('w_uq', 2), ('w_ukv', 2), ('w_br_a', 2), ('w_br_b', 2),
           ('w_out', 1), ('w_ple_gate', 1), ('w_ple_proj', 2))
SMALL = ('g_mix', 'sink', 'g_q', 'g_kv', 'g_ple', 'g_final')
N_CHIPS = 4


def _params(sem):
    return pltpu.CompilerParams(dimension_semantics=sem, vmem_limit_bytes=VMEM_LIMIT)


MM_TN = 512
ROW_TILE = 512
BIG_WEIGHT_BYTES = 16 * 1024 * 1024


def _row_tile(rows, weight_bytes=0):
    tm = ROW_TILE // 2 if weight_bytes > BIG_WEIGHT_BYTES else ROW_TILE
    return min(tm, rows)


def _ew(name, body, ins, outs, rows, accs=(), mms=(), tm=None):
    n_mm, n_in, n_out = len(mms), len(ins), len(outs)
    if tm is None:
        tm = _row_tile(rows, sum(b.size * b.dtype.itemsize for _, b in mms))
    in_specs, args = [], []
    for a, b in mms:
        in_specs += [pl.BlockSpec((tm, a.shape[1]), lambda i: (i, 0)), pl.BlockSpec(b.shape, lambda i: (0, 0))]
        args += [a, b]
    for arr, width, cb in ins:
        if width is None:
            in_specs.append(pl.BlockSpec(arr.shape, lambda i, nd=arr.ndim: (0,) * nd))
        else:
            in_specs.append(pl.BlockSpec((tm, width), lambda i, cb=cb: (i, cb)))
        args.append(arr)
    out_shape, out_specs, aliases = [], [], {}
    for k, out in enumerate(outs):
        if len(out) == 4:
            aliases[len(args)] = k
            in_specs.append(pl.BlockSpec(memory_space=pl.ANY))
            args.append(out[2])
            out_shape.append(jax.ShapeDtypeStruct(out[2].shape, out[2].dtype))
            out_specs.append(pl.BlockSpec((tm, out[0]), lambda i, cb=out[3]: (i, cb)))
        else:
            out_shape.append(jax.ShapeDtypeStruct((rows, out[0]), out[1]))
            out_specs.append(pl.BlockSpec((tm, out[0]), lambda i: (i, 0)))
    n_in += len(aliases)
    out_shape += [jax.ShapeDtypeStruct(s, F32) for s in accs]
    out_specs += [pl.BlockSpec(s, lambda i: (0, 0)) for s in accs]

    def kern(*refs):
        mm_refs, refs = refs[:2 * n_mm], refs[2 * n_mm:]
        in_refs, out_refs = refs[:n_in - len(aliases)], refs[n_in:n_in + n_out]
        acc_refs, prod_refs = refs[n_in + n_out:n_in + n_out + len(accs)], refs[n_in + n_out + len(accs):]
        if acc_refs:
            @pl.when(pl.program_id(0) == 0)
            def _():
                for r in acc_refs:
                    r[...] = jnp.zeros_like(r)
        for k in range(n_mm):
            a_ref, b_ref, prod = mm_refs[2 * k], mm_refs[2 * k + 1], prod_refs[k]
            av = a_ref[...].astype(BF16)
            n = b_ref.shape[1]
            tn = min(MM_TN, n)
            for j in range(n // tn):
                cols = slice(j * tn, (j + 1) * tn)
                prod[:, cols] = jnp.dot(av, b_ref[:, cols], preferred_element_type=F32)
        body(tuple(prod_refs) + tuple(in_refs), out_refs, acc_refs)

    scratch = [pltpu.VMEM((tm, b.shape[1]), F32) for _, b in mms]
    res = pl.pallas_call(kern, name=name, grid=(rows // tm,), in_specs=in_specs, out_specs=out_specs,
                         out_shape=out_shape, scratch_shapes=scratch, input_output_aliases=aliases,
                         compiler_params=_params(("arbitrary",)))(*args)
    return res


def _rms(xv, gv):
    r = lax.rsqrt(jnp.mean(xv * xv, axis=-1, keepdims=True) + EPS)
    return ((xv * r) * gv).astype(BF16)


def _rms_fwd(name, x, width, cb, g, rows):
    def body(ins, outs, _):
        outs[0][...] = _rms(ins[0][...].astype(F32), ins[1][...])
    return _ew(name, body, [(x, width, cb), (g.reshape(1, width), None, None)], [(width, BF16)], rows)[0]


def _ple_grads(d, u, e):
    s = jax.nn.sigmoid(u.astype(F32))
    return (d * s).astype(BF16), (d * e.astype(F32) * (s * (1.0 - s))).astype(BF16)


def _rms_bwd(name, x, width, cb, g, dh_mm, rows, out_dtype, dres=None, into=(), ple=None):
    def body(ins, outs, accs):
        dhv, xv, gv = ins[0][...], ins[1][...].astype(F32), ins[2][...]
        r = lax.rsqrt(jnp.mean(xv * xv, axis=-1, keepdims=True) + EPS)
        xhat = xv * r
        accs[0][...] += jnp.sum(dhv * xhat, axis=0, keepdims=True)
        dy = dhv * gv
        dx = r * (dy - xhat * jnp.mean(dy * xhat, axis=-1, keepdims=True))
        if dres is not None:
            dx = dx + ins[3][...]
        outs[0][...] = dx.astype(out_dtype)
        if ple is not None:
            outs[1][...], outs[2][...] = _ple_grads(dx, ins[-2][...], ins[-1][...])
    ins = [(x, width, cb), (g.reshape(1, width), None, None)]
    if dres is not None:
        ins.append((dres, width, 0))
    outs = [(width, out_dtype) + tuple(into)]
    if ple is not None:
        ins += [(ple[0], width, 0), (ple[1], width, 0)]
        outs += [(width, BF16), (width, BF16)]
    return _ew(name, body, ins, outs, rows, accs=[(1, width)], mms=[dh_mm])


def _mm(name, a, b, out_dtype, f32_cols=None, norms=(), tn=MM_TN):
    M, K = a.shape
    N = b.shape[1]
    tm, tn = _row_tile(M, b.size * b.dtype.itemsize), min(tn, N)
    c0, cw = f32_cols if f32_cols else (0, 0)
    n_norm, n_f32 = len(norms), 1 if f32_cols else 0
    assert c0 % tn == 0 and cw % tn == 0
    assert all(nc // tn == (nc + g.shape[-1] - 1) // tn for nc, g in norms)

    def kern(*refs):
        a_ref, b_ref, g_refs = refs[0], refs[1], refs[2:2 + n_norm]
        o_ref, extra = refs[2 + n_norm], refs[3 + n_norm:]
        av = a_ref[...].astype(BF16)
        for j in range(N // tn):
            cols = slice(j * tn, (j + 1) * tn)
            part = jnp.dot(av, b_ref[:, cols], preferred_element_type=F32)
            o_ref[:, cols] = part.astype(o_ref.dtype)
            if f32_cols and c0 <= j * tn and (j + 1) * tn <= c0 + cw:
                extra[0][:, j * tn - c0:(j + 1) * tn - c0] = part
            for k, (nc, g) in enumerate(norms):
                if nc // tn == j:
                    seg = part[:, nc - j * tn:nc - j * tn + g.shape[-1]]
                    extra[n_f32 + k][...] = _rms(seg, g_refs[k][...])

    in_specs = [pl.BlockSpec((tm, K), lambda i: (i, 0)), pl.BlockSpec((K, N), lambda i: (0, 0))]
    in_specs += [pl.BlockSpec((1, g.shape[-1]), lambda i: (0, 0)) for _, g in norms]
    widths = [(N, out_dtype)] + ([(cw, F32)] if f32_cols else []) + [(g.shape[-1], BF16) for _, g in norms]
    return pl.pallas_call(
        kern, name=name, grid=(M // tm,), in_specs=in_specs,
        out_specs=[pl.BlockSpec((tm, w), lambda i: (i, 0)) for w, _ in widths],
        out_shape=[jax.ShapeDtypeStruct((M, w), dt) for w, dt in widths],
        compiler_params=_params(("parallel",)))(a, b, *[g.reshape(1, -1) for _, g in norms])


def _mm_tn(name, a, b, tk=2048, tn=2048):
    T, M = a.shape
    N = b.shape[1]
    tn, tk = min(tn, N), min(tk, T)

    def kern(a_ref, b_ref, o_ref):
        k = pl.program_id(1)
        part = _dot_tn(a_ref[...].astype(BF16), b_ref[...].astype(BF16))

        @pl.when(k == 0)
        def _():
            o_ref[...] = part

        @pl.when(k > 0)
        def _():
            o_ref[...] += part

    return pl.pallas_call(
        kern, name=name, grid=(N // tn, T // tk),
        in_specs=[pl.BlockSpec((tk, M), lambda j, k: (k, 0)), pl.BlockSpec((tk, tn), lambda j, k: (k, j))],
        out_specs=pl.BlockSpec((M, tn), lambda j, k: (0, j)),
        out_shape=jax.ShapeDtypeStruct((M, N), F32),
        compiler_params=_params(("parallel", "arbitrary")))(a, b)


def _dot_nt(a, b):
    return lax.dot_general(a, b, (((1,), (1,)), ((), ())), preferred_element_type=F32)


def _dot_tn(a, b):
    return lax.dot_general(a, b, (((0,), (0,)), ((), ())), preferred_element_type=F32)


SWA_SCALE = HEAD_DIM ** -0.5


def _swa_band(n, pq_ref, pkp_ref, pkc_ref):
    posk = jnp.concatenate([pkp_ref[...], pkc_ref[...]], axis=0)
    dist = (pq_ref[0] - posk).astype(F32)
    kj = lax.broadcasted_iota(jnp.int32, (2 * BLOCK, BLOCK), 0)
    qi = lax.broadcasted_iota(jnp.int32, (2 * BLOCK, BLOCK), 1)
    t_abs = n * BLOCK + qi
    s_abs = n * BLOCK - BLOCK + kj
    return dist, (s_abs >= 0) & (s_abs <= t_abs) & (t_abs - s_abs < BLOCK)


SWA_GROUP = HEADS // SWA_KV_HEADS


def _head_gate(gate_ref, h):
    pair = gate_ref[:, (h // 2) * LANES:(h // 2 + 1) * LANES].astype(F32)
    return pair if h % 2 == 0 else pltpu.roll(pair, HEAD_DIM, 1)


def _swa_group_q(q_all, g):
    heads = range(g * SWA_GROUP, (g + 1) * SWA_GROUP)
    return jnp.concatenate([(q_all[:, h * LANES:(h + 1) * LANES] * SWA_SCALE).astype(BF16) for h in heads], axis=0)


def _swa_mask(s, dist, valid, h):
    return jnp.where(valid, s - (2.0 ** -(h + 1)) * dist, NEG)


def _rows_to_lanes(rows):
    block = jnp.concatenate(list(rows) + [jnp.zeros((LANES - len(rows), BLOCK), F32)], axis=0)
    return block.T


def _swa_specs(nb):
    prev = lambda b, n: b * nb + jnp.maximum(n - 1, 0)
    own = lambda b, n: b * nb + n
    return [
        pl.BlockSpec((BLOCK, HPAD), lambda b, n: (own(b, n), Z_AQ // HPAD)),
        pl.BlockSpec((BLOCK, KV_W), lambda b, n: (prev(b, n), Z_AK // KV_W)),
        pl.BlockSpec((BLOCK, KV_W), lambda b, n: (own(b, n), Z_AK // KV_W)),
        pl.BlockSpec((BLOCK, KV_W), lambda b, n: (prev(b, n), Z_AV // KV_W)),
        pl.BlockSpec((BLOCK, KV_W), lambda b, n: (own(b, n), Z_AV // KV_W)),
        pl.BlockSpec((1, 1, BLOCK), lambda b, n: (own(b, n), 0, 0)),
        pl.BlockSpec((BLOCK, 1), lambda b, n: (prev(b, n), 0)),
        pl.BlockSpec((BLOCK, 1), lambda b, n: (own(b, n), 0)),
    ]


def _swa_fwd(z, gate, pos_col, pos_row, sink_row, B, S):
    nb = S // BLOCK
    T = B * S

    def kern(q_ref, kp_ref, kc_ref, vp_ref, vc_ref, pq_ref, pkp_ref, pkc_ref, gate_ref, sink_ref,
             oraw_ref, og_ref, lse_ref):
        q_all = q_ref[...]
        kb = jnp.concatenate([kp_ref[...], kc_ref[...]], axis=0).astype(BF16)
        vb = jnp.concatenate([vp_ref[...], vc_ref[...]], axis=0).astype(BF16)
        dist, valid = _swa_band(pl.program_id(1), pq_ref, pkp_ref, pkc_ref)
        lse_rows = []
        for grp in range(SWA_KV_HEADS):
            gcols = slice(grp * LANES, (grp + 1) * LANES)
            s_all = _dot_nt(kb[:, gcols], _swa_group_q(q_all, grp))
            probs = []
            for hh in range(SWA_GROUP):
                h = grp * SWA_GROUP + hh
                s = _swa_mask(s_all[:, hh * BLOCK:(hh + 1) * BLOCK], dist, valid, h)
                sink_h = sink_ref[0:1, h:h + 1]
                m = jnp.maximum(jnp.max(s, axis=0, keepdims=True), sink_h)
                e = jnp.exp(s - m)
                denom = jnp.sum(e, axis=0, keepdims=True) + jnp.exp(sink_h - m)
                probs.append((e * (1.0 / denom)).astype(BF16))
                lse_rows.append(m + jnp.log(denom))
            o_all = jnp.dot(vb[:, gcols].T, jnp.concatenate(probs, axis=1), preferred_element_type=F32)
            for hh in range(SWA_GROUP):
                h = grp * SWA_GROUP + hh
                cols = slice(h * LANES, (h + 1) * LANES)
                o = o_all[:, hh * BLOCK:(hh + 1) * BLOCK].T
                oraw_ref[:, cols] = o
                g = _head_gate(gate_ref, h)
                og_ref[:, cols] = (o * (g * jax.nn.sigmoid(g))).astype(BF16)
        lse_ref[...] = _rows_to_lanes(lse_rows)

    own = lambda b, n: b * nb + n
    in_specs = _swa_specs(nb) + [
        pl.BlockSpec((BLOCK, GATE_W), lambda b, n: (own(b, n), 0)),
        pl.BlockSpec((1, LANES), lambda b, n: (0, 0)),
    ]
    out_specs = [pl.BlockSpec((BLOCK, HPAD), lambda b, n: (own(b, n), 0)),
                 pl.BlockSpec((BLOCK, HPAD), lambda b, n: (own(b, n), 0)),
                 pl.BlockSpec((BLOCK, LANES), lambda b, n: (own(b, n), 0))]
    out_shape = [jax.ShapeDtypeStruct((T, HPAD), F32), jax.ShapeDtypeStruct((T, HPAD), BF16),
                 jax.ShapeDtypeStruct((T, LANES), F32)]
    return pl.pallas_call(kern, name="swa_fwd", grid=(B, nb), in_specs=in_specs, out_specs=out_specs,
                          out_shape=out_shape, compiler_params=_params(("parallel", "arbitrary")))(
        z, z, z, z, z, pos_row, pos_col, pos_col, gate, sink_row)


def _swa_bwd(z, pos_col, pos_row, sink_row, lse, do_raw, delta, dz, B, S):
    nb = S // BLOCK
    T = B * S

    def kern(q_ref, kp_ref, kc_ref, vp_ref, vc_ref, pq_ref, pkp_ref, pkc_ref, sink_ref, lse_ref, do_ref,
             delta_ref, dz_ref, dq_ref, dk_ref, dv_ref, dsink_ref):
        b, n = pl.program_id(0), pl.program_id(1)

        @pl.when(n == 0)
        def _():
            dk_ref[...] = jnp.zeros_like(dk_ref)
            dv_ref[...] = jnp.zeros_like(dv_ref)

        @pl.when((b == 0) & (n == 0))
        def _():
            dsink_ref[...] = jnp.zeros_like(dsink_ref)

        q_all = q_ref[...]
        kb = jnp.concatenate([kp_ref[...], kc_ref[...]], axis=0).astype(BF16)
        vb = jnp.concatenate([vp_ref[...], vc_ref[...]], axis=0).astype(BF16)
        dist, valid = _swa_band(n, pq_ref, pkp_ref, pkc_ref)
        lse_t, delta_t = lse_ref[...].T, delta_ref[...].T
        lane1 = lax.broadcasted_iota(jnp.int32, (1, LANES), 1)
        dsink = jnp.zeros((1, LANES), F32)
        dk_band, dv_band = [], []
        for grp in range(SWA_KV_HEADS):
            gcols = slice(grp * LANES, (grp + 1) * LANES)
            heads = range(grp * SWA_GROUP, (grp + 1) * SWA_GROUP)
            qg = _swa_group_q(q_all, grp)
            dog = jnp.concatenate([do_ref[:, h * LANES:(h + 1) * LANES] for h in heads], axis=0)
            s_all = _dot_nt(kb[:, gcols], qg)
            dp_all = _dot_nt(vb[:, gcols], dog)
            ps, dss = [], []
            for hh, h in enumerate(heads):
                blk = slice(hh * BLOCK, (hh + 1) * BLOCK)
                lse_h, delta_h = lse_t[h:h + 1, :], delta_t[h:h + 1, :]
                p = jnp.exp(_swa_mask(s_all[:, blk], dist, valid, h) - lse_h)
                ps.append(p.astype(BF16))
                dss.append((p * (dp_all[:, blk] - delta_h)).astype(BF16))
                psink = jnp.exp(sink_ref[0:1, h:h + 1] - lse_h)
                dsink = dsink + jnp.where(lane1 == h, -jnp.sum(psink * delta_h, axis=1, keepdims=True), 0.0)
            dsg = jnp.concatenate(dss, axis=1)
            dq_all = jnp.dot(kb[:, gcols].T, dsg, preferred_element_type=F32) * SWA_SCALE
            for hh, h in enumerate(heads):
                dq_ref[:, h * LANES:(h + 1) * LANES] = dq_all[:, hh * BLOCK:(hh + 1) * BLOCK].T.astype(BF16)
            dk_band.append(jnp.dot(dsg, qg, preferred_element_type=F32))
            dv_band.append(jnp.dot(jnp.concatenate(ps, axis=1), dog, preferred_element_type=F32))
        dsink_ref[...] += dsink
        dkb = jnp.concatenate(dk_band, axis=1)
        dvb = jnp.concatenate(dv_band, axis=1)
        r_prev = pl.ds(pl.multiple_of(jnp.maximum(n - 1, 0) * BLOCK, BLOCK), BLOCK)
        r_own = pl.ds(pl.multiple_of(n * BLOCK, BLOCK), BLOCK)
        dk_ref[r_prev, :] += dkb[:BLOCK]
        dk_ref[r_own, :] += dkb[BLOCK:]
        dv_ref[r_prev, :] += dvb[:BLOCK]
        dv_ref[r_own, :] += dvb[BLOCK:]

    own = lambda b, n: b * nb + n
    in_specs = _swa_specs(nb) + [
        pl.BlockSpec((1, LANES), lambda b, n: (0, 0)),
        pl.BlockSpec((BLOCK, LANES), lambda b, n: (own(b, n), 0)),
        pl.BlockSpec((BLOCK, HPAD), lambda b, n: (own(b, n), 0)),
        pl.BlockSpec((BLOCK, LANES), lambda b, n: (own(b, n), 0)),
        pl.BlockSpec(memory_space=pl.ANY),
    ]
    out_specs = [pl.BlockSpec((BLOCK, HPAD), lambda b, n: (own(b, n), Z_AQ // HPAD)),
                 pl.BlockSpec((S, KV_W), lambda b, n: (b, 0)),
                 pl.BlockSpec((S, KV_W), lambda b, n: (b, 0)),
                 pl.BlockSpec((1, LANES), lambda b, n: (0, 0))]
    out_shape = [jax.ShapeDtypeStruct(dz.shape, dz.dtype), jax.ShapeDtypeStruct((T, KV_W), F32),
                 jax.ShapeDtypeStruct((T, KV_W), F32), jax.ShapeDtypeStruct((1, LANES), F32)]
    return pl.pallas_call(kern, name="swa_bwd", grid=(B, nb), in_specs=in_specs, out_specs=out_specs,
                          out_shape=out_shape, input_output_aliases={len(in_specs) - 1: 0},
                          compiler_params=_params(("arbitrary", "arbitrary")))(
        z, z, z, z, z, pos_row, pos_col, pos_col, sink_row, lse, do_raw, delta, dz)


MLA_T = 256
MLA_HG = 8
MLA_W = MLA_HG * LANES
MLA_HGB = 8
MLA_WB = MLA_HGB * LANES
MLA_SCALE = MLA_QK ** -0.5
LOG2E = 1.4426950408889634
MLA_QSCALE = MLA_SCALE * LOG2E


def _causal_t(s):
    key = lax.broadcasted_iota(jnp.int32, s.shape, 0)
    query = lax.broadcasted_iota(jnp.int32, s.shape, 1)
    return jnp.where(key <= query, s, NEG)


def _mla_fwd(q, k, v, z, B, S):
    T = B * S
    nq = S // MLA_T

    def kern(q_ref, k_ref, v_ref, gate_ref, oraw_ref, og_ref, lse_ref):
        i = pl.program_id(2)

        def scores(j):
            rows = pl.ds(pl.multiple_of(j * MLA_T, MLA_T), MLA_T)
            return tuple(_dot_nt(k_ref[rows, hh * LANES:(hh + 1) * LANES], q_ref[:, hh * LANES:(hh + 1) * LANES])
                         for hh in range(MLA_HG))

        def update(j, ss, state):
            rows = pl.ds(pl.multiple_of(j * MLA_T, MLA_T), MLA_T)
            out = []
            for hh in range(MLA_HG):
                (m, l, acc), s = state[hh], ss[hh]
                m_new = jnp.maximum(m, jnp.max(s, axis=0, keepdims=True))
                alpha = jnp.exp2(m - m_new)
                p = jnp.exp2(s - m_new)
                l = alpha * l + jnp.sum(p, axis=0, keepdims=True)
                pv = jnp.dot(v_ref[rows, hh * LANES:(hh + 1) * LANES].T, p.astype(BF16), preferred_element_type=F32)
                out.append((m_new, l, alpha * acc + pv))
            return tuple(out)

        def body(pair, state):
            j = 2 * pair
            s0, s1 = scores(j), scores(j + 1)
            return update(j + 1, s1, update(j, s0, state))

        init = tuple((jnp.full((1, MLA_T), NEG, F32), jnp.zeros((1, MLA_T), F32), jnp.zeros((LANES, MLA_T), F32))
                     for _ in range(MLA_HG))
        state = lax.fori_loop(0, i // 2, body, init)
        state = lax.cond(i % 2 == 1, lambda st: update(i - 1, scores(i - 1), st), lambda st: st, state)
        state = update(i, tuple(_causal_t(s) for s in scores(i)), state)
        for hh in range(MLA_HG):
            m, l, acc = state[hh]
            cols = slice(hh * LANES, (hh + 1) * LANES)
            o = (acc * (1.0 / l)).T
            oraw_ref[:, cols] = o.astype(BF16)
            g = _head_gate(gate_ref, hh)
            og_ref[:, cols] = (o * (g * jax.nn.sigmoid(g))).astype(BF16)
            lse_ref[0, 0, 0, hh:hh + 1, :] = m + jnp.log2(l)

    blk = lambda b, h, i: (b * nq + i, h)
    in_specs = [pl.BlockSpec((MLA_T, MLA_W), blk),
                pl.BlockSpec((S, MLA_W), lambda b, h, i: (b, h)),
                pl.BlockSpec((S, MLA_W), lambda b, h, i: (b, h)),
                pl.BlockSpec((MLA_T, MLA_W // 2), lambda b, h, i: (b * nq + i, Z_BGATE // (MLA_W // 2) + h))]
    out_specs = [pl.BlockSpec((MLA_T, MLA_W), blk), pl.BlockSpec((MLA_T, MLA_W), blk),
                 pl.BlockSpec((1, 1, 1, MLA_HG, MLA_T), lambda b, h, i: (b, h, i, 0, 0))]
    out_shape = [jax.ShapeDtypeStruct((T, HPAD), BF16), jax.ShapeDtypeStruct((T, HPAD), BF16),
                 jax.ShapeDtypeStruct((B, HEADS // MLA_HG, nq, MLA_HG, MLA_T), F32)]
    return pl.pallas_call(kern, name="mla_fwd", grid=(B, HEADS // MLA_HG, nq), in_specs=in_specs,
                          out_specs=out_specs, out_shape=out_shape,
                          compiler_params=_params(("parallel", "parallel", "arbitrary")))(q, k, v, z)


def _mla_bwd(q, k, v, do_raw, lse, delta, B, S):
    T = B * S
    nk = S // MLA_T

    def kern(q_ref, k_ref, v_ref, do_ref, lse_ref, delta_ref, dq_ref, dk_ref, dv_ref, dq_acc, dk_acc, dv_acc):
        j = pl.program_id(2)

        @pl.when(j == 0)
        def _():
            dq_acc[...] = jnp.zeros_like(dq_acc)

        dk_acc[...] = jnp.zeros_like(dk_acc)
        dv_acc[...] = jnp.zeros_like(dv_acc)
        kts = [k_ref[:, hh * LANES:(hh + 1) * LANES].T for hh in range(MLA_HGB)]

        def step(i, masked):
            rows = pl.ds(pl.multiple_of(i * MLA_T, MLA_T), MLA_T)
            for hh in range(MLA_HGB):
                cols = slice(hh * LANES, (hh + 1) * LANES)
                qv, do = q_ref[rows, cols], do_ref[rows, cols]
                st = _dot_nt(k_ref[:, cols], qv)
                if masked:
                    st = _causal_t(st)
                pt = jnp.exp2(st - lse_ref[0, 0, i, hh:hh + 1, :])
                dpt = _dot_nt(v_ref[:, cols], do)
                dst = (pt * (dpt - delta_ref[0, 0, i, hh:hh + 1, :])).astype(BF16)
                dv_acc[:, cols] += jnp.dot(pt.astype(BF16), do, preferred_element_type=F32)
                dk_acc[:, cols] += jnp.dot(dst, qv, preferred_element_type=F32)
                dq_acc[hh, i] += jnp.dot(kts[hh], dst, preferred_element_type=F32)

        step(j, True)

        def body(i, c):
            step(i, False)
            return c

        lax.fori_loop(j + 1, nk, body, 0)
        dk_ref[...] = (dk_acc[...] * (1.0 / LOG2E)).astype(BF16)
        dv_ref[...] = dv_acc[...].astype(BF16)

        @pl.when(j == nk - 1)
        def _():
            for hh in range(MLA_HGB):
                for t in range(nk):
                    dq_ref[t * MLA_T:(t + 1) * MLA_T, hh * LANES:(hh + 1) * LANES] = dq_acc[hh, t].T.astype(BF16)

    whole = lambda b, h, j: (b, h)
    tile = lambda b, h, j: (b * nk + j, h)
    stats = pl.BlockSpec((1, 1, nk, MLA_HGB, MLA_T), lambda b, h, j: (b, h, 0, 0, 0))
    in_specs = [pl.BlockSpec((S, MLA_WB), whole), pl.BlockSpec((MLA_T, MLA_WB), tile),
                pl.BlockSpec((MLA_T, MLA_WB), tile), pl.BlockSpec((S, MLA_WB), whole), stats, stats]
    out_specs = [pl.BlockSpec((S, MLA_WB), whole), pl.BlockSpec((MLA_T, MLA_WB), tile),
                 pl.BlockSpec((MLA_T, MLA_WB), tile)]
    out_shape = [jax.ShapeDtypeStruct((T, HPAD), BF16)] * 3
    scratch = [pltpu.VMEM((MLA_HGB, nk, LANES, MLA_T), F32), pltpu.VMEM((MLA_T, MLA_WB), F32),
               pltpu.VMEM((MLA_T, MLA_WB), F32)]
    return pl.pallas_call(kern, name="mla_bwd", grid=(B, HEADS // MLA_HGB, nk), in_specs=in_specs,
                          out_specs=out_specs, out_shape=out_shape, scratch_shapes=scratch,
                          compiler_params=_params(("parallel", "parallel", "arbitrary")))(
        q, k, v, do_raw, lse, delta)


def _rope_tables(pos_col, inv_lane, rows):
    def body(ins, outs, _):
        ang = ins[0][...].astype(F32) * ins[1][...]
        lane = lax.broadcasted_iota(jnp.int32, ang.shape, 1)
        cos, sin = jnp.cos(ang), jnp.sin(ang)
        first = (lane >= HEAD_DIM) & (lane < HEAD_DIM + MLA_ROPE // 2)
        second = (lane >= HEAD_DIM + MLA_ROPE // 2) & (lane < MLA_QK)
        outs[0][...] = jnp.where(lane < HEAD_DIM, 1.0, jnp.where(lane < MLA_QK, cos, 0.0))
        outs[1][...] = jnp.where(first, -sin, 0.0)
        outs[2][...] = jnp.where(second, sin, 0.0)
    return _ew("rope_tables", body, [(pos_col, 1, 0), (inv_lane, None, None)], [(LANES, F32)] * 3, rows)


def _rope(x, c, s1, s2):
    return x * c + pltpu.roll(x, 112, 1) * s1 + pltpu.roll(x, 16, 1) * s2


def _rope_t(d, c, s1, s2):
    return d * c + pltpu.roll(d * s1, 16, 1) + pltpu.roll(d * s2, 112, 1)


def _mla_prep(qdn, w_uq, kvdn, w_ukv, z, tabs, rows):
    def body(ins, outs, _):
        q_pre, kv_pre = ins[0], ins[1]
        c, s1, s2 = ins[3][...], ins[4][...], ins[5][...]
        kr = _rope(ins[2][...].astype(F32), c, s1, s2)
        for h in range(HEADS):
            cols = slice(h * LANES, (h + 1) * LANES)
            outs[0][:, cols] = (_rope(q_pre[:, cols], c, s1, s2) * MLA_QSCALE).astype(BF16)
            outs[1][:, cols] = (kv_pre[:, cols] + kr).astype(BF16)
        outs[2][...] = kv_pre[:, HPAD:].astype(BF16)
    ins = [(z, LANES, Z_BKR // LANES), (tabs[0], LANES, 0), (tabs[1], LANES, 0), (tabs[2], LANES, 0)]
    return _ew("mla_prep", body, ins, [(HPAD, BF16)] * 3, rows, mms=[(qdn, w_uq), (kvdn, w_ukv)])


def _mla_prep_bwd(dq, dk, dv, tabs, dz, rows):
    def body(ins, outs, _):
        c, s1, s2 = ins[3][...], ins[4][...], ins[5][...]
        lane = lax.broadcasted_iota(jnp.int32, c.shape, 1)
        dkr = jnp.zeros(c.shape, F32)
        for h in range(HEADS):
            cols = slice(h * LANES, (h + 1) * LANES)
            outs[0][:, cols] = _rope_t(ins[0][:, cols].astype(F32) * MLA_SCALE, c, s1, s2).astype(BF16)
            dkh = ins[1][:, cols].astype(F32)
            outs[1][:, cols] = jnp.where(lane < HEAD_DIM, dkh, 0.0).astype(BF16)
            dkr = dkr + dkh
        outs[1][:, HPAD:] = ins[2][...].astype(BF16)
        live = (lane >= HEAD_DIM) & (lane < MLA_QK)
        outs[2][...] = jnp.where(live, _rope_t(jnp.where(live, dkr, 0.0), c, s1, s2), 0.0).astype(BF16)
    ins = [(dq, HPAD, 0), (dk, HPAD, 0), (dv, HPAD, 0), (tabs[0], LANES, 0), (tabs[1], LANES, 0),
           (tabs[2], LANES, 0)]
    outs = [(HPAD, BF16), (2 * HPAD, BF16), (LANES, BF16, dz, Z_BKR // LANES)]
    return _ew("mla_prep_bwd", body, ins, outs, rows)


def _gate_bwd(name, d_o_mm, o_raw, gate, gate_cb, dz, dz_cb, rows):
    def body(ins, outs, _):
        lane = lax.broadcasted_iota(jnp.int32, outs[2].shape, 1)
        delta = jnp.zeros(outs[2].shape, F32)
        d_gate = [None] * HEADS
        for h in range(HEADS):
            cols = slice(h * LANES, (h + 1) * LANES)
            dog, o, g = ins[0][:, cols], ins[1][:, cols].astype(F32), _head_gate(ins[2], h)
            sg = jax.nn.sigmoid(g)
            do = dog * (g * sg)
            outs[0][:, cols] = do.astype(BF16)
            d_gate[h] = dog * o * (sg * (1.0 + g * (1.0 - sg)))
            delta = jnp.where(lane == h, jnp.sum(do * o, axis=-1, keepdims=True), delta)
        for pair in range(HEADS // 2):
            packed = d_gate[2 * pair] + pltpu.roll(d_gate[2 * pair + 1], HEAD_DIM, 1)
            outs[1][:, pair * LANES:(pair + 1) * LANES] = packed.astype(BF16)
        outs[2][...] = delta
    ins = [(o_raw, HPAD, 0), (gate, GATE_W, gate_cb)]
    outs = [(HPAD, BF16), (GATE_W, BF16, dz, dz_cb), (LANES, F32)]
    return _ew(name, body, ins, outs, rows, mms=[d_o_mm])


def _merge_out(oa, ob, w_br_a, w_br_b, z, w_out, x0, g_next, rows):
    tm = _row_tile(rows)

    def kern(oa_ref, ob_ref, wa_ref, wb_ref, ma_ref, mb_ref, w_ref, x0_ref, g_ref,
             ua_ref, ub_ref, y_ref, x1_ref, hn_ref):
        oa_v, ob_v = oa_ref[...], ob_ref[...]
        for j in range(D_MODEL // MM_TN):
            cols = slice(j * MM_TN, (j + 1) * MM_TN)
            ua = jnp.dot(oa_v, wa_ref[:, cols], preferred_element_type=F32)
            ub = jnp.dot(ob_v, wb_ref[:, cols], preferred_element_type=F32)
            ua_ref[:, cols] = ua.astype(BF16)
            ub_ref[:, cols] = ub.astype(BF16)
            m_a, m_b = ma_ref[:, cols].astype(F32), mb_ref[:, cols].astype(F32)
            y_ref[:, cols] = (jax.nn.sigmoid(m_a) * ua + jax.nn.sigmoid(m_b) * ub).astype(BF16)
        y = y_ref[...]
        for j in range(D_MODEL // MM_TN):
            cols = slice(j * MM_TN, (j + 1) * MM_TN)
            x1_ref[:, cols] = jnp.dot(y, w_ref[:, cols], preferred_element_type=F32) + x0_ref[:, cols]
        hn_ref[...] = _rms(x1_ref[...], g_ref[...])

    row = lambda cb: pl.BlockSpec((tm, D_MODEL), lambda i: (i, cb))
    whole = lambda a: pl.BlockSpec(a.shape, lambda i: (0, 0))
    bf16, f32 = jax.ShapeDtypeStruct((rows, D_MODEL), BF16), jax.ShapeDtypeStruct((rows, D_MODEL), F32)
    return pl.pallas_call(
        kern, name="merge_out", grid=(rows // tm,),
        in_specs=[row(0), row(0), whole(w_br_a), whole(w_br_b), row(Z_MA // D_MODEL), row(Z_MB // D_MODEL),
                  whole(w_out), row(0), pl.BlockSpec((1, D_MODEL), lambda i: (0, 0))],
        out_specs=[row(0)] * 5, out_shape=[bf16, bf16, bf16, f32, bf16],
        compiler_params=_params(("parallel",)))(oa, ob, w_br_a, w_br_b, z, z, w_out, x0, g_next.reshape(1, D_MODEL))


def _merge_bwd(dy_mm, ua, ub, z, dz, rows):
    def body(ins, outs, _):
        dyv = ins[0][...]
        for idx in range(2):
            s = jax.nn.sigmoid(ins[3 + idx][...].astype(F32))
            outs[idx][...] = (dyv * s).astype(BF16)
            d_m = (dyv * ins[1 + idx][...].astype(F32) * (s * (1.0 - s))).astype(BF16)
            outs[2][:, idx * D_MODEL:(idx + 1) * D_MODEL] = d_m
    ins = [(ua, D_MODEL, 0), (ub, D_MODEL, 0), (z, D_MODEL, Z_MA // D_MODEL), (z, D_MODEL, Z_MB // D_MODEL)]
    outs = [(D_MODEL, BF16), (D_MODEL, BF16), (2 * D_MODEL, BF16, dz, Z_MA // (2 * D_MODEL))]
    return _ew("merge_bwd", body, ins, outs, rows, mms=[dy_mm])


def _kv_grad_cast(dk, dv, dz, rows):
    def body(ins, outs, _):
        outs[0][:, :KV_W] = ins[0][...].astype(BF16)
        outs[0][:, KV_W:] = ins[1][...].astype(BF16)
    outs = [(2 * KV_W, BF16, dz, Z_AK // (2 * KV_W))]
    return _ew("kv_grad_cast", body, [(dk, KV_W, 0), (dv, KV_W, 0)], outs, rows)[0]


def _ple_fwd(x1, hn, w_pg, p, w_pp, g_next, rows):
    def body(ins, outs, _):
        u, e = ins[0][...], ins[1][...]
        x2 = ins[2][...] + jax.nn.sigmoid(u) * e
        outs[0][...] = x2
        outs[1][...] = u.astype(BF16)
        outs[2][...] = e.astype(BF16)
        if g_next is not None:
            outs[3][...] = _rms(x2, ins[3][...])
    ins = [(x1, D_MODEL, 0)] + ([(g_next.reshape(1, D_MODEL), None, None)] if g_next is not None else [])
    outs = [(D_MODEL, F32), (D_MODEL, BF16), (D_MODEL, BF16)] + ([(D_MODEL, BF16)] if g_next is not None else [])
    return _ew("ple_fwd", body, ins, outs, rows, mms=[(hn, w_pg), (p, w_pp)])


def _loss_head(x, g, target, ple, rows):
    def body(ins, outs, accs):
        xv, gv = ins[0][...], ins[1][...]
        r = lax.rsqrt(jnp.mean(xv * xv, axis=-1, keepdims=True) + EPS)
        xhat = xv * r
        err = xhat * gv - ins[2][...]
        accs[0][...] += jnp.broadcast_to(0.5 * jnp.sum(jnp.mean(err * err, axis=-1, keepdims=True),
                                                       axis=0, keepdims=True), (1, LANES))
        dyv = err * (1.0 / D_MODEL)
        accs[1][...] += jnp.sum(dyv * xhat, axis=0, keepdims=True)
        dy = dyv * gv
        dx = r * (dy - xhat * jnp.mean(dy * xhat, axis=-1, keepdims=True))
        outs[0][...] = dx
        outs[1][...], outs[2][...] = _ple_grads(dx, ins[3][...], ins[4][...])
    ins = [(x, D_MODEL, 0), (g.reshape(1, D_MODEL), None, None), (target, D_MODEL, 0),
           (ple[0], D_MODEL, 0), (ple[1], D_MODEL, 0)]
    outs = [(D_MODEL, F32), (D_MODEL, BF16), (D_MODEL, BF16)]
    return _ew("loss_head", body, ins, outs, rows, accs=[(1, LANES), (1, D_MODEL)])


def _pad_heads_cols(w, n_heads, dim):
    k = w.shape[0]
    return jnp.pad(w.reshape(k, n_heads, dim), ((0, 0), (0, 0), (0, LANES - dim))).reshape(k, n_heads * LANES)


def _unpad_heads_cols(w, n_heads, dim):
    k = w.shape[0]
    return w.reshape(k, n_heads, LANES)[:, :, :dim].reshape(k, n_heads * dim)


def _layer_weights(w, i):
    segs = jnp.split(w['w_in'][i], list(_cumsum(IN_SIZES))[:-1], axis=1)
    a_q, a_k, a_v, a_gate, b_qd, b_kvd, b_kr, b_gate, m_a, m_b = segs
    kr = jnp.pad(b_kr, ((0, 0), (HEAD_DIM, LANES - MLA_QK)))
    w_in = jnp.concatenate([
        m_a, m_b, _pad_heads_cols(a_q, HEADS, HEAD_DIM), a_gate, b_gate, _pad_heads_cols(a_k, SWA_KV_HEADS, HEAD_DIM),
        _pad_heads_cols(a_v, SWA_KV_HEADS, HEAD_DIM), b_qd, b_kvd, kr], axis=1)
    w_uq = _pad_heads_cols(w['w_uq'][i], HEADS, MLA_QK)
    ukv = w['w_ukv'][i].reshape(MLA_KV_LORA, HEADS, 2 * HEAD_DIM)
    pad = ((0, 0), (0, 0), (0, HEAD_DIM))
    w_ukv = jnp.concatenate([jnp.pad(ukv[:, :, :HEAD_DIM], pad).reshape(MLA_KV_LORA, HPAD),
                             jnp.pad(ukv[:, :, HEAD_DIM:], pad).reshape(MLA_KV_LORA, HPAD)], axis=1)
    w_br_a = _pad_heads_cols(w['w_br_a'][i].T, HEADS, HEAD_DIM).T
    w_br_b = _pad_heads_cols(w['w_br_b'][i].T, HEADS, HEAD_DIM).T
    out = dict(w_in=w_in, w_uq=w_uq, w_ukv=w_ukv, w_br_a=w_br_a, w_br_b=w_br_b, w_out=w['w_out'][i],
               w_pg=w['w_ple_gate'][i], w_pp=w['w_ple_proj'][i])
    for name in ('w_in', 'w_uq', 'w_ukv', 'w_br_a', 'w_br_b', 'w_out', 'w_pg'):
        out[name + '_t'] = out[name].T
    return out


def _cumsum(sizes):
    acc, out = 0, []
    for s in sizes:
        acc += s
        out.append(acc)
    return out


def _unpad_grads(g):
    d = g['w_in']
    seg = lambda off, width: d[:, off:off + width]
    b_kr = seg(Z_BKR, LANES)[:, HEAD_DIM:MLA_QK]
    w_in = jnp.concatenate([
        _unpad_heads_cols(seg(Z_AQ, HPAD), HEADS, HEAD_DIM), _unpad_heads_cols(seg(Z_AK, KV_W), SWA_KV_HEADS, HEAD_DIM),
        _unpad_heads_cols(seg(Z_AV, KV_W), SWA_KV_HEADS, HEAD_DIM), seg(Z_AGATE, GATE_W),
        seg(Z_BQD, MLA_Q_LORA), seg(Z_BKVD, MLA_KV_LORA), b_kr, seg(Z_BGATE, GATE_W),
        seg(Z_MA, D_MODEL), seg(Z_MB, D_MODEL)], axis=1)
    w_uq = _unpad_heads_cols(g['w_uq'], HEADS, MLA_QK)
    ukv = g['w_ukv'].reshape(MLA_KV_LORA, 2, HEADS, LANES)[:, :, :, :HEAD_DIM]
    w_ukv = jnp.concatenate([ukv[:, 0], ukv[:, 1]], axis=-1).reshape(MLA_KV_LORA, HEADS * 2 * HEAD_DIM)
    w_br_a = _unpad_heads_cols(g['w_br_a'].T, HEADS, HEAD_DIM).T
    w_br_b = _unpad_heads_cols(g['w_br_b'].T, HEADS, HEAD_DIM).T
    return dict(w_in=w_in, w_uq=w_uq, w_ukv=w_ukv, w_br_a=w_br_a, w_br_b=w_br_b, w_out=g['w_out'],
                w_ple_gate=g['w_pg'], w_ple_proj=g['w_pp'], g_mix=g['g_mix'], sink=g['sink'], g_q=g['g_q'],
                g_kv=g['g_kv'], g_ple=g['g_ple'])


def _layer_fwd(x0, h, p_i, lw, sm, i, pos_col, pos_row, tabs, B, S):
    T = B * S
    z, a_gate, qdn, kvdn = _mm("proj_in", h, lw['w_in'], BF16, f32_cols=(Z_AGATE, GATE_W),
                               norms=[(Z_BQD, sm['g_q'][i]), (Z_BKVD, sm['g_kv'][i])])
    sink_row = jnp.pad(sm['sink'][i], (0, LANES - HEADS)).reshape(1, LANES)
    oa_raw, oa, lse_a = _swa_fwd(z, a_gate, pos_col, pos_row, sink_row, B, S)
    qf, kf, vf = _mla_prep(qdn, lw['w_uq'], kvdn, lw['w_ukv'], z, tabs, T)
    ob_raw, ob, lse_b = _mla_fwd(qf, kf, vf, z, B, S)
    ua, ub, y, x1, hn = _merge_out(oa, ob, lw['w_br_a'], lw['w_br_b'], z, lw['w_out'], x0, sm['g_ple'][i], T)
    g_next = sm['g_mix'][i + 1] if i + 1 < DEPTH else None
    x2, u, e, *h_next = _ple_fwd(x1, hn, lw['w_pg'], p_i, lw['w_pp'], g_next, T)
    saved = dict(x0=x0, h=h, z=z, a_gate=a_gate, sink_row=sink_row, oa_raw=oa_raw, oa=oa, lse_a=lse_a, qdn=qdn, kvdn=kvdn,
                 qf=qf, kf=kf, vf=vf, ob_raw=ob_raw, ob=ob, lse_b=lse_b, ua=ua, ub=ub, y=y, x1=x1, hn=hn,
                 u=u, e=e, p=p_i)
    return x2, (h_next[0] if h_next else None), saved


def _layer_bwd(dx2, d_e, d_u, sv, below, lw, sm, i, pos_col, pos_row, tabs, B, S):
    T = B * S
    z = sv['z']
    g = {}
    g['w_pp'] = _mm_tn("grad_pp", sv['p'], d_e)
    g['w_pg'] = _mm_tn("grad_pg", sv['hn'], d_u)
    dx1, g['g_ple'] = _rms_bwd("norm_ple_bwd", sv['x1'], D_MODEL, 0, sm['g_ple'][i], (d_u, lw['w_pg_t']), T, F32,
                               dres=dx2)
    g['w_out'] = _mm_tn("grad_out", sv['y'], dx1)
    dz = lax.empty((T, Z_WIDTH), BF16)
    d_ua, d_ub, dz = _merge_bwd((dx1, lw['w_out_t']), sv['ua'], sv['ub'], z, dz, T)
    g['w_br_a'] = _mm_tn("grad_br_a", sv['oa'], d_ua)
    g['w_br_b'] = _mm_tn("grad_br_b", sv['ob'], d_ub)
    dob_raw, dz, delta_b = _gate_bwd("gate_b_bwd", (d_ub, lw['w_br_b_t']), sv['ob_raw'], z, Z_BGATE // GATE_W,
                                     dz, Z_BGATE // GATE_W, T)
    nq, groups = S // MLA_T, HEADS // MLA_HGB
    delta_rows = delta_b[:, :HEADS].reshape(B, nq, MLA_T, groups, MLA_HGB).transpose(0, 3, 1, 4, 2)
    lse_rows = sv['lse_b'].transpose(0, 2, 1, 3, 4).reshape(B, nq, groups, MLA_HGB, MLA_T).transpose(0, 2, 1, 3, 4)
    dq, dk, dv = _mla_bwd(sv['qf'], sv['kf'], sv['vf'], dob_raw, lse_rows, delta_rows, B, S)
    dq_pre, dkv_pre, dz = _mla_prep_bwd(dq, dk, dv, tabs, dz, T)
    g['w_uq'] = _mm_tn("grad_uq", sv['qdn'], dq_pre)
    g['w_ukv'] = _mm_tn("grad_ukv", sv['kvdn'], dkv_pre)
    dz, g['g_q'] = _rms_bwd("norm_q_bwd", z, MLA_Q_LORA, Z_BQD // MLA_Q_LORA, sm['g_q'][i],
                            (dq_pre, lw['w_uq_t']), T, BF16, into=(dz, Z_BQD // MLA_Q_LORA))
    dz, g['g_kv'] = _rms_bwd("norm_kv_bwd", z, MLA_KV_LORA, Z_BKVD // MLA_KV_LORA, sm['g_kv'][i],
                             (dkv_pre, lw['w_ukv_t']), T, BF16, into=(dz, Z_BKVD // MLA_KV_LORA))
    doa_raw, dz, delta_a = _gate_bwd("gate_a_bwd", (d_ua, lw['w_br_a_t']), sv['oa_raw'], sv['a_gate'], 0,
                                     dz, Z_AGATE // GATE_W, T)
    dz, d_ak, d_av, dsink = _swa_bwd(z, pos_col, pos_row, sv['sink_row'], sv['lse_a'], doa_raw, delta_a, dz, B, S)
    dz = _kv_grad_cast(d_ak, d_av, dz, T)
    g['sink'] = dsink[0, :HEADS]
    g['w_in'] = _mm_tn("grad_in", sv['h'], dz, tk=1024, tn=Z_WIDTH // 2)
    *down, g['g_mix'] = _rms_bwd("norm_mix_bwd", sv['x0'], D_MODEL, 0, sm['g_mix'][i], (dz, lw['w_in_t']), T, F32,
                                 dres=dx1, ple=below)
    for name in ('g_ple', 'g_q', 'g_kv', 'g_mix'):
        g[name] = g[name][0]
    return down, g


def _local_step(x, p, positions, wfull, sm, loss_target):
    B, S, _ = x.shape
    T = B * S
    pos_col = positions.reshape(T, 1)
    pos_row = positions.reshape(T // BLOCK, 1, BLOCK)
    half = MLA_ROPE // 2
    inv = ROPE_THETA ** (-jnp.arange(0, MLA_ROPE, 2, dtype=F32) / MLA_ROPE)
    inv_lane = jnp.tile(inv, LANES // half).reshape(1, LANES)
    tabs = _rope_tables(pos_col, inv_lane, T)
    xc = x.reshape(T, D_MODEL)
    h = _rms_fwd("norm_mix", xc, D_MODEL, 0, sm['g_mix'][0], T)
    lws, saved = [], []
    for i in range(DEPTH):
        lw = _layer_weights(wfull, i)
        xc, h, sv = _layer_fwd(xc, h, p[i].reshape(T, PLE_DIM), lw, sm, i, pos_col, pos_row, tabs, B, S)
        lws.append(lw)
        saved.append(sv)
    ple = [(sv['u'], sv['e']) for sv in saved]
    *down, loss, dg_final = _loss_head(xc, sm['g_final'], loss_target.reshape(T, D_MODEL), ple[-1], T)
    layer_grads = [None] * DEPTH
    for i in reversed(range(DEPTH)):
        down, g = _layer_bwd(*down, saved[i], ple[i - 1] if i else None, lws[i], sm, i, pos_col, pos_row, tabs, B, S)
        layer_grads[i] = _unpad_grads(g)
    return loss, down[0].reshape(B, S, D_MODEL), layer_grads, dg_final[0]


SMALL_ROWS = 48


SMALL_SIZE = 2 * (2 * D_MODEL + HEADS + MLA_Q_LORA + MLA_KV_LORA) + D_MODEL


def _pack_small(arrs, tail=()):
    flat = jnp.concatenate([arrs[name].reshape(-1) for name in SMALL] + [t.reshape(1) for t in tail])
    return jnp.pad(flat, (0, SMALL_ROWS * LANES - flat.shape[0])).reshape(SMALL_ROWS, LANES)


def _unpack_small(block, shapes):
    flat = block.reshape(-1)
    out, off = {}, 0
    for name in SMALL:
        n = math.prod(shapes[name])
        out[name] = flat[off:off + n].reshape(shapes[name])
        off += n
    return out


def _flipped(shard_shape):
    return shard_shape[-1] % LANES != 0


def _to_slots(g, axis):
    r, c = g.shape
    if axis == 0:
        return g.reshape(N_CHIPS, r // N_CHIPS, c)
    return g.reshape(r, N_CHIPS, c // N_CHIPS).transpose(1, 0, 2)


def _div_tile(rows, cap):
    return next(t for t in range(min(cap, rows) // 8 * 8, 0, -8) if rows % t == 0)


def _units(shapes):
    units = []
    for w, shape in enumerate(shapes):
        r = shape[-2]
        n = next((n for n in (8, 4, 2) if r % (16 * n) == 0), 1) if r >= 512 else 1
        units += [(w, k * (r // n), r // n) for k in range(n)]
    return units


def _place():
    x, y, c = lax.axis_index("x"), lax.axis_index("y"), lax.axis_index("c")
    chips = [(1 - x, y), (x, 1 - y), (1 - x, 1 - y)]
    return x, y, c, chips


ANY = pl.BlockSpec(memory_space=pl.ANY)


def _remote(send_sems, recv_sems, k, src, dst, to):
    return pltpu.make_async_remote_copy(src_ref=src, dst_ref=dst, send_sem=send_sems.at[k],
                                        recv_sem=recv_sems.at[k], device_id=to, device_id_type=MESH)


def _gather_weights(shards, carried):
    n, nc = len(shards), len(carried)
    units = _units([s.shape for s in shards])
    nu = len(units)

    def body(*refs):
        ins, outs = refs[:n], refs[n + nc:2 * n + nc]
        send_sems, recv_sems, local_sems = refs[2 * (n + nc):]
        x, y, c, chips = _place()
        me = 2 * x + y
        sibling = (x, y, 1 - c)
        copy = functools.partial(_remote, send_sems, recv_sems)
        keeps, sends = [], []
        for u, (w, r0, nr) in enumerate(units):
            rows = pl.ds(r0, nr)
            keeps.append(pltpu.make_async_copy(ins[w].at[:, rows, :], outs[w].at[me, :, rows, :], local_sems.at[u]))
            keeps[-1].start()
        for j, (cx, cy) in enumerate(chips):
            for u, (w, r0, nr) in enumerate(units):
                rows = pl.ds(r0, nr)
                sends.append(copy(j * nu + u, ins[w].at[c, rows, :], outs[w].at[me, c, rows, :], (cx, cy, c)))
                sends[-1].start()
        for j, (cx, cy) in enumerate(chips):
            for u, (w, r0, nr) in enumerate(units):
                landed = outs[w].at[2 * cx + cy, c, pl.ds(r0, nr), :]
                copy(j * nu + u, landed, landed, (cx, cy, c)).wait_recv()
                sends.append(copy((3 + j) * nu + u, landed, landed, sibling))
                sends[-1].start()
        for j, (cx, cy) in enumerate(chips):
            for u, (w, r0, nr) in enumerate(units):
                other = outs[w].at[2 * cx + cy, 1 - c, pl.ds(r0, nr), :]
                copy((3 + j) * nu + u, other, other, sibling).wait_recv()
        for cp in sends:
            cp.wait_send()
        for keep in keeps:
            keep.wait()

    out_shape = [jax.ShapeDtypeStruct((N_CHIPS,) + s.shape, s.dtype) for s in shards]
    out_shape += [jax.ShapeDtypeStruct(a.shape, a.dtype) for a in carried]
    res = pl.pallas_call(
        body, name="gather_weights", out_shape=out_shape,
        in_specs=[ANY] * (n + nc), out_specs=[ANY] * (n + nc),
        input_output_aliases={n + k: n + k for k in range(nc)},
        scratch_shapes=[pltpu.SemaphoreType.DMA((6 * nu,)), pltpu.SemaphoreType.DMA((6 * nu,)),
                        pltpu.SemaphoreType.DMA((nu,))])(*shards, *carried)
    return res[:n], res[n:]


def _pair_exchange(g0, g1):
    n = len(g0)

    def body(*refs):
        layers, outs = (refs[:n], refs[n:2 * n]), refs[2 * n:3 * n]
        send_sems, recv_sems = refs[3 * n:]
        x, y, c, _ = _place()
        copy = functools.partial(_remote, send_sems, recv_sems)
        for w in range(n):
            for q in range(N_CHIPS):
                for layer in range(DEPTH):
                    cp = copy(N_CHIPS * w + q, layers[layer][w].at[q], outs[w].at[q], (x, y, 1 - c))
                    pl.when(c == 1 - layer)(cp.start)
        for w in range(n):
            for q in range(N_CHIPS):
                copy(N_CHIPS * w + q, layers[0][w].at[q], outs[w].at[q], (x, y, 1 - c)).wait()

    return pl.pallas_call(
        body, name="pair_exchange", out_shape=[jax.ShapeDtypeStruct(g.shape, g.dtype) for g in g0],
        in_specs=[ANY] * (2 * n), out_specs=[ANY] * n,
        scratch_shapes=[pltpu.SemaphoreType.DMA((N_CHIPS * n,)), pltpu.SemaphoreType.DMA((N_CHIPS * n,))])(*g0, *g1)


def _pair_sum(name, g0, g1, theirs, cflag):
    shape = theirs.shape
    rows, width = shape[0] * shape[1], shape[2]

    def body(ins, outs, _):
        mine = jnp.where(ins[3][0:1, 0:1] == 0.0, ins[0][...], ins[1][...])
        tot = mine + ins[2][...]
        outs[0][...] = tot
        outs[1][...] = tot.astype(BF16)
    ins = [(a.reshape(rows, width), width, 0) for a in (g0, g1, theirs)] + [(cflag, None, None)]
    f32, bf16 = _ew(name, body, ins, [(width, F32), (width, BF16)], rows, tm=_div_tile(rows, ROW_TILE))
    return f32.reshape(shape), bf16.reshape(shape)


def _chip_exchange(parts):
    n = len(parts)

    def body(*refs):
        ins, outs = refs[:n], refs[n:2 * n]
        send_sems, recv_sems = refs[2 * n:]
        x, y, c, chips = _place()
        copy = functools.partial(_remote, send_sems, recv_sems)
        sends = []
        for j, (cx, cy) in enumerate(chips):
            for w in range(n):
                sends.append(copy(j * n + w, ins[w].at[2 * cx + cy], outs[w].at[j], (cx, cy, c)))
                sends[-1].start()
        for j, (cx, cy) in enumerate(chips):
            for w in range(n):
                copy(j * n + w, outs[w].at[j], outs[w].at[j], (cx, cy, c)).wait_recv()
        for cp in sends:
            cp.wait_send()

    return pl.pallas_call(
        body, name="chip_exchange",
        out_shape=[jax.ShapeDtypeStruct((3,) + a.shape[1:], a.dtype) for a in parts],
        in_specs=[ANY] * n, out_specs=[ANY] * n,
        scratch_shapes=[pltpu.SemaphoreType.DMA((3 * n,)), pltpu.SemaphoreType.DMA((3 * n,))])(*parts)


def _chip_sum(name, part, landed, chipflag):
    _, r, width = part.shape
    tm = _div_tile(r, ROW_TILE // 2)

    def kern(p_ref, l_ref, flag_ref, o_ref):
        me = flag_ref[0:1, 0:1]
        own = jnp.where(me == 0.0, p_ref[0], jnp.where(me == 1.0, p_ref[1], jnp.where(me == 2.0, p_ref[2], p_ref[3])))
        o_ref[...] = ((own + l_ref[0].astype(F32)) + l_ref[1].astype(F32)) + l_ref[2].astype(F32)

    return pl.pallas_call(
        kern, name=name, grid=(r // tm,),
        in_specs=[pl.BlockSpec((N_CHIPS, tm, width), lambda i: (0, i, 0)),
                  pl.BlockSpec((3, tm, width), lambda i: (0, i, 0)),
                  pl.BlockSpec((1, LANES), lambda i: (0, 0))],
        out_specs=pl.BlockSpec((tm, width), lambda i: (i, 0)),
        out_shape=jax.ShapeDtypeStruct((r, width), F32), compiler_params=_params(("arbitrary",)))(part, landed, chipflag)


def _pair_broadcast(mine):
    n = len(mine)
    units = _units([a.shape for a in mine])

    def body(*refs):
        ins, outs = refs[:n], refs[n:2 * n]
        send_sems, recv_sems = refs[2 * n:]
        x, y, c, _ = _place()
        copy = functools.partial(_remote, send_sems, recv_sems)
        cps = [copy(u, ins[w].at[pl.ds(r0, nr), :], outs[w].at[pl.ds(r0, nr), :], (x, y, 1 - c))
               for u, (w, r0, nr) in enumerate(units)]
        for cp in cps:
            cp.start()
        for cp in cps:
            cp.wait()

    return pl.pallas_call(
        body, name="pair_broadcast", out_shape=[jax.ShapeDtypeStruct(a.shape, a.dtype) for a in mine],
        in_specs=[ANY] * n, out_specs=[ANY] * n,
        scratch_shapes=[pltpu.SemaphoreType.DMA((len(units),)), pltpu.SemaphoreType.DMA((len(units),))])(*mine)


def _small_allreduce(v):
    offsets = [(dx, dy, dc) for dx in (0, 1) for dy in (0, 1) for dc in (0, 1)][1:]

    def body(v_ref, out_ref, recv_ref, send_sems, recv_sems):
        x, y, c, _ = _place()
        flip = lambda a, d: 1 - a if d else a
        peers = [(flip(x, dx), flip(y, dy), flip(c, dc)) for dx, dy, dc in offsets]
        copy = functools.partial(_remote, send_sems, recv_sems)
        me = 4 * x + 2 * y + c
        recv_ref[me] = v_ref[...]
        cps = [copy(k, v_ref, recv_ref.at[me], peer) for k, peer in enumerate(peers)]
        for cp in cps:
            cp.start()
        for k, (px, py, pc) in enumerate(peers):
            landed = recv_ref.at[4 * px + 2 * py + pc]
            copy(k, landed, landed, (px, py, pc)).wait_recv()
        for cp in cps:
            cp.wait_send()
        tot = recv_ref[0]
        for d in range(1, 8):
            tot = tot + recv_ref[d]
        out_ref[...] = tot

    vmem = pl.BlockSpec(memory_space=pltpu.VMEM)
    return pl.pallas_call(
        body, name="small_allreduce", out_shape=jax.ShapeDtypeStruct(v.shape, v.dtype),
        in_specs=[vmem], out_specs=vmem,
        scratch_shapes=[pltpu.VMEM((8,) + v.shape, v.dtype), pltpu.SemaphoreType.DMA((7,)),
                        pltpu.SemaphoreType.DMA((7,))])(v)


def _adam_math(gv, wv, mv, vv):
    mv = ADAM_B1 * mv + (1.0 - ADAM_B1) * gv
    vv = ADAM_B2 * vv + (1.0 - ADAM_B2) * (gv * gv)
    m_hat = mv / (1.0 - ADAM_B1 ** ADAM_STEP)
    v_hat = vv / (1.0 - ADAM_B2 ** ADAM_STEP)
    return -ADAM_LR * (m_hat / (jnp.sqrt(v_hat) + ADAM_EPS) + ADAM_WD * wv), mv, vv


def _adamw_big(name, mine, theirs, cflag, w, m, v):
    _, r, width = w.shape
    tm = _div_tile(r, ROW_TILE // 2)

    def kern(mine_ref, theirs_ref, flag_ref, w_ref, m_ref, v_ref, g_ref, d_ref, nm_ref, nv_ref):
        layer = pl.program_id(0).astype(F32)
        gv = jnp.where(flag_ref[0:1, 0:1] == layer, mine_ref[...], theirs_ref[...])
        g_ref[0] = gv
        d_ref[0], nm_ref[0], nv_ref[0] = _adam_math(gv, w_ref[0], m_ref[0], v_ref[0])

    flat = pl.BlockSpec((tm, width), lambda l, i: (i, 0))
    stacked = pl.BlockSpec((1, tm, width), lambda l, i: (l, i, 0))
    return pl.pallas_call(
        kern, name=name, grid=(DEPTH, r // tm),
        in_specs=[flat, flat, pl.BlockSpec((1, LANES), lambda l, i: (0, 0)), stacked, stacked, stacked],
        out_specs=[stacked] * 4, out_shape=[jax.ShapeDtypeStruct(w.shape, F32)] * 4,
        compiler_params=_params(("arbitrary", "arbitrary")))(mine, theirs, cflag, w, m, v)


def _adamw_small(g, w, m, v):
    def body(ins, outs, _):
        outs[0][...], outs[1][...], outs[2][...] = _adam_math(*(r[...] for r in ins))
    return _ew("adamw_small", body, [(a, LANES, 0) for a in (g, w, m, v)], [(LANES, F32)] * 3, SMALL_ROWS)


def kernel(x, p, positions, g_mix, w_in, sink, g_q, w_uq, g_kv, w_ukv, w_br_a, w_br_b, w_out, g_ple, w_ple_gate, w_ple_proj, g_final, loss_target, m_g_mix, m_w_in, m_sink, m_g_q, m_w_uq, m_g_kv, m_w_ukv, m_w_br_a, m_w_br_b, m_w_out, m_g_ple, m_w_ple_gate, m_w_ple_proj, m_g_final, v_g_mix, v_w_in, v_sink, v_g_q, v_w_uq, v_g_kv, v_w_ukv, v_w_br_a, v_w_br_b, v_w_out, v_g_ple, v_w_ple_gate, v_w_ple_proj, v_g_final):
    w = dict(g_mix=g_mix, w_in=w_in, sink=sink, g_q=g_q, w_uq=w_uq, g_kv=g_kv, w_ukv=w_ukv, w_br_a=w_br_a,
             w_br_b=w_br_b, w_out=w_out, g_ple=g_ple, w_ple_gate=w_ple_gate, w_ple_proj=w_ple_proj, g_final=g_final)
    m = dict(g_mix=m_g_mix, w_in=m_w_in, sink=m_sink, g_q=m_g_q, w_uq=m_w_uq, g_kv=m_g_kv, w_ukv=m_w_ukv,
             w_br_a=m_w_br_a, w_br_b=m_w_br_b, w_out=m_w_out, g_ple=m_g_ple, w_ple_gate=m_w_ple_gate,
             w_ple_proj=m_w_ple_proj, g_final=m_g_final)
    v = dict(g_mix=v_g_mix, w_in=v_w_in, sink=v_sink, g_q=v_g_q, w_uq=v_w_uq, g_kv=v_g_kv, w_ukv=v_w_ukv,
             w_br_a=v_w_br_a, w_br_b=v_w_br_b, w_out=v_w_out, g_ple=v_g_ple, w_ple_gate=v_w_ple_gate,
             w_ple_proj=v_w_ple_proj, g_final=v_g_final)
    wfull = _gather_full(w)
    sm = {name: w[name] for name in SMALL}
    loss_row, grad_x, layer_grads, dg_final = _local_step(x, p, positions, wfull, sm, loss_target)
    res, loss = _update(layer_grads, dg_final, loss_row[0, 0], w, m, v)
    return (loss, grad_x, *[res[name][kind] for kind in range(4) for name in WEIGHT_NAMES])


def _gather_behind(name, collective_id, shards):
    n = len(shards)
    srcs = [jax.new_ref(s, memory_space=pltpu.MemorySpace.HBM) for s in shards]
    lands = [jax.empty_ref(jax.ShapeDtypeStruct((N_CHIPS,) + s.shape, s.dtype), memory_space=pltpu.MemorySpace.HBM)
             for s in shards]

    @pl.kernel(mesh=plsc.ScalarSubcoreMesh(axis_name="sequencer", num_cores=1), name=name,
               scratch_types=(pltpu.SemaphoreType.DMA((3 * n,)), pltpu.SemaphoreType.DMA((3 * n,)),
                              pltpu.SemaphoreType.DMA((n,))),
               compiler_params=pltpu.CompilerParams(collective_id=collective_id))
    def launch(send_sems, recv_sems, local_sems):
        x, y, c, chips = _place()
        me = 2 * x + y
        barrier = pltpu.get_barrier_semaphore()
        for cx, cy in chips:
            pl.semaphore_signal(barrier, inc=1, device_id=(cx, cy, c), device_id_type=MESH)
        pl.semaphore_wait(barrier, len(chips))
        copy = functools.partial(_remote, send_sems, recv_sems)
        keeps = [pltpu.make_async_copy(srcs[w], lands[w].at[me], local_sems.at[w]) for w in range(n)]
        cps = [copy(j * n + w, srcs[w], lands[w].at[me], (cx, cy, c))
               for j, (cx, cy) in enumerate(chips) for w in range(n)]
        for cp in keeps + cps:
            cp.start()
        for cp in keeps + cps:
            cp.wait()

    launch()
    return [land[...] for land in lands]


def _gather_full(w):
    shards = [w[name].astype(BF16) for name, _ in SHARDED]
    w_in0 = shards[0][0]
    first, later = _gather_weights([w_in0.reshape((2, w_in0.shape[0] // 2) + w_in0.shape[1:])],
                                   [s[0] for s in shards[1:]] + [s[1] for s in shards])
    n_rest = len(shards) - 1
    layer0 = [first[0].reshape((N_CHIPS,) + w_in0.shape)] + _gather_behind("gather_rest", 0, later[:n_rest])
    layer1 = _gather_behind("gather_next", 1, later[n_rest:])
    return {name: [jnp.concatenate(list(blocks[k]), axis=axis - 1) for blocks in (layer0, layer1)]
            for k, (name, axis) in enumerate(SHARDED)}


def _update(layer_grads, dg_final, loss_local, w, m, v):
    small_shapes = {name: w[name].shape for name in SMALL}
    cflag = jnp.full((1, LANES), lax.axis_index("c"), F32)
    chipflag = jnp.full((1, LANES), 2 * lax.axis_index("x") + lax.axis_index("y"), F32)

    slots = [[_to_slots(layer_grads[layer][name], axis - 1) for name, axis in SHARDED] for layer in range(DEPTH)]
    theirs = _pair_exchange(slots[0], slots[1])
    pair = [_pair_sum("pair_sum_" + name, slots[0][k], slots[1][k], theirs[k], cflag)
            for k, (name, _) in enumerate(SHARDED)]
    landed = _chip_exchange([bf16 for _, bf16 in pair])
    mine = [_chip_sum("chip_sum_" + name, pair[k][0], landed[k], chipflag) for k, (name, _) in enumerate(SHARDED)]
    other = _pair_broadcast(mine)
    res = {}
    for k, (name, _) in enumerate(SHARDED):
        flip = _flipped(w[name].shape)
        view = (lambda a: jnp.swapaxes(a, -1, -2)) if flip else (lambda a: a)
        outs = _adamw_big("adamw_" + name, view(mine[k]), view(other[k]), cflag, view(w[name]), view(m[name]),
                          view(v[name]))
        res[name] = tuple(view(a) for a in outs)

    gsmall = {name: jnp.stack([layer_grads[layer][name] for layer in range(DEPTH)]) for name in SMALL[:-1]}
    gsmall['g_final'] = dg_final
    gsum = _small_allreduce(_pack_small(gsmall, tail=[loss_local]))
    small = (gsum,) + tuple(_adamw_small(gsum, _pack_small(w), _pack_small(m), _pack_small(v)))
    for name, arrs in zip(SMALL, zip(*[[_unpack_small(a, small_shapes)[n] for n in SMALL] for a in small])):
        res[name] = arrs
    return res, gsum.reshape(-1)[SMALL_SIZE]
```
